```python
import jax, jax.numpy as jnp
from jax import lax
import numpy as np

D_MODEL = 1024
BATCH = 8
SEQ = 8192
DEPTH = 1

HG_HEADS = 8
HG_DIM = 128
HG_WIDTH = HG_HEADS * HG_DIM
HG_CHUNK = 32
MLA_HEADS = 8
QK_NOPE = 128
QK_ROPE = 64
QK_DIM = QK_NOPE + QK_ROPE
V_DIM = 128
Q_LORA = 3 * D_MODEL // 8
KV_LORA = D_MODEL // 4
MLA_WIDTH = MLA_HEADS * V_DIM
Q_BLOCK = 128
ROPE_THETA = 10000.0
N_BRANCH = 2
EPS = 1e-6
IN_SPLITS = (HG_WIDTH, HG_WIDTH, HG_WIDTH, HG_WIDTH,
             Q_LORA, KV_LORA, QK_ROPE, MLA_WIDTH,
             N_BRANCH * D_MODEL)
IN_COLS = sum(IN_SPLITS)

kernel_name = "hgrn2_mla_gated_parallel_hybrid"


def _split_points():
    pts, acc = [], 0
    for w in IN_SPLITS[:-1]:
        acc += w
        pts.append(acc)
    return pts


def rms_norm(x, g):
    xf = x.astype(jnp.float32)
    y = xf * lax.rsqrt(jnp.mean(xf * xf, axis=-1, keepdims=True) + EPS)
    return (y * g.astype(jnp.float32)).astype(x.dtype)


def forget_lower_bounds(lb_logits):
    return jnp.cumsum(jax.nn.softmax(lb_logits.astype(jnp.float32), axis=0), axis=0)[:DEPTH]


def rope_tables(seq):
    inv = ROPE_THETA ** (-jnp.arange(0, QK_ROPE, 2, dtype=jnp.float32) / QK_ROPE)
    ang = jnp.arange(seq, dtype=jnp.float32)[:, None] * inv[None, :]
    return jnp.cos(ang), jnp.sin(ang)


def apply_rope(x, cos, sin):
    xf = x.astype(jnp.float32)
    x1, x2 = xf[..., : QK_ROPE // 2], xf[..., QK_ROPE // 2:]
    out = jnp.concatenate([x1 * cos - x2 * sin, x2 * cos + x1 * sin], axis=-1)
    return out.astype(x.dtype)


def hgrn2_recurrence(q, k, v, log_f):
    B, S, H, dk = q.shape
    dv = v.shape[-1]
    C = HG_CHUNK
    N = S // C

    def to_chunks(t):
        return t.reshape(B, N, C, H, t.shape[-1]).transpose(1, 0, 3, 2, 4)

    q, k, v, log_f = to_chunks(q), to_chunks(k), to_chunks(v), to_chunks(log_f)
    b = jnp.cumsum(log_f, axis=3)
    b_last = b[:, :, :, -1:, :]
    q_in = q * jnp.exp(b)
    k_in = k * jnp.exp(-b)
    k_out = k * jnp.exp(b_last - b)
    chunk_decay = jnp.exp(b_last[:, :, :, 0, :])

    causal = jnp.tril(jnp.ones((C, C), dtype=bool))
    scores = jnp.einsum('nbhtk,nbhsk->nbhts', q_in, k_in)
    scores = jnp.where(causal, scores, 0.0)
    o_intra = jnp.einsum('nbhts,nbhsv->nbhtv', scores, v)

    def step(state, inp):
        q_n, k_n, v_n, dec_n = inp
        o_inter = jnp.einsum('bhtk,bhkv->bhtv', q_n, state)
        state = state * dec_n[..., None] + jnp.einsum('bhsk,bhsv->bhkv', k_n, v_n)
        return state, o_inter

    state0 = jnp.zeros((B, H, dk, dv), jnp.float32)
    _, o_inter = lax.scan(step, state0, (q_in, k_out, v, chunk_decay))
    o = o_intra + o_inter
    return o.transpose(1, 0, 3, 2, 4).reshape(B, S, H, dv)


def hgrn2_branch(hq, hf, hi, hz, lb, hg_norm_g):
    B, S, _ = hq.shape
    dt = hq.dtype
    f = lb + (1.0 - lb) * jax.nn.sigmoid(hf.astype(jnp.float32))
    q = jax.nn.silu(hq.astype(jnp.float32)).reshape(B, S, HG_HEADS, HG_DIM)
    k = (1.0 - f).reshape(B, S, HG_HEADS, HG_DIM)
    v = hi.astype(jnp.float32).reshape(B, S, HG_HEADS, HG_DIM)
    log_f = jnp.log(f).reshape(B, S, HG_HEADS, HG_DIM)
    o = hgrn2_recurrence(q, k, v, log_f).astype(dt)
    o = rms_norm(o, hg_norm_g)
    o = o * jax.nn.silu(hz).reshape(B, S, HG_HEADS, HG_DIM)
    return o.reshape(B, S, HG_WIDTH)


def mla_branch(cq, ckv, kr, mz, q_a_g, w_uq, kv_a_g, w_ukv):
    B, S, _ = cq.shape
    cos, sin = rope_tables(S)
    q = (rms_norm(cq, q_a_g) @ w_uq).reshape(B, S, MLA_HEADS, QK_DIM)
    q_nope = q[..., :QK_NOPE]
    q_pe = apply_rope(q[..., QK_NOPE:], cos[:, None, :], sin[:, None, :])
    kv = (rms_norm(ckv, kv_a_g) @ w_ukv).reshape(B, S, MLA_HEADS, QK_NOPE + V_DIM)
    k_nope, v = kv[..., :QK_NOPE], kv[..., QK_NOPE:]
    k_pe = apply_rope(kr, cos, sin)
    scale = QK_DIM ** -0.5
    key_pos = jnp.arange(S)

    def attend_block(blk):
        start = blk * Q_BLOCK
        qn = lax.dynamic_slice_in_dim(q_nope, start, Q_BLOCK, axis=1)
        qp = lax.dynamic_slice_in_dim(q_pe, start, Q_BLOCK, axis=1)
        s = (jnp.einsum('bqhd,bkhd->bhqk', qn, k_nope)
             + jnp.einsum('bqhr,bkr->bhqk', qp, k_pe)).astype(jnp.float32) * scale
        q_pos = start + jnp.arange(Q_BLOCK)
        s = jnp.where(q_pos[:, None] >= key_pos[None, :], s, -jnp.inf)
        p = jax.nn.softmax(s, axis=-1).astype(v.dtype)
        return jnp.einsum('bhqk,bkhd->bqhd', p, v)

    out = lax.map(attend_block, jnp.arange(S // Q_BLOCK))
    out = out.transpose(1, 0, 2, 3, 4).reshape(B, S, MLA_WIDTH)
    return out * jax.nn.silu(mz)


def _fwd_setup_inputs(seed: int = 0) -> dict:
    key = jax.random.key(seed)
    ks = jax.random.split(key, 16)
    f32 = jnp.float32

    def w(k, shape, fan_in):
        return jax.random.normal(k, shape, f32) * fan_in ** -0.5

    def gain(k, shape):
        return 1.0 + 0.02 * jax.random.normal(k, shape, f32)

    return {
        "x": jax.random.normal(ks[0], (BATCH, SEQ, D_MODEL), f32),
        "norm_g": gain(ks[1], (DEPTH, D_MODEL)),
        "w_in": w(ks[2], (DEPTH, D_MODEL, IN_COLS), D_MODEL),
        "b_gate": 0.02 * jax.random.normal(ks[3], (DEPTH, N_BRANCH * D_MODEL), f32),
        "lb_logits": 0.1 * jax.random.normal(ks[4], (DEPTH + 1, HG_WIDTH), f32),
        "hg_norm_g": gain(ks[5], (DEPTH, HG_DIM)),
        "q_a_g": gain(ks[6], (DEPTH, Q_LORA)),
        "w_uq": w(ks[7], (DEPTH, Q_LORA, MLA_HEADS * QK_DIM), Q_LORA),
        "kv_a_g": gain(ks[8], (DEPTH, KV_LORA)),
        "w_ukv": w(ks[9], (DEPTH, KV_LORA, MLA_HEADS * (QK_NOPE + V_DIM)), KV_LORA),
        "w_proj_a": w(ks[10], (DEPTH, HG_WIDTH, D_MODEL), HG_WIDTH),
        "w_proj_b": w(ks[11], (DEPTH, MLA_WIDTH, D_MODEL), MLA_WIDTH),
        "w_out": w(ks[12], (DEPTH, D_MODEL, D_MODEL), D_MODEL),
        "final_norm_g": gain(ks[13], (D_MODEL,)),
    }


def _fwd_reference(x, norm_g, w_in, b_gate, lb_logits, hg_norm_g, q_a_g, w_uq, kv_a_g, w_ukv,
              w_proj_a, w_proj_b, w_out, final_norm_g):
    B, S, _ = x.shape
    lower_bounds = forget_lower_bounds(lb_logits)
    pts = _split_points()
    for l in range(DEPTH):
        h = rms_norm(x, norm_g[l])
        proj = h @ w_in[l]
        hq, hf, hi, hz, cq, ckv, kr, mz, glog = jnp.split(proj, pts, axis=-1)
        y_a = hgrn2_branch(hq, hf, hi, hz, lower_bounds[l], hg_norm_g[l])
        y_b = mla_branch(cq, ckv, kr, mz, q_a_g[l], w_uq[l], kv_a_g[l], w_ukv[l])
        gates = jax.nn.sigmoid((glog + b_gate[l]).astype(jnp.float32)).astype(x.dtype)
        gates = gates.reshape(B, S, N_BRANCH, D_MODEL)
        merged = gates[:, :, 0] * (y_a @ w_proj_a[l]) + gates[:, :, 1] * (y_b @ w_proj_b[l])
        x = x + merged @ w_out[l]
    return rms_norm(x, final_norm_g)


import jax as _jax
import jax.numpy as _jnp

TWIN_FORMAT = 'train_step'
FWD_PARAMS = ['x', 'norm_g', 'w_in', 'b_gate', 'lb_logits', 'hg_norm_g', 'q_a_g', 'w_uq', 'kv_a_g', 'w_ukv', 'w_proj_a', 'w_proj_b', 'w_out', 'final_norm_g']
TWIN_WEIGHTS = ['norm_g', 'w_in', 'b_gate', 'lb_logits', 'hg_norm_g', 'q_a_g', 'w_uq', 'kv_a_g', 'w_ukv', 'w_proj_a', 'w_proj_b', 'w_out', 'final_norm_g']
TWIN_DIFF_INPUT = 'x'
TWIN_INPUTS = ['x', 'norm_g', 'w_in', 'b_gate', 'lb_logits', 'hg_norm_g', 'q_a_g', 'w_uq', 'kv_a_g', 'w_ukv', 'w_proj_a', 'w_proj_b', 'w_out', 'final_norm_g', 'loss_target', 'm_norm_g', 'm_w_in', 'm_b_gate', 'm_lb_logits', 'm_hg_norm_g', 'm_q_a_g', 'm_w_uq', 'm_kv_a_g', 'm_w_ukv', 'm_w_proj_a', 'm_w_proj_b', 'm_w_out', 'm_final_norm_g', 'v_norm_g', 'v_w_in', 'v_b_gate', 'v_lb_logits', 'v_hg_norm_g', 'v_q_a_g', 'v_w_uq', 'v_kv_a_g', 'v_w_ukv', 'v_w_proj_a', 'v_w_proj_b', 'v_w_out', 'v_final_norm_g']
TWIN_OUTPUTS = ['loss', 'grad_x', 'grad_norm_g', 'grad_w_in', 'grad_b_gate', 'grad_lb_logits', 'grad_hg_norm_g', 'grad_q_a_g', 'grad_w_uq', 'grad_kv_a_g', 'grad_w_ukv', 'grad_w_proj_a', 'grad_w_proj_b', 'grad_w_out', 'grad_final_norm_g', 'delta_norm_g', 'delta_w_in', 'delta_b_gate', 'delta_lb_logits', 'delta_hg_norm_g', 'delta_q_a_g', 'delta_w_uq', 'delta_kv_a_g', 'delta_w_ukv', 'delta_w_proj_a', 'delta_w_proj_b', 'delta_w_out', 'delta_final_norm_g', 'new_m_norm_g', 'new_m_w_in', 'new_m_b_gate', 'new_m_lb_logits', 'new_m_hg_norm_g', 'new_m_q_a_g', 'new_m_w_uq', 'new_m_kv_a_g', 'new_m_w_ukv', 'new_m_w_proj_a', 'new_m_w_proj_b', 'new_m_w_out', 'new_m_final_norm_g', 'new_v_norm_g', 'new_v_w_in', 'new_v_b_gate', 'new_v_lb_logits', 'new_v_hg_norm_g', 'new_v_q_a_g', 'new_v_w_uq', 'new_v_kv_a_g', 'new_v_w_ukv', 'new_v_w_proj_a', 'new_v_w_proj_b', 'new_v_w_out', 'new_v_final_norm_g']
TWIN_LEAF_KINDS = {'loss': 'loss', 'grad_x': 'grad_x', 'grad_norm_g': 'grad_w', 'grad_w_in': 'grad_w', 'grad_b_gate': 'grad_w', 'grad_lb_logits': 'grad_w', 'grad_hg_norm_g': 'grad_w', 'grad_q_a_g': 'grad_w', 'grad_w_uq': 'grad_w', 'grad_kv_a_g': 'grad_w', 'grad_w_ukv': 'grad_w', 'grad_w_proj_a': 'grad_w', 'grad_w_proj_b': 'grad_w', 'grad_w_out': 'grad_w', 'grad_final_norm_g': 'grad_w', 'delta_norm_g': 'delta_w', 'delta_w_in': 'delta_w', 'delta_b_gate': 'delta_w', 'delta_lb_logits': 'delta_w', 'delta_hg_norm_g': 'delta_w', 'delta_q_a_g': 'delta_w', 'delta_w_uq': 'delta_w', 'delta_kv_a_g': 'delta_w', 'delta_w_ukv': 'delta_w', 'delta_w_proj_a': 'delta_w', 'delta_w_proj_b': 'delta_w', 'delta_w_out': 'delta_w', 'delta_final_norm_g': 'delta_w', 'new_m_norm_g': 'new_m', 'new_m_w_in': 'new_m', 'new_m_b_gate': 'new_m', 'new_m_lb_logits': 'new_m', 'new_m_hg_norm_g': 'new_m', 'new_m_q_a_g': 'new_m', 'new_m_w_uq': 'new_m', 'new_m_kv_a_g': 'new_m', 'new_m_w_ukv': 'new_m', 'new_m_w_proj_a': 'new_m', 'new_m_w_proj_b': 'new_m', 'new_m_w_out': 'new_m', 'new_m_final_norm_g': 'new_m', 'new_v_norm_g': 'new_v', 'new_v_w_in': 'new_v', 'new_v_b_gate': 'new_v', 'new_v_lb_logits': 'new_v', 'new_v_hg_norm_g': 'new_v', 'new_v_q_a_g': 'new_v', 'new_v_w_uq': 'new_v', 'new_v_kv_a_g': 'new_v', 'new_v_w_ukv': 'new_v', 'new_v_w_proj_a': 'new_v', 'new_v_w_proj_b': 'new_v', 'new_v_w_out': 'new_v', 'new_v_final_norm_g': 'new_v'}


def _forward(args):
    return _fwd_reference(*[args[k] for k in FWD_PARAMS])


def _output_shape():
    out = _jax.eval_shape(lambda: _forward(_fwd_setup_inputs(0)))
    return out.shape, out.dtype

N_MICROBATCH = 1
ADAM_LR = 0.001
ADAM_B1 = 0.9
ADAM_B2 = 0.999
ADAM_EPS = 1e-08
ADAM_WD = 0.01
ADAM_STEP = 10
PER_EXAMPLE_BATCH_AXIS = {'x': 0, 'loss_target': 0}
SHARED_INPUTS = []
_WEIGHT_DTYPES = {'norm_g': _jnp.float32, 'w_in': _jnp.float32, 'b_gate': _jnp.float32, 'lb_logits': _jnp.float32, 'hg_norm_g': _jnp.float32, 'q_a_g': _jnp.float32, 'w_uq': _jnp.float32, 'kv_a_g': _jnp.float32, 'w_ukv': _jnp.float32, 'w_proj_a': _jnp.float32, 'w_proj_b': _jnp.float32, 'w_out': _jnp.float32, 'final_norm_g': _jnp.float32}
MOMENT_SCALE = {'norm_g': 1.156174e-01, 'w_in': 4.315888e-02, 'b_gate': 2.250798e-02, 'lb_logits': 7.233551e-03, 'hg_norm_g': 2.194686e-01, 'q_a_g': 2.336998e-02, 'w_uq': 1.167582e-02, 'kv_a_g': 4.400831e-02, 'w_ukv': 1.468468e-02, 'w_proj_a': 7.665569e-02, 'w_proj_b': 1.683889e-02, 'w_out': 7.834921e-02, 'final_norm_g': 6.395048e+01}


def _to_microbatches(a, axis):
    t = _jnp.moveaxis(a, axis, 0)
    t = t.reshape((N_MICROBATCH, t.shape[0] // N_MICROBATCH) + t.shape[1:])
    return _jnp.moveaxis(t, 1, axis + 1)


def setup_inputs(seed: int = 0) -> dict:
    inp = _fwd_setup_inputs(seed)
    key = _jax.random.fold_in(_jax.random.key(seed), 7919)
    shape, _ = _output_shape()
    out = dict(inp)
    out["loss_target"] = _jax.random.normal(_jax.random.fold_in(key, 0), shape, _jnp.float32)
    for i, name in enumerate(TWIN_WEIGHTS):
        w = inp[name].astype(_jnp.float32)
        if MOMENT_SCALE is None:
            s = _jnp.sqrt(_jnp.mean(_jnp.square(w)) + 1e-30)
        else:
            s = MOMENT_SCALE[name]
        km, kv = _jax.random.split(_jax.random.fold_in(key, i + 1))
        out[name] = w
        out["m_" + name] = s * _jax.random.normal(km, w.shape, _jnp.float32)
        out["v_" + name] = (s * s) * _jax.random.uniform(kv, w.shape, _jnp.float32, 0.5, 1.5)
    if N_MICROBATCH > 1:
        for name, axis in PER_EXAMPLE_BATCH_AXIS.items():
            out[name] = _to_microbatches(out[name], axis)
    return {'x': out['x'], 'norm_g': out['norm_g'], 'w_in': out['w_in'], 'b_gate': out['b_gate'], 'lb_logits': out['lb_logits'], 'hg_norm_g': out['hg_norm_g'], 'q_a_g': out['q_a_g'], 'w_uq': out['w_uq'], 'kv_a_g': out['kv_a_g'], 'w_ukv': out['w_ukv'], 'w_proj_a': out['w_proj_a'], 'w_proj_b': out['w_proj_b'], 'w_out': out['w_out'], 'final_norm_g': out['final_norm_g'], 'loss_target': out['loss_target'], 'm_norm_g': out['m_norm_g'], 'm_w_in': out['m_w_in'], 'm_b_gate': out['m_b_gate'], 'm_lb_logits': out['m_lb_logits'], 'm_hg_norm_g': out['m_hg_norm_g'], 'm_q_a_g': out['m_q_a_g'], 'm_w_uq': out['m_w_uq'], 'm_kv_a_g': out['m_kv_a_g'], 'm_w_ukv': out['m_w_ukv'], 'm_w_proj_a': out['m_w_proj_a'], 'm_w_proj_b': out['m_w_proj_b'], 'm_w_out': out['m_w_out'], 'm_final_norm_g': out['m_final_norm_g'], 'v_norm_g': out['v_norm_g'], 'v_w_in': out['v_w_in'], 'v_b_gate': out['v_b_gate'], 'v_lb_logits': out['v_lb_logits'], 'v_hg_norm_g': out['v_hg_norm_g'], 'v_q_a_g': out['v_q_a_g'], 'v_w_uq': out['v_w_uq'], 'v_kv_a_g': out['v_kv_a_g'], 'v_w_ukv': out['v_w_ukv'], 'v_w_proj_a': out['v_w_proj_a'], 'v_w_proj_b': out['v_w_proj_b'], 'v_w_out': out['v_w_out'], 'v_final_norm_g': out['v_final_norm_g']}


def _loss(weights, diff, rest, loss_target):
    with _jax.named_scope("forward"):
        args = {**rest, TWIN_DIFF_INPUT: diff, **{k: w.astype(_WEIGHT_DTYPES[k]) for k, w in weights.items()}}
        y = _forward(args)
    with _jax.named_scope("loss_head"):
        err = _jnp.square(y.astype(_jnp.float32) - loss_target)
        return 0.5 * _jnp.sum(_jnp.mean(err, axis=-1)) if err.ndim else 0.5 * err


def _adamw(w, g, m, v):
    m = ADAM_B1 * m + (1.0 - ADAM_B1) * g
    v = ADAM_B2 * v + (1.0 - ADAM_B2) * _jnp.square(g)
    m_hat = m / (1.0 - ADAM_B1 ** ADAM_STEP)
    v_hat = v / (1.0 - ADAM_B2 ** ADAM_STEP)
    delta = -ADAM_LR * (m_hat / (_jnp.sqrt(v_hat) + ADAM_EPS) + ADAM_WD * w)
    return delta, m, v


def reference(x, norm_g, w_in, b_gate, lb_logits, hg_norm_g, q_a_g, w_uq, kv_a_g, w_ukv, w_proj_a, w_proj_b, w_out, final_norm_g, loss_target, m_norm_g, m_w_in, m_b_gate, m_lb_logits, m_hg_norm_g, m_q_a_g, m_w_uq, m_kv_a_g, m_w_ukv, m_w_proj_a, m_w_proj_b, m_w_out, m_final_norm_g, v_norm_g, v_w_in, v_b_gate, v_lb_logits, v_hg_norm_g, v_q_a_g, v_w_uq, v_kv_a_g, v_w_ukv, v_w_proj_a, v_w_proj_b, v_w_out, v_final_norm_g):
    given = dict(x=x, norm_g=norm_g, w_in=w_in, b_gate=b_gate, lb_logits=lb_logits, hg_norm_g=hg_norm_g, q_a_g=q_a_g, w_uq=w_uq, kv_a_g=kv_a_g, w_ukv=w_ukv, w_proj_a=w_proj_a, w_proj_b=w_proj_b, w_out=w_out, final_norm_g=final_norm_g, loss_target=loss_target, m_norm_g=m_norm_g, m_w_in=m_w_in, m_b_gate=m_b_gate, m_lb_logits=m_lb_logits, m_hg_norm_g=m_hg_norm_g, m_q_a_g=m_q_a_g, m_w_uq=m_w_uq, m_kv_a_g=m_kv_a_g, m_w_ukv=m_w_ukv, m_w_proj_a=m_w_proj_a, m_w_proj_b=m_w_proj_b, m_w_out=m_w_out, m_final_norm_g=m_final_norm_g, v_norm_g=v_norm_g, v_w_in=v_w_in, v_b_gate=v_b_gate, v_lb_logits=v_lb_logits, v_hg_norm_g=v_hg_norm_g, v_q_a_g=v_q_a_g, v_w_uq=v_w_uq, v_kv_a_g=v_kv_a_g, v_w_ukv=v_w_ukv, v_w_proj_a=v_w_proj_a, v_w_proj_b=v_w_proj_b, v_w_out=v_w_out, v_final_norm_g=v_final_norm_g)
    weights = {n: given[n] for n in TWIN_WEIGHTS}
    shared = {n: given[n] for n in SHARED_INPUTS}
    per_example = {n: given[n] for n in ['x']}
    grad_fn = _jax.value_and_grad(_loss, argnums=(0, 1))

    def one_microbatch(ex, loss_target):
        ex = dict(ex)
        diff = ex.pop(TWIN_DIFF_INPUT)
        return grad_fn(weights, diff, {**shared, **ex}, loss_target)

    if N_MICROBATCH == 1:
        loss, (grad_w, grad_x) = one_microbatch(per_example, given["loss_target"])
    else:
        def body(carry, xs):
            loss_sum, grad_sum = carry
            l_k, (gw_k, gx_k) = one_microbatch(xs[0], xs[1])
            with _jax.named_scope("update"):
                return (loss_sum + l_k, _jax.tree.map(_jnp.add, grad_sum, gw_k)), gx_k

        init = (_jnp.zeros((), _jnp.float32), _jax.tree.map(_jnp.zeros_like, weights))
        (loss, grad_w), grad_x = _jax.lax.scan(body, init, (per_example, given["loss_target"]))
    with _jax.named_scope("update"):
        delta_w, new_m, new_v = {}, {}, {}
        for n in TWIN_WEIGHTS:
            delta_w[n], new_m[n], new_v[n] = _adamw(weights[n], grad_w[n], given["m_" + n], given["v_" + n])
    return (loss, grad_x, *[grad_w[n] for n in TWIN_WEIGHTS], *[delta_w[n] for n in TWIN_WEIGHTS],
            *[new_m[n] for n in TWIN_WEIGHTS], *[new_v[n] for n in TWIN_WEIGHTS])
```

```python
import functools

import jax
import jax.numpy as jnp
from jax import lax
from jax.experimental import pallas as pl
from jax.experimental.pallas import tpu as pltpu

D_MODEL = 1024
HEADS = 8
HEAD_DIM = 128
HG_CHUNK = 32
Q_LORA = 384
KV_LORA = 256
QK_ROPE = 64
QK_DIM = 192
ROPE_THETA = 10000.0
EPS = 1e-6
IN_COLS = 7872
ADAM_LR = 0.001
ADAM_B1 = 0.9
ADAM_B2 = 0.999
ADAM_EPS = 1e-08
ADAM_WD = 0.01
ADAM_STEP = 10

N_DEV = 8
SEG = 1024
PROJ_W = 8 * SEG
SMALL_SEG = 4
MZ_SEG = 5
GL_SEG = 6
HEAD_PAD = 256
PACK_COLS = 1024
ROWS_W_IN = 984
ROWS_W_UQ = 72
ROWS_W_UKV = 64
ROWS_W_PROJ = 128
ROWS_BIG = ROWS_W_IN + ROWS_W_UQ + ROWS_W_UKV + 3 * ROWS_W_PROJ
ROWS_SMALL = 8
ROWS_PACK = ROWS_BIG + ROWS_SMALL

MXU_DTYPE = jnp.bfloat16
VMEM_LIMIT = 48 * 1024 * 1024

TM_INPROJ = 512
T_HGRN = 256
TM_ROW = 256
T_ATT = 512
TS_TN = 512
TM_DH = 512
TR_ADAM = 216

F32 = jnp.float32
MESH = pl.DeviceIdType.MESH


def _dot(a, b):
    return jnp.dot(a, b, preferred_element_type=F32)


def _dot_nt(a, b):
    return lax.dot_general(a, b, (((1,), (1,)), ((), ())), preferred_element_type=F32)


def _dot_tn(a, b):
    return lax.dot_general(a, b, (((0,), (0,)), ((), ())), preferred_element_type=F32)


def _dot_exact(a, b):
    return jnp.dot(a, b, preferred_element_type=F32, precision=lax.Precision.HIGHEST)


def _mx(a):
    return a.astype(MXU_DTYPE)


def _sigmoid(x):
    return 1.0 / (1.0 + jnp.exp(-x))


def _params(**kw):
    return pltpu.CompilerParams(vmem_limit_bytes=VMEM_LIMIT, **kw)


def _bcast_rows(row, n):
    return jnp.broadcast_to(row, (n, row.shape[-1]))


HBM_SPEC = pl.BlockSpec(memory_space=pltpu.HBM)


def _all_gather_packed(shard):
    rows, cols = shard.shape

    def body(x_ref, out_ref, send_sems, recv_sems, local_sem):
        x, y, c = lax.axis_index("x"), lax.axis_index("y"), lax.axis_index("c")
        me, sibling = (x, y, c), (x, y, 1 - c)
        chips = [(1 - x, y), (x, 1 - y), (1 - x, 1 - y)]

        def slot(px, py, pc):
            return out_ref.at[4 * px + 2 * py + pc]

        def copy(k, block, to, src=None):
            return pltpu.make_async_remote_copy(
                src_ref=slot(*block) if src is None else src,
                dst_ref=slot(*block),
                send_sem=send_sems.at[k],
                recv_sem=recv_sems.at[k],
                device_id=to,
                device_id_type=MESH,
            )

        mine = pltpu.make_async_copy(x_ref, slot(*me), local_sem)
        mine.start()
        first = [copy(0, me, sibling, src=x_ref)]
        first += [copy(1 + j, me, (*chip, c), src=x_ref) for j, chip in enumerate(chips)]
        for cp in first:
            cp.start()
        passed = [copy(4 + j, (*chip, c), sibling) for j, chip in enumerate(chips)]
        for j, chip in enumerate(chips):
            copy(1 + j, (*chip, c), me).wait_recv()
            passed[j].start()
        copy(0, sibling, me).wait_recv()
        for j, chip in enumerate(chips):
            copy(4 + j, (*chip, 1 - c), me).wait_recv()
        for cp in first + passed:
            cp.wait_send()
        mine.wait()

    return pl.pallas_call(
        body,
        name="ag_weights",
        out_shape=jax.ShapeDtypeStruct((N_DEV, rows, cols), shard.dtype),
        in_specs=[HBM_SPEC],
        out_specs=HBM_SPEC,
        scratch_shapes=[
            pltpu.SemaphoreType.DMA((7,)),
            pltpu.SemaphoreType.DMA((7,)),
            pltpu.SemaphoreType.DMA,
        ],
    )(shard)


def _reduce_scatter_exchange(slabs):
    _, rows, cols = slabs.shape

    def body(g_ref, recv_ref, send_sems, recv_sems, local_sem):
        x, y, c = lax.axis_index("x"), lax.axis_index("y"), lax.axis_index("c")
        me = 4 * x + 2 * y + c

        def copy(k):
            px, py, pc = x ^ ((k >> 2) & 1), y ^ ((k >> 1) & 1), c ^ (k & 1)
            return pltpu.make_async_remote_copy(
                src_ref=g_ref.at[4 * px + 2 * py + pc],
                dst_ref=recv_ref.at[me],
                send_sem=send_sems.at[k - 1],
                recv_sem=recv_sems.at[k - 1],
                device_id=(px, py, pc),
                device_id_type=MESH,
            )

        def landing(k):
            px, py, pc = x ^ ((k >> 2) & 1), y ^ ((k >> 1) & 1), c ^ (k & 1)
            peer = 4 * px + 2 * py + pc
            return pltpu.make_async_remote_copy(
                src_ref=g_ref.at[peer],
                dst_ref=recv_ref.at[peer],
                send_sem=send_sems.at[k - 1],
                recv_sem=recv_sems.at[k - 1],
                device_id=(px, py, pc),
                device_id_type=MESH,
            )

        mine = pltpu.make_async_copy(g_ref.at[me], recv_ref.at[me], local_sem)
        mine.start()
        sends = [copy(k) for k in range(1, N_DEV)]
        for cp in sends:
            cp.start()
        for k in range(1, N_DEV):
            landing(k).wait_recv()
        for cp in sends:
            cp.wait_send()
        mine.wait()

    return pl.pallas_call(
        body,
        name="rs_grads",
        out_shape=jax.ShapeDtypeStruct(slabs.shape, slabs.dtype),
        in_specs=[HBM_SPEC],
        out_specs=HBM_SPEC,
        scratch_shapes=[
            pltpu.SemaphoreType.DMA((7,)),
            pltpu.SemaphoreType.DMA((7,)),
            pltpu.SemaphoreType.DMA,
        ],
    )(slabs)


def _sum_adamw(recv, w, m, v):
    rows, cols = w.shape
    tr = TR_ADAM if rows % TR_ADAM == 0 else rows

    def body(r_ref, w_ref, m_ref, v_ref, g_out, d_out, m_out, v_out):
        g = r_ref[0]
        for i in range(1, N_DEV):
            g = g + r_ref[i]
        m_new = ADAM_B1 * m_ref[...] + (1.0 - ADAM_B1) * g
        v_new = ADAM_B2 * v_ref[...] + (1.0 - ADAM_B2) * (g * g)
        m_hat = m_new / (1.0 - ADAM_B1 ** ADAM_STEP)
        v_hat = v_new / (1.0 - ADAM_B2 ** ADAM_STEP)
        g_out[...] = g
        d_out[...] = -ADAM_LR * (m_hat / (jnp.sqrt(v_hat) + ADAM_EPS) + ADAM_WD * w_ref[...])
        m_out[...] = m_new
        v_out[...] = v_new

    row_spec = pl.BlockSpec((tr, cols), lambda i: (i, 0))
    shape = jax.ShapeDtypeStruct((rows, cols), F32)
    return pl.pallas_call(
        body,
        name="sum_adamw",
        grid=(rows // tr,),
        in_specs=[pl.BlockSpec((N_DEV, tr, cols), lambda i: (0, i, 0)), row_spec, row_spec, row_spec],
        out_specs=[row_spec] * 4,
        out_shape=[shape] * 4,
        compiler_params=_params(),
    )(recv, w, m, v)


def _inproj(x, norm_g, w_in_p):
    s = x.shape[0]
    tm = min(s, TM_INPROJ)

    def body(x_ref, g_ref, w_ref, proj_ref, h_ref):
        xf = x_ref[...]
        r = lax.rsqrt(jnp.mean(xf * xf, axis=-1, keepdims=True) + EPS)
        h = _mx(xf * r * g_ref[...])
        h_ref[...] = h
        proj_ref[...] = _dot(h, w_ref[...])

    return pl.pallas_call(
        body,
        name="inproj",
        grid=(s // tm, PROJ_W // SEG),
        in_specs=[
            pl.BlockSpec((tm, D_MODEL), lambda i, j: (i, 0)),
            pl.BlockSpec((1, D_MODEL), lambda i, j: (0, 0)),
            pl.BlockSpec((D_MODEL, SEG), lambda i, j: (0, j)),
        ],
        out_specs=[
            pl.BlockSpec((tm, SEG), lambda i, j: (i, j)),
            pl.BlockSpec((tm, D_MODEL), lambda i, j: (i, 0)),
        ],
        out_shape=[
            jax.ShapeDtypeStruct((s, PROJ_W), F32),
            jax.ShapeDtypeStruct((s, D_MODEL), MXU_DTYPE),
        ],
        compiler_params=_params(),
    )(x, norm_g, w_in_p)


def _chunk_masks(t):
    row = lax.broadcasted_iota(jnp.int32, (t, t), 0)
    col = lax.broadcasted_iota(jnp.int32, (t, t), 1)
    same = (row // HG_CHUNK) == (col // HG_CHUNK)
    lower = same & (col <= row)
    upper = same & (col >= row)
    return same, lower, upper


def _hgrn_gates(hq, hf, lb_logits, same_f, lower_f):
    lb = _sigmoid(lb_logits[0:1, :] - lb_logits[1:2, :])
    sig = _sigmoid(hf)
    f = lb + (1.0 - lb) * sig
    sq = _sigmoid(hq)
    q = hq * sq
    k = 1.0 - f
    logf = jnp.log(f)
    bcum = _dot_exact(lower_f, logf)
    blast = _dot_exact(same_f, logf)
    eb = jnp.exp(bcum)
    enb = jnp.exp(-bcum)
    eo = jnp.exp(blast - bcum)
    return dict(lb=lb, sig=sig, f=f, sq=sq, q=q, k=k, eb=eb, enb=enb, eo=eo,
                qi=q * eb, ki=k * enb, ko=k * eo, dec=jnp.exp(blast))


def _hgrn_fwd(proj, lb_logits, hg_norm_g):
    s = proj.shape[0]
    t = min(s, T_HGRN)
    nb = s // t
    nc = t // HG_CHUNK

    def body(hq_ref, hf_ref, hi_ref, hz_ref, lb_ref, g_ref, o_ref, ya_ref, st_ref, state):
        b = pl.program_id(1)

        @pl.when(b == 0)
        def _():
            state[...] = jnp.zeros_like(state)

        st = state[...]
        st_ref[0, 0] = st
        same, lower, _ = _chunk_masks(t)
        gt = _hgrn_gates(hq_ref[...], hf_ref[...], lb_ref[...], same.astype(F32), lower.astype(F32))
        v = hi_ref[...]
        vb = _mx(v)
        qib, kib, kob = _mx(gt["qi"]), _mx(gt["ki"]), _mx(gt["ko"])
        a = jnp.where(lower, _dot_nt(qib, kib), 0.0)
        o_intra = _dot(_mx(a), vb)
        outs = []
        for c in range(nc):
            sl = slice(c * HG_CHUNK, (c + 1) * HG_CHUNK)
            outs.append(o_intra[sl] + _dot_nt(qib[sl], _mx(st)))
            st = st * gt["dec"][c * HG_CHUNK:c * HG_CHUNK + 1, :] + _dot_tn(vb[sl], kob[sl])
        state[...] = st
        o = jnp.concatenate(outs, axis=0)
        o_ref[...] = o
        r = lax.rsqrt(jnp.mean(o * o, axis=-1, keepdims=True) + EPS)
        hz = hz_ref[...]
        ya_ref[...] = _mx((o * r * g_ref[...]) * (hz * _sigmoid(hz)))

    def seg(k):
        return pl.BlockSpec((t, HEAD_DIM), lambda h, b, k=k: (b, k * HEADS + h))

    return pl.pallas_call(
        body,
        name="hgrn_fwd",
        grid=(HEADS, nb),
        in_specs=[seg(0), seg(1), seg(2), seg(3),
                  pl.BlockSpec((2, HEAD_DIM), lambda h, b: (0, h)),
                  pl.BlockSpec((1, HEAD_DIM), lambda h, b: (0, 0))],
        out_specs=[
            pl.BlockSpec((t, HEAD_DIM), lambda h, b: (b, h)),
            pl.BlockSpec((t, HEAD_DIM), lambda h, b: (b, h)),
            pl.BlockSpec((1, 1, HEAD_DIM, HEAD_DIM), lambda h, b: (b, h, 0, 0)),
        ],
        out_shape=[
            jax.ShapeDtypeStruct((s, D_MODEL), F32),
            jax.ShapeDtypeStruct((s, D_MODEL), MXU_DTYPE),
            jax.ShapeDtypeStruct((nb, HEADS, HEAD_DIM, HEAD_DIM), F32),
        ],
        scratch_shapes=[pltpu.VMEM((HEAD_DIM, HEAD_DIM), F32)],
        compiler_params=_params(),
    )(proj, proj, proj, proj, lb_logits, hg_norm_g)


def _rope(x, cos, sin_a, sin_b):
    return x * cos + pltpu.roll(x, 96, 1) * sin_a + pltpu.roll(x, 32, 1) * sin_b


def _rope_t(d, cos, sin_a, sin_b):
    return d * cos + pltpu.roll(d * sin_a, 32, 1) + pltpu.roll(d * sin_b, 96, 1)


def _mla_prep(proj, q_a_g, kv_a_g, w_uq_p, w_kn, w_v, cos, sin_a, sin_b):
    s = proj.shape[0]
    tm = min(s, TM_ROW)

    def body(sm_ref, gq_ref, gk_ref, wq_ref, wkn_ref, wv_ref, cos_ref, sa_ref, sb_ref,
             q_ref, k_ref, v_ref, cqn_ref, ckvn_ref):
        small = sm_ref[...]
        cq = small[:, :Q_LORA]
        ckv = small[:, Q_LORA:Q_LORA + KV_LORA]
        krp = small[:, Q_LORA + KV_LORA:Q_LORA + KV_LORA + HEAD_DIM]
        rq = lax.rsqrt(jnp.mean(cq * cq, axis=-1, keepdims=True) + EPS)
        rk = lax.rsqrt(jnp.mean(ckv * ckv, axis=-1, keepdims=True) + EPS)
        cqn = _mx(cq * rq * gq_ref[...])
        ckvn = _mx(ckv * rk * gk_ref[...])
        cqn_ref[...] = cqn
        ckvn_ref[...] = ckvn
        q = _dot(cqn, wq_ref[...])
        kn = _dot(ckvn, wkn_ref[...])
        v_ref[...] = _mx(_dot(ckvn, wv_ref[...]))
        cos_t, sa, sb = cos_ref[...], sa_ref[...], sb_ref[...]
        kpe = _mx(_rope(krp, cos_t, sa, sb))
        for h in range(HEADS):
            lo = h * HEAD_PAD
            q_ref[:, lo:lo + HEAD_DIM] = _mx(q[:, lo:lo + HEAD_DIM])
            q_ref[:, lo + HEAD_DIM:lo + HEAD_PAD] = _mx(_rope(q[:, lo + HEAD_DIM:lo + HEAD_PAD], cos_t, sa, sb))
            k_ref[:, lo:lo + HEAD_DIM] = _mx(kn[:, h * HEAD_DIM:(h + 1) * HEAD_DIM])
            k_ref[:, lo + HEAD_DIM:lo + HEAD_PAD] = kpe

    def const(shape):
        return pl.BlockSpec(shape, lambda i: (0, 0))

    def rows(w):
        return pl.BlockSpec((tm, w), lambda i: (i, 0))

    return pl.pallas_call(
        body,
        name="mla_prep",
        grid=(s // tm,),
        in_specs=[
            pl.BlockSpec((tm, SEG), lambda i: (i, SMALL_SEG)),
            const((1, Q_LORA)), const((1, KV_LORA)),
            const((Q_LORA, HEADS * HEAD_PAD)), const((KV_LORA, D_MODEL)), const((KV_LORA, D_MODEL)),
            rows(HEAD_DIM), rows(HEAD_DIM), rows(HEAD_DIM),
        ],
        out_specs=[rows(HEADS * HEAD_PAD), rows(HEADS * HEAD_PAD), rows(D_MODEL), rows(Q_LORA), rows(KV_LORA)],
        out_shape=[
            jax.ShapeDtypeStruct((s, HEADS * HEAD_PAD), MXU_DTYPE),
            jax.ShapeDtypeStruct((s, HEADS * HEAD_PAD), MXU_DTYPE),
            jax.ShapeDtypeStruct((s, D_MODEL), MXU_DTYPE),
            jax.ShapeDtypeStruct((s, Q_LORA), MXU_DTYPE),
            jax.ShapeDtypeStruct((s, KV_LORA), MXU_DTYPE),
        ],
        compiler_params=_params(),
    )(proj, q_a_g, kv_a_g, w_uq_p, w_kn, w_v, cos, sin_a, sin_b)


def _causal_mask(qi, ki, t):
    row = lax.broadcasted_iota(jnp.int32, (t, t), 0) + qi * t
    col = lax.broadcasted_iota(jnp.int32, (t, t), 1) + ki * t
    return row >= col


def _flash_fwd(q_all, k_all, v_all, proj):
    s = q_all.shape[0]
    t = min(s, T_ATT)
    n = s // t
    scale = QK_DIM ** -0.5

    def body(q_ref, k_ref, v_ref, mz_ref, ao_ref, lse_ref, yb_ref, m_sc, l_sc, acc_sc):
        qi, ki = pl.program_id(1), pl.program_id(2)

        @pl.when(ki == 0)
        def _():
            m_sc[...] = jnp.full_like(m_sc, -jnp.inf)
            l_sc[...] = jnp.zeros_like(l_sc)
            acc_sc[...] = jnp.zeros_like(acc_sc)

        @pl.when(ki <= qi)
        def _():
            sc = _dot_nt(q_ref[...], k_ref[...]) * scale
            sc = jnp.where(_causal_mask(qi, ki, t), sc, -jnp.inf)
            m_prev = m_sc[...]
            m_new = jnp.maximum(m_prev, jnp.max(sc, axis=-1, keepdims=True))
            p = jnp.exp(sc - m_new)
            alpha = jnp.exp(m_prev - m_new)
            l_sc[...] = alpha * l_sc[...] + jnp.sum(p, axis=-1, keepdims=True)
            acc_sc[...] = alpha * acc_sc[...] + _dot(_mx(p), v_ref[...])
            m_sc[...] = m_new

        @pl.when(ki == qi)
        def _():
            ao = acc_sc[...] / l_sc[...]
            ao_ref[...] = ao
            lse_ref[...] = _bcast_cols(m_sc[...] + jnp.log(l_sc[...]))
            mz = mz_ref[...]
            yb_ref[...] = _mx(ao * (mz * _sigmoid(mz)))

    def _bcast_cols(col):
        return jnp.broadcast_to(col, (t, HEAD_DIM))

    kv_map = lambda h, qi, ki: (jnp.minimum(ki, qi), h)
    q_map = lambda h, qi, ki: (qi, h)
    return pl.pallas_call(
        body,
        name="flash_fwd",
        grid=(HEADS, n, n),
        in_specs=[
            pl.BlockSpec((t, HEAD_PAD), q_map),
            pl.BlockSpec((t, HEAD_PAD), kv_map),
            pl.BlockSpec((t, HEAD_DIM), kv_map),
            pl.BlockSpec((t, HEAD_DIM), lambda h, qi, ki: (qi, MZ_SEG * HEADS + h)),
        ],
        out_specs=[pl.BlockSpec((t, HEAD_DIM), q_map)] * 3,
        out_shape=[
            jax.ShapeDtypeStruct((s, D_MODEL), F32),
            jax.ShapeDtypeStruct((s, D_MODEL), F32),
            jax.ShapeDtypeStruct((s, D_MODEL), MXU_DTYPE),
        ],
        scratch_shapes=[
            pltpu.VMEM((t, 1), F32),
            pltpu.VMEM((t, 1), F32),
            pltpu.VMEM((t, HEAD_DIM), F32),
        ],
        compiler_params=_params(),
    )(q_all, k_all, v_all, proj)


def _merge_fwd_bwd(x, target, ya, yb, proj, b_gate, final_g, w_pa, w_pb, w_out):
    s = x.shape[0]
    tm = min(s, TM_ROW)

    def body(x_ref, t_ref, ya_ref, yb_ref, g0_ref, g1_ref, bg_ref, fg_ref, wpa_ref, wpb_ref, wo_ref,
             dx2_ref, dya_ref, dyb_ref, dg0_ref, dg1_ref, mb_ref, dpab_ref, dpbb_ref, dx2b_ref,
             loss_ref, dfg_ref, dbg_ref):
        i = pl.program_id(0)

        @pl.when(i == 0)
        def _():
            loss_ref[...] = jnp.zeros_like(loss_ref)
            dfg_ref[...] = jnp.zeros_like(dfg_ref)
            dbg_ref[...] = jnp.zeros_like(dbg_ref)

        pa = _dot(ya_ref[...], wpa_ref[...])
        pb = _dot(yb_ref[...], wpb_ref[...])
        bg = bg_ref[...]
        g0 = _sigmoid(g0_ref[...] + bg[:, :D_MODEL])
        g1 = _sigmoid(g1_ref[...] + bg[:, D_MODEL:])
        merged = g0 * pa + g1 * pb
        mb = _mx(merged)
        mb_ref[...] = mb
        x2 = x_ref[...] + _dot(mb, wo_ref[...])
        r = lax.rsqrt(jnp.mean(x2 * x2, axis=-1, keepdims=True) + EPS)
        xn = x2 * r
        fg = fg_ref[...]
        diff = xn * fg - t_ref[...]
        loss_ref[...] += 0.5 * jnp.sum(jnp.mean(diff * diff, axis=-1, keepdims=True))
        dy = diff * (1.0 / D_MODEL)
        dfg_ref[...] += _bcast_rows(jnp.sum(dy * xn, axis=0, keepdims=True), 8)
        tt = dy * fg
        dx2 = r * (tt - xn * jnp.mean(tt * xn, axis=-1, keepdims=True))
        dx2_ref[...] = dx2
        dx2b = _mx(dx2)
        dx2b_ref[...] = dx2b
        dmerged = _dot_nt(dx2b, wo_ref[...])
        dpa = _mx(dmerged * g0)
        dpb = _mx(dmerged * g1)
        dpab_ref[...] = dpa
        dpbb_ref[...] = dpb
        dg0 = dmerged * pa * (g0 * (1.0 - g0))
        dg1 = dmerged * pb * (g1 * (1.0 - g1))
        dg0_ref[...] = _mx(dg0)
        dg1_ref[...] = _mx(dg1)
        dbg_ref[:, :D_MODEL] += _bcast_rows(jnp.sum(dg0, axis=0, keepdims=True), 8)
        dbg_ref[:, D_MODEL:] += _bcast_rows(jnp.sum(dg1, axis=0, keepdims=True), 8)
        dya_ref[...] = _dot_nt(dpa, wpa_ref[...])
        dyb_ref[...] = _dot_nt(dpb, wpb_ref[...])

    def rows(w=D_MODEL):
        return pl.BlockSpec((tm, w), lambda i: (i, 0))

    def const(shape):
        return pl.BlockSpec(shape, lambda i: (0, 0))

    f32 = jax.ShapeDtypeStruct((s, D_MODEL), F32)
    b16 = jax.ShapeDtypeStruct((s, D_MODEL), MXU_DTYPE)
    return pl.pallas_call(
        body,
        name="merge_fwd_bwd",
        grid=(s // tm,),
        in_specs=[
            rows(), rows(), rows(), rows(),
            pl.BlockSpec((tm, SEG), lambda i: (i, GL_SEG)),
            pl.BlockSpec((tm, SEG), lambda i: (i, GL_SEG + 1)),
            const((1, 2 * D_MODEL)), const((1, D_MODEL)),
            const((D_MODEL, D_MODEL)), const((D_MODEL, D_MODEL)), const((D_MODEL, D_MODEL)),
        ],
        out_specs=[rows()] * 9 + [const((8, HEAD_DIM)), const((8, D_MODEL)), const((8, 2 * D_MODEL))],
        out_shape=[f32, f32, f32, b16, b16, b16, b16, b16, b16,
                   jax.ShapeDtypeStruct((8, HEAD_DIM), F32),
                   jax.ShapeDtypeStruct((8, D_MODEL), F32),
                   jax.ShapeDtypeStruct((8, 2 * D_MODEL), F32)],
        compiler_params=_params(),
    )(x, target, ya, yb, proj, proj, b_gate, final_g, w_pa, w_pb, w_out)


def _attn_gate_bwd(dyb, ao, proj):
    s = dyb.shape[0]
    tm = min(s, TM_ROW)

    def body(dyb_ref, ao_ref, mz_ref, dao_ref, dmz_ref):
        mz = mz_ref[...]
        sg = _sigmoid(mz)
        d = dyb_ref[...]
        dao_ref[...] = _mx(d * (mz * sg))
        dmz_ref[...] = _mx(d * ao_ref[...] * (sg + mz * sg * (1.0 - sg)))

    rows = pl.BlockSpec((tm, D_MODEL), lambda i: (i, 0))
    b16 = jax.ShapeDtypeStruct((s, D_MODEL), MXU_DTYPE)
    return pl.pallas_call(
        body,
        name="attn_gate_bwd",
        grid=(s // tm,),
        in_specs=[rows, rows, pl.BlockSpec((tm, SEG), lambda i: (i, MZ_SEG))],
        out_specs=[rows, rows],
        out_shape=[b16, b16],
        compiler_params=_params(),
    )(dyb, ao, proj)


def _flash_bwd(q_all, k_all, v_all, dao, ao, lse):
    s = q_all.shape[0]
    t = min(s, T_ATT)
    n = s // t
    scale = QK_DIM ** -0.5

    def body(q_ref, k_ref, v_ref, do_ref, ao_ref, lse_ref, dq_ref, dk_ref, dv_ref, dk_acc, dv_acc):
        ki, qi = pl.program_id(1), pl.program_id(2)

        @pl.when(qi == ki)
        def _():
            dk_acc[...] = jnp.zeros_like(dk_acc)
            dv_acc[...] = jnp.zeros_like(dv_acc)

        @pl.when(qi >= ki)
        def _():
            q, k = q_ref[...], k_ref[...]
            sc = _dot_nt(q, k) * scale
            sc = jnp.where(_causal_mask(qi, ki, t), sc, -jnp.inf)
            p = jnp.exp(sc - lse_ref[:, 0:1])
            do = do_ref[...]
            delta = jnp.sum(do.astype(F32) * ao_ref[...], axis=-1, keepdims=True)
            dv_acc[...] += _dot_tn(_mx(p), do)
            dp = _dot_nt(do, v_ref[...])
            ds = _mx(p * (dp - delta) * scale)
            dk_acc[...] += _dot_tn(ds, q)
            dq_part = _dot(ds, k)
            rows = pl.ds(pl.multiple_of(qi * t, t), t)

            @pl.when(ki == 0)
            def _():
                dq_ref[rows, :] = dq_part

            @pl.when(ki > 0)
            def _():
                dq_ref[rows, :] += dq_part

        @pl.when(qi == n - 1)
        def _():
            dk_ref[...] = dk_acc[...]
            dv_ref[...] = dv_acc[...]

    q_map = lambda h, ki, qi: (jnp.maximum(qi, ki), h)
    kv_map = lambda h, ki, qi: (ki, h)
    return pl.pallas_call(
        body,
        name="flash_bwd",
        grid=(HEADS, n, n),
        in_specs=[
            pl.BlockSpec((t, HEAD_PAD), q_map),
            pl.BlockSpec((t, HEAD_PAD), kv_map),
            pl.BlockSpec((t, HEAD_DIM), kv_map),
            pl.BlockSpec((t, HEAD_DIM), q_map),
            pl.BlockSpec((t, HEAD_DIM), q_map),
            pl.BlockSpec((t, HEAD_DIM), q_map),
        ],
        out_specs=[
            pl.BlockSpec((s, HEAD_PAD), lambda h, ki, qi: (0, h)),
            pl.BlockSpec((t, HEAD_PAD), kv_map),
            pl.BlockSpec((t, HEAD_DIM), kv_map),
        ],
        out_shape=[
            jax.ShapeDtypeStruct((s, HEADS * HEAD_PAD), F32),
            jax.ShapeDtypeStruct((s, HEADS * HEAD_PAD), F32),
            jax.ShapeDtypeStruct((s, D_MODEL), F32),
        ],
        scratch_shapes=[pltpu.VMEM((t, HEAD_PAD), F32), pltpu.VMEM((t, HEAD_DIM), F32)],
        compiler_params=_params(),
    )(q_all, k_all, v_all, dao, ao, lse)


def _mla_prep_bwd(dq_all, dk_all, dv_all, proj, q_a_g, kv_a_g, w_uq_p, w_kn, w_v, cos, sin_a, sin_b):
    s = proj.shape[0]
    tm = min(s, TM_ROW)

    def body(dq_ref, dk_ref, dv_ref, sm_ref, gq_ref, gk_ref, wq_ref, wkn_ref, wv_ref, cos_ref, sa_ref, sb_ref,
             dsm_ref, dqf_ref, dkn_ref, dvb_ref, dgq_ref, dgk_ref):
        i = pl.program_id(0)

        @pl.when(i == 0)
        def _():
            dgq_ref[...] = jnp.zeros_like(dgq_ref)
            dgk_ref[...] = jnp.zeros_like(dgk_ref)

        cos_t, sa, sb = cos_ref[...], sa_ref[...], sb_ref[...]
        dkpe = jnp.zeros((tm, HEAD_DIM), F32)
        for h in range(HEADS):
            lo = h * HEAD_PAD
            dqf_ref[:, lo:lo + HEAD_DIM] = _mx(dq_ref[:, lo:lo + HEAD_DIM])
            dqf_ref[:, lo + HEAD_DIM:lo + HEAD_PAD] = _mx(_rope_t(dq_ref[:, lo + HEAD_DIM:lo + HEAD_PAD], cos_t, sa, sb))
            dkn_ref[:, h * HEAD_DIM:(h + 1) * HEAD_DIM] = _mx(dk_ref[:, lo:lo + HEAD_DIM])
            dkpe = dkpe + dk_ref[:, lo + HEAD_DIM:lo + HEAD_PAD]
        dkr = _rope_t(dkpe, cos_t, sa, sb)
        dvb = _mx(dv_ref[...])
        dvb_ref[...] = dvb
        dcqn = _dot_nt(dqf_ref[...], wq_ref[...])
        dckvn = _dot_nt(dkn_ref[...], wkn_ref[...]) + _dot_nt(dvb, wv_ref[...])

        small = sm_ref[...]
        cq = small[:, :Q_LORA]
        ckv = small[:, Q_LORA:Q_LORA + KV_LORA]
        rq = lax.rsqrt(jnp.mean(cq * cq, axis=-1, keepdims=True) + EPS)
        rk = lax.rsqrt(jnp.mean(ckv * ckv, axis=-1, keepdims=True) + EPS)
        cqh = cq * rq
        ckh = ckv * rk
        dgq_ref[...] += _bcast_rows(jnp.sum(dcqn * cqh, axis=0, keepdims=True), 8)
        dgk_ref[...] += _bcast_rows(jnp.sum(dckvn * ckh, axis=0, keepdims=True), 8)
        tq = dcqn * gq_ref[...]
        tk = dckvn * gk_ref[...]
        dcq = rq * (tq - cqh * jnp.mean(tq * cqh, axis=-1, keepdims=True))
        dckv = rk * (tk - ckh * jnp.mean(tk * ckh, axis=-1, keepdims=True))
        dsm_ref[:, :Q_LORA] = _mx(dcq)
        dsm_ref[:, Q_LORA:Q_LORA + KV_LORA] = _mx(dckv)
        dsm_ref[:, Q_LORA + KV_LORA:Q_LORA + KV_LORA + HEAD_DIM] = _mx(dkr)
        dsm_ref[:, Q_LORA + KV_LORA + HEAD_DIM:] = jnp.zeros((tm, SEG - Q_LORA - KV_LORA - HEAD_DIM), MXU_DTYPE)

    def const(shape):
        return pl.BlockSpec(shape, lambda i: (0, 0))

    def rows(w):
        return pl.BlockSpec((tm, w), lambda i: (i, 0))

    return pl.pallas_call(
        body,
        name="mla_prep_bwd",
        grid=(s // tm,),
        in_specs=[
            rows(HEADS * HEAD_PAD), rows(HEADS * HEAD_PAD), rows(D_MODEL),
            pl.BlockSpec((tm, SEG), lambda i: (i, SMALL_SEG)),
            const((1, Q_LORA)), const((1, KV_LORA)),
            const((Q_LORA, HEADS * HEAD_PAD)), const((KV_LORA, D_MODEL)), const((KV_LORA, D_MODEL)),
            rows(HEAD_DIM), rows(HEAD_DIM), rows(HEAD_DIM),
        ],
        out_specs=[rows(SEG), rows(HEADS * HEAD_PAD), rows(D_MODEL), rows(D_MODEL),
                   const((8, Q_LORA)), const((8, KV_LORA))],
        out_shape=[
            jax.ShapeDtypeStruct((s, SEG), MXU_DTYPE),
            jax.ShapeDtypeStruct((s, HEADS * HEAD_PAD), MXU_DTYPE),
            jax.ShapeDtypeStruct((s, D_MODEL), MXU_DTYPE),
            jax.ShapeDtypeStruct((s, D_MODEL), MXU_DTYPE),
            jax.ShapeDtypeStruct((8, Q_LORA), F32),
            jax.ShapeDtypeStruct((8, KV_LORA), F32),
        ],
        compiler_params=_params(),
    )(dq_all, dk_all, dv_all, proj, q_a_g, kv_a_g, w_uq_p, w_kn, w_v, cos, sin_a, sin_b)


def _hgrn_bwd(proj, lb_logits, hg_norm_g, o_all, dya, states):
    s = proj.shape[0]
    t = min(s, T_HGRN)
    nb = s // t
    nc = t // HG_CHUNK

    def body(hq_ref, hf_ref, hi_ref, hz_ref, lb_ref, g_ref, o_ref, dya_ref, st_ref,
             dhq_ref, dhf_ref, dhi_ref, dhz_ref, dlb_ref, dg_ref, dstate):
        h, b = pl.program_id(0), pl.program_id(1)

        @pl.when(b == 0)
        def _():
            dstate[...] = jnp.zeros_like(dstate)
            dlb_ref[...] = jnp.zeros_like(dlb_ref)

        @pl.when((b == 0) & (h == 0))
        def _():
            dg_ref[...] = jnp.zeros_like(dg_ref)

        same, lower, upper = _chunk_masks(t)
        same_f = same.astype(F32)
        hq, hf, hz = hq_ref[...], hf_ref[...], hz_ref[...]
        gt = _hgrn_gates(hq, hf, lb_ref[...], same_f, lower.astype(F32))
        v = hi_ref[...]
        vb = _mx(v)
        qi, ki, ko = gt["qi"], gt["ki"], gt["ko"]
        qib, kib, kob = _mx(qi), _mx(ki), _mx(ko)

        o = o_ref[...]
        ghg = g_ref[...]
        sz = _sigmoid(hz)
        r = lax.rsqrt(jnp.mean(o * o, axis=-1, keepdims=True) + EPS)
        on = o * r
        dya_t = dya_ref[...]
        don = dya_t * (hz * sz)
        dhz_ref[...] = _mx(dya_t * (on * ghg) * (sz + hz * sz * (1.0 - sz)))
        dg_ref[...] += _bcast_rows(jnp.sum(don * on, axis=0, keepdims=True), 8)
        tt = don * ghg
        do = r * (tt - on * jnp.mean(tt * on, axis=-1, keepdims=True))
        dob = _mx(do)

        sts = [st_ref[0, 0]]
        for c in range(nc - 1):
            sl = slice(c * HG_CHUNK, (c + 1) * HG_CHUNK)
            sts.append(sts[-1] * gt["dec"][c * HG_CHUNK:c * HG_CHUNK + 1, :] + _dot_tn(vb[sl], kob[sl]))

        a = jnp.where(lower, _dot_nt(qib, kib), 0.0)
        da = _mx(jnp.where(lower, _dot_nt(dob, vb), 0.0))
        dqi_intra = _dot(da, kib)
        dki = _dot_tn(da, qib)
        dv_intra = _dot_tn(_mx(a), dob)

        dst = dstate[...]
        dqi_parts, dko_parts, dv_parts, dd_parts = [None] * nc, [None] * nc, [None] * nc, [None] * nc
        for c in reversed(range(nc)):
            sl = slice(c * HG_CHUNK, (c + 1) * HG_CHUNK)
            dec = gt["dec"][c * HG_CHUNK:c * HG_CHUNK + 1, :]
            dstb = _mx(dst)
            dv_parts[c] = dv_intra[sl] + _dot_nt(kob[sl], dstb)
            dko_parts[c] = _dot(vb[sl], dstb)
            dqi_parts[c] = dqi_intra[sl] + _dot(dob[sl], _mx(sts[c]))
            dd_parts[c] = _bcast_rows(jnp.sum(dst * sts[c], axis=0, keepdims=True) * dec, HG_CHUNK)
            dst = dst * dec + _dot_tn(dob[sl], qib[sl])
        dstate[...] = dst
        dqi = jnp.concatenate(dqi_parts, axis=0)
        dko = jnp.concatenate(dko_parts, axis=0)
        dv = jnp.concatenate(dv_parts, axis=0)
        dd = jnp.concatenate(dd_parts, axis=0)

        dq = dqi * gt["eb"]
        dk = dki * gt["enb"] + dko * gt["eo"]
        db = dqi * qi - dki * ki - dko * ko
        dlogf = _dot_exact(upper.astype(F32), db) + _dot_exact(same_f, dko * ko) + dd
        df = dlogf / gt["f"] - dk
        lb, sig, sq = gt["lb"], gt["sig"], gt["sq"]
        dhf_ref[...] = _mx(df * (1.0 - lb) * (sig * (1.0 - sig)))
        dhq_ref[...] = _mx(dq * (sq + hq * sq * (1.0 - sq)))
        dhi_ref[...] = _mx(dv)
        dlb = jnp.sum(df * (1.0 - sig), axis=0, keepdims=True) * (lb * (1.0 - lb))
        dlb_ref[...] += jnp.concatenate([dlb, -dlb], axis=0)

    def seg(k):
        return pl.BlockSpec((t, HEAD_DIM), lambda h, b, k=k: (nb - 1 - b, k * HEADS + h))

    blk = pl.BlockSpec((t, HEAD_DIM), lambda h, b: (nb - 1 - b, h))
    b16 = jax.ShapeDtypeStruct((s, D_MODEL), MXU_DTYPE)
    return pl.pallas_call(
        body,
        name="hgrn_bwd",
        grid=(HEADS, nb),
        in_specs=[seg(0), seg(1), seg(2), seg(3),
                  pl.BlockSpec((2, HEAD_DIM), lambda h, b: (0, h)),
                  pl.BlockSpec((1, HEAD_DIM), lambda h, b: (0, 0)),
                  blk, blk,
                  pl.BlockSpec((1, 1, HEAD_DIM, HEAD_DIM), lambda h, b: (nb - 1 - b, h, 0, 0))],
        out_specs=[blk, blk, blk, blk,
                   pl.BlockSpec((2, HEAD_DIM), lambda h, b: (0, h)),
                   pl.BlockSpec((8, HEAD_DIM), lambda h, b: (0, 0))],
        out_shape=[b16, b16, b16, b16,
                   jax.ShapeDtypeStruct((2, D_MODEL), F32),
                   jax.ShapeDtypeStruct((8, HEAD_DIM), F32)],
        scratch_shapes=[pltpu.VMEM((HEAD_DIM, HEAD_DIM), F32)],
        compiler_params=_params(),
    )(proj, proj, proj, proj, lb_logits, hg_norm_g, o_all, dya, states)


def _dh_bwd(dproj, w_in_p, x, dx2, norm_g):
    s = x.shape[0]
    tm = min(s, TM_DH)
    nk = PROJ_W // SEG

    def body(dp_ref, w_ref, x_ref, dx2_ref, g_ref, gx_ref, dng_ref, acc):
        i, k = pl.program_id(0), pl.program_id(1)

        @pl.when((i == 0) & (k == 0))
        def _():
            dng_ref[...] = jnp.zeros_like(dng_ref)

        part = _dot_nt(dp_ref[...], w_ref[...])

        @pl.when(k == 0)
        def _():
            acc[...] = part

        @pl.when(k > 0)
        def _():
            acc[...] += part

        @pl.when(k == nk - 1)
        def _():
            dh = acc[...]
            xf = x_ref[...]
            r = lax.rsqrt(jnp.mean(xf * xf, axis=-1, keepdims=True) + EPS)
            xh = xf * r
            dng_ref[...] += _bcast_rows(jnp.sum(dh * xh, axis=0, keepdims=True), 8)
            tt = dh * g_ref[...]
            gx_ref[...] = dx2_ref[...] + r * (tt - xh * jnp.mean(tt * xh, axis=-1, keepdims=True))

    rows = pl.BlockSpec((tm, D_MODEL), lambda i, k: (i, 0))
    return pl.pallas_call(
        body,
        name="dh_bwd",
        grid=(s // tm, nk),
        in_specs=[
            pl.BlockSpec((tm, SEG), lambda i, k: (i, k)),
            pl.BlockSpec((D_MODEL, SEG), lambda i, k: (0, k)),
            rows, rows,
            pl.BlockSpec((1, D_MODEL), lambda i, k: (0, 0)),
        ],
        out_specs=[rows, pl.BlockSpec((8, D_MODEL), lambda i, k: (0, 0))],
        out_shape=[jax.ShapeDtypeStruct((s, D_MODEL), F32), jax.ShapeDtypeStruct((8, D_MODEL), F32)],
        scratch_shapes=[pltpu.VMEM((tm, D_MODEL), F32)],
        compiler_params=_params(),
    )(dproj, w_in_p, x, dx2, norm_g)


def _matmul_tn(a, b, name):
    s, m = a.shape
    n = b.shape[1]
    ts = min(s, TS_TN)
    tn = min(n, SEG)

    def body(a_ref, b_ref, o_ref):
        k = pl.program_id(1)
        part = _dot_tn(a_ref[...], b_ref[...])

        @pl.when(k == 0)
        def _():
            o_ref[...] = part

        @pl.when(k > 0)
        def _():
            o_ref[...] += part

    return pl.pallas_call(
        body,
        name=name,
        grid=(n // tn, s // ts),
        in_specs=[pl.BlockSpec((ts, m), lambda j, k: (k, 0)), pl.BlockSpec((ts, tn), lambda j, k: (k, j))],
        out_specs=pl.BlockSpec((m, tn), lambda j, k: (0, j)),
        out_shape=jax.ShapeDtypeStruct((m, n), F32),
        compiler_params=_params(),
    )(a, b)


def _rope_tables(s):
    inv = ROPE_THETA ** (-jnp.arange(0, QK_ROPE, 2, dtype=F32) / QK_ROPE)
    ang = jnp.arange(s, dtype=F32)[:, None] * inv[None, :]
    cos, sin = jnp.cos(ang), jnp.sin(ang)
    z32 = jnp.zeros_like(cos)
    z64 = jnp.zeros((s, HEAD_DIM - QK_ROPE), F32)
    cos_t = jnp.concatenate([cos, cos, z64], axis=1)
    sin_a = jnp.concatenate([-sin, z32, z64], axis=1)
    sin_b = jnp.concatenate([z32, sin, z64], axis=1)
    return cos_t, sin_a, sin_b


def _pack_big(w_in, w_uq, w_ukv, w_pa, w_pb, w_out):
    return jnp.concatenate([a.reshape(-1, PACK_COLS) for a in (w_in, w_uq, w_ukv, w_pa, w_pb, w_out)], axis=0)


def _pack_small(norm_g, b_gate, lb_logits, hg_norm_g, q_a_g, kv_a_g, final_norm_g, extra):
    misc = jnp.concatenate([hg_norm_g.reshape(-1), q_a_g.reshape(-1), kv_a_g.reshape(-1), extra.reshape(-1),
                            jnp.zeros((PACK_COLS - HEAD_DIM - Q_LORA - KV_LORA - 1,), F32)])
    return jnp.concatenate([norm_g.reshape(1, -1), b_gate.reshape(2, -1), lb_logits.reshape(2, -1),
                            misc.reshape(1, -1), final_norm_g.reshape(1, -1), jnp.zeros((1, PACK_COLS), F32)], axis=0)


def _unpack(p):
    r0 = 0
    big = []
    for rows, shape in ((ROWS_W_IN, (1, D_MODEL, IN_COLS // N_DEV)), (ROWS_W_UQ, (1, Q_LORA, QK_DIM)),
                        (ROWS_W_UKV, (1, KV_LORA, 2 * HEAD_DIM)), (ROWS_W_PROJ, (1, HEAD_DIM, D_MODEL)),
                        (ROWS_W_PROJ, (1, HEAD_DIM, D_MODEL)), (ROWS_W_PROJ, (1, HEAD_DIM, D_MODEL))):
        big.append(p[r0:r0 + rows].reshape(shape))
        r0 += rows
    sm = p[ROWS_BIG:]
    w_in, w_uq, w_ukv, w_pa, w_pb, w_out = big
    misc = sm[5]
    return dict(
        norm_g=sm[0:1], w_in=w_in, b_gate=sm[1:3].reshape(1, -1), lb_logits=sm[3:5],
        hg_norm_g=misc[None, :HEAD_DIM], q_a_g=misc[None, HEAD_DIM:HEAD_DIM + Q_LORA],
        w_uq=w_uq, kv_a_g=misc[None, HEAD_DIM + Q_LORA:HEAD_DIM + Q_LORA + KV_LORA], w_ukv=w_ukv,
        w_proj_a=w_pa, w_proj_b=w_pb, w_out=w_out, final_norm_g=sm[6],
        extra=misc[HEAD_DIM + Q_LORA + KV_LORA],
    )


def _full_weights(gathered):
    r0 = 0
    w_in = gathered[:, r0:r0 + ROWS_W_IN].reshape(N_DEV, D_MODEL, IN_COLS // N_DEV)
    w_in = w_in.transpose(1, 0, 2).reshape(D_MODEL, IN_COLS)
    pad_at = 4 * SEG + Q_LORA + KV_LORA + QK_ROPE
    w_in_p = jnp.concatenate([w_in[:, :pad_at], jnp.zeros((D_MODEL, PROJ_W - IN_COLS), w_in.dtype),
                              w_in[:, pad_at:]], axis=1)
    r0 += ROWS_W_IN
    w_uq = gathered[:, r0:r0 + ROWS_W_UQ].reshape(N_DEV, Q_LORA, QK_DIM).transpose(1, 0, 2)
    w_uq_p = jnp.concatenate([w_uq, jnp.zeros((Q_LORA, HEADS, HEAD_PAD - QK_DIM), w_uq.dtype)], axis=2)
    w_uq_p = w_uq_p.reshape(Q_LORA, HEADS * HEAD_PAD)
    r0 += ROWS_W_UQ
    w_ukv = gathered[:, r0:r0 + ROWS_W_UKV].reshape(N_DEV, KV_LORA, 2 * HEAD_DIM).transpose(1, 0, 2)
    w_kn = w_ukv[:, :, :HEAD_DIM].reshape(KV_LORA, D_MODEL)
    w_v = w_ukv[:, :, HEAD_DIM:].reshape(KV_LORA, D_MODEL)
    r0 += ROWS_W_UKV
    mats = []
    for _ in range(3):
        mats.append(gathered[:, r0:r0 + ROWS_W_PROJ].reshape(D_MODEL, D_MODEL))
        r0 += ROWS_W_PROJ
    return w_in_p, w_uq_p, w_kn, w_v, mats[0], mats[1], mats[2]


def _grad_slabs(dw_in_p, dw_uq_p, dw_kn, dw_v, dw_pa, dw_pb, dw_out, small):
    pad_at = 4 * SEG + Q_LORA + KV_LORA + QK_ROPE
    dw_in = jnp.concatenate([dw_in_p[:, :pad_at], dw_in_p[:, pad_at + PROJ_W - IN_COLS:]], axis=1)
    dw_in = dw_in.reshape(D_MODEL, N_DEV, IN_COLS // N_DEV).transpose(1, 0, 2).reshape(N_DEV, ROWS_W_IN, PACK_COLS)
    dw_uq = dw_uq_p.reshape(Q_LORA, HEADS, HEAD_PAD)[:, :, :QK_DIM].transpose(1, 0, 2).reshape(N_DEV, ROWS_W_UQ, PACK_COLS)
    dw_ukv = jnp.concatenate([dw_kn.reshape(KV_LORA, HEADS, HEAD_DIM), dw_v.reshape(KV_LORA, HEADS, HEAD_DIM)], axis=2)
    dw_ukv = dw_ukv.transpose(1, 0, 2).reshape(N_DEV, ROWS_W_UKV, PACK_COLS)
    mats = [a.reshape(N_DEV, ROWS_W_PROJ, PACK_COLS) for a in (dw_pa, dw_pb, dw_out)]
    sm = jnp.broadcast_to(small[None], (N_DEV, ROWS_SMALL, PACK_COLS))
    return jnp.concatenate([dw_in, dw_uq, dw_ukv] + mats + [sm], axis=1)


def _local_grads(x, target, norm_g, b_gate, lb_logits, hg_norm_g, q_a_g, kv_a_g, final_g,
                 w_in_p, w_uq_p, w_kn, w_v, w_pa, w_pb, w_out):
    s = x.shape[0]
    cos, sin_a, sin_b = _rope_tables(s)
    proj, h = _inproj(x, norm_g, w_in_p)
    o_all, ya, states = _hgrn_fwd(proj, lb_logits, hg_norm_g)
    q_all, k_all, v_all, cqn, ckvn = _mla_prep(proj, q_a_g, kv_a_g, w_uq_p, w_kn, w_v, cos, sin_a, sin_b)
    ao, lse, yb = _flash_fwd(q_all, k_all, v_all, proj)
    (dx2, dya, dyb, dg0, dg1, merged_b, dpa_b, dpb_b, dx2_b,
     loss_acc, dfg_acc, dbg_acc) = _merge_fwd_bwd(x, target, ya, yb, proj, b_gate, final_g, w_pa, w_pb, w_out)
    dao, dmz = _attn_gate_bwd(dyb, ao, proj)
    dq_all, dk_all, dv_all = _flash_bwd(q_all, k_all, v_all, dao, ao, lse)
    dsmall, dqf_b, dkn_b, dv_b, dgq_acc, dgk_acc = _mla_prep_bwd(
        dq_all, dk_all, dv_all, proj, q_a_g, kv_a_g, w_uq_p, w_kn, w_v, cos, sin_a, sin_b)
    dhq, dhf, dhi, dhz, dlb, dhg_acc = _hgrn_bwd(proj, lb_logits, hg_norm_g, o_all, dya, states)
    segs = [dhq, dhf, dhi, dhz, dsmall, dmz, dg0, dg1]
    dproj = jnp.concatenate(segs, axis=1)
    grad_x, dng_acc = _dh_bwd(dproj, w_in_p, x, dx2, norm_g)
    dw_in_p = jnp.concatenate([_matmul_tn(h, sg, "dw_in_%d" % k) for k, sg in enumerate(segs)], axis=1)
    return dict(
        loss=loss_acc[0, 0], grad_x=grad_x,
        norm_g=dng_acc[0:1], b_gate=dbg_acc[0:1], lb_logits=dlb, hg_norm_g=dhg_acc[0:1],
        q_a_g=dgq_acc[0:1], kv_a_g=dgk_acc[0:1], final_norm_g=dfg_acc[0],
        w_in_p=dw_in_p,
        w_uq_p=_matmul_tn(cqn, dqf_b, "dw_uq"),
        w_kn=_matmul_tn(ckvn, dkn_b, "dw_kn"),
        w_v=_matmul_tn(ckvn, dv_b, "dw_v"),
        w_pa=_matmul_tn(ya, dpa_b, "dw_pa"),
        w_pb=_matmul_tn(yb, dpb_b, "dw_pb"),
        w_out=_matmul_tn(merged_b, dx2_b, "dw_out"),
    )


def kernel(x, norm_g, w_in, b_gate, lb_logits, hg_norm_g, q_a_g, w_uq, kv_a_g, w_ukv, w_proj_a, w_proj_b, w_out, final_norm_g, loss_target, m_norm_g, m_w_in, m_b_gate, m_lb_logits, m_hg_norm_g, m_q_a_g, m_w_uq, m_kv_a_g, m_w_ukv, m_w_proj_a, m_w_proj_b, m_w_out, m_final_norm_g, v_norm_g, v_w_in, v_b_gate, v_lb_logits, v_hg_norm_g, v_q_a_g, v_w_uq, v_kv_a_g, v_w_ukv, v_w_proj_a, v_w_proj_b, v_w_out, v_final_norm_g):
    zero = jnp.zeros((1,), F32)
    shard_b16 = _pack_big(w_in, w_uq, w_ukv, w_proj_a, w_proj_b, w_out).astype(MXU_DTYPE)
    full = _full_weights(_all_gather_packed(shard_b16))
    g = _local_grads(x[0], loss_target[0], norm_g, b_gate, lb_logits, hg_norm_g, q_a_g, kv_a_g,
                     final_norm_g.reshape(1, -1), *full)
    small = _pack_small(g["norm_g"], g["b_gate"], g["lb_logits"], g["hg_norm_g"], g["q_a_g"], g["kv_a_g"],
                        g["final_norm_g"], g["loss"])
    slabs = _grad_slabs(g["w_in_p"], g["w_uq_p"], g["w_kn"], g["w_v"], g["w_pa"], g["w_pb"], g["w_out"], small)
    recv = _reduce_scatter_exchange(slabs)

    def packed(w6, s7):
        return jnp.concatenate([_pack_big(*w6), _pack_small(*s7, zero)], axis=0)

    w_p = packed((w_in, w_uq, w_ukv, w_proj_a, w_proj_b, w_out),
                 (norm_g, b_gate, lb_logits, hg_norm_g, q_a_g, kv_a_g, final_norm_g))
    m_p = packed((m_w_in, m_w_uq, m_w_ukv, m_w_proj_a, m_w_proj_b, m_w_out),
                 (m_norm_g, m_b_gate, m_lb_logits, m_hg_norm_g, m_q_a_g, m_kv_a_g, m_final_norm_g))
    v_p = packed((v_w_in, v_w_uq, v_w_ukv, v_w_proj_a, v_w_proj_b, v_w_out),
                 (v_norm_g, v_b_gate, v_lb_logits, v_hg_norm_g, v_q_a_g, v_kv_a_g, v_final_norm_g))
    g_p, d_p, mn_p, vn_p = _sum_adamw(recv, w_p, m_p, v_p)
    names = ["norm_g", "w_in", "b_gate", "lb_logits", "hg_norm_g", "q_a_g", "w_uq", "kv_a_g", "w_ukv",
             "w_proj_a", "w_proj_b", "w_out", "final_norm_g"]
    grads, deltas, new_m, new_v = _unpack(g_p), _unpack(d_p), _unpack(mn_p), _unpack(vn_p)
    return (grads["extra"], g["grad_x"][None],
            *[grads[n] for n in names], *[deltas[n] for n in names],
            *[new_m[n] for n in names], *[new_v[n] for n in names])
```

```python
import functools

import jax
import jax.numpy as jnp
from jax import lax
from jax.experimental import pallas as pl
from jax.experimental.pallas import tpu as pltpu

D_MODEL = 1024
HEADS = 8
HEAD_DIM = 128
HG_CHUNK = 32
Q_LORA = 384
KV_LORA = 256
QK_ROPE = 64
QK_DIM = 192
ROPE_THETA = 10000.0
EPS = 1e-6
IN_COLS = 7872
ADAM_LR = 0.001
ADAM_B1 = 0.9
ADAM_B2 = 0.999
ADAM_EPS = 1e-08
ADAM_WD = 0.01
ADAM_STEP = 10

N_DEV = 8
SEG = 1024
PROJ_W = 8 * SEG
SMALL_SEG = 4
MZ_SEG = 5
GL_SEG = 6
HEAD_PAD = 256
PACK_COLS = 1024
ROWS_W_IN = 984
ROWS_W_UQ = 72
ROWS_W_UKV = 64
ROWS_W_PROJ = 128
ROWS_BIG = ROWS_W_IN + ROWS_W_UQ + ROWS_W_UKV + 3 * ROWS_W_PROJ
ROWS_SMALL = 8
ROWS_PACK = ROWS_BIG + ROWS_SMALL

QK_SCALE = QK_DIM ** -0.5
LOG2E = 1.4426950408889634
LN2 = 0.6931471805599453
Q_PRESCALE = QK_SCALE * LOG2E

MXU_DTYPE = jnp.bfloat16
VMEM_LIMIT = 48 * 1024 * 1024
VMEM_LIMIT_BIG = 60 * 1024 * 1024

TM_INPROJ = 512
T_HGRN = 256
TM_ROW = 256
T_ATT = 1024
T_ATT_BWD = 1024
TS_TN = 512
TM_DH = 512
TR_ADAM = 216

F32 = jnp.float32
MESH = pl.DeviceIdType.MESH


def _dot(a, b):
    return jnp.dot(a, b, preferred_element_type=F32)


def _dot_nt(a, b):
    return lax.dot_general(a, b, (((1,), (1,)), ((), ())), preferred_element_type=F32)


def _dot_tn(a, b):
    return lax.dot_general(a, b, (((0,), (0,)), ((), ())), preferred_element_type=F32)


def _dot_exact(a, b):
    return jnp.dot(a, b, preferred_element_type=F32, precision=lax.Precision.HIGHEST)


def _mx(a):
    return a.astype(MXU_DTYPE)


def _sigmoid(x):
    return 1.0 / (1.0 + jnp.exp(-x))


def _params(vmem=VMEM_LIMIT, **kw):
    return pltpu.CompilerParams(vmem_limit_bytes=vmem, **kw)


def _bcast_rows(row, n):
    return jnp.broadcast_to(row, (n, row.shape[-1]))


HBM_SPEC = pl.BlockSpec(memory_space=pltpu.HBM)


def _all_gather_packed(shard):
    rows, cols = shard.shape

    def body(x_ref, out_ref, send_sems, recv_sems, local_sem):
        x, y, c = lax.axis_index("x"), lax.axis_index("y"), lax.axis_index("c")
        me, sibling = (x, y, c), (x, y, 1 - c)
        chips = [(1 - x, y), (x, 1 - y), (1 - x, 1 - y)]

        def slot(px, py, pc):
            return out_ref.at[4 * px + 2 * py + pc]

        def copy(k, block, to, src=None):
            return pltpu.make_async_remote_copy(
                src_ref=slot(*block) if src is None else src,
                dst_ref=slot(*block),
                send_sem=send_sems.at[k],
                recv_sem=recv_sems.at[k],
                device_id=to,
                device_id_type=MESH,
            )

        mine = pltpu.make_async_copy(x_ref, slot(*me), local_sem)
        mine.start()
        first = [copy(0, me, sibling, src=x_ref)]
        first += [copy(1 + j, me, (*chip, c), src=x_ref) for j, chip in enumerate(chips)]
        for cp in first:
            cp.start()
        passed = [copy(4 + j, (*chip, c), sibling) for j, chip in enumerate(chips)]
        for j, chip in enumerate(chips):
            copy(1 + j, (*chip, c), me).wait_recv()
            passed[j].start()
        copy(0, sibling, me).wait_recv()
        for j, chip in enumerate(chips):
            copy(4 + j, (*chip, 1 - c), me).wait_recv()
        for cp in first + passed:
            cp.wait_send()
        mine.wait()

    return pl.pallas_call(
        body,
        name="ag_weights",
        out_shape=jax.ShapeDtypeStruct((N_DEV, rows, cols), shard.dtype),
        in_specs=[HBM_SPEC],
        out_specs=HBM_SPEC,
        scratch_shapes=[
            pltpu.SemaphoreType.DMA((7,)),
            pltpu.SemaphoreType.DMA((7,)),
            pltpu.SemaphoreType.DMA,
        ],
    )(shard)


def _reduce_scatter_exchange(slabs):
    _, rows, cols = slabs.shape

    def body(g_ref, recv_ref, send_sems, recv_sems, local_sem):
        x, y, c = lax.axis_index("x"), lax.axis_index("y"), lax.axis_index("c")
        me = 4 * x + 2 * y + c

        def copy(k):
            px, py, pc = x ^ ((k >> 2) & 1), y ^ ((k >> 1) & 1), c ^ (k & 1)
            return pltpu.make_async_remote_copy(
                src_ref=g_ref.at[4 * px + 2 * py + pc],
                dst_ref=recv_ref.at[me],
                send_sem=send_sems.at[k - 1],
                recv_sem=recv_sems.at[k - 1],
                device_id=(px, py, pc),
                device_id_type=MESH,
            )

        def landing(k):
            px, py, pc = x ^ ((k >> 2) & 1), y ^ ((k >> 1) & 1), c ^ (k & 1)
            peer = 4 * px + 2 * py + pc
            return pltpu.make_async_remote_copy(
                src_ref=g_ref.at[peer],
                dst_ref=recv_ref.at[peer],
                send_sem=send_sems.at[k - 1],
                recv_sem=recv_sems.at[k - 1],
                device_id=(px, py, pc),
                device_id_type=MESH,
            )

        mine = pltpu.make_async_copy(g_ref.at[me], recv_ref.at[me], local_sem)
        mine.start()
        sends = [copy(k) for k in range(1, N_DEV)]
        for cp in sends:
            cp.start()
        for k in range(1, N_DEV):
            landing(k).wait_recv()
        for cp in sends:
            cp.wait_send()
        mine.wait()

    return pl.pallas_call(
        body,
        name="rs_grads",
        out_shape=jax.ShapeDtypeStruct(slabs.shape, slabs.dtype),
        in_specs=[HBM_SPEC],
        out_specs=HBM_SPEC,
        scratch_shapes=[
            pltpu.SemaphoreType.DMA((7,)),
            pltpu.SemaphoreType.DMA((7,)),
            pltpu.SemaphoreType.DMA,
        ],
    )(slabs)


def _sum_adamw(recv, w, m, v):
    rows, cols = w.shape
    tr = TR_ADAM if rows % TR_ADAM == 0 else rows

    def body(r_ref, w_ref, m_ref, v_ref, g_out, d_out, m_out, v_out):
        g = r_ref[0]
        for i in range(1, N_DEV):
            g = g + r_ref[i]
        m_new = ADAM_B1 * m_ref[...] + (1.0 - ADAM_B1) * g
        v_new = ADAM_B2 * v_ref[...] + (1.0 - ADAM_B2) * (g * g)
        m_hat = m_new / (1.0 - ADAM_B1 ** ADAM_STEP)
        v_hat = v_new / (1.0 - ADAM_B2 ** ADAM_STEP)
        g_out[...] = g
        d_out[...] = -ADAM_LR * (m_hat / (jnp.sqrt(v_hat) + ADAM_EPS) + ADAM_WD * w_ref[...])
        m_out[...] = m_new
        v_out[...] = v_new

    row_spec = pl.BlockSpec((tr, cols), lambda i: (i, 0))
    shape = jax.ShapeDtypeStruct((rows, cols), F32)
    return pl.pallas_call(
        body,
        name="sum_adamw",
        grid=(rows // tr,),
        in_specs=[pl.BlockSpec((N_DEV, tr, cols), lambda i: (0, i, 0)), row_spec, row_spec, row_spec],
        out_specs=[row_spec] * 4,
        out_shape=[shape] * 4,
        compiler_params=_params(),
    )(recv, w, m, v)


def _inproj(x, norm_g, w_in_p):
    s = x.shape[0]
    tm = min(s, TM_INPROJ)

    def body(x_ref, g_ref, w_ref, proj_ref, h_ref):
        xf = x_ref[...]
        r = lax.rsqrt(jnp.mean(xf * xf, axis=-1, keepdims=True) + EPS)
        h = _mx(xf * r * g_ref[...])
        h_ref[...] = h
        proj_ref[...] = _dot(h, w_ref[...])

    return pl.pallas_call(
        body,
        name="inproj",
        grid=(s // tm, PROJ_W // SEG),
        in_specs=[
            pl.BlockSpec((tm, D_MODEL), lambda i, j: (i, 0)),
            pl.BlockSpec((1, D_MODEL), lambda i, j: (0, 0)),
            pl.BlockSpec((D_MODEL, SEG), lambda i, j: (0, j)),
        ],
        out_specs=[
            pl.BlockSpec((tm, SEG), lambda i, j: (i, j)),
            pl.BlockSpec((tm, D_MODEL), lambda i, j: (i, 0)),
        ],
        out_shape=[
            jax.ShapeDtypeStruct((s, PROJ_W), F32),
            jax.ShapeDtypeStruct((s, D_MODEL), MXU_DTYPE),
        ],
        compiler_params=_params(),
    )(x, norm_g, w_in_p)


def _chunk_masks(t):
    row = lax.broadcasted_iota(jnp.int32, (t, t), 0)
    col = lax.broadcasted_iota(jnp.int32, (t, t), 1)
    same = (row // HG_CHUNK) == (col // HG_CHUNK)
    lower = same & (col <= row)
    upper = same & (col >= row)
    return same, lower, upper


def _hgrn_gates(hq, hf, lb_logits, same_f, lower_f):
    lb = _sigmoid(lb_logits[0:1, :] - lb_logits[1:2, :])
    sig = _sigmoid(hf)
    f = lb + (1.0 - lb) * sig
    sq = _sigmoid(hq)
    q = hq * sq
    k = 1.0 - f
    logf = jnp.log(f)
    bcum = _dot_exact(lower_f, logf)
    blast = _dot_exact(same_f, logf)
    eb = jnp.exp(bcum)
    enb = jnp.exp(-bcum)
    eo = jnp.exp(blast - bcum)
    return dict(lb=lb, sig=sig, f=f, sq=sq, q=q, k=k, eb=eb, enb=enb, eo=eo,
                qi=q * eb, ki=k * enb, ko=k * eo, dec=jnp.exp(blast))


def _hgrn_fwd(proj, lb_logits, hg_norm_g):
    s = proj.shape[0]
    t = min(s, T_HGRN)
    nb = s // t
    nc = t // HG_CHUNK

    def body(hq_ref, hf_ref, hi_ref, hz_ref, lb_ref, g_ref, o_ref, ya_ref, st_ref, state):
        b = pl.program_id(1)

        @pl.when(b == 0)
        def _():
            state[...] = jnp.zeros_like(state)

        st = state[...]
        st_ref[0, 0] = st
        same, lower, _ = _chunk_masks(t)
        gt = _hgrn_gates(hq_ref[...], hf_ref[...], lb_ref[...], same.astype(F32), lower.astype(F32))
        v = hi_ref[...]
        vb = _mx(v)
        qib, kib, kob = _mx(gt["qi"]), _mx(gt["ki"]), _mx(gt["ko"])
        a = jnp.where(lower, _dot_nt(qib, kib), 0.0)
        o_intra = _dot(_mx(a), vb)
        outs = []
        for c in range(nc):
            sl = slice(c * HG_CHUNK, (c + 1) * HG_CHUNK)
            outs.append(o_intra[sl] + _dot_nt(qib[sl], _mx(st)))
            st = st * gt["dec"][c * HG_CHUNK:c * HG_CHUNK + 1, :] + _dot_tn(vb[sl], kob[sl])
        state[...] = st
        o = jnp.concatenate(outs, axis=0)
        o_ref[...] = o
        r = lax.rsqrt(jnp.mean(o * o, axis=-1, keepdims=True) + EPS)
        hz = hz_ref[...]
        ya_ref[...] = _mx((o * r * g_ref[...]) * (hz * _sigmoid(hz)))

    def seg(k):
        return pl.BlockSpec((t, HEAD_DIM), lambda h, b, k=k: (b, k * HEADS + h))

    return pl.pallas_call(
        body,
        name="hgrn_fwd",
        grid=(HEADS, nb),
        in_specs=[seg(0), seg(1), seg(2), seg(3),
                  pl.BlockSpec((2, HEAD_DIM), lambda h, b: (0, h)),
                  pl.BlockSpec((1, HEAD_DIM), lambda h, b: (0, 0))],
        out_specs=[
            pl.BlockSpec((t, HEAD_DIM), lambda h, b: (b, h)),
            pl.BlockSpec((t, HEAD_DIM), lambda h, b: (b, h)),
            pl.BlockSpec((1, 1, HEAD_DIM, HEAD_DIM), lambda h, b: (b, h, 0, 0)),
        ],
        out_shape=[
            jax.ShapeDtypeStruct((s, D_MODEL), F32),
            jax.ShapeDtypeStruct((s, D_MODEL), MXU_DTYPE),
            jax.ShapeDtypeStruct((nb, HEADS, HEAD_DIM, HEAD_DIM), F32),
        ],
        scratch_shapes=[pltpu.VMEM((HEAD_DIM, HEAD_DIM), F32)],
        compiler_params=_params(),
    )(proj, proj, proj, proj, lb_logits, hg_norm_g)


def _rope(x, cos, sin_a, sin_b):
    return x * cos + pltpu.roll(x, 96, 1) * sin_a + pltpu.roll(x, 32, 1) * sin_b


def _rope_t(d, cos, sin_a, sin_b):
    return d * cos + pltpu.roll(d * sin_a, 32, 1) + pltpu.roll(d * sin_b, 96, 1)


def _mla_prep(proj, q_a_g, kv_a_g, w_uq_p, w_kn, w_v, cos, sin_a, sin_b):
    s = proj.shape[0]
    tm = min(s, TM_ROW)

    def body(sm_ref, gq_ref, gk_ref, wq_ref, wkn_ref, wv_ref, cos_ref, sa_ref, sb_ref,
             q_ref, k_ref, v_ref, cqn_ref, ckvn_ref):
        small = sm_ref[...]
        cq = small[:, :Q_LORA]
        ckv = small[:, Q_LORA:Q_LORA + KV_LORA]
        krp = small[:, Q_LORA + KV_LORA:Q_LORA + KV_LORA + HEAD_DIM]
        rq = lax.rsqrt(jnp.mean(cq * cq, axis=-1, keepdims=True) + EPS)
        rk = lax.rsqrt(jnp.mean(ckv * ckv, axis=-1, keepdims=True) + EPS)
        cqn = _mx(cq * rq * gq_ref[...])
        ckvn = _mx(ckv * rk * gk_ref[...])
        cqn_ref[...] = cqn
        ckvn_ref[...] = ckvn
        q = _dot(cqn, wq_ref[...]) * Q_PRESCALE
        kn = _dot(ckvn, wkn_ref[...])
        v_ref[...] = _mx(_dot(ckvn, wv_ref[...]))
        cos_t, sa, sb = cos_ref[...], sa_ref[...], sb_ref[...]
        kpe = _mx(_rope(krp, cos_t, sa, sb))
        for h in range(HEADS):
            lo = h * HEAD_PAD
            q_ref[:, lo:lo + HEAD_DIM] = _mx(q[:, lo:lo + HEAD_DIM])
            q_ref[:, lo + HEAD_DIM:lo + HEAD_PAD] = _mx(_rope(q[:, lo + HEAD_DIM:lo + HEAD_PAD], cos_t, sa, sb))
            k_ref[:, lo:lo + HEAD_DIM] = _mx(kn[:, h * HEAD_DIM:(h + 1) * HEAD_DIM])
            k_ref[:, lo + HEAD_DIM:lo + HEAD_PAD] = kpe

    def const(shape):
        return pl.BlockSpec(shape, lambda i: (0, 0))

    def rows(w):
        return pl.BlockSpec((tm, w), lambda i: (i, 0))

    return pl.pallas_call(
        body,
        name="mla_prep",
        grid=(s // tm,),
        in_specs=[
            pl.BlockSpec((tm, SEG), lambda i: (i, SMALL_SEG)),
            const((1, Q_LORA)), const((1, KV_LORA)),
            const((Q_LORA, HEADS * HEAD_PAD)), const((KV_LORA, D_MODEL)), const((KV_LORA, D_MODEL)),
            rows(HEAD_DIM), rows(HEAD_DIM), rows(HEAD_DIM),
        ],
        out_specs=[rows(HEADS * HEAD_PAD), rows(HEADS * HEAD_PAD), rows(D_MODEL), rows(Q_LORA), rows(KV_LORA)],
        out_shape=[
            jax.ShapeDtypeStruct((s, HEADS * HEAD_PAD), MXU_DTYPE),
            jax.ShapeDtypeStruct((s, HEADS * HEAD_PAD), MXU_DTYPE),
            jax.ShapeDtypeStruct((s, D_MODEL), MXU_DTYPE),
            jax.ShapeDtypeStruct((s, Q_LORA), MXU_DTYPE),
            jax.ShapeDtypeStruct((s, KV_LORA), MXU_DTYPE),
        ],
        compiler_params=_params(),
    )(proj, q_a_g, kv_a_g, w_uq_p, w_kn, w_v, cos, sin_a, sin_b)


def _diag_mask(t):
    row = lax.broadcasted_iota(jnp.int32, (t, t), 0)
    col = lax.broadcasted_iota(jnp.int32, (t, t), 1)
    return row >= col


def _flash_fwd(q_all, k_all, v_all, proj):
    s = q_all.shape[0]
    t = min(s, T_ATT)
    n = s // t

    def body(q_ref, k_ref, v_ref, mz_ref, ao_ref, lse_ref, yb_ref, m_sc, l_sc, acc_sc):
        qi = pl.program_id(1)
        q = q_ref[...]
        m_sc[...] = jnp.full_like(m_sc, -jnp.inf)
        l_sc[...] = jnp.zeros_like(l_sc)
        acc_sc[...] = jnp.zeros_like(acc_sc)

        def update(sc, v_blk):
            m_prev = m_sc[...]
            m_new = jnp.maximum(m_prev, jnp.max(sc, axis=-1, keepdims=True))
            p = jnp.exp2(sc - m_new)
            alpha = jnp.exp2(m_prev - m_new)
            l_sc[...] = alpha * l_sc[...] + jnp.sum(p, axis=-1, keepdims=True)
            acc_sc[...] = alpha * acc_sc[...] + _dot(_mx(p), v_blk)
            m_sc[...] = m_new

        def below_diagonal(ki, carry):
            rows = pl.ds(pl.multiple_of(ki * t, t), t)
            update(_dot_nt(q, k_ref[rows, :]), v_ref[rows, :])
            return carry

        lax.fori_loop(0, qi, below_diagonal, 0)
        rows = pl.ds(pl.multiple_of(qi * t, t), t)
        update(jnp.where(_diag_mask(t), _dot_nt(q, k_ref[rows, :]), -jnp.inf), v_ref[rows, :])

        ao = acc_sc[...] / l_sc[...]
        ao_ref[...] = ao
        lse_ref[...] = jnp.broadcast_to(m_sc[...] + jnp.log2(l_sc[...]), (t, HEAD_DIM))
        mz = mz_ref[...]
        yb_ref[...] = _mx(ao * (mz * _sigmoid(mz)))

    q_map = lambda h, qi: (qi, h)
    return pl.pallas_call(
        body,
        name="flash_fwd",
        grid=(HEADS, n),
        in_specs=[
            pl.BlockSpec((t, HEAD_PAD), q_map),
            pl.BlockSpec((s, HEAD_PAD), lambda h, qi: (0, h)),
            pl.BlockSpec((s, HEAD_DIM), lambda h, qi: (0, h)),
            pl.BlockSpec((t, HEAD_DIM), lambda h, qi: (qi, MZ_SEG * HEADS + h)),
        ],
        out_specs=[pl.BlockSpec((t, HEAD_DIM), q_map)] * 3,
        out_shape=[
            jax.ShapeDtypeStruct((s, D_MODEL), F32),
            jax.ShapeDtypeStruct((s, D_MODEL), F32),
            jax.ShapeDtypeStruct((s, D_MODEL), MXU_DTYPE),
        ],
        scratch_shapes=[
            pltpu.VMEM((t, 1), F32),
            pltpu.VMEM((t, 1), F32),
            pltpu.VMEM((t, HEAD_DIM), F32),
        ],
        compiler_params=_params(),
    )(q_all, k_all, v_all, proj)


def _merge_fwd_bwd(x, target, ya, yb, proj, b_gate, final_g, w_pa, w_pb, w_out):
    s = x.shape[0]
    tm = min(s, TM_ROW)

    def body(x_ref, t_ref, ya_ref, yb_ref, g0_ref, g1_ref, bg_ref, fg_ref, wpa_ref, wpb_ref, wo_ref,
             dx2_ref, dya_ref, dyb_ref, dg0_ref, dg1_ref, mb_ref, dpab_ref, dpbb_ref, dx2b_ref,
             loss_ref, dfg_ref, dbg_ref):
        i = pl.program_id(0)

        @pl.when(i == 0)
        def _():
            loss_ref[...] = jnp.zeros_like(loss_ref)
            dfg_ref[...] = jnp.zeros_like(dfg_ref)
            dbg_ref[...] = jnp.zeros_like(dbg_ref)

        pa = _dot(ya_ref[...], wpa_ref[...])
        pb = _dot(yb_ref[...], wpb_ref[...])
        bg = bg_ref[...]
        g0 = _sigmoid(g0_ref[...] + bg[:, :D_MODEL])
        g1 = _sigmoid(g1_ref[...] + bg[:, D_MODEL:])
        merged = g0 * pa + g1 * pb
        mb = _mx(merged)
        mb_ref[...] = mb
        x2 = x_ref[...] + _dot(mb, wo_ref[...])
        r = lax.rsqrt(jnp.mean(x2 * x2, axis=-1, keepdims=True) + EPS)
        xn = x2 * r
        fg = fg_ref[...]
        diff = xn * fg - t_ref[...]
        loss_ref[...] += 0.5 * jnp.sum(jnp.mean(diff * diff, axis=-1, keepdims=True))
        dy = diff * (1.0 / D_MODEL)
        dfg_ref[...] += _bcast_rows(jnp.sum(dy * xn, axis=0, keepdims=True), 8)
        tt = dy * fg
        dx2 = r * (tt - xn * jnp.mean(tt * xn, axis=-1, keepdims=True))
        dx2_ref[...] = dx2
        dx2b = _mx(dx2)
        dx2b_ref[...] = dx2b
        dmerged = _dot_nt(dx2b, wo_ref[...])
        dpa = _mx(dmerged * g0)
        dpb = _mx(dmerged * g1)
        dpab_ref[...] = dpa
        dpbb_ref[...] = dpb
        dg0 = dmerged * pa * (g0 * (1.0 - g0))
        dg1 = dmerged * pb * (g1 * (1.0 - g1))
        dg0_ref[...] = _mx(dg0)
        dg1_ref[...] = _mx(dg1)
        dbg_ref[:, :D_MODEL] += _bcast_rows(jnp.sum(dg0, axis=0, keepdims=True), 8)
        dbg_ref[:, D_MODEL:] += _bcast_rows(jnp.sum(dg1, axis=0, keepdims=True), 8)
        dya_ref[...] = _dot_nt(dpa, wpa_ref[...])
        dyb_ref[...] = _dot_nt(dpb, wpb_ref[...])

    def rows(w=D_MODEL):
        return pl.BlockSpec((tm, w), lambda i: (i, 0))

    def const(shape):
        return pl.BlockSpec(shape, lambda i: (0, 0))

    f32 = jax.ShapeDtypeStruct((s, D_MODEL), F32)
    b16 = jax.ShapeDtypeStruct((s, D_MODEL), MXU_DTYPE)
    return pl.pallas_call(
        body,
        name="merge_fwd_bwd",
        grid=(s // tm,),
        in_specs=[
            rows(), rows(), rows(), rows(),
            pl.BlockSpec((tm, SEG), lambda i: (i, GL_SEG)),
            pl.BlockSpec((tm, SEG), lambda i: (i, GL_SEG + 1)),
            const((1, 2 * D_MODEL)), const((1, D_MODEL)),
            const((D_MODEL, D_MODEL)), const((D_MODEL, D_MODEL)), const((D_MODEL, D_MODEL)),
        ],
        out_specs=[rows()] * 9 + [const((8, HEAD_DIM)), const((8, D_MODEL)), const((8, 2 * D_MODEL))],
        out_shape=[f32, f32, f32, b16, b16, b16, b16, b16, b16,
                   jax.ShapeDtypeStruct((8, HEAD_DIM), F32),
                   jax.ShapeDtypeStruct((8, D_MODEL), F32),
                   jax.ShapeDtypeStruct((8, 2 * D_MODEL), F32)],
        compiler_params=_params(),
    )(x, target, ya, yb, proj, proj, b_gate, final_g, w_pa, w_pb, w_out)


def _attn_gate_bwd(dyb, ao, proj):
    s = dyb.shape[0]
    tm = min(s, TM_ROW)

    def body(dyb_ref, ao_ref, mz_ref, dao_ref, dmz_ref):
        mz = mz_ref[...]
        sg = _sigmoid(mz)
        d = dyb_ref[...]
        dao_ref[...] = _mx(d * (mz * sg))
        dmz_ref[...] = _mx(d * ao_ref[...] * (sg + mz * sg * (1.0 - sg)))

    rows = pl.BlockSpec((tm, D_MODEL), lambda i: (i, 0))
    b16 = jax.ShapeDtypeStruct((s, D_MODEL), MXU_DTYPE)
    return pl.pallas_call(
        body,
        name="attn_gate_bwd",
        grid=(s // tm,),
        in_specs=[rows, rows, pl.BlockSpec((tm, SEG), lambda i: (i, MZ_SEG))],
        out_specs=[rows, rows],
        out_shape=[b16, b16],
        compiler_params=_params(),
    )(dyb, ao, proj)


def _flash_bwd(q_all, k_all, v_all, dao, ao, lse):
    s = q_all.shape[0]
    t = min(s, T_ATT_BWD)
    n = s // t
    pairs = [(ki, qi) for ki in range(n) for qi in range(ki, n)]
    ki_list = jnp.asarray([p[0] for p in pairs], jnp.int32)
    qi_list = jnp.asarray([p[1] for p in pairs], jnp.int32)

    def body(ki_ref, qi_ref, q_ref, k_ref, v_ref, do_ref, ao_ref, lse_ref, dq_ref, dk_ref, dv_ref, dk_acc, dv_acc):
        step = pl.program_id(1)
        ki, qi = ki_ref[step], qi_ref[step]

        @pl.when(qi == ki)
        def _():
            dk_acc[...] = jnp.zeros_like(dk_acc)
            dv_acc[...] = jnp.zeros_like(dv_acc)

        def pair(masked):
            q, k = q_ref[...], k_ref[...]
            sc = _dot_nt(q, k)
            if masked:
                sc = jnp.where(_diag_mask(t), sc, -jnp.inf)
            p = jnp.exp2(sc - lse_ref[:, 0:1])
            do = do_ref[...]
            delta = jnp.sum(do.astype(F32) * ao_ref[...], axis=-1, keepdims=True)
            dv_acc[...] += _dot_tn(_mx(p), do)
            dp = _dot_nt(do, v_ref[...])
            ds = _mx(p * (dp - delta))
            dk_acc[...] += _dot_tn(ds, q)
            dq_part = _dot(ds, k)
            rows = pl.ds(pl.multiple_of(qi * t, t), t)

            @pl.when(ki == 0)
            def _():
                dq_ref[rows, :] = dq_part

            @pl.when(ki > 0)
            def _():
                dq_ref[rows, :] += dq_part

        @pl.when(qi == ki)
        def _():
            pair(True)

        @pl.when(qi > ki)
        def _():
            pair(False)

        @pl.when(qi == n - 1)
        def _():
            dk_ref[...] = dk_acc[...] * LN2
            dv_ref[...] = dv_acc[...]

    q_map = lambda h, p, ki_ref, qi_ref: (qi_ref[p], h)
    kv_map = lambda h, p, ki_ref, qi_ref: (ki_ref[p], h)
    grid_spec = pltpu.PrefetchScalarGridSpec(
        num_scalar_prefetch=2,
        grid=(HEADS, len(pairs)),
        in_specs=[
            pl.BlockSpec((t, HEAD_PAD), q_map),
            pl.BlockSpec((t, HEAD_PAD), kv_map),
            pl.BlockSpec((t, HEAD_DIM), kv_map),
            pl.BlockSpec((t, HEAD_DIM), q_map),
            pl.BlockSpec((t, HEAD_DIM), q_map),
            pl.BlockSpec((t, HEAD_DIM), q_map),
        ],
        out_specs=[
            pl.BlockSpec((s, HEAD_PAD), lambda h, p, ki_ref, qi_ref: (0, h)),
            pl.BlockSpec((t, HEAD_PAD), kv_map),
            pl.BlockSpec((t, HEAD_DIM), kv_map),
        ],
        scratch_shapes=[pltpu.VMEM((t, HEAD_PAD), F32), pltpu.VMEM((t, HEAD_DIM), F32)],
    )
    return pl.pallas_call(
        body,
        name="flash_bwd",
        grid_spec=grid_spec,
        out_shape=[
            jax.ShapeDtypeStruct((s, HEADS * HEAD_PAD), F32),
            jax.ShapeDtypeStruct((s, HEADS * HEAD_PAD), F32),
            jax.ShapeDtypeStruct((s, D_MODEL), F32),
        ],
        compiler_params=_params(VMEM_LIMIT_BIG),
    )(ki_list, qi_list, q_all, k_all, v_all, dao, ao, lse)


def _mla_prep_bwd(dq_all, dk_all, dv_all, proj, q_a_g, kv_a_g, w_uq_p, w_kn, w_v, cos, sin_a, sin_b):
    s = proj.shape[0]
    tm = min(s, TM_ROW)

    def body(dq_ref, dk_ref, dv_ref, sm_ref, gq_ref, gk_ref, wq_ref, wkn_ref, wv_ref, cos_ref, sa_ref, sb_ref,
             dsm_ref, dqf_ref, dkn_ref, dvb_ref, dgq_ref, dgk_ref):
        i = pl.program_id(0)

        @pl.when(i == 0)
        def _():
            dgq_ref[...] = jnp.zeros_like(dgq_ref)
            dgk_ref[...] = jnp.zeros_like(dgk_ref)

        cos_t, sa, sb = cos_ref[...], sa_ref[...], sb_ref[...]
        dkpe = jnp.zeros((tm, HEAD_DIM), F32)
        for h in range(HEADS):
            lo = h * HEAD_PAD
            dqf_ref[:, lo:lo + HEAD_DIM] = _mx(dq_ref[:, lo:lo + HEAD_DIM] * QK_SCALE)
            dqf_ref[:, lo + HEAD_DIM:lo + HEAD_PAD] = _mx(
                _rope_t(dq_ref[:, lo + HEAD_DIM:lo + HEAD_PAD] * QK_SCALE, cos_t, sa, sb))
            dkn_ref[:, h * HEAD_DIM:(h + 1) * HEAD_DIM] = _mx(dk_ref[:, lo:lo + HEAD_DIM])
            dkpe = dkpe + dk_ref[:, lo + HEAD_DIM:lo + HEAD_PAD]
        dkr = _rope_t(dkpe, cos_t, sa, sb)
        dvb = _mx(dv_ref[...])
        dvb_ref[...] = dvb
        dcqn = _dot_nt(dqf_ref[...], wq_ref[...])
        dckvn = _dot_nt(dkn_ref[...], wkn_ref[...]) + _dot_nt(dvb, wv_ref[...])

        small = sm_ref[...]
        cq = small[:, :Q_LORA]
        ckv = small[:, Q_LORA:Q_LORA + KV_LORA]
        rq = lax.rsqrt(jnp.mean(cq * cq, axis=-1, keepdims=True) + EPS)
        rk = lax.rsqrt(jnp.mean(ckv * ckv, axis=-1, keepdims=True) + EPS)
        cqh = cq * rq
        ckh = ckv * rk
        dgq_ref[...] += _bcast_rows(jnp.sum(dcqn * cqh, axis=0, keepdims=True), 8)
        dgk_ref[...] += _bcast_rows(jnp.sum(dckvn * ckh, axis=0, keepdims=True), 8)
        tq = dcqn * gq_ref[...]
        tk = dckvn * gk_ref[...]
        dcq = rq * (tq - cqh * jnp.mean(tq * cqh, axis=-1, keepdims=True))
        dckv = rk * (tk - ckh * jnp.mean(tk * ckh, axis=-1, keepdims=True))
        dsm_ref[:, :Q_LORA] = _mx(dcq)
        dsm_ref[:, Q_LORA:Q_LORA + KV_LORA] = _mx(dckv)
        dsm_ref[:, Q_LORA + KV_LORA:Q_LORA + KV_LORA + HEAD_DIM] = _mx(dkr)
        dsm_ref[:, Q_LORA + KV_LORA + HEAD_DIM:] = jnp.zeros((tm, SEG - Q_LORA - KV_LORA - HEAD_DIM), MXU_DTYPE)

    def const(shape):
        return pl.BlockSpec(shape, lambda i: (0, 0))

    def rows(w):
        return pl.BlockSpec((tm, w), lambda i: (i, 0))

    return pl.pallas_call(
        body,
        name="mla_prep_bwd",
        grid=(s // tm,),
        in_specs=[
            rows(HEADS * HEAD_PAD), rows(HEADS * HEAD_PAD), rows(D_MODEL),
            pl.BlockSpec((tm, SEG), lambda i: (i, SMALL_SEG)),
            const((1, Q_LORA)), const((1, KV_LORA)),
            const((Q_LORA, HEADS * HEAD_PAD)), const((KV_LORA, D_MODEL)), const((KV_LORA, D_MODEL)),
            rows(HEAD_DIM), rows(HEAD_DIM), rows(HEAD_DIM),
        ],
        out_specs=[rows(SEG), rows(HEADS * HEAD_PAD), rows(D_MODEL), rows(D_MODEL),
                   const((8, Q_LORA)), const((8, KV_LORA))],
        out_shape=[
            jax.ShapeDtypeStruct((s, SEG), MXU_DTYPE),
            jax.ShapeDtypeStruct((s, HEADS * HEAD_PAD), MXU_DTYPE),
            jax.ShapeDtypeStruct((s, D_MODEL), MXU_DTYPE),
            jax.ShapeDtypeStruct((s, D_MODEL), MXU_DTYPE),
            jax.ShapeDtypeStruct((8, Q_LORA), F32),
            jax.ShapeDtypeStruct((8, KV_LORA), F32),
        ],
        compiler_params=_params(),
    )(dq_all, dk_all, dv_all, proj, q_a_g, kv_a_g, w_uq_p, w_kn, w_v, cos, sin_a, sin_b)


def _hgrn_bwd(proj, lb_logits, hg_norm_g, o_all, dya, states):
    s = proj.shape[0]
    t = min(s, T_HGRN)
    nb = s // t
    nc = t // HG_CHUNK

    def body(hq_ref, hf_ref, hi_ref, hz_ref, lb_ref, g_ref, o_ref, dya_ref, st_ref,
             dhq_ref, dhf_ref, dhi_ref, dhz_ref, dlb_ref, dg_ref, dstate):
        h, b = pl.program_id(0), pl.program_id(1)

        @pl.when(b == 0)
        def _():
            dstate[...] = jnp.zeros_like(dstate)
            dlb_ref[...] = jnp.zeros_like(dlb_ref)

        @pl.when((b == 0) & (h == 0))
        def _():
            dg_ref[...] = jnp.zeros_like(dg_ref)

        same, lower, upper = _chunk_masks(t)
        same_f = same.astype(F32)
        hq, hf, hz = hq_ref[...], hf_ref[...], hz_ref[...]
        gt = _hgrn_gates(hq, hf, lb_ref[...], same_f, lower.astype(F32))
        v = hi_ref[...]
        vb = _mx(v)
        qi, ki, ko = gt["qi"], gt["ki"], gt["ko"]
        qib, kib, kob = _mx(qi), _mx(ki), _mx(ko)

        o = o_ref[...]
        ghg = g_ref[...]
        sz = _sigmoid(hz)
        r = lax.rsqrt(jnp.mean(o * o, axis=-1, keepdims=True) + EPS)
        on = o * r
        dya_t = dya_ref[...]
        don = dya_t * (hz * sz)
        dhz_ref[...] = _mx(dya_t * (on * ghg) * (sz + hz * sz * (1.0 - sz)))
        dg_ref[...] += _bcast_rows(jnp.sum(don * on, axis=0, keepdims=True), 8)
        tt = don * ghg
        do = r * (tt - on * jnp.mean(tt * on, axis=-1, keepdims=True))
        dob = _mx(do)

        sts = [st_ref[0, 0]]
        for c in range(nc - 1):
            sl = slice(c * HG_CHUNK, (c + 1) * HG_CHUNK)
            sts.append(sts[-1] * gt["dec"][c * HG_CHUNK:c * HG_CHUNK + 1, :] + _dot_tn(vb[sl], kob[sl]))

        a = jnp.where(lower, _dot_nt(qib, kib), 0.0)
        da = _mx(jnp.where(lower, _dot_nt(dob, vb), 0.0))
        dqi_intra = _dot(da, kib)
        dki = _dot_tn(da, qib)
        dv_intra = _dot_tn(_mx(a), dob)

        dst = dstate[...]
        dqi_parts, dko_parts, dv_parts, dd_parts = [None] * nc, [None] * nc, [None] * nc, [None] * nc
        for c in reversed(range(nc)):
            sl = slice(c * HG_CHUNK, (c + 1) * HG_CHUNK)
            dec = gt["dec"][c * HG_CHUNK:c * HG_CHUNK + 1, :]
            dstb = _mx(dst)
            dv_parts[c] = dv_intra[sl] + _dot_nt(kob[sl], dstb)
            dko_parts[c] = _dot(vb[sl], dstb)
            dqi_parts[c] = dqi_intra[sl] + _dot(dob[sl], _mx(sts[c]))
            dd_parts[c] = _bcast_rows(jnp.sum(dst * sts[c], axis=0, keepdims=True) * dec, HG_CHUNK)
            dst = dst * dec + _dot_tn(dob[sl], qib[sl])
        dstate[...] = dst
        dqi = jnp.concatenate(dqi_parts, axis=0)
        dko = jnp.concatenate(dko_parts, axis=0)
        dv = jnp.concatenate(dv_parts, axis=0)
        dd = jnp.concatenate(dd_parts, axis=0)

        dq = dqi * gt["eb"]
        dk = dki * gt["enb"] + dko * gt["eo"]
        db = dqi * qi - dki * ki - dko * ko
        dlogf = _dot_exact(upper.astype(F32), db) + _dot_exact(same_f, dko * ko) + dd
        df = dlogf / gt["f"] - dk
        lb, sig, sq = gt["lb"], gt["sig"], gt["sq"]
        dhf_ref[...] = _mx(df * (1.0 - lb) * (sig * (1.0 - sig)))
        dhq_ref[...] = _mx(dq * (sq + hq * sq * (1.0 - sq)))
        dhi_ref[...] = _mx(dv)
        dlb = jnp.sum(df * (1.0 - sig), axis=0, keepdims=True) * (lb * (1.0 - lb))
        dlb_ref[...] += jnp.concatenate([dlb, -dlb], axis=0)

    def seg(k):
        return pl.BlockSpec((t, HEAD_DIM), lambda h, b, k=k: (nb - 1 - b, k * HEADS + h))

    blk = pl.BlockSpec((t, HEAD_DIM), lambda h, b: (nb - 1 - b, h))
    b16 = jax.ShapeDtypeStruct((s, D_MODEL), MXU_DTYPE)
    return pl.pallas_call(
        body,
        name="hgrn_bwd",
        grid=(HEADS, nb),
        in_specs=[seg(0), seg(1), seg(2), seg(3),
                  pl.BlockSpec((2, HEAD_DIM), lambda h, b: (0, h)),
                  pl.BlockSpec((1, HEAD_DIM), lambda h, b: (0, 0)),
                  blk, blk,
                  pl.BlockSpec((1, 1, HEAD_DIM, HEAD_DIM), lambda h, b: (nb - 1 - b, h, 0, 0))],
        out_specs=[blk, blk, blk, blk,
                   pl.BlockSpec((2, HEAD_DIM), lambda h, b: (0, h)),
                   pl.BlockSpec((8, HEAD_DIM), lambda h, b: (0, 0))],
        out_shape=[b16, b16, b16, b16,
                   jax.ShapeDtypeStruct((2, D_MODEL), F32),
                   jax.ShapeDtypeStruct((8, HEAD_DIM), F32)],
        scratch_shapes=[pltpu.VMEM((HEAD_DIM, HEAD_DIM), F32)],
        compiler_params=_params(),
    )(proj, proj, proj, proj, lb_logits, hg_norm_g, o_all, dya, states)


def _dh_bwd(dproj, w_in_p, x, dx2, norm_g):
    s = x.shape[0]
    tm = min(s, TM_DH)
    nk = PROJ_W // SEG

    def body(dp_ref, w_ref, x_ref, dx2_ref, g_ref, gx_ref, dng_ref, acc):
        i, k = pl.program_id(0), pl.program_id(1)

        @pl.when((i == 0) & (k == 0))
        def _():
            dng_ref[...] = jnp.zeros_like(dng_ref)

        part = _dot_nt(dp_ref[...], w_ref[...])

        @pl.when(k == 0)
        def _():
            acc[...] = part

        @pl.when(k > 0)
        def _():
            acc[...] += part

        @pl.when(k == nk - 1)
        def _():
            dh = acc[...]
            xf = x_ref[...]
            r = lax.rsqrt(jnp.mean(xf * xf, axis=-1, keepdims=True) + EPS)
            xh = xf * r
            dng_ref[...] += _bcast_rows(jnp.sum(dh * xh, axis=0, keepdims=True), 8)
            tt = dh * g_ref[...]
            gx_ref[...] = dx2_ref[...] + r * (tt - xh * jnp.mean(tt * xh, axis=-1, keepdims=True))

    rows = pl.BlockSpec((tm, D_MODEL), lambda i, k: (i, 0))
    return pl.pallas_call(
        body,
        name="dh_bwd",
        grid=(s // tm, nk),
        in_specs=[
            pl.BlockSpec((tm, SEG), lambda i, k: (i, k)),
            pl.BlockSpec((D_MODEL, SEG), lambda i, k: (0, k)),
            rows, rows,
            pl.BlockSpec((1, D_MODEL), lambda i, k: (0, 0)),
        ],
        out_specs=[rows, pl.BlockSpec((8, D_MODEL), lambda i, k: (0, 0))],
        out_shape=[jax.ShapeDtypeStruct((s, D_MODEL), F32), jax.ShapeDtypeStruct((8, D_MODEL), F32)],
        scratch_shapes=[pltpu.VMEM((tm, D_MODEL), F32)],
        compiler_params=_params(),
    )(dproj, w_in_p, x, dx2, norm_g)


def _matmul_tn(a, b, name):
    s, m = a.shape
    n = b.shape[1]
    ts = min(s, TS_TN)
    tn = min(n, SEG)

    def body(a_ref, b_ref, o_ref):
        k = pl.program_id(1)
        part = _dot_tn(a_ref[...], b_ref[...])

        @pl.when(k == 0)
        def _():
            o_ref[...] = part

        @pl.when(k > 0)
        def _():
            o_ref[...] += part

    return pl.pallas_call(
        body,
        name=name,
        grid=(n // tn, s // ts),
        in_specs=[pl.BlockSpec((ts, m), lambda j, k: (k, 0)), pl.BlockSpec((ts, tn), lambda j, k: (k, j))],
        out_specs=pl.BlockSpec((m, tn), lambda j, k: (0, j)),
        out_shape=jax.ShapeDtypeStruct((m, n), F32),
        compiler_params=_params(),
    )(a, b)


def _rope_tables(s):
    inv = ROPE_THETA ** (-jnp.arange(0, QK_ROPE, 2, dtype=F32) / QK_ROPE)
    ang = jnp.arange(s, dtype=F32)[:, None] * inv[None, :]
    cos, sin = jnp.cos(ang), jnp.sin(ang)
    z32 = jnp.zeros_like(cos)
    z64 = jnp.zeros((s, HEAD_DIM - QK_ROPE), F32)
    cos_t = jnp.concatenate([cos, cos, z64], axis=1)
    sin_a = jnp.concatenate([-sin, z32, z64], axis=1)
    sin_b = jnp.concatenate([z32, sin, z64], axis=1)
    return cos_t, sin_a, sin_b


def _pack_big(w_in, w_uq, w_ukv, w_pa, w_pb, w_out):
    return jnp.concatenate([a.reshape(-1, PACK_COLS) for a in (w_in, w_uq, w_ukv, w_pa, w_pb, w_out)], axis=0)


def _pack_small(norm_g, b_gate, lb_logits, hg_norm_g, q_a_g, kv_a_g, final_norm_g, extra):
    misc = jnp.concatenate([hg_norm_g.reshape(-1), q_a_g.reshape(-1), kv_a_g.reshape(-1), extra.reshape(-1),
                            jnp.zeros((PACK_COLS - HEAD_DIM - Q_LORA - KV_LORA - 1,), F32)])
    return jnp.concatenate([norm_g.reshape(1, -1), b_gate.reshape(2, -1), lb_logits.reshape(2, -1),
                            misc.reshape(1, -1), final_norm_g.reshape(1, -1), jnp.zeros((1, PACK_COLS), F32)], axis=0)


def _unpack(p):
    r0 = 0
    big = []
    for rows, shape in ((ROWS_W_IN, (1, D_MODEL, IN_COLS // N_DEV)), (ROWS_W_UQ, (1, Q_LORA, QK_DIM)),
                        (ROWS_W_UKV, (1, KV_LORA, 2 * HEAD_DIM)), (ROWS_W_PROJ, (1, HEAD_DIM, D_MODEL)),
                        (ROWS_W_PROJ, (1, HEAD_DIM, D_MODEL)), (ROWS_W_PROJ, (1, HEAD_DIM, D_MODEL))):
        big.append(p[r0:r0 + rows].reshape(shape))
        r0 += rows
    sm = p[ROWS_BIG:]
    w_in, w_uq, w_ukv, w_pa, w_pb, w_out = big
    misc = sm[5]
    return dict(
        norm_g=sm[0:1], w_in=w_in, b_gate=sm[1:3].reshape(1, -1), lb_logits=sm[3:5],
        hg_norm_g=misc[None, :HEAD_DIM], q_a_g=misc[None, HEAD_DIM:HEAD_DIM + Q_LORA],
        w_uq=w_uq, kv_a_g=misc[None, HEAD_DIM + Q_LORA:HEAD_DIM + Q_LORA + KV_LORA], w_ukv=w_ukv,
        w_proj_a=w_pa, w_proj_b=w_pb, w_out=w_out, final_norm_g=sm[6],
        extra=misc[HEAD_DIM + Q_LORA + KV_LORA],
    )


def _full_weights(gathered):
    r0 = 0
    w_in = gathered[:, r0:r0 + ROWS_W_IN].reshape(N_DEV, D_MODEL, IN_COLS // N_DEV)
    w_in = w_in.transpose(1, 0, 2).reshape(D_MODEL, IN_COLS)
    pad_at = 4 * SEG + Q_LORA + KV_LORA + QK_ROPE
    w_in_p = jnp.concatenate([w_in[:, :pad_at], jnp.zeros((D_MODEL, PROJ_W - IN_COLS), w_in.dtype),
                              w_in[:, pad_at:]], axis=1)
    r0 += ROWS_W_IN
    w_uq = gathered[:, r0:r0 + ROWS_W_UQ].reshape(N_DEV, Q_LORA, QK_DIM).transpose(1, 0, 2)
    w_uq_p = jnp.concatenate([w_uq, jnp.zeros((Q_LORA, HEADS, HEAD_PAD - QK_DIM), w_uq.dtype)], axis=2)
    w_uq_p = w_uq_p.reshape(Q_LORA, HEADS * HEAD_PAD)
    r0 += ROWS_W_UQ
    w_ukv = gathered[:, r0:r0 + ROWS_W_UKV].reshape(N_DEV, KV_LORA, 2 * HEAD_DIM).transpose(1, 0, 2)
    w_kn = w_ukv[:, :, :HEAD_DIM].reshape(KV_LORA, D_MODEL)
    w_v = w_ukv[:, :, HEAD_DIM:].reshape(KV_LORA, D_MODEL)
    r0 += ROWS_W_UKV
    mats = []
    for _ in range(3):
        mats.append(gathered[:, r0:r0 + ROWS_W_PROJ].reshape(D_MODEL, D_MODEL))
        r0 += ROWS_W_PROJ
    return w_in_p, w_uq_p, w_kn, w_v, mats[0], mats[1], mats[2]


def _grad_slabs(dw_in_p, dw_uq_p, dw_kn, dw_v, dw_pa, dw_pb, dw_out, small):
    pad_at = 4 * SEG + Q_LORA + KV_LORA + QK_ROPE
    dw_in = jnp.concatenate([dw_in_p[:, :pad_at], dw_in_p[:, pad_at + PROJ_W - IN_COLS:]], axis=1)
    dw_in = dw_in.reshape(D_MODEL, N_DEV, IN_COLS // N_DEV).transpose(1, 0, 2).reshape(N_DEV, ROWS_W_IN, PACK_COLS)
    dw_uq = dw_uq_p.reshape(Q_LORA, HEADS, HEAD_PAD)[:, :, :QK_DIM].transpose(1, 0, 2).reshape(N_DEV, ROWS_W_UQ, PACK_COLS)
    dw_ukv = jnp.concatenate([dw_kn.reshape(KV_LORA, HEADS, HEAD_DIM), dw_v.reshape(KV_LORA, HEADS, HEAD_DIM)], axis=2)
    dw_ukv = dw_ukv.transpose(1, 0, 2).reshape(N_DEV, ROWS_W_UKV, PACK_COLS)
    mats = [a.reshape(N_DEV, ROWS_W_PROJ, PACK_COLS) for a in (dw_pa, dw_pb, dw_out)]
    sm = jnp.broadcast_to(small[None], (N_DEV, ROWS_SMALL, PACK_COLS))
    return jnp.concatenate([dw_in, dw_uq, dw_ukv] + mats + [sm], axis=1)


def _local_grads(x, target, norm_g, b_gate, lb_logits, hg_norm_g, q_a_g, kv_a_g, final_g,
                 w_in_p, w_uq_p, w_kn, w_v, w_pa, w_pb, w_out):
    s = x.shape[0]
    cos, sin_a, sin_b = _rope_tables(s)
    proj, h = _inproj(x, norm_g, w_in_p)
    o_all, ya, states = _hgrn_fwd(proj, lb_logits, hg_norm_g)
    q_all, k_all, v_all, cqn, ckvn = _mla_prep(proj, q_a_g, kv_a_g, w_uq_p, w_kn, w_v, cos, sin_a, sin_b)
    ao, lse, yb = _flash_fwd(q_all, k_all, v_all, proj)
    (dx2, dya, dyb, dg0, dg1, merged_b, dpa_b, dpb_b, dx2_b,
     loss_acc, dfg_acc, dbg_acc) = _merge_fwd_bwd(x, target, ya, yb, proj, b_gate, final_g, w_pa, w_pb, w_out)
    dao, dmz = _attn_gate_bwd(dyb, ao, proj)
    dq_all, dk_all, dv_all = _flash_bwd(q_all, k_all, v_all, dao, ao, lse)
    dsmall, dqf_b, dkn_b, dv_b, dgq_acc, dgk_acc = _mla_prep_bwd(
        dq_all, dk_all, dv_all, proj, q_a_g, kv_a_g, w_uq_p, w_kn, w_v, cos, sin_a, sin_b)
    dhq, dhf, dhi, dhz, dlb, dhg_acc = _hgrn_bwd(proj, lb_logits, hg_norm_g, o_all, dya, states)
    segs = [dhq, dhf, dhi, dhz, dsmall, dmz, dg0, dg1]
    dproj = jnp.concatenate(segs, axis=1)
    grad_x, dng_acc = _dh_bwd(dproj, w_in_p, x, dx2, norm_g)
    dw_in_p = jnp.concatenate([_matmul_tn(h, sg, "dw_in_%d" % k) for k, sg in enumerate(segs)], axis=1)
    return dict(
        loss=loss_acc[0, 0], grad_x=grad_x,
        norm_g=dng_acc[0:1], b_gate=dbg_acc[0:1], lb_logits=dlb, hg_norm_g=dhg_acc[0:1],
        q_a_g=dgq_acc[0:1], kv_a_g=dgk_acc[0:1], final_norm_g=dfg_acc[0],
        w_in_p=dw_in_p,
        w_uq_p=_matmul_tn(cqn, dqf_b, "dw_uq"),
        w_kn=_matmul_tn(ckvn, dkn_b, "dw_kn"),
        w_v=_matmul_tn(ckvn, dv_b, "dw_v"),
        w_pa=_matmul_tn(ya, dpa_b, "dw_pa"),
        w_pb=_matmul_tn(yb, dpb_b, "dw_pb"),
        w_out=_matmul_tn(merged_b, dx2_b, "dw_out"),
    )


def kernel(x, norm_g, w_in, b_gate, lb_logits, hg_norm_g, q_a_g, w_uq, kv_a_g, w_ukv, w_proj_a, w_proj_b, w_out, final_norm_g, loss_target, m_norm_g, m_w_in, m_b_gate, m_lb_logits, m_hg_norm_g, m_q_a_g, m_w_uq, m_kv_a_g, m_w_ukv, m_w_proj_a, m_w_proj_b, m_w_out, m_final_norm_g, v_norm_g, v_w_in, v_b_gate, v_lb_logits, v_hg_norm_g, v_q_a_g, v_w_uq, v_kv_a_g, v_w_ukv, v_w_proj_a, v_w_proj_b, v_w_out, v_final_norm_g):
    zero = jnp.zeros((1,), F32)
    shard_b16 = _pack_big(w_in, w_uq, w_ukv, w_proj_a, w_proj_b, w_out).astype(MXU_DTYPE)
    full = _full_weights(_all_gather_packed(shard_b16))
    g = _local_grads(x[0], loss_target[0], norm_g, b_gate, lb_logits, hg_norm_g, q_a_g, kv_a_g,
                     final_norm_g.reshape(1, -1), *full)
    small = _pack_small(g["norm_g"], g["b_gate"], g["lb_logits"], g["hg_norm_g"], g["q_a_g"], g["kv_a_g"],
                        g["final_norm_g"], g["loss"])
    slabs = _grad_slabs(g["w_in_p"], g["w_uq_p"], g["w_kn"], g["w_v"], g["w_pa"], g["w_pb"], g["w_out"], small)
    recv = _reduce_scatter_exchange(slabs)

    def packed(w6, s7):
        return jnp.concatenate([_pack_big(*w6), _pack_small(*s7, zero)], axis=0)

    w_p = packed((w_in, w_uq, w_ukv, w_proj_a, w_proj_b, w_out),
                 (norm_g, b_gate, lb_logits, hg_norm_g, q_a_g, kv_a_g, final_norm_g))
    m_p = packed((m_w_in, m_w_uq, m_w_ukv, m_w_proj_a, m_w_proj_b, m_w_out),
                 (m_norm_g, m_b_gate, m_lb_logits, m_hg_norm_g, m_q_a_g, m_kv_a_g, m_final_norm_g))
    v_p = packed((v_w_in, v_w_uq, v_w_ukv, v_w_proj_a, v_w_proj_b, v_w_out),
                 (v_norm_g, v_b_gate, v_lb_logits, v_hg_norm_g, v_q_a_g, v_kv_a_g, v_final_norm_g))
    g_p, d_p, mn_p, vn_p = _sum_adamw(recv, w_p, m_p, v_p)
    names = ["norm_g", "w_in", "b_gate", "lb_logits", "hg_norm_g", "q_a_g", "w_uq", "kv_a_g", "w_ukv",
             "w_proj_a", "w_proj_b", "w_out", "final_norm_g"]
    grads, deltas, new_m, new_v = _unpack(g_p), _unpack(d_p), _unpack(mn_p), _unpack(vn_p)
    return (grads["extra"], g["grad_x"][None],
            *[grads[n] for n in names], *[deltas[n] for n in names],
            *[new_m[n] for n in names], *[new_v[n] for n in names])
```

```python
import functools

import jax
import jax.numpy as jnp
from jax import lax
from jax.experimental import pallas as pl
from jax.experimental.pallas import tpu as pltpu

D_MODEL = 1024
HEADS = 8
HEAD_DIM = 128
HG_CHUNK = 32
Q_LORA = 384
KV_LORA = 256
QK_ROPE = 64
QK_DIM = 192
ROPE_THETA = 10000.0
EPS = 1e-6
IN_COLS = 7872
ADAM_LR = 0.001
ADAM_B1 = 0.9
ADAM_B2 = 0.999
ADAM_EPS = 1e-08
ADAM_WD = 0.01
ADAM_STEP = 10

N_DEV = 8
SEG = 1024
PROJ_W = 8 * SEG
SMALL_SEG = 4
MZ_SEG = 5
GL_SEG = 6
HEAD_PAD = 256
PACK_COLS = 1024
ROWS_W_IN = 984
ROWS_W_UQ = 72
ROWS_W_UKV = 64
ROWS_W_PROJ = 128
ROWS_BIG = ROWS_W_IN + ROWS_W_UQ + ROWS_W_UKV + 3 * ROWS_W_PROJ
ROWS_SMALL = 8
ROWS_PACK = ROWS_BIG + ROWS_SMALL

QK_SCALE = QK_DIM ** -0.5
LOG2E = 1.4426950408889634
LN2 = 0.6931471805599453
Q_PRESCALE = QK_SCALE * LOG2E

MXU_DTYPE = jnp.bfloat16
VMEM_LIMIT = 48 * 1024 * 1024
VMEM_LIMIT_BIG = 60 * 1024 * 1024

TM_INPROJ = 512
T_HGRN = 256
HG_HEADS_PER_STEP = 2
TM_ROW = 256
T_ATT = 1024
T_ATT_BWD = 1024
ATT_SUB = 4
TS_TN = 512
TM_DH = 512
TR_ADAM = 216

F32 = jnp.float32
MESH = pl.DeviceIdType.MESH


def _dot(a, b):
    return jnp.dot(a, b, preferred_element_type=F32)


def _dot_nt(a, b):
    return lax.dot_general(a, b, (((1,), (1,)), ((), ())), preferred_element_type=F32)


def _dot_tn(a, b):
    return lax.dot_general(a, b, (((0,), (0,)), ((), ())), preferred_element_type=F32)


def _dot_exact(a, b):
    return jnp.dot(a, b, preferred_element_type=F32, precision=lax.Precision.HIGHEST)


def _mx(a):
    return a.astype(MXU_DTYPE)


def _sigmoid(x):
    return 1.0 / (1.0 + jnp.exp(-x))


def _params(vmem=VMEM_LIMIT, **kw):
    return pltpu.CompilerParams(vmem_limit_bytes=vmem, **kw)


def _bcast_rows(row, n):
    return jnp.broadcast_to(row, (n, row.shape[-1]))


HBM_SPEC = pl.BlockSpec(memory_space=pltpu.HBM)


def _all_gather_packed(shard):
    rows, cols = shard.shape

    def body(x_ref, out_ref, send_sems, recv_sems, local_sem):
        x, y, c = lax.axis_index("x"), lax.axis_index("y"), lax.axis_index("c")
        me, sibling = (x, y, c), (x, y, 1 - c)
        chips = [(1 - x, y), (x, 1 - y), (1 - x, 1 - y)]

        def slot(px, py, pc):
            return out_ref.at[4 * px + 2 * py + pc]

        def copy(k, block, to, src=None):
            return pltpu.make_async_remote_copy(
                src_ref=slot(*block) if src is None else src,
                dst_ref=slot(*block),
                send_sem=send_sems.at[k],
                recv_sem=recv_sems.at[k],
                device_id=to,
                device_id_type=MESH,
            )

        mine = pltpu.make_async_copy(x_ref, slot(*me), local_sem)
        mine.start()
        first = [copy(0, me, sibling, src=x_ref)]
        first += [copy(1 + j, me, (*chip, c), src=x_ref) for j, chip in enumerate(chips)]
        for cp in first:
            cp.start()
        passed = [copy(4 + j, (*chip, c), sibling) for j, chip in enumerate(chips)]
        for j, chip in enumerate(chips):
            copy(1 + j, (*chip, c), me).wait_recv()
            passed[j].start()
        copy(0, sibling, me).wait_recv()
        for j, chip in enumerate(chips):
            copy(4 + j, (*chip, 1 - c), me).wait_recv()
        for cp in first + passed:
            cp.wait_send()
        mine.wait()

    return pl.pallas_call(
        body,
        name="ag_weights",
        out_shape=jax.ShapeDtypeStruct((N_DEV, rows, cols), shard.dtype),
        in_specs=[HBM_SPEC],
        out_specs=HBM_SPEC,
        scratch_shapes=[
            pltpu.SemaphoreType.DMA((7,)),
            pltpu.SemaphoreType.DMA((7,)),
            pltpu.SemaphoreType.DMA,
        ],
    )(shard)


def _reduce_scatter_exchange(slabs):
    _, rows, cols = slabs.shape

    def body(g_ref, recv_ref, send_sems, recv_sems, local_sem):
        x, y, c = lax.axis_index("x"), lax.axis_index("y"), lax.axis_index("c")
        me = 4 * x + 2 * y + c

        def copy(k):
            px, py, pc = x ^ ((k >> 2) & 1), y ^ ((k >> 1) & 1), c ^ (k & 1)
            return pltpu.make_async_remote_copy(
                src_ref=g_ref.at[4 * px + 2 * py + pc],
                dst_ref=recv_ref.at[me],
                send_sem=send_sems.at[k - 1],
                recv_sem=recv_sems.at[k - 1],
                device_id=(px, py, pc),
                device_id_type=MESH,
            )

        def landing(k):
            px, py, pc = x ^ ((k >> 2) & 1), y ^ ((k >> 1) & 1), c ^ (k & 1)
            peer = 4 * px + 2 * py + pc
            return pltpu.make_async_remote_copy(
                src_ref=g_ref.at[peer],
                dst_ref=recv_ref.at[peer],
                send_sem=send_sems.at[k - 1],
                recv_sem=recv_sems.at[k - 1],
                device_id=(px, py, pc),
                device_id_type=MESH,
            )

        mine = pltpu.make_async_copy(g_ref.at[me], recv_ref.at[me], local_sem)
        mine.start()
        sends = [copy(k) for k in range(1, N_DEV)]
        for cp in sends:
            cp.start()
        for k in range(1, N_DEV):
            landing(k).wait_recv()
        for cp in sends:
            cp.wait_send()
        mine.wait()

    return pl.pallas_call(
        body,
        name="rs_grads",
        out_shape=jax.ShapeDtypeStruct(slabs.shape, slabs.dtype),
        in_specs=[HBM_SPEC],
        out_specs=HBM_SPEC,
        scratch_shapes=[
            pltpu.SemaphoreType.DMA((7,)),
            pltpu.SemaphoreType.DMA((7,)),
            pltpu.SemaphoreType.DMA,
        ],
    )(slabs)


def _sum_adamw(recv, w, m, v):
    rows, cols = w.shape
    tr = TR_ADAM if rows % TR_ADAM == 0 else rows

    def body(r_ref, w_ref, m_ref, v_ref, g_out, d_out, m_out, v_out):
        g = r_ref[0]
        for i in range(1, N_DEV):
            g = g + r_ref[i]
        m_new = ADAM_B1 * m_ref[...] + (1.0 - ADAM_B1) * g
        v_new = ADAM_B2 * v_ref[...] + (1.0 - ADAM_B2) * (g * g)
        m_hat = m_new / (1.0 - ADAM_B1 ** ADAM_STEP)
        v_hat = v_new / (1.0 - ADAM_B2 ** ADAM_STEP)
        g_out[...] = g
        d_out[...] = -ADAM_LR * (m_hat / (jnp.sqrt(v_hat) + ADAM_EPS) + ADAM_WD * w_ref[...])
        m_out[...] = m_new
        v_out[...] = v_new

    row_spec = pl.BlockSpec((tr, cols), lambda i: (i, 0))
    shape = jax.ShapeDtypeStruct((rows, cols), F32)
    return pl.pallas_call(
        body,
        name="sum_adamw",
        grid=(rows // tr,),
        in_specs=[pl.BlockSpec((N_DEV, tr, cols), lambda i: (0, i, 0)), row_spec, row_spec, row_spec],
        out_specs=[row_spec] * 4,
        out_shape=[shape] * 4,
        compiler_params=_params(),
    )(recv, w, m, v)


def _inproj(x, norm_g, w_in_p):
    s = x.shape[0]
    tm = min(s, TM_INPROJ)

    def body(x_ref, g_ref, w_ref, proj_ref, h_ref):
        xf = x_ref[...]
        r = lax.rsqrt(jnp.mean(xf * xf, axis=-1, keepdims=True) + EPS)
        h = _mx(xf * r * g_ref[...])
        h_ref[...] = h
        proj_ref[...] = _dot(h, w_ref[...])

    return pl.pallas_call(
        body,
        name="inproj",
        grid=(s // tm, PROJ_W // SEG),
        in_specs=[
            pl.BlockSpec((tm, D_MODEL), lambda i, j: (i, 0)),
            pl.BlockSpec((1, D_MODEL), lambda i, j: (0, 0)),
            pl.BlockSpec((D_MODEL, SEG), lambda i, j: (0, j)),
        ],
        out_specs=[
            pl.BlockSpec((tm, SEG), lambda i, j: (i, j)),
            pl.BlockSpec((tm, D_MODEL), lambda i, j: (i, 0)),
        ],
        out_shape=[
            jax.ShapeDtypeStruct((s, PROJ_W), F32),
            jax.ShapeDtypeStruct((s, D_MODEL), MXU_DTYPE),
        ],
        compiler_params=_params(),
    )(x, norm_g, w_in_p)


def _chunk_lower_mask(t):
    row = lax.broadcasted_iota(jnp.int32, (t, t), 0)
    col = lax.broadcasted_iota(jnp.int32, (t, t), 1)
    return ((row // HG_CHUNK) == (col // HG_CHUNK)) & (col <= row)


def _chunk_pos(t):
    return lax.broadcasted_iota(jnp.int32, (t, HEAD_DIM), 0) & (HG_CHUNK - 1)


def _cumsum_chunk(x, pos):
    sh = 1
    while sh < HG_CHUNK:
        x = x + jnp.where(pos >= sh, pltpu.roll(x, sh, 0), 0.0)
        sh *= 2
    return x


def _rcumsum_chunk(x, pos):
    t = x.shape[0]
    sh = 1
    while sh < HG_CHUNK:
        x = x + jnp.where(pos < HG_CHUNK - sh, pltpu.roll(x, t - sh, 0), 0.0)
        sh *= 2
    return x


def _chunk_total(x):
    t, w = x.shape
    tot = jnp.sum(x.reshape(t // HG_CHUNK, HG_CHUNK, w), axis=1, keepdims=True)
    return jnp.broadcast_to(tot, (t // HG_CHUNK, HG_CHUNK, w)).reshape(t, w)


def _hgrn_gates(hq, hf, lb_logits, pos):
    lb = _sigmoid(lb_logits[0:1, :] - lb_logits[1:2, :])
    sig = _sigmoid(hf)
    f = lb + (1.0 - lb) * sig
    sq = _sigmoid(hq)
    q = hq * sq
    k = 1.0 - f
    logf = jnp.log(f)
    bcum = _cumsum_chunk(logf, pos)
    blast = _chunk_total(logf)
    eb = jnp.exp(bcum)
    enb = jnp.exp(-bcum)
    eo = jnp.exp(blast - bcum)
    return dict(lb=lb, sig=sig, f=f, sq=sq, q=q, k=k, eb=eb, enb=enb, eo=eo,
                qi=q * eb, ki=k * enb, ko=k * eo, dec=jnp.exp(blast))


def _hgrn_fwd(proj, lb_logits, hg_norm_g):
    s = proj.shape[0]
    t = min(s, T_HGRN)
    nb = s // t
    nc = t // HG_CHUNK
    hw = HG_HEADS_PER_STEP * HEAD_DIM

    def body(hq_ref, hf_ref, hi_ref, hz_ref, lb_ref, g_ref, o_ref, ya_ref, st_ref, state):
        b = pl.program_id(1)

        @pl.when(b == 0)
        def _():
            state[...] = jnp.zeros_like(state)

        lower = _chunk_lower_mask(t)
        pos = _chunk_pos(t)
        for hh in range(HG_HEADS_PER_STEP):
            cols = slice(hh * HEAD_DIM, (hh + 1) * HEAD_DIM)
            st = state[hh]
            st_ref[0, hh] = st
            gt = _hgrn_gates(hq_ref[:, cols], hf_ref[:, cols], lb_ref[:, cols], pos)
            vb = _mx(hi_ref[:, cols])
            qib, kib, kob = _mx(gt["qi"]), _mx(gt["ki"]), _mx(gt["ko"])
            a = jnp.where(lower, _dot_nt(qib, kib), 0.0)
            o_intra = _dot(_mx(a), vb)
            outs = []
            for c in range(nc):
                sl = slice(c * HG_CHUNK, (c + 1) * HG_CHUNK)
                outs.append(o_intra[sl] + _dot_nt(qib[sl], _mx(st)))
                st = st * gt["dec"][c * HG_CHUNK:c * HG_CHUNK + 1, :] + _dot_tn(vb[sl], kob[sl])
            state[hh] = st
            o = jnp.concatenate(outs, axis=0)
            o_ref[:, cols] = o
            r = lax.rsqrt(jnp.mean(o * o, axis=-1, keepdims=True) + EPS)
            hz = hz_ref[:, cols]
            ya_ref[:, cols] = _mx((o * r * g_ref[...]) * (hz * _sigmoid(hz)))

    hsteps = HEADS // HG_HEADS_PER_STEP

    def seg(k):
        return pl.BlockSpec((t, hw), lambda h, b, k=k: (b, k * hsteps + h))

    return pl.pallas_call(
        body,
        name="hgrn_fwd",
        grid=(hsteps, nb),
        in_specs=[seg(0), seg(1), seg(2), seg(3),
                  pl.BlockSpec((2, hw), lambda h, b: (0, h)),
                  pl.BlockSpec((1, HEAD_DIM), lambda h, b: (0, 0))],
        out_specs=[
            pl.BlockSpec((t, hw), lambda h, b: (b, h)),
            pl.BlockSpec((t, hw), lambda h, b: (b, h)),
            pl.BlockSpec((1, HG_HEADS_PER_STEP, HEAD_DIM, HEAD_DIM), lambda h, b: (b, h, 0, 0)),
        ],
        out_shape=[
            jax.ShapeDtypeStruct((s, D_MODEL), F32),
            jax.ShapeDtypeStruct((s, D_MODEL), MXU_DTYPE),
            jax.ShapeDtypeStruct((nb, HEADS, HEAD_DIM, HEAD_DIM), F32),
        ],
        scratch_shapes=[pltpu.VMEM((HG_HEADS_PER_STEP, HEAD_DIM, HEAD_DIM), F32)],
        compiler_params=_params(),
    )(proj, proj, proj, proj, lb_logits, hg_norm_g)


def _rope(x, cos, sin_a, sin_b):
    return x * cos + pltpu.roll(x, 96, 1) * sin_a + pltpu.roll(x, 32, 1) * sin_b


def _rope_t(d, cos, sin_a, sin_b):
    return d * cos + pltpu.roll(d * sin_a, 32, 1) + pltpu.roll(d * sin_b, 96, 1)


def _mla_prep(proj, q_a_g, kv_a_g, w_uq_p, w_kn, w_v, cos, sin_a, sin_b):
    s = proj.shape[0]
    tm = min(s, TM_ROW)

    def body(sm_ref, gq_ref, gk_ref, wq_ref, wkn_ref, wv_ref, cos_ref, sa_ref, sb_ref,
             q_ref, k_ref, v_ref, cqn_ref, ckvn_ref):
        small = sm_ref[...]
        cq = small[:, :Q_LORA]
        ckv = small[:, Q_LORA:Q_LORA + KV_LORA]
        krp = small[:, Q_LORA + KV_LORA:Q_LORA + KV_LORA + HEAD_DIM]
        rq = lax.rsqrt(jnp.mean(cq * cq, axis=-1, keepdims=True) + EPS)
        rk = lax.rsqrt(jnp.mean(ckv * ckv, axis=-1, keepdims=True) + EPS)
        cqn = _mx(cq * rq * gq_ref[...])
        ckvn = _mx(ckv * rk * gk_ref[...])
        cqn_ref[...] = cqn
        ckvn_ref[...] = ckvn
        q = _dot(cqn, wq_ref[...]) * Q_PRESCALE
        kn = _dot(ckvn, wkn_ref[...])
        v = _dot(ckvn, wv_ref[...])
        cos_t, sa, sb = cos_ref[...], sa_ref[...], sb_ref[...]
        kpe = _mx(_rope(krp, cos_t, sa, sb))
        ones_col = (lax.broadcasted_iota(jnp.int32, (tm, HEAD_DIM), 1) == 0).astype(MXU_DTYPE)
        for h in range(HEADS):
            lo = h * HEAD_PAD
            v_ref[:, lo:lo + HEAD_DIM] = _mx(v[:, h * HEAD_DIM:(h + 1) * HEAD_DIM])
            v_ref[:, lo + HEAD_DIM:lo + HEAD_PAD] = ones_col
            q_ref[:, lo:lo + HEAD_DIM] = _mx(q[:, lo:lo + HEAD_DIM])
            q_ref[:, lo + HEAD_DIM:lo + HEAD_PAD] = _mx(_rope(q[:, lo + HEAD_DIM:lo + HEAD_PAD], cos_t, sa, sb))
            k_ref[:, lo:lo + HEAD_DIM] = _mx(kn[:, h * HEAD_DIM:(h + 1) * HEAD_DIM])
            k_ref[:, lo + HEAD_DIM:lo + HEAD_PAD] = kpe

    def const(shape):
        return pl.BlockSpec(shape, lambda i: (0, 0))

    def rows(w):
        return pl.BlockSpec((tm, w), lambda i: (i, 0))

    return pl.pallas_call(
        body,
        name="mla_prep",
        grid=(s // tm,),
        in_specs=[
            pl.BlockSpec((tm, SEG), lambda i: (i, SMALL_SEG)),
            const((1, Q_LORA)), const((1, KV_LORA)),
            const((Q_LORA, HEADS * HEAD_PAD)), const((KV_LORA, D_MODEL)), const((KV_LORA, D_MODEL)),
            rows(HEAD_DIM), rows(HEAD_DIM), rows(HEAD_DIM),
        ],
        out_specs=[rows(HEADS * HEAD_PAD)] * 3 + [rows(Q_LORA), rows(KV_LORA)],
        out_shape=[
            jax.ShapeDtypeStruct((s, HEADS * HEAD_PAD), MXU_DTYPE),
            jax.ShapeDtypeStruct((s, HEADS * HEAD_PAD), MXU_DTYPE),
            jax.ShapeDtypeStruct((s, HEADS * HEAD_PAD), MXU_DTYPE),
            jax.ShapeDtypeStruct((s, Q_LORA), MXU_DTYPE),
            jax.ShapeDtypeStruct((s, KV_LORA), MXU_DTYPE),
        ],
        compiler_params=_params(),
    )(proj, q_a_g, kv_a_g, w_uq_p, w_kn, w_v, cos, sin_a, sin_b)


def _diag_mask(t):
    row = lax.broadcasted_iota(jnp.int32, (t, t), 0)
    col = lax.broadcasted_iota(jnp.int32, (t, t), 1)
    return row >= col


def _flash_fwd(q_all, k_all, v_all, proj):
    s = q_all.shape[0]
    t = min(s, T_ATT)
    n = s // t

    ts = t // ATT_SUB

    def body(q_ref, k_ref, v_ref, mz_ref, ao_ref, lse_ref, yb_ref, m_sc, acc_sc):
        qi = pl.program_id(1)
        m_sc[...] = jnp.full_like(m_sc, -jnp.inf)
        acc_sc[...] = jnp.zeros_like(acc_sc)

        def update(r, sc, v_blk):
            rs = slice(r * ts, (r + 1) * ts)
            m_prev = m_sc[rs]
            m_new = jnp.maximum(m_prev, jnp.max(sc, axis=-1, keepdims=True))
            p = jnp.exp2(sc - m_new)
            acc_sc[rs] = jnp.exp2(m_prev - m_new) * acc_sc[rs] + _dot(_mx(p), v_blk)
            m_sc[rs] = m_new

        def below_diagonal(ki, carry):
            rows = pl.ds(pl.multiple_of(ki * t, t), t)
            for r in range(ATT_SUB):
                update(r, _dot_nt(q_ref[r * ts:(r + 1) * ts], k_ref[rows, :]), v_ref[rows, :])
            return carry

        lax.fori_loop(0, qi, below_diagonal, 0)
        base = pl.multiple_of(qi * t, t)
        for r in range(ATT_SUB):
            w = (r + 1) * ts
            rows = pl.ds(base, w)
            row = lax.broadcasted_iota(jnp.int32, (ts, w), 0) + r * ts
            col = lax.broadcasted_iota(jnp.int32, (ts, w), 1)
            sc = _dot_nt(q_ref[r * ts:(r + 1) * ts], k_ref[rows, :])
            update(r, jnp.where(row >= col, sc, -jnp.inf), v_ref[rows, :])

        acc = acc_sc[...]
        l = acc[:, HEAD_DIM:HEAD_DIM + 1]
        ao = acc[:, :HEAD_DIM] / l
        ao_ref[...] = ao
        lse_ref[...] = jnp.broadcast_to(m_sc[...] + jnp.log2(l), (t, HEAD_DIM))
        mz = mz_ref[...]
        yb_ref[...] = _mx(ao * (mz * _sigmoid(mz)))

    q_map = lambda h, qi: (qi, h)
    return pl.pallas_call(
        body,
        name="flash_fwd",
        grid=(HEADS, n),
        in_specs=[
            pl.BlockSpec((t, HEAD_PAD), q_map),
            pl.BlockSpec((s, HEAD_PAD), lambda h, qi: (0, h)),
            pl.BlockSpec((s, HEAD_PAD), lambda h, qi: (0, h)),
            pl.BlockSpec((t, HEAD_DIM), lambda h, qi: (qi, MZ_SEG * HEADS + h)),
        ],
        out_specs=[pl.BlockSpec((t, HEAD_DIM), q_map)] * 3,
        out_shape=[
            jax.ShapeDtypeStruct((s, D_MODEL), F32),
            jax.ShapeDtypeStruct((s, D_MODEL), F32),
            jax.ShapeDtypeStruct((s, D_MODEL), MXU_DTYPE),
        ],
        scratch_shapes=[
            pltpu.VMEM((t, 1), F32),
            pltpu.VMEM((t, HEAD_PAD), F32),
        ],
        compiler_params=_params(),
    )(q_all, k_all, v_all, proj)


def _merge_fwd_bwd(x, target, ya, yb, proj, b_gate, final_g, w_pa, w_pb, w_out):
    s = x.shape[0]
    tm = min(s, TM_ROW)

    def body(x_ref, t_ref, ya_ref, yb_ref, g0_ref, g1_ref, bg_ref, fg_ref, wpa_ref, wpb_ref, wo_ref,
             dx2_ref, dya_ref, dyb_ref, dg0_ref, dg1_ref, mb_ref, dpab_ref, dpbb_ref, dx2b_ref,
             loss_ref, dfg_ref, dbg_ref):
        i = pl.program_id(0)

        @pl.when(i == 0)
        def _():
            loss_ref[...] = jnp.zeros_like(loss_ref)
            dfg_ref[...] = jnp.zeros_like(dfg_ref)
            dbg_ref[...] = jnp.zeros_like(dbg_ref)

        pa = _dot(ya_ref[...], wpa_ref[...])
        pb = _dot(yb_ref[...], wpb_ref[...])
        bg = bg_ref[...]
        g0 = _sigmoid(g0_ref[...] + bg[:, :D_MODEL])
        g1 = _sigmoid(g1_ref[...] + bg[:, D_MODEL:])
        merged = g0 * pa + g1 * pb
        mb = _mx(merged)
        mb_ref[...] = mb
        x2 = x_ref[...] + _dot(mb, wo_ref[...])
        r = lax.rsqrt(jnp.mean(x2 * x2, axis=-1, keepdims=True) + EPS)
        xn = x2 * r
        fg = fg_ref[...]
        diff = xn * fg - t_ref[...]
        loss_ref[...] += 0.5 * jnp.sum(jnp.mean(diff * diff, axis=-1, keepdims=True))
        dy = diff * (1.0 / D_MODEL)
        dfg_ref[...] += _bcast_rows(jnp.sum(dy * xn, axis=0, keepdims=True), 8)
        tt = dy * fg
        dx2 = r * (tt - xn * jnp.mean(tt * xn, axis=-1, keepdims=True))
        dx2_ref[...] = dx2
        dx2b = _mx(dx2)
        dx2b_ref[...] = dx2b
        dmerged = _dot_nt(dx2b, wo_ref[...])
        dpa = _mx(dmerged * g0)
        dpb = _mx(dmerged * g1)
        dpab_ref[...] = dpa
        dpbb_ref[...] = dpb
        dg0 = dmerged * pa * (g0 * (1.0 - g0))
        dg1 = dmerged * pb * (g1 * (1.0 - g1))
        dg0_ref[...] = _mx(dg0)
        dg1_ref[...] = _mx(dg1)
        dbg_ref[:, :D_MODEL] += _bcast_rows(jnp.sum(dg0, axis=0, keepdims=True), 8)
        dbg_ref[:, D_MODEL:] += _bcast_rows(jnp.sum(dg1, axis=0, keepdims=True), 8)
        dya_ref[...] = _dot_nt(dpa, wpa_ref[...])
        dyb_ref[...] = _dot_nt(dpb, wpb_ref[...])

    def rows(w=D_MODEL):
        return pl.BlockSpec((tm, w), lambda i: (i, 0))

    def const(shape):
        return pl.BlockSpec(shape, lambda i: (0, 0))

    f32 = jax.ShapeDtypeStruct((s, D_MODEL), F32)
    b16 = jax.ShapeDtypeStruct((s, D_MODEL), MXU_DTYPE)
    return pl.pallas_call(
        body,
        name="merge_fwd_bwd",
        grid=(s // tm,),
        in_specs=[
            rows(), rows(), rows(), rows(),
            pl.BlockSpec((tm, SEG), lambda i: (i, GL_SEG)),
            pl.BlockSpec((tm, SEG), lambda i: (i, GL_SEG + 1)),
            const((1, 2 * D_MODEL)), const((1, D_MODEL)),
            const((D_MODEL, D_MODEL)), const((D_MODEL, D_MODEL)), const((D_MODEL, D_MODEL)),
        ],
        out_specs=[rows()] * 9 + [const((8, HEAD_DIM)), const((8, D_MODEL)), const((8, 2 * D_MODEL))],
        out_shape=[f32, f32, f32, b16, b16, b16, b16, b16, b16,
                   jax.ShapeDtypeStruct((8, HEAD_DIM), F32),
                   jax.ShapeDtypeStruct((8, D_MODEL), F32),
                   jax.ShapeDtypeStruct((8, 2 * D_MODEL), F32)],
        compiler_params=_params(),
    )(x, target, ya, yb, proj, proj, b_gate, final_g, w_pa, w_pb, w_out)


def _attn_gate_bwd(dyb, ao, proj):
    s = dyb.shape[0]
    tm = min(s, TM_ROW)

    def body(dyb_ref, ao_ref, mz_ref, dao_ref, dmz_ref):
        mz = mz_ref[...]
        sg = _sigmoid(mz)
        d = dyb_ref[...]
        dao_ref[...] = _mx(d * (mz * sg))
        dmz_ref[...] = _mx(d * ao_ref[...] * (sg + mz * sg * (1.0 - sg)))

    rows = pl.BlockSpec((tm, D_MODEL), lambda i: (i, 0))
    b16 = jax.ShapeDtypeStruct((s, D_MODEL), MXU_DTYPE)
    return pl.pallas_call(
        body,
        name="attn_gate_bwd",
        grid=(s // tm,),
        in_specs=[rows, rows, pl.BlockSpec((tm, SEG), lambda i: (i, MZ_SEG))],
        out_specs=[rows, rows],
        out_shape=[b16, b16],
        compiler_params=_params(),
    )(dyb, ao, proj)


def _flash_bwd(q_all, k_all, v_all, dao, ao, lse):
    s = q_all.shape[0]
    t = min(s, T_ATT_BWD)
    n = s // t
    ts = t // ATT_SUB
    pairs =[(ki, qi) for ki in range(n) for qi in range(ki, n)]
    ki_list = jnp.asarray([p[0] for p in pairs], jnp.int32)
    qi_list = jnp.asarray([p[1] for p in pairs], jnp.int32)

    def body(ki_ref, qi_ref, q_ref, k_ref, v_ref, do_ref, ao_ref, lse_ref, dq_ref, dk_ref, dv_ref, dk_acc, dv_acc):
        step = pl.program_id(1)
        ki, qi = ki_ref[step], qi_ref[step]

        @pl.when(qi == ki)
        def _():
            dk_acc[...] = jnp.zeros_like(dk_acc)
            dv_acc[...] = jnp.zeros_like(dv_acc)

        @pl.when(ki == 0)
        def _():
            dq_ref[pl.ds(pl.multiple_of(qi * t, t), t), :] = jnp.zeros((t, HEAD_PAD), F32)

        def pair(masked):
            k = k_ref[...]
            v = v_ref[:, :HEAD_DIM]
            dk_parts, dv_parts = [], []
            for r in range(ATT_SUB):
                rs = slice(r * ts, (r + 1) * ts)
                q = q_ref[rs]
                sc = _dot_nt(q, k)
                if masked:
                    row = lax.broadcasted_iota(jnp.int32, (ts, t), 0) + r * ts
                    col = lax.broadcasted_iota(jnp.int32, (ts, t), 1)
                    sc = jnp.where(row >= col, sc, -jnp.inf)
                p = jnp.exp2(sc - lse_ref[rs, 0:1])
                do = do_ref[rs]
                delta = jnp.sum(do.astype(F32) * ao_ref[rs], axis=-1, keepdims=True)
                dv_parts.append(_dot_tn(_mx(p), do))
                ds = _mx(p * (_dot_nt(do, v) - delta))
                dk_parts.append(_dot_tn(ds, q))
                rows = pl.ds(pl.multiple_of(qi * t + r * ts, ts), ts)
                dq_ref[rows, :] += _dot(ds, k)

            dk_acc[...] += sum(dk_parts[1:], dk_parts[0])
            dv_acc[...] += sum(dv_parts[1:], dv_parts[0])

        @pl.when(qi == ki)
        def _():
            pair(True)

        @pl.when(qi > ki)
        def _():
            pair(False)

        @pl.when(qi == n - 1)
        def _():
            dk_ref[...] = dk_acc[...] * LN2
            dv_ref[...] = dv_acc[...]

    q_map = lambda h, p, ki_ref, qi_ref: (qi_ref[p], h)
    kv_map = lambda h, p, ki_ref, qi_ref: (ki_ref[p], h)
    grid_spec = pltpu.PrefetchScalarGridSpec(
        num_scalar_prefetch=2,
        grid=(HEADS, len(pairs)),
        in_specs=[
            pl.BlockSpec((t, HEAD_PAD), q_map),
            pl.BlockSpec((t, HEAD_PAD), kv_map),
            pl.BlockSpec((t, HEAD_PAD), kv_map),
            pl.BlockSpec((t, HEAD_DIM), q_map),
            pl.BlockSpec((t, HEAD_DIM), q_map),
            pl.BlockSpec((t, HEAD_DIM), q_map),
        ],
        out_specs=[
            pl.BlockSpec((s, HEAD_PAD), lambda h, p, ki_ref, qi_ref: (0, h)),
            pl.BlockSpec((t, HEAD_PAD), kv_map),
            pl.BlockSpec((t, HEAD_DIM), kv_map),
        ],
        scratch_shapes=[pltpu.VMEM((t, HEAD_PAD), F32), pltpu.VMEM((t, HEAD_DIM), F32)],
    )
    return pl.pallas_call(
        body,
        name="flash_bwd",
        grid_spec=grid_spec,
        out_shape=[
            jax.ShapeDtypeStruct((s, HEADS * HEAD_PAD), F32),
            jax.ShapeDtypeStruct((s, HEADS * HEAD_PAD), F32),
            jax.ShapeDtypeStruct((s, D_MODEL), F32),
        ],
        compiler_params=_params(VMEM_LIMIT_BIG),
    )(ki_list, qi_list, q_all, k_all, v_all, dao, ao, lse)


def _mla_prep_bwd(dq_all, dk_all, dv_all, proj, q_a_g, kv_a_g, w_uq_p, w_kn, w_v, cos, sin_a, sin_b):
    s = proj.shape[0]
    tm = min(s, TM_ROW)

    def body(dq_ref, dk_ref, dv_ref, sm_ref, gq_ref, gk_ref, wq_ref, wkn_ref, wv_ref, cos_ref, sa_ref, sb_ref,
             dsm_ref, dqf_ref, dkn_ref, dvb_ref, dgq_ref, dgk_ref):
        i = pl.program_id(0)

        @pl.when(i == 0)
        def _():
            dgq_ref[...] = jnp.zeros_like(dgq_ref)
            dgk_ref[...] = jnp.zeros_like(dgk_ref)

        cos_t, sa, sb = cos_ref[...], sa_ref[...], sb_ref[...]
        dkpe = jnp.zeros((tm, HEAD_DIM), F32)
        for h in range(HEADS):
            lo = h * HEAD_PAD
            dqf_ref[:, lo:lo + HEAD_DIM] = _mx(dq_ref[:, lo:lo + HEAD_DIM] * QK_SCALE)
            dqf_ref[:, lo + HEAD_DIM:lo + HEAD_PAD] = _mx(
                _rope_t(dq_ref[:, lo + HEAD_DIM:lo + HEAD_PAD] * QK_SCALE, cos_t, sa, sb))
            dkn_ref[:, h * HEAD_DIM:(h + 1) * HEAD_DIM] = _mx(dk_ref[:, lo:lo + HEAD_DIM])
            dkpe = dkpe + dk_ref[:, lo + HEAD_DIM:lo + HEAD_PAD]
        dkr = _rope_t(dkpe, cos_t, sa, sb)
        dvb = _mx(dv_ref[...])
        dvb_ref[...] = dvb
        dcqn = _dot_nt(dqf_ref[...], wq_ref[...])
        dckvn = _dot_nt(dkn_ref[...], wkn_ref[...]) + _dot_nt(dvb, wv_ref[...])

        small = sm_ref[...]
        cq = small[:, :Q_LORA]
        ckv = small[:, Q_LORA:Q_LORA + KV_LORA]
        rq = lax.rsqrt(jnp.mean(cq * cq, axis=-1, keepdims=True) + EPS)
        rk = lax.rsqrt(jnp.mean(ckv * ckv, axis=-1, keepdims=True) + EPS)
        cqh = cq * rq
        ckh = ckv * rk
        dgq_ref[...] += _bcast_rows(jnp.sum(dcqn * cqh, axis=0, keepdims=True), 8)
        dgk_ref[...] += _bcast_rows(jnp.sum(dckvn * ckh, axis=0, keepdims=True), 8)
        tq = dcqn * gq_ref[...]
        tk = dckvn * gk_ref[...]
        dcq = rq * (tq - cqh * jnp.mean(tq * cqh, axis=-1, keepdims=True))
        dckv = rk * (tk - ckh * jnp.mean(tk * ckh, axis=-1, keepdims=True))
        dsm_ref[:, :Q_LORA] = _mx(dcq)
        dsm_ref[:, Q_LORA:Q_LORA + KV_LORA] = _mx(dckv)
        dsm_ref[:, Q_LORA + KV_LORA:Q_LORA + KV_LORA + HEAD_DIM] = _mx(dkr)
        dsm_ref[:, Q_LORA + KV_LORA + HEAD_DIM:] = jnp.zeros((tm, SEG - Q_LORA - KV_LORA - HEAD_DIM), MXU_DTYPE)

    def const(shape):
        return pl.BlockSpec(shape, lambda i: (0, 0))

    def rows(w):
        return pl.BlockSpec((tm, w), lambda i: (i, 0))

    return pl.pallas_call(
        body,
        name="mla_prep_bwd",
        grid=(s // tm,),
        in_specs=[
            rows(HEADS * HEAD_PAD), rows(HEADS * HEAD_PAD), rows(D_MODEL),
            pl.BlockSpec((tm, SEG), lambda i: (i, SMALL_SEG)),
            const((1, Q_LORA)), const((1, KV_LORA)),
            const((Q_LORA, HEADS * HEAD_PAD)), const((KV_LORA, D_MODEL)), const((KV_LORA, D_MODEL)),
            rows(HEAD_DIM), rows(HEAD_DIM), rows(HEAD_DIM),
        ],
        out_specs=[rows(SEG), rows(HEADS * HEAD_PAD), rows(D_MODEL), rows(D_MODEL),
                   const((8, Q_LORA)), const((8, KV_LORA))],
        out_shape=[
            jax.ShapeDtypeStruct((s, SEG), MXU_DTYPE),
            jax.ShapeDtypeStruct((s, HEADS * HEAD_PAD), MXU_DTYPE),
            jax.ShapeDtypeStruct((s, D_MODEL), MXU_DTYPE),
            jax.ShapeDtypeStruct((s, D_MODEL), MXU_DTYPE),
            jax.ShapeDtypeStruct((8, Q_LORA), F32),
            jax.ShapeDtypeStruct((8, KV_LORA), F32),
        ],
        compiler_params=_params(),
    )(dq_all, dk_all, dv_all, proj, q_a_g, kv_a_g, w_uq_p, w_kn, w_v, cos, sin_a, sin_b)


def _hgrn_bwd(proj, lb_logits, hg_norm_g, o_all, dya, states):
    s = proj.shape[0]
    t = min(s, T_HGRN)
    nb = s // t
    nc = t // HG_CHUNK

    def body(hq_ref, hf_ref, hi_ref, hz_ref, lb_ref, g_ref, o_ref, dya_ref, st_ref,
             dhq_ref, dhf_ref, dhi_ref, dhz_ref, dlb_ref, dg_ref, dstate):
        h, b = pl.program_id(0), pl.program_id(1)

        @pl.when(b == 0)
        def _():
            dstate[...] = jnp.zeros_like(dstate)
            dlb_ref[...] = jnp.zeros_like(dlb_ref)

        @pl.when((b == 0) & (h == 0))
        def _():
            dg_ref[...] = jnp.zeros_like(dg_ref)

        lower = _chunk_lower_mask(t)
        pos = _chunk_pos(t)
        ghg = g_ref[...]
        for hh in range(HG_HEADS_PER_STEP):
            cols = slice(hh * HEAD_DIM, (hh + 1) * HEAD_DIM)
            hq, hf, hz = hq_ref[:, cols], hf_ref[:, cols], hz_ref[:, cols]
            gt = _hgrn_gates(hq, hf, lb_ref[:, cols], pos)
            vb = _mx(hi_ref[:, cols])
            qi, ki, ko = gt["qi"], gt["ki"], gt["ko"]
            qib, kib, kob = _mx(qi), _mx(ki), _mx(ko)

            o = o_ref[:, cols]
            sz = _sigmoid(hz)
            r = lax.rsqrt(jnp.mean(o * o, axis=-1, keepdims=True) + EPS)
            on = o * r
            dya_t = dya_ref[:, cols]
            don = dya_t * (hz * sz)
            dhz_ref[:, cols] = _mx(dya_t * (on * ghg) * (sz + hz * sz * (1.0 - sz)))
            dg_ref[...] += _bcast_rows(jnp.sum(don * on, axis=0, keepdims=True), 8)
            tt = don * ghg
            do = r * (tt - on * jnp.mean(tt * on, axis=-1, keepdims=True))
            dob = _mx(do)

            sts = [st_ref[0, hh]]
            for c in range(nc - 1):
                sl = slice(c * HG_CHUNK, (c + 1) * HG_CHUNK)
                sts.append(sts[-1] * gt["dec"][c * HG_CHUNK:c * HG_CHUNK + 1, :] + _dot_tn(vb[sl], kob[sl]))

            a = jnp.where(lower, _dot_nt(qib, kib), 0.0)
            da = _mx(jnp.where(lower, _dot_nt(dob, vb), 0.0))
            dqi_intra = _dot(da, kib)
            dki = _dot_tn(da, qib)
            dv_intra = _dot_tn(_mx(a), dob)

            dst = dstate[hh]
            dqi_parts, dko_parts, dv_parts, dd_parts = [None] * nc, [None] * nc, [None] * nc, [None] * nc
            for c in reversed(range(nc)):
                sl = slice(c * HG_CHUNK, (c + 1) * HG_CHUNK)
                dec = gt["dec"][c * HG_CHUNK:c * HG_CHUNK + 1, :]
                dstb = _mx(dst)
                dv_parts[c] = dv_intra[sl] + _dot_nt(kob[sl], dstb)
                dko_parts[c] = _dot(vb[sl], dstb)
                dqi_parts[c] = dqi_intra[sl] + _dot(dob[sl], _mx(sts[c]))
                dd_parts[c] = _bcast_rows(jnp.sum(dst * sts[c], axis=0, keepdims=True) * dec, HG_CHUNK)
                dst = dst * dec + _dot_tn(dob[sl], qib[sl])
            dstate[hh] = dst
            dqi = jnp.concatenate(dqi_parts, axis=0)
            dko = jnp.concatenate(dko_parts, axis=0)
            dv = jnp.concatenate(dv_parts, axis=0)
            dd = jnp.concatenate(dd_parts, axis=0)

            dq = dqi * gt["eb"]
            dk = dki * gt["enb"] + dko * gt["eo"]
            db = dqi * qi - dki * ki - dko * ko
            dlogf = _rcumsum_chunk(db, pos) + _chunk_total(dko * ko) + dd
            df = dlogf / gt["f"] - dk
            lb, sig, sq = gt["lb"], gt["sig"], gt["sq"]
            dhf_ref[:, cols] = _mx(df * (1.0 - lb) * (sig * (1.0 - sig)))
            dhq_ref[:, cols] = _mx(dq * (sq + hq * sq * (1.0 - sq)))
            dhi_ref[:, cols] = _mx(dv)
            dlb = jnp.sum(df * (1.0 - sig), axis=0, keepdims=True) * (lb * (1.0 - lb))
            dlb_ref[:, cols] += jnp.concatenate([dlb, -dlb], axis=0)

    hw = HG_HEADS_PER_STEP * HEAD_DIM
    hsteps = HEADS // HG_HEADS_PER_STEP

    def seg(k):
        return pl.BlockSpec((t, hw), lambda h, b, k=k: (nb - 1 - b, k * hsteps + h))

    blk = pl.BlockSpec((t, hw), lambda h, b: (nb - 1 - b, h))
    b16 = jax.ShapeDtypeStruct((s, D_MODEL), MXU_DTYPE)
    return pl.pallas_call(
        body,
        name="hgrn_bwd",
        grid=(hsteps, nb),
        in_specs=[seg(0), seg(1), seg(2), seg(3),
                  pl.BlockSpec((2, hw), lambda h, b: (0, h)),
                  pl.BlockSpec((1, HEAD_DIM), lambda h, b: (0, 0)),
                  blk, blk,
                  pl.BlockSpec((1, HG_HEADS_PER_STEP, HEAD_DIM, HEAD_DIM), lambda h, b: (nb - 1 - b, h, 0, 0))],
        out_specs=[blk, blk, blk, blk,
                   pl.BlockSpec((2, hw), lambda h, b: (0, h)),
                   pl.BlockSpec((8, HEAD_DIM), lambda h, b: (0, 0))],
        out_shape=[b16, b16, b16, b16,
                   jax.ShapeDtypeStruct((2, D_MODEL), F32),
                   jax.ShapeDtypeStruct((8, HEAD_DIM), F32)],
        scratch_shapes=[pltpu.VMEM((HG_HEADS_PER_STEP, HEAD_DIM, HEAD_DIM), F32)],
        compiler_params=_params(),
    )(proj, proj, proj, proj, lb_logits, hg_norm_g, o_all, dya, states)


def _dh_bwd(dproj, w_in_p, x, dx2, norm_g):
    s = x.shape[0]
    tm = min(s, TM_DH)
    nk = PROJ_W // SEG

    def body(dp_ref, w_ref, x_ref, dx2_ref, g_ref, gx_ref, dng_ref, acc):
        i, k = pl.program_id(0), pl.program_id(1)

        @pl.when((i == 0) & (k == 0))
        def _():
            dng_ref[...] = jnp.zeros_like(dng_ref)

        part = _dot_nt(dp_ref[...], w_ref[...])

        @pl.when(k == 0)
        def _():
            acc[...] = part

        @pl.when(k > 0)
        def _():
            acc[...] += part

        @pl.when(k == nk - 1)
        def _():
            dh = acc[...]
            xf = x_ref[...]
            r = lax.rsqrt(jnp.mean(xf * xf, axis=-1, keepdims=True) + EPS)
            xh = xf * r
            dng_ref[...] += _bcast_rows(jnp.sum(dh * xh, axis=0, keepdims=True), 8)
            tt = dh * g_ref[...]
            gx_ref[...] = dx2_ref[...] + r * (tt - xh * jnp.mean(tt * xh, axis=-1, keepdims=True))

    rows = pl.BlockSpec((tm, D_MODEL), lambda i, k: (i, 0))
    return pl.pallas_call(
        body,
        name="dh_bwd",
        grid=(s // tm, nk),
        in_specs=[
            pl.BlockSpec((tm, SEG), lambda i, k: (i, k)),
            pl.BlockSpec((D_MODEL, SEG), lambda i, k: (0, k)),
            rows, rows,
            pl.BlockSpec((1, D_MODEL), lambda i, k: (0, 0)),
        ],
        out_specs=[rows, pl.BlockSpec((8, D_MODEL), lambda i, k: (0, 0))],
        out_shape=[jax.ShapeDtypeStruct((s, D_MODEL), F32), jax.ShapeDtypeStruct((8, D_MODEL), F32)],
        scratch_shapes=[pltpu.VMEM((tm, D_MODEL), F32)],
        compiler_params=_params(),
    )(dproj, w_in_p, x, dx2, norm_g)


def _matmul_tn(a, b, name):
    s, m = a.shape
    n = b.shape[1]
    ts = min(s, TS_TN)
    tn = min(n, SEG)

    def body(a_ref, b_ref, o_ref):
        k = pl.program_id(1)
        part = _dot_tn(a_ref[...], b_ref[...])

        @pl.when(k == 0)
        def _():
            o_ref[...] = part

        @pl.when(k > 0)
        def _():
            o_ref[...] += part

    return pl.pallas_call(
        body,
        name=name,
        grid=(n // tn, s // ts),
        in_specs=[pl.BlockSpec((ts, m), lambda j, k: (k, 0)), pl.BlockSpec((ts, tn), lambda j, k: (k, j))],
        out_specs=pl.BlockSpec((m, tn), lambda j, k: (0, j)),
        out_shape=jax.ShapeDtypeStruct((m, n), F32),
        compiler_params=_params(),
    )(a, b)


def _rope_tables(s):
    inv = ROPE_THETA ** (-jnp.arange(0, QK_ROPE, 2, dtype=F32) / QK_ROPE)
    ang = jnp.arange(s, dtype=F32)[:, None] * inv[None, :]
    cos, sin = jnp.cos(ang), jnp.sin(ang)
    z32 = jnp.zeros_like(cos)
    z64 = jnp.zeros((s, HEAD_DIM - QK_ROPE), F32)
    cos_t = jnp.concatenate([cos, cos, z64], axis=1)
    sin_a = jnp.concatenate([-sin, z32, z64], axis=1)
    sin_b = jnp.concatenate([z32, sin, z64], axis=1)
    return cos_t, sin_a, sin_b


def _pack_big(w_in, w_uq, w_ukv, w_pa, w_pb, w_out):
    return jnp.concatenate([a.reshape(-1, PACK_COLS) for a in (w_in, w_uq, w_ukv, w_pa, w_pb, w_out)], axis=0)


def _pack_small(norm_g, b_gate, lb_logits, hg_norm_g, q_a_g, kv_a_g, final_norm_g, extra):
    misc = jnp.concatenate([hg_norm_g.reshape(-1), q_a_g.reshape(-1), kv_a_g.reshape(-1), extra.reshape(-1),
                            jnp.zeros((PACK_COLS - HEAD_DIM - Q_LORA - KV_LORA - 1,), F32)])
    return jnp.concatenate([norm_g.reshape(1, -1), b_gate.reshape(2, -1), lb_logits.reshape(2, -1),
                            misc.reshape(1, -1), final_norm_g.reshape(1, -1), jnp.zeros((1, PACK_COLS), F32)], axis=0)


def _unpack(p):
    r0 = 0
    big = []
    for rows, shape in ((ROWS_W_IN, (1, D_MODEL, IN_COLS // N_DEV)), (ROWS_W_UQ, (1, Q_LORA, QK_DIM)),
                        (ROWS_W_UKV, (1, KV_LORA, 2 * HEAD_DIM)), (ROWS_W_PROJ, (1, HEAD_DIM, D_MODEL)),
                        (ROWS_W_PROJ, (1, HEAD_DIM, D_MODEL)), (ROWS_W_PROJ, (1, HEAD_DIM, D_MODEL))):
        big.append(p[r0:r0 + rows].reshape(shape))
        r0 += rows
    sm = p[ROWS_BIG:]
    w_in, w_uq, w_ukv, w_pa, w_pb, w_out = big
    misc = sm[5]
    return dict(
        norm_g=sm[0:1], w_in=w_in, b_gate=sm[1:3].reshape(1, -1), lb_logits=sm[3:5],
        hg_norm_g=misc[None, :HEAD_DIM], q_a_g=misc[None, HEAD_DIM:HEAD_DIM + Q_LORA],
        w_uq=w_uq, kv_a_g=misc[None, HEAD_DIM + Q_LORA:HEAD_DIM + Q_LORA + KV_LORA], w_ukv=w_ukv,
        w_proj_a=w_pa, w_proj_b=w_pb, w_out=w_out, final_norm_g=sm[6],
        extra=misc[HEAD_DIM + Q_LORA + KV_LORA],
    )


def _full_weights(gathered):
    r0 = 0
    w_in = gathered[:, r0:r0 + ROWS_W_IN].reshape(N_DEV, D_MODEL, IN_COLS // N_DEV)
    w_in = w_in.transpose(1, 0, 2).reshape(D_MODEL, IN_COLS)
    pad_at = 4 * SEG + Q_LORA + KV_LORA + QK_ROPE
    w_in_p = jnp.concatenate([w_in[:, :pad_at], jnp.zeros((D_MODEL, PROJ_W - IN_COLS), w_in.dtype),
                              w_in[:, pad_at:]], axis=1)
    r0 += ROWS_W_IN
    w_uq = gathered[:, r0:r0 + ROWS_W_UQ].reshape(N_DEV, Q_LORA, QK_DIM).transpose(1, 0, 2)
    w_uq_p = jnp.concatenate([w_uq, jnp.zeros((Q_LORA, HEADS, HEAD_PAD - QK_DIM), w_uq.dtype)], axis=2)
    w_uq_p = w_uq_p.reshape(Q_LORA, HEADS * HEAD_PAD)
    r0 += ROWS_W_UQ
    w_ukv = gathered[:, r0:r0 + ROWS_W_UKV].reshape(N_DEV, KV_LORA, 2 * HEAD_DIM).transpose(1, 0, 2)
    w_kn = w_ukv[:, :, :HEAD_DIM].reshape(KV_LORA, D_MODEL)
    w_v = w_ukv[:, :, HEAD_DIM:].reshape(KV_LORA, D_MODEL)
    r0 += ROWS_W_UKV
    mats = []
    for _ in range(3):
        mats.append(gathered[:, r0:r0 + ROWS_W_PROJ].reshape(D_MODEL, D_MODEL))
        r0 += ROWS_W_PROJ
    return w_in_p, w_uq_p, w_kn, w_v, mats[0], mats[1], mats[2]


def _grad_slabs(dw_in_p, dw_uq_p, dw_kn, dw_v, dw_pa, dw_pb, dw_out, small):
    pad_at = 4 * SEG + Q_LORA + KV_LORA + QK_ROPE
    dw_in = jnp.concatenate([dw_in_p[:, :pad_at], dw_in_p[:, pad_at + PROJ_W - IN_COLS:]], axis=1)
    dw_in = dw_in.reshape(D_MODEL, N_DEV, IN_COLS // N_DEV).transpose(1, 0, 2).reshape(N_DEV, ROWS_W_IN, PACK_COLS)
    dw_uq = dw_uq_p.reshape(Q_LORA, HEADS, HEAD_PAD)[:, :, :QK_DIM].transpose(1, 0, 2).reshape(N_DEV, ROWS_W_UQ, PACK_COLS)
    dw_ukv = jnp.concatenate([dw_kn.reshape(KV_LORA, HEADS, HEAD_DIM), dw_v.reshape(KV_LORA, HEADS, HEAD_DIM)], axis=2)
    dw_ukv = dw_ukv.transpose(1, 0, 2).reshape(N_DEV, ROWS_W_UKV, PACK_COLS)
    mats = [a.reshape(N_DEV, ROWS_W_PROJ, PACK_COLS) for a in (dw_pa, dw_pb, dw_out)]
    sm = jnp.broadcast_to(small[None], (N_DEV, ROWS_SMALL, PACK_COLS))
    return jnp.concatenate([dw_in, dw_uq, dw_ukv] + mats + [sm], axis=1)


def _local_grads(x, target, norm_g, b_gate, lb_logits, hg_norm_g, q_a_g, kv_a_g, final_g,
                 w_in_p, w_uq_p, w_kn, w_v, w_pa, w_pb, w_out):
    s = x.shape[0]
    cos, sin_a, sin_b = _rope_tables(s)
    proj, h = _inproj(x, norm_g, w_in_p)
    o_all, ya, states = _hgrn_fwd(proj, lb_logits, hg_norm_g)
    q_all, k_all, v_all, cqn, ckvn = _mla_prep(proj, q_a_g, kv_a_g, w_uq_p, w_kn, w_v, cos, sin_a, sin_b)
    ao, lse, yb = _flash_fwd(q_all, k_all, v_all, proj)
    (dx2, dya, dyb, dg0, dg1, merged_b, dpa_b, dpb_b, dx2_b,
     loss_acc, dfg_acc, dbg_acc) = _merge_fwd_bwd(x, target, ya, yb, proj, b_gate, final_g, w_pa, w_pb, w_out)
    dao, dmz = _attn_gate_bwd(dyb, ao, proj)
    dq_all, dk_all, dv_all = _flash_bwd(q_all, k_all, v_all, dao, ao, lse)
    dsmall, dqf_b, dkn_b, dv_b, dgq_acc, dgk_acc = _mla_prep_bwd(
        dq_all, dk_all, dv_all, proj, q_a_g, kv_a_g, w_uq_p, w_kn, w_v, cos, sin_a, sin_b)
    dhq, dhf, dhi, dhz, dlb, dhg_acc = _hgrn_bwd(proj, lb_logits, hg_norm_g, o_all, dya, states)
    segs = [dhq, dhf, dhi, dhz, dsmall, dmz, dg0, dg1]
    dproj = jnp.concatenate(segs, axis=1)
    grad_x, dng_acc = _dh_bwd(dproj, w_in_p, x, dx2, norm_g)
    dw_in_p = jnp.concatenate([_matmul_tn(h, sg, "dw_in_%d" % k) for k, sg in enumerate(segs)], axis=1)
    return dict(
        loss=loss_acc[0, 0], grad_x=grad_x,
        norm_g=dng_acc[0:1], b_gate=dbg_acc[0:1], lb_logits=dlb, hg_norm_g=dhg_acc[0:1],
        q_a_g=dgq_acc[0:1], kv_a_g=dgk_acc[0:1], final_norm_g=dfg_acc[0],
        w_in_p=dw_in_p,
        w_uq_p=_matmul_tn(cqn, dqf_b, "dw_uq"),
        w_kn=_matmul_tn(ckvn, dkn_b, "dw_kn"),
        w_v=_matmul_tn(ckvn, dv_b, "dw_v"),
        w_pa=_matmul_tn(ya, dpa_b, "dw_pa"),
        w_pb=_matmul_tn(yb, dpb_b, "dw_pb"),
        w_out=_matmul_tn(merged_b, dx2_b, "dw_out"),
    )


def kernel(x, norm_g, w_in, b_gate, lb_logits, hg_norm_g, q_a_g, w_uq, kv_a_g, w_ukv, w_proj_a, w_proj_b, w_out, final_norm_g, loss_target, m_norm_g, m_w_in, m_b_gate, m_lb_logits, m_hg_norm_g, m_q_a_g, m_w_uq, m_kv_a_g, m_w_ukv, m_w_proj_a, m_w_proj_b, m_w_out, m_final_norm_g, v_norm_g, v_w_in, v_b_gate, v_lb_logits, v_hg_norm_g, v_q_a_g, v_w_uq, v_kv_a_g, v_w_ukv, v_w_proj_a, v_w_proj_b, v_w_out, v_final_norm_g):
    zero = jnp.zeros((1,), F32)
    shard_b16 = _pack_big(w_in, w_uq, w_ukv, w_proj_a, w_proj_b, w_out).astype(MXU_DTYPE)
    full = _full_weights(_all_gather_packed(shard_b16))
    g = _local_grads(x[0], loss_target[0], norm_g, b_gate, lb_logits, hg_norm_g, q_a_g, kv_a_g,
                     final_norm_g.reshape(1, -1), *full)
    small = _pack_small(g["norm_g"], g["b_gate"], g["lb_logits"], g["hg_norm_g"], g["q_a_g"], g["kv_a_g"],
                        g["final_norm_g"], g["loss"])
    slabs = _grad_slabs(g["w_in_p"], g["w_uq_p"], g["w_kn"], g["w_v"], g["w_pa"], g["w_pb"], g["w_out"], small)
    recv = _reduce_scatter_exchange(slabs)

    def packed(w6, s7):
        return jnp.concatenate([_pack_big(*w6), _pack_small(*s7, zero)], axis=0)

    w_p = packed((w_in, w_uq, w_ukv, w_proj_a, w_proj_b, w_out),
                 (norm_g, b_gate, lb_logits, hg_norm_g, q_a_g, kv_a_g, final_norm_g))
    m_p = packed((m_w_in, m_w_uq, m_w_ukv, m_w_proj_a, m_w_proj_b, m_w_out),
                 (m_norm_g, m_b_gate, m_lb_logits, m_hg_norm_g, m_q_a_g, m_kv_a_g, m_final_norm_g))
    v_p = packed((v_w_in, v_w_uq, v_w_ukv, v_w_proj_a, v_w_proj_b, v_w_out),
                 (v_norm_g, v_b_gate, v_lb_logits, v_hg_norm_g, v_q_a_g, v_kv_a_g, v_final_norm_g))
    g_p, d_p, mn_p, vn_p = _sum_adamw(recv, w_p, m_p, v_p)
    names = ["norm_g", "w_in", "b_gate", "lb_logits", "hg_norm_g", "q_a_g", "w_uq", "kv_a_g", "w_ukv",
             "w_proj_a", "w_proj_b", "w_out", "final_norm_g"]
    grads, deltas, new_m, new_v = _unpack(g_p), _unpack(d_p), _unpack(mn_p), _unpack(vn_p)
    return (grads["extra"], g["grad_x"][None],
            *[grads[n] for n in names], *[deltas[n] for n in names],
            *[new_m[n] for n in names], *[new_v[n] for n in names])
```

```python
import functools

import jax
import jax.numpy as jnp
from jax import lax
from jax.experimental import pallas as pl
from jax.experimental.pallas import tpu as pltpu

D_MODEL = 1024
HEADS = 8
HEAD_DIM = 128
HG_CHUNK = 32
Q_LORA = 384
KV_LORA = 256
QK_ROPE = 64
QK_DIM = 192
ROPE_THETA = 10000.0
EPS = 1e-6
IN_COLS = 7872
ADAM_LR = 0.001
ADAM_B1 = 0.9
ADAM_B2 = 0.999
ADAM_EPS = 1e-08
ADAM_WD = 0.01
ADAM_STEP = 10

N_DEV = 8
SEG = 1024
PROJ_W = 8 * SEG
SMALL_SEG = 4
MZ_SEG = 5
GL_SEG = 6
HEAD_PAD = 256
PACK_COLS = 1024
ROWS_W_UQ = 72
ROWS_W_UKV = 64
ROWS_W_PROJ = 128
ROWS_AG_USED = D_MODEL + 3 * ROWS_W_PROJ + ROWS_W_UQ + ROWS_W_UKV
ROWS_AG = 1552

QK_SCALE = QK_DIM ** -0.5
LOG2E = 1.4426950408889634
LN2 = 0.6931471805599453
Q_PRESCALE = QK_SCALE * LOG2E

MXU_DTYPE = jnp.bfloat16
TRANSPORT_DTYPE = jnp.bfloat16
VMEM_LIMIT = 48 * 1024 * 1024
VMEM_LIMIT_BIG = 60 * 1024 * 1024

TM_INPROJ = 512
T_HGRN = 256
HG_HEADS_PER_STEP = 2
TM_ROW = 256
T_ATT = 1024
T_ATT_BWD = 1024
ATT_SUB = 4
TS_TN = 512
TM_DH = 512
TR_ADAM = 256

F32 = jnp.float32
MESH = pl.DeviceIdType.MESH


def _dot(a, b):
    return jnp.dot(a, b, preferred_element_type=F32)


def _dot_nt(a, b):
    return lax.dot_general(a, b, (((1,), (1,)), ((), ())), preferred_element_type=F32)


def _dot_tn(a, b):
    return lax.dot_general(a, b, (((0,), (0,)), ((), ())), preferred_element_type=F32)


def _mx(a):
    return a.astype(MXU_DTYPE)


def _sigmoid(x):
    return 1.0 / (1.0 + jnp.exp(-x))


def _params(vmem=VMEM_LIMIT, **kw):
    return pltpu.CompilerParams(vmem_limit_bytes=vmem, **kw)


def _bcast_rows(row, n):
    return jnp.broadcast_to(row, (n, row.shape[-1]))


HBM_SPEC = pl.BlockSpec(memory_space=pltpu.HBM)


def _all_gather_packed(shard):
    rows, cols = shard.shape

    def body(x_ref, out_ref, send_sems, recv_sems, local_sem):
        x, y, c = lax.axis_index("x"), lax.axis_index("y"), lax.axis_index("c")
        me, sibling = (x, y, c), (x, y, 1 - c)
        chips = [(1 - x, y), (x, 1 - y), (1 - x, 1 - y)]

        def slot(px, py, pc):
            return out_ref.at[4 * px + 2 * py + pc]

        def copy(k, block, to, src=None):
            return pltpu.make_async_remote_copy(
                src_ref=slot(*block) if src is None else src,
                dst_ref=slot(*block),
                send_sem=send_sems.at[k],
                recv_sem=recv_sems.at[k],
                device_id=to,
                device_id_type=MESH,
            )

        mine = pltpu.make_async_copy(x_ref, slot(*me), local_sem)
        mine.start()
        first = [copy(0, me, sibling, src=x_ref)]
        first += [copy(1 + j, me, (*chip, c), src=x_ref) for j, chip in enumerate(chips)]
        for cp in first:
            cp.start()
        passed = [copy(4 + j, (*chip, c), sibling) for j, chip in enumerate(chips)]
        for j, chip in enumerate(chips):
            copy(1 + j, (*chip, c), me).wait_recv()
            passed[j].start()
        copy(0, sibling, me).wait_recv()
        for j, chip in enumerate(chips):
            copy(4 + j, (*chip, 1 - c), me).wait_recv()
        for cp in first + passed:
            cp.wait_send()
        mine.wait()

    return pl.pallas_call(
        body,
        name="ag_weights",
        out_shape=jax.ShapeDtypeStruct((N_DEV, rows, cols), shard.dtype),
        in_specs=[HBM_SPEC],
        out_specs=HBM_SPEC,
        scratch_shapes=[
            pltpu.SemaphoreType.DMA((7,)),
            pltpu.SemaphoreType.DMA((7,)),
            pltpu.SemaphoreType.DMA,
        ],
    )(shard)


def _reduce_scatter_exchange(slab_sets):
    n_ops = len(slab_sets)

    def body(*refs):
        g_refs, recv_refs = refs[:n_ops], refs[n_ops:2 * n_ops]
        send_sems, recv_sems, local_sems = refs[2 * n_ops:]
        x, y, c = lax.axis_index("x"), lax.axis_index("y"), lax.axis_index("c")
        me = 4 * x + 2 * y + c

        def copy(i, k, landing):
            px, py, pc = x ^ ((k >> 2) & 1), y ^ ((k >> 1) & 1), c ^ (k & 1)
            peer = 4 * px + 2 * py + pc
            return pltpu.make_async_remote_copy(
                src_ref=g_refs[i].at[peer],
                dst_ref=recv_refs[i].at[peer if landing else me],
                send_sem=send_sems.at[i * (N_DEV - 1) + k - 1],
                recv_sem=recv_sems.at[i * (N_DEV - 1) + k - 1],
                device_id=(px, py, pc),
                device_id_type=MESH,
            )

        mine = [pltpu.make_async_copy(g_refs[i].at[me], recv_refs[i].at[me], local_sems.at[i]) for i in range(n_ops)]
        for cp in mine:
            cp.start()
        sends = [copy(i, k, False) for i in range(n_ops) for k in range(1, N_DEV)]
        for cp in sends:
            cp.start()
        for i in range(n_ops):
            for k in range(1, N_DEV):
                copy(i, k, True).wait_recv()
        for cp in sends:
            cp.wait_send()
        for cp in mine:
            cp.wait()

    return pl.pallas_call(
        body,
        name="rs_grads",
        out_shape=[jax.ShapeDtypeStruct(a.shape, a.dtype) for a in slab_sets],
        in_specs=[HBM_SPEC] * n_ops,
        out_specs=[HBM_SPEC] * n_ops,
        scratch_shapes=[
            pltpu.SemaphoreType.DMA((n_ops * (N_DEV - 1),)),
            pltpu.SemaphoreType.DMA((n_ops * (N_DEV - 1),)),
            pltpu.SemaphoreType.DMA((n_ops,)),
        ],
    )(*slab_sets)


def _sum_adamw(recv, w, m, v, name):
    rows, cols = w.shape
    tr = min(rows, TR_ADAM)

    def body(r_ref, w_ref, m_ref, v_ref, g_out, d_out, m_out, v_out):
        g = r_ref[0].astype(F32)
        for i in range(1, N_DEV):
            g = g + r_ref[i].astype(F32)
        g = g[:, :cols]
        m_new = ADAM_B1 * m_ref[...] + (1.0 - ADAM_B1) * g
        v_new = ADAM_B2 * v_ref[...] + (1.0 - ADAM_B2) * (g * g)
        m_hat = m_new / (1.0 - ADAM_B1 ** ADAM_STEP)
        v_hat = v_new / (1.0 - ADAM_B2 ** ADAM_STEP)
        g_out[...] = g
        d_out[...] = -ADAM_LR * (m_hat / (jnp.sqrt(v_hat) + ADAM_EPS) + ADAM_WD * w_ref[...])
        m_out[...] = m_new
        v_out[...] = v_new

    row_spec = pl.BlockSpec((tr, cols), lambda i: (i, 0))
    shape = jax.ShapeDtypeStruct((rows, cols), F32)
    return pl.pallas_call(
        body,
        name=name,
        grid=(rows // tr,),
        in_specs=[pl.BlockSpec((N_DEV, tr, PACK_COLS), lambda i: (0, i, 0)), row_spec, row_spec, row_spec],
        out_specs=[row_spec] * 4,
        out_shape=[shape] * 4,
        compiler_params=_params(),
    )(recv, w, m, v)


def _inproj(x, norm_g, w_in_p):
    s = x.shape[0]
    tm = min(s, TM_INPROJ)

    def body(x_ref, g_ref, w_ref, proj_ref, h_ref):
        xf = x_ref[...]
        r = lax.rsqrt(jnp.mean(xf * xf, axis=-1, keepdims=True) + EPS)
        h = _mx(xf * r * g_ref[...])
        h_ref[...] = h
        proj_ref[...] = _dot(h, w_ref[...])

    return pl.pallas_call(
        body,
        name="inproj",
        grid=(s // tm, PROJ_W // SEG),
        in_specs=[
            pl.BlockSpec((tm, D_MODEL), lambda i, j: (i, 0)),
            pl.BlockSpec((1, D_MODEL), lambda i, j: (0, 0)),
            pl.BlockSpec((D_MODEL, SEG), lambda i, j: (0, j)),
        ],
        out_specs=[
            pl.BlockSpec((tm, SEG), lambda i, j: (i, j)),
            pl.BlockSpec((tm, D_MODEL), lambda i, j: (i, 0)),
        ],
        out_shape=[
            jax.ShapeDtypeStruct((s, PROJ_W), F32),
            jax.ShapeDtypeStruct((s, D_MODEL), MXU_DTYPE),
        ],
        compiler_params=_params(),
    )(x, norm_g, w_in_p)


def _chunk_lower_mask(t):
    row = lax.broadcasted_iota(jnp.int32, (t, t), 0)
    col = lax.broadcasted_iota(jnp.int32, (t, t), 1)
    return ((row // HG_CHUNK) == (col // HG_CHUNK)) & (col <= row)


def _chunk_pos(t):
    return lax.broadcasted_iota(jnp.int32, (t, HEAD_DIM), 0) & (HG_CHUNK - 1)


def _cumsum_chunk(x, pos):
    sh = 1
    while sh < HG_CHUNK:
        x = x + jnp.where(pos >= sh, pltpu.roll(x, sh, 0), 0.0)
        sh *= 2
    return x


def _rcumsum_chunk(x, pos):
    t = x.shape[0]
    sh = 1
    while sh < HG_CHUNK:
        x = x + jnp.where(pos < HG_CHUNK - sh, pltpu.roll(x, t - sh, 0), 0.0)
        sh *= 2
    return x


def _chunk_total(x):
    t, w = x.shape
    tot = jnp.sum(x.reshape(t // HG_CHUNK, HG_CHUNK, w), axis=1, keepdims=True)
    return jnp.broadcast_to(tot, (t // HG_CHUNK, HG_CHUNK, w)).reshape(t, w)


def _hgrn_gates(hq, hf, lb_logits, pos):
    lb = _sigmoid(lb_logits[0:1, :] - lb_logits[1:2, :])
    sig = _sigmoid(hf)
    f = lb + (1.0 - lb) * sig
    sq = _sigmoid(hq)
    q = hq * sq
    k = 1.0 - f
    logf = jnp.log(f)
    bcum = _cumsum_chunk(logf, pos)
    blast = _chunk_total(logf)
    eb = jnp.exp(bcum)
    enb = jnp.exp(-bcum)
    eo = jnp.exp(blast - bcum)
    return dict(lb=lb, sig=sig, f=f, sq=sq, q=q, k=k, eb=eb, enb=enb, eo=eo,
                qi=q * eb, ki=k * enb, ko=k * eo, dec=jnp.exp(blast))


def _hgrn_fwd(proj, lb_logits, hg_norm_g):
    s = proj.shape[0]
    t = min(s, T_HGRN)
    nb = s // t
    nc = t // HG_CHUNK
    hw = HG_HEADS_PER_STEP * HEAD_DIM

    def body(hq_ref, hf_ref, hi_ref, hz_ref, lb_ref, g_ref, o_ref, ya_ref, st_ref, state):
        b = pl.program_id(1)

        @pl.when(b == 0)
        def _():
            state[...] = jnp.zeros_like(state)

        lower = _chunk_lower_mask(t)
        pos = _chunk_pos(t)
        for hh in range(HG_HEADS_PER_STEP):
            cols = slice(hh * HEAD_DIM, (hh + 1) * HEAD_DIM)
            st = state[hh]
            st_ref[0, hh] = st
            gt = _hgrn_gates(hq_ref[:, cols], hf_ref[:, cols], lb_ref[:, cols], pos)
            vb = _mx(hi_ref[:, cols])
            qib, kib, kob = _mx(gt["qi"]), _mx(gt["ki"]), _mx(gt["ko"])
            a = jnp.where(lower, _dot_nt(qib, kib), 0.0)
            o_intra = _dot(_mx(a), vb)
            outs = []
            for c in range(nc):
                sl = slice(c * HG_CHUNK, (c + 1) * HG_CHUNK)
                outs.append(o_intra[sl] + _dot_nt(qib[sl], _mx(st)))
                st = st * gt["dec"][c * HG_CHUNK:c * HG_CHUNK + 1, :] + _dot_tn(vb[sl], kob[sl])
            state[hh] = st
            o = jnp.concatenate(outs, axis=0)
            o_ref[:, cols] = o
            r = lax.rsqrt(jnp.mean(o * o, axis=-1, keepdims=True) + EPS)
            hz = hz_ref[:, cols]
            ya_ref[:, cols] = _mx((o * r * g_ref[...]) * (hz * _sigmoid(hz)))

    hsteps = HEADS // HG_HEADS_PER_STEP

    def seg(k):
        return pl.BlockSpec((t, hw), lambda h, b, k=k: (b, k * hsteps + h))

    return pl.pallas_call(
        body,
        name="hgrn_fwd",
        grid=(hsteps, nb),
        in_specs=[seg(0), seg(1), seg(2), seg(3),
                  pl.BlockSpec((2, hw), lambda h, b: (0, h)),
                  pl.BlockSpec((1, HEAD_DIM), lambda h, b: (0, 0))],
        out_specs=[
            pl.BlockSpec((t, hw), lambda h, b: (b, h)),
            pl.BlockSpec((t, hw), lambda h, b: (b, h)),
            pl.BlockSpec((1, HG_HEADS_PER_STEP, HEAD_DIM, HEAD_DIM), lambda h, b: (b, h, 0, 0)),
        ],
        out_shape=[
            jax.ShapeDtypeStruct((s, D_MODEL), F32),
            jax.ShapeDtypeStruct((s, D_MODEL), MXU_DTYPE),
            jax.ShapeDtypeStruct((nb, HEADS, HEAD_DIM, HEAD_DIM), F32),
        ],
        scratch_shapes=[pltpu.VMEM((HG_HEADS_PER_STEP, HEAD_DIM, HEAD_DIM), F32)],
        compiler_params=_params(),
    )(proj, proj, proj, proj, lb_logits, hg_norm_g)


def _rope(x, cos, sin_a, sin_b):
    return x * cos + pltpu.roll(x, 96, 1) * sin_a + pltpu.roll(x, 32, 1) * sin_b


def _rope_t(d, cos, sin_a, sin_b):
    return d * cos + pltpu.roll(d * sin_a, 32, 1) + pltpu.roll(d * sin_b, 96, 1)


def _mla_prep(proj, q_a_g, kv_a_g, w_uq_p, w_kn, w_v, cos, sin_a, sin_b):
    s = proj.shape[0]
    tm = min(s, TM_ROW)

    def body(sm_ref, gq_ref, gk_ref, wq_ref, wkn_ref, wv_ref, cos_ref, sa_ref, sb_ref,
             q_ref, k_ref, v_ref, cqn_ref, ckvn_ref):
        small = sm_ref[...]
        cq = small[:, :Q_LORA]
        ckv = small[:, Q_LORA:Q_LORA + KV_LORA]
        krp = small[:, Q_LORA + KV_LORA:Q_LORA + KV_LORA + HEAD_DIM]
        rq = lax.rsqrt(jnp.mean(cq * cq, axis=-1, keepdims=True) + EPS)
        rk = lax.rsqrt(jnp.mean(ckv * ckv, axis=-1, keepdims=True) + EPS)
        cqn = _mx(cq * rq * gq_ref[...])
        ckvn = _mx(ckv * rk * gk_ref[...])
        cqn_ref[...] = cqn
        ckvn_ref[...] = ckvn
        q = _dot(cqn, wq_ref[...]) * Q_PRESCALE
        kn = _dot(ckvn, wkn_ref[...])
        v = _dot(ckvn, wv_ref[...])
        cos_t, sa, sb = cos_ref[...], sa_ref[...], sb_ref[...]
        kpe = _mx(_rope(krp, cos_t, sa, sb))
        ones_col = (lax.broadcasted_iota(jnp.int32, (tm, HEAD_DIM), 1) == 0).astype(MXU_DTYPE)
        for h in range(HEADS):
            lo = h * HEAD_PAD
            v_ref[:, lo:lo + HEAD_DIM] = _mx(v[:, h * HEAD_DIM:(h + 1) * HEAD_DIM])
            v_ref[:, lo + HEAD_DIM:lo + HEAD_PAD] = ones_col
            q_ref[:, lo:lo + HEAD_DIM] = _mx(q[:, lo:lo + HEAD_DIM])
            q_ref[:, lo + HEAD_DIM:lo + HEAD_PAD] = _mx(_rope(q[:, lo + HEAD_DIM:lo + HEAD_PAD], cos_t, sa, sb))
            k_ref[:, lo:lo + HEAD_DIM] = _mx(kn[:, h * HEAD_DIM:(h + 1) * HEAD_DIM])
            k_ref[:, lo + HEAD_DIM:lo + HEAD_PAD] = kpe

    def const(shape):
        return pl.BlockSpec(shape, lambda i: (0, 0))

    def rows(w):
        return pl.BlockSpec((tm, w), lambda i: (i, 0))

    return pl.pallas_call(
        body,
        name="mla_prep",
        grid=(s // tm,),
        in_specs=[
            pl.BlockSpec((tm, SEG), lambda i: (i, SMALL_SEG)),
            const((1, Q_LORA)), const((1, KV_LORA)),
            const((Q_LORA, HEADS * HEAD_PAD)), const((KV_LORA, D_MODEL)), const((KV_LORA, D_MODEL)),
            rows(HEAD_DIM), rows(HEAD_DIM), rows(HEAD_DIM),
        ],
        out_specs=[rows(HEADS * HEAD_PAD)] * 3 + [rows(Q_LORA), rows(KV_LORA)],
        out_shape=[
            jax.ShapeDtypeStruct((s, HEADS * HEAD_PAD), MXU_DTYPE),
            jax.ShapeDtypeStruct((s, HEADS * HEAD_PAD), MXU_DTYPE),
            jax.ShapeDtypeStruct((s, HEADS * HEAD_PAD), MXU_DTYPE),
            jax.ShapeDtypeStruct((s, Q_LORA), MXU_DTYPE),
            jax.ShapeDtypeStruct((s, KV_LORA), MXU_DTYPE),
        ],
        compiler_params=_params(),
    )(proj, q_a_g, kv_a_g, w_uq_p, w_kn, w_v, cos, sin_a, sin_b)


def _diag_mask(t):
    row = lax.broadcasted_iota(jnp.int32, (t, t), 0)
    col = lax.broadcasted_iota(jnp.int32, (t, t), 1)
    return row >= col


def _flash_fwd(q_all, k_all, v_all, proj):
    s = q_all.shape[0]
    t = min(s, T_ATT)
    n = s // t

    ts = t // ATT_SUB

    def body(q_ref, k_ref, v_ref, mz_ref, ao_ref, lse_ref, yb_ref, m_sc, acc_sc):
        qi = pl.program_id(1)
        m_sc[...] = jnp.full_like(m_sc, -jnp.inf)
        acc_sc[...] = jnp.zeros_like(acc_sc)

        def update(r, sc, v_blk):
            rs = slice(r * ts, (r + 1) * ts)
            m_prev = m_sc[rs]
            m_new = jnp.maximum(m_prev, jnp.max(sc, axis=-1, keepdims=True))
            p = jnp.exp2(sc - m_new)
            acc_sc[rs] = jnp.exp2(m_prev - m_new) * acc_sc[rs] + _dot(_mx(p), v_blk)
            m_sc[rs] = m_new

        def below_diagonal(ki, carry):
            rows = pl.ds(pl.multiple_of(ki * t, t), t)
            for r in range(ATT_SUB):
                update(r, _dot_nt(q_ref[r * ts:(r + 1) * ts], k_ref[rows, :]), v_ref[rows, :])
            return carry

        lax.fori_loop(0, qi, below_diagonal, 0)
        base = pl.multiple_of(qi * t, t)
        for r in range(ATT_SUB):
            w = (r + 1) * ts
            rows = pl.ds(base, w)
            row = lax.broadcasted_iota(jnp.int32, (ts, w), 0) + r * ts
            col = lax.broadcasted_iota(jnp.int32, (ts, w), 1)
            sc = _dot_nt(q_ref[r * ts:(r + 1) * ts], k_ref[rows, :])
            update(r, jnp.where(row >= col, sc, -jnp.inf), v_ref[rows, :])

        acc = acc_sc[...]
        l = acc[:, HEAD_DIM:HEAD_DIM + 1]
        ao = acc[:, :HEAD_DIM] / l
        ao_ref[...] = ao
        lse_ref[...] = jnp.broadcast_to(m_sc[...] + jnp.log2(l), (t, HEAD_DIM))
        mz = mz_ref[...]
        yb_ref[...] = _mx(ao * (mz * _sigmoid(mz)))

    q_map = lambda h, qi: (qi, h)
    return pl.pallas_call(
        body,
        name="flash_fwd",
        grid=(HEADS, n),
        in_specs=[
            pl.BlockSpec((t, HEAD_PAD), q_map),
            pl.BlockSpec((s, HEAD_PAD), lambda h, qi: (0, h)),
            pl.BlockSpec((s, HEAD_PAD), lambda h, qi: (0, h)),
            pl.BlockSpec((t, HEAD_DIM), lambda h, qi: (qi, MZ_SEG * HEADS + h)),
        ],
        out_specs=[pl.BlockSpec((t, HEAD_DIM), q_map)] * 3,
        out_shape=[
            jax.ShapeDtypeStruct((s, D_MODEL), F32),
            jax.ShapeDtypeStruct((s, D_MODEL), F32),
            jax.ShapeDtypeStruct((s, D_MODEL), MXU_DTYPE),
        ],
        scratch_shapes=[
            pltpu.VMEM((t, 1), F32),
            pltpu.VMEM((t, HEAD_PAD), F32),
        ],
        compiler_params=_params(),
    )(q_all, k_all, v_all, proj)


def _merge_fwd_bwd(x, target, ya, yb, proj, b_gate, final_g, w_pa, w_pb, w_out):
    s = x.shape[0]
    tm = min(s, TM_ROW)

    def body(x_ref, t_ref, ya_ref, yb_ref, g0_ref, g1_ref, bg_ref, fg_ref, wpa_ref, wpb_ref, wo_ref,
             dx2_ref, dya_ref, dyb_ref, dg0_ref, dg1_ref, mb_ref, dpab_ref, dpbb_ref, dx2b_ref,
             loss_ref, dfg_ref, dbg_ref):
        i = pl.program_id(0)

        @pl.when(i == 0)
        def _():
            loss_ref[...] = jnp.zeros_like(loss_ref)
            dfg_ref[...] = jnp.zeros_like(dfg_ref)
            dbg_ref[...] = jnp.zeros_like(dbg_ref)

        pa = _dot(ya_ref[...], wpa_ref[...])
        pb = _dot(yb_ref[...], wpb_ref[...])
        bg = bg_ref[...]
        g0 = _sigmoid(g0_ref[...] + bg[:, :D_MODEL])
        g1 = _sigmoid(g1_ref[...] + bg[:, D_MODEL:])
        merged = g0 * pa + g1 * pb
        mb = _mx(merged)
        mb_ref[...] = mb
        x2 = x_ref[...] + _dot(mb, wo_ref[...])
        r = lax.rsqrt(jnp.mean(x2 * x2, axis=-1, keepdims=True) + EPS)
        xn = x2 * r
        fg = fg_ref[...]
        diff = xn * fg - t_ref[...]
        loss_ref[...] += 0.5 * jnp.sum(jnp.mean(diff * diff, axis=-1, keepdims=True))
        dy = diff * (1.0 / D_MODEL)
        dfg_ref[...] += _bcast_rows(jnp.sum(dy * xn, axis=0, keepdims=True), 8)
        tt = dy * fg
        dx2 = r * (tt - xn * jnp.mean(tt * xn, axis=-1, keepdims=True))
        dx2_ref[...] = dx2
        dx2b = _mx(dx2)
        dx2b_ref[...] = dx2b
        dmerged = _dot_nt(dx2b, wo_ref[...])
        dpa = _mx(dmerged * g0)
        dpb = _mx(dmerged * g1)
        dpab_ref[...] = dpa
        dpbb_ref[...] = dpb
        dg0 = dmerged * pa * (g0 * (1.0 - g0))
        dg1 = dmerged * pb * (g1 * (1.0 - g1))
        dg0_ref[...] = _mx(dg0)
        dg1_ref[...] = _mx(dg1)
        dbg_ref[:, :D_MODEL] += _bcast_rows(jnp.sum(dg0, axis=0, keepdims=True), 8)
        dbg_ref[:, D_MODEL:] += _bcast_rows(jnp.sum(dg1, axis=0, keepdims=True), 8)
        dya_ref[...] = _dot_nt(dpa, wpa_ref[...])
        dyb_ref[...] = _dot_nt(dpb, wpb_ref[...])

    def rows(w=D_MODEL):
        return pl.BlockSpec((tm, w), lambda i: (i, 0))

    def const(shape):
        return pl.BlockSpec(shape, lambda i: (0, 0))

    f32 = jax.ShapeDtypeStruct((s, D_MODEL), F32)
    b16 = jax.ShapeDtypeStruct((s, D_MODEL), MXU_DTYPE)
    return pl.pallas_call(
        body,
        name="merge_fwd_bwd",
        grid=(s // tm,),
        in_specs=[
            rows(), rows(), rows(), rows(),
            pl.BlockSpec((tm, SEG), lambda i: (i, GL_SEG)),
            pl.BlockSpec((tm, SEG), lambda i: (i, GL_SEG + 1)),
            const((1, 2 * D_MODEL)), const((1, D_MODEL)),
            const((D_MODEL, D_MODEL)), const((D_MODEL, D_MODEL)), const((D_MODEL, D_MODEL)),
        ],
        out_specs=[rows()] * 9 + [const((8, HEAD_DIM)), const((8, D_MODEL)), const((8, 2 * D_MODEL))],
        out_shape=[f32, f32, f32, b16, b16, b16, b16, b16, b16,
                   jax.ShapeDtypeStruct((8, HEAD_DIM), F32),
                   jax.ShapeDtypeStruct((8, D_MODEL), F32),
                   jax.ShapeDtypeStruct((8, 2 * D_MODEL), F32)],
        compiler_params=_params(),
    )(x, target, ya, yb, proj, proj, b_gate, final_g, w_pa, w_pb, w_out)


def _attn_gate_bwd(dyb, ao, proj):
    s = dyb.shape[0]
    tm = min(s, TM_ROW)

    def body(dyb_ref, ao_ref, mz_ref, dao_ref, dmz_ref):
        mz = mz_ref[...]
        sg = _sigmoid(mz)
        d = dyb_ref[...]
        dao_ref[...] = _mx(d * (mz * sg))
        dmz_ref[...] = _mx(d * ao_ref[...] * (sg + mz * sg * (1.0 - sg)))

    rows = pl.BlockSpec((tm, D_MODEL), lambda i: (i, 0))
    b16 = jax.ShapeDtypeStruct((s, D_MODEL), MXU_DTYPE)
    return pl.pallas_call(
        body,
        name="attn_gate_bwd",
        grid=(s // tm,),
        in_specs=[rows, rows, pl.BlockSpec((tm, SEG), lambda i: (i, MZ_SEG))],
        out_specs=[rows, rows],
        out_shape=[b16, b16],
        compiler_params=_params(),
    )(dyb, ao, proj)


def _flash_bwd(q_all, k_all, v_all, dao, ao, lse):
    s = q_all.shape[0]
    t = min(s, T_ATT_BWD)
    n = s // t
    ts = t // ATT_SUB
    pairs =[(ki, qi) for ki in range(n) for qi in range(ki, n)]
    ki_list = jnp.asarray([p[0] for p in pairs], jnp.int32)
    qi_list = jnp.asarray([p[1] for p in pairs], jnp.int32)

    def body(ki_ref, qi_ref, q_ref, k_ref, v_ref, do_ref, ao_ref, lse_ref, dq_ref, dk_ref, dv_ref, dk_acc, dv_acc):
        step = pl.program_id(1)
        ki, qi = ki_ref[step], qi_ref[step]

        @pl.when(qi == ki)
        def _():
            dk_acc[...] = jnp.zeros_like(dk_acc)
            dv_acc[...] = jnp.zeros_like(dv_acc)

        @pl.when(ki == 0)
        def _():
            dq_ref[pl.ds(pl.multiple_of(qi * t, t), t), :] = jnp.zeros((t, HEAD_PAD), F32)

        def pair(masked):
            k = k_ref[...]
            v = v_ref[:, :HEAD_DIM]
            dk_parts, dv_parts = [], []
            for r in range(ATT_SUB):
                rs = slice(r * ts, (r + 1) * ts)
                q = q_ref[rs]
                sc = _dot_nt(q, k)
                if masked:
                    row = lax.broadcasted_iota(jnp.int32, (ts, t), 0) + r * ts
                    col = lax.broadcasted_iota(jnp.int32, (ts, t), 1)
                    sc = jnp.where(row >= col, sc, -jnp.inf)
                p = jnp.exp2(sc - lse_ref[rs, 0:1])
                do = do_ref[rs]
                delta = jnp.sum(do.astype(F32) * ao_ref[rs], axis=-1, keepdims=True)
                dv_parts.append(_dot_tn(_mx(p), do))
                ds = _mx(p * (_dot_nt(do, v) - delta))
                dk_parts.append(_dot_tn(ds, q))
                rows = pl.ds(pl.multiple_of(qi * t + r * ts, ts), ts)
                dq_ref[rows, :] += _dot(ds, k)

            dk_acc[...] += sum(dk_parts[1:], dk_parts[0])
            dv_acc[...] += sum(dv_parts[1:], dv_parts[0])

        @pl.when(qi == ki)
        def _():
            pair(True)

        @pl.when(qi > ki)
        def _():
            pair(False)

        @pl.when(qi == n - 1)
        def _():
            dk_ref[...] = dk_acc[...] * LN2
            dv_ref[...] = dv_acc[...]

    q_map = lambda h, p, ki_ref, qi_ref: (qi_ref[p], h)
    kv_map = lambda h, p, ki_ref, qi_ref: (ki_ref[p], h)
    grid_spec = pltpu.PrefetchScalarGridSpec(
        num_scalar_prefetch=2,
        grid=(HEADS, len(pairs)),
        in_specs=[
            pl.BlockSpec((t, HEAD_PAD), q_map),
            pl.BlockSpec((t, HEAD_PAD), kv_map),
            pl.BlockSpec((t, HEAD_PAD), kv_map),
            pl.BlockSpec((t, HEAD_DIM), q_map),
            pl.BlockSpec((t, HEAD_DIM), q_map),
            pl.BlockSpec((t, HEAD_DIM), q_map),
        ],
        out_specs=[
            pl.BlockSpec((s, HEAD_PAD), lambda h, p, ki_ref, qi_ref: (0, h)),
            pl.BlockSpec((t, HEAD_PAD), kv_map),
            pl.BlockSpec((t, HEAD_DIM), kv_map),
        ],
        scratch_shapes=[pltpu.VMEM((t, HEAD_PAD), F32), pltpu.VMEM((t, HEAD_DIM), F32)],
    )
    return pl.pallas_call(
        body,
        name="flash_bwd",
        grid_spec=grid_spec,
        out_shape=[
            jax.ShapeDtypeStruct((s, HEADS * HEAD_PAD), F32),
            jax.ShapeDtypeStruct((s, HEADS * HEAD_PAD), F32),
            jax.ShapeDtypeStruct((s, D_MODEL), F32),
        ],
        compiler_params=_params(VMEM_LIMIT_BIG),
    )(ki_list, qi_list, q_all, k_all, v_all, dao, ao, lse)


def _mla_prep_bwd(dq_all, dk_all, dv_all, proj, q_a_g, kv_a_g, w_uq_p, w_kn, w_v, cos, sin_a, sin_b):
    s = proj.shape[0]
    tm = min(s, TM_ROW)

    def body(dq_ref, dk_ref, dv_ref, sm_ref, gq_ref, gk_ref, wq_ref, wkn_ref, wv_ref, cos_ref, sa_ref, sb_ref,
             dsm_ref, dqf_ref, dkn_ref, dvb_ref, dgq_ref, dgk_ref):
        i = pl.program_id(0)

        @pl.when(i == 0)
        def _():
            dgq_ref[...] = jnp.zeros_like(dgq_ref)
            dgk_ref[...] = jnp.zeros_like(dgk_ref)

        cos_t, sa, sb = cos_ref[...], sa_ref[...], sb_ref[...]
        dkpe = jnp.zeros((tm, HEAD_DIM), F32)
        for h in range(HEADS):
            lo = h * HEAD_PAD
            dqf_ref[:, lo:lo + HEAD_DIM] = _mx(dq_ref[:, lo:lo + HEAD_DIM] * QK_SCALE)
            dqf_ref[:, lo + HEAD_DIM:lo + HEAD_PAD] = _mx(
                _rope_t(dq_ref[:, lo + HEAD_DIM:lo + HEAD_PAD] * QK_SCALE, cos_t, sa, sb))
            dkn_ref[:, h * HEAD_DIM:(h + 1) * HEAD_DIM] = _mx(dk_ref[:, lo:lo + HEAD_DIM])
            dkpe = dkpe + dk_ref[:, lo + HEAD_DIM:lo + HEAD_PAD]
        dkr = _rope_t(dkpe, cos_t, sa, sb)
        dvb = _mx(dv_ref[...])
        dvb_ref[...] = dvb
        dcqn = _dot_nt(dqf_ref[...], wq_ref[...])
        dckvn = _dot_nt(dkn_ref[...], wkn_ref[...]) + _dot_nt(dvb, wv_ref[...])

        small = sm_ref[...]
        cq = small[:, :Q_LORA]
        ckv = small[:, Q_LORA:Q_LORA + KV_LORA]
        rq = lax.rsqrt(jnp.mean(cq * cq, axis=-1, keepdims=True) + EPS)
        rk = lax.rsqrt(jnp.mean(ckv * ckv, axis=-1, keepdims=True) + EPS)
        cqh = cq * rq
        ckh = ckv * rk
        dgq_ref[...] += _bcast_rows(jnp.sum(dcqn * cqh, axis=0, keepdims=True), 8)
        dgk_ref[...] += _bcast_rows(jnp.sum(dckvn * ckh, axis=0, keepdims=True), 8)
        tq = dcqn * gq_ref[...]
        tk = dckvn * gk_ref[...]
        dcq = rq * (tq - cqh * jnp.mean(tq * cqh, axis=-1, keepdims=True))
        dckv = rk * (tk - ckh * jnp.mean(tk * ckh, axis=-1, keepdims=True))
        dsm_ref[:, :Q_LORA] = _mx(dcq)
        dsm_ref[:, Q_LORA:Q_LORA + KV_LORA] = _mx(dckv)
        dsm_ref[:, Q_LORA + KV_LORA:Q_LORA + KV_LORA + HEAD_DIM] = _mx(dkr)
        dsm_ref[:, Q_LORA + KV_LORA + HEAD_DIM:] = jnp.zeros((tm, SEG - Q_LORA - KV_LORA - HEAD_DIM), MXU_DTYPE)

    def const(shape):
        return pl.BlockSpec(shape, lambda i: (0, 0))

    def rows(w):
        return pl.BlockSpec((tm, w), lambda i: (i, 0))

    return pl.pallas_call(
        body,
        name="mla_prep_bwd",
        grid=(s // tm,),
        in_specs=[
            rows(HEADS * HEAD_PAD), rows(HEADS * HEAD_PAD), rows(D_MODEL),
            pl.BlockSpec((tm, SEG), lambda i: (i, SMALL_SEG)),
            const((1, Q_LORA)), const((1, KV_LORA)),
            const((Q_LORA, HEADS * HEAD_PAD)), const((KV_LORA, D_MODEL)), const((KV_LORA, D_MODEL)),
            rows(HEAD_DIM), rows(HEAD_DIM), rows(HEAD_DIM),
        ],
        out_specs=[rows(SEG), rows(HEADS * HEAD_PAD), rows(D_MODEL), rows(D_MODEL),
                   const((8, Q_LORA)), const((8, KV_LORA))],
        out_shape=[
            jax.ShapeDtypeStruct((s, SEG), MXU_DTYPE),
            jax.ShapeDtypeStruct((s, HEADS * HEAD_PAD), MXU_DTYPE),
            jax.ShapeDtypeStruct((s, D_MODEL), MXU_DTYPE),
            jax.ShapeDtypeStruct((s, D_MODEL), MXU_DTYPE),
            jax.ShapeDtypeStruct((8, Q_LORA), F32),
            jax.ShapeDtypeStruct((8, KV_LORA), F32),
        ],
        compiler_params=_params(),
    )(dq_all, dk_all, dv_all, proj, q_a_g, kv_a_g, w_uq_p, w_kn, w_v, cos, sin_a, sin_b)


def _hgrn_bwd(proj, lb_logits, hg_norm_g, o_all, dya, states):
    s = proj.shape[0]
    t = min(s, T_HGRN)
    nb = s // t
    nc = t // HG_CHUNK

    def body(hq_ref, hf_ref, hi_ref, hz_ref, lb_ref, g_ref, o_ref, dya_ref, st_ref,
             dhq_ref, dhf_ref, dhi_ref, dhz_ref, dlb_ref, dg_ref, dstate):
        h, b = pl.program_id(0), pl.program_id(1)

        @pl.when(b == 0)
        def _():
            dstate[...] = jnp.zeros_like(dstate)
            dlb_ref[...] = jnp.zeros_like(dlb_ref)

        @pl.when((b == 0) & (h == 0))
        def _():
            dg_ref[...] = jnp.zeros_like(dg_ref)

        lower = _chunk_lower_mask(t)
        pos = _chunk_pos(t)
        ghg = g_ref[...]
        for hh in range(HG_HEADS_PER_STEP):
            cols = slice(hh * HEAD_DIM, (hh + 1) * HEAD_DIM)
            hq, hf, hz = hq_ref[:, cols], hf_ref[:, cols], hz_ref[:, cols]
            gt = _hgrn_gates(hq, hf, lb_ref[:, cols], pos)
            vb = _mx(hi_ref[:, cols])
            qi, ki, ko = gt["qi"], gt["ki"], gt["ko"]
            qib, kib, kob = _mx(qi), _mx(ki), _mx(ko)

            o = o_ref[:, cols]
            sz = _sigmoid(hz)
            r = lax.rsqrt(jnp.mean(o * o, axis=-1, keepdims=True) + EPS)
            on = o * r
            dya_t = dya_ref[:, cols]
            don = dya_t * (hz * sz)
            dhz_ref[:, cols] = _mx(dya_t * (on * ghg) * (sz + hz * sz * (1.0 - sz)))
            dg_ref[...] += _bcast_rows(jnp.sum(don * on, axis=0, keepdims=True), 8)
            tt = don * ghg
            do = r * (tt - on * jnp.mean(tt * on, axis=-1, keepdims=True))
            dob = _mx(do)

            sts = [st_ref[0, hh]]
            for c in range(nc - 1):
                sl = slice(c * HG_CHUNK, (c + 1) * HG_CHUNK)
                sts.append(sts[-1] * gt["dec"][c * HG_CHUNK:c * HG_CHUNK + 1, :] + _dot_tn(vb[sl], kob[sl]))

            a = jnp.where(lower, _dot_nt(qib, kib), 0.0)
            da = _mx(jnp.where(lower, _dot_nt(dob, vb), 0.0))
            dqi_intra = _dot(da, kib)
            dki = _dot_tn(da, qib)
            dv_intra = _dot_tn(_mx(a), dob)

            dst = dstate[hh]
            dqi_parts, dko_parts, dv_parts, dd_parts = [None] * nc, [None] * nc, [None] * nc, [None] * nc
            for c in reversed(range(nc)):
                sl = slice(c * HG_CHUNK, (c + 1) * HG_CHUNK)
                dec = gt["dec"][c * HG_CHUNK:c * HG_CHUNK + 1, :]
                dstb = _mx(dst)
                dv_parts[c] = dv_intra[sl] + _dot_nt(kob[sl], dstb)
                dko_parts[c] = _dot(vb[sl], dstb)
                dqi_parts[c] = dqi_intra[sl] + _dot(dob[sl], _mx(sts[c]))
                dd_parts[c] = _bcast_rows(jnp.sum(dst * sts[c], axis=0, keepdims=True) * dec, HG_CHUNK)
                dst = dst * dec + _dot_tn(dob[sl], qib[sl])
            dstate[hh] = dst
            dqi = jnp.concatenate(dqi_parts, axis=0)
            dko = jnp.concatenate(dko_parts, axis=0)
            dv = jnp.concatenate(dv_parts, axis=0)
            dd = jnp.concatenate(dd_parts, axis=0)

            dq = dqi * gt["eb"]
            dk = dki * gt["enb"] + dko * gt["eo"]
            db = dqi * qi - dki * ki - dko * ko
            dlogf = _rcumsum_chunk(db, pos) + _chunk_total(dko * ko) + dd
            df = dlogf / gt["f"] - dk
            lb, sig, sq = gt["lb"], gt["sig"], gt["sq"]
            dhf_ref[:, cols] = _mx(df * (1.0 - lb) * (sig * (1.0 - sig)))
            dhq_ref[:, cols] = _mx(dq * (sq + hq * sq * (1.0 - sq)))
            dhi_ref[:, cols] = _mx(dv)
            dlb = jnp.sum(df * (1.0 - sig), axis=0, keepdims=True) * (lb * (1.0 - lb))
            dlb_ref[:, cols] += jnp.concatenate([dlb, -dlb], axis=0)

    hw = HG_HEADS_PER_STEP * HEAD_DIM
    hsteps = HEADS // HG_HEADS_PER_STEP

    def seg(k):
        return pl.BlockSpec((t, hw), lambda h, b, k=k: (nb - 1 - b, k * hsteps + h))

    blk = pl.BlockSpec((t, hw), lambda h, b: (nb - 1 - b, h))
    b16 = jax.ShapeDtypeStruct((s, D_MODEL), MXU_DTYPE)
    return pl.pallas_call(
        body,
        name="hgrn_bwd",
        grid=(hsteps, nb),
        in_specs=[seg(0), seg(1), seg(2), seg(3),
                  pl.BlockSpec((2, hw), lambda h, b: (0, h)),
                  pl.BlockSpec((1, HEAD_DIM), lambda h, b: (0, 0)),
                  blk, blk,
                  pl.BlockSpec((1, HG_HEADS_PER_STEP, HEAD_DIM, HEAD_DIM), lambda h, b: (nb - 1 - b, h, 0, 0))],
        out_specs=[blk, blk, blk, blk,
                   pl.BlockSpec((2, hw), lambda h, b: (0, h)),
                   pl.BlockSpec((8, HEAD_DIM), lambda h, b: (0, 0))],
        out_shape=[b16, b16, b16, b16,
                   jax.ShapeDtypeStruct((2, D_MODEL), F32),
                   jax.ShapeDtypeStruct((8, HEAD_DIM), F32)],
        scratch_shapes=[pltpu.VMEM((HG_HEADS_PER_STEP, HEAD_DIM, HEAD_DIM), F32)],
        compiler_params=_params(),
    )(proj, proj, proj, proj, lb_logits, hg_norm_g, o_all, dya, states)


def _dh_bwd(dproj, w_in_p, x, dx2, norm_g):
    s = x.shape[0]
    tm = min(s, TM_DH)
    nk = PROJ_W // SEG

    def body(dp_ref, w_ref, x_ref, dx2_ref, g_ref, gx_ref, dng_ref, acc):
        i, k = pl.program_id(0), pl.program_id(1)

        @pl.when((i == 0) & (k == 0))
        def _():
            dng_ref[...] = jnp.zeros_like(dng_ref)

        part = _dot_nt(dp_ref[...], w_ref[...])

        @pl.when(k == 0)
        def _():
            acc[...] = part

        @pl.when(k > 0)
        def _():
            acc[...] += part

        @pl.when(k == nk - 1)
        def _():
            dh = acc[...]
            xf = x_ref[...]
            r = lax.rsqrt(jnp.mean(xf * xf, axis=-1, keepdims=True) + EPS)
            xh = xf * r
            dng_ref[...] += _bcast_rows(jnp.sum(dh * xh, axis=0, keepdims=True), 8)
            tt = dh * g_ref[...]
            gx_ref[...] = dx2_ref[...] + r * (tt - xh * jnp.mean(tt * xh, axis=-1, keepdims=True))

    rows = pl.BlockSpec((tm, D_MODEL), lambda i, k: (i, 0))
    return pl.pallas_call(
        body,
        name="dh_bwd",
        grid=(s // tm, nk),
        in_specs=[
            pl.BlockSpec((tm, SEG), lambda i, k: (i, k)),
            pl.BlockSpec((D_MODEL, SEG), lambda i, k: (0, k)),
            rows, rows,
            pl.BlockSpec((1, D_MODEL), lambda i, k: (0, 0)),
        ],
        out_specs=[rows, pl.BlockSpec((8, D_MODEL), lambda i, k: (0, 0))],
        out_shape=[jax.ShapeDtypeStruct((s, D_MODEL), F32), jax.ShapeDtypeStruct((8, D_MODEL), F32)],
        scratch_shapes=[pltpu.VMEM((tm, D_MODEL), F32)],
        compiler_params=_params(),
    )(dproj, w_in_p, x, dx2, norm_g)


def _matmul_tn(a, b, name, out_dtype=F32):
    s, m = a.shape
    n = b.shape[1]
    ts = min(s, TS_TN)
    tn = min(n, SEG)
    nk = s // ts

    def body(a_ref, b_ref, o_ref, acc):
        k = pl.program_id(1)
        part = _dot_tn(a_ref[...], b_ref[...])

        @pl.when(k == 0)
        def _():
            acc[...] = part

        @pl.when(k > 0)
        def _():
            acc[...] += part

        @pl.when(k == nk - 1)
        def _():
            o_ref[...] = acc[...].astype(out_dtype)

    return pl.pallas_call(
        body,
        name=name,
        grid=(n // tn, nk),
        in_specs=[pl.BlockSpec((ts, m), lambda j, k: (k, 0)), pl.BlockSpec((ts, tn), lambda j, k: (k, j))],
        out_specs=pl.BlockSpec((m, tn), lambda j, k: (0, j)),
        out_shape=jax.ShapeDtypeStruct((m, n), out_dtype),
        scratch_shapes=[pltpu.VMEM((m, tn), F32)],
        compiler_params=_params(),
    )(a, b)


def _w_in_pieces():
    per = IN_COLS // N_DEV
    pad_at = SMALL_SEG * SEG + Q_LORA + KV_LORA + QK_ROPE
    pieces = []
    for j in range(N_DEV):
        u0, u1 = j * per, (j + 1) * per
        cuts = [u0] + ([pad_at] if u0 < pad_at < u1 else []) + [u1]
        for a, b in zip(cuts[:-1], cuts[1:]):
            pieces.append((j, a - u0, b - u0, a if a < pad_at else a + PROJ_W - IN_COLS))
    return pad_at, pieces


def _assemble_w_in(gathered):
    tr = TM_ROW
    pad_at, pieces = _w_in_pieces()

    def body(in_ref, out_ref):
        out_ref[:, pad_at:pad_at + PROJ_W - IN_COLS] = jnp.zeros((tr, PROJ_W - IN_COLS), gathered.dtype)
        for j, a, b, p0 in pieces:
            out_ref[:, p0:p0 + b - a] = in_ref[j, :, a:b]

    return pl.pallas_call(
        body,
        name="assemble_w_in",
        grid=(D_MODEL // tr,),
        in_specs=[pl.BlockSpec((N_DEV, tr, PACK_COLS), lambda i: (0, i, 0))],
        out_specs=pl.BlockSpec((tr, PROJ_W), lambda i: (i, 0)),
        out_shape=jax.ShapeDtypeStruct((D_MODEL, PROJ_W), gathered.dtype),
        compiler_params=_params(),
    )(gathered)


def _scatter_dw_in(dw_in_p):
    tr = TM_ROW
    _, pieces = _w_in_pieces()
    per = IN_COLS // N_DEV

    def body(in_ref, out_ref):
        for j in range(N_DEV):
            out_ref[j, :, per:] = jnp.zeros((tr, PACK_COLS - per), TRANSPORT_DTYPE)
        for j, a, b, p0 in pieces:
            out_ref[j, :, a:b] = in_ref[:, p0:p0 + b - a].astype(TRANSPORT_DTYPE)

    return pl.pallas_call(
        body,
        name="scatter_dw_in",
        grid=(D_MODEL // tr,),
        in_specs=[pl.BlockSpec((tr, PROJ_W), lambda i: (i, 0))],
        out_specs=pl.BlockSpec((N_DEV, tr, PACK_COLS), lambda i: (0, i, 0)),
        out_shape=jax.ShapeDtypeStruct((N_DEV, D_MODEL, PACK_COLS), TRANSPORT_DTYPE),
        compiler_params=_params(),
    )(dw_in_p)


def _rope_tables(s):
    inv = ROPE_THETA ** (-jnp.arange(0, QK_ROPE, 2, dtype=F32) / QK_ROPE)
    ang = jnp.arange(s, dtype=F32)[:, None] * inv[None, :]
    cos, sin = jnp.cos(ang), jnp.sin(ang)
    z32 = jnp.zeros_like(cos)
    z64 = jnp.zeros((s, HEAD_DIM - QK_ROPE), F32)
    cos_t = jnp.concatenate([cos, cos, z64], axis=1)
    sin_a = jnp.concatenate([-sin, z32, z64], axis=1)
    sin_b = jnp.concatenate([z32, sin, z64], axis=1)
    return cos_t, sin_a, sin_b


def _pack_misc(w_uq, w_ukv, norm_g, b_gate, lb_logits, hg_norm_g, q_a_g, kv_a_g, final_norm_g, extra):
    misc = jnp.concatenate([hg_norm_g.reshape(-1), q_a_g.reshape(-1), kv_a_g.reshape(-1), extra.reshape(-1),
                            jnp.zeros((PACK_COLS - HEAD_DIM - Q_LORA - KV_LORA - 1,), F32)])
    return jnp.concatenate([w_uq.reshape(ROWS_W_UQ, PACK_COLS), w_ukv.reshape(ROWS_W_UKV, PACK_COLS),
                            norm_g.reshape(1, -1), b_gate.reshape(2, -1), lb_logits.reshape(2, -1),
                            misc.reshape(1, -1), final_norm_g.reshape(1, -1), jnp.zeros((1, PACK_COLS), F32)], axis=0)


def _unpack_misc(p):
    sm = p[ROWS_W_UQ + ROWS_W_UKV:]
    misc = sm[5]
    return dict(
        w_uq=p[:ROWS_W_UQ].reshape(1, Q_LORA, QK_DIM),
        w_ukv=p[ROWS_W_UQ:ROWS_W_UQ + ROWS_W_UKV].reshape(1, KV_LORA, 2 * HEAD_DIM),
        norm_g=sm[0:1], b_gate=sm[1:3].reshape(1, -1), lb_logits=sm[3:5],
        hg_norm_g=misc[None, :HEAD_DIM], q_a_g=misc[None, HEAD_DIM:HEAD_DIM + Q_LORA],
        kv_a_g=misc[None, HEAD_DIM + Q_LORA:HEAD_DIM + Q_LORA + KV_LORA], final_norm_g=sm[6],
        extra=misc[HEAD_DIM + Q_LORA + KV_LORA],
    )


def _weight_shard_buffer(w_in, w_uq, w_ukv, w_pa, w_pb, w_out):
    w_in_pad = jnp.pad(w_in.reshape(D_MODEL, -1), ((0, 0), (0, PACK_COLS - IN_COLS // N_DEV)))
    parts = [w_in_pad] + [a.reshape(-1, PACK_COLS) for a in (w_pa, w_pb, w_out, w_uq, w_ukv)]
    return jnp.concatenate(parts + [jnp.zeros((ROWS_AG - ROWS_AG_USED, PACK_COLS), F32)], axis=0)


def _full_weights(gathered):
    w_in_p = _assemble_w_in(gathered)
    r0 = D_MODEL
    mats = []
    for _ in range(3):
        mats.append(gathered[:, r0:r0 + ROWS_W_PROJ].reshape(D_MODEL, D_MODEL))
        r0 += ROWS_W_PROJ
    w_uq = gathered[:, r0:r0 + ROWS_W_UQ].reshape(N_DEV, Q_LORA, QK_DIM).transpose(1, 0, 2)
    w_uq_p = jnp.concatenate([w_uq, jnp.zeros((Q_LORA, HEADS, HEAD_PAD - QK_DIM), w_uq.dtype)], axis=2)
    w_uq_p = w_uq_p.reshape(Q_LORA, HEADS * HEAD_PAD)
    r0 += ROWS_W_UQ
    w_ukv = gathered[:, r0:r0 + ROWS_W_UKV].reshape(N_DEV, KV_LORA, 2 * HEAD_DIM).transpose(1, 0, 2)
    w_kn = w_ukv[:, :, :HEAD_DIM].reshape(KV_LORA, D_MODEL)
    w_v = w_ukv[:, :, HEAD_DIM:].reshape(KV_LORA, D_MODEL)
    return w_in_p, w_uq_p, w_kn, w_v, mats[0], mats[1], mats[2]


def _grad_slabs(g):
    dw_uq = g["w_uq_p"].reshape(Q_LORA, HEADS, HEAD_PAD)[:, :, :QK_DIM].transpose(1, 0, 2)
    dw_ukv = jnp.concatenate([g["w_kn"].reshape(KV_LORA, HEADS, HEAD_DIM),
                              g["w_v"].reshape(KV_LORA, HEADS, HEAD_DIM)], axis=2).transpose(1, 0, 2)
    misc = [_pack_misc(dw_uq[j], dw_ukv[j], g["norm_g"], g["b_gate"], g["lb_logits"], g["hg_norm_g"],
                       g["q_a_g"], g["kv_a_g"], g["final_norm_g"], g["loss"]) for j in range(N_DEV)]
    mats = [g[n].reshape(N_DEV, ROWS_W_PROJ, PACK_COLS) for n in ("w_pa", "w_pb", "w_out")]
    return [_scatter_dw_in(g["w_in_p"])] + mats + [jnp.stack(misc)]


def _local_grads(x, target, norm_g, b_gate, lb_logits, hg_norm_g, q_a_g, kv_a_g, final_g,
                 w_in_p, w_uq_p, w_kn, w_v, w_pa, w_pb, w_out):
    s = x.shape[0]
    cos, sin_a, sin_b = _rope_tables(s)
    proj, h = _inproj(x, norm_g, w_in_p)
    o_all, ya, states = _hgrn_fwd(proj, lb_logits, hg_norm_g)
    q_all, k_all, v_all, cqn, ckvn = _mla_prep(proj, q_a_g, kv_a_g, w_uq_p, w_kn, w_v, cos, sin_a, sin_b)
    ao, lse, yb = _flash_fwd(q_all, k_all, v_all, proj)
    (dx2, dya, dyb, dg0, dg1, merged_b, dpa_b, dpb_b, dx2_b,
     loss_acc, dfg_acc, dbg_acc) = _merge_fwd_bwd(x, target, ya, yb, proj, b_gate, final_g, w_pa, w_pb, w_out)
    dao, dmz = _attn_gate_bwd(dyb, ao, proj)
    dq_all, dk_all, dv_all = _flash_bwd(q_all, k_all, v_all, dao, ao, lse)
    dsmall, dqf_b, dkn_b, dv_b, dgq_acc, dgk_acc = _mla_prep_bwd(
        dq_all, dk_all, dv_all, proj, q_a_g, kv_a_g, w_uq_p, w_kn, w_v, cos, sin_a, sin_b)
    dhq, dhf, dhi, dhz, dlb, dhg_acc = _hgrn_bwd(proj, lb_logits, hg_norm_g, o_all, dya, states)
    segs = [dhq, dhf, dhi, dhz, dsmall, dmz, dg0, dg1]
    dproj = jnp.concatenate(segs, axis=1)
    grad_x, dng_acc = _dh_bwd(dproj, w_in_p, x, dx2, norm_g)
    return dict(
        loss=loss_acc[0, 0], grad_x=grad_x,
        norm_g=dng_acc[0:1], b_gate=dbg_acc[0:1], lb_logits=dlb, hg_norm_g=dhg_acc[0:1],
        q_a_g=dgq_acc[0:1], kv_a_g=dgk_acc[0:1], final_norm_g=dfg_acc[0],
        w_in_p=_matmul_tn(h, dproj, "dw_in"),
        w_uq_p=_matmul_tn(cqn, dqf_b, "dw_uq"),
        w_kn=_matmul_tn(ckvn, dkn_b, "dw_kn"),
        w_v=_matmul_tn(ckvn, dv_b, "dw_v"),
        w_pa=_matmul_tn(ya, dpa_b, "dw_pa", TRANSPORT_DTYPE),
        w_pb=_matmul_tn(yb, dpb_b, "dw_pb", TRANSPORT_DTYPE),
        w_out=_matmul_tn(merged_b, dx2_b, "dw_out", TRANSPORT_DTYPE),
    )


def kernel(x, norm_g, w_in, b_gate, lb_logits, hg_norm_g, q_a_g, w_uq, kv_a_g, w_ukv, w_proj_a, w_proj_b, w_out, final_norm_g, loss_target, m_norm_g, m_w_in, m_b_gate, m_lb_logits, m_hg_norm_g, m_q_a_g, m_w_uq, m_kv_a_g, m_w_ukv, m_w_proj_a, m_w_proj_b, m_w_out, m_final_norm_g, v_norm_g, v_w_in, v_b_gate, v_lb_logits, v_hg_norm_g, v_q_a_g, v_w_uq, v_kv_a_g, v_w_ukv, v_w_proj_a, v_w_proj_b, v_w_out, v_final_norm_g):
    zero = jnp.zeros((1,), F32)
    shard = _weight_shard_buffer(w_in, w_uq, w_ukv, w_proj_a, w_proj_b, w_out).astype(MXU_DTYPE)
    full = _full_weights(_all_gather_packed(shard))
    g = _local_grads(x[0], loss_target[0], norm_g, b_gate, lb_logits, hg_norm_g, q_a_g, kv_a_g,
                     final_norm_g.reshape(1, -1), *full)
    recv_in, recv_pa, recv_pb, recv_out, recv_misc = _reduce_scatter_exchange(_grad_slabs(g))

    out_in = _sum_adamw(recv_in, w_in[0], m_w_in[0], v_w_in[0], "adamw_w_in")
    out_pa = _sum_adamw(recv_pa, w_proj_a[0], m_w_proj_a[0], v_w_proj_a[0], "adamw_w_pa")
    out_pb = _sum_adamw(recv_pb, w_proj_b[0], m_w_proj_b[0], v_w_proj_b[0], "adamw_w_pb")
    out_out = _sum_adamw(recv_out, w_out[0], m_w_out[0], v_w_out[0], "adamw_w_out")
    out_misc = _sum_adamw(
        recv_misc,
        _pack_misc(w_uq, w_ukv, norm_g, b_gate, lb_logits, hg_norm_g, q_a_g, kv_a_g, final_norm_g, zero),
        _pack_misc(m_w_uq, m_w_ukv, m_norm_g, m_b_gate, m_lb_logits, m_hg_norm_g, m_q_a_g, m_kv_a_g,
                   m_final_norm_g, zero),
        _pack_misc(v_w_uq, v_w_ukv, v_norm_g, v_b_gate, v_lb_logits, v_hg_norm_g, v_q_a_g, v_kv_a_g,
                   v_final_norm_g, zero),
        "adamw_misc")
    names = ["norm_g", "w_in", "b_gate", "lb_logits", "hg_norm_g", "q_a_g", "w_uq", "kv_a_g", "w_ukv",
             "w_proj_a", "w_proj_b", "w_out", "final_norm_g"]
    kinds = []
    for i in range(4):
        d = _unpack_misc(out_misc[i])
        d.update(w_in=out_in[i][None], w_proj_a=out_pa[i][None], w_proj_b=out_pb[i][None], w_out=out_out[i][None])
        kinds.append(d)
    return (kinds[0]["extra"], g["grad_x"][None], *[d[n] for d in kinds for n in names])
```

```python
import functools

import jax
import jax.numpy as jnp
from jax import lax
from jax.experimental import pallas as pl
from jax.experimental.pallas import tpu as pltpu

D_MODEL = 1024
HEADS = 8
HEAD_DIM = 128
HG_CHUNK = 32
Q_LORA = 384
KV_LORA = 256
QK_ROPE = 64
QK_DIM = 192
ROPE_THETA = 10000.0
EPS = 1e-6
IN_COLS = 7872
ADAM_LR = 0.001
ADAM_B1 = 0.9
ADAM_B2 = 0.999
ADAM_EPS = 1e-08
ADAM_WD = 0.01
ADAM_STEP = 10

N_DEV = 8
SEG = 1024
PROJ_W = 8 * SEG
SMALL_SEG = 4
MZ_SEG = 5
GL_SEG = 6
HEAD_PAD = 256
PACK_COLS = 1024
ROWS_W_UQ = 72
ROWS_W_UKV = 64
ROWS_W_PROJ = 128
ROWS_AG_USED = D_MODEL + 3 * ROWS_W_PROJ + ROWS_W_UQ + ROWS_W_UKV
ROWS_AG = 1552

QK_SCALE = QK_DIM ** -0.5
LOG2E = 1.4426950408889634
LN2 = 0.6931471805599453
Q_PRESCALE = QK_SCALE * LOG2E

MXU_DTYPE = jnp.bfloat16
TRANSPORT_DTYPE = jnp.bfloat16
VMEM_LIMIT = 48 * 1024 * 1024
VMEM_LIMIT_BIG = 60 * 1024 * 1024

T_HGRN = 256
HG_HEADS_PER_STEP = 2
TM_ROW = 256
T_ATT = 1024
T_ATT_BWD = 1024
ATT_SUB = 4
TS_TN = 2048
TR_ADAM = 256

F32 = jnp.float32
MESH = pl.DeviceIdType.MESH


def _dot(a, b):
    return jnp.dot(a, b, preferred_element_type=F32)


def _dot_nt(a, b):
    return lax.dot_general(a, b, (((1,), (1,)), ((), ())), preferred_element_type=F32)


def _dot_tn(a, b):
    return lax.dot_general(a, b, (((0,), (0,)), ((), ())), preferred_element_type=F32)


def _mx(a):
    return a.astype(MXU_DTYPE)


def _sigmoid(x):
    return 1.0 / (1.0 + jnp.exp(-x))


def _params(vmem=VMEM_LIMIT, **kw):
    return pltpu.CompilerParams(vmem_limit_bytes=vmem, **kw)


def _bcast_rows(row, n):
    return jnp.broadcast_to(row, (n, row.shape[-1]))


def _resident(shape):
    return pl.BlockSpec(shape, lambda *_: (0, 0), pipeline_mode=pl.Buffered(1))


HBM_SPEC = pl.BlockSpec(memory_space=pltpu.HBM)


def _all_gather_packed(shard):
    rows, cols = shard.shape

    def body(x_ref, out_ref, send_sems, recv_sems, local_sem):
        x, y, c = lax.axis_index("x"), lax.axis_index("y"), lax.axis_index("c")
        me, sibling = (x, y, c), (x, y, 1 - c)
        chips = [(1 - x, y), (x, 1 - y), (1 - x, 1 - y)]

        def slot(px, py, pc):
            return out_ref.at[4 * px + 2 * py + pc]

        def copy(k, block, to, src=None):
            return pltpu.make_async_remote_copy(
                src_ref=slot(*block) if src is None else src,
                dst_ref=slot(*block),
                send_sem=send_sems.at[k],
                recv_sem=recv_sems.at[k],
                device_id=to,
                device_id_type=MESH,
            )

        mine = pltpu.make_async_copy(x_ref, slot(*me), local_sem)
        mine.start()
        first = [copy(0, me, sibling, src=x_ref)]
        first += [copy(1 + j, me, (*chip, c), src=x_ref) for j, chip in enumerate(chips)]
        for cp in first:
            cp.start()
        passed = [copy(4 + j, (*chip, c), sibling) for j, chip in enumerate(chips)]
        for j, chip in enumerate(chips):
            copy(1 + j, (*chip, c), me).wait_recv()
            passed[j].start()
        copy(0, sibling, me).wait_recv()
        for j, chip in enumerate(chips):
            copy(4 + j, (*chip, 1 - c), me).wait_recv()
        for cp in first + passed:
            cp.wait_send()
        mine.wait()

    return pl.pallas_call(
        body,
        name="ag_weights",
        out_shape=jax.ShapeDtypeStruct((N_DEV, rows, cols), shard.dtype),
        in_specs=[HBM_SPEC],
        out_specs=HBM_SPEC,
        scratch_shapes=[
            pltpu.SemaphoreType.DMA((7,)),
            pltpu.SemaphoreType.DMA((7,)),
            pltpu.SemaphoreType.DMA,
        ],
    )(shard)


def _reduce_scatter_exchange(slab_sets):
    n_ops = len(slab_sets)

    def body(*refs):
        g_refs, recv_refs = refs[:n_ops], refs[n_ops:2 * n_ops]
        send_sems, recv_sems, local_sems = refs[2 * n_ops:]
        x, y, c = lax.axis_index("x"), lax.axis_index("y"), lax.axis_index("c")
        me = 4 * x + 2 * y + c

        def copy(i, k, landing):
            px, py, pc = x ^ ((k >> 2) & 1), y ^ ((k >> 1) & 1), c ^ (k & 1)
            peer = 4 * px + 2 * py + pc
            return pltpu.make_async_remote_copy(
                src_ref=g_refs[i].at[peer],
                dst_ref=recv_refs[i].at[peer if landing else me],
                send_sem=send_sems.at[i * (N_DEV - 1) + k - 1],
                recv_sem=recv_sems.at[i * (N_DEV - 1) + k - 1],
                device_id=(px, py, pc),
                device_id_type=MESH,
            )

        mine = [pltpu.make_async_copy(g_refs[i].at[me], recv_refs[i].at[me], local_sems.at[i]) for i in range(n_ops)]
        for cp in mine:
            cp.start()
        sends = [copy(i, k, False) for i in range(n_ops) for k in range(1, N_DEV)]
        for cp in sends:
            cp.start()
        for i in range(n_ops):
            for k in range(1, N_DEV):
                copy(i, k, True).wait_recv()
        for cp in sends:
            cp.wait_send()
        for cp in mine:
            cp.wait()

    return pl.pallas_call(
        body,
        name="rs_grads",
        out_shape=[jax.ShapeDtypeStruct(a.shape, a.dtype) for a in slab_sets],
        in_specs=[HBM_SPEC] * n_ops,
        out_specs=[HBM_SPEC] * n_ops,
        scratch_shapes=[
            pltpu.SemaphoreType.DMA((n_ops * (N_DEV - 1),)),
            pltpu.SemaphoreType.DMA((n_ops * (N_DEV - 1),)),
            pltpu.SemaphoreType.DMA((n_ops,)),
        ],
    )(*slab_sets)


def _sum_adamw(recv, w, m, v, name):
    rows, cols = w.shape
    tr = min(rows, TR_ADAM)

    def body(r_ref, w_ref, m_ref, v_ref, g_out, d_out, m_out, v_out):
        g = r_ref[0].astype(F32)
        for i in range(1, N_DEV):
            g = g + r_ref[i].astype(F32)
        g = g[:, :cols]
        m_new = ADAM_B1 * m_ref[...] + (1.0 - ADAM_B1) * g
        v_new = ADAM_B2 * v_ref[...] + (1.0 - ADAM_B2) * (g * g)
        m_hat = m_new / (1.0 - ADAM_B1 ** ADAM_STEP)
        v_hat = v_new / (1.0 - ADAM_B2 ** ADAM_STEP)
        g_out[...] = g
        d_out[...] = -ADAM_LR * (m_hat / (jnp.sqrt(v_hat) + ADAM_EPS) + ADAM_WD * w_ref[...])
        m_out[...] = m_new
        v_out[...] = v_new

    row_spec = pl.BlockSpec((tr, cols), lambda i: (i, 0))
    shape = jax.ShapeDtypeStruct((rows, cols), F32)
    return pl.pallas_call(
        body,
        name=name,
        grid=(rows // tr,),
        in_specs=[pl.BlockSpec((N_DEV, tr, PACK_COLS), lambda i: (0, i, 0)), row_spec, row_spec, row_spec],
        out_specs=[row_spec] * 4,
        out_shape=[shape] * 4,
        compiler_params=_params(),
    )(recv, w, m, v)


def _inproj(x, norm_g, w_in_p):
    s = x.shape[0]
    tm = min(s, TM_ROW)

    def body(x_ref, g_ref, w_ref, proj_ref, h_ref):
        xf = x_ref[...]
        r = lax.rsqrt(jnp.mean(xf * xf, axis=-1, keepdims=True) + EPS)
        h = _mx(xf * r * g_ref[...])
        h_ref[...] = h
        for j in range(PROJ_W // SEG):
            cols = slice(j * SEG, (j + 1) * SEG)
            proj_ref[:, cols] = _dot(h, w_ref[:, cols])

    return pl.pallas_call(
        body,
        name="inproj",
        grid=(s // tm,),
        in_specs=[
            pl.BlockSpec((tm, D_MODEL), lambda i: (i, 0)),
            pl.BlockSpec((1, D_MODEL), lambda i: (0, 0)),
            _resident((D_MODEL, PROJ_W)),
        ],
        out_specs=[
            pl.BlockSpec((tm, PROJ_W), lambda i: (i, 0)),
            pl.BlockSpec((tm, D_MODEL), lambda i: (i, 0)),
        ],
        out_shape=[
            jax.ShapeDtypeStruct((s, PROJ_W), F32),
            jax.ShapeDtypeStruct((s, D_MODEL), MXU_DTYPE),
        ],
        compiler_params=_params(),
    )(x, norm_g, w_in_p)


def _chunk_lower_mask(t):
    row = lax.broadcasted_iota(jnp.int32, (t, t), 0)
    col = lax.broadcasted_iota(jnp.int32, (t, t), 1)
    return ((row // HG_CHUNK) == (col // HG_CHUNK)) & (col <= row)


def _chunk_pos(t):
    return lax.broadcasted_iota(jnp.int32, (t, HEAD_DIM), 0) & (HG_CHUNK - 1)


def _cumsum_chunk(x, pos):
    sh = 1
    while sh < HG_CHUNK:
        x = x + jnp.where(pos >= sh, pltpu.roll(x, sh, 0), 0.0)
        sh *= 2
    return x


def _rcumsum_chunk(x, pos):
    t = x.shape[0]
    sh = 1
    while sh < HG_CHUNK:
        x = x + jnp.where(pos < HG_CHUNK - sh, pltpu.roll(x, t - sh, 0), 0.0)
        sh *= 2
    return x


def _chunk_total(x):
    t, w = x.shape
    tot = jnp.sum(x.reshape(t // HG_CHUNK, HG_CHUNK, w), axis=1, keepdims=True)
    return jnp.broadcast_to(tot, (t // HG_CHUNK, HG_CHUNK, w)).reshape(t, w)


def _hgrn_gates(hq, hf, lb_logits, pos):
    lb = _sigmoid(lb_logits[0:1, :] - lb_logits[1:2, :])
    sig = _sigmoid(hf)
    f = lb + (1.0 - lb) * sig
    sq = _sigmoid(hq)
    q = hq * sq
    k = 1.0 - f
    logf = jnp.log(f)
    bcum = _cumsum_chunk(logf, pos)
    blast = _chunk_total(logf)
    eb = jnp.exp(bcum)
    enb = jnp.exp(-bcum)
    eo = jnp.exp(blast - bcum)
    return dict(lb=lb, sig=sig, f=f, sq=sq, q=q, k=k, eb=eb, enb=enb, eo=eo,
                qi=q * eb, ki=k * enb, ko=k * eo, dec=jnp.exp(blast))


def _hgrn_fwd(proj, lb_logits, hg_norm_g):
    s = proj.shape[0]
    t = min(s, T_HGRN)
    nb = s // t
    nc = t // HG_CHUNK
    hw = HG_HEADS_PER_STEP * HEAD_DIM

    def body(hq_ref, hf_ref, hi_ref, hz_ref, lb_ref, g_ref, o_ref, ya_ref, st_ref, state):
        b = pl.program_id(1)

        @pl.when(b == 0)
        def _():
            state[...] = jnp.zeros_like(state)

        lower = _chunk_lower_mask(t)
        pos = _chunk_pos(t)
        for hh in range(HG_HEADS_PER_STEP):
            cols = slice(hh * HEAD_DIM, (hh + 1) * HEAD_DIM)
            st = state[hh]
            st_ref[0, hh] = st
            gt = _hgrn_gates(hq_ref[:, cols], hf_ref[:, cols], lb_ref[:, cols], pos)
            vb = _mx(hi_ref[:, cols])
            qib, kib, kob = _mx(gt["qi"]), _mx(gt["ki"]), _mx(gt["ko"])
            a = jnp.where(lower, _dot_nt(qib, kib), 0.0)
            o_intra = _dot(_mx(a), vb)
            outs = []
            for c in range(nc):
                sl = slice(c * HG_CHUNK, (c + 1) * HG_CHUNK)
                outs.append(o_intra[sl] + _dot_nt(qib[sl], _mx(st)))
                st = st * gt["dec"][c * HG_CHUNK:c * HG_CHUNK + 1, :] + _dot_tn(vb[sl], kob[sl])
            state[hh] = st
            o = jnp.concatenate(outs, axis=0)
            o_ref[:, cols] = o
            r = lax.rsqrt(jnp.mean(o * o, axis=-1, keepdims=True) + EPS)
            hz = hz_ref[:, cols]
            ya_ref[:, cols] = _mx((o * r * g_ref[...]) * (hz * _sigmoid(hz)))

    hsteps = HEADS // HG_HEADS_PER_STEP

    def seg(k):
        return pl.BlockSpec((t, hw), lambda h, b, k=k: (b, k * hsteps + h))

    return pl.pallas_call(
        body,
        name="hgrn_fwd",
        grid=(hsteps, nb),
        in_specs=[seg(0), seg(1), seg(2), seg(3),
                  pl.BlockSpec((2, hw), lambda h, b: (0, h)),
                  pl.BlockSpec((1, HEAD_DIM), lambda h, b: (0, 0))],
        out_specs=[
            pl.BlockSpec((t, hw), lambda h, b: (b, h)),
            pl.BlockSpec((t, hw), lambda h, b: (b, h)),
            pl.BlockSpec((1, HG_HEADS_PER_STEP, HEAD_DIM, HEAD_DIM), lambda h, b: (b, h, 0, 0)),
        ],
        out_shape=[
            jax.ShapeDtypeStruct((s, D_MODEL), F32),
            jax.ShapeDtypeStruct((s, D_MODEL), MXU_DTYPE),
            jax.ShapeDtypeStruct((nb, HEADS, HEAD_DIM, HEAD_DIM), F32),
        ],
        scratch_shapes=[pltpu.VMEM((HG_HEADS_PER_STEP, HEAD_DIM, HEAD_DIM), F32)],
        compiler_params=_params(),
    )(proj, proj, proj, proj, lb_logits, hg_norm_g)


def _rope(x, cos, sin_a, sin_b):
    return x * cos + pltpu.roll(x, 96, 1) * sin_a + pltpu.roll(x, 32, 1) * sin_b


def _rope_t(d, cos, sin_a, sin_b):
    return d * cos + pltpu.roll(d * sin_a, 32, 1) + pltpu.roll(d * sin_b, 96, 1)


def _mla_prep(proj, q_a_g, kv_a_g, w_uq_p, w_kn, w_v, cos, sin_a, sin_b):
    s = proj.shape[0]
    tm = min(s, TM_ROW)

    def body(sm_ref, gq_ref, gk_ref, wq_ref, wkn_ref, wv_ref, cos_ref, sa_ref, sb_ref,
             q_ref, k_ref, v_ref, cqn_ref, ckvn_ref):
        small = sm_ref[...]
        cq = small[:, :Q_LORA]
        ckv = small[:, Q_LORA:Q_LORA + KV_LORA]
        krp = small[:, Q_LORA + KV_LORA:Q_LORA + KV_LORA + HEAD_DIM]
        rq = lax.rsqrt(jnp.mean(cq * cq, axis=-1, keepdims=True) + EPS)
        rk = lax.rsqrt(jnp.mean(ckv * ckv, axis=-1, keepdims=True) + EPS)
        cqn = _mx(cq * rq * gq_ref[...])
        ckvn = _mx(ckv * rk * gk_ref[...])
        cqn_ref[...] = cqn
        ckvn_ref[...] = ckvn
        q = _dot(cqn, wq_ref[...]) * Q_PRESCALE
        kn = _dot(ckvn, wkn_ref[...])
        v = _dot(ckvn, wv_ref[...])
        cos_t, sa, sb = cos_ref[...], sa_ref[...], sb_ref[...]
        kpe = _mx(_rope(krp, cos_t, sa, sb))
        ones_col = (lax.broadcasted_iota(jnp.int32, (tm, HEAD_DIM), 1) == 0).astype(MXU_DTYPE)
        for h in range(HEADS):
            lo = h * HEAD_PAD
            v_ref[:, lo:lo + HEAD_DIM] = _mx(v[:, h * HEAD_DIM:(h + 1) * HEAD_DIM])
            v_ref[:, lo + HEAD_DIM:lo + HEAD_PAD] = ones_col
            q_ref[:, lo:lo + HEAD_DIM] = _mx(q[:, lo:lo + HEAD_DIM])
            q_ref[:, lo + HEAD_DIM:lo + HEAD_PAD] = _mx(_rope(q[:, lo + HEAD_DIM:lo + HEAD_PAD], cos_t, sa, sb))
            k_ref[:, lo:lo + HEAD_DIM] = _mx(kn[:, h * HEAD_DIM:(h + 1) * HEAD_DIM])
            k_ref[:, lo + HEAD_DIM:lo + HEAD_PAD] = kpe

    def const(shape):
        return pl.BlockSpec(shape, lambda i: (0, 0))

    def rows(w):
        return pl.BlockSpec((tm, w), lambda i: (i, 0))

    return pl.pallas_call(
        body,
        name="mla_prep",
        grid=(s // tm,),
        in_specs=[
            pl.BlockSpec((tm, SEG), lambda i: (i, SMALL_SEG)),
            const((1, Q_LORA)), const((1, KV_LORA)),
            const((Q_LORA, HEADS * HEAD_PAD)), const((KV_LORA, D_MODEL)), const((KV_LORA, D_MODEL)),
            rows(HEAD_DIM), rows(HEAD_DIM), rows(HEAD_DIM),
        ],
        out_specs=[rows(HEADS * HEAD_PAD)] * 3 + [rows(Q_LORA), rows(KV_LORA)],
        out_shape=[
            jax.ShapeDtypeStruct((s, HEADS * HEAD_PAD), MXU_DTYPE),
            jax.ShapeDtypeStruct((s, HEADS * HEAD_PAD), MXU_DTYPE),
            jax.ShapeDtypeStruct((s, HEADS * HEAD_PAD), MXU_DTYPE),
            jax.ShapeDtypeStruct((s, Q_LORA), MXU_DTYPE),
            jax.ShapeDtypeStruct((s, KV_LORA), MXU_DTYPE),
        ],
        compiler_params=_params(),
    )(proj, q_a_g, kv_a_g, w_uq_p, w_kn, w_v, cos, sin_a, sin_b)


def _diag_mask(t):
    row = lax.broadcasted_iota(jnp.int32, (t, t), 0)
    col = lax.broadcasted_iota(jnp.int32, (t, t), 1)
    return row >= col


def _flash_fwd(q_all, k_all, v_all, proj):
    s = q_all.shape[0]
    t = min(s, T_ATT)
    n = s // t

    ts = t // ATT_SUB

    def body(q_ref, k_ref, v_ref, mz_ref, ao_ref, lse_ref, yb_ref, m_sc, acc_sc):
        qi = pl.program_id(1)
        m_sc[...] = jnp.full_like(m_sc, -jnp.inf)
        acc_sc[...] = jnp.zeros_like(acc_sc)

        def update(r, sc, v_blk):
            rs = slice(r * ts, (r + 1) * ts)
            m_prev = m_sc[rs]
            m_new = jnp.maximum(m_prev, jnp.max(sc, axis=-1, keepdims=True))
            p = jnp.exp2(sc - m_new)
            acc_sc[rs] = jnp.exp2(m_prev - m_new) * acc_sc[rs] + _dot(_mx(p), v_blk)
            m_sc[rs] = m_new

        def below_diagonal(ki, carry):
            rows = pl.ds(pl.multiple_of(ki * t, t), t)
            for r in range(ATT_SUB):
                update(r, _dot_nt(q_ref[r * ts:(r + 1) * ts], k_ref[rows, :]), v_ref[rows, :])
            return carry

        lax.fori_loop(0, qi, below_diagonal, 0)
        base = pl.multiple_of(qi * t, t)
        for r in range(ATT_SUB):
            w = (r + 1) * ts
            rows = pl.ds(base, w)
            row = lax.broadcasted_iota(jnp.int32, (ts, w), 0) + r * ts
            col = lax.broadcasted_iota(jnp.int32, (ts, w), 1)
            sc = _dot_nt(q_ref[r * ts:(r + 1) * ts], k_ref[rows, :])
            update(r, jnp.where(row >= col, sc, -jnp.inf), v_ref[rows, :])

        acc = acc_sc[...]
        l = acc[:, HEAD_DIM:HEAD_DIM + 1]
        ao = acc[:, :HEAD_DIM] / l
        ao_ref[...] = ao
        lse_ref[...] = jnp.broadcast_to(m_sc[...] + jnp.log2(l), (t, HEAD_DIM))
        mz = mz_ref[...]
        yb_ref[...] = _mx(ao * (mz * _sigmoid(mz)))

    q_map = lambda h, qi: (qi, h)
    return pl.pallas_call(
        body,
        name="flash_fwd",
        grid=(HEADS, n),
        in_specs=[
            pl.BlockSpec((t, HEAD_PAD), q_map),
            pl.BlockSpec((s, HEAD_PAD), lambda h, qi: (0, h)),
            pl.BlockSpec((s, HEAD_PAD), lambda h, qi: (0, h)),
            pl.BlockSpec((t, HEAD_DIM), lambda h, qi: (qi, MZ_SEG * HEADS + h)),
        ],
        out_specs=[pl.BlockSpec((t, HEAD_DIM), q_map)] * 3,
        out_shape=[
            jax.ShapeDtypeStruct((s, D_MODEL), F32),
            jax.ShapeDtypeStruct((s, D_MODEL), F32),
            jax.ShapeDtypeStruct((s, D_MODEL), MXU_DTYPE),
        ],
        scratch_shapes=[
            pltpu.VMEM((t, 1), F32),
            pltpu.VMEM((t, HEAD_PAD), F32),
        ],
        compiler_params=_params(),
    )(q_all, k_all, v_all, proj)


def _merge_fwd_bwd(x, target, ya, yb, proj, b_gate, final_g, w_pa, w_pb, w_out):
    s = x.shape[0]
    tm = min(s, TM_ROW)

    def body(x_ref, t_ref, ya_ref, yb_ref, g0_ref, g1_ref, bg_ref, fg_ref, wpa_ref, wpb_ref, wo_ref,
             dx2_ref, dya_ref, dyb_ref, dg0_ref, dg1_ref, mb_ref, dpab_ref, dpbb_ref, dx2b_ref,
             loss_ref, dfg_ref, dbg_ref):
        i = pl.program_id(0)

        @pl.when(i == 0)
        def _():
            loss_ref[...] = jnp.zeros_like(loss_ref)
            dfg_ref[...] = jnp.zeros_like(dfg_ref)
            dbg_ref[...] = jnp.zeros_like(dbg_ref)

        pa = _dot(ya_ref[...], wpa_ref[...])
        pb = _dot(yb_ref[...], wpb_ref[...])
        bg = bg_ref[...]
        g0 = _sigmoid(g0_ref[...] + bg[:, :D_MODEL])
        g1 = _sigmoid(g1_ref[...] + bg[:, D_MODEL:])
        merged = g0 * pa + g1 * pb
        mb = _mx(merged)
        mb_ref[...] = mb
        x2 = x_ref[...] + _dot(mb, wo_ref[...])
        r = lax.rsqrt(jnp.mean(x2 * x2, axis=-1, keepdims=True) + EPS)
        xn = x2 * r
        fg = fg_ref[...]
        diff = xn * fg - t_ref[...]
        loss_ref[...] += 0.5 * jnp.sum(jnp.mean(diff * diff, axis=-1, keepdims=True))
        dy = diff * (1.0 / D_MODEL)
        dfg_ref[...] += _bcast_rows(jnp.sum(dy * xn, axis=0, keepdims=True), 8)
        tt = dy * fg
        dx2 = r * (tt - xn * jnp.mean(tt * xn, axis=-1, keepdims=True))
        dx2_ref[...] = dx2
        dx2b = _mx(dx2)
        dx2b_ref[...] = dx2b
        dmerged = _dot_nt(dx2b, wo_ref[...])
        dpa = _mx(dmerged * g0)
        dpb = _mx(dmerged * g1)
        dpab_ref[...] = dpa
        dpbb_ref[...] = dpb
        dg0 = dmerged * pa * (g0 * (1.0 - g0))
        dg1 = dmerged * pb * (g1 * (1.0 - g1))
        dg0_ref[...] = _mx(dg0)
        dg1_ref[...] = _mx(dg1)
        dbg_ref[:, :D_MODEL] += _bcast_rows(jnp.sum(dg0, axis=0, keepdims=True), 8)
        dbg_ref[:, D_MODEL:] += _bcast_rows(jnp.sum(dg1, axis=0, keepdims=True), 8)
        dya_ref[...] = _dot_nt(dpa, wpa_ref[...])
        dyb_ref[...] = _dot_nt(dpb, wpb_ref[...])

    def rows(w=D_MODEL):
        return pl.BlockSpec((tm, w), lambda i: (i, 0))

    def const(shape):
        return pl.BlockSpec(shape, lambda i: (0, 0))

    f32 = jax.ShapeDtypeStruct((s, D_MODEL), F32)
    b16 = jax.ShapeDtypeStruct((s, D_MODEL), MXU_DTYPE)
    return pl.pallas_call(
        body,
        name="merge_fwd_bwd",
        grid=(s // tm,),
        in_specs=[
            rows(), rows(), rows(), rows(),
            pl.BlockSpec((tm, SEG), lambda i: (i, GL_SEG)),
            pl.BlockSpec((tm, SEG), lambda i: (i, GL_SEG + 1)),
            const((1, 2 * D_MODEL)), const((1, D_MODEL)),
            const((D_MODEL, D_MODEL)), const((D_MODEL, D_MODEL)), const((D_MODEL, D_MODEL)),
        ],
        out_specs=[rows()] * 9 + [const((8, HEAD_DIM)), const((8, D_MODEL)), const((8, 2 * D_MODEL))],
        out_shape=[f32, f32, f32, b16, b16, b16, b16, b16, b16,
                   jax.ShapeDtypeStruct((8, HEAD_DIM), F32),
                   jax.ShapeDtypeStruct((8, D_MODEL), F32),
                   jax.ShapeDtypeStruct((8, 2 * D_MODEL), F32)],
        compiler_params=_params(),
    )(x, target, ya, yb, proj, proj, b_gate, final_g, w_pa, w_pb, w_out)


def _attn_gate_bwd(dyb, ao, proj):
    s = dyb.shape[0]
    tm = min(s, TM_ROW)

    def body(dyb_ref, ao_ref, mz_ref, dao_ref, dmz_ref):
        mz = mz_ref[...]
        sg = _sigmoid(mz)
        d = dyb_ref[...]
        dao_ref[...] = _mx(d * (mz * sg))
        dmz_ref[...] = _mx(d * ao_ref[...] * (sg + mz * sg * (1.0 - sg)))

    rows = pl.BlockSpec((tm, D_MODEL), lambda i: (i, 0))
    b16 = jax.ShapeDtypeStruct((s, D_MODEL), MXU_DTYPE)
    return pl.pallas_call(
        body,
        name="attn_gate_bwd",
        grid=(s // tm,),
        in_specs=[rows, rows, pl.BlockSpec((tm, SEG), lambda i: (i, MZ_SEG))],
        out_specs=[rows, rows],
        out_shape=[b16, b16],
        compiler_params=_params(),
    )(dyb, ao, proj)


def _flash_bwd(q_all, k_all, v_all, dao, ao, lse):
    s = q_all.shape[0]
    t = min(s, T_ATT_BWD)
    n = s // t
    ts = t // ATT_SUB
    pairs =[(ki, qi) for ki in range(n) for qi in range(ki, n)]
    ki_list = jnp.asarray([p[0] for p in pairs], jnp.int32)
    qi_list = jnp.asarray([p[1] for p in pairs], jnp.int32)

    def body(ki_ref, qi_ref, q_ref, k_ref, v_ref, do_ref, ao_ref, lse_ref, dq_ref, dk_ref, dv_ref, dk_acc, dv_acc):
        step = pl.program_id(1)
        ki, qi = ki_ref[step], qi_ref[step]

        @pl.when(qi == ki)
        def _():
            dk_acc[...] = jnp.zeros_like(dk_acc)
            dv_acc[...] = jnp.zeros_like(dv_acc)

        @pl.when(ki == 0)
        def _():
            dq_ref[pl.ds(pl.multiple_of(qi * t, t), t), :] = jnp.zeros((t, HEAD_PAD), F32)

        def pair(masked):
            k = k_ref[...]
            v = v_ref[:, :HEAD_DIM]
            dk_parts, dv_parts = [], []
            for r in range(ATT_SUB):
                rs = slice(r * ts, (r + 1) * ts)
                q = q_ref[rs]
                sc = _dot_nt(q, k)
                if masked:
                    row = lax.broadcasted_iota(jnp.int32, (ts, t), 0) + r * ts
                    col = lax.broadcasted_iota(jnp.int32, (ts, t), 1)
                    sc = jnp.where(row >= col, sc, -jnp.inf)
                p = jnp.exp2(sc - lse_ref[rs, 0:1])
                do = do_ref[rs]
                delta = jnp.sum(do.astype(F32) * ao_ref[rs], axis=-1, keepdims=True)
                dv_parts.append(_dot_tn(_mx(p), do))
                ds = _mx(p * (_dot_nt(do, v) - delta))
                dk_parts.append(_dot_tn(ds, q))
                rows = pl.ds(pl.multiple_of(qi * t + r * ts, ts), ts)
                dq_ref[rows, :] += _dot(ds, k)

            dk_acc[...] += sum(dk_parts[1:], dk_parts[0])
            dv_acc[...] += sum(dv_parts[1:], dv_parts[0])

        @pl.when(qi == ki)
        def _():
            pair(True)

        @pl.when(qi > ki)
        def _():
            pair(False)

        @pl.when(qi == n - 1)
        def _():
            dk_ref[...] = dk_acc[...] * LN2
            dv_ref[...] = dv_acc[...]

    q_map = lambda h, p, ki_ref, qi_ref: (qi_ref[p], h)
    kv_map = lambda h, p, ki_ref, qi_ref: (ki_ref[p], h)
    grid_spec = pltpu.PrefetchScalarGridSpec(
        num_scalar_prefetch=2,
        grid=(HEADS, len(pairs)),
        in_specs=[
            pl.BlockSpec((t, HEAD_PAD), q_map),
            pl.BlockSpec((t, HEAD_PAD), kv_map),
            pl.BlockSpec((t, HEAD_PAD), kv_map),
            pl.BlockSpec((t, HEAD_DIM), q_map),
            pl.BlockSpec((t, HEAD_DIM), q_map),
            pl.BlockSpec((t, HEAD_DIM), q_map),
        ],
        out_specs=[
            pl.BlockSpec((s, HEAD_PAD), lambda h, p, ki_ref, qi_ref: (0, h)),
            pl.BlockSpec((t, HEAD_PAD), kv_map),
            pl.BlockSpec((t, HEAD_DIM), kv_map),
        ],
        scratch_shapes=[pltpu.VMEM((t, HEAD_PAD), F32), pltpu.VMEM((t, HEAD_DIM), F32)],
    )
    return pl.pallas_call(
        body,
        name="flash_bwd",
        grid_spec=grid_spec,
        out_shape=[
            jax.ShapeDtypeStruct((s, HEADS * HEAD_PAD), F32),
            jax.ShapeDtypeStruct((s, HEADS * HEAD_PAD), F32),
            jax.ShapeDtypeStruct((s, D_MODEL), F32),
        ],
        compiler_params=_params(VMEM_LIMIT_BIG),
    )(ki_list, qi_list, q_all, k_all, v_all, dao, ao, lse)


def _mla_prep_bwd(dq_all, dk_all, dv_all, proj, q_a_g, kv_a_g, w_uq_p, w_kn, w_v, cos, sin_a, sin_b):
    s = proj.shape[0]
    tm = min(s, TM_ROW)

    def body(dq_ref, dk_ref, dv_ref, sm_ref, gq_ref, gk_ref, wq_ref, wkn_ref, wv_ref, cos_ref, sa_ref, sb_ref,
             dsm_ref, dqf_ref, dkn_ref, dvb_ref, dgq_ref, dgk_ref):
        i = pl.program_id(0)

        @pl.when(i == 0)
        def _():
            dgq_ref[...] = jnp.zeros_like(dgq_ref)
            dgk_ref[...] = jnp.zeros_like(dgk_ref)

        cos_t, sa, sb = cos_ref[...], sa_ref[...], sb_ref[...]
        dkpe = jnp.zeros((tm, HEAD_DIM), F32)
        for h in range(HEADS):
            lo = h * HEAD_PAD
            dqf_ref[:, lo:lo + HEAD_DIM] = _mx(dq_ref[:, lo:lo + HEAD_DIM] * QK_SCALE)
            dqf_ref[:, lo + HEAD_DIM:lo + HEAD_PAD] = _mx(
                _rope_t(dq_ref[:, lo + HEAD_DIM:lo + HEAD_PAD] * QK_SCALE, cos_t, sa, sb))
            dkn_ref[:, h * HEAD_DIM:(h + 1) * HEAD_DIM] = _mx(dk_ref[:, lo:lo + HEAD_DIM])
            dkpe = dkpe + dk_ref[:, lo + HEAD_DIM:lo + HEAD_PAD]
        dkr = _rope_t(dkpe, cos_t, sa, sb)
        dvb = _mx(dv_ref[...])
        dvb_ref[...] = dvb
        dcqn = _dot_nt(dqf_ref[...], wq_ref[...])
        dckvn = _dot_nt(dkn_ref[...], wkn_ref[...]) + _dot_nt(dvb, wv_ref[...])

        small = sm_ref[...]
        cq = small[:, :Q_LORA]
        ckv = small[:, Q_LORA:Q_LORA + KV_LORA]
        rq = lax.rsqrt(jnp.mean(cq * cq, axis=-1, keepdims=True) + EPS)
        rk = lax.rsqrt(jnp.mean(ckv * ckv, axis=-1, keepdims=True) + EPS)
        cqh = cq * rq
        ckh = ckv * rk
        dgq_ref[...] += _bcast_rows(jnp.sum(dcqn * cqh, axis=0, keepdims=True), 8)
        dgk_ref[...] += _bcast_rows(jnp.sum(dckvn * ckh, axis=0, keepdims=True), 8)
        tq = dcqn * gq_ref[...]
        tk = dckvn * gk_ref[...]
        dcq = rq * (tq - cqh * jnp.mean(tq * cqh, axis=-1, keepdims=True))
        dckv = rk * (tk - ckh * jnp.mean(tk * ckh, axis=-1, keepdims=True))
        dsm_ref[:, :Q_LORA] = _mx(dcq)
        dsm_ref[:, Q_LORA:Q_LORA + KV_LORA] = _mx(dckv)
        dsm_ref[:, Q_LORA + KV_LORA:Q_LORA + KV_LORA + HEAD_DIM] = _mx(dkr)
        dsm_ref[:, Q_LORA + KV_LORA + HEAD_DIM:] = jnp.zeros((tm, SEG - Q_LORA - KV_LORA - HEAD_DIM), MXU_DTYPE)

    def const(shape):
        return pl.BlockSpec(shape, lambda i: (0, 0))

    def rows(w):
        return pl.BlockSpec((tm, w), lambda i: (i, 0))

    return pl.pallas_call(
        body,
        name="mla_prep_bwd",
        grid=(s // tm,),
        in_specs=[
            rows(HEADS * HEAD_PAD), rows(HEADS * HEAD_PAD), rows(D_MODEL),
            pl.BlockSpec((tm, SEG), lambda i: (i, SMALL_SEG)),
            const((1, Q_LORA)), const((1, KV_LORA)),
            const((Q_LORA, HEADS * HEAD_PAD)), const((KV_LORA, D_MODEL)), const((KV_LORA, D_MODEL)),
            rows(HEAD_DIM), rows(HEAD_DIM), rows(HEAD_DIM),
        ],
        out_specs=[rows(SEG), rows(HEADS * HEAD_PAD), rows(D_MODEL), rows(D_MODEL),
                   const((8, Q_LORA)), const((8, KV_LORA))],
        out_shape=[
            jax.ShapeDtypeStruct((s, SEG), MXU_DTYPE),
            jax.ShapeDtypeStruct((s, HEADS * HEAD_PAD), MXU_DTYPE),
            jax.ShapeDtypeStruct((s, D_MODEL), MXU_DTYPE),
            jax.ShapeDtypeStruct((s, D_MODEL), MXU_DTYPE),
            jax.ShapeDtypeStruct((8, Q_LORA), F32),
            jax.ShapeDtypeStruct((8, KV_LORA), F32),
        ],
        compiler_params=_params(),
    )(dq_all, dk_all, dv_all, proj, q_a_g, kv_a_g, w_uq_p, w_kn, w_v, cos, sin_a, sin_b)


def _hgrn_bwd(proj, lb_logits, hg_norm_g, o_all, dya, states):
    s = proj.shape[0]
    t = min(s, T_HGRN)
    nb = s // t
    nc = t // HG_CHUNK

    def body(hq_ref, hf_ref, hi_ref, hz_ref, lb_ref, g_ref, o_ref, dya_ref, st_ref,
             dhq_ref, dhf_ref, dhi_ref, dhz_ref, dlb_ref, dg_ref, dstate):
        h, b = pl.program_id(0), pl.program_id(1)

        @pl.when(b == 0)
        def _():
            dstate[...] = jnp.zeros_like(dstate)
            dlb_ref[...] = jnp.zeros_like(dlb_ref)

        @pl.when((b == 0) & (h == 0))
        def _():
            dg_ref[...] = jnp.zeros_like(dg_ref)

        lower = _chunk_lower_mask(t)
        pos = _chunk_pos(t)
        ghg = g_ref[...]
        for hh in range(HG_HEADS_PER_STEP):
            cols = slice(hh * HEAD_DIM, (hh + 1) * HEAD_DIM)
            hq, hf, hz = hq_ref[:, cols], hf_ref[:, cols], hz_ref[:, cols]
            gt = _hgrn_gates(hq, hf, lb_ref[:, cols], pos)
            vb = _mx(hi_ref[:, cols])
            qi, ki, ko = gt["qi"], gt["ki"], gt["ko"]
            qib, kib, kob = _mx(qi), _mx(ki), _mx(ko)

            o = o_ref[:, cols]
            sz = _sigmoid(hz)
            r = lax.rsqrt(jnp.mean(o * o, axis=-1, keepdims=True) + EPS)
            on = o * r
            dya_t = dya_ref[:, cols]
            don = dya_t * (hz * sz)
            dhz_ref[:, cols] = _mx(dya_t * (on * ghg) * (sz + hz * sz * (1.0 - sz)))
            dg_ref[...] += _bcast_rows(jnp.sum(don * on, axis=0, keepdims=True), 8)
            tt = don * ghg
            do = r * (tt - on * jnp.mean(tt * on, axis=-1, keepdims=True))
            dob = _mx(do)

            sts = [st_ref[0, hh]]
            for c in range(nc - 1):
                sl = slice(c * HG_CHUNK, (c + 1) * HG_CHUNK)
                sts.append(sts[-1] * gt["dec"][c * HG_CHUNK:c * HG_CHUNK + 1, :] + _dot_tn(vb[sl], kob[sl]))

            a = jnp.where(lower, _dot_nt(qib, kib), 0.0)
            da = _mx(jnp.where(lower, _dot_nt(dob, vb), 0.0))
            dqi_intra = _dot(da, kib)
            dki = _dot_tn(da, qib)
            dv_intra = _dot_tn(_mx(a), dob)

            dst = dstate[hh]
            dqi_parts, dko_parts, dv_parts, dd_parts = [None] * nc, [None] * nc, [None] * nc, [None] * nc
            for c in reversed(range(nc)):
                sl = slice(c * HG_CHUNK, (c + 1) * HG_CHUNK)
                dec = gt["dec"][c * HG_CHUNK:c * HG_CHUNK + 1, :]
                dstb = _mx(dst)
                dv_parts[c] = dv_intra[sl] + _dot_nt(kob[sl], dstb)
                dko_parts[c] = _dot(vb[sl], dstb)
                dqi_parts[c] = dqi_intra[sl] + _dot(dob[sl], _mx(sts[c]))
                dd_parts[c] = _bcast_rows(jnp.sum(dst * sts[c], axis=0, keepdims=True) * dec, HG_CHUNK)
                dst = dst * dec + _dot_tn(dob[sl], qib[sl])
            dstate[hh] = dst
            dqi = jnp.concatenate(dqi_parts, axis=0)
            dko = jnp.concatenate(dko_parts, axis=0)
            dv = jnp.concatenate(dv_parts, axis=0)
            dd = jnp.concatenate(dd_parts, axis=0)

            dq = dqi * gt["eb"]
            dk = dki * gt["enb"] + dko * gt["eo"]
            db = dqi * qi - dki * ki - dko * ko
            dlogf = _rcumsum_chunk(db, pos) + _chunk_total(dko * ko) + dd
            df = dlogf / gt["f"] - dk
            lb, sig, sq = gt["lb"], gt["sig"], gt["sq"]
            dhf_ref[:, cols] = _mx(df * (1.0 - lb) * (sig * (1.0 - sig)))
            dhq_ref[:, cols] = _mx(dq * (sq + hq * sq * (1.0 - sq)))
            dhi_ref[:, cols] = _mx(dv)
            dlb = jnp.sum(df * (1.0 - sig), axis=0, keepdims=True) * (lb * (1.0 - lb))
            dlb_ref[:, cols] += jnp.concatenate([dlb, -dlb], axis=0)

    hw = HG_HEADS_PER_STEP * HEAD_DIM
    hsteps = HEADS // HG_HEADS_PER_STEP

    def seg(k):
        return pl.BlockSpec((t, hw), lambda h, b, k=k: (nb - 1 - b, k * hsteps + h))

    blk = pl.BlockSpec((t, hw), lambda h, b: (nb - 1 - b, h))
    b16 = jax.ShapeDtypeStruct((s, D_MODEL), MXU_DTYPE)
    return pl.pallas_call(
        body,
        name="hgrn_bwd",
        grid=(hsteps, nb),
        in_specs=[seg(0), seg(1), seg(2), seg(3),
                  pl.BlockSpec((2, hw), lambda h, b: (0, h)),
                  pl.BlockSpec((1, HEAD_DIM), lambda h, b: (0, 0)),
                  blk, blk,
                  pl.BlockSpec((1, HG_HEADS_PER_STEP, HEAD_DIM, HEAD_DIM), lambda h, b: (nb - 1 - b, h, 0, 0))],
        out_specs=[blk, blk, blk, blk,
                   pl.BlockSpec((2, hw), lambda h, b: (0, h)),
                   pl.BlockSpec((8, HEAD_DIM), lambda h, b: (0, 0))],
        out_shape=[b16, b16, b16, b16,
                   jax.ShapeDtypeStruct((2, D_MODEL), F32),
                   jax.ShapeDtypeStruct((8, HEAD_DIM), F32)],
        scratch_shapes=[pltpu.VMEM((HG_HEADS_PER_STEP, HEAD_DIM, HEAD_DIM), F32)],
        compiler_params=_params(),
    )(proj, proj, proj, proj, lb_logits, hg_norm_g, o_all, dya, states)


def _dh_bwd(segs, w_in_p, x, dx2, norm_g):
    s = x.shape[0]
    tm = min(s, TM_ROW)
    nseg = len(segs)

    def body(*refs):
        seg_refs = refs[:nseg]
        w_ref, x_ref, dx2_ref, g_ref, gx_ref, dng_ref, dp_buf = refs[nseg:]
        i = pl.program_id(0)

        @pl.when(i == 0)
        def _():
            dng_ref[...] = jnp.zeros_like(dng_ref)

        for k, sref in enumerate(seg_refs):
            dp_buf[:, k * SEG:(k + 1) * SEG] = sref[...]
        dh = _dot_nt(dp_buf[...], w_ref[...])
        xf = x_ref[...]
        r = lax.rsqrt(jnp.mean(xf * xf, axis=-1, keepdims=True) + EPS)
        xh = xf * r
        dng_ref[...] += _bcast_rows(jnp.sum(dh * xh, axis=0, keepdims=True), 8)
        tt = dh * g_ref[...]
        gx_ref[...] = dx2_ref[...] + r * (tt - xh * jnp.mean(tt * xh, axis=-1, keepdims=True))

    rows = pl.BlockSpec((tm, D_MODEL), lambda i: (i, 0))
    return pl.pallas_call(
        body,
        name="dh_bwd",
        grid=(s // tm,),
        in_specs=[pl.BlockSpec((tm, SEG), lambda i: (i, 0))] * nseg + [
            _resident((D_MODEL, PROJ_W)),
            rows, rows,
            pl.BlockSpec((1, D_MODEL), lambda i: (0, 0)),
        ],
        out_specs=[rows, pl.BlockSpec((8, D_MODEL), lambda i: (0, 0))],
        out_shape=[jax.ShapeDtypeStruct((s, D_MODEL), F32), jax.ShapeDtypeStruct((8, D_MODEL), F32)],
        scratch_shapes=[pltpu.VMEM((tm, PROJ_W), MXU_DTYPE)],
        compiler_params=_params(),
    )(*segs, w_in_p, x, dx2, norm_g)


def _matmul_tn(a, b, name, out_dtype=F32):
    s, m = a.shape
    n = b.shape[1]
    ts = min(s, TS_TN)
    tn = min(n, SEG)
    nk = s // ts

    def body(a_ref, b_ref, o_ref, acc):
        k = pl.program_id(1)
        part = _dot_tn(a_ref[...], b_ref[...])

        @pl.when(k == 0)
        def _():
            acc[...] = part

        @pl.when(k > 0)
        def _():
            acc[...] += part

        @pl.when(k == nk - 1)
        def _():
            o_ref[...] = acc[...].astype(out_dtype)

    return pl.pallas_call(
        body,
        name=name,
        grid=(n // tn, nk),
        in_specs=[pl.BlockSpec((ts, m), lambda j, k: (k, 0)), pl.BlockSpec((ts, tn), lambda j, k: (k, j))],
        out_specs=pl.BlockSpec((m, tn), lambda j, k: (0, j)),
        out_shape=jax.ShapeDtypeStruct((m, n), out_dtype),
        scratch_shapes=[pltpu.VMEM((m, tn), F32)],
        compiler_params=_params(),
    )(a, b)


def _w_in_pieces():
    per = IN_COLS // N_DEV
    pad_at = SMALL_SEG * SEG + Q_LORA + KV_LORA + QK_ROPE
    pieces = []
    for j in range(N_DEV):
        u0, u1 = j * per, (j + 1) * per
        cuts = [u0] + ([pad_at] if u0 < pad_at < u1 else []) + [u1]
        for a, b in zip(cuts[:-1], cuts[1:]):
            pieces.append((j, a - u0, b - u0, a if a < pad_at else a + PROJ_W - IN_COLS))
    return pad_at, pieces


def _assemble_w_in(gathered):
    tr = TM_ROW
    pad_at, pieces = _w_in_pieces()

    def body(in_ref, out_ref):
        out_ref[:, pad_at:pad_at + PROJ_W - IN_COLS] = jnp.zeros((tr, PROJ_W - IN_COLS), gathered.dtype)
        for j, a, b, p0 in pieces:
            out_ref[:, p0:p0 + b - a] = in_ref[j, :, a:b]

    return pl.pallas_call(
        body,
        name="assemble_w_in",
        grid=(D_MODEL // tr,),
        in_specs=[pl.BlockSpec((N_DEV, tr, PACK_COLS), lambda i: (0, i, 0))],
        out_specs=pl.BlockSpec((tr, PROJ_W), lambda i: (i, 0)),
        out_shape=jax.ShapeDtypeStruct((D_MODEL, PROJ_W), gathered.dtype),
        compiler_params=_params(),
    )(gathered)


def _scatter_dw_in(dw_segs):
    tr = TM_ROW
    _, pieces = _w_in_pieces()
    per = IN_COLS // N_DEV
    nseg = len(dw_segs)

    def body(*refs):
        out_ref, buf = refs[nseg:]
        for k in range(nseg):
            buf[:, k * SEG:(k + 1) * SEG] = refs[k][...]
        for j in range(N_DEV):
            out_ref[j, :, per:] = jnp.zeros((tr, PACK_COLS - per), TRANSPORT_DTYPE)
        for j, a, b, p0 in pieces:
            out_ref[j, :, a:b] = buf[:, p0:p0 + b - a].astype(TRANSPORT_DTYPE)

    return pl.pallas_call(
        body,
        name="scatter_dw_in",
        grid=(D_MODEL // tr,),
        in_specs=[pl.BlockSpec((tr, SEG), lambda i: (i, 0))] * nseg,
        out_specs=pl.BlockSpec((N_DEV, tr, PACK_COLS), lambda i: (0, i, 0)),
        out_shape=jax.ShapeDtypeStruct((N_DEV, D_MODEL, PACK_COLS), TRANSPORT_DTYPE),
        scratch_shapes=[pltpu.VMEM((tr, PROJ_W), F32)],
        compiler_params=_params(),
    )(*dw_segs)


def _rope_tables(s):
    inv = ROPE_THETA ** (-jnp.arange(0, QK_ROPE, 2, dtype=F32) / QK_ROPE)
    ang = jnp.arange(s, dtype=F32)[:, None] * inv[None, :]
    cos, sin = jnp.cos(ang), jnp.sin(ang)
    z32 = jnp.zeros_like(cos)
    z64 = jnp.zeros((s, HEAD_DIM - QK_ROPE), F32)
    cos_t = jnp.concatenate([cos, cos, z64], axis=1)
    sin_a = jnp.concatenate([-sin, z32, z64], axis=1)
    sin_b = jnp.concatenate([z32, sin, z64], axis=1)
    return cos_t, sin_a, sin_b


def _pack_misc(w_uq, w_ukv, norm_g, b_gate, lb_logits, hg_norm_g, q_a_g, kv_a_g, final_norm_g, extra):
    misc = jnp.concatenate([hg_norm_g.reshape(-1), q_a_g.reshape(-1), kv_a_g.reshape(-1), extra.reshape(-1),
                            jnp.zeros((PACK_COLS - HEAD_DIM - Q_LORA - KV_LORA - 1,), F32)])
    return jnp.concatenate([w_uq.reshape(ROWS_W_UQ, PACK_COLS), w_ukv.reshape(ROWS_W_UKV, PACK_COLS),
                            norm_g.reshape(1, -1), b_gate.reshape(2, -1), lb_logits.reshape(2, -1),
                            misc.reshape(1, -1), final_norm_g.reshape(1, -1), jnp.zeros((1, PACK_COLS), F32)], axis=0)


def _unpack_misc(p):
    sm = p[ROWS_W_UQ + ROWS_W_UKV:]
    misc = sm[5]
    return dict(
        w_uq=p[:ROWS_W_UQ].reshape(1, Q_LORA, QK_DIM),
        w_ukv=p[ROWS_W_UQ:ROWS_W_UQ + ROWS_W_UKV].reshape(1, KV_LORA, 2 * HEAD_DIM),
        norm_g=sm[0:1], b_gate=sm[1:3].reshape(1, -1), lb_logits=sm[3:5],
        hg_norm_g=misc[None, :HEAD_DIM], q_a_g=misc[None, HEAD_DIM:HEAD_DIM + Q_LORA],
        kv_a_g=misc[None, HEAD_DIM + Q_LORA:HEAD_DIM + Q_LORA + KV_LORA], final_norm_g=sm[6],
        extra=misc[HEAD_DIM + Q_LORA + KV_LORA],
    )


def _weight_shard_buffer(w_in, w_uq, w_ukv, w_pa, w_pb, w_out):
    w_in_pad = jnp.pad(w_in.reshape(D_MODEL, -1), ((0, 0), (0, PACK_COLS - IN_COLS // N_DEV)))
    parts = [w_in_pad] + [a.reshape(-1, PACK_COLS) for a in (w_pa, w_pb, w_out, w_uq, w_ukv)]
    return jnp.concatenate(parts + [jnp.zeros((ROWS_AG - ROWS_AG_USED, PACK_COLS), F32)], axis=0)


def _full_weights(gathered):
    w_in_p = _assemble_w_in(gathered)
    r0 = D_MODEL
    mats = []
    for _ in range(3):
        mats.append(gathered[:, r0:r0 + ROWS_W_PROJ].reshape(D_MODEL, D_MODEL))
        r0 += ROWS_W_PROJ
    w_uq = gathered[:, r0:r0 + ROWS_W_UQ].reshape(N_DEV, Q_LORA, QK_DIM).transpose(1, 0, 2)
    w_uq_p = jnp.concatenate([w_uq, jnp.zeros((Q_LORA, HEADS, HEAD_PAD - QK_DIM), w_uq.dtype)], axis=2)
    w_uq_p = w_uq_p.reshape(Q_LORA, HEADS * HEAD_PAD)
    r0 += ROWS_W_UQ
    w_ukv = gathered[:, r0:r0 + ROWS_W_UKV].reshape(N_DEV, KV_LORA, 2 * HEAD_DIM).transpose(1, 0, 2)
    w_kn = w_ukv[:, :, :HEAD_DIM].reshape(KV_LORA, D_MODEL)
    w_v = w_ukv[:, :, HEAD_DIM:].reshape(KV_LORA, D_MODEL)
    return w_in_p, w_uq_p, w_kn, w_v, mats[0], mats[1], mats[2]


def _grad_slabs(g):
    dw_uq = g["w_uq_p"].reshape(Q_LORA, HEADS, HEAD_PAD)[:, :, :QK_DIM].transpose(1, 0, 2)
    dw_ukv = jnp.concatenate([g["w_kn"].reshape(KV_LORA, HEADS, HEAD_DIM),
                              g["w_v"].reshape(KV_LORA, HEADS, HEAD_DIM)], axis=2).transpose(1, 0, 2)
    misc = [_pack_misc(dw_uq[j], dw_ukv[j], g["norm_g"], g["b_gate"], g["lb_logits"], g["hg_norm_g"],
                       g["q_a_g"], g["kv_a_g"], g["final_norm_g"], g["loss"]) for j in range(N_DEV)]
    mats = [g[n].reshape(N_DEV, ROWS_W_PROJ, PACK_COLS) for n in ("w_pa", "w_pb", "w_out")]
    return [_scatter_dw_in(g["w_in_segs"])] + mats + [jnp.stack(misc)]


def _local_grads(x, target, norm_g, b_gate, lb_logits, hg_norm_g, q_a_g, kv_a_g, final_g,
                 w_in_p, w_uq_p, w_kn, w_v, w_pa, w_pb, w_out):
    s = x.shape[0]
    cos, sin_a, sin_b = _rope_tables(s)
    proj, h = _inproj(x, norm_g, w_in_p)
    o_all, ya, states = _hgrn_fwd(proj, lb_logits, hg_norm_g)
    q_all, k_all, v_all, cqn, ckvn = _mla_prep(proj, q_a_g, kv_a_g, w_uq_p, w_kn, w_v, cos, sin_a, sin_b)
    ao, lse, yb = _flash_fwd(q_all, k_all, v_all, proj)
    (dx2, dya, dyb, dg0, dg1, merged_b, dpa_b, dpb_b, dx2_b,
     loss_acc, dfg_acc, dbg_acc) = _merge_fwd_bwd(x, target, ya, yb, proj, b_gate, final_g, w_pa, w_pb, w_out)
    dao, dmz = _attn_gate_bwd(dyb, ao, proj)
    dq_all, dk_all, dv_all = _flash_bwd(q_all, k_all, v_all, dao, ao, lse)
    dsmall, dqf_b, dkn_b, dv_b, dgq_acc, dgk_acc = _mla_prep_bwd(
        dq_all, dk_all, dv_all, proj, q_a_g, kv_a_g, w_uq_p, w_kn, w_v, cos, sin_a, sin_b)
    dhq, dhf, dhi, dhz, dlb, dhg_acc = _hgrn_bwd(proj, lb_logits, hg_norm_g, o_all, dya, states)
    segs = [dhq, dhf, dhi, dhz, dsmall, dmz, dg0, dg1]
    grad_x, dng_acc = _dh_bwd(segs, w_in_p, x, dx2, norm_g)
    return dict(
        loss=loss_acc[0, 0], grad_x=grad_x,
        norm_g=dng_acc[0:1], b_gate=dbg_acc[0:1], lb_logits=dlb, hg_norm_g=dhg_acc[0:1],
        q_a_g=dgq_acc[0:1], kv_a_g=dgk_acc[0:1], final_norm_g=dfg_acc[0],
        w_in_segs=[_matmul_tn(h, sg, "dw_in_%d" % k) for k, sg in enumerate(segs)],
        w_uq_p=_matmul_tn(cqn, dqf_b, "dw_uq"),
        w_kn=_matmul_tn(ckvn, dkn_b, "dw_kn"),
        w_v=_matmul_tn(ckvn, dv_b, "dw_v"),
        w_pa=_matmul_tn(ya, dpa_b, "dw_pa", TRANSPORT_DTYPE),
        w_pb=_matmul_tn(yb, dpb_b, "dw_pb", TRANSPORT_DTYPE),
        w_out=_matmul_tn(merged_b, dx2_b, "dw_out", TRANSPORT_DTYPE),
    )


def kernel(x, norm_g, w_in, b_gate, lb_logits, hg_norm_g, q_a_g, w_uq, kv_a_g, w_ukv, w_proj_a, w_proj_b, w_out, final_norm_g, loss_target, m_norm_g, m_w_in, m_b_gate, m_lb_logits, m_hg_norm_g, m_q_a_g, m_w_uq, m_kv_a_g, m_w_ukv, m_w_proj_a, m_w_proj_b, m_w_out, m_final_norm_g, v_norm_g, v_w_in, v_b_gate, v_lb_logits, v_hg_norm_g, v_q_a_g, v_w_uq, v_kv_a_g, v_w_ukv, v_w_proj_a, v_w_proj_b, v_w_out, v_final_norm_g):
    zero = jnp.zeros((1,), F32)
    shard = _weight_shard_buffer(w_in, w_uq, w_ukv, w_proj_a, w_proj_b, w_out).astype(MXU_DTYPE)
    full = _full_weights(_all_gather_packed(shard))
    g = _local_grads(x[0], loss_target[0], norm_g, b_gate, lb_logits, hg_norm_g, q_a_g, kv_a_g,
                     final_norm_g.reshape(1, -1), *full)
    recv_in, recv_pa, recv_pb, recv_out, recv_misc = _reduce_scatter_exchange(_grad_slabs(g))

    out_in = _sum_adamw(recv_in, w_in[0], m_w_in[0], v_w_in[0], "adamw_w_in")
    out_pa = _sum_adamw(recv_pa, w_proj_a[0], m_w_proj_a[0], v_w_proj_a[0], "adamw_w_pa")
    out_pb = _sum_adamw(recv_pb, w_proj_b[0], m_w_proj_b[0], v_w_proj_b[0], "adamw_w_pb")
    out_out = _sum_adamw(recv_out, w_out[0], m_w_out[0], v_w_out[0], "adamw_w_out")
    out_misc = _sum_adamw(
        recv_misc,
        _pack_misc(w_uq, w_ukv, norm_g, b_gate, lb_logits, hg_norm_g, q_a_g, kv_a_g, final_norm_g, zero),
        _pack_misc(m_w_uq, m_w_ukv, m_norm_g, m_b_gate, m_lb_logits, m_hg_norm_g, m_q_a_g, m_kv_a_g,
                   m_final_norm_g, zero),
        _pack_misc(v_w_uq, v_w_ukv, v_norm_g, v_b_gate, v_lb_logits, v_hg_norm_g, v_q_a_g, v_kv_a_g,
                   v_final_norm_g, zero),
        "adamw_misc")
    names = ["norm_g", "w_in", "b_gate", "lb_logits", "hg_norm_g", "q_a_g", "w_uq", "kv_a_g", "w_ukv",
             "w_proj_a", "w_proj_b", "w_out", "final_norm_g"]
    kinds = []
    for i in range(4):
        d = _unpack_misc(out_misc[i])
        d.update(w_in=out_in[i][None], w_proj_a=out_pa[i][None], w_proj_b=out_pb[i][None], w_out=out_out[i][None])
        kinds.append(d)
    return (kinds[0]["extra"], g["grad_x"][None], *[d[n] for d in kinds for n in names])
```

```python
import functools

import jax
import jax.numpy as jnp
from jax import lax
from jax.experimental import pallas as pl
from jax.experimental.pallas import tpu as pltpu

D_MODEL = 1024
HEADS = 8
HEAD_DIM = 128
HG_CHUNK = 32
Q_LORA = 384
KV_LORA = 256
QK_ROPE = 64
QK_DIM = 192
ROPE_THETA = 10000.0
EPS = 1e-6
IN_COLS = 7872
ADAM_LR = 0.001
ADAM_B1 = 0.9
ADAM_B2 = 0.999
ADAM_EPS = 1e-08
ADAM_WD = 0.01
ADAM_STEP = 10

N_DEV = 8
SEG = 1024
PROJ_W = 8 * SEG
SMALL_SEG = 4
MZ_SEG = 5
GL_SEG = 6
HEAD_PAD = 256
PACK_COLS = 1024
ROWS_W_UQ = 72
ROWS_W_UKV = 64
ROWS_W_PROJ = 128
ROWS_AG_USED = D_MODEL + 3 * ROWS_W_PROJ + ROWS_W_UQ + ROWS_W_UKV
ROWS_AG = 1552

QK_SCALE = QK_DIM ** -0.5
LOG2E = 1.4426950408889634
LN2 = 0.6931471805599453
Q_PRESCALE = QK_SCALE * LOG2E

MXU_DTYPE = jnp.bfloat16
TRANSPORT_DTYPE = jnp.bfloat16
VMEM_LIMIT = 48 * 1024 * 1024
VMEM_LIMIT_BIG = 60 * 1024 * 1024

T_HGRN = 256
HG_HEADS_PER_STEP = 4
TM_ROW = 256
T_ATT = 1024
T_ATT_BWD = 1024
ATT_SUB = 4
TS_TN = 2048
TR_ADAM = 256

F32 = jnp.float32
MESH = pl.DeviceIdType.MESH


def _dot(a, b):
    return jnp.dot(a, b, preferred_element_type=F32)


def _dot_nt(a, b):
    return lax.dot_general(a, b, (((1,), (1,)), ((), ())), preferred_element_type=F32)


def _dot_tn(a, b):
    return lax.dot_general(a, b, (((0,), (0,)), ((), ())), preferred_element_type=F32)


def _mx(a):
    return a.astype(MXU_DTYPE)


def _sigmoid(x):
    return 1.0 / (1.0 + jnp.exp(-x))


def _params(vmem=VMEM_LIMIT, **kw):
    return pltpu.CompilerParams(vmem_limit_bytes=vmem, **kw)


def _bcast_rows(row, n):
    return jnp.broadcast_to(row, (n, row.shape[-1]))


def _resident(shape):
    return pl.BlockSpec(shape, lambda *_: (0, 0), pipeline_mode=pl.Buffered(1))


HBM_SPEC = pl.BlockSpec(memory_space=pltpu.HBM)


def _all_gather_packed(shard):
    rows, cols = shard.shape

    def body(x_ref, out_ref, send_sems, recv_sems, local_sem):
        x, y, c = lax.axis_index("x"), lax.axis_index("y"), lax.axis_index("c")
        me, sibling = (x, y, c), (x, y, 1 - c)
        chips = [(1 - x, y), (x, 1 - y), (1 - x, 1 - y)]

        def slot(px, py, pc):
            return out_ref.at[4 * px + 2 * py + pc]

        def copy(k, block, to, src=None):
            return pltpu.make_async_remote_copy(
                src_ref=slot(*block) if src is None else src,
                dst_ref=slot(*block),
                send_sem=send_sems.at[k],
                recv_sem=recv_sems.at[k],
                device_id=to,
                device_id_type=MESH,
            )

        mine = pltpu.make_async_copy(x_ref, slot(*me), local_sem)
        mine.start()
        first = [copy(0, me, sibling, src=x_ref)]
        first += [copy(1 + j, me, (*chip, c), src=x_ref) for j, chip in enumerate(chips)]
        for cp in first:
            cp.start()
        passed = [copy(4 + j, (*chip, c), sibling) for j, chip in enumerate(chips)]
        for j, chip in enumerate(chips):
            copy(1 + j, (*chip, c), me).wait_recv()
            passed[j].start()
        copy(0, sibling, me).wait_recv()
        for j, chip in enumerate(chips):
            copy(4 + j, (*chip, 1 - c), me).wait_recv()
        for cp in first + passed:
            cp.wait_send()
        mine.wait()

    return pl.pallas_call(
        body,
        name="ag_weights",
        out_shape=jax.ShapeDtypeStruct((N_DEV, rows, cols), shard.dtype),
        in_specs=[HBM_SPEC],
        out_specs=HBM_SPEC,
        scratch_shapes=[
            pltpu.SemaphoreType.DMA((7,)),
            pltpu.SemaphoreType.DMA((7,)),
            pltpu.SemaphoreType.DMA,
        ],
    )(shard)


def _reduce_scatter_exchange(slab_sets):
    n_ops = len(slab_sets)

    def body(*refs):
        g_refs, recv_refs = refs[:n_ops], refs[n_ops:2 * n_ops]
        send_sems, recv_sems, local_sems = refs[2 * n_ops:]
        x, y, c = lax.axis_index("x"), lax.axis_index("y"), lax.axis_index("c")
        me = 4 * x + 2 * y + c

        def copy(i, k, landing):
            px, py, pc = x ^ ((k >> 2) & 1), y ^ ((k >> 1) & 1), c ^ (k & 1)
            peer = 4 * px + 2 * py + pc
            return pltpu.make_async_remote_copy(
                src_ref=g_refs[i].at[peer],
                dst_ref=recv_refs[i].at[peer if landing else me],
                send_sem=send_sems.at[i * (N_DEV - 1) + k - 1],
                recv_sem=recv_sems.at[i * (N_DEV - 1) + k - 1],
                device_id=(px, py, pc),
                device_id_type=MESH,
            )

        mine = [pltpu.make_async_copy(g_refs[i].at[me], recv_refs[i].at[me], local_sems.at[i]) for i in range(n_ops)]
        for cp in mine:
            cp.start()
        sends = [copy(i, k, False) for i in range(n_ops) for k in range(1, N_DEV)]
        for cp in sends:
            cp.start()
        for i in range(n_ops):
            for k in range(1, N_DEV):
                copy(i, k, True).wait_recv()
        for cp in sends:
            cp.wait_send()
        for cp in mine:
            cp.wait()

    return pl.pallas_call(
        body,
        name="rs_grads",
        out_shape=[jax.ShapeDtypeStruct(a.shape, a.dtype) for a in slab_sets],
        in_specs=[HBM_SPEC] * n_ops,
        out_specs=[HBM_SPEC] * n_ops,
        scratch_shapes=[
            pltpu.SemaphoreType.DMA((n_ops * (N_DEV - 1),)),
            pltpu.SemaphoreType.DMA((n_ops * (N_DEV - 1),)),
            pltpu.SemaphoreType.DMA((n_ops,)),
        ],
    )(*slab_sets)


def _sum_adamw(recv, w, m, v, name):
    rows, cols = w.shape
    tr = min(rows, TR_ADAM)

    def body(r_ref, w_ref, m_ref, v_ref, g_out, d_out, m_out, v_out):
        g = r_ref[0].astype(F32)
        for i in range(1, N_DEV):
            g = g + r_ref[i].astype(F32)
        g = g[:, :cols]
        m_new = ADAM_B1 * m_ref[...] + (1.0 - ADAM_B1) * g
        v_new = ADAM_B2 * v_ref[...] + (1.0 - ADAM_B2) * (g * g)
        m_hat = m_new / (1.0 - ADAM_B1 ** ADAM_STEP)
        v_hat = v_new / (1.0 - ADAM_B2 ** ADAM_STEP)
        g_out[...] = g
        d_out[...] = -ADAM_LR * (m_hat / (jnp.sqrt(v_hat) + ADAM_EPS) + ADAM_WD * w_ref[...])
        m_out[...] = m_new
        v_out[...] = v_new

    row_spec = pl.BlockSpec((tr, cols), lambda i: (i, 0))
    shape = jax.ShapeDtypeStruct((rows, cols), F32)
    return pl.pallas_call(
        body,
        name=name,
        grid=(rows // tr,),
        in_specs=[pl.BlockSpec((N_DEV, tr, PACK_COLS), lambda i: (0, i, 0)), row_spec, row_spec, row_spec],
        out_specs=[row_spec] * 4,
        out_shape=[shape] * 4,
        compiler_params=_params(),
    )(recv, w, m, v)


def _inproj(x, norm_g, w_in_p):
    s = x.shape[0]
    tm = min(s, TM_ROW)

    def body(x_ref, g_ref, w_ref, proj_ref, h_ref):
        xf = x_ref[...]
        r = lax.rsqrt(jnp.mean(xf * xf, axis=-1, keepdims=True) + EPS)
        h = _mx(xf * r * g_ref[...])
        h_ref[...] = h
        for j in range(PROJ_W // SEG):
            cols = slice(j * SEG, (j + 1) * SEG)
            proj_ref[:, cols] = _dot(h, w_ref[:, cols])

    return pl.pallas_call(
        body,
        name="inproj",
        grid=(s // tm,),
        in_specs=[
            pl.BlockSpec((tm, D_MODEL), lambda i: (i, 0)),
            pl.BlockSpec((1, D_MODEL), lambda i: (0, 0)),
            _resident((D_MODEL, PROJ_W)),
        ],
        out_specs=[
            pl.BlockSpec((tm, PROJ_W), lambda i: (i, 0)),
            pl.BlockSpec((tm, D_MODEL), lambda i: (i, 0)),
        ],
        out_shape=[
            jax.ShapeDtypeStruct((s, PROJ_W), F32),
            jax.ShapeDtypeStruct((s, D_MODEL), MXU_DTYPE),
        ],
        compiler_params=_params(),
    )(x, norm_g, w_in_p)


def _chunk_lower_mask(t):
    row = lax.broadcasted_iota(jnp.int32, (t, t), 0)
    col = lax.broadcasted_iota(jnp.int32, (t, t), 1)
    return ((row // HG_CHUNK) == (col // HG_CHUNK)) & (col <= row)


def _chunk_pos(t):
    return lax.broadcasted_iota(jnp.int32, (t, HEAD_DIM), 0) & (HG_CHUNK - 1)


def _cumsum_chunk(x, pos):
    sh = 1
    while sh < HG_CHUNK:
        x = x + jnp.where(pos >= sh, pltpu.roll(x, sh, 0), 0.0)
        sh *= 2
    return x


def _rcumsum_chunk(x, pos):
    t = x.shape[0]
    sh = 1
    while sh < HG_CHUNK:
        x = x + jnp.where(pos < HG_CHUNK - sh, pltpu.roll(x, t - sh, 0), 0.0)
        sh *= 2
    return x


def _chunk_total(x):
    t, w = x.shape
    tot = jnp.sum(x.reshape(t // HG_CHUNK, HG_CHUNK, w), axis=1, keepdims=True)
    return jnp.broadcast_to(tot, (t // HG_CHUNK, HG_CHUNK, w)).reshape(t, w)


def _hgrn_gates(hq, hf, lb_logits, pos):
    lb = _sigmoid(lb_logits[0:1, :] - lb_logits[1:2, :])
    sig = _sigmoid(hf)
    f = lb + (1.0 - lb) * sig
    sq = _sigmoid(hq)
    q = hq * sq
    k = 1.0 - f
    logf = jnp.log(f)
    bcum = _cumsum_chunk(logf, pos)
    blast = _chunk_total(logf)
    eb = jnp.exp(bcum)
    enb = jnp.exp(-bcum)
    eo = jnp.exp(blast - bcum)
    return dict(lb=lb, sig=sig, f=f, sq=sq, q=q, k=k, eb=eb, enb=enb, eo=eo,
                qi=q * eb, ki=k * enb, ko=k * eo, dec=jnp.exp(blast))


def _hgrn_fwd(proj, lb_logits, hg_norm_g):
    s = proj.shape[0]
    t = min(s, T_HGRN)
    nb = s // t
    nc = t // HG_CHUNK
    hw = HG_HEADS_PER_STEP * HEAD_DIM

    def body(hq_ref, hf_ref, hi_ref, hz_ref, lb_ref, g_ref, o_ref, ya_ref, st_ref, state, u_sc, stb_sc):
        b = pl.program_id(1)

        @pl.when(b == 0)
        def _():
            state[...] = jnp.zeros_like(state)

        lower = _chunk_lower_mask(t)
        pos = _chunk_pos(t)
        for hh in range(HG_HEADS_PER_STEP):
            cols = slice(hh * HEAD_DIM, (hh + 1) * HEAD_DIM)
            st = state[hh]
            st_ref[0, hh] = st
            gt = _hgrn_gates(hq_ref[:, cols], hf_ref[:, cols], lb_ref[:, cols], pos)
            vb = _mx(hi_ref[:, cols])
            qib, kib, kob = _mx(gt["qi"]), _mx(gt["ki"]), _mx(gt["ko"])
            a = jnp.where(lower, _dot_nt(qib, kib), 0.0)
            o_intra = _dot(_mx(a), vb)
            for c in range(nc):
                sl = slice(c * HG_CHUNK, (c + 1) * HG_CHUNK)
                u_sc[hh, c] = _dot_tn(vb[sl], kob[sl])
            for c in range(nc):
                stb_sc[hh, c] = _mx(st)
                st = st * gt["dec"][c * HG_CHUNK:c * HG_CHUNK + 1, :] + u_sc[hh, c]
            state[hh] = st
            outs = []
            for c in range(nc):
                sl = slice(c * HG_CHUNK, (c + 1) * HG_CHUNK)
                outs.append(o_intra[sl] + _dot_nt(qib[sl], stb_sc[hh, c]))
            o = jnp.concatenate(outs, axis=0)
            o_ref[:, cols] = o
            r = lax.rsqrt(jnp.mean(o * o, axis=-1, keepdims=True) + EPS)
            hz = hz_ref[:, cols]
            ya_ref[:, cols] = _mx((o * r * g_ref[...]) * (hz * _sigmoid(hz)))

    hsteps = HEADS // HG_HEADS_PER_STEP

    def seg(k):
        return pl.BlockSpec((t, hw), lambda h, b, k=k: (b, k * hsteps + h))

    return pl.pallas_call(
        body,
        name="hgrn_fwd",
        grid=(hsteps, nb),
        in_specs=[seg(0), seg(1), seg(2), seg(3),
                  pl.BlockSpec((2, hw), lambda h, b: (0, h)),
                  pl.BlockSpec((1, HEAD_DIM), lambda h, b: (0, 0))],
        out_specs=[
            pl.BlockSpec((t, hw), lambda h, b: (b, h)),
            pl.BlockSpec((t, hw), lambda h, b: (b, h)),
            pl.BlockSpec((1, HG_HEADS_PER_STEP, HEAD_DIM, HEAD_DIM), lambda h, b: (b, h, 0, 0)),
        ],
        out_shape=[
            jax.ShapeDtypeStruct((s, D_MODEL), F32),
            jax.ShapeDtypeStruct((s, D_MODEL), MXU_DTYPE),
            jax.ShapeDtypeStruct((nb, HEADS, HEAD_DIM, HEAD_DIM), F32),
        ],
        scratch_shapes=[pltpu.VMEM((HG_HEADS_PER_STEP, HEAD_DIM, HEAD_DIM), F32),
                        pltpu.VMEM((HG_HEADS_PER_STEP, nc, HEAD_DIM, HEAD_DIM), F32),
                        pltpu.VMEM((HG_HEADS_PER_STEP, nc, HEAD_DIM, HEAD_DIM), MXU_DTYPE)],
        compiler_params=_params(),
    )(proj, proj, proj, proj, lb_logits, hg_norm_g)


def _rope(x, cos, sin_a, sin_b):
    return x * cos + pltpu.roll(x, 96, 1) * sin_a + pltpu.roll(x, 32, 1) * sin_b


def _rope_t(d, cos, sin_a, sin_b):
    return d * cos + pltpu.roll(d * sin_a, 32, 1) + pltpu.roll(d * sin_b, 96, 1)


def _mla_prep(proj, q_a_g, kv_a_g, w_uq_p, w_kn, w_v, cos, sin_a, sin_b):
    s = proj.shape[0]
    tm = min(s, TM_ROW)

    def body(sm_ref, gq_ref, gk_ref, wq_ref, wkn_ref, wv_ref, cos_ref, sa_ref, sb_ref,
             q_ref, k_ref, v_ref, cqn_ref, ckvn_ref):
        small = sm_ref[...]
        cq = small[:, :Q_LORA]
        ckv = small[:, Q_LORA:Q_LORA + KV_LORA]
        krp = small[:, Q_LORA + KV_LORA:Q_LORA + KV_LORA + HEAD_DIM]
        rq = lax.rsqrt(jnp.mean(cq * cq, axis=-1, keepdims=True) + EPS)
        rk = lax.rsqrt(jnp.mean(ckv * ckv, axis=-1, keepdims=True) + EPS)
        cqn = _mx(cq * rq * gq_ref[...])
        ckvn = _mx(ckv * rk * gk_ref[...])
        cqn_ref[...] = cqn
        ckvn_ref[...] = ckvn
        q = _dot(cqn, wq_ref[...]) * Q_PRESCALE
        kn = _dot(ckvn, wkn_ref[...])
        v = _dot(ckvn, wv_ref[...])
        cos_t, sa, sb = cos_ref[...], sa_ref[...], sb_ref[...]
        kpe = _mx(_rope(krp, cos_t, sa, sb))
        ones_col = (lax.broadcasted_iota(jnp.int32, (tm, HEAD_DIM), 1) == 0).astype(MXU_DTYPE)
        for h in range(HEADS):
            lo = h * HEAD_PAD
            v_ref[:, lo:lo + HEAD_DIM] = _mx(v[:, h * HEAD_DIM:(h + 1) * HEAD_DIM])
            v_ref[:, lo + HEAD_DIM:lo + HEAD_PAD] = ones_col
            q_ref[:, lo:lo + HEAD_DIM] = _mx(q[:, lo:lo + HEAD_DIM])
            q_ref[:, lo + HEAD_DIM:lo + HEAD_PAD] = _mx(_rope(q[:, lo + HEAD_DIM:lo + HEAD_PAD], cos_t, sa, sb))
            k_ref[:, lo:lo + HEAD_DIM] = _mx(kn[:, h * HEAD_DIM:(h + 1) * HEAD_DIM])
            k_ref[:, lo + HEAD_DIM:lo + HEAD_PAD] = kpe

    def const(shape):
        return pl.BlockSpec(shape, lambda i: (0, 0))

    def rows(w):
        return pl.BlockSpec((tm, w), lambda i: (i, 0))

    return pl.pallas_call(
        body,
        name="mla_prep",
        grid=(s // tm,),
        in_specs=[
            pl.BlockSpec((tm, SEG), lambda i: (i, SMALL_SEG)),
            const((1, Q_LORA)), const((1, KV_LORA)),
            const((Q_LORA, HEADS * HEAD_PAD)), const((KV_LORA, D_MODEL)), const((KV_LORA, D_MODEL)),
            rows(HEAD_DIM), rows(HEAD_DIM), rows(HEAD_DIM),
        ],
        out_specs=[rows(HEADS * HEAD_PAD)] * 3 + [rows(Q_LORA), rows(KV_LORA)],
        out_shape=[
            jax.ShapeDtypeStruct((s, HEADS * HEAD_PAD), MXU_DTYPE),
            jax.ShapeDtypeStruct((s, HEADS * HEAD_PAD), MXU_DTYPE),
            jax.ShapeDtypeStruct((s, HEADS * HEAD_PAD), MXU_DTYPE),
            jax.ShapeDtypeStruct((s, Q_LORA), MXU_DTYPE),
            jax.ShapeDtypeStruct((s, KV_LORA), MXU_DTYPE),
        ],
        compiler_params=_params(),
    )(proj, q_a_g, kv_a_g, w_uq_p, w_kn, w_v, cos, sin_a, sin_b)


def _diag_mask(t):
    row = lax.broadcasted_iota(jnp.int32, (t, t), 0)
    col = lax.broadcasted_iota(jnp.int32, (t, t), 1)
    return row >= col


def _flash_fwd(q_all, k_all, v_all, proj):
    s = q_all.shape[0]
    t = min(s, T_ATT)
    n = s // t

    ts = t // ATT_SUB

    def body(q_ref, k_ref, v_ref, mz_ref, ao_ref, lse_ref, yb_ref, m_sc, acc_sc):
        qi = pl.program_id(1)
        m_sc[...] = jnp.full_like(m_sc, -jnp.inf)
        acc_sc[...] = jnp.zeros_like(acc_sc)

        def update(r, sc, v_blk):
            rs = slice(r * ts, (r + 1) * ts)
            m_prev = m_sc[rs]
            m_new = jnp.maximum(m_prev, jnp.max(sc, axis=-1, keepdims=True))
            p = jnp.exp2(sc - m_new)
            acc_sc[rs] = jnp.exp2(m_prev - m_new) * acc_sc[rs] + _dot(_mx(p), v_blk)
            m_sc[rs] = m_new

        def below_diagonal(ki, carry):
            rows = pl.ds(pl.multiple_of(ki * t, t), t)
            for r in range(ATT_SUB):
                update(r, _dot_nt(q_ref[r * ts:(r + 1) * ts], k_ref[rows, :]), v_ref[rows, :])
            return carry

        lax.fori_loop(0, qi, below_diagonal, 0)
        base = pl.multiple_of(qi * t, t)
        for r in range(ATT_SUB):
            w = (r + 1) * ts
            rows = pl.ds(base, w)
            row = lax.broadcasted_iota(jnp.int32, (ts, w), 0) + r * ts
            col = lax.broadcasted_iota(jnp.int32, (ts, w), 1)
            sc = _dot_nt(q_ref[r * ts:(r + 1) * ts], k_ref[rows, :])
            update(r, jnp.where(row >= col, sc, -jnp.inf), v_ref[rows, :])

        acc = acc_sc[...]
        l = acc[:, HEAD_DIM:HEAD_DIM + 1]
        ao = acc[:, :HEAD_DIM] / l
        ao_ref[...] = ao
        lse_ref[...] = jnp.broadcast_to(m_sc[...] + jnp.log2(l), (t, HEAD_DIM))
        mz = mz_ref[...]
        yb_ref[...] = _mx(ao * (mz * _sigmoid(mz)))

    q_map = lambda h, qi: (qi, h)
    return pl.pallas_call(
        body,
        name="flash_fwd",
        grid=(HEADS, n),
        in_specs=[
            pl.BlockSpec((t, HEAD_PAD), q_map),
            pl.BlockSpec((s, HEAD_PAD), lambda h, qi: (0, h)),
            pl.BlockSpec((s, HEAD_PAD), lambda h, qi: (0, h)),
            pl.BlockSpec((t, HEAD_DIM), lambda h, qi: (qi, MZ_SEG * HEADS + h)),
        ],
        out_specs=[pl.BlockSpec((t, HEAD_DIM), q_map)] * 3,
        out_shape=[
            jax.ShapeDtypeStruct((s, D_MODEL), F32),
            jax.ShapeDtypeStruct((s, D_MODEL), F32),
            jax.ShapeDtypeStruct((s, D_MODEL), MXU_DTYPE),
        ],
        scratch_shapes=[
            pltpu.VMEM((t, 1), F32),
            pltpu.VMEM((t, HEAD_PAD), F32),
        ],
        compiler_params=_params(),
    )(q_all, k_all, v_all, proj)


def _merge_fwd_bwd(x, target, ya, yb, proj, b_gate, final_g, w_pa, w_pb, w_out):
    s = x.shape[0]
    tm = min(s, TM_ROW)

    def body(x_ref, t_ref, ya_ref, yb_ref, g0_ref, g1_ref, bg_ref, fg_ref, wpa_ref, wpb_ref, wo_ref,
             dx2_ref, dya_ref, dyb_ref, dg0_ref, dg1_ref, mb_ref, dpab_ref, dpbb_ref, dx2b_ref,
             loss_ref, dfg_ref, dbg_ref):
        i = pl.program_id(0)

        @pl.when(i == 0)
        def _():
            loss_ref[...] = jnp.zeros_like(loss_ref)
            dfg_ref[...] = jnp.zeros_like(dfg_ref)
            dbg_ref[...] = jnp.zeros_like(dbg_ref)

        pa = _dot(ya_ref[...], wpa_ref[...])
        pb = _dot(yb_ref[...], wpb_ref[...])
        bg = bg_ref[...]
        g0 = _sigmoid(g0_ref[...] + bg[:, :D_MODEL])
        g1 = _sigmoid(g1_ref[...] + bg[:, D_MODEL:])
        merged = g0 * pa + g1 * pb
        mb = _mx(merged)
        mb_ref[...] = mb
        x2 = x_ref[...] + _dot(mb, wo_ref[...])
        r = lax.rsqrt(jnp.mean(x2 * x2, axis=-1, keepdims=True) + EPS)
        xn = x2 * r
        fg = fg_ref[...]
        diff = xn * fg - t_ref[...]
        loss_ref[...] += 0.5 * jnp.sum(jnp.mean(diff * diff, axis=-1, keepdims=True))
        dy = diff * (1.0 / D_MODEL)
        dfg_ref[...] += _bcast_rows(jnp.sum(dy * xn, axis=0, keepdims=True), 8)
        tt = dy * fg
        dx2 = r * (tt - xn * jnp.mean(tt * xn, axis=-1, keepdims=True))
        dx2_ref[...] = dx2
        dx2b = _mx(dx2)
        dx2b_ref[...] = dx2b
        dmerged = _dot_nt(dx2b, wo_ref[...])
        dpa = _mx(dmerged * g0)
        dpb = _mx(dmerged * g1)
        dpab_ref[...] = dpa
        dpbb_ref[...] = dpb
        dg0 = dmerged * pa * (g0 * (1.0 - g0))
        dg1 = dmerged * pb * (g1 * (1.0 - g1))
        dg0_ref[...] = _mx(dg0)
        dg1_ref[...] = _mx(dg1)
        dbg_ref[:, :D_MODEL] += _bcast_rows(jnp.sum(dg0, axis=0, keepdims=True), 8)
        dbg_ref[:, D_MODEL:] += _bcast_rows(jnp.sum(dg1, axis=0, keepdims=True), 8)
        dya_ref[...] = _dot_nt(dpa, wpa_ref[...])
        dyb_ref[...] = _dot_nt(dpb, wpb_ref[...])

    def rows(w=D_MODEL):
        return pl.BlockSpec((tm, w), lambda i: (i, 0))

    def const(shape):
        return pl.BlockSpec(shape, lambda i: (0, 0))

    f32 = jax.ShapeDtypeStruct((s, D_MODEL), F32)
    b16 = jax.ShapeDtypeStruct((s, D_MODEL), MXU_DTYPE)
    return pl.pallas_call(
        body,
        name="merge_fwd_bwd",
        grid=(s // tm,),
        in_specs=[
            rows(), rows(), rows(), rows(),
            pl.BlockSpec((tm, SEG), lambda i: (i, GL_SEG)),
            pl.BlockSpec((tm, SEG), lambda i: (i, GL_SEG + 1)),
            const((1, 2 * D_MODEL)), const((1, D_MODEL)),
            const((D_MODEL, D_MODEL)), const((D_MODEL, D_MODEL)), const((D_MODEL, D_MODEL)),
        ],
        out_specs=[rows()] * 9 + [const((8, HEAD_DIM)), const((8, D_MODEL)), const((8, 2 * D_MODEL))],
        out_shape=[f32, f32, f32, b16, b16, b16, b16, b16, b16,
                   jax.ShapeDtypeStruct((8, HEAD_DIM), F32),
                   jax.ShapeDtypeStruct((8, D_MODEL), F32),
                   jax.ShapeDtypeStruct((8, 2 * D_MODEL), F32)],
        compiler_params=_params(),
    )(x, target, ya, yb, proj, proj, b_gate, final_g, w_pa, w_pb, w_out)


def _attn_gate_bwd(dyb, ao, proj):
    s = dyb.shape[0]
    tm = min(s, TM_ROW)

    def body(dyb_ref, ao_ref, mz_ref, dao_ref, dmz_ref):
        mz = mz_ref[...]
        sg = _sigmoid(mz)
        d = dyb_ref[...]
        dao_ref[...] = _mx(d * (mz * sg))
        dmz_ref[...] = _mx(d * ao_ref[...] * (sg + mz * sg * (1.0 - sg)))

    rows = pl.BlockSpec((tm, D_MODEL), lambda i: (i, 0))
    b16 = jax.ShapeDtypeStruct((s, D_MODEL), MXU_DTYPE)
    return pl.pallas_call(
        body,
        name="attn_gate_bwd",
        grid=(s // tm,),
        in_specs=[rows, rows, pl.BlockSpec((tm, SEG), lambda i: (i, MZ_SEG))],
        out_specs=[rows, rows],
        out_shape=[b16, b16],
        compiler_params=_params(),
    )(dyb, ao, proj)


def _flash_bwd(q_all, k_all, v_all, dao, ao, lse):
    s = q_all.shape[0]
    t = min(s, T_ATT_BWD)
    n = s // t
    ts = t // ATT_SUB
    pairs =[(ki, qi) for ki in range(n) for qi in range(ki, n)]
    ki_list = jnp.asarray([p[0] for p in pairs], jnp.int32)
    qi_list = jnp.asarray([p[1] for p in pairs], jnp.int32)

    def body(ki_ref, qi_ref, q_ref, k_ref, v_ref, do_ref, ao_ref, lse_ref, dq_ref, dk_ref, dv_ref, dk_acc, dv_acc):
        step = pl.program_id(1)
        ki, qi = ki_ref[step], qi_ref[step]

        @pl.when(qi == ki)
        def _():
            dk_acc[...] = jnp.zeros_like(dk_acc)
            dv_acc[...] = jnp.zeros_like(dv_acc)

        @pl.when(ki == 0)
        def _():
            dq_ref[pl.ds(pl.multiple_of(qi * t, t), t), :] = jnp.zeros((t, HEAD_PAD), F32)

        def pair(masked):
            dk_parts, dv_parts = [], []
            for r in range(ATT_SUB):
                rs = slice(r * ts, (r + 1) * ts)
                w = (r + 1) * ts if masked else t
                k = k_ref[:w]
                v = v_ref[:w, :HEAD_DIM]
                q = q_ref[rs]
                sc = _dot_nt(q, k)
                if masked:
                    row = lax.broadcasted_iota(jnp.int32, (ts, w), 0) + r * ts
                    col = lax.broadcasted_iota(jnp.int32, (ts, w), 1)
                    sc = jnp.where(row >= col, sc, -jnp.inf)
                p = jnp.exp2(sc - lse_ref[rs, 0:1])
                do = do_ref[rs]
                delta = jnp.sum(do.astype(F32) * ao_ref[rs], axis=-1, keepdims=True)
                dv_part = _dot_tn(_mx(p), do)
                ds = _mx(p * (_dot_nt(do, v) - delta))
                dk_part = _dot_tn(ds, q)
                rows = pl.ds(pl.multiple_of(qi * t + r * ts, ts), ts)
                dq_ref[rows, :] += _dot(ds, k)
                if masked:
                    dk_acc[:w] += dk_part
                    dv_acc[:w] += dv_part
                else:
                    dk_parts.append(dk_part)
                    dv_parts.append(dv_part)

            if not masked:
                dk_acc[...] += sum(dk_parts[1:], dk_parts[0])
                dv_acc[...] += sum(dv_parts[1:], dv_parts[0])

        @pl.when(qi == ki)
        def _():
            pair(True)

        @pl.when(qi > ki)
        def _():
            pair(False)

        @pl.when(qi == n - 1)
        def _():
            dk_ref[...] = dk_acc[...] * LN2
            dv_ref[...] = dv_acc[...]

    q_map = lambda h, p, ki_ref, qi_ref: (qi_ref[p], h)
    kv_map = lambda h, p, ki_ref, qi_ref: (ki_ref[p], h)
    grid_spec = pltpu.PrefetchScalarGridSpec(
        num_scalar_prefetch=2,
        grid=(HEADS, len(pairs)),
        in_specs=[
            pl.BlockSpec((t, HEAD_PAD), q_map),
            pl.BlockSpec((t, HEAD_PAD), kv_map),
            pl.BlockSpec((t, HEAD_PAD), kv_map),
            pl.BlockSpec((t, HEAD_DIM), q_map),
            pl.BlockSpec((t, HEAD_DIM), q_map),
            pl.BlockSpec((t, HEAD_DIM), q_map),
        ],
        out_specs=[
            pl.BlockSpec((s, HEAD_PAD), lambda h, p, ki_ref, qi_ref: (0, h)),
            pl.BlockSpec((t, HEAD_PAD), kv_map),
            pl.BlockSpec((t, HEAD_DIM), kv_map),
        ],
        scratch_shapes=[pltpu.VMEM((t, HEAD_PAD), F32), pltpu.VMEM((t, HEAD_DIM), F32)],
    )
    return pl.pallas_call(
        body,
        name="flash_bwd",
        grid_spec=grid_spec,
        out_shape=[
            jax.ShapeDtypeStruct((s, HEADS * HEAD_PAD), F32),
            jax.ShapeDtypeStruct((s, HEADS * HEAD_PAD), F32),
            jax.ShapeDtypeStruct((s, D_MODEL), F32),
        ],
        compiler_params=_params(VMEM_LIMIT_BIG),
    )(ki_list, qi_list, q_all, k_all, v_all, dao, ao, lse)


def _mla_prep_bwd(dq_all, dk_all, dv_all, proj, q_a_g, kv_a_g, w_uq_p, w_kn, w_v, cos, sin_a, sin_b):
    s = proj.shape[0]
    tm = min(s, TM_ROW)

    def body(dq_ref, dk_ref, dv_ref, sm_ref, gq_ref, gk_ref, wq_ref, wkn_ref, wv_ref, cos_ref, sa_ref, sb_ref,
             dsm_ref, dqf_ref, dkn_ref, dvb_ref, dgq_ref, dgk_ref):
        i = pl.program_id(0)

        @pl.when(i == 0)
        def _():
            dgq_ref[...] = jnp.zeros_like(dgq_ref)
            dgk_ref[...] = jnp.zeros_like(dgk_ref)

        cos_t, sa, sb = cos_ref[...], sa_ref[...], sb_ref[...]
        dkpe = jnp.zeros((tm, HEAD_DIM), F32)
        for h in range(HEADS):
            lo = h * HEAD_PAD
            dqf_ref[:, lo:lo + HEAD_DIM] = _mx(dq_ref[:, lo:lo + HEAD_DIM] * QK_SCALE)
            dqf_ref[:, lo + HEAD_DIM:lo + HEAD_PAD] = _mx(
                _rope_t(dq_ref[:, lo + HEAD_DIM:lo + HEAD_PAD] * QK_SCALE, cos_t, sa, sb))
            dkn_ref[:, h * HEAD_DIM:(h + 1) * HEAD_DIM] = _mx(dk_ref[:, lo:lo + HEAD_DIM])
            dkpe = dkpe + dk_ref[:, lo + HEAD_DIM:lo + HEAD_PAD]
        dkr = _rope_t(dkpe, cos_t, sa, sb)
        dvb = _mx(dv_ref[...])
        dvb_ref[...] = dvb
        dcqn = _dot_nt(dqf_ref[...], wq_ref[...])
        dckvn = _dot_nt(dkn_ref[...], wkn_ref[...]) + _dot_nt(dvb, wv_ref[...])

        small = sm_ref[...]
        cq = small[:, :Q_LORA]
        ckv = small[:, Q_LORA:Q_LORA + KV_LORA]
        rq = lax.rsqrt(jnp.mean(cq * cq, axis=-1, keepdims=True) + EPS)
        rk = lax.rsqrt(jnp.mean(ckv * ckv, axis=-1, keepdims=True) + EPS)
        cqh = cq * rq
        ckh = ckv * rk
        dgq_ref[...] += _bcast_rows(jnp.sum(dcqn * cqh, axis=0, keepdims=True), 8)
        dgk_ref[...] += _bcast_rows(jnp.sum(dckvn * ckh, axis=0, keepdims=True), 8)
        tq = dcqn * gq_ref[...]
        tk = dckvn * gk_ref[...]
        dcq = rq * (tq - cqh * jnp.mean(tq * cqh, axis=-1, keepdims=True))
        dckv = rk * (tk - ckh * jnp.mean(tk * ckh, axis=-1, keepdims=True))
        dsm_ref[:, :Q_LORA] = _mx(dcq)
        dsm_ref[:, Q_LORA:Q_LORA + KV_LORA] = _mx(dckv)
        dsm_ref[:, Q_LORA + KV_LORA:Q_LORA + KV_LORA + HEAD_DIM] = _mx(dkr)
        dsm_ref[:, Q_LORA + KV_LORA + HEAD_DIM:] = jnp.zeros((tm, SEG - Q_LORA - KV_LORA - HEAD_DIM), MXU_DTYPE)

    def const(shape):
        return pl.BlockSpec(shape, lambda i: (0, 0))

    def rows(w):
        return pl.BlockSpec((tm, w), lambda i: (i, 0))

    return pl.pallas_call(
        body,
        name="mla_prep_bwd",
        grid=(s // tm,),
        in_specs=[
            rows(HEADS * HEAD_PAD), rows(HEADS * HEAD_PAD), rows(D_MODEL),
            pl.BlockSpec((tm, SEG), lambda i: (i, SMALL_SEG)),
            const((1, Q_LORA)), const((1, KV_LORA)),
            const((Q_LORA, HEADS * HEAD_PAD)), const((KV_LORA, D_MODEL)), const((KV_LORA, D_MODEL)),
            rows(HEAD_DIM), rows(HEAD_DIM), rows(HEAD_DIM),
        ],
        out_specs=[rows(SEG), rows(HEADS * HEAD_PAD), rows(D_MODEL), rows(D_MODEL),
                   const((8, Q_LORA)), const((8, KV_LORA))],
        out_shape=[
            jax.ShapeDtypeStruct((s, SEG), MXU_DTYPE),
            jax.ShapeDtypeStruct((s, HEADS * HEAD_PAD), MXU_DTYPE),
            jax.ShapeDtypeStruct((s, D_MODEL), MXU_DTYPE),
            jax.ShapeDtypeStruct((s, D_MODEL), MXU_DTYPE),
            jax.ShapeDtypeStruct((8, Q_LORA), F32),
            jax.ShapeDtypeStruct((8, KV_LORA), F32),
        ],
        compiler_params=_params(),
    )(dq_all, dk_all, dv_all, proj, q_a_g, kv_a_g, w_uq_p, w_kn, w_v, cos, sin_a, sin_b)


def _hgrn_bwd(proj, lb_logits, hg_norm_g, o_all, dya, states):
    s = proj.shape[0]
    t = min(s, T_HGRN)
    nb = s // t
    nc = t // HG_CHUNK

    def body(hq_ref, hf_ref, hi_ref, hz_ref, lb_ref, g_ref, o_ref, dya_ref, st_ref,
             dhq_ref, dhf_ref, dhi_ref, dhz_ref, dlb_ref, dg_ref, dstate, u_sc, g_sc, stf_sc, stb_sc, dstb_sc):
        h, b = pl.program_id(0), pl.program_id(1)

        @pl.when(b == 0)
        def _():
            dstate[...] = jnp.zeros_like(dstate)
            dlb_ref[...] = jnp.zeros_like(dlb_ref)

        @pl.when((b == 0) & (h == 0))
        def _():
            dg_ref[...] = jnp.zeros_like(dg_ref)

        lower = _chunk_lower_mask(t)
        pos = _chunk_pos(t)
        ghg = g_ref[...]
        for hh in range(HG_HEADS_PER_STEP):
            cols = slice(hh * HEAD_DIM, (hh + 1) * HEAD_DIM)
            hq, hf, hz = hq_ref[:, cols], hf_ref[:, cols], hz_ref[:, cols]
            gt = _hgrn_gates(hq, hf, lb_ref[:, cols], pos)
            vb = _mx(hi_ref[:, cols])
            qi, ki, ko = gt["qi"], gt["ki"], gt["ko"]
            qib, kib, kob = _mx(qi), _mx(ki), _mx(ko)

            o = o_ref[:, cols]
            sz = _sigmoid(hz)
            r = lax.rsqrt(jnp.mean(o * o, axis=-1, keepdims=True) + EPS)
            on = o * r
            dya_t = dya_ref[:, cols]
            don = dya_t * (hz * sz)
            dhz_ref[:, cols] = _mx(dya_t * (on * ghg) * (sz + hz * sz * (1.0 - sz)))
            dg_ref[...] += _bcast_rows(jnp.sum(don * on, axis=0, keepdims=True), 8)
            tt = don * ghg
            do = r * (tt - on * jnp.mean(tt * on, axis=-1, keepdims=True))
            dob = _mx(do)

            for c in range(nc):
                sl = slice(c * HG_CHUNK, (c + 1) * HG_CHUNK)
                u_sc[hh, c] = _dot_tn(vb[sl], kob[sl])
                g_sc[hh, c] = _dot_tn(dob[sl], qib[sl])

            st = st_ref[0, hh]
            for c in range(nc):
                stf_sc[hh, c] = st
                stb_sc[hh, c] = _mx(st)
                if c < nc - 1:
                    st = st * gt["dec"][c * HG_CHUNK:c * HG_CHUNK + 1, :] + u_sc[hh, c]

            dst = dstate[hh]
            dd_parts = [None] * nc
            for c in reversed(range(nc)):
                dec = gt["dec"][c * HG_CHUNK:c * HG_CHUNK + 1, :]
                dstb_sc[hh, c] = _mx(dst)
                dd_parts[c] = _bcast_rows(jnp.sum(dst * stf_sc[hh, c], axis=0, keepdims=True) * dec, HG_CHUNK)
                dst = dst * dec + g_sc[hh, c]
            dstate[hh] = dst

            a = jnp.where(lower, _dot_nt(qib, kib), 0.0)
            da = _mx(jnp.where(lower, _dot_nt(dob, vb), 0.0))
            dqi_intra = _dot(da, kib)
            dki = _dot_tn(da, qib)
            dv_intra = _dot_tn(_mx(a), dob)

            dqi_parts, dko_parts, dv_parts = [None] * nc, [None] * nc, [None] * nc
            for c in range(nc):
                sl = slice(c * HG_CHUNK, (c + 1) * HG_CHUNK)
                dv_parts[c] = dv_intra[sl] + _dot_nt(kob[sl], dstb_sc[hh, c])
                dko_parts[c] = _dot(vb[sl], dstb_sc[hh, c])
                dqi_parts[c] = dqi_intra[sl] + _dot(dob[sl], stb_sc[hh, c])
            dqi = jnp.concatenate(dqi_parts, axis=0)
            dko = jnp.concatenate(dko_parts, axis=0)
            dv = jnp.concatenate(dv_parts, axis=0)
            dd = jnp.concatenate(dd_parts, axis=0)

            dq = dqi * gt["eb"]
            dk = dki * gt["enb"] + dko * gt["eo"]
            db = dqi * qi - dki * ki - dko * ko
            dlogf = _rcumsum_chunk(db, pos) + _chunk_total(dko * ko) + dd
            df = dlogf / gt["f"] - dk
            lb, sig, sq = gt["lb"], gt["sig"], gt["sq"]
            dhf_ref[:, cols] = _mx(df * (1.0 - lb) * (sig * (1.0 - sig)))
            dhq_ref[:, cols] = _mx(dq * (sq + hq * sq * (1.0 - sq)))
            dhi_ref[:, cols] = _mx(dv)
            dlb = jnp.sum(df * (1.0 - sig), axis=0, keepdims=True) * (lb * (1.0 - lb))
            dlb_ref[:, cols] += jnp.concatenate([dlb, -dlb], axis=0)

    hw = HG_HEADS_PER_STEP * HEAD_DIM
    hsteps = HEADS // HG_HEADS_PER_STEP

    def seg(k):
        return pl.BlockSpec((t, hw), lambda h, b, k=k: (nb - 1 - b, k * hsteps + h))

    blk = pl.BlockSpec((t, hw), lambda h, b: (nb - 1 - b, h))
    b16 = jax.ShapeDtypeStruct((s, D_MODEL), MXU_DTYPE)
    return pl.pallas_call(
        body,
        name="hgrn_bwd",
        grid=(hsteps, nb),
        in_specs=[seg(0), seg(1), seg(2), seg(3),
                  pl.BlockSpec((2, hw), lambda h, b: (0, h)),
                  pl.BlockSpec((1, HEAD_DIM), lambda h, b: (0, 0)),
                  blk, blk,
                  pl.BlockSpec((1, HG_HEADS_PER_STEP, HEAD_DIM, HEAD_DIM), lambda h, b: (nb - 1 - b, h, 0, 0))],
        out_specs=[blk, blk, blk, blk,
                   pl.BlockSpec((2, hw), lambda h, b: (0, h)),
                   pl.BlockSpec((8, HEAD_DIM), lambda h, b: (0, 0))],
        out_shape=[b16, b16, b16, b16,
                   jax.ShapeDtypeStruct((2, D_MODEL), F32),
                   jax.ShapeDtypeStruct((8, HEAD_DIM), F32)],
        scratch_shapes=[pltpu.VMEM((HG_HEADS_PER_STEP, HEAD_DIM, HEAD_DIM), F32)]
        + [pltpu.VMEM((HG_HEADS_PER_STEP, nc, HEAD_DIM, HEAD_DIM), F32)] * 3
        + [pltpu.VMEM((HG_HEADS_PER_STEP, nc, HEAD_DIM, HEAD_DIM), MXU_DTYPE)] * 2,
        compiler_params=_params(),
    )(proj, proj, proj, proj, lb_logits, hg_norm_g, o_all, dya, states)


def _dh_bwd(segs, w_in_p, x, dx2, norm_g):
    s = x.shape[0]
    tm = min(s, TM_ROW)
    nseg = len(segs)

    def body(*refs):
        seg_refs = refs[:nseg]
        w_ref, x_ref, dx2_ref, g_ref, gx_ref, dng_ref, dp_buf = refs[nseg:]
        i = pl.program_id(0)

        @pl.when(i == 0)
        def _():
            dng_ref[...] = jnp.zeros_like(dng_ref)

        for k, sref in enumerate(seg_refs):
            dp_buf[:, k * SEG:(k + 1) * SEG] = sref[...]
        dh = _dot_nt(dp_buf[...], w_ref[...])
        xf = x_ref[...]
        r = lax.rsqrt(jnp.mean(xf * xf, axis=-1, keepdims=True) + EPS)
        xh = xf * r
        dng_ref[...] += _bcast_rows(jnp.sum(dh * xh, axis=0, keepdims=True), 8)
        tt = dh * g_ref[...]
        gx_ref[...] = dx2_ref[...] + r * (tt - xh * jnp.mean(tt * xh, axis=-1, keepdims=True))

    rows = pl.BlockSpec((tm, D_MODEL), lambda i: (i, 0))
    return pl.pallas_call(
        body,
        name="dh_bwd",
        grid=(s // tm,),
        in_specs=[pl.BlockSpec((tm, SEG), lambda i: (i, 0))] * nseg + [
            _resident((D_MODEL, PROJ_W)),
            rows, rows,
            pl.BlockSpec((1, D_MODEL), lambda i: (0, 0)),
        ],
        out_specs=[rows, pl.BlockSpec((8, D_MODEL), lambda i: (0, 0))],
        out_shape=[jax.ShapeDtypeStruct((s, D_MODEL), F32), jax.ShapeDtypeStruct((8, D_MODEL), F32)],
        scratch_shapes=[pltpu.VMEM((tm, PROJ_W), MXU_DTYPE)],
        compiler_params=_params(),
    )(*segs, w_in_p, x, dx2, norm_g)


def _matmul_tn(a, b, name, out_dtype=F32):
    s, m = a.shape
    n = b.shape[1]
    ts = min(s, TS_TN)
    tn = min(n, SEG)
    nk = s // ts

    def body(a_ref, b_ref, o_ref, acc):
        k = pl.program_id(1)
        part = _dot_tn(a_ref[...], b_ref[...])

        @pl.when(k == 0)
        def _():
            acc[...] = part

        @pl.when(k > 0)
        def _():
            acc[...] += part

        @pl.when(k == nk - 1)
        def _():
            o_ref[...] = acc[...].astype(out_dtype)

    return pl.pallas_call(
        body,
        name=name,
        grid=(n // tn, nk),
        in_specs=[pl.BlockSpec((ts, m), lambda j, k: (k, 0)), pl.BlockSpec((ts, tn), lambda j, k: (k, j))],
        out_specs=pl.BlockSpec((m, tn), lambda j, k: (0, j)),
        out_shape=jax.ShapeDtypeStruct((m, n), out_dtype),
        scratch_shapes=[pltpu.VMEM((m, tn), F32)],
        compiler_params=_params(),
    )(a, b)


def _w_in_pieces():
    per = IN_COLS // N_DEV
    pad_at = SMALL_SEG * SEG + Q_LORA + KV_LORA + QK_ROPE
    pieces = []
    for j in range(N_DEV):
        u0, u1 = j * per, (j + 1) * per
        cuts = [u0] + ([pad_at] if u0 < pad_at < u1 else []) + [u1]
        for a, b in zip(cuts[:-1], cuts[1:]):
            pieces.append((j, a - u0, b - u0, a if a < pad_at else a + PROJ_W - IN_COLS))
    return pad_at, pieces


def _assemble_w_in(gathered):
    tr = TM_ROW
    pad_at, pieces = _w_in_pieces()

    def body(in_ref, out_ref):
        out_ref[:, pad_at:pad_at + PROJ_W - IN_COLS] = jnp.zeros((tr, PROJ_W - IN_COLS), gathered.dtype)
        for j, a, b, p0 in pieces:
            out_ref[:, p0:p0 + b - a] = in_ref[j, :, a:b]

    return pl.pallas_call(
        body,
        name="assemble_w_in",
        grid=(D_MODEL // tr,),
        in_specs=[pl.BlockSpec((N_DEV, tr, PACK_COLS), lambda i: (0, i, 0))],
        out_specs=pl.BlockSpec((tr, PROJ_W), lambda i: (i, 0)),
        out_shape=jax.ShapeDtypeStruct((D_MODEL, PROJ_W), gathered.dtype),
        compiler_params=_params(),
    )(gathered)


def _scatter_dw_in(dw_segs):
    tr = TM_ROW
    _, pieces = _w_in_pieces()
    per = IN_COLS // N_DEV
    nseg = len(dw_segs)

    def body(*refs):
        out_ref, buf = refs[nseg:]
        for k in range(nseg):
            buf[:, k * SEG:(k + 1) * SEG] = refs[k][...]
        for j in range(N_DEV):
            out_ref[j, :, per:] = jnp.zeros((tr, PACK_COLS - per), TRANSPORT_DTYPE)
        for j, a, b, p0 in pieces:
            out_ref[j, :, a:b] = buf[:, p0:p0 + b - a].astype(TRANSPORT_DTYPE)

    return pl.pallas_call(
        body,
        name="scatter_dw_in",
        grid=(D_MODEL // tr,),
        in_specs=[pl.BlockSpec((tr, SEG), lambda i: (i, 0))] * nseg,
        out_specs=pl.BlockSpec((N_DEV, tr, PACK_COLS), lambda i: (0, i, 0)),
        out_shape=jax.ShapeDtypeStruct((N_DEV, D_MODEL, PACK_COLS), TRANSPORT_DTYPE),
        scratch_shapes=[pltpu.VMEM((tr, PROJ_W), F32)],
        compiler_params=_params(),
    )(*dw_segs)


def _rope_tables(s):
    inv = ROPE_THETA ** (-jnp.arange(0, QK_ROPE, 2, dtype=F32) / QK_ROPE)
    ang = jnp.arange(s, dtype=F32)[:, None] * inv[None, :]
    cos, sin = jnp.cos(ang), jnp.sin(ang)
    z32 = jnp.zeros_like(cos)
    z64 = jnp.zeros((s, HEAD_DIM - QK_ROPE), F32)
    cos_t = jnp.concatenate([cos, cos, z64], axis=1)
    sin_a = jnp.concatenate([-sin, z32, z64], axis=1)
    sin_b = jnp.concatenate([z32, sin, z64], axis=1)
    return cos_t, sin_a, sin_b


def _pack_misc(w_uq, w_ukv, norm_g, b_gate, lb_logits, hg_norm_g, q_a_g, kv_a_g, final_norm_g, extra):
    misc = jnp.concatenate([hg_norm_g.reshape(-1), q_a_g.reshape(-1), kv_a_g.reshape(-1), extra.reshape(-1),
                            jnp.zeros((PACK_COLS - HEAD_DIM - Q_LORA - KV_LORA - 1,), F32)])
    return jnp.concatenate([w_uq.reshape(ROWS_W_UQ, PACK_COLS), w_ukv.reshape(ROWS_W_UKV, PACK_COLS),
                            norm_g.reshape(1, -1), b_gate.reshape(2, -1), lb_logits.reshape(2, -1),
                            misc.reshape(1, -1), final_norm_g.reshape(1, -1), jnp.zeros((1, PACK_COLS), F32)], axis=0)


def _unpack_misc(p):
    sm = p[ROWS_W_UQ + ROWS_W_UKV:]
    misc = sm[5]
    return dict(
        w_uq=p[:ROWS_W_UQ].reshape(1, Q_LORA, QK_DIM),
        w_ukv=p[ROWS_W_UQ:ROWS_W_UQ + ROWS_W_UKV].reshape(1, KV_LORA, 2 * HEAD_DIM),
        norm_g=sm[0:1], b_gate=sm[1:3].reshape(1, -1), lb_logits=sm[3:5],
        hg_norm_g=misc[None, :HEAD_DIM], q_a_g=misc[None, HEAD_DIM:HEAD_DIM + Q_LORA],
        kv_a_g=misc[None, HEAD_DIM + Q_LORA:HEAD_DIM + Q_LORA + KV_LORA], final_norm_g=sm[6],
        extra=misc[HEAD_DIM + Q_LORA + KV_LORA],
    )


def _weight_shard_buffer(w_in, w_uq, w_ukv, w_pa, w_pb, w_out):
    w_in_pad = jnp.pad(w_in.reshape(D_MODEL, -1), ((0, 0), (0, PACK_COLS - IN_COLS // N_DEV)))
    parts = [w_in_pad] + [a.reshape(-1, PACK_COLS) for a in (w_pa, w_pb, w_out, w_uq, w_ukv)]
    return jnp.concatenate(parts + [jnp.zeros((ROWS_AG - ROWS_AG_USED, PACK_COLS), F32)], axis=0)


def _full_weights(gathered):
    w_in_p = _assemble_w_in(gathered)
    r0 = D_MODEL
    mats = []
    for _ in range(3):
        mats.append(gathered[:, r0:r0 + ROWS_W_PROJ].reshape(D_MODEL, D_MODEL))
        r0 += ROWS_W_PROJ
    w_uq = gathered[:, r0:r0 + ROWS_W_UQ].reshape(N_DEV, Q_LORA, QK_DIM).transpose(1, 0, 2)
    w_uq_p = jnp.concatenate([w_uq, jnp.zeros((Q_LORA, HEADS, HEAD_PAD - QK_DIM), w_uq.dtype)], axis=2)
    w_uq_p = w_uq_p.reshape(Q_LORA, HEADS * HEAD_PAD)
    r0 += ROWS_W_UQ
    w_ukv = gathered[:, r0:r0 + ROWS_W_UKV].reshape(N_DEV, KV_LORA, 2 * HEAD_DIM).transpose(1, 0, 2)
    w_kn = w_ukv[:, :, :HEAD_DIM].reshape(KV_LORA, D_MODEL)
    w_v = w_ukv[:, :, HEAD_DIM:].reshape(KV_LORA, D_MODEL)
    return w_in_p, w_uq_p, w_kn, w_v, mats[0], mats[1], mats[2]


def _grad_slabs(g):
    dw_uq = g["w_uq_p"].reshape(Q_LORA, HEADS, HEAD_PAD)[:, :, :QK_DIM].transpose(1, 0, 2)
    dw_ukv = jnp.concatenate([g["w_kn"].reshape(KV_LORA, HEADS, HEAD_DIM),
                              g["w_v"].reshape(KV_LORA, HEADS, HEAD_DIM)], axis=2).transpose(1, 0, 2)
    misc = [_pack_misc(dw_uq[j], dw_ukv[j], g["norm_g"], g["b_gate"], g["lb_logits"], g["hg_norm_g"],
                       g["q_a_g"], g["kv_a_g"], g["final_norm_g"], g["loss"]) for j in range(N_DEV)]
    mats = [g[n].reshape(N_DEV, ROWS_W_PROJ, PACK_COLS) for n in ("w_pa", "w_pb", "w_out")]
    return [_scatter_dw_in(g["w_in_segs"])] + mats + [jnp.stack(misc)]


def _local_grads(x, target, norm_g, b_gate, lb_logits, hg_norm_g, q_a_g, kv_a_g, final_g,
                 w_in_p, w_uq_p, w_kn, w_v, w_pa, w_pb, w_out):
    s = x.shape[0]
    cos, sin_a, sin_b = _rope_tables(s)
    proj, h = _inproj(x, norm_g, w_in_p)
    o_all, ya, states = _hgrn_fwd(proj, lb_logits, hg_norm_g)
    q_all, k_all, v_all, cqn, ckvn = _mla_prep(proj, q_a_g, kv_a_g, w_uq_p, w_kn, w_v, cos, sin_a, sin_b)
    ao, lse, yb = _flash_fwd(q_all, k_all, v_all, proj)
    (dx2, dya, dyb, dg0, dg1, merged_b, dpa_b, dpb_b, dx2_b,
     loss_acc, dfg_acc, dbg_acc) = _merge_fwd_bwd(x, target, ya, yb, proj, b_gate, final_g, w_pa, w_pb, w_out)
    dao, dmz = _attn_gate_bwd(dyb, ao, proj)
    dq_all, dk_all, dv_all = _flash_bwd(q_all, k_all, v_all, dao, ao, lse)
    dsmall, dqf_b, dkn_b, dv_b, dgq_acc, dgk_acc = _mla_prep_bwd(
        dq_all, dk_all, dv_all, proj, q_a_g, kv_a_g, w_uq_p, w_kn, w_v, cos, sin_a, sin_b)
    dhq, dhf, dhi, dhz, dlb, dhg_acc = _hgrn_bwd(proj, lb_logits, hg_norm_g, o_all, dya, states)
    segs = [dhq, dhf, dhi, dhz, dsmall, dmz, dg0, dg1]
    grad_x, dng_acc = _dh_bwd(segs, w_in_p, x, dx2, norm_g)
    return dict(
        loss=loss_acc[0, 0], grad_x=grad_x,
        norm_g=dng_acc[0:1], b_gate=dbg_acc[0:1], lb_logits=dlb, hg_norm_g=dhg_acc[0:1],
        q_a_g=dgq_acc[0:1], kv_a_g=dgk_acc[0:1], final_norm_g=dfg_acc[0],
        w_in_segs=[_matmul_tn(h, sg, "dw_in_%d" % k) for k, sg in enumerate(segs)],
        w_uq_p=_matmul_tn(cqn, dqf_b, "dw_uq"),
        w_kn=_matmul_tn(ckvn, dkn_b, "dw_kn"),
        w_v=_matmul_tn(ckvn, dv_b, "dw_v"),
        w_pa=_matmul_tn(ya, dpa_b, "dw_pa", TRANSPORT_DTYPE),
        w_pb=_matmul_tn(yb, dpb_b, "dw_pb", TRANSPORT_DTYPE),
        w_out=_matmul_tn(merged_b, dx2_b, "dw_out", TRANSPORT_DTYPE),
    )


def kernel(x, norm_g, w_in, b_gate, lb_logits, hg_norm_g, q_a_g, w_uq, kv_a_g, w_ukv, w_proj_a, w_proj_b, w_out, final_norm_g, loss_target, m_norm_g, m_w_in, m_b_gate, m_lb_logits, m_hg_norm_g, m_q_a_g, m_w_uq, m_kv_a_g, m_w_ukv, m_w_proj_a, m_w_proj_b, m_w_out, m_final_norm_g, v_norm_g, v_w_in, v_b_gate, v_lb_logits, v_hg_norm_g, v_q_a_g, v_w_uq, v_kv_a_g, v_w_ukv, v_w_proj_a, v_w_proj_b, v_w_out, v_final_norm_g):
    zero = jnp.zeros((1,), F32)
    shard = _weight_shard_buffer(w_in, w_uq, w_ukv, w_proj_a, w_proj_b, w_out).astype(MXU_DTYPE)
    full = _full_weights(_all_gather_packed(shard))
    g = _local_grads(x[0], loss_target[0], norm_g, b_gate, lb_logits, hg_norm_g, q_a_g, kv_a_g,
                     final_norm_g.reshape(1, -1), *full)
    recv_in, recv_pa, recv_pb, recv_out, recv_misc = _reduce_scatter_exchange(_grad_slabs(g))

    out_in = _sum_adamw(recv_in, w_in[0], m_w_in[0], v_w_in[0], "adamw_w_in")
    out_pa = _sum_adamw(recv_pa, w_proj_a[0], m_w_proj_a[0], v_w_proj_a[0], "adamw_w_pa")
    out_pb = _sum_adamw(recv_pb, w_proj_b[0], m_w_proj_b[0], v_w_proj_b[0], "adamw_w_pb")
    out_out = _sum_adamw(recv_out, w_out[0], m_w_out[0], v_w_out[0], "adamw_w_out")
    out_misc = _sum_adamw(
        recv_misc,
        _pack_misc(w_uq, w_ukv, norm_g, b_gate, lb_logits, hg_norm_g, q_a_g, kv_a_g, final_norm_g, zero),
        _pack_misc(m_w_uq, m_w_ukv, m_norm_g, m_b_gate, m_lb_logits, m_hg_norm_g, m_q_a_g, m_kv_a_g,
                   m_final_norm_g, zero),
        _pack_misc(v_w_uq, v_w_ukv, v_norm_g, v_b_gate, v_lb_logits, v_hg_norm_g, v_q_a_g, v_kv_a_g,
                   v_final_norm_g, zero),
        "adamw_misc")
    names = ["norm_g", "w_in", "b_gate", "lb_logits", "hg_norm_g", "q_a_g", "w_uq", "kv_a_g", "w_ukv",
             "w_proj_a", "w_proj_b", "w_out", "final_norm_g"]
    kinds = []
    for i in range(4):
        d = _unpack_misc(out_misc[i])
        d.update(w_in=out_in[i][None], w_proj_a=out_pa[i][None], w_proj_b=out_pb[i][None], w_out=out_out[i][None])
        kinds.append(d)
    return (kinds[0]["extra"], g["grad_x"][None], *[d[n] for d in kinds for n in names])
```

```python
import functools

import jax
import jax.numpy as jnp
from jax import lax
from jax.experimental import pallas as pl
from jax.experimental.pallas import tpu as pltpu

D_MODEL = 1024
HEADS = 8
HEAD_DIM = 128
HG_CHUNK = 32
Q_LORA = 384
KV_LORA = 256
QK_ROPE = 64
QK_DIM = 192
ROPE_THETA = 10000.0
EPS = 1e-6
IN_COLS = 7872
ADAM_LR = 0.001
ADAM_B1 = 0.9
ADAM_B2 = 0.999
ADAM_EPS = 1e-08
ADAM_WD = 0.01
ADAM_STEP = 10

N_DEV = 8
SEG = 1024
PROJ_W = 8 * SEG
SMALL_SEG = 4
MZ_SEG = 5
GL_SEG = 6
HEAD_PAD = 256
PACK_COLS = 1024
ROWS_W_UQ = 72
ROWS_W_UKV = 64
ROWS_W_PROJ = 128
ROWS_AG_USED = D_MODEL + 3 * ROWS_W_PROJ + ROWS_W_UQ + ROWS_W_UKV
ROWS_AG = 1552

QK_SCALE = QK_DIM ** -0.5
LOG2E = 1.4426950408889634
LN2 = 0.6931471805599453
Q_PRESCALE = QK_SCALE * LOG2E

MXU_DTYPE = jnp.bfloat16
TRANSPORT_DTYPE = jnp.bfloat16
VMEM_LIMIT = 48 * 1024 * 1024
VMEM_LIMIT_BIG = 60 * 1024 * 1024

T_HGRN = 256
HG_HEADS_PER_STEP = 4
TM_ROW = 256
T_ATT = 1024
T_ATT_BWD = 1024
ATT_SUB = 4
ATT_SUB_BWD = 2
TS_TN = 2048
TR_ADAM = 256

F32 = jnp.float32
MESH = pl.DeviceIdType.MESH


def _dot(a, b):
    return jnp.dot(a, b, preferred_element_type=F32)


def _dot_nt(a, b):
    return lax.dot_general(a, b, (((1,), (1,)), ((), ())), preferred_element_type=F32)


def _dot_tn(a, b):
    return lax.dot_general(a, b, (((0,), (0,)), ((), ())), preferred_element_type=F32)


def _mx(a):
    return a.astype(MXU_DTYPE)


def _sigmoid(x):
    return 1.0 / (1.0 + jnp.exp(-x))


def _params(vmem=VMEM_LIMIT, **kw):
    return pltpu.CompilerParams(vmem_limit_bytes=vmem, **kw)


def _bcast_rows(row, n):
    return jnp.broadcast_to(row, (n, row.shape[-1]))


def _resident(shape):
    return pl.BlockSpec(shape, lambda *_: (0, 0), pipeline_mode=pl.Buffered(1))


HBM_SPEC = pl.BlockSpec(memory_space=pltpu.HBM)


def _all_gather_packed(shard):
    rows, cols = shard.shape

    def body(x_ref, out_ref, send_sems, recv_sems, local_sem):
        x, y, c = lax.axis_index("x"), lax.axis_index("y"), lax.axis_index("c")
        me, sibling = (x, y, c), (x, y, 1 - c)
        chips = [(1 - x, y), (x, 1 - y), (1 - x, 1 - y)]

        def slot(px, py, pc):
            return out_ref.at[4 * px + 2 * py + pc]

        def copy(k, block, to, src=None):
            return pltpu.make_async_remote_copy(
                src_ref=slot(*block) if src is None else src,
                dst_ref=slot(*block),
                send_sem=send_sems.at[k],
                recv_sem=recv_sems.at[k],
                device_id=to,
                device_id_type=MESH,
            )

        mine = pltpu.make_async_copy(x_ref, slot(*me), local_sem)
        mine.start()
        first = [copy(0, me, sibling, src=x_ref)]
        first += [copy(1 + j, me, (*chip, c), src=x_ref) for j, chip in enumerate(chips)]
        for cp in first:
            cp.start()
        passed = [copy(4 + j, (*chip, c), sibling) for j, chip in enumerate(chips)]
        for j, chip in enumerate(chips):
            copy(1 + j, (*chip, c), me).wait_recv()
            passed[j].start()
        copy(0, sibling, me).wait_recv()
        for j, chip in enumerate(chips):
            copy(4 + j, (*chip, 1 - c), me).wait_recv()
        for cp in first + passed:
            cp.wait_send()
        mine.wait()

    return pl.pallas_call(
        body,
        name="ag_weights",
        out_shape=jax.ShapeDtypeStruct((N_DEV, rows, cols), shard.dtype),
        in_specs=[HBM_SPEC],
        out_specs=HBM_SPEC,
        scratch_shapes=[
            pltpu.SemaphoreType.DMA((7,)),
            pltpu.SemaphoreType.DMA((7,)),
            pltpu.SemaphoreType.DMA,
        ],
    )(shard)


def _reduce_scatter_exchange(slab_sets):
    n_ops = len(slab_sets)

    def body(*refs):
        g_refs, recv_refs = refs[:n_ops], refs[n_ops:2 * n_ops]
        send_sems, recv_sems, local_sems = refs[2 * n_ops:]
        x, y, c = lax.axis_index("x"), lax.axis_index("y"), lax.axis_index("c")
        me = 4 * x + 2 * y + c

        def copy(i, k, landing):
            px, py, pc = x ^ ((k >> 2) & 1), y ^ ((k >> 1) & 1), c ^ (k & 1)
            peer = 4 * px + 2 * py + pc
            return pltpu.make_async_remote_copy(
                src_ref=g_refs[i].at[peer],
                dst_ref=recv_refs[i].at[peer if landing else me],
                send_sem=send_sems.at[i * (N_DEV - 1) + k - 1],
                recv_sem=recv_sems.at[i * (N_DEV - 1) + k - 1],
                device_id=(px, py, pc),
                device_id_type=MESH,
            )

        mine = [pltpu.make_async_copy(g_refs[i].at[me], recv_refs[i].at[me], local_sems.at[i]) for i in range(n_ops)]
        for cp in mine:
            cp.start()
        sends = [copy(i, k, False) for i in range(n_ops) for k in range(1, N_DEV)]
        for cp in sends:
            cp.start()
        for i in range(n_ops):
            for k in range(1, N_DEV):
                copy(i, k, True).wait_recv()
        for cp in sends:
            cp.wait_send()
        for cp in mine:
            cp.wait()

    return pl.pallas_call(
        body,
        name="rs_grads",
        out_shape=[jax.ShapeDtypeStruct(a.shape, a.dtype) for a in slab_sets],
        in_specs=[HBM_SPEC] * n_ops,
        out_specs=[HBM_SPEC] * n_ops,
        scratch_shapes=[
            pltpu.SemaphoreType.DMA((n_ops * (N_DEV - 1),)),
            pltpu.SemaphoreType.DMA((n_ops * (N_DEV - 1),)),
            pltpu.SemaphoreType.DMA((n_ops,)),
        ],
    )(*slab_sets)


def _sum_adamw(recv, w, m, v, name):
    rows, cols = w.shape
    tr = min(rows, TR_ADAM)

    def body(r_ref, w_ref, m_ref, v_ref, g_out, d_out, m_out, v_out):
        g = r_ref[0].astype(F32)
        for i in range(1, N_DEV):
            g = g + r_ref[i].astype(F32)
        g = g[:, :cols]
        m_new = ADAM_B1 * m_ref[...] + (1.0 - ADAM_B1) * g
        v_new = ADAM_B2 * v_ref[...] + (1.0 - ADAM_B2) * (g * g)
        m_hat = m_new / (1.0 - ADAM_B1 ** ADAM_STEP)
        v_hat = v_new / (1.0 - ADAM_B2 ** ADAM_STEP)
        g_out[...] = g
        d_out[...] = -ADAM_LR * (m_hat / (jnp.sqrt(v_hat) + ADAM_EPS) + ADAM_WD * w_ref[...])
        m_out[...] = m_new
        v_out[...] = v_new

    row_spec = pl.BlockSpec((tr, cols), lambda i: (i, 0))
    shape = jax.ShapeDtypeStruct((rows, cols), F32)
    return pl.pallas_call(
        body,
        name=name,
        grid=(rows // tr,),
        in_specs=[pl.BlockSpec((N_DEV, tr, PACK_COLS), lambda i: (0, i, 0)), row_spec, row_spec, row_spec],
        out_specs=[row_spec] * 4,
        out_shape=[shape] * 4,
        compiler_params=_params(),
    )(recv, w, m, v)


def _inproj(x, norm_g, w_in_p):
    s = x.shape[0]
    tm = min(s, TM_ROW)

    def body(x_ref, g_ref, w_ref, proj_ref, h_ref):
        xf = x_ref[...]
        r = lax.rsqrt(jnp.mean(xf * xf, axis=-1, keepdims=True) + EPS)
        h = _mx(xf * r * g_ref[...])
        h_ref[...] = h
        for j in range(PROJ_W // SEG):
            cols = slice(j * SEG, (j + 1) * SEG)
            proj_ref[:, cols] = _dot(h, w_ref[:, cols])

    return pl.pallas_call(
        body,
        name="inproj",
        grid=(s // tm,),
        in_specs=[
            pl.BlockSpec((tm, D_MODEL), lambda i: (i, 0)),
            pl.BlockSpec((1, D_MODEL), lambda i: (0, 0)),
            _resident((D_MODEL, PROJ_W)),
        ],
        out_specs=[
            pl.BlockSpec((tm, PROJ_W), lambda i: (i, 0)),
            pl.BlockSpec((tm, D_MODEL), lambda i: (i, 0)),
        ],
        out_shape=[
            jax.ShapeDtypeStruct((s, PROJ_W), F32),
            jax.ShapeDtypeStruct((s, D_MODEL), MXU_DTYPE),
        ],
        compiler_params=_params(),
    )(x, norm_g, w_in_p)


def _chunk_lower_mask(t):
    row = lax.broadcasted_iota(jnp.int32, (t, t), 0)
    col = lax.broadcasted_iota(jnp.int32, (t, t), 1)
    return ((row // HG_CHUNK) == (col // HG_CHUNK)) & (col <= row)


def _chunk_pos(t):
    return lax.broadcasted_iota(jnp.int32, (t, HEAD_DIM), 0) & (HG_CHUNK - 1)


def _cumsum_chunk(x, pos):
    sh = 1
    while sh < HG_CHUNK:
        x = x + jnp.where(pos >= sh, pltpu.roll(x, sh, 0), 0.0)
        sh *= 2
    return x


def _rcumsum_chunk(x, pos):
    t = x.shape[0]
    sh = 1
    while sh < HG_CHUNK:
        x = x + jnp.where(pos < HG_CHUNK - sh, pltpu.roll(x, t - sh, 0), 0.0)
        sh *= 2
    return x


def _chunk_total(x):
    t, w = x.shape
    tot = jnp.sum(x.reshape(t // HG_CHUNK, HG_CHUNK, w), axis=1, keepdims=True)
    return jnp.broadcast_to(tot, (t // HG_CHUNK, HG_CHUNK, w)).reshape(t, w)


def _hgrn_gates(hq, hf, lb_logits, pos):
    lb = _sigmoid(lb_logits[0:1, :] - lb_logits[1:2, :])
    sig = _sigmoid(hf)
    f = lb + (1.0 - lb) * sig
    sq = _sigmoid(hq)
    q = hq * sq
    k = 1.0 - f
    logf = jnp.log(f)
    bcum = _cumsum_chunk(logf, pos)
    blast = _chunk_total(logf)
    eb = jnp.exp(bcum)
    enb = jnp.exp(-bcum)
    eo = jnp.exp(blast - bcum)
    return dict(lb=lb, sig=sig, f=f, sq=sq, q=q, k=k, eb=eb, enb=enb, eo=eo,
                qi=q * eb, ki=k * enb, ko=k * eo, dec=jnp.exp(blast))


def _hgrn_fwd(proj, lb_logits, hg_norm_g):
    s = proj.shape[0]
    t = min(s, T_HGRN)
    nb = s // t
    nc = t // HG_CHUNK
    hw = HG_HEADS_PER_STEP * HEAD_DIM

    def body(hq_ref, hf_ref, hi_ref, hz_ref, lb_ref, g_ref, o_ref, ya_ref, st_ref, state, u_sc, stb_sc):
        b = pl.program_id(1)

        @pl.when(b == 0)
        def _():
            state[...] = jnp.zeros_like(state)

        lower = _chunk_lower_mask(t)
        pos = _chunk_pos(t)
        for hh in range(HG_HEADS_PER_STEP):
            cols = slice(hh * HEAD_DIM, (hh + 1) * HEAD_DIM)
            st = state[hh]
            st_ref[0, hh] = st
            gt = _hgrn_gates(hq_ref[:, cols], hf_ref[:, cols], lb_ref[:, cols], pos)
            vb = _mx(hi_ref[:, cols])
            qib, kib, kob = _mx(gt["qi"]), _mx(gt["ki"]), _mx(gt["ko"])
            a = jnp.where(lower, _dot_nt(qib, kib), 0.0)
            o_intra = _dot(_mx(a), vb)
            for c in range(nc):
                sl = slice(c * HG_CHUNK, (c + 1) * HG_CHUNK)
                u_sc[hh, c] = _dot_tn(vb[sl], kob[sl])
            for c in range(nc):
                stb_sc[hh, c] = _mx(st)
                st = st * gt["dec"][c * HG_CHUNK:c * HG_CHUNK + 1, :] + u_sc[hh, c]
            state[hh] = st
            outs = []
            for c in range(nc):
                sl = slice(c * HG_CHUNK, (c + 1) * HG_CHUNK)
                outs.append(o_intra[sl] + _dot_nt(qib[sl], stb_sc[hh, c]))
            o = jnp.concatenate(outs, axis=0)
            o_ref[:, cols] = o
            r = lax.rsqrt(jnp.mean(o * o, axis=-1, keepdims=True) + EPS)
            hz = hz_ref[:, cols]
            ya_ref[:, cols] = _mx((o * r * g_ref[...]) * (hz * _sigmoid(hz)))

    hsteps = HEADS // HG_HEADS_PER_STEP

    def seg(k):
        return pl.BlockSpec((t, hw), lambda h, b, k=k: (b, k * hsteps + h))

    return pl.pallas_call(
        body,
        name="hgrn_fwd",
        grid=(hsteps, nb),
        in_specs=[seg(0), seg(1), seg(2), seg(3),
                  pl.BlockSpec((2, hw), lambda h, b: (0, h)),
                  pl.BlockSpec((1, HEAD_DIM), lambda h, b: (0, 0))],
        out_specs=[
            pl.BlockSpec((t, hw), lambda h, b: (b, h)),
            pl.BlockSpec((t, hw), lambda h, b: (b, h)),
            pl.BlockSpec((1, HG_HEADS_PER_STEP, HEAD_DIM, HEAD_DIM), lambda h, b: (b, h, 0, 0)),
        ],
        out_shape=[
            jax.ShapeDtypeStruct((s, D_MODEL), F32),
            jax.ShapeDtypeStruct((s, D_MODEL), MXU_DTYPE),
            jax.ShapeDtypeStruct((nb, HEADS, HEAD_DIM, HEAD_DIM), F32),
        ],
        scratch_shapes=[pltpu.VMEM((HG_HEADS_PER_STEP, HEAD_DIM, HEAD_DIM), F32),
                        pltpu.VMEM((HG_HEADS_PER_STEP, nc, HEAD_DIM, HEAD_DIM), F32),
                        pltpu.VMEM((HG_HEADS_PER_STEP, nc, HEAD_DIM, HEAD_DIM), MXU_DTYPE)],
        compiler_params=_params(),
    )(proj, proj, proj, proj, lb_logits, hg_norm_g)


def _rope(x, cos, sin_a, sin_b):
    return x * cos + pltpu.roll(x, 96, 1) * sin_a + pltpu.roll(x, 32, 1) * sin_b


def _rope_t(d, cos, sin_a, sin_b):
    return d * cos + pltpu.roll(d * sin_a, 32, 1) + pltpu.roll(d * sin_b, 96, 1)


def _mla_prep(proj, q_a_g, kv_a_g, w_uq_p, w_kn, w_v, cos, sin_a, sin_b):
    s = proj.shape[0]
    tm = min(s, TM_ROW)

    def body(sm_ref, gq_ref, gk_ref, wq_ref, wkn_ref, wv_ref, cos_ref, sa_ref, sb_ref,
             q_ref, k_ref, v_ref, cqn_ref, ckvn_ref):
        small = sm_ref[...]
        cq = small[:, :Q_LORA]
        ckv = small[:, Q_LORA:Q_LORA + KV_LORA]
        krp = small[:, Q_LORA + KV_LORA:Q_LORA + KV_LORA + HEAD_DIM]
        rq = lax.rsqrt(jnp.mean(cq * cq, axis=-1, keepdims=True) + EPS)
        rk = lax.rsqrt(jnp.mean(ckv * ckv, axis=-1, keepdims=True) + EPS)
        cqn = _mx(cq * rq * gq_ref[...])
        ckvn = _mx(ckv * rk * gk_ref[...])
        cqn_ref[...] = cqn
        ckvn_ref[...] = ckvn
        q = _dot(cqn, wq_ref[...]) * Q_PRESCALE
        kn = _dot(ckvn, wkn_ref[...])
        v = _dot(ckvn, wv_ref[...])
        cos_t, sa, sb = cos_ref[...], sa_ref[...], sb_ref[...]
        kpe = _mx(_rope(krp, cos_t, sa, sb))
        ones_col = (lax.broadcasted_iota(jnp.int32, (tm, HEAD_DIM), 1) == 0).astype(MXU_DTYPE)
        for h in range(HEADS):
            lo = h * HEAD_PAD
            v_ref[:, lo:lo + HEAD_DIM] = _mx(v[:, h * HEAD_DIM:(h + 1) * HEAD_DIM])
            v_ref[:, lo + HEAD_DIM:lo + HEAD_PAD] = ones_col
            q_ref[:, lo:lo + HEAD_DIM] = _mx(q[:, lo:lo + HEAD_DIM])
            q_ref[:, lo + HEAD_DIM:lo + HEAD_PAD] = _mx(_rope(q[:, lo + HEAD_DIM:lo + HEAD_PAD], cos_t, sa, sb))
            k_ref[:, lo:lo + HEAD_DIM] = _mx(kn[:, h * HEAD_DIM:(h + 1) * HEAD_DIM])
            k_ref[:, lo + HEAD_DIM:lo + HEAD_PAD] = kpe

    def const(shape):
        return pl.BlockSpec(shape, lambda i: (0, 0))

    def rows(w):
        return pl.BlockSpec((tm, w), lambda i: (i, 0))

    return pl.pallas_call(
        body,
        name="mla_prep",
        grid=(s // tm,),
        in_specs=[
            pl.BlockSpec((tm, SEG), lambda i: (i, SMALL_SEG)),
            const((1, Q_LORA)), const((1, KV_LORA)),
            const((Q_LORA, HEADS * HEAD_PAD)), const((KV_LORA, D_MODEL)), const((KV_LORA, D_MODEL)),
            rows(HEAD_DIM), rows(HEAD_DIM), rows(HEAD_DIM),
        ],
        out_specs=[rows(HEADS * HEAD_PAD)] * 3 + [rows(Q_LORA), rows(KV_LORA)],
        out_shape=[
            jax.ShapeDtypeStruct((s, HEADS * HEAD_PAD), MXU_DTYPE),
            jax.ShapeDtypeStruct((s, HEADS * HEAD_PAD), MXU_DTYPE),
            jax.ShapeDtypeStruct((s, HEADS * HEAD_PAD), MXU_DTYPE),
            jax.ShapeDtypeStruct((s, Q_LORA), MXU_DTYPE),
            jax.ShapeDtypeStruct((s, KV_LORA), MXU_DTYPE),
        ],
        compiler_params=_params(),
    )(proj, q_a_g, kv_a_g, w_uq_p, w_kn, w_v, cos, sin_a, sin_b)


def _diag_mask(t):
    row = lax.broadcasted_iota(jnp.int32, (t, t), 0)
    col = lax.broadcasted_iota(jnp.int32, (t, t), 1)
    return row >= col


def _flash_fwd(q_all, k_all, v_all, proj):
    s = q_all.shape[0]
    t = min(s, T_ATT)
    n = s // t

    ts = t // ATT_SUB

    def body(q_ref, k_ref, v_ref, mz_ref, ao_ref, lse_ref, yb_ref, m_sc, acc_sc):
        qi = pl.program_id(1)
        m_sc[...] = jnp.full_like(m_sc, -jnp.inf)
        acc_sc[...] = jnp.zeros_like(acc_sc)

        def key_block(base, diagonal):
            sc, pb, alpha = {}, {}, {}

            def width(r):
                return (r + 1) * ts if diagonal else t

            def scores(r):
                w = width(r)
                s_r = _dot_nt(q_ref[r * ts:(r + 1) * ts], k_ref[pl.ds(base, w), :])
                if diagonal:
                    row = lax.broadcasted_iota(jnp.int32, (ts, w), 0) + r * ts
                    col = lax.broadcasted_iota(jnp.int32, (ts, w), 1)
                    s_r = jnp.where(row >= col, s_r, -jnp.inf)
                sc[r] = s_r

            def softmax(r):
                rs = slice(r * ts, (r + 1) * ts)
                m_prev = m_sc[rs]
                m_new = jnp.maximum(m_prev, jnp.max(sc[r], axis=-1, keepdims=True))
                pb[r] = _mx(jnp.exp2(sc[r] - m_new))
                alpha[r] = jnp.exp2(m_prev - m_new)
                m_sc[rs] = m_new

            def weighted_values(r):
                rs = slice(r * ts, (r + 1) * ts)
                acc_sc[rs] = alpha[r] * acc_sc[rs] + _dot(pb[r], v_ref[pl.ds(base, width(r)), :])

            for step in range(ATT_SUB + 2):
                if step < ATT_SUB:
                    scores(step)
                if 1 <= step <= ATT_SUB:
                    softmax(step - 1)
                if step >= 2:
                    weighted_values(step - 2)

        def below_diagonal(ki, carry):
            key_block(pl.multiple_of(ki * t, t), False)
            return carry

        lax.fori_loop(0, qi, below_diagonal, 0)
        key_block(pl.multiple_of(qi * t, t), True)

        acc = acc_sc[...]
        l = acc[:, HEAD_DIM:HEAD_DIM + 1]
        ao = acc[:, :HEAD_DIM] / l
        ao_ref[...] = ao
        lse_ref[...] = jnp.broadcast_to(m_sc[...] + jnp.log2(l), (t, HEAD_DIM))
        mz = mz_ref[...]
        yb_ref[...] = _mx(ao * (mz * _sigmoid(mz)))

    q_map = lambda h, qi: (qi, h)
    return pl.pallas_call(
        body,
        name="flash_fwd",
        grid=(HEADS, n),
        in_specs=[
            pl.BlockSpec((t, HEAD_PAD), q_map),
            pl.BlockSpec((s, HEAD_PAD), lambda h, qi: (0, h)),
            pl.BlockSpec((s, HEAD_PAD), lambda h, qi: (0, h)),
            pl.BlockSpec((t, HEAD_DIM), lambda h, qi: (qi, MZ_SEG * HEADS + h)),
        ],
        out_specs=[pl.BlockSpec((t, HEAD_DIM), q_map)] * 3,
        out_shape=[
            jax.ShapeDtypeStruct((s, D_MODEL), F32),
            jax.ShapeDtypeStruct((s, D_MODEL), F32),
            jax.ShapeDtypeStruct((s, D_MODEL), MXU_DTYPE),
        ],
        scratch_shapes=[
            pltpu.VMEM((t, 1), F32),
            pltpu.VMEM((t, HEAD_PAD), F32),
        ],
        compiler_params=_params(),
    )(q_all, k_all, v_all, proj)


def _merge_fwd_bwd(x, target, ya, yb, proj, b_gate, final_g, w_pa, w_pb, w_out):
    s = x.shape[0]
    tm = min(s, TM_ROW)

    def body(x_ref, t_ref, ya_ref, yb_ref, g0_ref, g1_ref, bg_ref, fg_ref, wpa_ref, wpb_ref, wo_ref,
             dx2_ref, dya_ref, dyb_ref, dg0_ref, dg1_ref, mb_ref, dpab_ref, dpbb_ref, dx2b_ref,
             loss_ref, dfg_ref, dbg_ref):
        i = pl.program_id(0)

        @pl.when(i == 0)
        def _():
            loss_ref[...] = jnp.zeros_like(loss_ref)
            dfg_ref[...] = jnp.zeros_like(dfg_ref)
            dbg_ref[...] = jnp.zeros_like(dbg_ref)

        pa = _dot(ya_ref[...], wpa_ref[...])
        pb = _dot(yb_ref[...], wpb_ref[...])
        bg = bg_ref[...]
        g0 = _sigmoid(g0_ref[...] + bg[:, :D_MODEL])
        g1 = _sigmoid(g1_ref[...] + bg[:, D_MODEL:])
        merged = g0 * pa + g1 * pb
        mb = _mx(merged)
        mb_ref[...] = mb
        x2 = x_ref[...] + _dot(mb, wo_ref[...])
        r = lax.rsqrt(jnp.mean(x2 * x2, axis=-1, keepdims=True) + EPS)
        xn = x2 * r
        fg = fg_ref[...]
        diff = xn * fg - t_ref[...]
        loss_ref[...] += 0.5 * jnp.sum(jnp.mean(diff * diff, axis=-1, keepdims=True))
        dy = diff * (1.0 / D_MODEL)
        dfg_ref[...] += _bcast_rows(jnp.sum(dy * xn, axis=0, keepdims=True), 8)
        tt = dy * fg
        dx2 = r * (tt - xn * jnp.mean(tt * xn, axis=-1, keepdims=True))
        dx2_ref[...] = dx2
        dx2b = _mx(dx2)
        dx2b_ref[...] = dx2b
        dmerged = _dot_nt(dx2b, wo_ref[...])
        dpa = _mx(dmerged * g0)
        dpb = _mx(dmerged * g1)
        dpab_ref[...] = dpa
        dpbb_ref[...] = dpb
        dg0 = dmerged * pa * (g0 * (1.0 - g0))
        dg1 = dmerged * pb * (g1 * (1.0 - g1))
        dg0_ref[...] = _mx(dg0)
        dg1_ref[...] = _mx(dg1)
        dbg_ref[:, :D_MODEL] += _bcast_rows(jnp.sum(dg0, axis=0, keepdims=True), 8)
        dbg_ref[:, D_MODEL:] += _bcast_rows(jnp.sum(dg1, axis=0, keepdims=True), 8)
        dya_ref[...] = _dot_nt(dpa, wpa_ref[...])
        dyb_ref[...] = _dot_nt(dpb, wpb_ref[...])

    def rows(w=D_MODEL):
        return pl.BlockSpec((tm, w), lambda i: (i, 0))

    def const(shape):
        return pl.BlockSpec(shape, lambda i: (0, 0))

    f32 = jax.ShapeDtypeStruct((s, D_MODEL), F32)
    b16 = jax.ShapeDtypeStruct((s, D_MODEL), MXU_DTYPE)
    return pl.pallas_call(
        body,
        name="merge_fwd_bwd",
        grid=(s // tm,),
        in_specs=[
            rows(), rows(), rows(), rows(),
            pl.BlockSpec((tm, SEG), lambda i: (i, GL_SEG)),
            pl.BlockSpec((tm, SEG), lambda i: (i, GL_SEG + 1)),
            const((1, 2 * D_MODEL)), const((1, D_MODEL)),
            const((D_MODEL, D_MODEL)), const((D_MODEL, D_MODEL)), const((D_MODEL, D_MODEL)),
        ],
        out_specs=[rows()] * 9 + [const((8, HEAD_DIM)), const((8, D_MODEL)), const((8, 2 * D_MODEL))],
        out_shape=[f32, f32, f32, b16, b16, b16, b16, b16, b16,
                   jax.ShapeDtypeStruct((8, HEAD_DIM), F32),
                   jax.ShapeDtypeStruct((8, D_MODEL), F32),
                   jax.ShapeDtypeStruct((8, 2 * D_MODEL), F32)],
        compiler_params=_params(),
    )(x, target, ya, yb, proj, proj, b_gate, final_g, w_pa, w_pb, w_out)


def _attn_gate_bwd(dyb, ao, proj):
    s = dyb.shape[0]
    tm = min(s, TM_ROW)

    def body(dyb_ref, ao_ref, mz_ref, dao_ref, dmz_ref):
        mz = mz_ref[...]
        sg = _sigmoid(mz)
        d = dyb_ref[...]
        dao_ref[...] = _mx(d * (mz * sg))
        dmz_ref[...] = _mx(d * ao_ref[...] * (sg + mz * sg * (1.0 - sg)))

    rows = pl.BlockSpec((tm, D_MODEL), lambda i: (i, 0))
    b16 = jax.ShapeDtypeStruct((s, D_MODEL), MXU_DTYPE)
    return pl.pallas_call(
        body,
        name="attn_gate_bwd",
        grid=(s // tm,),
        in_specs=[rows, rows, pl.BlockSpec((tm, SEG), lambda i: (i, MZ_SEG))],
        out_specs=[rows, rows],
        out_shape=[b16, b16],
        compiler_params=_params(),
    )(dyb, ao, proj)


def _flash_bwd(q_all, k_all, v_all, dao, ao, lse):
    s = q_all.shape[0]
    t = min(s, T_ATT_BWD)
    n = s // t
    pairs = [(ki, qi) for ki in range(n) for qi in range(ki, n)]
    ki_list = jnp.asarray([p[0] for p in pairs], jnp.int32)
    qi_list = jnp.asarray([p[1] for p in pairs], jnp.int32)

    def body(ki_ref, qi_ref, q_ref, k_ref, v_ref, do_ref, ao_ref, lse_ref, dq_ref, dk_ref, dv_ref, dk_acc, dv_acc):
        step = pl.program_id(1)
        ki, qi = ki_ref[step], qi_ref[step]

        @pl.when(qi == ki)
        def _():
            dk_acc[...] = jnp.zeros_like(dk_acc)
            dv_acc[...] = jnp.zeros_like(dv_acc)

        @pl.when(ki == 0)
        def _():
            dq_ref[pl.ds(pl.multiple_of(qi * t, t), t), :] = jnp.zeros((t, HEAD_PAD), F32)

        def pair(masked):
            nsub = ATT_SUB if masked else ATT_SUB_BWD
            ts = t // nsub
            dk_parts, dv_parts = [], []
            for r in range(nsub):
                rs = slice(r * ts, (r + 1) * ts)
                w = (r + 1) * ts if masked else t
                k = k_ref[:w]
                v = v_ref[:w, :HEAD_DIM]
                q = q_ref[rs]
                sc = _dot_nt(q, k)
                if masked:
                    row = lax.broadcasted_iota(jnp.int32, (ts, w), 0) + r * ts
                    col = lax.broadcasted_iota(jnp.int32, (ts, w), 1)
                    sc = jnp.where(row >= col, sc, -jnp.inf)
                p = jnp.exp2(sc - lse_ref[rs, 0:1])
                do = do_ref[rs]
                delta = jnp.sum(do.astype(F32) * ao_ref[rs], axis=-1, keepdims=True)
                dv_part = _dot_tn(_mx(p), do)
                ds = _mx(p * (_dot_nt(do, v) - delta))
                dk_part = _dot_tn(ds, q)
                rows = pl.ds(pl.multiple_of(qi * t + r * ts, ts), ts)
                dq_ref[rows, :] += _dot(ds, k)
                if masked:
                    dk_acc[:w] += dk_part
                    dv_acc[:w] += dv_part
                else:
                    dk_parts.append(dk_part)
                    dv_parts.append(dv_part)

            if not masked:
                dk_acc[...] += sum(dk_parts[1:], dk_parts[0])
                dv_acc[...] += sum(dv_parts[1:], dv_parts[0])

        @pl.when(qi == ki)
        def _():
            pair(True)

        @pl.when(qi > ki)
        def _():
            pair(False)

        @pl.when(qi == n - 1)
        def _():
            dk_ref[...] = dk_acc[...] * LN2
            dv_ref[...] = dv_acc[...]

    q_map = lambda h, p, ki_ref, qi_ref: (qi_ref[p], h)
    kv_map = lambda h, p, ki_ref, qi_ref: (ki_ref[p], h)
    grid_spec = pltpu.PrefetchScalarGridSpec(
        num_scalar_prefetch=2,
        grid=(HEADS, len(pairs)),
        in_specs=[
            pl.BlockSpec((t, HEAD_PAD), q_map),
            pl.BlockSpec((t, HEAD_PAD), kv_map),
            pl.BlockSpec((t, HEAD_PAD), kv_map),
            pl.BlockSpec((t, HEAD_DIM), q_map),
            pl.BlockSpec((t, HEAD_DIM), q_map),
            pl.BlockSpec((t, HEAD_DIM), q_map),
        ],
        out_specs=[
            pl.BlockSpec((s, HEAD_PAD), lambda h, p, ki_ref, qi_ref: (0, h)),
            pl.BlockSpec((t, HEAD_PAD), kv_map),
            pl.BlockSpec((t, HEAD_DIM), kv_map),
        ],
        scratch_shapes=[pltpu.VMEM((t, HEAD_PAD), F32), pltpu.VMEM((t, HEAD_DIM), F32)],
    )
    return pl.pallas_call(
        body,
        name="flash_bwd",
        grid_spec=grid_spec,
        out_shape=[
            jax.ShapeDtypeStruct((s, HEADS * HEAD_PAD), F32),
            jax.ShapeDtypeStruct((s, HEADS * HEAD_PAD), F32),
            jax.ShapeDtypeStruct((s, D_MODEL), F32),
        ],
        compiler_params=_params(VMEM_LIMIT_BIG),
    )(ki_list, qi_list, q_all, k_all, v_all, dao, ao, lse)


def _mla_prep_bwd(dq_all, dk_all, dv_all, proj, q_a_g, kv_a_g, w_uq_p, w_kn, w_v, cos, sin_a, sin_b):
    s = proj.shape[0]
    tm = min(s, TM_ROW)

    def body(dq_ref, dk_ref, dv_ref, sm_ref, gq_ref, gk_ref, wq_ref, wkn_ref, wv_ref, cos_ref, sa_ref, sb_ref,
             dsm_ref, dqf_ref, dkn_ref, dvb_ref, dgq_ref, dgk_ref):
        i = pl.program_id(0)

        @pl.when(i == 0)
        def _():
            dgq_ref[...] = jnp.zeros_like(dgq_ref)
            dgk_ref[...] = jnp.zeros_like(dgk_ref)

        cos_t, sa, sb = cos_ref[...], sa_ref[...], sb_ref[...]
        dkpe = jnp.zeros((tm, HEAD_DIM), F32)
        for h in range(HEADS):
            lo = h * HEAD_PAD
            dqf_ref[:, lo:lo + HEAD_DIM] = _mx(dq_ref[:, lo:lo + HEAD_DIM] * QK_SCALE)
            dqf_ref[:, lo + HEAD_DIM:lo + HEAD_PAD] = _mx(
                _rope_t(dq_ref[:, lo + HEAD_DIM:lo + HEAD_PAD] * QK_SCALE, cos_t, sa, sb))
            dkn_ref[:, h * HEAD_DIM:(h + 1) * HEAD_DIM] = _mx(dk_ref[:, lo:lo + HEAD_DIM])
            dkpe = dkpe + dk_ref[:, lo + HEAD_DIM:lo + HEAD_PAD]
        dkr = _rope_t(dkpe, cos_t, sa, sb)
        dvb = _mx(dv_ref[...])
        dvb_ref[...] = dvb
        dcqn = _dot_nt(dqf_ref[...], wq_ref[...])
        dckvn = _dot_nt(dkn_ref[...], wkn_ref[...]) + _dot_nt(dvb, wv_ref[...])

        small = sm_ref[...]
        cq = small[:, :Q_LORA]
        ckv = small[:, Q_LORA:Q_LORA + KV_LORA]
        rq = lax.rsqrt(jnp.mean(cq * cq, axis=-1, keepdims=True) + EPS)
        rk = lax.rsqrt(jnp.mean(ckv * ckv, axis=-1, keepdims=True) + EPS)
        cqh = cq * rq
        ckh = ckv * rk
        dgq_ref[...] += _bcast_rows(jnp.sum(dcqn * cqh, axis=0, keepdims=True), 8)
        dgk_ref[...] += _bcast_rows(jnp.sum(dckvn * ckh, axis=0, keepdims=True), 8)
        tq = dcqn * gq_ref[...]
        tk = dckvn * gk_ref[...]
        dcq = rq * (tq - cqh * jnp.mean(tq * cqh, axis=-1, keepdims=True))
        dckv = rk * (tk - ckh * jnp.mean(tk * ckh, axis=-1, keepdims=True))
        dsm_ref[:, :Q_LORA] = _mx(dcq)
        dsm_ref[:, Q_LORA:Q_LORA + KV_LORA] = _mx(dckv)
        dsm_ref[:, Q_LORA + KV_LORA:Q_LORA + KV_LORA + HEAD_DIM] = _mx(dkr)
        dsm_ref[:, Q_LORA + KV_LORA + HEAD_DIM:] = jnp.zeros((tm, SEG - Q_LORA - KV_LORA - HEAD_DIM), MXU_DTYPE)

    def const(shape):
        return pl.BlockSpec(shape, lambda i: (0, 0))

    def rows(w):
        return pl.BlockSpec((tm, w), lambda i: (i, 0))

    return pl.pallas_call(
        body,
        name="mla_prep_bwd",
        grid=(s // tm,),
        in_specs=[
            rows(HEADS * HEAD_PAD), rows(HEADS * HEAD_PAD), rows(D_MODEL),
            pl.BlockSpec((tm, SEG), lambda i: (i, SMALL_SEG)),
            const((1, Q_LORA)), const((1, KV_LORA)),
            const((Q_LORA, HEADS * HEAD_PAD)), const((KV_LORA, D_MODEL)), const((KV_LORA, D_MODEL)),
            rows(HEAD_DIM), rows(HEAD_DIM), rows(HEAD_DIM),
        ],
        out_specs=[rows(SEG), rows(HEADS * HEAD_PAD), rows(D_MODEL), rows(D_MODEL),
                   const((8, Q_LORA)), const((8, KV_LORA))],
        out_shape=[
            jax.ShapeDtypeStruct((s, SEG), MXU_DTYPE),
            jax.ShapeDtypeStruct((s, HEADS * HEAD_PAD), MXU_DTYPE),
            jax.ShapeDtypeStruct((s, D_MODEL), MXU_DTYPE),
            jax.ShapeDtypeStruct((s, D_MODEL), MXU_DTYPE),
            jax.ShapeDtypeStruct((8, Q_LORA), F32),
            jax.ShapeDtypeStruct((8, KV_LORA), F32),
        ],
        compiler_params=_params(),
    )(dq_all, dk_all, dv_all, proj, q_a_g, kv_a_g, w_uq_p, w_kn, w_v, cos, sin_a, sin_b)


def _hgrn_bwd(proj, lb_logits, hg_norm_g, o_all, dya, states):
    s = proj.shape[0]
    t = min(s, T_HGRN)
    nb = s // t
    nc = t // HG_CHUNK

    def body(hq_ref, hf_ref, hi_ref, hz_ref, lb_ref, g_ref, o_ref, dya_ref, st_ref,
             dhq_ref, dhf_ref, dhi_ref, dhz_ref, dlb_ref, dg_ref, dstate, u_sc, g_sc, stf_sc, stb_sc, dstb_sc):
        h, b = pl.program_id(0), pl.program_id(1)

        @pl.when(b == 0)
        def _():
            dstate[...] = jnp.zeros_like(dstate)
            dlb_ref[...] = jnp.zeros_like(dlb_ref)

        @pl.when((b == 0) & (h == 0))
        def _():
            dg_ref[...] = jnp.zeros_like(dg_ref)

        lower = _chunk_lower_mask(t)
        pos = _chunk_pos(t)
        ghg = g_ref[...]
        for hh in range(HG_HEADS_PER_STEP):
            cols = slice(hh * HEAD_DIM, (hh + 1) * HEAD_DIM)
            hq, hf, hz = hq_ref[:, cols], hf_ref[:, cols], hz_ref[:, cols]
            gt = _hgrn_gates(hq, hf, lb_ref[:, cols], pos)
            vb = _mx(hi_ref[:, cols])
            qi, ki, ko = gt["qi"], gt["ki"], gt["ko"]
            qib, kib, kob = _mx(qi), _mx(ki), _mx(ko)

            o = o_ref[:, cols]
            sz = _sigmoid(hz)
            r = lax.rsqrt(jnp.mean(o * o, axis=-1, keepdims=True) + EPS)
            on = o * r
            dya_t = dya_ref[:, cols]
            don = dya_t * (hz * sz)
            dhz_ref[:, cols] = _mx(dya_t * (on * ghg) * (sz + hz * sz * (1.0 - sz)))
            dg_ref[...] += _bcast_rows(jnp.sum(don * on, axis=0, keepdims=True), 8)
            tt = don * ghg
            do = r * (tt - on * jnp.mean(tt * on, axis=-1, keepdims=True))
            dob = _mx(do)

            for c in range(nc):
                sl = slice(c * HG_CHUNK, (c + 1) * HG_CHUNK)
                u_sc[hh, c] = _dot_tn(vb[sl], kob[sl])
                g_sc[hh, c] = _dot_tn(dob[sl], qib[sl])

            st = st_ref[0, hh]
            for c in range(nc):
                stf_sc[hh, c] = st
                stb_sc[hh, c] = _mx(st)
                if c < nc - 1:
                    st = st * gt["dec"][c * HG_CHUNK:c * HG_CHUNK + 1, :] + u_sc[hh, c]

            dst = dstate[hh]
            dd_parts = [None] * nc
            for c in reversed(range(nc)):
                dec = gt["dec"][c * HG_CHUNK:c * HG_CHUNK + 1, :]
                dstb_sc[hh, c] = _mx(dst)
                dd_parts[c] = _bcast_rows(jnp.sum(dst * stf_sc[hh, c], axis=0, keepdims=True) * dec, HG_CHUNK)
                dst = dst * dec + g_sc[hh, c]
            dstate[hh] = dst

            a = jnp.where(lower, _dot_nt(qib, kib), 0.0)
            da = _mx(jnp.where(lower, _dot_nt(dob, vb), 0.0))
            dqi_intra = _dot(da, kib)
            dki = _dot_tn(da, qib)
            dv_intra = _dot_tn(_mx(a), dob)

            dqi_parts, dko_parts, dv_parts = [None] * nc, [None] * nc, [None] * nc
            for c in range(nc):
                sl = slice(c * HG_CHUNK, (c + 1) * HG_CHUNK)
                dv_parts[c] = dv_intra[sl] + _dot_nt(kob[sl], dstb_sc[hh, c])
                dko_parts[c] = _dot(vb[sl], dstb_sc[hh, c])
                dqi_parts[c] = dqi_intra[sl] + _dot(dob[sl], stb_sc[hh, c])
            dqi = jnp.concatenate(dqi_parts, axis=0)
            dko = jnp.concatenate(dko_parts, axis=0)
            dv = jnp.concatenate(dv_parts, axis=0)
            dd = jnp.concatenate(dd_parts, axis=0)

            dq = dqi * gt["eb"]
            dk = dki * gt["enb"] + dko * gt["eo"]
            db = dqi * qi - dki * ki - dko * ko
            dlogf = _rcumsum_chunk(db, pos) + _chunk_total(dko * ko) + dd
            df = dlogf / gt["f"] - dk
            lb, sig, sq = gt["lb"], gt["sig"], gt["sq"]
            dhf_ref[:, cols] = _mx(df * (1.0 - lb) * (sig * (1.0 - sig)))
            dhq_ref[:, cols] = _mx(dq * (sq + hq * sq * (1.0 - sq)))
            dhi_ref[:, cols] = _mx(dv)
            dlb = jnp.sum(df * (1.0 - sig), axis=0, keepdims=True) * (lb * (1.0 - lb))
            dlb_ref[:, cols] += jnp.concatenate([dlb, -dlb], axis=0)

    hw = HG_HEADS_PER_STEP * HEAD_DIM
    hsteps = HEADS // HG_HEADS_PER_STEP

    def seg(k):
        return pl.BlockSpec((t, hw), lambda h, b, k=k: (nb - 1 - b, k * hsteps + h))

    blk = pl.BlockSpec((t, hw), lambda h, b: (nb - 1 - b, h))
    b16 = jax.ShapeDtypeStruct((s, D_MODEL), MXU_DTYPE)
    return pl.pallas_call(
        body,
        name="hgrn_bwd",
        grid=(hsteps, nb),
        in_specs=[seg(0), seg(1), seg(2), seg(3),
                  pl.BlockSpec((2, hw), lambda h, b: (0, h)),
                  pl.BlockSpec((1, HEAD_DIM), lambda h, b: (0, 0)),
                  blk, blk,
                  pl.BlockSpec((1, HG_HEADS_PER_STEP, HEAD_DIM, HEAD_DIM), lambda h, b: (nb - 1 - b, h, 0, 0))],
        out_specs=[blk, blk, blk, blk,
                   pl.BlockSpec((2, hw), lambda h, b: (0, h)),
                   pl.BlockSpec((8, HEAD_DIM), lambda h, b: (0, 0))],
        out_shape=[b16, b16, b16, b16,
                   jax.ShapeDtypeStruct((2, D_MODEL), F32),
                   jax.ShapeDtypeStruct((8, HEAD_DIM), F32)],
        scratch_shapes=[pltpu.VMEM((HG_HEADS_PER_STEP, HEAD_DIM, HEAD_DIM), F32)]
        + [pltpu.VMEM((HG_HEADS_PER_STEP, nc, HEAD_DIM, HEAD_DIM), F32)] * 3
        + [pltpu.VMEM((HG_HEADS_PER_STEP, nc, HEAD_DIM, HEAD_DIM), MXU_DTYPE)] * 2,
        compiler_params=_params(),
    )(proj, proj, proj, proj, lb_logits, hg_norm_g, o_all, dya, states)


def _dh_bwd(segs, w_in_p, x, dx2, norm_g):
    s = x.shape[0]
    tm = min(s, TM_ROW)
    nseg = len(segs)

    def body(*refs):
        seg_refs = refs[:nseg]
        w_ref, x_ref, dx2_ref, g_ref, gx_ref, dng_ref, dp_buf = refs[nseg:]
        i = pl.program_id(0)

        @pl.when(i == 0)
        def _():
            dng_ref[...] = jnp.zeros_like(dng_ref)

        for k, sref in enumerate(seg_refs):
            dp_buf[:, k * SEG:(k + 1) * SEG] = sref[...]
        dh = _dot_nt(dp_buf[...], w_ref[...])
        xf = x_ref[...]
        r = lax.rsqrt(jnp.mean(xf * xf, axis=-1, keepdims=True) + EPS)
        xh = xf * r
        dng_ref[...] += _bcast_rows(jnp.sum(dh * xh, axis=0, keepdims=True), 8)
        tt = dh * g_ref[...]
        gx_ref[...] = dx2_ref[...] + r * (tt - xh * jnp.mean(tt * xh, axis=-1, keepdims=True))

    rows = pl.BlockSpec((tm, D_MODEL), lambda i: (i, 0))
    return pl.pallas_call(
        body,
        name="dh_bwd",
        grid=(s // tm,),
        in_specs=[pl.BlockSpec((tm, SEG), lambda i: (i, 0))] * nseg + [
            _resident((D_MODEL, PROJ_W)),
            rows, rows,
            pl.BlockSpec((1, D_MODEL), lambda i: (0, 0)),
        ],
        out_specs=[rows, pl.BlockSpec((8, D_MODEL), lambda i: (0, 0))],
        out_shape=[jax.ShapeDtypeStruct((s, D_MODEL), F32), jax.ShapeDtypeStruct((8, D_MODEL), F32)],
        scratch_shapes=[pltpu.VMEM((tm, PROJ_W), MXU_DTYPE)],
        compiler_params=_params(),
    )(*segs, w_in_p, x, dx2, norm_g)


def _matmul_tn(a, b, name, out_dtype=F32):
    s, m = a.shape
    n = b.shape[1]
    ts = min(s, TS_TN)
    tn = min(n, SEG)
    nk = s // ts

    def body(a_ref, b_ref, o_ref, acc):
        k = pl.program_id(1)
        part = _dot_tn(a_ref[...], b_ref[...])

        @pl.when(k == 0)
        def _():
            acc[...] = part

        @pl.when(k > 0)
        def _():
            acc[...] += part

        @pl.when(k == nk - 1)
        def _():
            o_ref[...] = acc[...].astype(out_dtype)

    return pl.pallas_call(
        body,
        name=name,
        grid=(n // tn, nk),
        in_specs=[pl.BlockSpec((ts, m), lambda j, k: (k, 0)), pl.BlockSpec((ts, tn), lambda j, k: (k, j))],
        out_specs=pl.BlockSpec((m, tn), lambda j, k: (0, j)),
        out_shape=jax.ShapeDtypeStruct((m, n), out_dtype),
        scratch_shapes=[pltpu.VMEM((m, tn), F32)],
        compiler_params=_params(),
    )(a, b)


def _w_in_pieces():
    per = IN_COLS // N_DEV
    pad_at = SMALL_SEG * SEG + Q_LORA + KV_LORA + QK_ROPE
    pieces = []
    for j in range(N_DEV):
        u0, u1 = j * per, (j + 1) * per
        cuts = [u0] + ([pad_at] if u0 < pad_at < u1 else []) + [u1]
        for a, b in zip(cuts[:-1], cuts[1:]):
            pieces.append((j, a - u0, b - u0, a if a < pad_at else a + PROJ_W - IN_COLS))
    return pad_at, pieces


def _assemble_w_in(gathered):
    tr = TM_ROW
    pad_at, pieces = _w_in_pieces()

    def body(in_ref, out_ref):
        out_ref[:, pad_at:pad_at + PROJ_W - IN_COLS] = jnp.zeros((tr, PROJ_W - IN_COLS), gathered.dtype)
        for j, a, b, p0 in pieces:
            out_ref[:, p0:p0 + b - a] = in_ref[j, :, a:b]

    return pl.pallas_call(
        body,
        name="assemble_w_in",
        grid=(D_MODEL // tr,),
        in_specs=[pl.BlockSpec((N_DEV, tr, PACK_COLS), lambda i: (0, i, 0))],
        out_specs=pl.BlockSpec((tr, PROJ_W), lambda i: (i, 0)),
        out_shape=jax.ShapeDtypeStruct((D_MODEL, PROJ_W), gathered.dtype),
        compiler_params=_params(),
    )(gathered)


def _scatter_dw_in(dw_segs):
    tr = TM_ROW
    _, pieces = _w_in_pieces()
    per = IN_COLS // N_DEV
    nseg = len(dw_segs)

    def body(*refs):
        out_ref, buf = refs[nseg:]
        for k in range(nseg):
            buf[:, k * SEG:(k + 1) * SEG] = refs[k][...]
        for j in range(N_DEV):
            out_ref[j, :, per:] = jnp.zeros((tr, PACK_COLS - per), TRANSPORT_DTYPE)
        for j, a, b, p0 in pieces:
            out_ref[j, :, a:b] = buf[:, p0:p0 + b - a].astype(TRANSPORT_DTYPE)

    return pl.pallas_call(
        body,
        name="scatter_dw_in",
        grid=(D_MODEL // tr,),
        in_specs=[pl.BlockSpec((tr, SEG), lambda i: (i, 0))] * nseg,
        out_specs=pl.BlockSpec((N_DEV, tr, PACK_COLS), lambda i: (0, i, 0)),
        out_shape=jax.ShapeDtypeStruct((N_DEV, D_MODEL, PACK_COLS), TRANSPORT_DTYPE),
        scratch_shapes=[pltpu.VMEM((tr, PROJ_W), F32)],
        compiler_params=_params(),
    )(*dw_segs)


def _rope_tables(s):
    inv = ROPE_THETA ** (-jnp.arange(0, QK_ROPE, 2, dtype=F32) / QK_ROPE)
    ang = jnp.arange(s, dtype=F32)[:, None] * inv[None, :]
    cos, sin = jnp.cos(ang), jnp.sin(ang)
    z32 = jnp.zeros_like(cos)
    z64 = jnp.zeros((s, HEAD_DIM - QK_ROPE), F32)
    cos_t = jnp.concatenate([cos, cos, z64], axis=1)
    sin_a = jnp.concatenate([-sin, z32, z64], axis=1)
    sin_b = jnp.concatenate([z32, sin, z64], axis=1)
    return cos_t, sin_a, sin_b


def _pack_misc(w_uq, w_ukv, norm_g, b_gate, lb_logits, hg_norm_g, q_a_g, kv_a_g, final_norm_g, extra):
    misc = jnp.concatenate([hg_norm_g.reshape(-1), q_a_g.reshape(-1), kv_a_g.reshape(-1), extra.reshape(-1),
                            jnp.zeros((PACK_COLS - HEAD_DIM - Q_LORA - KV_LORA - 1,), F32)])
    return jnp.concatenate([w_uq.reshape(ROWS_W_UQ, PACK_COLS), w_ukv.reshape(ROWS_W_UKV, PACK_COLS),
                            norm_g.reshape(1, -1), b_gate.reshape(2, -1), lb_logits.reshape(2, -1),
                            misc.reshape(1, -1), final_norm_g.reshape(1, -1), jnp.zeros((1, PACK_COLS), F32)], axis=0)


def _unpack_misc(p):
    sm = p[ROWS_W_UQ + ROWS_W_UKV:]
    misc = sm[5]
    return dict(
        w_uq=p[:ROWS_W_UQ].reshape(1, Q_LORA, QK_DIM),
        w_ukv=p[ROWS_W_UQ:ROWS_W_UQ + ROWS_W_UKV].reshape(1, KV_LORA, 2 * HEAD_DIM),
        norm_g=sm[0:1], b_gate=sm[1:3].reshape(1, -1), lb_logits=sm[3:5],
        hg_norm_g=misc[None, :HEAD_DIM], q_a_g=misc[None, HEAD_DIM:HEAD_DIM + Q_LORA],
        kv_a_g=misc[None, HEAD_DIM + Q_LORA:HEAD_DIM + Q_LORA + KV_LORA], final_norm_g=sm[6],
        extra=misc[HEAD_DIM + Q_LORA + KV_LORA],
    )


def _weight_shard_buffer(w_in, w_uq, w_ukv, w_pa, w_pb, w_out):
    w_in_pad = jnp.pad(w_in.reshape(D_MODEL, -1), ((0, 0), (0, PACK_COLS - IN_COLS // N_DEV)))
    parts = [w_in_pad] + [a.reshape(-1, PACK_COLS) for a in (w_pa, w_pb, w_out, w_uq, w_ukv)]
    return jnp.concatenate(parts + [jnp.zeros((ROWS_AG - ROWS_AG_USED, PACK_COLS), F32)], axis=0)


def _full_weights(gathered):
    w_in_p = _assemble_w_in(gathered)
    r0 = D_MODEL
    mats = []
    for _ in range(3):
        mats.append(gathered[:, r0:r0 + ROWS_W_PROJ].reshape(D_MODEL, D_MODEL))
        r0 += ROWS_W_PROJ
    w_uq = gathered[:, r0:r0 + ROWS_W_UQ].reshape(N_DEV, Q_LORA, QK_DIM).transpose(1, 0, 2)
    w_uq_p = jnp.concatenate([w_uq, jnp.zeros((Q_LORA, HEADS, HEAD_PAD - QK_DIM), w_uq.dtype)], axis=2)
    w_uq_p = w_uq_p.reshape(Q_LORA, HEADS * HEAD_PAD)
    r0 += ROWS_W_UQ
    w_ukv = gathered[:, r0:r0 + ROWS_W_UKV].reshape(N_DEV, KV_LORA, 2 * HEAD_DIM).transpose(1, 0, 2)
    w_kn = w_ukv[:, :, :HEAD_DIM].reshape(KV_LORA, D_MODEL)
    w_v = w_ukv[:, :, HEAD_DIM:].reshape(KV_LORA, D_MODEL)
    return w_in_p, w_uq_p, w_kn, w_v, mats[0], mats[1], mats[2]


def _grad_slabs(g):
    dw_uq = g["w_uq_p"].reshape(Q_LORA, HEADS, HEAD_PAD)[:, :, :QK_DIM].transpose(1, 0, 2)
    dw_ukv = jnp.concatenate([g["w_kn"].reshape(KV_LORA, HEADS, HEAD_DIM),
                              g["w_v"].reshape(KV_LORA, HEADS, HEAD_DIM)], axis=2).transpose(1, 0, 2)
    misc = [_pack_misc(dw_uq[j], dw_ukv[j], g["norm_g"], g["b_gate"], g["lb_logits"], g["hg_norm_g"],
                       g["q_a_g"], g["kv_a_g"], g["final_norm_g"], g["loss"]) for j in range(N_DEV)]
    mats = [g[n].reshape(N_DEV, ROWS_W_PROJ, PACK_COLS) for n in ("w_pa", "w_pb", "w_out")]
    return [_scatter_dw_in(g["w_in_segs"])] + mats + [jnp.stack(misc)]


def _local_grads(x, target, norm_g, b_gate, lb_logits, hg_norm_g, q_a_g, kv_a_g, final_g,
                 w_in_p, w_uq_p, w_kn, w_v, w_pa, w_pb, w_out):
    s = x.shape[0]
    cos, sin_a, sin_b = _rope_tables(s)
    proj, h = _inproj(x, norm_g, w_in_p)
    o_all, ya, states = _hgrn_fwd(proj, lb_logits, hg_norm_g)
    q_all, k_all, v_all, cqn, ckvn = _mla_prep(proj, q_a_g, kv_a_g, w_uq_p, w_kn, w_v, cos, sin_a, sin_b)
    ao, lse, yb = _flash_fwd(q_all, k_all, v_all, proj)
    (dx2, dya, dyb, dg0, dg1, merged_b, dpa_b, dpb_b, dx2_b,
     loss_acc, dfg_acc, dbg_acc) = _merge_fwd_bwd(x, target, ya, yb, proj, b_gate, final_g, w_pa, w_pb, w_out)
    dao, dmz = _attn_gate_bwd(dyb, ao, proj)
    dq_all, dk_all, dv_all = _flash_bwd(q_all, k_all, v_all, dao, ao, lse)
    dsmall, dqf_b, dkn_b, dv_b, dgq_acc, dgk_acc = _mla_prep_bwd(
        dq_all, dk_all, dv_all, proj, q_a_g, kv_a_g, w_uq_p, w_kn, w_v, cos, sin_a, sin_b)
    dhq, dhf, dhi, dhz, dlb, dhg_acc = _hgrn_bwd(proj, lb_logits, hg_norm_g, o_all, dya, states)
    segs = [dhq, dhf, dhi, dhz, dsmall, dmz, dg0, dg1]
    grad_x, dng_acc = _dh_bwd(segs, w_in_p, x, dx2, norm_g)
    return dict(
        loss=loss_acc[0, 0], grad_x=grad_x,
        norm_g=dng_acc[0:1], b_gate=dbg_acc[0:1], lb_logits=dlb, hg_norm_g=dhg_acc[0:1],
        q_a_g=dgq_acc[0:1], kv_a_g=dgk_acc[0:1], final_norm_g=dfg_acc[0],
        w_in_segs=[_matmul_tn(h, sg, "dw_in_%d" % k) for k, sg in enumerate(segs)],
        w_uq_p=_matmul_tn(cqn, dqf_b, "dw_uq"),
        w_kn=_matmul_tn(ckvn, dkn_b, "dw_kn"),
        w_v=_matmul_tn(ckvn, dv_b, "dw_v"),
        w_pa=_matmul_tn(ya, dpa_b, "dw_pa", TRANSPORT_DTYPE),
        w_pb=_matmul_tn(yb, dpb_b, "dw_pb", TRANSPORT_DTYPE),
        w_out=_matmul_tn(merged_b, dx2_b, "dw_out", TRANSPORT_DTYPE),
    )


def kernel(x, norm_g, w_in, b_gate, lb_logits, hg_norm_g, q_a_g, w_uq, kv_a_g, w_ukv, w_proj_a, w_proj_b, w_out, final_norm_g, loss_target, m_norm_g, m_w_in, m_b_gate, m_lb_logits, m_hg_norm_g, m_q_a_g, m_w_uq, m_kv_a_g, m_w_ukv, m_w_proj_a, m_w_proj_b, m_w_out, m_final_norm_g, v_norm_g, v_w_in, v_b_gate, v_lb_logits, v_hg_norm_g, v_q_a_g, v_w_uq, v_kv_a_g, v_w_ukv, v_w_proj_a, v_w_proj_b, v_w_out, v_final_norm_g):
    zero = jnp.zeros((1,), F32)
    shard = _weight_shard_buffer(w_in, w_uq, w_ukv, w_proj_a, w_proj_b, w_out).astype(MXU_DTYPE)
    full = _full_weights(_all_gather_packed(shard))
    g = _local_grads(x[0], loss_target[0], norm_g, b_gate, lb_logits, hg_norm_g, q_a_g, kv_a_g,
                     final_norm_g.reshape(1, -1), *full)
    recv_in, recv_pa, recv_pb, recv_out, recv_misc = _reduce_scatter_exchange(_grad_slabs(g))

    out_in = _sum_adamw(recv_in, w_in[0], m_w_in[0], v_w_in[0], "adamw_w_in")
    out_pa = _sum_adamw(recv_pa, w_proj_a[0], m_w_proj_a[0], v_w_proj_a[0], "adamw_w_pa")
    out_pb = _sum_adamw(recv_pb, w_proj_b[0], m_w_proj_b[0], v_w_proj_b[0], "adamw_w_pb")
    out_out = _sum_adamw(recv_out, w_out[0], m_w_out[0], v_w_out[0], "adamw_w_out")
    out_misc = _sum_adamw(
        recv_misc,
        _pack_misc(w_uq, w_ukv, norm_g, b_gate, lb_logits, hg_norm_g, q_a_g, kv_a_g, final_norm_g, zero),
        _pack_misc(m_w_uq, m_w_ukv, m_norm_g, m_b_gate, m_lb_logits, m_hg_norm_g, m_q_a_g, m_kv_a_g,
                   m_final_norm_g, zero),
        _pack_misc(v_w_uq, v_w_ukv, v_norm_g, v_b_gate, v_lb_logits, v_hg_norm_g, v_q_a_g, v_kv_a_g,
                   v_final_norm_g, zero),
        "adamw_misc")
    names = ["norm_g", "w_in", "b_gate", "lb_logits", "hg_norm_g", "q_a_g", "w_uq", "kv_a_g", "w_ukv",
             "w_proj_a", "w_proj_b", "w_out", "final_norm_g"]
    kinds = []
    for i in range(4):
        d = _unpack_misc(out_misc[i])
        d.update(w_in=out_in[i][None], w_proj_a=out_pa[i][None], w_proj_b=out_pb[i][None], w_out=out_out[i][None])
        kinds.append(d)
    return (kinds[0]["extra"], g["grad_x"][None], *[d[n] for d in kinds for n in names])
```

```python
import functools

import jax
import jax.numpy as jnp
from jax import lax
from jax.experimental import pallas as pl
from jax.experimental.pallas import tpu as pltpu

D_MODEL = 1024
HEADS = 8
HEAD_DIM = 128
HG_CHUNK = 32
Q_LORA = 384
KV_LORA = 256
QK_ROPE = 64
QK_DIM = 192
ROPE_THETA = 10000.0
EPS = 1e-6
IN_COLS = 7872
ADAM_LR = 0.001
ADAM_B1 = 0.9
ADAM_B2 = 0.999
ADAM_EPS = 1e-08
ADAM_WD = 0.01
ADAM_STEP = 10

N_DEV = 8
SEG = 1024
PROJ_W = 8 * SEG
SMALL_SEG = 4
MZ_SEG = 5
GL_SEG = 6
HEAD_PAD = 256
PACK_COLS = 1024
ROWS_W_UQ = 72
ROWS_W_UKV = 64
ROWS_W_PROJ = 128
ROWS_AG_USED = D_MODEL + 3 * ROWS_W_PROJ + ROWS_W_UQ + ROWS_W_UKV
ROWS_AG = 1552
LATE_DEV = (SMALL_SEG * SEG) // (IN_COLS // N_DEV)
assert (SMALL_SEG * SEG + Q_LORA + KV_LORA + QK_ROPE - 1) // (IN_COLS // N_DEV) == LATE_DEV

QK_SCALE = QK_DIM ** -0.5
LOG2E = 1.4426950408889634
LN2 = 0.6931471805599453
Q_PRESCALE = QK_SCALE * LOG2E

MXU_DTYPE = jnp.bfloat16
TRANSPORT_DTYPE = jnp.bfloat16
VMEM_LIMIT = 48 * 1024 * 1024
VMEM_LIMIT_BIG = 60 * 1024 * 1024

T_HGRN = 256
HG_HEADS_PER_STEP = 4
TM_ROW = 256
T_ATT = 1024
T_ATT_BWD = 1024
ATT_SUB = 4
ATT_SUB_BWD = 2
TS_TN = 2048
TR_ADAM = 256

F32 = jnp.float32
MESH = pl.DeviceIdType.MESH


def _dot(a, b):
    return jnp.dot(a, b, preferred_element_type=F32)


def _dot_nt(a, b):
    return lax.dot_general(a, b, (((1,), (1,)), ((), ())), preferred_element_type=F32)


def _dot_tn(a, b):
    return lax.dot_general(a, b, (((0,), (0,)), ((), ())), preferred_element_type=F32)


def _mx(a):
    return a.astype(MXU_DTYPE)


def _sigmoid(x):
    return 1.0 / (1.0 + jnp.exp(-x))


def _params(vmem=VMEM_LIMIT, **kw):
    return pltpu.CompilerParams(vmem_limit_bytes=vmem, **kw)


def _bcast_rows(row, n):
    return jnp.broadcast_to(row, (n, row.shape[-1]))


def _resident(shape):
    return pl.BlockSpec(shape, lambda *_: (0, 0), pipeline_mode=pl.Buffered(1))


HBM_SPEC = pl.BlockSpec(memory_space=pltpu.HBM)


def _all_gather_packed(shard):
    rows, cols = shard.shape

    def body(x_ref, out_ref, send_sems, recv_sems, local_sem):
        x, y, c = lax.axis_index("x"), lax.axis_index("y"), lax.axis_index("c")
        me, sibling = (x, y, c), (x, y, 1 - c)
        chips = [(1 - x, y), (x, 1 - y), (1 - x, 1 - y)]

        def slot(px, py, pc):
            return out_ref.at[4 * px + 2 * py + pc]

        def copy(k, block, to, src=None):
            return pltpu.make_async_remote_copy(
                src_ref=slot(*block) if src is None else src,
                dst_ref=slot(*block),
                send_sem=send_sems.at[k],
                recv_sem=recv_sems.at[k],
                device_id=to,
                device_id_type=MESH,
            )

        mine = pltpu.make_async_copy(x_ref, slot(*me), local_sem)
        mine.start()
        first = [copy(0, me, sibling, src=x_ref)]
        first += [copy(1 + j, me, (*chip, c), src=x_ref) for j, chip in enumerate(chips)]
        for cp in first:
            cp.start()
        passed = [copy(4 + j, (*chip, c), sibling) for j, chip in enumerate(chips)]
        for j, chip in enumerate(chips):
            copy(1 + j, (*chip, c), me).wait_recv()
            passed[j].start()
        copy(0, sibling, me).wait_recv()
        for j, chip in enumerate(chips):
            copy(4 + j, (*chip, 1 - c), me).wait_recv()
        for cp in first + passed:
            cp.wait_send()
        mine.wait()

    return pl.pallas_call(
        body,
        name="ag_weights",
        out_shape=jax.ShapeDtypeStruct((N_DEV, rows, cols), shard.dtype),
        in_specs=[HBM_SPEC],
        out_specs=HBM_SPEC,
        scratch_shapes=[
            pltpu.SemaphoreType.DMA((7,)),
            pltpu.SemaphoreType.DMA((7,)),
            pltpu.SemaphoreType.DMA,
        ],
    )(shard)


class _Exchange:
    def __init__(self, g_refs, recv_refs, send_sems, recv_sems, local_sems):
        x, y, c = lax.axis_index("x"), lax.axis_index("y"), lax.axis_index("c")
        me = 4 * x + 2 * y + c
        n_ops = len(g_refs)

        def copy(i, k, landing):
            px, py, pc = x ^ ((k >> 2) & 1), y ^ ((k >> 1) & 1), c ^ (k & 1)
            peer = 4 * px + 2 * py + pc
            return pltpu.make_async_remote_copy(
                src_ref=g_refs[i].at[peer],
                dst_ref=recv_refs[i].at[peer if landing else me],
                send_sem=send_sems.at[i * (N_DEV - 1) + k - 1],
                recv_sem=recv_sems.at[i * (N_DEV - 1) + k - 1],
                device_id=(px, py, pc),
                device_id_type=MESH,
            )

        pairs = [(i, k) for i in range(n_ops) for k in range(1, N_DEV)]
        self.mine = lambda: [pltpu.make_async_copy(g_refs[i].at[me], recv_refs[i].at[me], local_sems.at[i])
                             for i in range(n_ops)]
        self.sends = lambda: [copy(i, k, False) for i, k in pairs]
        self.landings = lambda: [copy(i, k, True) for i, k in pairs]

    def start(self):
        for cp in self.mine() + self.sends():
            cp.start()

    def wait(self):
        for cp in self.landings():
            cp.wait_recv()
        for cp in self.sends():
            cp.wait_send()
        for cp in self.mine():
            cp.wait()

    @staticmethod
    def semaphores(n_ops):
        return [pltpu.SemaphoreType.DMA((n_ops * (N_DEV - 1),)),
                pltpu.SemaphoreType.DMA((n_ops * (N_DEV - 1),)),
                pltpu.SemaphoreType.DMA((n_ops,))]


def _exchange_rows(slabs):
    def body(g_ref, recv_ref, send_sems, recv_sems, local_sems):
        exchange = _Exchange([g_ref], [recv_ref], send_sems, recv_sems, local_sems)
        exchange.start()
        exchange.wait()

    return pl.pallas_call(
        body,
        name="exchange_rows",
        out_shape=jax.ShapeDtypeStruct(slabs.shape, slabs.dtype),
        in_specs=[HBM_SPEC],
        out_specs=HBM_SPEC,
        scratch_shapes=_Exchange.semaphores(1),
    )(slabs)


def _sum_adamw(recvs, w, m, v, name):
    rows, cols = w.shape
    tr = min(rows, TR_ADAM)
    n_recv = len(recvs)

    def body(*refs):
        w_ref, m_ref, v_ref, g_out, d_out, m_out, v_out = refs[n_recv:]
        g = None
        for r_ref in refs[:n_recv]:
            for i in range(N_DEV):
                part = r_ref[i].astype(F32)
                g = part if g is None else g + part
        g = g[:, :cols]
        m_new = ADAM_B1 * m_ref[...] + (1.0 - ADAM_B1) * g
        v_new = ADAM_B2 * v_ref[...] + (1.0 - ADAM_B2) * (g * g)
        m_hat = m_new / (1.0 - ADAM_B1 ** ADAM_STEP)
        v_hat = v_new / (1.0 - ADAM_B2 ** ADAM_STEP)
        g_out[...] = g
        d_out[...] = -ADAM_LR * (m_hat / (jnp.sqrt(v_hat) + ADAM_EPS) + ADAM_WD * w_ref[...])
        m_out[...] = m_new
        v_out[...] = v_new

    row_spec = pl.BlockSpec((tr, cols), lambda i: (i, 0))
    shape = jax.ShapeDtypeStruct((rows, cols), F32)
    return pl.pallas_call(
        body,
        name=name,
        grid=(rows // tr,),
        in_specs=[pl.BlockSpec((N_DEV, tr, PACK_COLS), lambda i: (0, i, 0))] * n_recv + [row_spec] * 3,
        out_specs=[row_spec] * 4,
        out_shape=[shape] * 4,
        compiler_params=_params(),
    )(*recvs, w, m, v)


def _inproj(x, norm_g, w_in_p):
    s = x.shape[0]
    tm = min(s, TM_ROW)

    def body(x_ref, g_ref, w_ref, proj_ref, h_ref):
        xf = x_ref[...]
        r = lax.rsqrt(jnp.mean(xf * xf, axis=-1, keepdims=True) + EPS)
        h = _mx(xf * r * g_ref[...])
        h_ref[...] = h
        for j in range(PROJ_W // SEG):
            cols = slice(j * SEG, (j + 1) * SEG)
            proj_ref[:, cols] = _dot(h, w_ref[:, cols])

    return pl.pallas_call(
        body,
        name="inproj",
        grid=(s // tm,),
        in_specs=[
            pl.BlockSpec((tm, D_MODEL), lambda i: (i, 0)),
            pl.BlockSpec((1, D_MODEL), lambda i: (0, 0)),
            _resident((D_MODEL, PROJ_W)),
        ],
        out_specs=[
            pl.BlockSpec((tm, PROJ_W), lambda i: (i, 0)),
            pl.BlockSpec((tm, D_MODEL), lambda i: (i, 0)),
        ],
        out_shape=[
            jax.ShapeDtypeStruct((s, PROJ_W), F32),
            jax.ShapeDtypeStruct((s, D_MODEL), MXU_DTYPE),
        ],
        compiler_params=_params(),
    )(x, norm_g, w_in_p)


def _chunk_lower_mask(t):
    row = lax.broadcasted_iota(jnp.int32, (t, t), 0)
    col = lax.broadcasted_iota(jnp.int32, (t, t), 1)
    return ((row // HG_CHUNK) == (col // HG_CHUNK)) & (col <= row)


def _chunk_pos(t):
    return lax.broadcasted_iota(jnp.int32, (t, HEAD_DIM), 0) & (HG_CHUNK - 1)


def _cumsum_chunk(x, pos):
    sh = 1
    while sh < HG_CHUNK:
        x = x + jnp.where(pos >= sh, pltpu.roll(x, sh, 0), 0.0)
        sh *= 2
    return x


def _rcumsum_chunk(x, pos):
    t = x.shape[0]
    sh = 1
    while sh < HG_CHUNK:
        x = x + jnp.where(pos < HG_CHUNK - sh, pltpu.roll(x, t - sh, 0), 0.0)
        sh *= 2
    return x


def _chunk_total(x):
    t, w = x.shape
    tot = jnp.sum(x.reshape(t // HG_CHUNK, HG_CHUNK, w), axis=1, keepdims=True)
    return jnp.broadcast_to(tot, (t // HG_CHUNK, HG_CHUNK, w)).reshape(t, w)


def _hgrn_gates(hq, hf, lb_logits, pos):
    lb = _sigmoid(lb_logits[0:1, :] - lb_logits[1:2, :])
    sig = _sigmoid(hf)
    f = lb + (1.0 - lb) * sig
    sq = _sigmoid(hq)
    q = hq * sq
    k = 1.0 - f
    logf = jnp.log(f)
    bcum = _cumsum_chunk(logf, pos)
    blast = _chunk_total(logf)
    eb = jnp.exp(bcum)
    enb = jnp.exp(-bcum)
    eo = jnp.exp(blast - bcum)
    return dict(lb=lb, sig=sig, f=f, sq=sq, q=q, k=k, eb=eb, enb=enb, eo=eo,
                qi=q * eb, ki=k * enb, ko=k * eo, dec=jnp.exp(blast))


def _hgrn_fwd(proj, lb_logits, hg_norm_g):
    s = proj.shape[0]
    t = min(s, T_HGRN)
    nb = s // t
    nc = t // HG_CHUNK
    hw = HG_HEADS_PER_STEP * HEAD_DIM

    def body(hq_ref, hf_ref, hi_ref, hz_ref, lb_ref, g_ref, o_ref, ya_ref, st_ref, state, u_sc, stb_sc):
        b = pl.program_id(1)

        @pl.when(b == 0)
        def _():
            state[...] = jnp.zeros_like(state)

        lower = _chunk_lower_mask(t)
        pos = _chunk_pos(t)
        for hh in range(HG_HEADS_PER_STEP):
            cols = slice(hh * HEAD_DIM, (hh + 1) * HEAD_DIM)
            st = state[hh]
            st_ref[0, hh] = st
            gt = _hgrn_gates(hq_ref[:, cols], hf_ref[:, cols], lb_ref[:, cols], pos)
            vb = _mx(hi_ref[:, cols])
            qib, kib, kob = _mx(gt["qi"]), _mx(gt["ki"]), _mx(gt["ko"])
            a = jnp.where(lower, _dot_nt(qib, kib), 0.0)
            o_intra = _dot(_mx(a), vb)
            for c in range(nc):
                sl = slice(c * HG_CHUNK, (c + 1) * HG_CHUNK)
                u_sc[hh, c] = _dot_tn(vb[sl], kob[sl])
            for c in range(nc):
                stb_sc[hh, c] = _mx(st)
                st = st * gt["dec"][c * HG_CHUNK:c * HG_CHUNK + 1, :] + u_sc[hh, c]
            state[hh] = st
            outs = []
            for c in range(nc):
                sl = slice(c * HG_CHUNK, (c + 1) * HG_CHUNK)
                outs.append(o_intra[sl] + _dot_nt(qib[sl], stb_sc[hh, c]))
            o = jnp.concatenate(outs, axis=0)
            o_ref[:, cols] = o
            r = lax.rsqrt(jnp.mean(o * o, axis=-1, keepdims=True) + EPS)
            hz = hz_ref[:, cols]
            ya_ref[:, cols] = _mx((o * r * g_ref[...]) * (hz * _sigmoid(hz)))

    hsteps = HEADS // HG_HEADS_PER_STEP

    def seg(k):
        return pl.BlockSpec((t, hw), lambda h, b, k=k: (b, k * hsteps + h))

    return pl.pallas_call(
        body,
        name="hgrn_fwd",
        grid=(hsteps, nb),
        in_specs=[seg(0), seg(1), seg(2), seg(3),
                  pl.BlockSpec((2, hw), lambda h, b: (0, h)),
                  pl.BlockSpec((1, HEAD_DIM), lambda h, b: (0, 0))],
        out_specs=[
            pl.BlockSpec((t, hw), lambda h, b: (b, h)),
            pl.BlockSpec((t, hw), lambda h, b: (b, h)),
            pl.BlockSpec((1, HG_HEADS_PER_STEP, HEAD_DIM, HEAD_DIM), lambda h, b: (b, h, 0, 0)),
        ],
        out_shape=[
            jax.ShapeDtypeStruct((s, D_MODEL), F32),
            jax.ShapeDtypeStruct((s, D_MODEL), MXU_DTYPE),
            jax.ShapeDtypeStruct((nb, HEADS, HEAD_DIM, HEAD_DIM), F32),
        ],
        scratch_shapes=[pltpu.VMEM((HG_HEADS_PER_STEP, HEAD_DIM, HEAD_DIM), F32),
                        pltpu.VMEM((HG_HEADS_PER_STEP, nc, HEAD_DIM, HEAD_DIM), F32),
                        pltpu.VMEM((HG_HEADS_PER_STEP, nc, HEAD_DIM, HEAD_DIM), MXU_DTYPE)],
        compiler_params=_params(),
    )(proj, proj, proj, proj, lb_logits, hg_norm_g)


def _rope(x, cos, sin_a, sin_b):
    return x * cos + pltpu.roll(x, 96, 1) * sin_a + pltpu.roll(x, 32, 1) * sin_b


def _rope_t(d, cos, sin_a, sin_b):
    return d * cos + pltpu.roll(d * sin_a, 32, 1) + pltpu.roll(d * sin_b, 96, 1)


def _mla_prep(proj, q_a_g, kv_a_g, w_uq_p, w_kn, w_v, cos, sin_a, sin_b):
    s = proj.shape[0]
    tm = min(s, TM_ROW)

    def body(sm_ref, gq_ref, gk_ref, wq_ref, wkn_ref, wv_ref, cos_ref, sa_ref, sb_ref,
             q_ref, k_ref, v_ref, cqn_ref, ckvn_ref):
        small = sm_ref[...]
        cq = small[:, :Q_LORA]
        ckv = small[:, Q_LORA:Q_LORA + KV_LORA]
        krp = small[:, Q_LORA + KV_LORA:Q_LORA + KV_LORA + HEAD_DIM]
        rq = lax.rsqrt(jnp.mean(cq * cq, axis=-1, keepdims=True) + EPS)
        rk = lax.rsqrt(jnp.mean(ckv * ckv, axis=-1, keepdims=True) + EPS)
        cqn = _mx(cq * rq * gq_ref[...])
        ckvn = _mx(ckv * rk * gk_ref[...])
        cqn_ref[...] = cqn
        ckvn_ref[...] = ckvn
        q = _dot(cqn, wq_ref[...]) * Q_PRESCALE
        kn = _dot(ckvn, wkn_ref[...])
        v = _dot(ckvn, wv_ref[...])
        cos_t, sa, sb = cos_ref[...], sa_ref[...], sb_ref[...]
        kpe = _mx(_rope(krp, cos_t, sa, sb))
        ones_col = (lax.broadcasted_iota(jnp.int32, (tm, HEAD_DIM), 1) == 0).astype(MXU_DTYPE)
        for h in range(HEADS):
            lo = h * HEAD_PAD
            v_ref[:, lo:lo + HEAD_DIM] = _mx(v[:, h * HEAD_DIM:(h + 1) * HEAD_DIM])
            v_ref[:, lo + HEAD_DIM:lo + HEAD_PAD] = ones_col
            q_ref[:, lo:lo + HEAD_DIM] = _mx(q[:, lo:lo + HEAD_DIM])
            q_ref[:, lo + HEAD_DIM:lo + HEAD_PAD] = _mx(_rope(q[:, lo + HEAD_DIM:lo + HEAD_PAD], cos_t, sa, sb))
            k_ref[:, lo:lo + HEAD_DIM] = _mx(kn[:, h * HEAD_DIM:(h + 1) * HEAD_DIM])
            k_ref[:, lo + HEAD_DIM:lo + HEAD_PAD] = kpe

    def const(shape):
        return pl.BlockSpec(shape, lambda i: (0, 0))

    def rows(w):
        return pl.BlockSpec((tm, w), lambda i: (i, 0))

    return pl.pallas_call(
        body,
        name="mla_prep",
        grid=(s // tm,),
        in_specs=[
            pl.BlockSpec((tm, SEG), lambda i: (i, SMALL_SEG)),
            const((1, Q_LORA)), const((1, KV_LORA)),
            const((Q_LORA, HEADS * HEAD_PAD)), const((KV_LORA, D_MODEL)), const((KV_LORA, D_MODEL)),
            rows(HEAD_DIM), rows(HEAD_DIM), rows(HEAD_DIM),
        ],
        out_specs=[rows(HEADS * HEAD_PAD)] * 3 + [rows(Q_LORA), rows(KV_LORA)],
        out_shape=[
            jax.ShapeDtypeStruct((s, HEADS * HEAD_PAD), MXU_DTYPE),
            jax.ShapeDtypeStruct((s, HEADS * HEAD_PAD), MXU_DTYPE),
            jax.ShapeDtypeStruct((s, HEADS * HEAD_PAD), MXU_DTYPE),
            jax.ShapeDtypeStruct((s, Q_LORA), MXU_DTYPE),
            jax.ShapeDtypeStruct((s, KV_LORA), MXU_DTYPE),
        ],
        compiler_params=_params(),
    )(proj, q_a_g, kv_a_g, w_uq_p, w_kn, w_v, cos, sin_a, sin_b)


def _diag_mask(t):
    row = lax.broadcasted_iota(jnp.int32, (t, t), 0)
    col = lax.broadcasted_iota(jnp.int32, (t, t), 1)
    return row >= col


def _flash_fwd(q_all, k_all, v_all, proj):
    s = q_all.shape[0]
    t = min(s, T_ATT)
    n = s // t

    ts = t // ATT_SUB

    def body(q_ref, k_ref, v_ref, mz_ref, ao_ref, lse_ref, yb_ref, m_sc, acc_sc):
        qi = pl.program_id(1)
        m_sc[...] = jnp.full_like(m_sc, -jnp.inf)
        acc_sc[...] = jnp.zeros_like(acc_sc)

        def key_block(base, diagonal):
            sc, pb, alpha = {}, {}, {}

            def width(r):
                return (r + 1) * ts if diagonal else t

            def scores(r):
                w = width(r)
                s_r = _dot_nt(q_ref[r * ts:(r + 1) * ts], k_ref[pl.ds(base, w), :])
                if diagonal:
                    row = lax.broadcasted_iota(jnp.int32, (ts, w), 0) + r * ts
                    col = lax.broadcasted_iota(jnp.int32, (ts, w), 1)
                    s_r = jnp.where(row >= col, s_r, -jnp.inf)
                sc[r] = s_r

            def softmax(r):
                rs = slice(r * ts, (r + 1) * ts)
                m_prev = m_sc[rs]
                m_new = jnp.maximum(m_prev, jnp.max(sc[r], axis=-1, keepdims=True))
                pb[r] = _mx(jnp.exp2(sc[r] - m_new))
                alpha[r] = jnp.exp2(m_prev - m_new)
                m_sc[rs] = m_new

            def weighted_values(r):
                rs = slice(r * ts, (r + 1) * ts)
                acc_sc[rs] = alpha[r] * acc_sc[rs] + _dot(pb[r], v_ref[pl.ds(base, width(r)), :])

            for step in range(ATT_SUB + 2):
                if step < ATT_SUB:
                    scores(step)
                if 1 <= step <= ATT_SUB:
                    softmax(step - 1)
                if step >= 2:
                    weighted_values(step - 2)

        def below_diagonal(ki, carry):
            key_block(pl.multiple_of(ki * t, t), False)
            return carry

        lax.fori_loop(0, qi, below_diagonal, 0)
        key_block(pl.multiple_of(qi * t, t), True)

        acc = acc_sc[...]
        l = acc[:, HEAD_DIM:HEAD_DIM + 1]
        ao = acc[:, :HEAD_DIM] / l
        ao_ref[...] = ao
        lse_ref[...] = jnp.broadcast_to(m_sc[...] + jnp.log2(l), (t, HEAD_DIM))
        mz = mz_ref[...]
        yb_ref[...] = _mx(ao * (mz * _sigmoid(mz)))

    q_map = lambda h, qi: (qi, h)
    return pl.pallas_call(
        body,
        name="flash_fwd",
        grid=(HEADS, n),
        in_specs=[
            pl.BlockSpec((t, HEAD_PAD), q_map),
            pl.BlockSpec((s, HEAD_PAD), lambda h, qi: (0, h)),
            pl.BlockSpec((s, HEAD_PAD), lambda h, qi: (0, h)),
            pl.BlockSpec((t, HEAD_DIM), lambda h, qi: (qi, MZ_SEG * HEADS + h)),
        ],
        out_specs=[pl.BlockSpec((t, HEAD_DIM), q_map)] * 3,
        out_shape=[
            jax.ShapeDtypeStruct((s, D_MODEL), F32),
            jax.ShapeDtypeStruct((s, D_MODEL), F32),
            jax.ShapeDtypeStruct((s, D_MODEL), MXU_DTYPE),
        ],
        scratch_shapes=[
            pltpu.VMEM((t, 1), F32),
            pltpu.VMEM((t, HEAD_PAD), F32),
        ],
        compiler_params=_params(),
    )(q_all, k_all, v_all, proj)


def _merge_fwd_bwd(x, target, ya, yb, proj, b_gate, final_g, w_pa, w_pb, w_out):
    s = x.shape[0]
    tm = min(s, TM_ROW)

    def body(x_ref, t_ref, ya_ref, yb_ref, g0_ref, g1_ref, bg_ref, fg_ref, wpa_ref, wpb_ref, wo_ref,
             dx2_ref, dya_ref, dyb_ref, dg0_ref, dg1_ref, mb_ref, dpab_ref, dpbb_ref, dx2b_ref,
             loss_ref, dfg_ref, dbg_ref):
        i = pl.program_id(0)

        @pl.when(i == 0)
        def _():
            loss_ref[...] = jnp.zeros_like(loss_ref)
            dfg_ref[...] = jnp.zeros_like(dfg_ref)
            dbg_ref[...] = jnp.zeros_like(dbg_ref)

        pa = _dot(ya_ref[...], wpa_ref[...])
        pb = _dot(yb_ref[...], wpb_ref[...])
        bg = bg_ref[...]
        g0 = _sigmoid(g0_ref[...] + bg[:, :D_MODEL])
        g1 = _sigmoid(g1_ref[...] + bg[:, D_MODEL:])
        merged = g0 * pa + g1 * pb
        mb = _mx(merged)
        mb_ref[...] = mb
        x2 = x_ref[...] + _dot(mb, wo_ref[...])
        r = lax.rsqrt(jnp.mean(x2 * x2, axis=-1, keepdims=True) + EPS)
        xn = x2 * r
        fg = fg_ref[...]
        diff = xn * fg - t_ref[...]
        loss_ref[...] += 0.5 * jnp.sum(jnp.mean(diff * diff, axis=-1, keepdims=True))
        dy = diff * (1.0 / D_MODEL)
        dfg_ref[...] += _bcast_rows(jnp.sum(dy * xn, axis=0, keepdims=True), 8)
        tt = dy * fg
        dx2 = r * (tt - xn * jnp.mean(tt * xn, axis=-1, keepdims=True))
        dx2_ref[...] = dx2
        dx2b = _mx(dx2)
        dx2b_ref[...] = dx2b
        dmerged = _dot_nt(dx2b, wo_ref[...])
        dpa = _mx(dmerged * g0)
        dpb = _mx(dmerged * g1)
        dpab_ref[...] = dpa
        dpbb_ref[...] = dpb
        dg0 = dmerged * pa * (g0 * (1.0 - g0))
        dg1 = dmerged * pb * (g1 * (1.0 - g1))
        dg0_ref[...] = _mx(dg0)
        dg1_ref[...] = _mx(dg1)
        dbg_ref[:, :D_MODEL] += _bcast_rows(jnp.sum(dg0, axis=0, keepdims=True), 8)
        dbg_ref[:, D_MODEL:] += _bcast_rows(jnp.sum(dg1, axis=0, keepdims=True), 8)
        dya_ref[...] = _dot_nt(dpa, wpa_ref[...])
        dyb_ref[...] = _dot_nt(dpb, wpb_ref[...])

    def rows(w=D_MODEL):
        return pl.BlockSpec((tm, w), lambda i: (i, 0))

    def const(shape):
        return pl.BlockSpec(shape, lambda i: (0, 0))

    f32 = jax.ShapeDtypeStruct((s, D_MODEL), F32)
    b16 = jax.ShapeDtypeStruct((s, D_MODEL), MXU_DTYPE)
    return pl.pallas_call(
        body,
        name="merge_fwd_bwd",
        grid=(s // tm,),
        in_specs=[
            rows(), rows(), rows(), rows(),
            pl.BlockSpec((tm, SEG), lambda i: (i, GL_SEG)),
            pl.BlockSpec((tm, SEG), lambda i: (i, GL_SEG + 1)),
            const((1, 2 * D_MODEL)), const((1, D_MODEL)),
            const((D_MODEL, D_MODEL)), const((D_MODEL, D_MODEL)), const((D_MODEL, D_MODEL)),
        ],
        out_specs=[rows()] * 9 + [const((8, HEAD_DIM)), const((8, D_MODEL)), const((8, 2 * D_MODEL))],
        out_shape=[f32, f32, f32, b16, b16, b16, b16, b16, b16,
                   jax.ShapeDtypeStruct((8, HEAD_DIM), F32),
                   jax.ShapeDtypeStruct((8, D_MODEL), F32),
                   jax.ShapeDtypeStruct((8, 2 * D_MODEL), F32)],
        compiler_params=_params(),
    )(x, target, ya, yb, proj, proj, b_gate, final_g, w_pa, w_pb, w_out)


def _attn_gate_bwd(dyb, ao, proj):
    s = dyb.shape[0]
    tm = min(s, TM_ROW)

    def body(dyb_ref, ao_ref, mz_ref, dao_ref, dmz_ref):
        mz = mz_ref[...]
        sg = _sigmoid(mz)
        d = dyb_ref[...]
        dao_ref[...] = _mx(d * (mz * sg))
        dmz_ref[...] = _mx(d * ao_ref[...] * (sg + mz * sg * (1.0 - sg)))

    rows = pl.BlockSpec((tm, D_MODEL), lambda i: (i, 0))
    b16 = jax.ShapeDtypeStruct((s, D_MODEL), MXU_DTYPE)
    return pl.pallas_call(
        body,
        name="attn_gate_bwd",
        grid=(s // tm,),
        in_specs=[rows, rows, pl.BlockSpec((tm, SEG), lambda i: (i, MZ_SEG))],
        out_specs=[rows, rows],
        out_shape=[b16, b16],
        compiler_params=_params(),
    )(dyb, ao, proj)


def _flash_bwd(q_all, k_all, v_all, dao, ao, lse, slab_sets):
    s = q_all.shape[0]
    t = min(s, T_ATT_BWD)
    n = s // t
    pairs = [(ki, qi) for ki in range(n) for qi in range(ki, n)]
    ki_list = jnp.asarray([p[0] for p in pairs], jnp.int32)
    qi_list = jnp.asarray([p[1] for p in pairs], jnp.int32)
    n_ops = len(slab_sets)

    def body(ki_ref, qi_ref, q_ref, k_ref, v_ref, do_ref, ao_ref, lse_ref, *rest):
        g_refs = rest[:n_ops]
        dq_ref, dk_ref, dv_ref = rest[n_ops:n_ops + 3]
        recv_refs = rest[n_ops + 3:2 * n_ops + 3]
        dk_acc, dv_acc, send_sems, recv_sems, local_sems = rest[2 * n_ops + 3:]
        head, step = pl.program_id(0), pl.program_id(1)
        ki, qi = ki_ref[step], qi_ref[step]

        @pl.when((head == 0) & (step == 0))
        def _():
            _Exchange(g_refs, recv_refs, send_sems, recv_sems, local_sems).start()

        @pl.when(qi == ki)
        def _():
            dk_acc[...] = jnp.zeros_like(dk_acc)
            dv_acc[...] = jnp.zeros_like(dv_acc)

        @pl.when(ki == 0)
        def _():
            dq_ref[pl.ds(pl.multiple_of(qi * t, t), t), :] = jnp.zeros((t, HEAD_PAD), F32)

        def pair(masked):
            nsub = ATT_SUB if masked else ATT_SUB_BWD
            ts = t // nsub
            dk_parts, dv_parts = [], []
            for r in range(nsub):
                rs = slice(r * ts, (r + 1) * ts)
                w = (r + 1) * ts if masked else t
                k = k_ref[:w]
                v = v_ref[:w, :HEAD_DIM]
                q = q_ref[rs]
                sc = _dot_nt(q, k)
                if masked:
                    row = lax.broadcasted_iota(jnp.int32, (ts, w), 0) + r * ts
                    col = lax.broadcasted_iota(jnp.int32, (ts, w), 1)
                    sc = jnp.where(row >= col, sc, -jnp.inf)
                p = jnp.exp2(sc - lse_ref[rs, 0:1])
                do = do_ref[rs]
                delta = jnp.sum(do.astype(F32) * ao_ref[rs], axis=-1, keepdims=True)
                dv_part = _dot_tn(_mx(p), do)
                ds = _mx(p * (_dot_nt(do, v) - delta))
                dk_part = _dot_tn(ds, q)
                rows = pl.ds(pl.multiple_of(qi * t + r * ts, ts), ts)
                dq_ref[rows, :] += _dot(ds, k)
                if masked:
                    dk_acc[:w] += dk_part
                    dv_acc[:w] += dv_part
                else:
                    dk_parts.append(dk_part)
                    dv_parts.append(dv_part)

            if not masked:
                dk_acc[...] += sum(dk_parts[1:], dk_parts[0])
                dv_acc[...] += sum(dv_parts[1:], dv_parts[0])

        @pl.when(qi == ki)
        def _():
            pair(True)

        @pl.when(qi > ki)
        def _():
            pair(False)

        @pl.when(qi == n - 1)
        def _():
            dk_ref[...] = dk_acc[...] * LN2
            dv_ref[...] = dv_acc[...]

        @pl.when((head == HEADS - 1) & (step == len(pairs) - 1))
        def _():
            _Exchange(g_refs, recv_refs, send_sems, recv_sems, local_sems).wait()

    q_map = lambda h, p, ki_ref, qi_ref: (qi_ref[p], h)
    kv_map = lambda h, p, ki_ref, qi_ref: (ki_ref[p], h)
    grid_spec = pltpu.PrefetchScalarGridSpec(
        num_scalar_prefetch=2,
        grid=(HEADS, len(pairs)),
        in_specs=[
            pl.BlockSpec((t, HEAD_PAD), q_map),
            pl.BlockSpec((t, HEAD_PAD), kv_map),
            pl.BlockSpec((t, HEAD_PAD), kv_map),
            pl.BlockSpec((t, HEAD_DIM), q_map),
            pl.BlockSpec((t, HEAD_DIM), q_map),
            pl.BlockSpec((t, HEAD_DIM), q_map),
        ] + [HBM_SPEC] * n_ops,
        out_specs=[
            pl.BlockSpec((s, HEAD_PAD), lambda h, p, ki_ref, qi_ref: (0, h)),
            pl.BlockSpec((t, HEAD_PAD), kv_map),
            pl.BlockSpec((t, HEAD_DIM), kv_map),
        ] + [HBM_SPEC] * n_ops,
        scratch_shapes=[pltpu.VMEM((t, HEAD_PAD), F32), pltpu.VMEM((t, HEAD_DIM), F32)]
        + _Exchange.semaphores(n_ops),
    )
    outs = pl.pallas_call(
        body,
        name="flash_bwd",
        grid_spec=grid_spec,
        out_shape=[
            jax.ShapeDtypeStruct((s, HEADS * HEAD_PAD), F32),
            jax.ShapeDtypeStruct((s, HEADS * HEAD_PAD), F32),
            jax.ShapeDtypeStruct((s, D_MODEL), F32),
        ] + [jax.ShapeDtypeStruct(a.shape, a.dtype) for a in slab_sets],
        compiler_params=_params(VMEM_LIMIT_BIG),
    )(ki_list, qi_list, q_all, k_all, v_all, dao, ao, lse, *slab_sets)
    return outs[0], outs[1], outs[2], outs[3:]


def _mla_prep_bwd(dq_all, dk_all, dv_all, proj, q_a_g, kv_a_g, w_uq_p, w_kn, w_v, cos, sin_a, sin_b):
    s = proj.shape[0]
    tm = min(s, TM_ROW)

    def body(dq_ref, dk_ref, dv_ref, sm_ref, gq_ref, gk_ref, wq_ref, wkn_ref, wv_ref, cos_ref, sa_ref, sb_ref,
             dsm_ref, dqf_ref, dkn_ref, dvb_ref, dgq_ref, dgk_ref):
        i = pl.program_id(0)

        @pl.when(i == 0)
        def _():
            dgq_ref[...] = jnp.zeros_like(dgq_ref)
            dgk_ref[...] = jnp.zeros_like(dgk_ref)

        cos_t, sa, sb = cos_ref[...], sa_ref[...], sb_ref[...]
        dkpe = jnp.zeros((tm, HEAD_DIM), F32)
        for h in range(HEADS):
            lo = h * HEAD_PAD
            dqf_ref[:, lo:lo + HEAD_DIM] = _mx(dq_ref[:, lo:lo + HEAD_DIM] * QK_SCALE)
            dqf_ref[:, lo + HEAD_DIM:lo + HEAD_PAD] = _mx(
                _rope_t(dq_ref[:, lo + HEAD_DIM:lo + HEAD_PAD] * QK_SCALE, cos_t, sa, sb))
            dkn_ref[:, h * HEAD_DIM:(h + 1) * HEAD_DIM] = _mx(dk_ref[:, lo:lo + HEAD_DIM])
            dkpe = dkpe + dk_ref[:, lo + HEAD_DIM:lo + HEAD_PAD]
        dkr = _rope_t(dkpe, cos_t, sa, sb)
        dvb = _mx(dv_ref[...])
        dvb_ref[...] = dvb
        dcqn = _dot_nt(dqf_ref[...], wq_ref[...])
        dckvn = _dot_nt(dkn_ref[...], wkn_ref[...]) + _dot_nt(dvb, wv_ref[...])

        small = sm_ref[...]
        cq = small[:, :Q_LORA]
        ckv = small[:, Q_LORA:Q_LORA + KV_LORA]
        rq = lax.rsqrt(jnp.mean(cq * cq, axis=-1, keepdims=True) + EPS)
        rk = lax.rsqrt(jnp.mean(ckv * ckv, axis=-1, keepdims=True) + EPS)
        cqh = cq * rq
        ckh = ckv * rk
        dgq_ref[...] += _bcast_rows(jnp.sum(dcqn * cqh, axis=0, keepdims=True), 8)
        dgk_ref[...] += _bcast_rows(jnp.sum(dckvn * ckh, axis=0, keepdims=True), 8)
        tq = dcqn * gq_ref[...]
        tk = dckvn * gk_ref[...]
        dcq = rq * (tq - cqh * jnp.mean(tq * cqh, axis=-1, keepdims=True))
        dckv = rk * (tk - ckh * jnp.mean(tk * ckh, axis=-1, keepdims=True))
        dsm_ref[:, :Q_LORA] = _mx(dcq)
        dsm_ref[:, Q_LORA:Q_LORA + KV_LORA] = _mx(dckv)
        dsm_ref[:, Q_LORA + KV_LORA:Q_LORA + KV_LORA + HEAD_DIM] = _mx(dkr)
        dsm_ref[:, Q_LORA + KV_LORA + HEAD_DIM:] = jnp.zeros((tm, SEG - Q_LORA - KV_LORA - HEAD_DIM), MXU_DTYPE)

    def const(shape):
        return pl.BlockSpec(shape, lambda i: (0, 0))

    def rows(w):
        return pl.BlockSpec((tm, w), lambda i: (i, 0))

    return pl.pallas_call(
        body,
        name="mla_prep_bwd",
        grid=(s // tm,),
        in_specs=[
            rows(HEADS * HEAD_PAD), rows(HEADS * HEAD_PAD), rows(D_MODEL),
            pl.BlockSpec((tm, SEG), lambda i: (i, SMALL_SEG)),
            const((1, Q_LORA)), const((1, KV_LORA)),
            const((Q_LORA, HEADS * HEAD_PAD)), const((KV_LORA, D_MODEL)), const((KV_LORA, D_MODEL)),
            rows(HEAD_DIM), rows(HEAD_DIM), rows(HEAD_DIM),
        ],
        out_specs=[rows(SEG), rows(HEADS * HEAD_PAD), rows(D_MODEL), rows(D_MODEL),
                   const((8, Q_LORA)), const((8, KV_LORA))],
        out_shape=[
            jax.ShapeDtypeStruct((s, SEG), MXU_DTYPE),
            jax.ShapeDtypeStruct((s, HEADS * HEAD_PAD), MXU_DTYPE),
            jax.ShapeDtypeStruct((s, D_MODEL), MXU_DTYPE),
            jax.ShapeDtypeStruct((s, D_MODEL), MXU_DTYPE),
            jax.ShapeDtypeStruct((8, Q_LORA), F32),
            jax.ShapeDtypeStruct((8, KV_LORA), F32),
        ],
        compiler_params=_params(),
    )(dq_all, dk_all, dv_all, proj, q_a_g, kv_a_g, w_uq_p, w_kn, w_v, cos, sin_a, sin_b)


def _hgrn_bwd(proj, lb_logits, hg_norm_g, o_all, dya, states):
    s = proj.shape[0]
    t = min(s, T_HGRN)
    nb = s // t
    nc = t // HG_CHUNK

    def body(hq_ref, hf_ref, hi_ref, hz_ref, lb_ref, g_ref, o_ref, dya_ref, st_ref,
             dhq_ref, dhf_ref, dhi_ref, dhz_ref, dlb_ref, dg_ref, dstate, u_sc, g_sc, stf_sc, stb_sc, dstb_sc):
        h, b = pl.program_id(0), pl.program_id(1)

        @pl.when(b == 0)
        def _():
            dstate[...] = jnp.zeros_like(dstate)
            dlb_ref[...] = jnp.zeros_like(dlb_ref)

        @pl.when((b == 0) & (h == 0))
        def _():
            dg_ref[...] = jnp.zeros_like(dg_ref)

        lower = _chunk_lower_mask(t)
        pos = _chunk_pos(t)
        ghg = g_ref[...]
        for hh in range(HG_HEADS_PER_STEP):
            cols = slice(hh * HEAD_DIM, (hh + 1) * HEAD_DIM)
            hq, hf, hz = hq_ref[:, cols], hf_ref[:, cols], hz_ref[:, cols]
            gt = _hgrn_gates(hq, hf, lb_ref[:, cols], pos)
            vb = _mx(hi_ref[:, cols])
            qi, ki, ko = gt["qi"], gt["ki"], gt["ko"]
            qib, kib, kob = _mx(qi), _mx(ki), _mx(ko)

            o = o_ref[:, cols]
            sz = _sigmoid(hz)
            r = lax.rsqrt(jnp.mean(o * o, axis=-1, keepdims=True) + EPS)
            on = o * r
            dya_t = dya_ref[:, cols]
            don = dya_t * (hz * sz)
            dhz_ref[:, cols] = _mx(dya_t * (on * ghg) * (sz + hz * sz * (1.0 - sz)))
            dg_ref[...] += _bcast_rows(jnp.sum(don * on, axis=0, keepdims=True), 8)
            tt = don * ghg
            do = r * (tt - on * jnp.mean(tt * on, axis=-1, keepdims=True))
            dob = _mx(do)

            for c in range(nc):
                sl = slice(c * HG_CHUNK, (c + 1) * HG_CHUNK)
                u_sc[hh, c] = _dot_tn(vb[sl], kob[sl])
                g_sc[hh, c] = _dot_tn(dob[sl], qib[sl])

            st = st_ref[0, hh]
            for c in range(nc):
                stf_sc[hh, c] = st
                stb_sc[hh, c] = _mx(st)
                if c < nc - 1:
                    st = st * gt["dec"][c * HG_CHUNK:c * HG_CHUNK + 1, :] + u_sc[hh, c]

            dst = dstate[hh]
            dd_parts = [None] * nc
            for c in reversed(range(nc)):
                dec = gt["dec"][c * HG_CHUNK:c * HG_CHUNK + 1, :]
                dstb_sc[hh, c] = _mx(dst)
                dd_parts[c] = _bcast_rows(jnp.sum(dst * stf_sc[hh, c], axis=0, keepdims=True) * dec, HG_CHUNK)
                dst = dst * dec + g_sc[hh, c]
            dstate[hh] = dst

            a = jnp.where(lower, _dot_nt(qib, kib), 0.0)
            da = _mx(jnp.where(lower, _dot_nt(dob, vb), 0.0))
            dqi_intra = _dot(da, kib)
            dki = _dot_tn(da, qib)
            dv_intra = _dot_tn(_mx(a), dob)

            dqi_parts, dko_parts, dv_parts = [None] * nc, [None] * nc, [None] * nc
            for c in range(nc):
                sl = slice(c * HG_CHUNK, (c + 1) * HG_CHUNK)
                dv_parts[c] = dv_intra[sl] + _dot_nt(kob[sl], dstb_sc[hh, c])
                dko_parts[c] = _dot(vb[sl], dstb_sc[hh, c])
                dqi_parts[c] = dqi_intra[sl] + _dot(dob[sl], stb_sc[hh, c])
            dqi = jnp.concatenate(dqi_parts, axis=0)
            dko = jnp.concatenate(dko_parts, axis=0)
            dv = jnp.concatenate(dv_parts, axis=0)
            dd = jnp.concatenate(dd_parts, axis=0)

            dq = dqi * gt["eb"]
            dk = dki * gt["enb"] + dko * gt["eo"]
            db = dqi * qi - dki * ki - dko * ko
            dlogf = _rcumsum_chunk(db, pos) + _chunk_total(dko * ko) + dd
            df = dlogf / gt["f"] - dk
            lb, sig, sq = gt["lb"], gt["sig"], gt["sq"]
            dhf_ref[:, cols] = _mx(df * (1.0 - lb) * (sig * (1.0 - sig)))
            dhq_ref[:, cols] = _mx(dq * (sq + hq * sq * (1.0 - sq)))
            dhi_ref[:, cols] = _mx(dv)
            dlb = jnp.sum(df * (1.0 - sig), axis=0, keepdims=True) * (lb * (1.0 - lb))
            dlb_ref[:, cols] += jnp.concatenate([dlb, -dlb], axis=0)

    hw = HG_HEADS_PER_STEP * HEAD_DIM
    hsteps = HEADS // HG_HEADS_PER_STEP

    def seg(k):
        return pl.BlockSpec((t, hw), lambda h, b, k=k: (nb - 1 - b, k * hsteps + h))

    blk = pl.BlockSpec((t, hw), lambda h, b: (nb - 1 - b, h))
    b16 = jax.ShapeDtypeStruct((s, D_MODEL), MXU_DTYPE)
    return pl.pallas_call(
        body,
        name="hgrn_bwd",
        grid=(hsteps, nb),
        in_specs=[seg(0), seg(1), seg(2), seg(3),
                  pl.BlockSpec((2, hw), lambda h, b: (0, h)),
                  pl.BlockSpec((1, HEAD_DIM), lambda h, b: (0, 0)),
                  blk, blk,
                  pl.BlockSpec((1, HG_HEADS_PER_STEP, HEAD_DIM, HEAD_DIM), lambda h, b: (nb - 1 - b, h, 0, 0))],
        out_specs=[blk, blk, blk, blk,
                   pl.BlockSpec((2, hw), lambda h, b: (0, h)),
                   pl.BlockSpec((8, HEAD_DIM), lambda h, b: (0, 0))],
        out_shape=[b16, b16, b16, b16,
                   jax.ShapeDtypeStruct((2, D_MODEL), F32),
                   jax.ShapeDtypeStruct((8, HEAD_DIM), F32)],
        scratch_shapes=[pltpu.VMEM((HG_HEADS_PER_STEP, HEAD_DIM, HEAD_DIM), F32)]
        + [pltpu.VMEM((HG_HEADS_PER_STEP, nc, HEAD_DIM, HEAD_DIM), F32)] * 3
        + [pltpu.VMEM((HG_HEADS_PER_STEP, nc, HEAD_DIM, HEAD_DIM), MXU_DTYPE)] * 2,
        compiler_params=_params(),
    )(proj, proj, proj, proj, lb_logits, hg_norm_g, o_all, dya, states)


def _dh_bwd(segs, w_in_p, x, dx2, norm_g, late_slab, late_recv_init, misc_slabs):
    s = x.shape[0]
    tm = min(s, TM_ROW)
    nseg = len(segs)
    nsteps = s // tm
    late_xyc = ((LATE_DEV >> 2) & 1, (LATE_DEV >> 1) & 1, LATE_DEV & 1)

    def body(*refs):
        seg_refs = refs[:nseg]
        (w_ref, x_ref, dx2_ref, g_ref, late_ref, _, misc_ref,
         gx_ref, dng_ref, late_recv_ref, misc_recv_ref,
         dp_buf, send_sems, recv_sems, local_sems, late_send, late_recvs, late_local) = refs[nseg:]
        i = pl.program_id(0)
        me = 4 * lax.axis_index("x") + 2 * lax.axis_index("y") + lax.axis_index("c")

        def misc_exchange():
            return _Exchange([misc_ref], [misc_recv_ref], send_sems, recv_sems, local_sems)

        def late_copy(sender):
            return pltpu.make_async_remote_copy(
                src_ref=late_ref.at[0], dst_ref=late_recv_ref.at[sender], send_sem=late_send,
                recv_sem=late_recvs.at[(sender ^ LATE_DEV) - 1], device_id=late_xyc, device_id_type=MESH)

        def late_own():
            return pltpu.make_async_copy(late_ref.at[0], late_recv_ref.at[LATE_DEV], late_local)

        @pl.when(i == 0)
        def _():
            dng_ref[...] = jnp.zeros_like(dng_ref)
            misc_exchange().start()

        @pl.when((i == 0) & (me != LATE_DEV))
        def _():
            late_copy(me).start()

        @pl.when((i == 0) & (me == LATE_DEV))
        def _():
            late_own().start()

        for k, sref in enumerate(seg_refs):
            dp_buf[:, k * SEG:(k + 1) * SEG] = sref[...]
        dh = _dot_nt(dp_buf[...], w_ref[...])
        xf = x_ref[...]
        r = lax.rsqrt(jnp.mean(xf * xf, axis=-1, keepdims=True) + EPS)
        xh = xf * r
        dng_ref[...] += _bcast_rows(jnp.sum(dh * xh, axis=0, keepdims=True), 8)
        tt = dh * g_ref[...]
        gx_ref[...] = dx2_ref[...] + r * (tt - xh * jnp.mean(tt * xh, axis=-1, keepdims=True))

        @pl.when(i == nsteps - 1)
        def _():
            misc_exchange().wait()

        @pl.when((i == nsteps - 1) & (me != LATE_DEV))
        def _():
            late_copy(me).wait_send()

        @pl.when((i == nsteps - 1) & (me == LATE_DEV))
        def _():
            for k in range(1, N_DEV):
                late_copy(LATE_DEV ^ k).wait_recv()
            late_own().wait()

    rows = pl.BlockSpec((tm, D_MODEL), lambda i: (i, 0))
    return pl.pallas_call(
        body,
        name="dh_bwd",
        grid=(nsteps,),
        in_specs=[pl.BlockSpec((tm, SEG), lambda i: (i, 0))] * nseg + [
            _resident((D_MODEL, PROJ_W)),
            rows, rows,
            pl.BlockSpec((1, D_MODEL), lambda i: (0, 0)),
            HBM_SPEC, HBM_SPEC, HBM_SPEC,
        ],
        out_specs=[rows, pl.BlockSpec((8, D_MODEL), lambda i: (0, 0)), HBM_SPEC, HBM_SPEC],
        out_shape=[jax.ShapeDtypeStruct((s, D_MODEL), F32), jax.ShapeDtypeStruct((8, D_MODEL), F32),
                   jax.ShapeDtypeStruct(late_recv_init.shape, late_recv_init.dtype),
                   jax.ShapeDtypeStruct(misc_slabs.shape, misc_slabs.dtype)],
        input_output_aliases={nseg + 5: 2},
        scratch_shapes=[pltpu.VMEM((tm, PROJ_W), MXU_DTYPE)] + _Exchange.semaphores(1)
        + [pltpu.SemaphoreType.DMA, pltpu.SemaphoreType.DMA((N_DEV - 1,)), pltpu.SemaphoreType.DMA],
        compiler_params=_params(),
    )(*segs, w_in_p, x, dx2, norm_g, late_slab, late_recv_init, misc_slabs)


def _matmul_tn(a, b, name, out_dtype=F32):
    s, m = a.shape
    n = b.shape[1]
    ts = min(s, TS_TN)
    tn = min(n, SEG)
    nk = s // ts

    def body(a_ref, b_ref, o_ref, acc):
        k = pl.program_id(1)
        part = _dot_tn(a_ref[...], b_ref[...])

        @pl.when(k == 0)
        def _():
            acc[...] = part

        @pl.when(k > 0)
        def _():
            acc[...] += part

        @pl.when(k == nk - 1)
        def _():
            o_ref[...] = acc[...].astype(out_dtype)

    return pl.pallas_call(
        body,
        name=name,
        grid=(n // tn, nk),
        in_specs=[pl.BlockSpec((ts, m), lambda j, k: (k, 0)), pl.BlockSpec((ts, tn), lambda j, k: (k, j))],
        out_specs=pl.BlockSpec((m, tn), lambda j, k: (0, j)),
        out_shape=jax.ShapeDtypeStruct((m, n), out_dtype),
        scratch_shapes=[pltpu.VMEM((m, tn), F32)],
        compiler_params=_params(),
    )(a, b)


def _w_in_pieces():
    per = IN_COLS // N_DEV
    pad_at = SMALL_SEG * SEG + Q_LORA + KV_LORA + QK_ROPE
    pieces = []
    for j in range(N_DEV):
        u0, u1 = j * per, (j + 1) * per
        cuts = [u0] + ([pad_at] if u0 < pad_at < u1 else []) + [u1]
        for a, b in zip(cuts[:-1], cuts[1:]):
            pieces.append((j, a - u0, b - u0, a if a < pad_at else a + PROJ_W - IN_COLS))
    return pad_at, pieces


def _assemble_w_in(gathered):
    tr = TM_ROW
    pad_at, pieces = _w_in_pieces()

    def body(in_ref, out_ref):
        out_ref[:, pad_at:pad_at + PROJ_W - IN_COLS] = jnp.zeros((tr, PROJ_W - IN_COLS), gathered.dtype)
        for j, a, b, p0 in pieces:
            out_ref[:, p0:p0 + b - a] = in_ref[j, :, a:b]

    return pl.pallas_call(
        body,
        name="assemble_w_in",
        grid=(D_MODEL // tr,),
        in_specs=[pl.BlockSpec((N_DEV, tr, PACK_COLS), lambda i: (0, i, 0))],
        out_specs=pl.BlockSpec((tr, PROJ_W), lambda i: (i, 0)),
        out_shape=jax.ShapeDtypeStruct((D_MODEL, PROJ_W), gathered.dtype),
        compiler_params=_params(),
    )(gathered)


def _scatter_dw_in(dw_segs, devs, name):
    tr = TM_ROW
    _, pieces = _w_in_pieces()
    per = IN_COLS // N_DEV
    seg_ids = sorted(dw_segs)
    nseg = len(seg_ids)

    def body(*refs):
        out_ref, buf = refs[nseg:]
        for k in range(PROJ_W // SEG):
            if k in seg_ids:
                buf[:, k * SEG:(k + 1) * SEG] = refs[seg_ids.index(k)][...]
            else:
                buf[:, k * SEG:(k + 1) * SEG] = jnp.zeros((tr, SEG), F32)
        for slot, dev in enumerate(devs):
            out_ref[slot, :, per:] = jnp.zeros((tr, PACK_COLS - per), TRANSPORT_DTYPE)
            for j, a, b, p0 in pieces:
                if j == dev:
                    out_ref[slot, :, a:b] = buf[:, p0:p0 + b - a].astype(TRANSPORT_DTYPE)

    return pl.pallas_call(
        body,
        name=name,
        grid=(D_MODEL // tr,),
        in_specs=[pl.BlockSpec((tr, SEG), lambda i: (i, 0))] * nseg,
        out_specs=pl.BlockSpec((len(devs), tr, PACK_COLS), lambda i: (0, i, 0)),
        out_shape=jax.ShapeDtypeStruct((len(devs), D_MODEL, PACK_COLS), TRANSPORT_DTYPE),
        scratch_shapes=[pltpu.VMEM((tr, PROJ_W), F32)],
        compiler_params=_params(),
    )(*[dw_segs[k] for k in seg_ids])


def _rope_tables(s):
    inv = ROPE_THETA ** (-jnp.arange(0, QK_ROPE, 2, dtype=F32) / QK_ROPE)
    ang = jnp.arange(s, dtype=F32)[:, None] * inv[None, :]
    cos, sin = jnp.cos(ang), jnp.sin(ang)
    z32 = jnp.zeros_like(cos)
    z64 = jnp.zeros((s, HEAD_DIM - QK_ROPE), F32)
    cos_t = jnp.concatenate([cos, cos, z64], axis=1)
    sin_a = jnp.concatenate([-sin, z32, z64], axis=1)
    sin_b = jnp.concatenate([z32, sin, z64], axis=1)
    return cos_t, sin_a, sin_b


def _pack_misc(w_uq, w_ukv, norm_g, b_gate, lb_logits, hg_norm_g, q_a_g, kv_a_g, final_norm_g, extra):
    misc = jnp.concatenate([hg_norm_g.reshape(-1), q_a_g.reshape(-1), kv_a_g.reshape(-1), extra.reshape(-1),
                            jnp.zeros((PACK_COLS - HEAD_DIM - Q_LORA - KV_LORA - 1,), F32)])
    return jnp.concatenate([w_uq.reshape(ROWS_W_UQ, PACK_COLS), w_ukv.reshape(ROWS_W_UKV, PACK_COLS),
                            norm_g.reshape(1, -1), b_gate.reshape(2, -1), lb_logits.reshape(2, -1),
                            misc.reshape(1, -1), final_norm_g.reshape(1, -1), jnp.zeros((1, PACK_COLS), F32)], axis=0)


def _unpack_misc(p):
    sm = p[ROWS_W_UQ + ROWS_W_UKV:]
    misc = sm[5]
    return dict(
        w_uq=p[:ROWS_W_UQ].reshape(1, Q_LORA, QK_DIM),
        w_ukv=p[ROWS_W_UQ:ROWS_W_UQ + ROWS_W_UKV].reshape(1, KV_LORA, 2 * HEAD_DIM),
        norm_g=sm[0:1], b_gate=sm[1:3].reshape(1, -1), lb_logits=sm[3:5],
        hg_norm_g=misc[None, :HEAD_DIM], q_a_g=misc[None, HEAD_DIM:HEAD_DIM + Q_LORA],
        kv_a_g=misc[None, HEAD_DIM + Q_LORA:HEAD_DIM + Q_LORA + KV_LORA], final_norm_g=sm[6],
        extra=misc[HEAD_DIM + Q_LORA + KV_LORA],
    )


def _weight_shard_buffer(w_in, w_uq, w_ukv, w_pa, w_pb, w_out):
    w_in_pad = jnp.pad(w_in.reshape(D_MODEL, -1), ((0, 0), (0, PACK_COLS - IN_COLS // N_DEV)))
    parts = [w_in_pad] + [a.reshape(-1, PACK_COLS) for a in (w_pa, w_pb, w_out, w_uq, w_ukv)]
    return jnp.concatenate(parts + [jnp.zeros((ROWS_AG - ROWS_AG_USED, PACK_COLS), F32)], axis=0)


def _full_weights(gathered):
    w_in_p = _assemble_w_in(gathered)
    r0 = D_MODEL
    mats = []
    for _ in range(3):
        mats.append(gathered[:, r0:r0 + ROWS_W_PROJ].reshape(D_MODEL, D_MODEL))
        r0 += ROWS_W_PROJ
    w_uq = gathered[:, r0:r0 + ROWS_W_UQ].reshape(N_DEV, Q_LORA, QK_DIM).transpose(1, 0, 2)
    w_uq_p = jnp.concatenate([w_uq, jnp.zeros((Q_LORA, HEADS, HEAD_PAD - QK_DIM), w_uq.dtype)], axis=2)
    w_uq_p = w_uq_p.reshape(Q_LORA, HEADS * HEAD_PAD)
    r0 += ROWS_W_UQ
    w_ukv = gathered[:, r0:r0 + ROWS_W_UKV].reshape(N_DEV, KV_LORA, 2 * HEAD_DIM).transpose(1, 0, 2)
    w_kn = w_ukv[:, :, :HEAD_DIM].reshape(KV_LORA, D_MODEL)
    w_v = w_ukv[:, :, HEAD_DIM:].reshape(KV_LORA, D_MODEL)
    return w_in_p, w_uq_p, w_kn, w_v, mats[0], mats[1], mats[2]


def _misc_slabs(dw_uq_p, dw_kn, dw_v, small):
    dw_uq = dw_uq_p.reshape(Q_LORA, HEADS, HEAD_PAD)[:, :, :QK_DIM].transpose(1, 0, 2)
    dw_ukv = jnp.concatenate([dw_kn.reshape(KV_LORA, HEADS, HEAD_DIM),
                              dw_v.reshape(KV_LORA, HEADS, HEAD_DIM)], axis=2).transpose(1, 0, 2)
    return jnp.stack([_pack_misc(dw_uq[j], dw_ukv[j], *small) for j in range(N_DEV)])


def _step_gradients(x, target, norm_g, b_gate, lb_logits, hg_norm_g, q_a_g, kv_a_g, final_g,
                    w_in_p, w_uq_p, w_kn, w_v, w_pa, w_pb, w_out):
    s = x.shape[0]
    cos, sin_a, sin_b = _rope_tables(s)
    proj, h = _inproj(x, norm_g, w_in_p)
    o_all, ya, states = _hgrn_fwd(proj, lb_logits, hg_norm_g)
    q_all, k_all, v_all, cqn, ckvn = _mla_prep(proj, q_a_g, kv_a_g, w_uq_p, w_kn, w_v, cos, sin_a, sin_b)
    ao, lse, yb = _flash_fwd(q_all, k_all, v_all, proj)
    (dx2, dya, dyb, dg0, dg1, merged_b, dpa_b, dpb_b, dx2_b,
     loss_acc, dfg_acc, dbg_acc) = _merge_fwd_bwd(x, target, ya, yb, proj, b_gate, final_g, w_pa, w_pb, w_out)
    dao, dmz = _attn_gate_bwd(dyb, ao, proj)
    dhq, dhf, dhi, dhz, dlb, dhg_acc = _hgrn_bwd(proj, lb_logits, hg_norm_g, o_all, dya, states)

    early = {0: dhq, 1: dhf, 2: dhi, 3: dhz, MZ_SEG: dmz, GL_SEG: dg0, GL_SEG + 1: dg1}
    dw_early = {k: _matmul_tn(h, sg, "dw_in_%d" % k) for k, sg in early.items()}
    mats = [_matmul_tn(a, b, name, TRANSPORT_DTYPE).reshape(N_DEV, ROWS_W_PROJ, PACK_COLS)
            for a, b, name in ((ya, dpa_b, "dw_pa"), (yb, dpb_b, "dw_pb"), (merged_b, dx2_b, "dw_out"))]
    early_slabs = [_scatter_dw_in(dw_early, list(range(N_DEV)), "scatter_dw_in")] + mats
    dq_all, dk_all, dv_all, (recv_in, recv_pa, recv_pb, recv_out) = _flash_bwd(
        q_all, k_all, v_all, dao, ao, lse, early_slabs)

    dsmall, dqf_b, dkn_b, dv_b, dgq_acc, dgk_acc = _mla_prep_bwd(
        dq_all, dk_all, dv_all, proj, q_a_g, kv_a_g, w_uq_p, w_kn, w_v, cos, sin_a, sin_b)
    late_slab = _scatter_dw_in({SMALL_SEG: _matmul_tn(h, dsmall, "dw_in_%d" % SMALL_SEG)}, [LATE_DEV],
                               "scatter_dw_in_late")
    segs = [dhq, dhf, dhi, dhz, dsmall, dmz, dg0, dg1]
    return dict(
        segs=segs, h=h, dx2=dx2, late_slab=late_slab,
        dw_uq_p=_matmul_tn(cqn, dqf_b, "dw_uq"), dw_kn=_matmul_tn(ckvn, dkn_b, "dw_kn"),
        dw_v=_matmul_tn(ckvn, dv_b, "dw_v"),
        small=dict(b_gate=dbg_acc[0:1], lb_logits=dlb, hg_norm_g=dhg_acc[0:1], q_a_g=dgq_acc[0:1],
                   kv_a_g=dgk_acc[0:1], final_norm_g=dfg_acc[0], loss=loss_acc[0, 0]),
        recv=dict(w_in=[recv_in], w_pa=[recv_pa], w_pb=[recv_pb], w_out=[recv_out]),
    )


def kernel(x, norm_g, w_in, b_gate, lb_logits, hg_norm_g, q_a_g, w_uq, kv_a_g, w_ukv, w_proj_a, w_proj_b, w_out, final_norm_g, loss_target, m_norm_g, m_w_in, m_b_gate, m_lb_logits, m_hg_norm_g, m_q_a_g, m_w_uq, m_kv_a_g, m_w_ukv, m_w_proj_a, m_w_proj_b, m_w_out, m_final_norm_g, v_norm_g, v_w_in, v_b_gate, v_lb_logits, v_hg_norm_g, v_q_a_g, v_w_uq, v_kv_a_g, v_w_ukv, v_w_proj_a, v_w_proj_b, v_w_out, v_final_norm_g):
    zero = jnp.zeros((1,), F32)
    xs = x[0]
    shard = _weight_shard_buffer(w_in, w_uq, w_ukv, w_proj_a, w_proj_b, w_out).astype(MXU_DTYPE)
    full = _full_weights(_all_gather_packed(shard))
    g = _step_gradients(xs, loss_target[0], norm_g, b_gate, lb_logits, hg_norm_g, q_a_g, kv_a_g,
                        final_norm_g.reshape(1, -1), *full)
    sm = g["small"]
    misc_slabs = _misc_slabs(g["dw_uq_p"], g["dw_kn"], g["dw_v"],
                             (jnp.zeros_like(norm_g), sm["b_gate"], sm["lb_logits"], sm["hg_norm_g"], sm["q_a_g"],
                              sm["kv_a_g"], sm["final_norm_g"], sm["loss"]))
    late_recv_init = jnp.zeros((N_DEV, D_MODEL, PACK_COLS), TRANSPORT_DTYPE)
    grad_x, dng_acc, recv_late, recv_misc = _dh_bwd(g["segs"], full[0], xs, g["dx2"], norm_g,
                                                    g["late_slab"], late_recv_init, misc_slabs)
    recv_ng = _exchange_rows(jnp.broadcast_to(dng_acc[None], (N_DEV, 8, D_MODEL)))

    recv = g["recv"]
    out_in = _sum_adamw(recv["w_in"] + [recv_late], w_in[0], m_w_in[0], v_w_in[0], "adamw_w_in")
    out_pa = _sum_adamw(recv["w_pa"], w_proj_a[0], m_w_proj_a[0], v_w_proj_a[0], "adamw_w_pa")
    out_pb = _sum_adamw(recv["w_pb"], w_proj_b[0], m_w_proj_b[0], v_w_proj_b[0], "adamw_w_pb")
    out_out = _sum_adamw(recv["w_out"], w_out[0], m_w_out[0], v_w_out[0], "adamw_w_out")
    out_ng = _sum_adamw([recv_ng], *[jnp.broadcast_to(a, (8, D_MODEL)) for a in (norm_g, m_norm_g, v_norm_g)],
                        "adamw_norm_g")
    out_misc = _sum_adamw(
        [recv_misc],
        _pack_misc(w_uq, w_ukv, norm_g, b_gate, lb_logits, hg_norm_g, q_a_g, kv_a_g, final_norm_g, zero),
        _pack_misc(m_w_uq, m_w_ukv, m_norm_g, m_b_gate, m_lb_logits, m_hg_norm_g, m_q_a_g, m_kv_a_g,
                   m_final_norm_g, zero),
        _pack_misc(v_w_uq, v_w_ukv, v_norm_g, v_b_gate, v_lb_logits, v_hg_norm_g, v_q_a_g, v_kv_a_g,
                   v_final_norm_g, zero),
        "adamw_misc")
    names = ["norm_g", "w_in", "b_gate", "lb_logits", "hg_norm_g", "q_a_g", "w_uq", "kv_a_g", "w_ukv",
             "w_proj_a", "w_proj_b", "w_out", "final_norm_g"]
    kinds = []
    for i in range(4):
        d = _unpack_misc(out_misc[i])
        d.update(w_in=out_in[i][None], w_proj_a=out_pa[i][None], w_proj_b=out_pb[i][None], w_out=out_out[i][None],
                 norm_g=out_ng[i][0:1])
        kinds.append(d)
    return (kinds[0]["extra"], grad_x[None], *[d[n] for d in kinds for n in names])
```

```python
import functools

import jax
import jax.numpy as jnp
from jax import lax
from jax.experimental import pallas as pl
from jax.experimental.pallas import tpu as pltpu

D_MODEL = 1024
HEADS = 8
HEAD_DIM = 128
HG_CHUNK = 32
Q_LORA = 384
KV_LORA = 256
QK_ROPE = 64
QK_DIM = 192
ROPE_THETA = 10000.0
EPS = 1e-6
IN_COLS = 7872
ADAM_LR = 0.001
ADAM_B1 = 0.9
ADAM_B2 = 0.999
ADAM_EPS = 1e-08
ADAM_WD = 0.01
ADAM_STEP = 10

N_DEV = 8
SEG = 1024
PROJ_W = 8 * SEG
SMALL_SEG = 4
MZ_SEG = 5
GL_SEG = 6
HEAD_PAD = 256
PACK_COLS = 1024
ROWS_W_UQ = 72
ROWS_W_UKV = 64
ROWS_W_PROJ = 128
ROWS_OTHER_USED = 3 * ROWS_W_PROJ + ROWS_W_UQ + ROWS_W_UKV
ROWS_OTHER = 528
LATE_DEV = (SMALL_SEG * SEG) // (IN_COLS // N_DEV)
assert (SMALL_SEG * SEG + Q_LORA + KV_LORA + QK_ROPE - 1) // (IN_COLS // N_DEV) == LATE_DEV

QK_SCALE = QK_DIM ** -0.5
LOG2E = 1.4426950408889634
LN2 = 0.6931471805599453
Q_PRESCALE = QK_SCALE * LOG2E

MXU_DTYPE = jnp.bfloat16
TRANSPORT_DTYPE = jnp.bfloat16
VMEM_LIMIT = 48 * 1024 * 1024
VMEM_LIMIT_BIG = 60 * 1024 * 1024

T_HGRN = 256
HG_HEADS_PER_STEP = 4
TM_ROW = 256
T_ATT = 1024
T_ATT_BWD = 1024
ATT_SUB = 4
ATT_SUB_BWD = 2
TS_TN = 2048
TR_ADAM = 256

F32 = jnp.float32
MESH = pl.DeviceIdType.MESH


def _dot(a, b):
    return jnp.dot(a, b, preferred_element_type=F32)


def _dot_nt(a, b):
    return lax.dot_general(a, b, (((1,), (1,)), ((), ())), preferred_element_type=F32)


def _dot_tn(a, b):
    return lax.dot_general(a, b, (((0,), (0,)), ((), ())), preferred_element_type=F32)


def _mx(a):
    return a.astype(MXU_DTYPE)


def _sigmoid(x):
    return 1.0 / (1.0 + jnp.exp(-x))


def _params(vmem=VMEM_LIMIT, **kw):
    return pltpu.CompilerParams(vmem_limit_bytes=vmem, **kw)


def _bcast_rows(row, n):
    return jnp.broadcast_to(row, (n, row.shape[-1]))


def _resident(shape):
    return pl.BlockSpec(shape, lambda *_: (0, 0), pipeline_mode=pl.Buffered(1))


HBM_SPEC = pl.BlockSpec(memory_space=pltpu.HBM)


def _all_gather_packed(shard):
    rows, cols = shard.shape

    def body(x_ref, out_ref, send_sems, recv_sems, local_sem):
        x, y, c = lax.axis_index("x"), lax.axis_index("y"), lax.axis_index("c")
        me, sibling = (x, y, c), (x, y, 1 - c)
        chips = [(1 - x, y), (x, 1 - y), (1 - x, 1 - y)]

        def slot(px, py, pc):
            return out_ref.at[4 * px + 2 * py + pc]

        def copy(k, block, to, src=None):
            return pltpu.make_async_remote_copy(
                src_ref=slot(*block) if src is None else src,
                dst_ref=slot(*block),
                send_sem=send_sems.at[k],
                recv_sem=recv_sems.at[k],
                device_id=to,
                device_id_type=MESH,
            )

        mine = pltpu.make_async_copy(x_ref, slot(*me), local_sem)
        mine.start()
        first = [copy(0, me, sibling, src=x_ref)]
        first += [copy(1 + j, me, (*chip, c), src=x_ref) for j, chip in enumerate(chips)]
        for cp in first:
            cp.start()
        passed = [copy(4 + j, (*chip, c), sibling) for j, chip in enumerate(chips)]
        for j, chip in enumerate(chips):
            copy(1 + j, (*chip, c), me).wait_recv()
            passed[j].start()
        copy(0, sibling, me).wait_recv()
        for j, chip in enumerate(chips):
            copy(4 + j, (*chip, 1 - c), me).wait_recv()
        for cp in first + passed:
            cp.wait_send()
        mine.wait()

    return pl.pallas_call(
        body,
        name="ag_weights",
        out_shape=jax.ShapeDtypeStruct((N_DEV, rows, cols), shard.dtype),
        in_specs=[HBM_SPEC],
        out_specs=HBM_SPEC,
        scratch_shapes=[
            pltpu.SemaphoreType.DMA((7,)),
            pltpu.SemaphoreType.DMA((7,)),
            pltpu.SemaphoreType.DMA,
        ],
    )(shard)


class _Exchange:
    def __init__(self, g_refs, recv_refs, send_sems, recv_sems, local_sems, gather=False):
        x, y, c = lax.axis_index("x"), lax.axis_index("y"), lax.axis_index("c")
        me = 4 * x + 2 * y + c
        n_ops = len(g_refs)

        def source(i, dest):
            return g_refs[i] if gather else g_refs[i].at[dest]

        def copy(i, k, landing):
            px, py, pc = x ^ ((k >> 2) & 1), y ^ ((k >> 1) & 1), c ^ (k & 1)
            peer = 4 * px + 2 * py + pc
            return pltpu.make_async_remote_copy(
                src_ref=source(i, peer),
                dst_ref=recv_refs[i].at[peer if landing else me],
                send_sem=send_sems.at[i * (N_DEV - 1) + k - 1],
                recv_sem=recv_sems.at[i * (N_DEV - 1) + k - 1],
                device_id=(px, py, pc),
                device_id_type=MESH,
            )

        pairs = [(i, k) for i in range(n_ops) for k in range(1, N_DEV)]
        self.mine = lambda: [pltpu.make_async_copy(source(i, me), recv_refs[i].at[me], local_sems.at[i])
                             for i in range(n_ops)]
        self.sends = lambda: [copy(i, k, False) for i, k in pairs]
        self.landings = lambda: [copy(i, k, True) for i, k in pairs]

    def start(self):
        for cp in self.mine() + self.sends():
            cp.start()

    def wait(self):
        for cp in self.landings():
            cp.wait_recv()
        for cp in self.sends():
            cp.wait_send()
        for cp in self.mine():
            cp.wait()

    @staticmethod
    def semaphores(n_ops):
        return [pltpu.SemaphoreType.DMA((n_ops * (N_DEV - 1),)),
                pltpu.SemaphoreType.DMA((n_ops * (N_DEV - 1),)),
                pltpu.SemaphoreType.DMA((n_ops,))]


def _exchange_rows(slabs):
    def body(g_ref, recv_ref, send_sems, recv_sems, local_sems):
        exchange = _Exchange([g_ref], [recv_ref], send_sems, recv_sems, local_sems)
        exchange.start()
        exchange.wait()

    return pl.pallas_call(
        body,
        name="exchange_rows",
        out_shape=jax.ShapeDtypeStruct(slabs.shape, slabs.dtype),
        in_specs=[HBM_SPEC],
        out_specs=HBM_SPEC,
        scratch_shapes=_Exchange.semaphores(1),
    )(slabs)


def _sum_adamw(recvs, w, m, v, name):
    rows, cols = w.shape
    tr = min(rows, TR_ADAM)
    n_recv = len(recvs)

    def body(*refs):
        w_ref, m_ref, v_ref, g_out, d_out, m_out, v_out = refs[n_recv:]
        g = None
        for r_ref in refs[:n_recv]:
            for i in range(N_DEV):
                part = r_ref[i].astype(F32)
                g = part if g is None else g + part
        g = g[:, :cols]
        m_new = ADAM_B1 * m_ref[...] + (1.0 - ADAM_B1) * g
        v_new = ADAM_B2 * v_ref[...] + (1.0 - ADAM_B2) * (g * g)
        m_hat = m_new / (1.0 - ADAM_B1 ** ADAM_STEP)
        v_hat = v_new / (1.0 - ADAM_B2 ** ADAM_STEP)
        g_out[...] = g
        d_out[...] = -ADAM_LR * (m_hat / (jnp.sqrt(v_hat) + ADAM_EPS) + ADAM_WD * w_ref[...])
        m_out[...] = m_new
        v_out[...] = v_new

    row_spec = pl.BlockSpec((tr, cols), lambda i: (i, 0))
    shape = jax.ShapeDtypeStruct((rows, cols), F32)
    return pl.pallas_call(
        body,
        name=name,
        grid=(rows // tr,),
        in_specs=[pl.BlockSpec((N_DEV, tr, PACK_COLS), lambda i: (0, i, 0))] * n_recv + [row_spec] * 3,
        out_specs=[row_spec] * 4,
        out_shape=[shape] * 4,
        compiler_params=_params(),
    )(*recvs, w, m, v)


def _inproj(x, norm_g, w_in_p, other_shard):
    s = x.shape[0]
    tm = min(s, TM_ROW)
    nsteps = s // tm

    def body(x_ref, g_ref, w_ref, shard_ref, proj_ref, h_ref, gathered_ref, send_sems, recv_sems, local_sems):
        i = pl.program_id(0)

        def all_gather():
            return _Exchange([shard_ref], [gathered_ref], send_sems, recv_sems, local_sems, gather=True)

        @pl.when(i == 0)
        def _():
            all_gather().start()

        xf = x_ref[...]
        r = lax.rsqrt(jnp.mean(xf * xf, axis=-1, keepdims=True) + EPS)
        h = _mx(xf * r * g_ref[...])
        h_ref[...] = h
        for j in range(PROJ_W // SEG):
            cols = slice(j * SEG, (j + 1) * SEG)
            proj_ref[:, cols] = _dot(h, w_ref[:, cols])

        @pl.when(i == nsteps - 1)
        def _():
            all_gather().wait()

    return pl.pallas_call(
        body,
        name="inproj",
        grid=(nsteps,),
        in_specs=[
            pl.BlockSpec((tm, D_MODEL), lambda i: (i, 0)),
            pl.BlockSpec((1, D_MODEL), lambda i: (0, 0)),
            _resident((D_MODEL, PROJ_W)),
            HBM_SPEC,
        ],
        out_specs=[
            pl.BlockSpec((tm, PROJ_W), lambda i: (i, 0)),
            pl.BlockSpec((tm, D_MODEL), lambda i: (i, 0)),
            HBM_SPEC,
        ],
        out_shape=[
            jax.ShapeDtypeStruct((s, PROJ_W), F32),
            jax.ShapeDtypeStruct((s, D_MODEL), MXU_DTYPE),
            jax.ShapeDtypeStruct((N_DEV,) + other_shard.shape, other_shard.dtype),
        ],
        scratch_shapes=_Exchange.semaphores(1),
        compiler_params=_params(),
    )(x, norm_g, w_in_p, other_shard)


def _chunk_lower_mask(t):
    row = lax.broadcasted_iota(jnp.int32, (t, t), 0)
    col = lax.broadcasted_iota(jnp.int32, (t, t), 1)
    return ((row // HG_CHUNK) == (col // HG_CHUNK)) & (col <= row)


def _chunk_pos(t):
    return lax.broadcasted_iota(jnp.int32, (t, HEAD_DIM), 0) & (HG_CHUNK - 1)


def _cumsum_chunk(x, pos):
    sh = 1
    while sh < HG_CHUNK:
        x = x + jnp.where(pos >= sh, pltpu.roll(x, sh, 0), 0.0)
        sh *= 2
    return x


def _rcumsum_chunk(x, pos):
    t = x.shape[0]
    sh = 1
    while sh < HG_CHUNK:
        x = x + jnp.where(pos < HG_CHUNK - sh, pltpu.roll(x, t - sh, 0), 0.0)
        sh *= 2
    return x


def _chunk_total(x):
    t, w = x.shape
    tot = jnp.sum(x.reshape(t // HG_CHUNK, HG_CHUNK, w), axis=1, keepdims=True)
    return jnp.broadcast_to(tot, (t // HG_CHUNK, HG_CHUNK, w)).reshape(t, w)


def _hgrn_gates(hq, hf, lb_logits, pos):
    lb = _sigmoid(lb_logits[0:1, :] - lb_logits[1:2, :])
    sig = _sigmoid(hf)
    f = lb + (1.0 - lb) * sig
    sq = _sigmoid(hq)
    q = hq * sq
    k = 1.0 - f
    logf = jnp.log(f)
    bcum = _cumsum_chunk(logf, pos)
    blast = _chunk_total(logf)
    eb = jnp.exp(bcum)
    enb = jnp.exp(-bcum)
    eo = jnp.exp(blast - bcum)
    return dict(lb=lb, sig=sig, f=f, sq=sq, q=q, k=k, eb=eb, enb=enb, eo=eo,
                qi=q * eb, ki=k * enb, ko=k * eo, dec=jnp.exp(blast))


def _hgrn_fwd(proj, lb_logits, hg_norm_g):
    s = proj.shape[0]
    t = min(s, T_HGRN)
    nb = s // t
    nc = t // HG_CHUNK
    hw = HG_HEADS_PER_STEP * HEAD_DIM

    def body(hq_ref, hf_ref, hi_ref, hz_ref, lb_ref, g_ref, o_ref, ya_ref, st_ref, state, u_sc, stb_sc):
        b = pl.program_id(1)

        @pl.when(b == 0)
        def _():
            state[...] = jnp.zeros_like(state)

        lower = _chunk_lower_mask(t)
        pos = _chunk_pos(t)
        for hh in range(HG_HEADS_PER_STEP):
            cols = slice(hh * HEAD_DIM, (hh + 1) * HEAD_DIM)
            st = state[hh]
            st_ref[0, hh] = st
            gt = _hgrn_gates(hq_ref[:, cols], hf_ref[:, cols], lb_ref[:, cols], pos)
            vb = _mx(hi_ref[:, cols])
            qib, kib, kob = _mx(gt["qi"]), _mx(gt["ki"]), _mx(gt["ko"])
            a = jnp.where(lower, _dot_nt(qib, kib), 0.0)
            o_intra = _dot(_mx(a), vb)
            for c in range(nc):
                sl = slice(c * HG_CHUNK, (c + 1) * HG_CHUNK)
                u_sc[hh, c] = _dot_tn(vb[sl], kob[sl])
            for c in range(nc):
                stb_sc[hh, c] = _mx(st)
                st = st * gt["dec"][c * HG_CHUNK:c * HG_CHUNK + 1, :] + u_sc[hh, c]
            state[hh] = st
            outs = []
            for c in range(nc):
                sl = slice(c * HG_CHUNK, (c + 1) * HG_CHUNK)
                outs.append(o_intra[sl] + _dot_nt(qib[sl], stb_sc[hh, c]))
            o = jnp.concatenate(outs, axis=0)
            o_ref[:, cols] = o
            r = lax.rsqrt(jnp.mean(o * o, axis=-1, keepdims=True) + EPS)
            hz = hz_ref[:, cols]
            ya_ref[:, cols] = _mx((o * r * g_ref[...]) * (hz * _sigmoid(hz)))

    hsteps = HEADS // HG_HEADS_PER_STEP

    def seg(k):
        return pl.BlockSpec((t, hw), lambda h, b, k=k: (b, k * hsteps + h))

    return pl.pallas_call(
        body,
        name="hgrn_fwd",
        grid=(hsteps, nb),
        in_specs=[seg(0), seg(1), seg(2), seg(3),
                  pl.BlockSpec((2, hw), lambda h, b: (0, h)),
                  pl.BlockSpec((1, HEAD_DIM), lambda h, b: (0, 0))],
        out_specs=[
            pl.BlockSpec((t, hw), lambda h, b: (b, h)),
            pl.BlockSpec((t, hw), lambda h, b: (b, h)),
            pl.BlockSpec((1, HG_HEADS_PER_STEP, HEAD_DIM, HEAD_DIM), lambda h, b: (b, h, 0, 0)),
        ],
        out_shape=[
            jax.ShapeDtypeStruct((s, D_MODEL), F32),
            jax.ShapeDtypeStruct((s, D_MODEL), MXU_DTYPE),
            jax.ShapeDtypeStruct((nb, HEADS, HEAD_DIM, HEAD_DIM), F32),
        ],
        scratch_shapes=[pltpu.VMEM((HG_HEADS_PER_STEP, HEAD_DIM, HEAD_DIM), F32),
                        pltpu.VMEM((HG_HEADS_PER_STEP, nc, HEAD_DIM, HEAD_DIM), F32),
                        pltpu.VMEM((HG_HEADS_PER_STEP, nc, HEAD_DIM, HEAD_DIM), MXU_DTYPE)],
        compiler_params=_params(),
    )(proj, proj, proj, proj, lb_logits, hg_norm_g)


def _rope(x, cos, sin_a, sin_b):
    return x * cos + pltpu.roll(x, 96, 1) * sin_a + pltpu.roll(x, 32, 1) * sin_b


def _rope_t(d, cos, sin_a, sin_b):
    return d * cos + pltpu.roll(d * sin_a, 32, 1) + pltpu.roll(d * sin_b, 96, 1)


def _mla_prep(proj, q_a_g, kv_a_g, w_uq_p, w_kn, w_v, cos, sin_a, sin_b):
    s = proj.shape[0]
    tm = min(s, TM_ROW)

    def body(sm_ref, gq_ref, gk_ref, wq_ref, wkn_ref, wv_ref, cos_ref, sa_ref, sb_ref,
             q_ref, k_ref, v_ref, cqn_ref, ckvn_ref):
        small = sm_ref[...]
        cq = small[:, :Q_LORA]
        ckv = small[:, Q_LORA:Q_LORA + KV_LORA]
        krp = small[:, Q_LORA + KV_LORA:Q_LORA + KV_LORA + HEAD_DIM]
        rq = lax.rsqrt(jnp.mean(cq * cq, axis=-1, keepdims=True) + EPS)
        rk = lax.rsqrt(jnp.mean(ckv * ckv, axis=-1, keepdims=True) + EPS)
        cqn = _mx(cq * rq * gq_ref[...])
        ckvn = _mx(ckv * rk * gk_ref[...])
        cqn_ref[...] = cqn
        ckvn_ref[...] = ckvn
        q = _dot(cqn, wq_ref[...]) * Q_PRESCALE
        kn = _dot(ckvn, wkn_ref[...])
        v = _dot(ckvn, wv_ref[...])
        cos_t, sa, sb = cos_ref[...], sa_ref[...], sb_ref[...]
        kpe = _mx(_rope(krp, cos_t, sa, sb))
        ones_col = (lax.broadcasted_iota(jnp.int32, (tm, HEAD_DIM), 1) == 0).astype(MXU_DTYPE)
        for h in range(HEADS):
            lo = h * HEAD_PAD
            v_ref[:, lo:lo + HEAD_DIM] = _mx(v[:, h * HEAD_DIM:(h + 1) * HEAD_DIM])
            v_ref[:, lo + HEAD_DIM:lo + HEAD_PAD] = ones_col
            q_ref[:, lo:lo + HEAD_DIM] = _mx(q[:, lo:lo + HEAD_DIM])
            q_ref[:, lo + HEAD_DIM:lo + HEAD_PAD] = _mx(_rope(q[:, lo + HEAD_DIM:lo + HEAD_PAD], cos_t, sa, sb))
            k_ref[:, lo:lo + HEAD_DIM] = _mx(kn[:, h * HEAD_DIM:(h + 1) * HEAD_DIM])
            k_ref[:, lo + HEAD_DIM:lo + HEAD_PAD] = kpe

    def const(shape):
        return pl.BlockSpec(shape, lambda i: (0, 0))

    def rows(w):
        return pl.BlockSpec((tm, w), lambda i: (i, 0))

    return pl.pallas_call(
        body,
        name="mla_prep",
        grid=(s // tm,),
        in_specs=[
            pl.BlockSpec((tm, SEG), lambda i: (i, SMALL_SEG)),
            const((1, Q_LORA)), const((1, KV_LORA)),
            const((Q_LORA, HEADS * HEAD_PAD)), const((KV_LORA, D_MODEL)), const((KV_LORA, D_MODEL)),
            rows(HEAD_DIM), rows(HEAD_DIM), rows(HEAD_DIM),
        ],
        out_specs=[rows(HEADS * HEAD_PAD)] * 3 + [rows(Q_LORA), rows(KV_LORA)],
        out_shape=[
            jax.ShapeDtypeStruct((s, HEADS * HEAD_PAD), MXU_DTYPE),
            jax.ShapeDtypeStruct((s, HEADS * HEAD_PAD), MXU_DTYPE),
            jax.ShapeDtypeStruct((s, HEADS * HEAD_PAD), MXU_DTYPE),
            jax.ShapeDtypeStruct((s, Q_LORA), MXU_DTYPE),
            jax.ShapeDtypeStruct((s, KV_LORA), MXU_DTYPE),
        ],
        compiler_params=_params(),
    )(proj, q_a_g, kv_a_g, w_uq_p, w_kn, w_v, cos, sin_a, sin_b)


def _diag_mask(t):
    row = lax.broadcasted_iota(jnp.int32, (t, t), 0)
    col = lax.broadcasted_iota(jnp.int32, (t, t), 1)
    return row >= col


def _flash_fwd(q_all, k_all, v_all, proj):
    s = q_all.shape[0]
    t = min(s, T_ATT)
    n = s // t

    ts = t // ATT_SUB

    def body(q_ref, k_ref, v_ref, mz_ref, ao_ref, lse_ref, yb_ref, m_sc, acc_sc):
        qi = pl.program_id(1)
        m_sc[...] = jnp.full_like(m_sc, -jnp.inf)
        acc_sc[...] = jnp.zeros_like(acc_sc)

        def key_block(base, diagonal):
            sc, pb, alpha = {}, {}, {}

            def width(r):
                return (r + 1) * ts if diagonal else t

            def scores(r):
                w = width(r)
                s_r = _dot_nt(q_ref[r * ts:(r + 1) * ts], k_ref[pl.ds(base, w), :])
                if diagonal:
                    row = lax.broadcasted_iota(jnp.int32, (ts, w), 0) + r * ts
                    col = lax.broadcasted_iota(jnp.int32, (ts, w), 1)
                    s_r = jnp.where(row >= col, s_r, -jnp.inf)
                sc[r] = s_r

            def softmax(r):
                rs = slice(r * ts, (r + 1) * ts)
                m_prev = m_sc[rs]
                m_new = jnp.maximum(m_prev, jnp.max(sc[r], axis=-1, keepdims=True))
                pb[r] = _mx(jnp.exp2(sc[r] - m_new))
                alpha[r] = jnp.exp2(m_prev - m_new)
                m_sc[rs] = m_new

            def weighted_values(r):
                rs = slice(r * ts, (r + 1) * ts)
                acc_sc[rs] = alpha[r] * acc_sc[rs] + _dot(pb[r], v_ref[pl.ds(base, width(r)), :])

            for step in range(ATT_SUB + 2):
                if step < ATT_SUB:
                    scores(step)
                if 1 <= step <= ATT_SUB:
                    softmax(step - 1)
                if step >= 2:
                    weighted_values(step - 2)

        def below_diagonal(ki, carry):
            key_block(pl.multiple_of(ki * t, t), False)
            return carry

        lax.fori_loop(0, qi, below_diagonal, 0)
        key_block(pl.multiple_of(qi * t, t), True)

        acc = acc_sc[...]
        l = acc[:, HEAD_DIM:HEAD_DIM + 1]
        ao = acc[:, :HEAD_DIM] / l
        ao_ref[...] = ao
        lse_ref[...] = jnp.broadcast_to(m_sc[...] + jnp.log2(l), (t, HEAD_DIM))
        mz = mz_ref[...]
        yb_ref[...] = _mx(ao * (mz * _sigmoid(mz)))

    q_map = lambda h, qi: (qi, h)
    return pl.pallas_call(
        body,
        name="flash_fwd",
        grid=(HEADS, n),
        in_specs=[
            pl.BlockSpec((t, HEAD_PAD), q_map),
            pl.BlockSpec((s, HEAD_PAD), lambda h, qi: (0, h)),
            pl.BlockSpec((s, HEAD_PAD), lambda h, qi: (0, h)),
            pl.BlockSpec((t, HEAD_DIM), lambda h, qi: (qi, MZ_SEG * HEADS + h)),
        ],
        out_specs=[pl.BlockSpec((t, HEAD_DIM), q_map)] * 3,
        out_shape=[
            jax.ShapeDtypeStruct((s, D_MODEL), F32),
            jax.ShapeDtypeStruct((s, D_MODEL), F32),
            jax.ShapeDtypeStruct((s, D_MODEL), MXU_DTYPE),
        ],
        scratch_shapes=[
            pltpu.VMEM((t, 1), F32),
            pltpu.VMEM((t, HEAD_PAD), F32),
        ],
        compiler_params=_params(),
    )(q_all, k_all, v_all, proj)


def _merge_fwd_bwd(x, target, ya, yb, ao, proj, b_gate, final_g, w_pa, w_pb, w_out):
    s = x.shape[0]
    tm = min(s, TM_ROW)

    def body(x_ref, t_ref, ya_ref, yb_ref, ao_ref, mz_ref, g0_ref, g1_ref, bg_ref, fg_ref, wpa_ref, wpb_ref, wo_ref,
             dx2_ref, dya_ref, dao_ref, dmz_ref, dg0_ref, dg1_ref, mb_ref, dpab_ref, dpbb_ref, dx2b_ref,
             loss_ref, dfg_ref, dbg_ref):
        i = pl.program_id(0)

        @pl.when(i == 0)
        def _():
            loss_ref[...] = jnp.zeros_like(loss_ref)
            dfg_ref[...] = jnp.zeros_like(dfg_ref)
            dbg_ref[...] = jnp.zeros_like(dbg_ref)

        pa = _dot(ya_ref[...], wpa_ref[...])
        pb = _dot(yb_ref[...], wpb_ref[...])
        bg = bg_ref[...]
        g0 = _sigmoid(g0_ref[...] + bg[:, :D_MODEL])
        g1 = _sigmoid(g1_ref[...] + bg[:, D_MODEL:])
        merged = g0 * pa + g1 * pb
        mb = _mx(merged)
        mb_ref[...] = mb
        x2 = x_ref[...] + _dot(mb, wo_ref[...])
        r = lax.rsqrt(jnp.mean(x2 * x2, axis=-1, keepdims=True) + EPS)
        xn = x2 * r
        fg = fg_ref[...]
        diff = xn * fg - t_ref[...]
        loss_ref[...] += 0.5 * jnp.sum(jnp.mean(diff * diff, axis=-1, keepdims=True))
        dy = diff * (1.0 / D_MODEL)
        dfg_ref[...] += _bcast_rows(jnp.sum(dy * xn, axis=0, keepdims=True), 8)
        tt = dy * fg
        dx2 = r * (tt - xn * jnp.mean(tt * xn, axis=-1, keepdims=True))
        dx2_ref[...] = dx2
        dx2b = _mx(dx2)
        dx2b_ref[...] = dx2b
        dmerged = _dot_nt(dx2b, wo_ref[...])
        dpa = _mx(dmerged * g0)
        dpb = _mx(dmerged * g1)
        dpab_ref[...] = dpa
        dpbb_ref[...] = dpb
        dg0 = dmerged * pa * (g0 * (1.0 - g0))
        dg1 = dmerged * pb * (g1 * (1.0 - g1))
        dg0_ref[...] = _mx(dg0)
        dg1_ref[...] = _mx(dg1)
        dbg_ref[:, :D_MODEL] += _bcast_rows(jnp.sum(dg0, axis=0, keepdims=True), 8)
        dbg_ref[:, D_MODEL:] += _bcast_rows(jnp.sum(dg1, axis=0, keepdims=True), 8)
        dya_ref[...] = _dot_nt(dpa, wpa_ref[...])
        dyb = _dot_nt(dpb, wpb_ref[...])
        mz = mz_ref[...]
        sg = _sigmoid(mz)
        dao_ref[...] = _mx(dyb * (mz * sg))
        dmz_ref[...] = _mx(dyb * ao_ref[...] * (sg + mz * sg * (1.0 - sg)))

    def rows(w=D_MODEL):
        return pl.BlockSpec((tm, w), lambda i: (i, 0))

    def const(shape):
        return pl.BlockSpec(shape, lambda i: (0, 0))

    def seg(k):
        return pl.BlockSpec((tm, SEG), lambda i: (i, k))

    f32 = jax.ShapeDtypeStruct((s, D_MODEL), F32)
    b16 = jax.ShapeDtypeStruct((s, D_MODEL), MXU_DTYPE)
    return pl.pallas_call(
        body,
        name="merge_fwd_bwd",
        grid=(s // tm,),
        in_specs=[
            rows(), rows(), rows(), rows(), rows(),
            seg(MZ_SEG), seg(GL_SEG), seg(GL_SEG + 1),
            const((1, 2 * D_MODEL)), const((1, D_MODEL)),
            _resident((D_MODEL, D_MODEL)), _resident((D_MODEL, D_MODEL)), _resident((D_MODEL, D_MODEL)),
        ],
        out_specs=[rows()] * 10 + [const((8, HEAD_DIM)), const((8, D_MODEL)), const((8, 2 * D_MODEL))],
        out_shape=[f32, f32, b16, b16, b16, b16, b16, b16, b16, b16,
                   jax.ShapeDtypeStruct((8, HEAD_DIM), F32),
                   jax.ShapeDtypeStruct((8, D_MODEL), F32),
                   jax.ShapeDtypeStruct((8, 2 * D_MODEL), F32)],
        compiler_params=_params(),
    )(x, target, ya, yb, ao, proj, proj, proj, b_gate, final_g, w_pa, w_pb, w_out)


def _flash_bwd(q_all, k_all, v_all, dao, ao, lse, slab_sets):
    s = q_all.shape[0]
    t = min(s, T_ATT_BWD)
    n = s // t
    pairs = [(ki, qi) for ki in range(n) for qi in range(ki, n)]
    ki_list = jnp.asarray([p[0] for p in pairs], jnp.int32)
    qi_list = jnp.asarray([p[1] for p in pairs], jnp.int32)
    n_ops = len(slab_sets)

    def body(ki_ref, qi_ref, q_ref, k_ref, v_ref, do_ref, ao_ref, lse_ref, *rest):
        g_refs = rest[:n_ops]
        dq_ref, dk_ref, dv_ref = rest[n_ops:n_ops + 3]
        recv_refs = rest[n_ops + 3:2 * n_ops + 3]
        dk_acc, dv_acc, send_sems, recv_sems, local_sems = rest[2 * n_ops + 3:]
        head, step = pl.program_id(0), pl.program_id(1)
        ki, qi = ki_ref[step], qi_ref[step]

        @pl.when((head == 0) & (step == 0))
        def _():
            _Exchange(g_refs, recv_refs, send_sems, recv_sems, local_sems).start()

        @pl.when(qi == ki)
        def _():
            dk_acc[...] = jnp.zeros_like(dk_acc)
            dv_acc[...] = jnp.zeros_like(dv_acc)

        @pl.when(ki == 0)
        def _():
            dq_ref[pl.ds(pl.multiple_of(qi * t, t), t), :] = jnp.zeros((t, HEAD_PAD), F32)

        def pair(masked):
            nsub = ATT_SUB if masked else ATT_SUB_BWD
            ts = t // nsub
            dk_parts, dv_parts = [], []
            for r in range(nsub):
                rs = slice(r * ts, (r + 1) * ts)
                w = (r + 1) * ts if masked else t
                k = k_ref[:w]
                v = v_ref[:w, :HEAD_DIM]
                q = q_ref[rs]
                sc = _dot_nt(q, k)
                if masked:
                    row = lax.broadcasted_iota(jnp.int32, (ts, w), 0) + r * ts
                    col = lax.broadcasted_iota(jnp.int32, (ts, w), 1)
                    sc = jnp.where(row >= col, sc, -jnp.inf)
                p = jnp.exp2(sc - lse_ref[rs, 0:1])
                do = do_ref[rs]
                delta = jnp.sum(do.astype(F32) * ao_ref[rs], axis=-1, keepdims=True)
                dv_part = _dot_tn(_mx(p), do)
                ds = _mx(p * (_dot_nt(do, v) - delta))
                dk_part = _dot_tn(ds, q)
                rows = pl.ds(pl.multiple_of(qi * t + r * ts, ts), ts)
                dq_ref[rows, :] += _dot(ds, k)
                if masked:
                    dk_acc[:w] += dk_part
                    dv_acc[:w] += dv_part
                else:
                    dk_parts.append(dk_part)
                    dv_parts.append(dv_part)

            if not masked:
                dk_acc[...] += sum(dk_parts[1:], dk_parts[0])
                dv_acc[...] += sum(dv_parts[1:], dv_parts[0])

        @pl.when(qi == ki)
        def _():
            pair(True)

        @pl.when(qi > ki)
        def _():
            pair(False)

        @pl.when(qi == n - 1)
        def _():
            dk_ref[...] = dk_acc[...] * LN2
            dv_ref[...] = dv_acc[...]

        @pl.when((head == HEADS - 1) & (step == len(pairs) - 1))
        def _():
            _Exchange(g_refs, recv_refs, send_sems, recv_sems, local_sems).wait()

    q_map = lambda h, p, ki_ref, qi_ref: (qi_ref[p], h)
    kv_map = lambda h, p, ki_ref, qi_ref: (ki_ref[p], h)
    grid_spec = pltpu.PrefetchScalarGridSpec(
        num_scalar_prefetch=2,
        grid=(HEADS, len(pairs)),
        in_specs=[
            pl.BlockSpec((t, HEAD_PAD), q_map),
            pl.BlockSpec((t, HEAD_PAD), kv_map),
            pl.BlockSpec((t, HEAD_PAD), kv_map),
            pl.BlockSpec((t, HEAD_DIM), q_map),
            pl.BlockSpec((t, HEAD_DIM), q_map),
            pl.BlockSpec((t, HEAD_DIM), q_map),
        ] + [HBM_SPEC] * n_ops,
        out_specs=[
            pl.BlockSpec((s, HEAD_PAD), lambda h, p, ki_ref, qi_ref: (0, h)),
            pl.BlockSpec((t, HEAD_PAD), kv_map),
            pl.BlockSpec((t, HEAD_DIM), kv_map),
        ] + [HBM_SPEC] * n_ops,
        scratch_shapes=[pltpu.VMEM((t, HEAD_PAD), F32), pltpu.VMEM((t, HEAD_DIM), F32)]
        + _Exchange.semaphores(n_ops),
    )
    outs = pl.pallas_call(
        body,
        name="flash_bwd",
        grid_spec=grid_spec,
        out_shape=[
            jax.ShapeDtypeStruct((s, HEADS * HEAD_PAD), F32),
            jax.ShapeDtypeStruct((s, HEADS * HEAD_PAD), F32),
            jax.ShapeDtypeStruct((s, D_MODEL), F32),
        ] + [jax.ShapeDtypeStruct(a.shape, a.dtype) for a in slab_sets],
        compiler_params=_params(VMEM_LIMIT_BIG),
    )(ki_list, qi_list, q_all, k_all, v_all, dao, ao, lse, *slab_sets)
    return outs[0], outs[1], outs[2], outs[3:]


def _mla_prep_bwd(dq_all, dk_all, dv_all, proj, q_a_g, kv_a_g, w_uq_p, w_kn, w_v, cos, sin_a, sin_b):
    s = proj.shape[0]
    tm = min(s, TM_ROW)

    def body(dq_ref, dk_ref, dv_ref, sm_ref, gq_ref, gk_ref, wq_ref, wkn_ref, wv_ref, cos_ref, sa_ref, sb_ref,
             dsm_ref, dqf_ref, dkn_ref, dvb_ref, dgq_ref, dgk_ref):
        i = pl.program_id(0)

        @pl.when(i == 0)
        def _():
            dgq_ref[...] = jnp.zeros_like(dgq_ref)
            dgk_ref[...] = jnp.zeros_like(dgk_ref)

        cos_t, sa, sb = cos_ref[...], sa_ref[...], sb_ref[...]
        dkpe = jnp.zeros((tm, HEAD_DIM), F32)
        for h in range(HEADS):
            lo = h * HEAD_PAD
            dqf_ref[:, lo:lo + HEAD_DIM] = _mx(dq_ref[:, lo:lo + HEAD_DIM] * QK_SCALE)
            dqf_ref[:, lo + HEAD_DIM:lo + HEAD_PAD] = _mx(
                _rope_t(dq_ref[:, lo + HEAD_DIM:lo + HEAD_PAD] * QK_SCALE, cos_t, sa, sb))
            dkn_ref[:, h * HEAD_DIM:(h + 1) * HEAD_DIM] = _mx(dk_ref[:, lo:lo + HEAD_DIM])
            dkpe = dkpe + dk_ref[:, lo + HEAD_DIM:lo + HEAD_PAD]
        dkr = _rope_t(dkpe, cos_t, sa, sb)
        dvb = _mx(dv_ref[...])
        dvb_ref[...] = dvb
        dcqn = _dot_nt(dqf_ref[...], wq_ref[...])
        dckvn = _dot_nt(dkn_ref[...], wkn_ref[...]) + _dot_nt(dvb, wv_ref[...])

        small = sm_ref[...]
        cq = small[:, :Q_LORA]
        ckv = small[:, Q_LORA:Q_LORA + KV_LORA]
        rq = lax.rsqrt(jnp.mean(cq * cq, axis=-1, keepdims=True) + EPS)
        rk = lax.rsqrt(jnp.mean(ckv * ckv, axis=-1, keepdims=True) + EPS)
        cqh = cq * rq
        ckh = ckv * rk
        dgq_ref[...] += _bcast_rows(jnp.sum(dcqn * cqh, axis=0, keepdims=True), 8)
        dgk_ref[...] += _bcast_rows(jnp.sum(dckvn * ckh, axis=0, keepdims=True), 8)
        tq = dcqn * gq_ref[...]
        tk = dckvn * gk_ref[...]
        dcq = rq * (tq - cqh * jnp.mean(tq * cqh, axis=-1, keepdims=True))
        dckv = rk * (tk - ckh * jnp.mean(tk * ckh, axis=-1, keepdims=True))
        dsm_ref[:, :Q_LORA] = _mx(dcq)
        dsm_ref[:, Q_LORA:Q_LORA + KV_LORA] = _mx(dckv)
        dsm_ref[:, Q_LORA + KV_LORA:Q_LORA + KV_LORA + HEAD_DIM] = _mx(dkr)
        dsm_ref[:, Q_LORA + KV_LORA + HEAD_DIM:] = jnp.zeros((tm, SEG - Q_LORA - KV_LORA - HEAD_DIM), MXU_DTYPE)

    def const(shape):
        return pl.BlockSpec(shape, lambda i: (0, 0))

    def rows(w):
        return pl.BlockSpec((tm, w), lambda i: (i, 0))

    return pl.pallas_call(
        body,
        name="mla_prep_bwd",
        grid=(s // tm,),
        in_specs=[
            rows(HEADS * HEAD_PAD), rows(HEADS * HEAD_PAD), rows(D_MODEL),
            pl.BlockSpec((tm, SEG), lambda i: (i, SMALL_SEG)),
            const((1, Q_LORA)), const((1, KV_LORA)),
            const((Q_LORA, HEADS * HEAD_PAD)), const((KV_LORA, D_MODEL)), const((KV_LORA, D_MODEL)),
            rows(HEAD_DIM), rows(HEAD_DIM), rows(HEAD_DIM),
        ],
        out_specs=[rows(SEG), rows(HEADS * HEAD_PAD), rows(D_MODEL), rows(D_MODEL),
                   const((8, Q_LORA)), const((8, KV_LORA))],
        out_shape=[
            jax.ShapeDtypeStruct((s, SEG), MXU_DTYPE),
            jax.ShapeDtypeStruct((s, HEADS * HEAD_PAD), MXU_DTYPE),
            jax.ShapeDtypeStruct((s, D_MODEL), MXU_DTYPE),
            jax.ShapeDtypeStruct((s, D_MODEL), MXU_DTYPE),
            jax.ShapeDtypeStruct((8, Q_LORA), F32),
            jax.ShapeDtypeStruct((8, KV_LORA), F32),
        ],
        compiler_params=_params(),
    )(dq_all, dk_all, dv_all, proj, q_a_g, kv_a_g, w_uq_p, w_kn, w_v, cos, sin_a, sin_b)


def _hgrn_bwd(proj, lb_logits, hg_norm_g, o_all, dya, states):
    s = proj.shape[0]
    t = min(s, T_HGRN)
    nb = s // t
    nc = t // HG_CHUNK

    def body(hq_ref, hf_ref, hi_ref, hz_ref, lb_ref, g_ref, o_ref, dya_ref, st_ref,
             dhq_ref, dhf_ref, dhi_ref, dhz_ref, dlb_ref, dg_ref, dstate, u_sc, g_sc, stf_sc, stb_sc, dstb_sc):
        h, b = pl.program_id(0), pl.program_id(1)

        @pl.when(b == 0)
        def _():
            dstate[...] = jnp.zeros_like(dstate)
            dlb_ref[...] = jnp.zeros_like(dlb_ref)

        @pl.when((b == 0) & (h == 0))
        def _():
            dg_ref[...] = jnp.zeros_like(dg_ref)

        lower = _chunk_lower_mask(t)
        pos = _chunk_pos(t)
        ghg = g_ref[...]
        for hh in range(HG_HEADS_PER_STEP):
            cols = slice(hh * HEAD_DIM, (hh + 1) * HEAD_DIM)
            hq, hf, hz = hq_ref[:, cols], hf_ref[:, cols], hz_ref[:, cols]
            gt = _hgrn_gates(hq, hf, lb_ref[:, cols], pos)
            vb = _mx(hi_ref[:, cols])
            qi, ki, ko = gt["qi"], gt["ki"], gt["ko"]
            qib, kib, kob = _mx(qi), _mx(ki), _mx(ko)

            o = o_ref[:, cols]
            sz = _sigmoid(hz)
            r = lax.rsqrt(jnp.mean(o * o, axis=-1, keepdims=True) + EPS)
            on = o * r
            dya_t = dya_ref[:, cols]
            don = dya_t * (hz * sz)
            dhz_ref[:, cols] = _mx(dya_t * (on * ghg) * (sz + hz * sz * (1.0 - sz)))
            dg_ref[...] += _bcast_rows(jnp.sum(don * on, axis=0, keepdims=True), 8)
            tt = don * ghg
            do = r * (tt - on * jnp.mean(tt * on, axis=-1, keepdims=True))
            dob = _mx(do)

            for c in range(nc):
                sl = slice(c * HG_CHUNK, (c + 1) * HG_CHUNK)
                u_sc[hh, c] = _dot_tn(vb[sl], kob[sl])
                g_sc[hh, c] = _dot_tn(dob[sl], qib[sl])

            st = st_ref[0, hh]
            for c in range(nc):
                stf_sc[hh, c] = st
                stb_sc[hh, c] = _mx(st)
                if c < nc - 1:
                    st = st * gt["dec"][c * HG_CHUNK:c * HG_CHUNK + 1, :] + u_sc[hh, c]

            dst = dstate[hh]
            dd_parts = [None] * nc
            for c in reversed(range(nc)):
                dec = gt["dec"][c * HG_CHUNK:c * HG_CHUNK + 1, :]
                dstb_sc[hh, c] = _mx(dst)
                dd_parts[c] = _bcast_rows(jnp.sum(dst * stf_sc[hh, c], axis=0, keepdims=True) * dec, HG_CHUNK)
                dst = dst * dec + g_sc[hh, c]
            dstate[hh] = dst

            a = jnp.where(lower, _dot_nt(qib, kib), 0.0)
            da = _mx(jnp.where(lower, _dot_nt(dob, vb), 0.0))
            dqi_intra = _dot(da, kib)
            dki = _dot_tn(da, qib)
            dv_intra = _dot_tn(_mx(a), dob)

            dqi_parts, dko_parts, dv_parts = [None] * nc, [None] * nc, [None] * nc
            for c in range(nc):
                sl = slice(c * HG_CHUNK, (c + 1) * HG_CHUNK)
                dv_parts[c] = dv_intra[sl] + _dot_nt(kob[sl], dstb_sc[hh, c])
                dko_parts[c] = _dot(vb[sl], dstb_sc[hh, c])
                dqi_parts[c] = dqi_intra[sl] + _dot(dob[sl], stb_sc[hh, c])
            dqi = jnp.concatenate(dqi_parts, axis=0)
            dko = jnp.concatenate(dko_parts, axis=0)
            dv = jnp.concatenate(dv_parts, axis=0)
            dd = jnp.concatenate(dd_parts, axis=0)

            dq = dqi * gt["eb"]
            dk = dki * gt["enb"] + dko * gt["eo"]
            db = dqi * qi - dki * ki - dko * ko
            dlogf = _rcumsum_chunk(db, pos) + _chunk_total(dko * ko) + dd
            df = dlogf / gt["f"] - dk
            lb, sig, sq = gt["lb"], gt["sig"], gt["sq"]
            dhf_ref[:, cols] = _mx(df * (1.0 - lb) * (sig * (1.0 - sig)))
            dhq_ref[:, cols] = _mx(dq * (sq + hq * sq * (1.0 - sq)))
            dhi_ref[:, cols] = _mx(dv)
            dlb = jnp.sum(df * (1.0 - sig), axis=0, keepdims=True) * (lb * (1.0 - lb))
            dlb_ref[:, cols] += jnp.concatenate([dlb, -dlb], axis=0)

    hw = HG_HEADS_PER_STEP * HEAD_DIM
    hsteps = HEADS // HG_HEADS_PER_STEP

    def seg(k):
        return pl.BlockSpec((t, hw), lambda h, b, k=k: (nb - 1 - b, k * hsteps + h))

    blk = pl.BlockSpec((t, hw), lambda h, b: (nb - 1 - b, h))
    b16 = jax.ShapeDtypeStruct((s, D_MODEL), MXU_DTYPE)
    return pl.pallas_call(
        body,
        name="hgrn_bwd",
        grid=(hsteps, nb),
        in_specs=[seg(0), seg(1), seg(2), seg(3),
                  pl.BlockSpec((2, hw), lambda h, b: (0, h)),
                  pl.BlockSpec((1, HEAD_DIM), lambda h, b: (0, 0)),
                  blk, blk,
                  pl.BlockSpec((1, HG_HEADS_PER_STEP, HEAD_DIM, HEAD_DIM), lambda h, b: (nb - 1 - b, h, 0, 0))],
        out_specs=[blk, blk, blk, blk,
                   pl.BlockSpec((2, hw), lambda h, b: (0, h)),
                   pl.BlockSpec((8, HEAD_DIM), lambda h, b: (0, 0))],
        out_shape=[b16, b16, b16, b16,
                   jax.ShapeDtypeStruct((2, D_MODEL), F32),
                   jax.ShapeDtypeStruct((8, HEAD_DIM), F32)],
        scratch_shapes=[pltpu.VMEM((HG_HEADS_PER_STEP, HEAD_DIM, HEAD_DIM), F32)]
        + [pltpu.VMEM((HG_HEADS_PER_STEP, nc, HEAD_DIM, HEAD_DIM), F32)] * 3
        + [pltpu.VMEM((HG_HEADS_PER_STEP, nc, HEAD_DIM, HEAD_DIM), MXU_DTYPE)] * 2,
        compiler_params=_params(),
    )(proj, proj, proj, proj, lb_logits, hg_norm_g, o_all, dya, states)


def _dh_bwd(segs, w_in_p, x, dx2, norm_g, late_slab, late_recv_init, misc_slabs):
    s = x.shape[0]
    tm = min(s, TM_ROW)
    nseg = len(segs)
    nsteps = s // tm
    late_xyc = ((LATE_DEV >> 2) & 1, (LATE_DEV >> 1) & 1, LATE_DEV & 1)

    def body(*refs):
        seg_refs = refs[:nseg]
        (w_ref, x_ref, dx2_ref, g_ref, late_ref, _, misc_ref,
         gx_ref, dng_ref, late_recv_ref, misc_recv_ref,
         dp_buf, send_sems, recv_sems, local_sems, late_send, late_recvs, late_local) = refs[nseg:]
        i = pl.program_id(0)
        me = 4 * lax.axis_index("x") + 2 * lax.axis_index("y") + lax.axis_index("c")

        def misc_exchange():
            return _Exchange([misc_ref], [misc_recv_ref], send_sems, recv_sems, local_sems)

        def late_copy(sender):
            return pltpu.make_async_remote_copy(
                src_ref=late_ref.at[0], dst_ref=late_recv_ref.at[sender], send_sem=late_send,
                recv_sem=late_recvs.at[(sender ^ LATE_DEV) - 1], device_id=late_xyc, device_id_type=MESH)

        def late_own():
            return pltpu.make_async_copy(late_ref.at[0], late_recv_ref.at[LATE_DEV], late_local)

        @pl.when(i == 0)
        def _():
            dng_ref[...] = jnp.zeros_like(dng_ref)
            misc_exchange().start()

        @pl.when((i == 0) & (me != LATE_DEV))
        def _():
            late_copy(me).start()

        @pl.when((i == 0) & (me == LATE_DEV))
        def _():
            late_own().start()

        for k, sref in enumerate(seg_refs):
            dp_buf[:, k * SEG:(k + 1) * SEG] = sref[...]
        dh = _dot_nt(dp_buf[...], w_ref[...])
        xf = x_ref[...]
        r = lax.rsqrt(jnp.mean(xf * xf, axis=-1, keepdims=True) + EPS)
        xh = xf * r
        dng_ref[...] += _bcast_rows(jnp.sum(dh * xh, axis=0, keepdims=True), 8)
        tt = dh * g_ref[...]
        gx_ref[...] = dx2_ref[...] + r * (tt - xh * jnp.mean(tt * xh, axis=-1, keepdims=True))

        @pl.when(i == nsteps - 1)
        def _():
            misc_exchange().wait()

        @pl.when((i == nsteps - 1) & (me != LATE_DEV))
        def _():
            late_copy(me).wait_send()

        @pl.when((i == nsteps - 1) & (me == LATE_DEV))
        def _():
            for k in range(1, N_DEV):
                late_copy(LATE_DEV ^ k).wait_recv()
            late_own().wait()

    rows = pl.BlockSpec((tm, D_MODEL), lambda i: (i, 0))
    return pl.pallas_call(
        body,
        name="dh_bwd",
        grid=(nsteps,),
        in_specs=[pl.BlockSpec((tm, SEG), lambda i: (i, 0))] * nseg + [
            _resident((D_MODEL, PROJ_W)),
            rows, rows,
            pl.BlockSpec((1, D_MODEL), lambda i: (0, 0)),
            HBM_SPEC, HBM_SPEC, HBM_SPEC,
        ],
        out_specs=[rows, pl.BlockSpec((8, D_MODEL), lambda i: (0, 0)), HBM_SPEC, HBM_SPEC],
        out_shape=[jax.ShapeDtypeStruct((s, D_MODEL), F32), jax.ShapeDtypeStruct((8, D_MODEL), F32),
                   jax.ShapeDtypeStruct(late_recv_init.shape, late_recv_init.dtype),
                   jax.ShapeDtypeStruct(misc_slabs.shape, misc_slabs.dtype)],
        input_output_aliases={nseg + 5: 2},
        scratch_shapes=[pltpu.VMEM((tm, PROJ_W), MXU_DTYPE)] + _Exchange.semaphores(1)
        + [pltpu.SemaphoreType.DMA, pltpu.SemaphoreType.DMA((N_DEV - 1,)), pltpu.SemaphoreType.DMA],
        compiler_params=_params(),
    )(*segs, w_in_p, x, dx2, norm_g, late_slab, late_recv_init, misc_slabs)


def _matmul_tn(a, b, name, out_dtype=F32):
    s, m = a.shape
    n = b.shape[1]
    ts = min(s, TS_TN)
    tn = min(n, SEG)
    nk = s // ts

    def body(a_ref, b_ref, o_ref, acc):
        k = pl.program_id(1)
        part = _dot_tn(a_ref[...], b_ref[...])

        @pl.when(k == 0)
        def _():
            acc[...] = part

        @pl.when(k > 0)
        def _():
            acc[...] += part

        @pl.when(k == nk - 1)
        def _():
            o_ref[...] = acc[...].astype(out_dtype)

    return pl.pallas_call(
        body,
        name=name,
        grid=(n // tn, nk),
        in_specs=[pl.BlockSpec((ts, m), lambda j, k: (k, 0)), pl.BlockSpec((ts, tn), lambda j, k: (k, j))],
        out_specs=pl.BlockSpec((m, tn), lambda j, k: (0, j)),
        out_shape=jax.ShapeDtypeStruct((m, n), out_dtype),
        scratch_shapes=[pltpu.VMEM((m, tn), F32)],
        compiler_params=_params(),
    )(a, b)


def _w_in_pieces():
    per = IN_COLS // N_DEV
    pad_at = SMALL_SEG * SEG + Q_LORA + KV_LORA + QK_ROPE
    pieces = []
    for j in range(N_DEV):
        u0, u1 = j * per, (j + 1) * per
        cuts = [u0] + ([pad_at] if u0 < pad_at < u1 else []) + [u1]
        for a, b in zip(cuts[:-1], cuts[1:]):
            pieces.append((j, a - u0, b - u0, a if a < pad_at else a + PROJ_W - IN_COLS))
    return pad_at, pieces


def _assemble_w_in(gathered):
    tr = TM_ROW
    pad_at, pieces = _w_in_pieces()

    def body(in_ref, out_ref):
        out_ref[:, pad_at:pad_at + PROJ_W - IN_COLS] = jnp.zeros((tr, PROJ_W - IN_COLS), gathered.dtype)
        for j, a, b, p0 in pieces:
            out_ref[:, p0:p0 + b - a] = in_ref[j, :, a:b]

    return pl.pallas_call(
        body,
        name="assemble_w_in",
        grid=(D_MODEL // tr,),
        in_specs=[pl.BlockSpec((N_DEV, tr, PACK_COLS), lambda i: (0, i, 0))],
        out_specs=pl.BlockSpec((tr, PROJ_W), lambda i: (i, 0)),
        out_shape=jax.ShapeDtypeStruct((D_MODEL, PROJ_W), gathered.dtype),
        compiler_params=_params(),
    )(gathered)


def _scatter_dw_in(dw_segs, devs, name):
    tr = TM_ROW
    _, pieces = _w_in_pieces()
    per = IN_COLS // N_DEV
    seg_ids = sorted(dw_segs)
    nseg = len(seg_ids)

    def body(*refs):
        out_ref, buf = refs[nseg:]
        for k in range(PROJ_W // SEG):
            if k in seg_ids:
                buf[:, k * SEG:(k + 1) * SEG] = refs[seg_ids.index(k)][...]
            else:
                buf[:, k * SEG:(k + 1) * SEG] = jnp.zeros((tr, SEG), F32)
        for slot, dev in enumerate(devs):
            out_ref[slot, :, per:] = jnp.zeros((tr, PACK_COLS - per), TRANSPORT_DTYPE)
            for j, a, b, p0 in pieces:
                if j == dev:
                    out_ref[slot, :, a:b] = buf[:, p0:p0 + b - a].astype(TRANSPORT_DTYPE)

    return pl.pallas_call(
        body,
        name=name,
        grid=(D_MODEL // tr,),
        in_specs=[pl.BlockSpec((tr, SEG), lambda i: (i, 0))] * nseg,
        out_specs=pl.BlockSpec((len(devs), tr, PACK_COLS), lambda i: (0, i, 0)),
        out_shape=jax.ShapeDtypeStruct((len(devs), D_MODEL, PACK_COLS), TRANSPORT_DTYPE),
        scratch_shapes=[pltpu.VMEM((tr, PROJ_W), F32)],
        compiler_params=_params(),
    )(*[dw_segs[k] for k in seg_ids])


def _rope_tables(s):
    inv = ROPE_THETA ** (-jnp.arange(0, QK_ROPE, 2, dtype=F32) / QK_ROPE)
    ang = jnp.arange(s, dtype=F32)[:, None] * inv[None, :]
    cos, sin = jnp.cos(ang), jnp.sin(ang)
    z32 = jnp.zeros_like(cos)
    z64 = jnp.zeros((s, HEAD_DIM - QK_ROPE), F32)
    cos_t = jnp.concatenate([cos, cos, z64], axis=1)
    sin_a = jnp.concatenate([-sin, z32, z64], axis=1)
    sin_b = jnp.concatenate([z32, sin, z64], axis=1)
    return cos_t, sin_a, sin_b


def _pack_misc(w_uq, w_ukv, norm_g, b_gate, lb_logits, hg_norm_g, q_a_g, kv_a_g, final_norm_g, extra):
    misc = jnp.concatenate([hg_norm_g.reshape(-1), q_a_g.reshape(-1), kv_a_g.reshape(-1), extra.reshape(-1),
                            jnp.zeros((PACK_COLS - HEAD_DIM - Q_LORA - KV_LORA - 1,), F32)])
    return jnp.concatenate([w_uq.reshape(ROWS_W_UQ, PACK_COLS), w_ukv.reshape(ROWS_W_UKV, PACK_COLS),
                            norm_g.reshape(1, -1), b_gate.reshape(2, -1), lb_logits.reshape(2, -1),
                            misc.reshape(1, -1), final_norm_g.reshape(1, -1), jnp.zeros((1, PACK_COLS), F32)], axis=0)


def _unpack_misc(p):
    sm = p[ROWS_W_UQ + ROWS_W_UKV:]
    misc = sm[5]
    return dict(
        w_uq=p[:ROWS_W_UQ].reshape(1, Q_LORA, QK_DIM),
        w_ukv=p[ROWS_W_UQ:ROWS_W_UQ + ROWS_W_UKV].reshape(1, KV_LORA, 2 * HEAD_DIM),
        norm_g=sm[0:1], b_gate=sm[1:3].reshape(1, -1), lb_logits=sm[3:5],
        hg_norm_g=misc[None, :HEAD_DIM], q_a_g=misc[None, HEAD_DIM:HEAD_DIM + Q_LORA],
        kv_a_g=misc[None, HEAD_DIM + Q_LORA:HEAD_DIM + Q_LORA + KV_LORA], final_norm_g=sm[6],
        extra=misc[HEAD_DIM + Q_LORA + KV_LORA],
    )


def _weight_shard_buffers(w_in, w_uq, w_ukv, w_pa, w_pb, w_out):
    w_in_pad = jnp.pad(w_in.reshape(D_MODEL, -1), ((0, 0), (0, PACK_COLS - IN_COLS // N_DEV)))
    parts = [a.reshape(-1, PACK_COLS) for a in (w_pa, w_pb, w_out, w_uq, w_ukv)]
    others = jnp.concatenate(parts + [jnp.zeros((ROWS_OTHER - ROWS_OTHER_USED, PACK_COLS), F32)], axis=0)
    return w_in_pad.astype(MXU_DTYPE), others.astype(MXU_DTYPE)


def _other_weights(gathered):
    r0 = 0
    mats = []
    for _ in range(3):
        mats.append(gathered[:, r0:r0 + ROWS_W_PROJ].reshape(D_MODEL, D_MODEL))
        r0 += ROWS_W_PROJ
    w_uq = gathered[:, r0:r0 + ROWS_W_UQ].reshape(N_DEV, Q_LORA, QK_DIM).transpose(1, 0, 2)
    w_uq_p = jnp.concatenate([w_uq, jnp.zeros((Q_LORA, HEADS, HEAD_PAD - QK_DIM), w_uq.dtype)], axis=2)
    w_uq_p = w_uq_p.reshape(Q_LORA, HEADS * HEAD_PAD)
    r0 += ROWS_W_UQ
    w_ukv = gathered[:, r0:r0 + ROWS_W_UKV].reshape(N_DEV, KV_LORA, 2 * HEAD_DIM).transpose(1, 0, 2)
    w_kn = w_ukv[:, :, :HEAD_DIM].reshape(KV_LORA, D_MODEL)
    w_v = w_ukv[:, :, HEAD_DIM:].reshape(KV_LORA, D_MODEL)
    return w_uq_p, w_kn, w_v, mats[0], mats[1], mats[2]


def _misc_slabs(dw_uq_p, dw_kn, dw_v, small):
    dw_uq = dw_uq_p.reshape(Q_LORA, HEADS, HEAD_PAD)[:, :, :QK_DIM].transpose(1, 0, 2)
    dw_ukv = jnp.concatenate([dw_kn.reshape(KV_LORA, HEADS, HEAD_DIM),
                              dw_v.reshape(KV_LORA, HEADS, HEAD_DIM)], axis=2).transpose(1, 0, 2)
    return jnp.stack([_pack_misc(dw_uq[j], dw_ukv[j], *small) for j in range(N_DEV)])


def _step_gradients(x, target, norm_g, b_gate, lb_logits, hg_norm_g, q_a_g, kv_a_g, final_g,
                    w_in_p, other_shard):
    s = x.shape[0]
    cos, sin_a, sin_b = _rope_tables(s)
    proj, h, gathered = _inproj(x, norm_g, w_in_p, other_shard)
    w_uq_p, w_kn, w_v, w_pa, w_pb, w_out = _other_weights(gathered)
    o_all, ya, states = _hgrn_fwd(proj, lb_logits, hg_norm_g)
    q_all, k_all, v_all, cqn, ckvn = _mla_prep(proj, q_a_g, kv_a_g, w_uq_p, w_kn, w_v, cos, sin_a, sin_b)
    ao, lse, yb = _flash_fwd(q_all, k_all, v_all, proj)
    (dx2, dya, dao, dmz, dg0, dg1, merged_b, dpa_b, dpb_b, dx2_b,
     loss_acc, dfg_acc, dbg_acc) = _merge_fwd_bwd(x, target, ya, yb, ao, proj, b_gate, final_g, w_pa, w_pb, w_out)
    dhq, dhf, dhi, dhz, dlb, dhg_acc = _hgrn_bwd(proj, lb_logits, hg_norm_g, o_all, dya, states)

    early = {0: dhq, 1: dhf, 2: dhi, 3: dhz, MZ_SEG: dmz, GL_SEG: dg0, GL_SEG + 1: dg1}
    dw_early = {k: _matmul_tn(h, sg, "dw_in_%d" % k) for k, sg in early.items()}
    mats = [_matmul_tn(a, b, name, TRANSPORT_DTYPE).reshape(N_DEV, ROWS_W_PROJ, PACK_COLS)
            for a, b, name in ((ya, dpa_b, "dw_pa"), (yb, dpb_b, "dw_pb"), (merged_b, dx2_b, "dw_out"))]
    early_slabs = [_scatter_dw_in(dw_early, list(range(N_DEV)), "scatter_dw_in")] + mats
    dq_all, dk_all, dv_all, (recv_in, recv_pa, recv_pb, recv_out) = _flash_bwd(
        q_all, k_all, v_all, dao, ao, lse, early_slabs)

    dsmall, dqf_b, dkn_b, dv_b, dgq_acc, dgk_acc = _mla_prep_bwd(
        dq_all, dk_all, dv_all, proj, q_a_g, kv_a_g, w_uq_p, w_kn, w_v, cos, sin_a, sin_b)
    late_slab = _scatter_dw_in({SMALL_SEG: _matmul_tn(h, dsmall, "dw_in_%d" % SMALL_SEG)}, [LATE_DEV],
                               "scatter_dw_in_late")
    segs = [dhq, dhf, dhi, dhz, dsmall, dmz, dg0, dg1]
    return dict(
        segs=segs, h=h, dx2=dx2, late_slab=late_slab,
        dw_uq_p=_matmul_tn(cqn, dqf_b, "dw_uq"), dw_kn=_matmul_tn(ckvn, dkn_b, "dw_kn"),
        dw_v=_matmul_tn(ckvn, dv_b, "dw_v"),
        small=dict(b_gate=dbg_acc[0:1], lb_logits=dlb, hg_norm_g=dhg_acc[0:1], q_a_g=dgq_acc[0:1],
                   kv_a_g=dgk_acc[0:1], final_norm_g=dfg_acc[0], loss=loss_acc[0, 0]),
        recv=dict(w_in=[recv_in], w_pa=[recv_pa], w_pb=[recv_pb], w_out=[recv_out]),
    )


def kernel(x, norm_g, w_in, b_gate, lb_logits, hg_norm_g, q_a_g, w_uq, kv_a_g, w_ukv, w_proj_a, w_proj_b, w_out, final_norm_g, loss_target, m_norm_g, m_w_in, m_b_gate, m_lb_logits, m_hg_norm_g, m_q_a_g, m_w_uq, m_kv_a_g, m_w_ukv, m_w_proj_a, m_w_proj_b, m_w_out, m_final_norm_g, v_norm_g, v_w_in, v_b_gate, v_lb_logits, v_hg_norm_g, v_q_a_g, v_w_uq, v_kv_a_g, v_w_ukv, v_w_proj_a, v_w_proj_b, v_w_out, v_final_norm_g):
    zero = jnp.zeros((1,), F32)
    xs = x[0]
    w_in_shard, other_shard = _weight_shard_buffers(w_in, w_uq, w_ukv, w_proj_a, w_proj_b, w_out)
    w_in_p = _assemble_w_in(_all_gather_packed(w_in_shard))
    g = _step_gradients(xs, loss_target[0], norm_g, b_gate, lb_logits, hg_norm_g, q_a_g, kv_a_g,
                        final_norm_g.reshape(1, -1), w_in_p, other_shard)
    sm = g["small"]
    misc_slabs = _misc_slabs(g["dw_uq_p"], g["dw_kn"], g["dw_v"],
                             (jnp.zeros_like(norm_g), sm["b_gate"], sm["lb_logits"], sm["hg_norm_g"], sm["q_a_g"],
                              sm["kv_a_g"], sm["final_norm_g"], sm["loss"]))
    late_recv_init = jnp.zeros((N_DEV, D_MODEL, PACK_COLS), TRANSPORT_DTYPE)
    grad_x, dng_acc, recv_late, recv_misc = _dh_bwd(g["segs"], w_in_p, xs, g["dx2"], norm_g,
                                                    g["late_slab"], late_recv_init, misc_slabs)
    recv_ng = _exchange_rows(jnp.broadcast_to(dng_acc[None], (N_DEV, 8, D_MODEL)))

    recv = g["recv"]
    out_in = _sum_adamw(recv["w_in"] + [recv_late], w_in[0], m_w_in[0], v_w_in[0], "adamw_w_in")
    out_pa = _sum_adamw(recv["w_pa"], w_proj_a[0], m_w_proj_a[0], v_w_proj_a[0], "adamw_w_pa")
    out_pb = _sum_adamw(recv["w_pb"], w_proj_b[0], m_w_proj_b[0], v_w_proj_b[0], "adamw_w_pb")
    out_out = _sum_adamw(recv["w_out"], w_out[0], m_w_out[0], v_w_out[0], "adamw_w_out")
    out_ng = _sum_adamw([recv_ng], *[jnp.broadcast_to(a, (8, D_MODEL)) for a in (norm_g, m_norm_g, v_norm_g)],
                        "adamw_norm_g")
    out_misc = _sum_adamw(
        [recv_misc],
        _pack_misc(w_uq, w_ukv, norm_g, b_gate, lb_logits, hg_norm_g, q_a_g, kv_a_g, final_norm_g, zero),
        _pack_misc(m_w_uq, m_w_ukv, m_norm_g, m_b_gate, m_lb_logits, m_hg_norm_g, m_q_a_g, m_kv_a_g,
                   m_final_norm_g, zero),
        _pack_misc(v_w_uq, v_w_ukv, v_norm_g, v_b_gate, v_lb_logits, v_hg_norm_g, v_q_a_g, v_kv_a_g,
                   v_final_norm_g, zero),
        "adamw_misc")
    names = ["norm_g", "w_in", "b_gate", "lb_logits", "hg_norm_g", "q_a_g", "w_uq", "kv_a_g", "w_ukv",
             "w_proj_a", "w_proj_b", "w_out", "final_norm_g"]
    kinds = []
    for i in range(4):
        d = _unpack_misc(out_misc[i])
        d.update(w_in=out_in[i][None], w_proj_a=out_pa[i][None], w_proj_b=out_pb[i][None], w_out=out_out[i][None],
                 norm_g=out_ng[i][0:1])
        kinds.append(d)
    return (kinds[0]["extra"], grad_x[None], *[d[n] for d in kinds for n in names])
```

```python
import functools

import jax
import jax.numpy as jnp
from jax import lax
from jax.experimental import pallas as pl
from jax.experimental.pallas import tpu as pltpu

D_MODEL = 1024
HEADS = 8
HEAD_DIM = 128
HG_CHUNK = 32
Q_LORA = 384
KV_LORA = 256
QK_ROPE = 64
QK_DIM = 192
ROPE_THETA = 10000.0
EPS = 1e-6
IN_COLS = 7872
ADAM_LR = 0.001
ADAM_B1 = 0.9
ADAM_B2 = 0.999
ADAM_EPS = 1e-08
ADAM_WD = 0.01
ADAM_STEP = 10

N_DEV = 8
SEG = 1024
PROJ_W = 8 * SEG
SMALL_SEG = 4
MZ_SEG = 5
GL_SEG = 6
HEAD_PAD = 256
PACK_COLS = 1024
ROWS_W_UQ = 72
ROWS_W_UKV = 64
ROWS_W_PROJ = 128
ROWS_OTHER_USED = 3 * ROWS_W_PROJ + ROWS_W_UQ + ROWS_W_UKV
ROWS_OTHER = 528
LATE_DEV = (SMALL_SEG * SEG) // (IN_COLS // N_DEV)
assert (SMALL_SEG * SEG + Q_LORA + KV_LORA + QK_ROPE - 1) // (IN_COLS // N_DEV) == LATE_DEV

QK_SCALE = QK_DIM ** -0.5
LOG2E = 1.4426950408889634
LN2 = 0.6931471805599453
Q_PRESCALE = QK_SCALE * LOG2E

MXU_DTYPE = jnp.bfloat16
TRANSPORT_DTYPE = jnp.bfloat16
VMEM_LIMIT = 48 * 1024 * 1024
VMEM_LIMIT_BIG = 60 * 1024 * 1024

T_HGRN = 256
HG_HEADS_PER_STEP = 4
TM_ROW = 256
T_ATT = 1024
T_ATT_BWD = T_ATT
ATT_SUB = 4
ATT_SUB_BWD = 2
TS_TN = 2048
TR_ADAM = 256

F32 = jnp.float32
MESH = pl.DeviceIdType.MESH


def _dot(a, b):
    return jnp.dot(a, b, preferred_element_type=F32)


def _dot_nt(a, b):
    return lax.dot_general(a, b, (((1,), (1,)), ((), ())), preferred_element_type=F32)


def _dot_tn(a, b):
    return lax.dot_general(a, b, (((0,), (0,)), ((), ())), preferred_element_type=F32)


def _mx(a):
    return a.astype(MXU_DTYPE)


def _sigmoid(x):
    return 1.0 / (1.0 + jnp.exp(-x))


def _params(vmem=VMEM_LIMIT, **kw):
    return pltpu.CompilerParams(vmem_limit_bytes=vmem, **kw)


def _bcast_rows(row, n):
    return jnp.broadcast_to(row, (n, row.shape[-1]))


def _resident(shape):
    return pl.BlockSpec(shape, lambda *_: (0, 0), pipeline_mode=pl.Buffered(1))


HBM_SPEC = pl.BlockSpec(memory_space=pltpu.HBM)


def _all_gather_packed(shard):
    rows, cols = shard.shape

    def body(x_ref, out_ref, send_sems, recv_sems, local_sem):
        x, y, c = lax.axis_index("x"), lax.axis_index("y"), lax.axis_index("c")
        me, sibling = (x, y, c), (x, y, 1 - c)
        chips = [(1 - x, y), (x, 1 - y), (1 - x, 1 - y)]

        def slot(px, py, pc):
            return out_ref.at[4 * px + 2 * py + pc]

        def copy(k, block, to, src=None):
            return pltpu.make_async_remote_copy(
                src_ref=slot(*block) if src is None else src,
                dst_ref=slot(*block),
                send_sem=send_sems.at[k],
                recv_sem=recv_sems.at[k],
                device_id=to,
                device_id_type=MESH,
            )

        mine = pltpu.make_async_copy(x_ref, slot(*me), local_sem)
        mine.start()
        first = [copy(0, me, sibling, src=x_ref)]
        first += [copy(1 + j, me, (*chip, c), src=x_ref) for j, chip in enumerate(chips)]
        for cp in first:
            cp.start()
        passed = [copy(4 + j, (*chip, c), sibling) for j, chip in enumerate(chips)]
        for j, chip in enumerate(chips):
            copy(1 + j, (*chip, c), me).wait_recv()
            passed[j].start()
        copy(0, sibling, me).wait_recv()
        for j, chip in enumerate(chips):
            copy(4 + j, (*chip, 1 - c), me).wait_recv()
        for cp in first + passed:
            cp.wait_send()
        mine.wait()

    return pl.pallas_call(
        body,
        name="ag_weights",
        out_shape=jax.ShapeDtypeStruct((N_DEV, rows, cols), shard.dtype),
        in_specs=[HBM_SPEC],
        out_specs=HBM_SPEC,
        scratch_shapes=[
            pltpu.SemaphoreType.DMA((7,)),
            pltpu.SemaphoreType.DMA((7,)),
            pltpu.SemaphoreType.DMA,
        ],
    )(shard)


class _Exchange:
    def __init__(self, g_refs, recv_refs, send_sems, recv_sems, local_sems, gather=False):
        x, y, c = lax.axis_index("x"), lax.axis_index("y"), lax.axis_index("c")
        me = 4 * x + 2 * y + c
        n_ops = len(g_refs)

        def source(i, dest):
            return g_refs[i] if gather else g_refs[i].at[dest]

        def copy(i, k, landing):
            px, py, pc = x ^ ((k >> 2) & 1), y ^ ((k >> 1) & 1), c ^ (k & 1)
            peer = 4 * px + 2 * py + pc
            return pltpu.make_async_remote_copy(
                src_ref=source(i, peer),
                dst_ref=recv_refs[i].at[peer if landing else me],
                send_sem=send_sems.at[i * (N_DEV - 1) + k - 1],
                recv_sem=recv_sems.at[i * (N_DEV - 1) + k - 1],
                device_id=(px, py, pc),
                device_id_type=MESH,
            )

        pairs = [(i, k) for i in range(n_ops) for k in range(1, N_DEV)]
        self.mine = lambda: [pltpu.make_async_copy(source(i, me), recv_refs[i].at[me], local_sems.at[i])
                             for i in range(n_ops)]
        self.sends = lambda: [copy(i, k, False) for i, k in pairs]
        self.landings = lambda: [copy(i, k, True) for i, k in pairs]

    def start(self):
        for cp in self.mine() + self.sends():
            cp.start()

    def wait(self):
        for cp in self.landings():
            cp.wait_recv()
        for cp in self.sends():
            cp.wait_send()
        for cp in self.mine():
            cp.wait()

    @staticmethod
    def semaphores(n_ops):
        return [pltpu.SemaphoreType.DMA((n_ops * (N_DEV - 1),)),
                pltpu.SemaphoreType.DMA((n_ops * (N_DEV - 1),)),
                pltpu.SemaphoreType.DMA((n_ops,))]


def _exchange_rows(slabs):
    def body(g_ref, recv_ref, send_sems, recv_sems, local_sems):
        exchange = _Exchange([g_ref], [recv_ref], send_sems, recv_sems, local_sems)
        exchange.start()
        exchange.wait()

    return pl.pallas_call(
        body,
        name="exchange_rows",
        out_shape=jax.ShapeDtypeStruct(slabs.shape, slabs.dtype),
        in_specs=[HBM_SPEC],
        out_specs=HBM_SPEC,
        scratch_shapes=_Exchange.semaphores(1),
    )(slabs)


def _sum_adamw(recvs, w, m, v, name):
    rows, cols = w.shape
    tr = min(rows, TR_ADAM)
    n_recv = len(recvs)

    def body(*refs):
        w_ref, m_ref, v_ref, g_out, d_out, m_out, v_out = refs[n_recv:]
        g = None
        for r_ref in refs[:n_recv]:
            for i in range(N_DEV):
                part = r_ref[i].astype(F32)
                g = part if g is None else g + part
        g = g[:, :cols]
        m_new = ADAM_B1 * m_ref[...] + (1.0 - ADAM_B1) * g
        v_new = ADAM_B2 * v_ref[...] + (1.0 - ADAM_B2) * (g * g)
        m_hat = m_new / (1.0 - ADAM_B1 ** ADAM_STEP)
        v_hat = v_new / (1.0 - ADAM_B2 ** ADAM_STEP)
        g_out[...] = g
        d_out[...] = -ADAM_LR * (m_hat / (jnp.sqrt(v_hat) + ADAM_EPS) + ADAM_WD * w_ref[...])
        m_out[...] = m_new
        v_out[...] = v_new

    row_spec = pl.BlockSpec((tr, cols), lambda i: (i, 0))
    shape = jax.ShapeDtypeStruct((rows, cols), F32)
    return pl.pallas_call(
        body,
        name=name,
        grid=(rows // tr,),
        in_specs=[pl.BlockSpec((N_DEV, tr, PACK_COLS), lambda i: (0, i, 0))] * n_recv + [row_spec] * 3,
        out_specs=[row_spec] * 4,
        out_shape=[shape] * 4,
        compiler_params=_params(),
    )(*recvs, w, m, v)


def _inproj(x, norm_g, w_in_p, other_shard):
    s = x.shape[0]
    tm = min(s, TM_ROW)
    nsteps = s // tm

    def body(x_ref, g_ref, w_ref, shard_ref, proj_ref, h_ref, gathered_ref, send_sems, recv_sems, local_sems):
        i = pl.program_id(0)

        def all_gather():
            return _Exchange([shard_ref], [gathered_ref], send_sems, recv_sems, local_sems, gather=True)

        @pl.when(i == 0)
        def _():
            all_gather().start()

        xf = x_ref[...]
        r = lax.rsqrt(jnp.mean(xf * xf, axis=-1, keepdims=True) + EPS)
        h = _mx(xf * r * g_ref[...])
        h_ref[...] = h
        for j in range(PROJ_W // SEG):
            cols = slice(j * SEG, (j + 1) * SEG)
            proj_ref[:, cols] = _dot(h, w_ref[:, cols])

        @pl.when(i == nsteps - 1)
        def _():
            all_gather().wait()

    return pl.pallas_call(
        body,
        name="inproj",
        grid=(nsteps,),
        in_specs=[
            pl.BlockSpec((tm, D_MODEL), lambda i: (i, 0)),
            pl.BlockSpec((1, D_MODEL), lambda i: (0, 0)),
            _resident((D_MODEL, PROJ_W)),
            HBM_SPEC,
        ],
        out_specs=[
            pl.BlockSpec((tm, PROJ_W), lambda i: (i, 0)),
            pl.BlockSpec((tm, D_MODEL), lambda i: (i, 0)),
            HBM_SPEC,
        ],
        out_shape=[
            jax.ShapeDtypeStruct((s, PROJ_W), F32),
            jax.ShapeDtypeStruct((s, D_MODEL), MXU_DTYPE),
            jax.ShapeDtypeStruct((N_DEV,) + other_shard.shape, other_shard.dtype),
        ],
        scratch_shapes=_Exchange.semaphores(1),
        compiler_params=_params(),
    )(x, norm_g, w_in_p, other_shard)


def _chunk_lower_mask(t):
    row = lax.broadcasted_iota(jnp.int32, (t, t), 0)
    col = lax.broadcasted_iota(jnp.int32, (t, t), 1)
    return ((row // HG_CHUNK) == (col // HG_CHUNK)) & (col <= row)


def _chunk_pos(t):
    return lax.broadcasted_iota(jnp.int32, (t, HEAD_DIM), 0) & (HG_CHUNK - 1)


def _cumsum_chunk(x, pos):
    sh = 1
    while sh < HG_CHUNK:
        x = x + jnp.where(pos >= sh, pltpu.roll(x, sh, 0), 0.0)
        sh *= 2
    return x


def _rcumsum_chunk(x, pos):
    t = x.shape[0]
    sh = 1
    while sh < HG_CHUNK:
        x = x + jnp.where(pos < HG_CHUNK - sh, pltpu.roll(x, t - sh, 0), 0.0)
        sh *= 2
    return x


def _chunk_total(x):
    t, w = x.shape
    tot = jnp.sum(x.reshape(t // HG_CHUNK, HG_CHUNK, w), axis=1, keepdims=True)
    return jnp.broadcast_to(tot, (t // HG_CHUNK, HG_CHUNK, w)).reshape(t, w)


def _hgrn_gates(hq, hf, lb_logits, pos):
    lb = _sigmoid(lb_logits[0:1, :] - lb_logits[1:2, :])
    sig = _sigmoid(hf)
    f = lb + (1.0 - lb) * sig
    sq = _sigmoid(hq)
    q = hq * sq
    k = 1.0 - f
    logf = jnp.log(f)
    bcum = _cumsum_chunk(logf, pos)
    blast = _chunk_total(logf)
    eb = jnp.exp(bcum)
    enb = jnp.exp(-bcum)
    eo = jnp.exp(blast - bcum)
    return dict(lb=lb, sig=sig, f=f, sq=sq, q=q, k=k, eb=eb, enb=enb, eo=eo,
                qi=q * eb, ki=k * enb, ko=k * eo, dec=jnp.exp(blast))


def _hgrn_fwd(proj, lb_logits, hg_norm_g):
    s = proj.shape[0]
    t = min(s, T_HGRN)
    nb = s // t
    nc = t // HG_CHUNK
    hw = HG_HEADS_PER_STEP * HEAD_DIM

    def body(hq_ref, hf_ref, hi_ref, hz_ref, lb_ref, g_ref, o_ref, ya_ref, st_ref, state, u_sc, stb_sc):
        b = pl.program_id(1)

        @pl.when(b == 0)
        def _():
            state[...] = jnp.zeros_like(state)

        lower = _chunk_lower_mask(t)
        pos = _chunk_pos(t)
        for hh in range(HG_HEADS_PER_STEP):
            cols = slice(hh * HEAD_DIM, (hh + 1) * HEAD_DIM)
            st = state[hh]
            st_ref[0, hh] = st
            gt = _hgrn_gates(hq_ref[:, cols], hf_ref[:, cols], lb_ref[:, cols], pos)
            vb = _mx(hi_ref[:, cols])
            qib, kib, kob = _mx(gt["qi"]), _mx(gt["ki"]), _mx(gt["ko"])
            a = jnp.where(lower, _dot_nt(qib, kib), 0.0)
            o_intra = _dot(_mx(a), vb)
            for c in range(nc):
                sl = slice(c * HG_CHUNK, (c + 1) * HG_CHUNK)
                u_sc[hh, c] = _dot_tn(vb[sl], kob[sl])
            for c in range(nc):
                stb_sc[hh, c] = _mx(st)
                st = st * gt["dec"][c * HG_CHUNK:c * HG_CHUNK + 1, :] + u_sc[hh, c]
            state[hh] = st
            outs = []
            for c in range(nc):
                sl = slice(c * HG_CHUNK, (c + 1) * HG_CHUNK)
                outs.append(o_intra[sl] + _dot_nt(qib[sl], stb_sc[hh, c]))
            o = jnp.concatenate(outs, axis=0)
            o_ref[:, cols] = o
            r = lax.rsqrt(jnp.mean(o * o, axis=-1, keepdims=True) + EPS)
            hz = hz_ref[:, cols]
            ya_ref[:, cols] = _mx((o * r * g_ref[...]) * (hz * _sigmoid(hz)))

    hsteps = HEADS // HG_HEADS_PER_STEP

    def seg(k):
        return pl.BlockSpec((t, hw), lambda h, b, k=k: (b, k * hsteps + h))

    return pl.pallas_call(
        body,
        name="hgrn_fwd",
        grid=(hsteps, nb),
        in_specs=[seg(0), seg(1), seg(2), seg(3),
                  pl.BlockSpec((2, hw), lambda h, b: (0, h)),
                  pl.BlockSpec((1, HEAD_DIM), lambda h, b: (0, 0))],
        out_specs=[
            pl.BlockSpec((t, hw), lambda h, b: (b, h)),
            pl.BlockSpec((t, hw), lambda h, b: (b, h)),
            pl.BlockSpec((1, HG_HEADS_PER_STEP, HEAD_DIM, HEAD_DIM), lambda h, b: (b, h, 0, 0)),
        ],
        out_shape=[
            jax.ShapeDtypeStruct((s, D_MODEL), F32),
            jax.ShapeDtypeStruct((s, D_MODEL), MXU_DTYPE),
            jax.ShapeDtypeStruct((nb, HEADS, HEAD_DIM, HEAD_DIM), F32),
        ],
        scratch_shapes=[pltpu.VMEM((HG_HEADS_PER_STEP, HEAD_DIM, HEAD_DIM), F32),
                        pltpu.VMEM((HG_HEADS_PER_STEP, nc, HEAD_DIM, HEAD_DIM), F32),
                        pltpu.VMEM((HG_HEADS_PER_STEP, nc, HEAD_DIM, HEAD_DIM), MXU_DTYPE)],
        compiler_params=_params(),
    )(proj, proj, proj, proj, lb_logits, hg_norm_g)


def _rope(x, cos, sin_a, sin_b):
    return x * cos + pltpu.roll(x, 96, 1) * sin_a + pltpu.roll(x, 32, 1) * sin_b


def _rope_t(d, cos, sin_a, sin_b):
    return d * cos + pltpu.roll(d * sin_a, 32, 1) + pltpu.roll(d * sin_b, 96, 1)


def _mla_prep(proj, q_a_g, kv_a_g, w_uq_p, w_kn, w_v, cos, sin_a, sin_b):
    s = proj.shape[0]
    tm = min(s, TM_ROW)

    def body(sm_ref, gq_ref, gk_ref, wq_ref, wkn_ref, wv_ref, cos_ref, sa_ref, sb_ref,
             q_ref, k_ref, v_ref, cqn_ref, ckvn_ref):
        small = sm_ref[...]
        cq = small[:, :Q_LORA]
        ckv = small[:, Q_LORA:Q_LORA + KV_LORA]
        krp = small[:, Q_LORA + KV_LORA:Q_LORA + KV_LORA + HEAD_DIM]
        rq = lax.rsqrt(jnp.mean(cq * cq, axis=-1, keepdims=True) + EPS)
        rk = lax.rsqrt(jnp.mean(ckv * ckv, axis=-1, keepdims=True) + EPS)
        cqn = _mx(cq * rq * gq_ref[...])
        ckvn = _mx(ckv * rk * gk_ref[...])
        cqn_ref[...] = cqn
        ckvn_ref[...] = ckvn
        q = _dot(cqn, wq_ref[...]) * Q_PRESCALE
        kn = _dot(ckvn, wkn_ref[...])
        v = _dot(ckvn, wv_ref[...])
        cos_t, sa, sb = cos_ref[...], sa_ref[...], sb_ref[...]
        kpe = _mx(_rope(krp, cos_t, sa, sb))
        ones_col = (lax.broadcasted_iota(jnp.int32, (tm, HEAD_DIM), 1) == 0).astype(MXU_DTYPE)
        for h in range(HEADS):
            lo = h * HEAD_PAD
            v_ref[:, lo:lo + HEAD_DIM] = _mx(v[:, h * HEAD_DIM:(h + 1) * HEAD_DIM])
            v_ref[:, lo + HEAD_DIM:lo + HEAD_PAD] = ones_col
            q_ref[:, lo:lo + HEAD_DIM] = _mx(q[:, lo:lo + HEAD_DIM])
            q_ref[:, lo + HEAD_DIM:lo + HEAD_PAD] = _mx(_rope(q[:, lo + HEAD_DIM:lo + HEAD_PAD], cos_t, sa, sb))
            k_ref[:, lo:lo + HEAD_DIM] = _mx(kn[:, h * HEAD_DIM:(h + 1) * HEAD_DIM])
            k_ref[:, lo + HEAD_DIM:lo + HEAD_PAD] = kpe

    def const(shape):
        return pl.BlockSpec(shape, lambda i: (0, 0))

    def rows(w):
        return pl.BlockSpec((tm, w), lambda i: (i, 0))

    return pl.pallas_call(
        body,
        name="mla_prep",
        grid=(s // tm,),
        in_specs=[
            pl.BlockSpec((tm, SEG), lambda i: (i, SMALL_SEG)),
            const((1, Q_LORA)), const((1, KV_LORA)),
            const((Q_LORA, HEADS * HEAD_PAD)), const((KV_LORA, D_MODEL)), const((KV_LORA, D_MODEL)),
            rows(HEAD_DIM), rows(HEAD_DIM), rows(HEAD_DIM),
        ],
        out_specs=[rows(HEADS * HEAD_PAD)] * 3 + [rows(Q_LORA), rows(KV_LORA)],
        out_shape=[
            jax.ShapeDtypeStruct((s, HEADS * HEAD_PAD), MXU_DTYPE),
            jax.ShapeDtypeStruct((s, HEADS * HEAD_PAD), MXU_DTYPE),
            jax.ShapeDtypeStruct((s, HEADS * HEAD_PAD), MXU_DTYPE),
            jax.ShapeDtypeStruct((s, Q_LORA), MXU_DTYPE),
            jax.ShapeDtypeStruct((s, KV_LORA), MXU_DTYPE),
        ],
        compiler_params=_params(),
    )(proj, q_a_g, kv_a_g, w_uq_p, w_kn, w_v, cos, sin_a, sin_b)


def _diag_mask(t):
    row = lax.broadcasted_iota(jnp.int32, (t, t), 0)
    col = lax.broadcasted_iota(jnp.int32, (t, t), 1)
    return row >= col


def _flash_fwd(q_all, k_all, v_all, proj):
    s = q_all.shape[0]
    t = min(s, T_ATT)
    n = s // t
    ts = t // ATT_SUB
    n_pairs = n * (n + 1) // 2

    def body(q_ref, k_ref, v_ref, mz_ref, ao_ref, lse_ref, yb_ref, mblk_ref, p_hbm, m_sc, acc_sc, stage, p_sems):
        head, qi = pl.program_id(0), pl.program_id(1)
        m_sc[...] = jnp.full_like(m_sc, -jnp.inf)
        acc_sc[...] = jnp.zeros_like(acc_sc)
        mblk_ref[...] = jnp.zeros_like(mblk_ref)
        lane = lax.broadcasted_iota(jnp.int32, (ts, HEAD_DIM), 1)

        def p_copy(slot, pair):
            return pltpu.make_async_copy(stage.at[slot], p_hbm.at[head, pair], p_sems.at[slot])

        def key_block(ki, diagonal):
            base = pl.multiple_of(ki * t, t)
            slot = lax.rem(ki, 2)
            sc, pb, alpha = {}, {}, {}

            @pl.when(ki >= 2)
            def _():
                p_copy(slot, 0).wait()

            if diagonal:
                stage[slot] = jnp.zeros((t, t), MXU_DTYPE)

            def width(r):
                return (r + 1) * ts if diagonal else t

            def scores(r):
                w = width(r)
                s_r = _dot_nt(q_ref[r * ts:(r + 1) * ts], k_ref[pl.ds(base, w), :])
                if diagonal:
                    row = lax.broadcasted_iota(jnp.int32, (ts, w), 0) + r * ts
                    col = lax.broadcasted_iota(jnp.int32, (ts, w), 1)
                    s_r = jnp.where(row >= col, s_r, -jnp.inf)
                sc[r] = s_r

            def softmax(r):
                rs = slice(r * ts, (r + 1) * ts)
                m_prev = m_sc[rs]
                m_new = jnp.maximum(m_prev, jnp.max(sc[r], axis=-1, keepdims=True))
                pb[r] = _mx(jnp.exp2(sc[r] - m_new))
                alpha[r] = jnp.exp2(m_prev - m_new)
                m_sc[rs] = m_new
                mblk_ref[rs] = jnp.where(lane == ki, m_new, mblk_ref[rs])
                stage[slot, rs, :width(r)] = pb[r]

            def weighted_values(r):
                rs = slice(r * ts, (r + 1) * ts)
                acc_sc[rs] = alpha[r] * acc_sc[rs] + _dot(pb[r], v_ref[pl.ds(base, width(r)), :])

            for step in range(ATT_SUB + 2):
                if step < ATT_SUB:
                    scores(step)
                if 1 <= step <= ATT_SUB:
                    softmax(step - 1)
                if step >= 2:
                    weighted_values(step - 2)
            p_copy(slot, qi * (qi + 1) // 2 + ki).start()

        def below_diagonal(ki, carry):
            key_block(ki, False)
            return carry

        lax.fori_loop(0, qi, below_diagonal, 0)
        key_block(qi, True)

        @pl.when(qi >= 1)
        def _():
            p_copy(1 - lax.rem(qi, 2), 0).wait()

        p_copy(lax.rem(qi, 2), 0).wait()

        acc = acc_sc[...]
        l = acc[:, HEAD_DIM:HEAD_DIM + 1]
        ao = acc[:, :HEAD_DIM] / l
        ao_ref[...] = ao
        lse_ref[...] = jnp.broadcast_to(m_sc[...] + jnp.log2(l), (t, HEAD_DIM))
        mz = mz_ref[...]
        yb_ref[...] = _mx(ao * (mz * _sigmoid(mz)))

    q_map = lambda h, qi: (qi, h)
    return pl.pallas_call(
        body,
        name="flash_fwd",
        grid=(HEADS, n),
        in_specs=[
            pl.BlockSpec((t, HEAD_PAD), q_map),
            pl.BlockSpec((s, HEAD_PAD), lambda h, qi: (0, h)),
            pl.BlockSpec((s, HEAD_PAD), lambda h, qi: (0, h)),
            pl.BlockSpec((t, HEAD_DIM), lambda h, qi: (qi, MZ_SEG * HEADS + h)),
        ],
        out_specs=[pl.BlockSpec((t, HEAD_DIM), q_map)] * 4 + [HBM_SPEC],
        out_shape=[
            jax.ShapeDtypeStruct((s, D_MODEL), F32),
            jax.ShapeDtypeStruct((s, D_MODEL), F32),
            jax.ShapeDtypeStruct((s, D_MODEL), MXU_DTYPE),
            jax.ShapeDtypeStruct((s, D_MODEL), F32),
            jax.ShapeDtypeStruct((HEADS, n_pairs, t, t), MXU_DTYPE),
        ],
        scratch_shapes=[
            pltpu.VMEM((t, 1), F32),
            pltpu.VMEM((t, HEAD_PAD), F32),
            pltpu.VMEM((2, t, t), MXU_DTYPE),
            pltpu.SemaphoreType.DMA((2,)),
        ],
        compiler_params=_params(),
    )(q_all, k_all, v_all, proj)


def _merge_fwd_bwd(x, target, ya, yb, ao, proj, b_gate, final_g, w_pa, w_pb, w_out):
    s = x.shape[0]
    tm = min(s, TM_ROW)

    def body(x_ref, t_ref, ya_ref, yb_ref, ao_ref, mz_ref, g0_ref, g1_ref, bg_ref, fg_ref, wpa_ref, wpb_ref, wo_ref,
             dx2_ref, dya_ref, dao_ref, dmz_ref, dg0_ref, dg1_ref, mb_ref, dpab_ref, dpbb_ref, dx2b_ref,
             loss_ref, dfg_ref, dbg_ref):
        i = pl.program_id(0)

        @pl.when(i == 0)
        def _():
            loss_ref[...] = jnp.zeros_like(loss_ref)
            dfg_ref[...] = jnp.zeros_like(dfg_ref)
            dbg_ref[...] = jnp.zeros_like(dbg_ref)

        pa = _dot(ya_ref[...], wpa_ref[...])
        pb = _dot(yb_ref[...], wpb_ref[...])
        bg = bg_ref[...]
        g0 = _sigmoid(g0_ref[...] + bg[:, :D_MODEL])
        g1 = _sigmoid(g1_ref[...] + bg[:, D_MODEL:])
        merged = g0 * pa + g1 * pb
        mb = _mx(merged)
        mb_ref[...] = mb
        x2 = x_ref[...] + _dot(mb, wo_ref[...])
        r = lax.rsqrt(jnp.mean(x2 * x2, axis=-1, keepdims=True) + EPS)
        xn = x2 * r
        fg = fg_ref[...]
        diff = xn * fg - t_ref[...]
        loss_ref[...] += 0.5 * jnp.sum(jnp.mean(diff * diff, axis=-1, keepdims=True))
        dy = diff * (1.0 / D_MODEL)
        dfg_ref[...] += _bcast_rows(jnp.sum(dy * xn, axis=0, keepdims=True), 8)
        tt = dy * fg
        dx2 = r * (tt - xn * jnp.mean(tt * xn, axis=-1, keepdims=True))
        dx2_ref[...] = dx2
        dx2b = _mx(dx2)
        dx2b_ref[...] = dx2b
        dmerged = _dot_nt(dx2b, wo_ref[...])
        dpa = _mx(dmerged * g0)
        dpb = _mx(dmerged * g1)
        dpab_ref[...] = dpa
        dpbb_ref[...] = dpb
        dg0 = dmerged * pa * (g0 * (1.0 - g0))
        dg1 = dmerged * pb * (g1 * (1.0 - g1))
        dg0_ref[...] = _mx(dg0)
        dg1_ref[...] = _mx(dg1)
        dbg_ref[:, :D_MODEL] += _bcast_rows(jnp.sum(dg0, axis=0, keepdims=True), 8)
        dbg_ref[:, D_MODEL:] += _bcast_rows(jnp.sum(dg1, axis=0, keepdims=True), 8)
        dya_ref[...] = _dot_nt(dpa, wpa_ref[...])
        dyb = _dot_nt(dpb, wpb_ref[...])
        mz = mz_ref[...]
        sg = _sigmoid(mz)
        dao_ref[...] = _mx(dyb * (mz * sg))
        dmz_ref[...] = _mx(dyb * ao_ref[...] * (sg + mz * sg * (1.0 - sg)))

    def rows(w=D_MODEL):
        return pl.BlockSpec((tm, w), lambda i: (i, 0))

    def const(shape):
        return pl.BlockSpec(shape, lambda i: (0, 0))

    def seg(k):
        return pl.BlockSpec((tm, SEG), lambda i: (i, k))

    f32 = jax.ShapeDtypeStruct((s, D_MODEL), F32)
    b16 = jax.ShapeDtypeStruct((s, D_MODEL), MXU_DTYPE)
    return pl.pallas_call(
        body,
        name="merge_fwd_bwd",
        grid=(s // tm,),
        in_specs=[
            rows(), rows(), rows(), rows(), rows(),
            seg(MZ_SEG), seg(GL_SEG), seg(GL_SEG + 1),
            const((1, 2 * D_MODEL)), const((1, D_MODEL)),
            _resident((D_MODEL, D_MODEL)), _resident((D_MODEL, D_MODEL)), _resident((D_MODEL, D_MODEL)),
        ],
        out_specs=[rows()] * 10 + [const((8, HEAD_DIM)), const((8, D_MODEL)), const((8, 2 * D_MODEL))],
        out_shape=[f32, f32, b16, b16, b16, b16, b16, b16, b16, b16,
                   jax.ShapeDtypeStruct((8, HEAD_DIM), F32),
                   jax.ShapeDtypeStruct((8, D_MODEL), F32),
                   jax.ShapeDtypeStruct((8, 2 * D_MODEL), F32)],
        compiler_params=_params(),
    )(x, target, ya, yb, ao, proj, proj, proj, b_gate, final_g, w_pa, w_pb, w_out)


def _flash_bwd(q_all, k_all, v_all, dao, ao, lse, mblk, p_all, slab_sets):
    s = q_all.shape[0]
    t = min(s, T_ATT_BWD)
    n = s // t
    pairs = [(ki, qi) for ki in range(n) for qi in range(ki, n)]
    ki_list = jnp.asarray([p[0] for p in pairs], jnp.int32)
    qi_list = jnp.asarray([p[1] for p in pairs], jnp.int32)
    p_list = jnp.asarray([qi * (qi + 1) // 2 + ki for ki, qi in pairs], jnp.int32)
    n_ops = len(slab_sets)

    def body(ki_ref, qi_ref, pidx_ref, q_ref, k_ref, v_ref, do_ref, ao_ref, lse_ref, mblk_ref, p_ref, *rest):
        g_refs = rest[:n_ops]
        dq_ref, dk_ref, dv_ref = rest[n_ops:n_ops + 3]
        recv_refs = rest[n_ops + 3:2 * n_ops + 3]
        dk_acc, dv_acc, send_sems, recv_sems, local_sems = rest[2 * n_ops + 3:]
        head, step = pl.program_id(0), pl.program_id(1)
        ki, qi = ki_ref[step], qi_ref[step]

        @pl.when((head == 0) & (step == 0))
        def _():
            _Exchange(g_refs, recv_refs, send_sems, recv_sems, local_sems).start()

        @pl.when(qi == ki)
        def _():
            dk_acc[...] = jnp.zeros_like(dk_acc)
            dv_acc[...] = jnp.zeros_like(dv_acc)

        @pl.when(ki == 0)
        def _():
            dq_ref[pl.ds(pl.multiple_of(qi * t, t), t), :] = jnp.zeros((t, HEAD_PAD), F32)

        def pair(masked):
            nsub = ATT_SUB if masked else ATT_SUB_BWD
            ts = t // nsub
            dk_parts, dv_parts = [], []
            for r in range(nsub):
                rs = slice(r * ts, (r + 1) * ts)
                w = (r + 1) * ts if masked else t
                k = k_ref[:w]
                v = v_ref[:w, :HEAD_DIM]
                q = q_ref[rs]
                lane = lax.broadcasted_iota(jnp.int32, (ts, HEAD_DIM), 1)
                m_blk = jnp.max(jnp.where(lane == ki, mblk_ref[rs], -jnp.inf), axis=-1, keepdims=True)
                factor = jnp.exp2(m_blk - lse_ref[rs, 0:1])
                p_st = p_ref[0, 0, rs, :w]
                do = do_ref[rs]
                do_f = do.astype(F32)
                delta = jnp.sum(do_f * ao_ref[rs], axis=-1, keepdims=True)
                dv_part = _dot_tn(p_st, _mx(do_f * factor))
                ds = p_st * _mx((_dot_nt(do, v) - delta) * factor)
                dk_part = _dot_tn(ds, q)
                rows = pl.ds(pl.multiple_of(qi * t + r * ts, ts), ts)
                dq_ref[rows, :] += _dot(ds, k)
                if masked:
                    dk_acc[:w] += dk_part
                    dv_acc[:w] += dv_part
                else:
                    dk_parts.append(dk_part)
                    dv_parts.append(dv_part)

            if not masked:
                dk_acc[...] += sum(dk_parts[1:], dk_parts[0])
                dv_acc[...] += sum(dv_parts[1:], dv_parts[0])

        @pl.when(qi == ki)
        def _():
            pair(True)

        @pl.when(qi > ki)
        def _():
            pair(False)

        @pl.when(qi == n - 1)
        def _():
            dk_ref[...] = dk_acc[...] * LN2
            dv_ref[...] = dv_acc[...]

        @pl.when((head == HEADS - 1) & (step == len(pairs) - 1))
        def _():
            _Exchange(g_refs, recv_refs, send_sems, recv_sems, local_sems).wait()

    q_map = lambda h, p, ki_ref, qi_ref, pidx_ref: (qi_ref[p], h)
    kv_map = lambda h, p, ki_ref, qi_ref, pidx_ref: (ki_ref[p], h)
    grid_spec = pltpu.PrefetchScalarGridSpec(
        num_scalar_prefetch=3,
        grid=(HEADS, len(pairs)),
        in_specs=[
            pl.BlockSpec((t, HEAD_PAD), q_map),
            pl.BlockSpec((t, HEAD_PAD), kv_map),
            pl.BlockSpec((t, HEAD_PAD), kv_map),
            pl.BlockSpec((t, HEAD_DIM), q_map),
            pl.BlockSpec((t, HEAD_DIM), q_map),
            pl.BlockSpec((t, HEAD_DIM), q_map),
            pl.BlockSpec((t, HEAD_DIM), q_map),
            pl.BlockSpec((1, 1, t, t), lambda h, p, ki_ref, qi_ref, pidx_ref: (h, pidx_ref[p], 0, 0)),
        ] + [HBM_SPEC] * n_ops,
        out_specs=[
            pl.BlockSpec((s, HEAD_PAD), lambda h, p, ki_ref, qi_ref, pidx_ref: (0, h)),
            pl.BlockSpec((t, HEAD_PAD), kv_map),
            pl.BlockSpec((t, HEAD_DIM), kv_map),
        ] + [HBM_SPEC] * n_ops,
        scratch_shapes=[pltpu.VMEM((t, HEAD_PAD), F32), pltpu.VMEM((t, HEAD_DIM), F32)]
        + _Exchange.semaphores(n_ops),
    )
    outs = pl.pallas_call(
        body,
        name="flash_bwd",
        grid_spec=grid_spec,
        out_shape=[
            jax.ShapeDtypeStruct((s, HEADS * HEAD_PAD), F32),
            jax.ShapeDtypeStruct((s, HEADS * HEAD_PAD), F32),
            jax.ShapeDtypeStruct((s, D_MODEL), F32),
        ] + [jax.ShapeDtypeStruct(a.shape, a.dtype) for a in slab_sets],
        compiler_params=_params(VMEM_LIMIT_BIG),
    )(ki_list, qi_list, p_list, q_all, k_all, v_all, dao, ao, lse, mblk, p_all, *slab_sets)
    return outs[0], outs[1], outs[2], outs[3:]


def _mla_prep_bwd(dq_all, dk_all, dv_all, proj, q_a_g, kv_a_g, w_uq_p, w_kn, w_v, cos, sin_a, sin_b):
    s = proj.shape[0]
    tm = min(s, TM_ROW)

    def body(dq_ref, dk_ref, dv_ref, sm_ref, gq_ref, gk_ref, wq_ref, wkn_ref, wv_ref, cos_ref, sa_ref, sb_ref,
             dsm_ref, dqf_ref, dkn_ref, dvb_ref, dgq_ref, dgk_ref):
        i = pl.program_id(0)

        @pl.when(i == 0)
        def _():
            dgq_ref[...] = jnp.zeros_like(dgq_ref)
            dgk_ref[...] = jnp.zeros_like(dgk_ref)

        cos_t, sa, sb = cos_ref[...], sa_ref[...], sb_ref[...]
        dkpe = jnp.zeros((tm, HEAD_DIM), F32)
        for h in range(HEADS):
            lo = h * HEAD_PAD
            dqf_ref[:, lo:lo + HEAD_DIM] = _mx(dq_ref[:, lo:lo + HEAD_DIM] * QK_SCALE)
            dqf_ref[:, lo + HEAD_DIM:lo + HEAD_PAD] = _mx(
                _rope_t(dq_ref[:, lo + HEAD_DIM:lo + HEAD_PAD] * QK_SCALE, cos_t, sa, sb))
            dkn_ref[:, h * HEAD_DIM:(h + 1) * HEAD_DIM] = _mx(dk_ref[:, lo:lo + HEAD_DIM])
            dkpe = dkpe + dk_ref[:, lo + HEAD_DIM:lo + HEAD_PAD]
        dkr = _rope_t(dkpe, cos_t, sa, sb)
        dvb = _mx(dv_ref[...])
        dvb_ref[...] = dvb
        dcqn = _dot_nt(dqf_ref[...], wq_ref[...])
        dckvn = _dot_nt(dkn_ref[...], wkn_ref[...]) + _dot_nt(dvb, wv_ref[...])

        small = sm_ref[...]
        cq = small[:, :Q_LORA]
        ckv = small[:, Q_LORA:Q_LORA + KV_LORA]
        rq = lax.rsqrt(jnp.mean(cq * cq, axis=-1, keepdims=True) + EPS)
        rk = lax.rsqrt(jnp.mean(ckv * ckv, axis=-1, keepdims=True) + EPS)
        cqh = cq * rq
        ckh = ckv * rk
        dgq_ref[...] += _bcast_rows(jnp.sum(dcqn * cqh, axis=0, keepdims=True), 8)
        dgk_ref[...] += _bcast_rows(jnp.sum(dckvn * ckh, axis=0, keepdims=True), 8)
        tq = dcqn * gq_ref[...]
        tk = dckvn * gk_ref[...]
        dcq = rq * (tq - cqh * jnp.mean(tq * cqh, axis=-1, keepdims=True))
        dckv = rk * (tk - ckh * jnp.mean(tk * ckh, axis=-1, keepdims=True))
        dsm_ref[:, :Q_LORA] = _mx(dcq)
        dsm_ref[:, Q_LORA:Q_LORA + KV_LORA] = _mx(dckv)
        dsm_ref[:, Q_LORA + KV_LORA:Q_LORA + KV_LORA + HEAD_DIM] = _mx(dkr)
        dsm_ref[:, Q_LORA + KV_LORA + HEAD_DIM:] = jnp.zeros((tm, SEG - Q_LORA - KV_LORA - HEAD_DIM), MXU_DTYPE)

    def const(shape):
        return pl.BlockSpec(shape, lambda i: (0, 0))

    def rows(w):
        return pl.BlockSpec((tm, w), lambda i: (i, 0))

    return pl.pallas_call(
        body,
        name="mla_prep_bwd",
        grid=(s // tm,),
        in_specs=[
            rows(HEADS * HEAD_PAD), rows(HEADS * HEAD_PAD), rows(D_MODEL),
            pl.BlockSpec((tm, SEG), lambda i: (i, SMALL_SEG)),
            const((1, Q_LORA)), const((1, KV_LORA)),
            const((Q_LORA, HEADS * HEAD_PAD)), const((KV_LORA, D_MODEL)), const((KV_LORA, D_MODEL)),
            rows(HEAD_DIM), rows(HEAD_DIM), rows(HEAD_DIM),
        ],
        out_specs=[rows(SEG), rows(HEADS * HEAD_PAD), rows(D_MODEL), rows(D_MODEL),
                   const((8, Q_LORA)), const((8, KV_LORA))],
        out_shape=[
            jax.ShapeDtypeStruct((s, SEG), MXU_DTYPE),
            jax.ShapeDtypeStruct((s, HEADS * HEAD_PAD), MXU_DTYPE),
            jax.ShapeDtypeStruct((s, D_MODEL), MXU_DTYPE),
            jax.ShapeDtypeStruct((s, D_MODEL), MXU_DTYPE),
            jax.ShapeDtypeStruct((8, Q_LORA), F32),
            jax.ShapeDtypeStruct((8, KV_LORA), F32),
        ],
        compiler_params=_params(),
    )(dq_all, dk_all, dv_all, proj, q_a_g, kv_a_g, w_uq_p, w_kn, w_v, cos, sin_a, sin_b)


def _hgrn_bwd(proj, lb_logits, hg_norm_g, o_all, dya, states):
    s = proj.shape[0]
    t = min(s, T_HGRN)
    nb = s // t
    nc = t // HG_CHUNK

    def body(hq_ref, hf_ref, hi_ref, hz_ref, lb_ref, g_ref, o_ref, dya_ref, st_ref,
             dhq_ref, dhf_ref, dhi_ref, dhz_ref, dlb_ref, dg_ref, dstate, u_sc, g_sc, stf_sc, stb_sc, dstb_sc):
        h, b = pl.program_id(0), pl.program_id(1)

        @pl.when(b == 0)
        def _():
            dstate[...] = jnp.zeros_like(dstate)
            dlb_ref[...] = jnp.zeros_like(dlb_ref)

        @pl.when((b == 0) & (h == 0))
        def _():
            dg_ref[...] = jnp.zeros_like(dg_ref)

        lower = _chunk_lower_mask(t)
        pos = _chunk_pos(t)
        ghg = g_ref[...]
        for hh in range(HG_HEADS_PER_STEP):
            cols = slice(hh * HEAD_DIM, (hh + 1) * HEAD_DIM)
            hq, hf, hz = hq_ref[:, cols], hf_ref[:, cols], hz_ref[:, cols]
            gt = _hgrn_gates(hq, hf, lb_ref[:, cols], pos)
            vb = _mx(hi_ref[:, cols])
            qi, ki, ko = gt["qi"], gt["ki"], gt["ko"]
            qib, kib, kob = _mx(qi), _mx(ki), _mx(ko)

            o = o_ref[:, cols]
            sz = _sigmoid(hz)
            r = lax.rsqrt(jnp.mean(o * o, axis=-1, keepdims=True) + EPS)
            on = o * r
            dya_t = dya_ref[:, cols]
            don = dya_t * (hz * sz)
            dhz_ref[:, cols] = _mx(dya_t * (on * ghg) * (sz + hz * sz * (1.0 - sz)))
            dg_ref[...] += _bcast_rows(jnp.sum(don * on, axis=0, keepdims=True), 8)
            tt = don * ghg
            do = r * (tt - on * jnp.mean(tt * on, axis=-1, keepdims=True))
            dob = _mx(do)

            for c in range(nc):
                sl = slice(c * HG_CHUNK, (c + 1) * HG_CHUNK)
                u_sc[hh, c] = _dot_tn(vb[sl], kob[sl])
                g_sc[hh, c] = _dot_tn(dob[sl], qib[sl])

            st = st_ref[0, hh]
            for c in range(nc):
                stf_sc[hh, c] = st
                stb_sc[hh, c] = _mx(st)
                if c < nc - 1:
                    st = st * gt["dec"][c * HG_CHUNK:c * HG_CHUNK + 1, :] + u_sc[hh, c]

            dst = dstate[hh]
            dd_parts = [None] * nc
            for c in reversed(range(nc)):
                dec = gt["dec"][c * HG_CHUNK:c * HG_CHUNK + 1, :]
                dstb_sc[hh, c] = _mx(dst)
                dd_parts[c] = _bcast_rows(jnp.sum(dst * stf_sc[hh, c], axis=0, keepdims=True) * dec, HG_CHUNK)
                dst = dst * dec + g_sc[hh, c]
            dstate[hh] = dst

            a = jnp.where(lower, _dot_nt(qib, kib), 0.0)
            da = _mx(jnp.where(lower, _dot_nt(dob, vb), 0.0))
            dqi_intra = _dot(da, kib)
            dki = _dot_tn(da, qib)
            dv_intra = _dot_tn(_mx(a), dob)

            dqi_parts, dko_parts, dv_parts = [None] * nc, [None] * nc, [None] * nc
            for c in range(nc):
                sl = slice(c * HG_CHUNK, (c + 1) * HG_CHUNK)
                dv_parts[c] = dv_intra[sl] + _dot_nt(kob[sl], dstb_sc[hh, c])
                dko_parts[c] = _dot(vb[sl], dstb_sc[hh, c])
                dqi_parts[c] = dqi_intra[sl] + _dot(dob[sl], stb_sc[hh, c])
            dqi = jnp.concatenate(dqi_parts, axis=0)
            dko = jnp.concatenate(dko_parts, axis=0)
            dv = jnp.concatenate(dv_parts, axis=0)
            dd = jnp.concatenate(dd_parts, axis=0)

            dq = dqi * gt["eb"]
            dk = dki * gt["enb"] + dko * gt["eo"]
            db = dqi * qi - dki * ki - dko * ko
            dlogf = _rcumsum_chunk(db, pos) + _chunk_total(dko * ko) + dd
            df = dlogf / gt["f"] - dk
            lb, sig, sq = gt["lb"], gt["sig"], gt["sq"]
            dhf_ref[:, cols] = _mx(df * (1.0 - lb) * (sig * (1.0 - sig)))
            dhq_ref[:, cols] = _mx(dq * (sq + hq * sq * (1.0 - sq)))
            dhi_ref[:, cols] = _mx(dv)
            dlb = jnp.sum(df * (1.0 - sig), axis=0, keepdims=True) * (lb * (1.0 - lb))
            dlb_ref[:, cols] += jnp.concatenate([dlb, -dlb], axis=0)

    hw = HG_HEADS_PER_STEP * HEAD_DIM
    hsteps = HEADS // HG_HEADS_PER_STEP

    def seg(k):
        return pl.BlockSpec((t, hw), lambda h, b, k=k: (nb - 1 - b, k * hsteps + h))

    blk = pl.BlockSpec((t, hw), lambda h, b: (nb - 1 - b, h))
    b16 = jax.ShapeDtypeStruct((s, D_MODEL), MXU_DTYPE)
    return pl.pallas_call(
        body,
        name="hgrn_bwd",
        grid=(hsteps, nb),
        in_specs=[seg(0), seg(1), seg(2), seg(3),
                  pl.BlockSpec((2, hw), lambda h, b: (0, h)),
                  pl.BlockSpec((1, HEAD_DIM), lambda h, b: (0, 0)),
                  blk, blk,
                  pl.BlockSpec((1, HG_HEADS_PER_STEP, HEAD_DIM, HEAD_DIM), lambda h, b: (nb - 1 - b, h, 0, 0))],
        out_specs=[blk, blk, blk, blk,
                   pl.BlockSpec((2, hw), lambda h, b: (0, h)),
                   pl.BlockSpec((8, HEAD_DIM), lambda h, b: (0, 0))],
        out_shape=[b16, b16, b16, b16,
                   jax.ShapeDtypeStruct((2, D_MODEL), F32),
                   jax.ShapeDtypeStruct((8, HEAD_DIM), F32)],
        scratch_shapes=[pltpu.VMEM((HG_HEADS_PER_STEP, HEAD_DIM, HEAD_DIM), F32)]
        + [pltpu.VMEM((HG_HEADS_PER_STEP, nc, HEAD_DIM, HEAD_DIM), F32)] * 3
        + [pltpu.VMEM((HG_HEADS_PER_STEP, nc, HEAD_DIM, HEAD_DIM), MXU_DTYPE)] * 2,
        compiler_params=_params(),
    )(proj, proj, proj, proj, lb_logits, hg_norm_g, o_all, dya, states)


def _dh_bwd(segs, w_in_p, x, dx2, norm_g, late_slab, late_recv_init, misc_slabs):
    s = x.shape[0]
    tm = min(s, TM_ROW)
    nseg = len(segs)
    nsteps = s // tm
    late_xyc = ((LATE_DEV >> 2) & 1, (LATE_DEV >> 1) & 1, LATE_DEV & 1)

    def body(*refs):
        seg_refs = refs[:nseg]
        (w_ref, x_ref, dx2_ref, g_ref, late_ref, _, misc_ref,
         gx_ref, dng_ref, late_recv_ref, misc_recv_ref,
         dp_buf, send_sems, recv_sems, local_sems, late_send, late_recvs, late_local) = refs[nseg:]
        i = pl.program_id(0)
        me = 4 * lax.axis_index("x") + 2 * lax.axis_index("y") + lax.axis_index("c")

        def misc_exchange():
            return _Exchange([misc_ref], [misc_recv_ref], send_sems, recv_sems, local_sems)

        def late_copy(sender):
            return pltpu.make_async_remote_copy(
                src_ref=late_ref.at[0], dst_ref=late_recv_ref.at[sender], send_sem=late_send,
                recv_sem=late_recvs.at[(sender ^ LATE_DEV) - 1], device_id=late_xyc, device_id_type=MESH)

        def late_own():
            return pltpu.make_async_copy(late_ref.at[0], late_recv_ref.at[LATE_DEV], late_local)

        @pl.when(i == 0)
        def _():
            dng_ref[...] = jnp.zeros_like(dng_ref)
            misc_exchange().start()

        @pl.when((i == 0) & (me != LATE_DEV))
        def _():
            late_copy(me).start()

        @pl.when((i == 0) & (me == LATE_DEV))
        def _():
            late_own().start()

        for k, sref in enumerate(seg_refs):
            dp_buf[:, k * SEG:(k + 1) * SEG] = sref[...]
        dh = _dot_nt(dp_buf[...], w_ref[...])
        xf = x_ref[...]
        r = lax.rsqrt(jnp.mean(xf * xf, axis=-1, keepdims=True) + EPS)
        xh = xf * r
        dng_ref[...] += _bcast_rows(jnp.sum(dh * xh, axis=0, keepdims=True), 8)
        tt = dh * g_ref[...]
        gx_ref[...] = dx2_ref[...] + r * (tt - xh * jnp.mean(tt * xh, axis=-1, keepdims=True))

        @pl.when(i == nsteps - 1)
        def _():
            misc_exchange().wait()

        @pl.when((i == nsteps - 1) & (me != LATE_DEV))
        def _():
            late_copy(me).wait_send()

        @pl.when((i == nsteps - 1) & (me == LATE_DEV))
        def _():
            for k in range(1, N_DEV):
                late_copy(LATE_DEV ^ k).wait_recv()
            late_own().wait()

    rows = pl.BlockSpec((tm, D_MODEL), lambda i: (i, 0))
    return pl.pallas_call(
        body,
        name="dh_bwd",
        grid=(nsteps,),
        in_specs=[pl.BlockSpec((tm, SEG), lambda i: (i, 0))] * nseg + [
            _resident((D_MODEL, PROJ_W)),
            rows, rows,
            pl.BlockSpec((1, D_MODEL), lambda i: (0, 0)),
            HBM_SPEC, HBM_SPEC, HBM_SPEC,
        ],
        out_specs=[rows, pl.BlockSpec((8, D_MODEL), lambda i: (0, 0)), HBM_SPEC, HBM_SPEC],
        out_shape=[jax.ShapeDtypeStruct((s, D_MODEL), F32), jax.ShapeDtypeStruct((8, D_MODEL), F32),
                   jax.ShapeDtypeStruct(late_recv_init.shape, late_recv_init.dtype),
                   jax.ShapeDtypeStruct(misc_slabs.shape, misc_slabs.dtype)],
        input_output_aliases={nseg + 5: 2},
        scratch_shapes=[pltpu.VMEM((tm, PROJ_W), MXU_DTYPE)] + _Exchange.semaphores(1)
        + [pltpu.SemaphoreType.DMA, pltpu.SemaphoreType.DMA((N_DEV - 1,)), pltpu.SemaphoreType.DMA],
        compiler_params=_params(),
    )(*segs, w_in_p, x, dx2, norm_g, late_slab, late_recv_init, misc_slabs)


def _matmul_tn(a, b, name, out_dtype=F32):
    s, m = a.shape
    n = b.shape[1]
    ts = min(s, TS_TN)
    tn = min(n, SEG)
    nk = s // ts

    def body(a_ref, b_ref, o_ref, acc):
        k = pl.program_id(1)
        part = _dot_tn(a_ref[...], b_ref[...])

        @pl.when(k == 0)
        def _():
            acc[...] = part

        @pl.when(k > 0)
        def _():
            acc[...] += part

        @pl.when(k == nk - 1)
        def _():
            o_ref[...] = acc[...].astype(out_dtype)

    return pl.pallas_call(
        body,
        name=name,
        grid=(n // tn, nk),
        in_specs=[pl.BlockSpec((ts, m), lambda j, k: (k, 0)), pl.BlockSpec((ts, tn), lambda j, k: (k, j))],
        out_specs=pl.BlockSpec((m, tn), lambda j, k: (0, j)),
        out_shape=jax.ShapeDtypeStruct((m, n), out_dtype),
        scratch_shapes=[pltpu.VMEM((m, tn), F32)],
        compiler_params=_params(),
    )(a, b)


def _w_in_pieces():
    per = IN_COLS // N_DEV
    pad_at = SMALL_SEG * SEG + Q_LORA + KV_LORA + QK_ROPE
    pieces = []
    for j in range(N_DEV):
        u0, u1 = j * per, (j + 1) * per
        cuts = [u0] + ([pad_at] if u0 < pad_at < u1 else []) + [u1]
        for a, b in zip(cuts[:-1], cuts[1:]):
            pieces.append((j, a - u0, b - u0, a if a < pad_at else a + PROJ_W - IN_COLS))
    return pad_at, pieces


def _assemble_w_in(gathered):
    tr = TM_ROW
    pad_at, pieces = _w_in_pieces()

    def body(in_ref, out_ref):
        out_ref[:, pad_at:pad_at + PROJ_W - IN_COLS] = jnp.zeros((tr, PROJ_W - IN_COLS), gathered.dtype)
        for j, a, b, p0 in pieces:
            out_ref[:, p0:p0 + b - a] = in_ref[j, :, a:b]

    return pl.pallas_call(
        body,
        name="assemble_w_in",
        grid=(D_MODEL // tr,),
        in_specs=[pl.BlockSpec((N_DEV, tr, PACK_COLS), lambda i: (0, i, 0))],
        out_specs=pl.BlockSpec((tr, PROJ_W), lambda i: (i, 0)),
        out_shape=jax.ShapeDtypeStruct((D_MODEL, PROJ_W), gathered.dtype),
        compiler_params=_params(),
    )(gathered)


def _scatter_dw_in(dw_segs, devs, name):
    tr = TM_ROW
    _, pieces = _w_in_pieces()
    per = IN_COLS // N_DEV
    seg_ids = sorted(dw_segs)
    nseg = len(seg_ids)

    def body(*refs):
        out_ref, buf = refs[nseg:]
        for k in range(PROJ_W // SEG):
            if k in seg_ids:
                buf[:, k * SEG:(k + 1) * SEG] = refs[seg_ids.index(k)][...]
            else:
                buf[:, k * SEG:(k + 1) * SEG] = jnp.zeros((tr, SEG), F32)
        for slot, dev in enumerate(devs):
            out_ref[slot, :, per:] = jnp.zeros((tr, PACK_COLS - per), TRANSPORT_DTYPE)
            for j, a, b, p0 in pieces:
                if j == dev:
                    out_ref[slot, :, a:b] = buf[:, p0:p0 + b - a].astype(TRANSPORT_DTYPE)

    return pl.pallas_call(
        body,
        name=name,
        grid=(D_MODEL // tr,),
        in_specs=[pl.BlockSpec((tr, SEG), lambda i: (i, 0))] * nseg,
        out_specs=pl.BlockSpec((len(devs), tr, PACK_COLS), lambda i: (0, i, 0)),
        out_shape=jax.ShapeDtypeStruct((len(devs), D_MODEL, PACK_COLS), TRANSPORT_DTYPE),
        scratch_shapes=[pltpu.VMEM((tr, PROJ_W), F32)],
        compiler_params=_params(),
    )(*[dw_segs[k] for k in seg_ids])


def _rope_tables(s):
    inv = ROPE_THETA ** (-jnp.arange(0, QK_ROPE, 2, dtype=F32) / QK_ROPE)
    ang = jnp.arange(s, dtype=F32)[:, None] * inv[None, :]
    cos, sin = jnp.cos(ang), jnp.sin(ang)
    z32 = jnp.zeros_like(cos)
    z64 = jnp.zeros((s, HEAD_DIM - QK_ROPE), F32)
    cos_t = jnp.concatenate([cos, cos, z64], axis=1)
    sin_a = jnp.concatenate([-sin, z32, z64], axis=1)
    sin_b = jnp.concatenate([z32, sin, z64], axis=1)
    return cos_t, sin_a, sin_b


def _pack_misc(w_uq, w_ukv, norm_g, b_gate, lb_logits, hg_norm_g, q_a_g, kv_a_g, final_norm_g, extra):
    misc = jnp.concatenate([hg_norm_g.reshape(-1), q_a_g.reshape(-1), kv_a_g.reshape(-1), extra.reshape(-1),
                            jnp.zeros((PACK_COLS - HEAD_DIM - Q_LORA - KV_LORA - 1,), F32)])
    return jnp.concatenate([w_uq.reshape(ROWS_W_UQ, PACK_COLS), w_ukv.reshape(ROWS_W_UKV, PACK_COLS),
                            norm_g.reshape(1, -1), b_gate.reshape(2, -1), lb_logits.reshape(2, -1),
                            misc.reshape(1, -1), final_norm_g.reshape(1, -1), jnp.zeros((1, PACK_COLS), F32)], axis=0)


def _unpack_misc(p):
    sm = p[ROWS_W_UQ + ROWS_W_UKV:]
    misc = sm[5]
    return dict(
        w_uq=p[:ROWS_W_UQ].reshape(1, Q_LORA, QK_DIM),
        w_ukv=p[ROWS_W_UQ:ROWS_W_UQ + ROWS_W_UKV].reshape(1, KV_LORA, 2 * HEAD_DIM),
        norm_g=sm[0:1], b_gate=sm[1:3].reshape(1, -1), lb_logits=sm[3:5],
        hg_norm_g=misc[None, :HEAD_DIM], q_a_g=misc[None, HEAD_DIM:HEAD_DIM + Q_LORA],
        kv_a_g=misc[None, HEAD_DIM + Q_LORA:HEAD_DIM + Q_LORA + KV_LORA], final_norm_g=sm[6],
        extra=misc[HEAD_DIM + Q_LORA + KV_LORA],
    )


def _weight_shard_buffers(w_in, w_uq, w_ukv, w_pa, w_pb, w_out):
    w_in_pad = jnp.pad(w_in.reshape(D_MODEL, -1), ((0, 0), (0, PACK_COLS - IN_COLS // N_DEV)))
    parts = [a.reshape(-1, PACK_COLS) for a in (w_pa, w_pb, w_out, w_uq, w_ukv)]
    others = jnp.concatenate(parts + [jnp.zeros((ROWS_OTHER - ROWS_OTHER_USED, PACK_COLS), F32)], axis=0)
    return w_in_pad.astype(MXU_DTYPE), others.astype(MXU_DTYPE)


def _other_weights(gathered):
    r0 = 0
    mats = []
    for _ in range(3):
        mats.append(gathered[:, r0:r0 + ROWS_W_PROJ].reshape(D_MODEL, D_MODEL))
        r0 += ROWS_W_PROJ
    w_uq = gathered[:, r0:r0 + ROWS_W_UQ].reshape(N_DEV, Q_LORA, QK_DIM).transpose(1, 0, 2)
    w_uq_p = jnp.concatenate([w_uq, jnp.zeros((Q_LORA, HEADS, HEAD_PAD - QK_DIM), w_uq.dtype)], axis=2)
    w_uq_p = w_uq_p.reshape(Q_LORA, HEADS * HEAD_PAD)
    r0 += ROWS_W_UQ
    w_ukv = gathered[:, r0:r0 + ROWS_W_UKV].reshape(N_DEV, KV_LORA, 2 * HEAD_DIM).transpose(1, 0, 2)
    w_kn = w_ukv[:, :, :HEAD_DIM].reshape(KV_LORA, D_MODEL)
    w_v = w_ukv[:, :, HEAD_DIM:].reshape(KV_LORA, D_MODEL)
    return w_uq_p, w_kn, w_v, mats[0], mats[1], mats[2]


def _misc_slabs(dw_uq_p, dw_kn, dw_v, small):
    dw_uq = dw_uq_p.reshape(Q_LORA, HEADS, HEAD_PAD)[:, :, :QK_DIM].transpose(1, 0, 2)
    dw_ukv = jnp.concatenate([dw_kn.reshape(KV_LORA, HEADS, HEAD_DIM),
                              dw_v.reshape(KV_LORA, HEADS, HEAD_DIM)], axis=2).transpose(1, 0, 2)
    return jnp.stack([_pack_misc(dw_uq[j], dw_ukv[j], *small) for j in range(N_DEV)])


def _step_gradients(x, target, norm_g, b_gate, lb_logits, hg_norm_g, q_a_g, kv_a_g, final_g,
                    w_in_p, other_shard):
    s = x.shape[0]
    cos, sin_a, sin_b = _rope_tables(s)
    proj, h, gathered = _inproj(x, norm_g, w_in_p, other_shard)
    w_uq_p, w_kn, w_v, w_pa, w_pb, w_out = _other_weights(gathered)
    o_all, ya, states = _hgrn_fwd(proj, lb_logits, hg_norm_g)
    q_all, k_all, v_all, cqn, ckvn = _mla_prep(proj, q_a_g, kv_a_g, w_uq_p, w_kn, w_v, cos, sin_a, sin_b)
    ao, lse, yb, mblk, p_all = _flash_fwd(q_all, k_all, v_all, proj)
    (dx2, dya, dao, dmz, dg0, dg1, merged_b, dpa_b, dpb_b, dx2_b,
     loss_acc, dfg_acc, dbg_acc) = _merge_fwd_bwd(x, target, ya, yb, ao, proj, b_gate, final_g, w_pa, w_pb, w_out)
    dhq, dhf, dhi, dhz, dlb, dhg_acc = _hgrn_bwd(proj, lb_logits, hg_norm_g, o_all, dya, states)

    early = {0: dhq, 1: dhf, 2: dhi, 3: dhz, MZ_SEG: dmz, GL_SEG: dg0, GL_SEG + 1: dg1}
    dw_early = {k: _matmul_tn(h, sg, "dw_in_%d" % k) for k, sg in early.items()}
    mats = [_matmul_tn(a, b, name, TRANSPORT_DTYPE).reshape(N_DEV, ROWS_W_PROJ, PACK_COLS)
            for a, b, name in ((ya, dpa_b, "dw_pa"), (yb, dpb_b, "dw_pb"), (merged_b, dx2_b, "dw_out"))]
    early_slabs = [_scatter_dw_in(dw_early, list(range(N_DEV)), "scatter_dw_in")] + mats
    dq_all, dk_all, dv_all, (recv_in, recv_pa, recv_pb, recv_out) = _flash_bwd(
        q_all, k_all, v_all, dao, ao, lse, mblk, p_all, early_slabs)

    dsmall, dqf_b, dkn_b, dv_b, dgq_acc, dgk_acc = _mla_prep_bwd(
        dq_all, dk_all, dv_all, proj, q_a_g, kv_a_g, w_uq_p, w_kn, w_v, cos, sin_a, sin_b)
    late_slab = _scatter_dw_in({SMALL_SEG: _matmul_tn(h, dsmall, "dw_in_%d" % SMALL_SEG)}, [LATE_DEV],
                               "scatter_dw_in_late")
    segs = [dhq, dhf, dhi, dhz, dsmall, dmz, dg0, dg1]
    return dict(
        segs=segs, h=h, dx2=dx2, late_slab=late_slab,
        dw_uq_p=_matmul_tn(cqn, dqf_b, "dw_uq"), dw_kn=_matmul_tn(ckvn, dkn_b, "dw_kn"),
        dw_v=_matmul_tn(ckvn, dv_b, "dw_v"),
        small=dict(b_gate=dbg_acc[0:1], lb_logits=dlb, hg_norm_g=dhg_acc[0:1], q_a_g=dgq_acc[0:1],
                   kv_a_g=dgk_acc[0:1], final_norm_g=dfg_acc[0], loss=loss_acc[0, 0]),
        recv=dict(w_in=[recv_in], w_pa=[recv_pa], w_pb=[recv_pb], w_out=[recv_out]),
    )


def kernel(x, norm_g, w_in, b_gate, lb_logits, hg_norm_g, q_a_g, w_uq, kv_a_g, w_ukv, w_proj_a, w_proj_b, w_out, final_norm_g, loss_target, m_norm_g, m_w_in, m_b_gate, m_lb_logits, m_hg_norm_g, m_q_a_g, m_w_uq, m_kv_a_g, m_w_ukv, m_w_proj_a, m_w_proj_b, m_w_out, m_final_norm_g, v_norm_g, v_w_in, v_b_gate, v_lb_logits, v_hg_norm_g, v_q_a_g, v_w_uq, v_kv_a_g, v_w_ukv, v_w_proj_a, v_w_proj_b, v_w_out, v_final_norm_g):
    zero = jnp.zeros((1,), F32)
    xs = x[0]
    w_in_shard, other_shard = _weight_shard_buffers(w_in, w_uq, w_ukv, w_proj_a, w_proj_b, w_out)
    w_in_p = _assemble_w_in(_all_gather_packed(w_in_shard))
    g = _step_gradients(xs, loss_target[0], norm_g, b_gate, lb_logits, hg_norm_g, q_a_g, kv_a_g,
                        final_norm_g.reshape(1, -1), w_in_p, other_shard)
    sm = g["small"]
    misc_slabs = _misc_slabs(g["dw_uq_p"], g["dw_kn"], g["dw_v"],
                             (jnp.zeros_like(norm_g), sm["b_gate"], sm["lb_logits"], sm["hg_norm_g"], sm["q_a_g"],
                              sm["kv_a_g"], sm["final_norm_g"], sm["loss"]))
    late_recv_init = jnp.zeros((N_DEV, D_MODEL, PACK_COLS), TRANSPORT_DTYPE)
    grad_x, dng_acc, recv_late, recv_misc = _dh_bwd(g["segs"], w_in_p, xs, g["dx2"], norm_g,
                                                    g["late_slab"], late_recv_init, misc_slabs)
    recv_ng = _exchange_rows(jnp.broadcast_to(dng_acc[None], (N_DEV, 8, D_MODEL)))

    recv = g["recv"]
    out_in = _sum_adamw(recv["w_in"] + [recv_late], w_in[0], m_w_in[0], v_w_in[0], "adamw_w_in")
    out_pa = _sum_adamw(recv["w_pa"], w_proj_a[0], m_w_proj_a[0], v_w_proj_a[0], "adamw_w_pa")
    out_pb = _sum_adamw(recv["w_pb"], w_proj_b[0], m_w_proj_b[0], v_w_proj_b[0], "adamw_w_pb")
    out_out = _sum_adamw(recv["w_out"], w_out[0], m_w_out[0], v_w_out[0], "adamw_w_out")
    out_ng = _sum_adamw([recv_ng], *[jnp.broadcast_to(a, (8, D_MODEL)) for a in (norm_g, m_norm_g, v_norm_g)],
                        "adamw_norm_g")
    out_misc = _sum_adamw(
        [recv_misc],
        _pack_misc(w_uq, w_ukv, norm_g, b_gate, lb_logits, hg_norm_g, q_a_g, kv_a_g, final_norm_g, zero),
        _pack_misc(m_w_uq, m_w_ukv, m_norm_g, m_b_gate, m_lb_logits, m_hg_norm_g, m_q_a_g, m_kv_a_g,
                   m_final_norm_g, zero),
        _pack_misc(v_w_uq, v_w_ukv, v_norm_g, v_b_gate, v_lb_logits, v_hg_norm_g, v_q_a_g, v_kv_a_g,
                   v_final_norm_g, zero),
        "adamw_misc")
    names = ["norm_g", "w_in", "b_gate", "lb_logits", "hg_norm_g", "q_a_g", "w_uq", "kv_a_g", "w_ukv",
             "w_proj_a", "w_proj_b", "w_out", "final_norm_g"]
    kinds = []
    for i in range(4):
        d = _unpack_misc(out_misc[i])
        d.update(w_in=out_in[i][None], w_proj_a=out_pa[i][None], w_proj_b=out_pb[i][None], w_out=out_out[i][None],
                 norm_g=out_ng[i][0:1])
        kinds.append(d)
    return (kinds[0]["extra"], grad_x[None], *[d[n] for d in kinds for n in names])
```

```python
import functools

import jax
import jax.numpy as jnp
from jax import lax
from jax.experimental import pallas as pl
from jax.experimental.pallas import tpu as pltpu

D_MODEL = 1024
HEADS = 8
HEAD_DIM = 128
HG_CHUNK = 32
Q_LORA = 384
KV_LORA = 256
QK_ROPE = 64
QK_DIM = 192
ROPE_THETA = 10000.0
EPS = 1e-6
IN_COLS = 7872
ADAM_LR = 0.001
ADAM_B1 = 0.9
ADAM_B2 = 0.999
ADAM_EPS = 1e-08
ADAM_WD = 0.01
ADAM_STEP = 10

N_DEV = 8
SEG = 1024
PROJ_W = 8 * SEG
SMALL_SEG = 4
MZ_SEG = 5
GL_SEG = 6
HEAD_PAD = 256
PACK_COLS = 1024
ROWS_W_UQ = 72
ROWS_W_UKV = 64
ROWS_W_PROJ = 128
ROWS_OTHER_USED = 3 * ROWS_W_PROJ + ROWS_W_UQ + ROWS_W_UKV
ROWS_OTHER = 528
LATE_DEV = (SMALL_SEG * SEG) // (IN_COLS // N_DEV)
assert (SMALL_SEG * SEG + Q_LORA + KV_LORA + QK_ROPE - 1) // (IN_COLS // N_DEV) == LATE_DEV

QK_SCALE = QK_DIM ** -0.5
LOG2E = 1.4426950408889634
LN2 = 0.6931471805599453
Q_PRESCALE = QK_SCALE * LOG2E

MXU_DTYPE = jnp.bfloat16
TRANSPORT_DTYPE = jnp.bfloat16
VMEM_LIMIT = 48 * 1024 * 1024
VMEM_LIMIT_BIG = 60 * 1024 * 1024

T_HGRN = 256
HG_HEADS_PER_STEP = 4
TM_ROW = 256
T_ATT = 1024
T_ATT_BWD = T_ATT
ATT_SUB = 4
LSE_LANE = 127
ATT_SUB_BWD = 2
TS_TN = 2048
TR_ADAM = 256

F32 = jnp.float32
MESH = pl.DeviceIdType.MESH


def _dot(a, b):
    return jnp.dot(a, b, preferred_element_type=F32)


def _dot_nt(a, b):
    return lax.dot_general(a, b, (((1,), (1,)), ((), ())), preferred_element_type=F32)


def _dot_tn(a, b):
    return lax.dot_general(a, b, (((0,), (0,)), ((), ())), preferred_element_type=F32)


def _mx(a):
    return a.astype(MXU_DTYPE)


def _sigmoid(x):
    return 1.0 / (1.0 + jnp.exp(-x))


def _params(vmem=VMEM_LIMIT, **kw):
    return pltpu.CompilerParams(vmem_limit_bytes=vmem, **kw)


def _bcast_rows(row, n):
    return jnp.broadcast_to(row, (n, row.shape[-1]))


def _resident(shape):
    return pl.BlockSpec(shape, lambda *_: (0, 0), pipeline_mode=pl.Buffered(1))


HBM_SPEC = pl.BlockSpec(memory_space=pltpu.HBM)


def _all_gather_packed(shard):
    rows, cols = shard.shape

    def body(x_ref, out_ref, send_sems, recv_sems, local_sem):
        x, y, c = lax.axis_index("x"), lax.axis_index("y"), lax.axis_index("c")
        me, sibling = (x, y, c), (x, y, 1 - c)
        chips = [(1 - x, y), (x, 1 - y), (1 - x, 1 - y)]

        def slot(px, py, pc):
            return out_ref.at[4 * px + 2 * py + pc]

        def copy(k, block, to, src=None):
            return pltpu.make_async_remote_copy(
                src_ref=slot(*block) if src is None else src,
                dst_ref=slot(*block),
                send_sem=send_sems.at[k],
                recv_sem=recv_sems.at[k],
                device_id=to,
                device_id_type=MESH,
            )

        mine = pltpu.make_async_copy(x_ref, slot(*me), local_sem)
        mine.start()
        first = [copy(0, me, sibling, src=x_ref)]
        first += [copy(1 + j, me, (*chip, c), src=x_ref) for j, chip in enumerate(chips)]
        for cp in first:
            cp.start()
        passed = [copy(4 + j, (*chip, c), sibling) for j, chip in enumerate(chips)]
        for j, chip in enumerate(chips):
            copy(1 + j, (*chip, c), me).wait_recv()
            passed[j].start()
        copy(0, sibling, me).wait_recv()
        for j, chip in enumerate(chips):
            copy(4 + j, (*chip, 1 - c), me).wait_recv()
        for cp in first + passed:
            cp.wait_send()
        mine.wait()

    return pl.pallas_call(
        body,
        name="ag_weights",
        out_shape=jax.ShapeDtypeStruct((N_DEV, rows, cols), shard.dtype),
        in_specs=[HBM_SPEC],
        out_specs=HBM_SPEC,
        scratch_shapes=[
            pltpu.SemaphoreType.DMA((7,)),
            pltpu.SemaphoreType.DMA((7,)),
            pltpu.SemaphoreType.DMA,
        ],
    )(shard)


class _Exchange:
    def __init__(self, g_refs, recv_refs, send_sems, recv_sems, local_sems, gather=False):
        x, y, c = lax.axis_index("x"), lax.axis_index("y"), lax.axis_index("c")
        me = 4 * x + 2 * y + c
        n_ops = len(g_refs)

        def source(i, dest):
            return g_refs[i] if gather else g_refs[i].at[dest]

        def copy(i, k, landing):
            px, py, pc = x ^ ((k >> 2) & 1), y ^ ((k >> 1) & 1), c ^ (k & 1)
            peer = 4 * px + 2 * py + pc
            return pltpu.make_async_remote_copy(
                src_ref=source(i, peer),
                dst_ref=recv_refs[i].at[peer if landing else me],
                send_sem=send_sems.at[i * (N_DEV - 1) + k - 1],
                recv_sem=recv_sems.at[i * (N_DEV - 1) + k - 1],
                device_id=(px, py, pc),
                device_id_type=MESH,
            )

        pairs = [(i, k) for i in range(n_ops) for k in range(1, N_DEV)]
        self.mine = lambda: [pltpu.make_async_copy(source(i, me), recv_refs[i].at[me], local_sems.at[i])
                             for i in range(n_ops)]
        self.sends = lambda: [copy(i, k, False) for i, k in pairs]
        self.landings = lambda: [copy(i, k, True) for i, k in pairs]

    def start(self):
        for cp in self.mine() + self.sends():
            cp.start()

    def wait(self):
        for cp in self.landings():
            cp.wait_recv()
        for cp in self.sends():
            cp.wait_send()
        for cp in self.mine():
            cp.wait()

    @staticmethod
    def semaphores(n_ops):
        return [pltpu.SemaphoreType.DMA((n_ops * (N_DEV - 1),)),
                pltpu.SemaphoreType.DMA((n_ops * (N_DEV - 1),)),
                pltpu.SemaphoreType.DMA((n_ops,))]


def _exchange_rows(slabs):
    def body(g_ref, recv_ref, send_sems, recv_sems, local_sems):
        exchange = _Exchange([g_ref], [recv_ref], send_sems, recv_sems, local_sems)
        exchange.start()
        exchange.wait()

    return pl.pallas_call(
        body,
        name="exchange_rows",
        out_shape=jax.ShapeDtypeStruct(slabs.shape, slabs.dtype),
        in_specs=[HBM_SPEC],
        out_specs=HBM_SPEC,
        scratch_shapes=_Exchange.semaphores(1),
    )(slabs)


def _sum_adamw(recvs, w, m, v, name):
    rows, cols = w.shape
    tr = min(rows, TR_ADAM)
    n_recv = len(recvs)

    def body(*refs):
        w_ref, m_ref, v_ref, g_out, d_out, m_out, v_out = refs[n_recv:]
        g = None
        for r_ref in refs[:n_recv]:
            for i in range(N_DEV):
                part = r_ref[i].astype(F32)
                g = part if g is None else g + part
        g = g[:, :cols]
        m_new = ADAM_B1 * m_ref[...] + (1.0 - ADAM_B1) * g
        v_new = ADAM_B2 * v_ref[...] + (1.0 - ADAM_B2) * (g * g)
        m_hat = m_new / (1.0 - ADAM_B1 ** ADAM_STEP)
        v_hat = v_new / (1.0 - ADAM_B2 ** ADAM_STEP)
        g_out[...] = g
        d_out[...] = -ADAM_LR * (m_hat / (jnp.sqrt(v_hat) + ADAM_EPS) + ADAM_WD * w_ref[...])
        m_out[...] = m_new
        v_out[...] = v_new

    row_spec = pl.BlockSpec((tr, cols), lambda i: (i, 0))
    shape = jax.ShapeDtypeStruct((rows, cols), F32)
    return pl.pallas_call(
        body,
        name=name,
        grid=(rows // tr,),
        in_specs=[pl.BlockSpec((N_DEV, tr, PACK_COLS), lambda i: (0, i, 0))] * n_recv + [row_spec] * 3,
        out_specs=[row_spec] * 4,
        out_shape=[shape] * 4,
        compiler_params=_params(),
    )(*recvs, w, m, v)


def _inproj(x, norm_g, w_in_p, other_shard):
    s = x.shape[0]
    tm = min(s, TM_ROW)
    nsteps = s // tm

    def body(x_ref, g_ref, w_ref, shard_ref, proj_ref, h_ref, gathered_ref, send_sems, recv_sems, local_sems):
        i = pl.program_id(0)

        def all_gather():
            return _Exchange([shard_ref], [gathered_ref], send_sems, recv_sems, local_sems, gather=True)

        @pl.when(i == 0)
        def _():
            all_gather().start()

        xf = x_ref[...]
        r = lax.rsqrt(jnp.mean(xf * xf, axis=-1, keepdims=True) + EPS)
        h = _mx(xf * r * g_ref[...])
        h_ref[...] = h
        for j in range(PROJ_W // SEG):
            cols = slice(j * SEG, (j + 1) * SEG)
            proj_ref[:, cols] = _dot(h, w_ref[:, cols])

        @pl.when(i == nsteps - 1)
        def _():
            all_gather().wait()

    return pl.pallas_call(
        body,
        name="inproj",
        grid=(nsteps,),
        in_specs=[
            pl.BlockSpec((tm, D_MODEL), lambda i: (i, 0)),
            pl.BlockSpec((1, D_MODEL), lambda i: (0, 0)),
            _resident((D_MODEL, PROJ_W)),
            HBM_SPEC,
        ],
        out_specs=[
            pl.BlockSpec((tm, PROJ_W), lambda i: (i, 0)),
            pl.BlockSpec((tm, D_MODEL), lambda i: (i, 0)),
            HBM_SPEC,
        ],
        out_shape=[
            jax.ShapeDtypeStruct((s, PROJ_W), F32),
            jax.ShapeDtypeStruct((s, D_MODEL), MXU_DTYPE),
            jax.ShapeDtypeStruct((N_DEV,) + other_shard.shape, other_shard.dtype),
        ],
        scratch_shapes=_Exchange.semaphores(1),
        compiler_params=_params(),
    )(x, norm_g, w_in_p, other_shard)


def _chunk_lower_mask(t):
    row = lax.broadcasted_iota(jnp.int32, (t, t), 0)
    col = lax.broadcasted_iota(jnp.int32, (t, t), 1)
    return ((row // HG_CHUNK) == (col // HG_CHUNK)) & (col <= row)


def _chunk_pos(t):
    return lax.broadcasted_iota(jnp.int32, (t, HEAD_DIM), 0) & (HG_CHUNK - 1)


def _cumsum_chunk(x, pos):
    sh = 1
    while sh < HG_CHUNK:
        x = x + jnp.where(pos >= sh, pltpu.roll(x, sh, 0), 0.0)
        sh *= 2
    return x


def _rcumsum_chunk(x, pos):
    t = x.shape[0]
    sh = 1
    while sh < HG_CHUNK:
        x = x + jnp.where(pos < HG_CHUNK - sh, pltpu.roll(x, t - sh, 0), 0.0)
        sh *= 2
    return x


def _chunk_total(x):
    t, w = x.shape
    tot = jnp.sum(x.reshape(t // HG_CHUNK, HG_CHUNK, w), axis=1, keepdims=True)
    return jnp.broadcast_to(tot, (t // HG_CHUNK, HG_CHUNK, w)).reshape(t, w)


def _hgrn_gates(hq, hf, lb_logits, pos):
    lb = _sigmoid(lb_logits[0:1, :] - lb_logits[1:2, :])
    sig = _sigmoid(hf)
    f = lb + (1.0 - lb) * sig
    sq = _sigmoid(hq)
    q = hq * sq
    k = 1.0 - f
    logf = jnp.log(f)
    bcum = _cumsum_chunk(logf, pos)
    blast = _chunk_total(logf)
    eb = jnp.exp(bcum)
    enb = jnp.exp(-bcum)
    eo = jnp.exp(blast - bcum)
    return dict(lb=lb, sig=sig, f=f, sq=sq, q=q, k=k, eb=eb, enb=enb, eo=eo,
                qi=q * eb, ki=k * enb, ko=k * eo, dec=jnp.exp(blast))


def _hgrn_fwd(proj, lb_logits, hg_norm_g):
    s = proj.shape[0]
    t = min(s, T_HGRN)
    nb = s // t
    nc = t // HG_CHUNK
    hw = HG_HEADS_PER_STEP * HEAD_DIM

    def body(hq_ref, hf_ref, hi_ref, hz_ref, lb_ref, g_ref, o_ref, ya_ref, st_ref, state, u_sc, stb_sc):
        b = pl.program_id(1)

        @pl.when(b == 0)
        def _():
            state[...] = jnp.zeros_like(state)

        lower = _chunk_lower_mask(t)
        pos = _chunk_pos(t)
        for hh in range(HG_HEADS_PER_STEP):
            cols = slice(hh * HEAD_DIM, (hh + 1) * HEAD_DIM)
            st = state[hh]
            st_ref[0, hh] = st
            gt = _hgrn_gates(hq_ref[:, cols], hf_ref[:, cols], lb_ref[:, cols], pos)
            vb = _mx(hi_ref[:, cols])
            qib, kib, kob = _mx(gt["qi"]), _mx(gt["ki"]), _mx(gt["ko"])
            a = jnp.where(lower, _dot_nt(qib, kib), 0.0)
            o_intra = _dot(_mx(a), vb)
            for c in range(nc):
                sl = slice(c * HG_CHUNK, (c + 1) * HG_CHUNK)
                u_sc[hh, c] = _dot_tn(vb[sl], kob[sl])
            for c in range(nc):
                stb_sc[hh, c] = _mx(st)
                st = st * gt["dec"][c * HG_CHUNK:c * HG_CHUNK + 1, :] + u_sc[hh, c]
            state[hh] = st
            outs = []
            for c in range(nc):
                sl = slice(c * HG_CHUNK, (c + 1) * HG_CHUNK)
                outs.append(o_intra[sl] + _dot_nt(qib[sl], stb_sc[hh, c]))
            o = jnp.concatenate(outs, axis=0)
            o_ref[:, cols] = o
            r = lax.rsqrt(jnp.mean(o * o, axis=-1, keepdims=True) + EPS)
            hz = hz_ref[:, cols]
            ya_ref[:, cols] = _mx((o * r * g_ref[...]) * (hz * _sigmoid(hz)))

    hsteps = HEADS // HG_HEADS_PER_STEP

    def seg(k):
        return pl.BlockSpec((t, hw), lambda h, b, k=k: (b, k * hsteps + h))

    return pl.pallas_call(
        body,
        name="hgrn_fwd",
        grid=(hsteps, nb),
        in_specs=[seg(0), seg(1), seg(2), seg(3),
                  pl.BlockSpec((2, hw), lambda h, b: (0, h)),
                  pl.BlockSpec((1, HEAD_DIM), lambda h, b: (0, 0))],
        out_specs=[
            pl.BlockSpec((t, hw), lambda h, b: (b, h)),
            pl.BlockSpec((t, hw), lambda h, b: (b, h)),
            pl.BlockSpec((1, HG_HEADS_PER_STEP, HEAD_DIM, HEAD_DIM), lambda h, b: (b, h, 0, 0)),
        ],
        out_shape=[
            jax.ShapeDtypeStruct((s, D_MODEL), F32),
            jax.ShapeDtypeStruct((s, D_MODEL), MXU_DTYPE),
            jax.ShapeDtypeStruct((nb, HEADS, HEAD_DIM, HEAD_DIM), F32),
        ],
        scratch_shapes=[pltpu.VMEM((HG_HEADS_PER_STEP, HEAD_DIM, HEAD_DIM), F32),
                        pltpu.VMEM((HG_HEADS_PER_STEP, nc, HEAD_DIM, HEAD_DIM), F32),
                        pltpu.VMEM((HG_HEADS_PER_STEP, nc, HEAD_DIM, HEAD_DIM), MXU_DTYPE)],
        compiler_params=_params(),
    )(proj, proj, proj, proj, lb_logits, hg_norm_g)


def _rope(x, cos, sin_a, sin_b):
    return x * cos + pltpu.roll(x, 96, 1) * sin_a + pltpu.roll(x, 32, 1) * sin_b


def _rope_t(d, cos, sin_a, sin_b):
    return d * cos + pltpu.roll(d * sin_a, 32, 1) + pltpu.roll(d * sin_b, 96, 1)


def _mla_prep(proj, q_a_g, kv_a_g, w_uq_p, w_kn, w_v, cos, sin_a, sin_b):
    s = proj.shape[0]
    tm = min(s, TM_ROW)

    def body(sm_ref, gq_ref, gk_ref, wq_ref, wkn_ref, wv_ref, cos_ref, sa_ref, sb_ref,
             q_ref, k_ref, v_ref, cqn_ref, ckvn_ref):
        small = sm_ref[...]
        cq = small[:, :Q_LORA]
        ckv = small[:, Q_LORA:Q_LORA + KV_LORA]
        krp = small[:, Q_LORA + KV_LORA:Q_LORA + KV_LORA + HEAD_DIM]
        rq = lax.rsqrt(jnp.mean(cq * cq, axis=-1, keepdims=True) + EPS)
        rk = lax.rsqrt(jnp.mean(ckv * ckv, axis=-1, keepdims=True) + EPS)
        cqn = _mx(cq * rq * gq_ref[...])
        ckvn = _mx(ckv * rk * gk_ref[...])
        cqn_ref[...] = cqn
        ckvn_ref[...] = ckvn
        q = _dot(cqn, wq_ref[...]) * Q_PRESCALE
        kn = _dot(ckvn, wkn_ref[...])
        v = _dot(ckvn, wv_ref[...])
        cos_t, sa, sb = cos_ref[...], sa_ref[...], sb_ref[...]
        kpe = _mx(_rope(krp, cos_t, sa, sb))
        ones_col = (lax.broadcasted_iota(jnp.int32, (tm, HEAD_DIM), 1) == 0).astype(MXU_DTYPE)
        for h in range(HEADS):
            lo = h * HEAD_PAD
            v_ref[:, lo:lo + HEAD_DIM] = _mx(v[:, h * HEAD_DIM:(h + 1) * HEAD_DIM])
            v_ref[:, lo + HEAD_DIM:lo + HEAD_PAD] = ones_col
            q_ref[:, lo:lo + HEAD_DIM] = _mx(q[:, lo:lo + HEAD_DIM])
            q_ref[:, lo + HEAD_DIM:lo + HEAD_PAD] = _mx(_rope(q[:, lo + HEAD_DIM:lo + HEAD_PAD], cos_t, sa, sb))
            k_ref[:, lo:lo + HEAD_DIM] = _mx(kn[:, h * HEAD_DIM:(h + 1) * HEAD_DIM])
            k_ref[:, lo + HEAD_DIM:lo + HEAD_PAD] = kpe

    def const(shape):
        return pl.BlockSpec(shape, lambda i: (0, 0))

    def rows(w):
        return pl.BlockSpec((tm, w), lambda i: (i, 0))

    return pl.pallas_call(
        body,
        name="mla_prep",
        grid=(s // tm,),
        in_specs=[
            pl.BlockSpec((tm, SEG), lambda i: (i, SMALL_SEG)),
            const((1, Q_LORA)), const((1, KV_LORA)),
            const((Q_LORA, HEADS * HEAD_PAD)), const((KV_LORA, D_MODEL)), const((KV_LORA, D_MODEL)),
            rows(HEAD_DIM), rows(HEAD_DIM), rows(HEAD_DIM),
        ],
        out_specs=[rows(HEADS * HEAD_PAD)] * 3 + [rows(Q_LORA), rows(KV_LORA)],
        out_shape=[
            jax.ShapeDtypeStruct((s, HEADS * HEAD_PAD), MXU_DTYPE),
            jax.ShapeDtypeStruct((s, HEADS * HEAD_PAD), MXU_DTYPE),
            jax.ShapeDtypeStruct((s, HEADS * HEAD_PAD), MXU_DTYPE),
            jax.ShapeDtypeStruct((s, Q_LORA), MXU_DTYPE),
            jax.ShapeDtypeStruct((s, KV_LORA), MXU_DTYPE),
        ],
        compiler_params=_params(),
    )(proj, q_a_g, kv_a_g, w_uq_p, w_kn, w_v, cos, sin_a, sin_b)


def _diag_mask(t):
    row = lax.broadcasted_iota(jnp.int32, (t, t), 0)
    col = lax.broadcasted_iota(jnp.int32, (t, t), 1)
    return row >= col


def _flash_fwd(q_all, k_all, v_all, proj):
    s = q_all.shape[0]
    t = min(s, T_ATT)
    n = s // t
    ts = t // ATT_SUB
    n_pairs = n * (n + 1) // 2

    def body(q_ref, k_ref, v_ref, mz_ref, ao_ref, yb_ref, mblk_ref, p_hbm, m_sc, acc_sc, stage, p_sems):
        head, qi = pl.program_id(0), pl.program_id(1)
        m_sc[...] = jnp.full_like(m_sc, -jnp.inf)
        acc_sc[...] = jnp.zeros_like(acc_sc)
        mblk_ref[...] = jnp.zeros_like(mblk_ref)
        lane = lax.broadcasted_iota(jnp.int32, (ts, HEAD_DIM), 1)

        def p_copy(slot, pair, r):
            rows = pl.ds(r * ts, ts)
            return pltpu.make_async_copy(stage.at[slot, rows], p_hbm.at[head, pair, rows], p_sems.at[slot, r])

        def p_wait(slot):
            for r in range(ATT_SUB):
                p_copy(slot, 0, r).wait()

        def key_block(ki, diagonal):
            base = pl.multiple_of(ki * t, t)
            slot = lax.rem(ki, 2)
            sc, pb, alpha = {}, {}, {}

            @pl.when(ki >= 2)
            def _():
                p_wait(slot)

            if diagonal:
                stage[slot] = jnp.zeros((t, t), MXU_DTYPE)

            def width(r):
                return (r + 1) * ts if diagonal else t

            def scores(r):
                w = width(r)
                s_r = _dot_nt(q_ref[r * ts:(r + 1) * ts], k_ref[pl.ds(base, w), :])
                if diagonal:
                    row = lax.broadcasted_iota(jnp.int32, (ts, w), 0) + r * ts
                    col = lax.broadcasted_iota(jnp.int32, (ts, w), 1)
                    s_r = jnp.where(row >= col, s_r, -jnp.inf)
                sc[r] = s_r

            def softmax(r):
                rs = slice(r * ts, (r + 1) * ts)
                m_prev = m_sc[rs]
                m_new = jnp.maximum(m_prev, jnp.max(sc[r], axis=-1, keepdims=True))
                pb[r] = _mx(jnp.exp2(sc[r] - m_new))
                alpha[r] = jnp.exp2(m_prev - m_new)
                m_sc[rs] = m_new
                mblk_ref[rs] = jnp.where(lane == ki, m_new, mblk_ref[rs])
                stage[slot, rs, :width(r)] = pb[r]
                p_copy(slot, qi * (qi + 1) // 2 + ki, r).start()

            def weighted_values(r):
                rs = slice(r * ts, (r + 1) * ts)
                acc_sc[rs] = alpha[r] * acc_sc[rs] + _dot(pb[r], v_ref[pl.ds(base, width(r)), :])

            for step in range(ATT_SUB + 2):
                if step < ATT_SUB:
                    scores(step)
                if 1 <= step <= ATT_SUB:
                    softmax(step - 1)
                if step >= 2:
                    weighted_values(step - 2)

        def below_diagonal(ki, carry):
            key_block(ki, False)
            return carry

        lax.fori_loop(0, qi, below_diagonal, 0)
        key_block(qi, True)

        @pl.when(qi >= 1)
        def _():
            p_wait(1 - lax.rem(qi, 2))

        p_wait(lax.rem(qi, 2))

        acc = acc_sc[...]
        l = acc[:, HEAD_DIM:HEAD_DIM + 1]
        ao = acc[:, :HEAD_DIM] / l
        ao_ref[...] = ao
        lane_t = lax.broadcasted_iota(jnp.int32, (t, HEAD_DIM), 1)
        mblk_ref[...] = jnp.where(lane_t == LSE_LANE, m_sc[...] + jnp.log2(l), mblk_ref[...])
        mz = mz_ref[...]
        yb_ref[...] = _mx(ao * (mz * _sigmoid(mz)))

    q_map = lambda h, qi: (qi, h)
    return pl.pallas_call(
        body,
        name="flash_fwd",
        grid=(HEADS, n),
        in_specs=[
            pl.BlockSpec((t, HEAD_PAD), q_map),
            pl.BlockSpec((s, HEAD_PAD), lambda h, qi: (0, h)),
            pl.BlockSpec((s, HEAD_PAD), lambda h, qi: (0, h)),
            pl.BlockSpec((t, HEAD_DIM), lambda h, qi: (qi, MZ_SEG * HEADS + h)),
        ],
        out_specs=[pl.BlockSpec((t, HEAD_DIM), q_map)] * 3 + [HBM_SPEC],
        out_shape=[
            jax.ShapeDtypeStruct((s, D_MODEL), F32),
            jax.ShapeDtypeStruct((s, D_MODEL), MXU_DTYPE),
            jax.ShapeDtypeStruct((s, D_MODEL), F32),
            jax.ShapeDtypeStruct((HEADS, n_pairs, t, t), MXU_DTYPE),
        ],
        scratch_shapes=[
            pltpu.VMEM((t, 1), F32),
            pltpu.VMEM((t, HEAD_PAD), F32),
            pltpu.VMEM((2, t, t), MXU_DTYPE),
            pltpu.SemaphoreType.DMA((2, ATT_SUB)),
        ],
        compiler_params=_params(),
    )(q_all, k_all, v_all, proj)


def _merge_fwd_bwd(x, target, ya, yb, ao, proj, b_gate, final_g, w_pa, w_pb, w_out):
    s = x.shape[0]
    tm = min(s, TM_ROW)

    def body(x_ref, t_ref, ya_ref, yb_ref, ao_ref, mz_ref, g0_ref, g1_ref, bg_ref, fg_ref, wpa_ref, wpb_ref, wo_ref,
             dx2_ref, dya_ref, dao_ref, dmz_ref, dg0_ref, dg1_ref, mb_ref, dpab_ref, dpbb_ref, dx2b_ref,
             loss_ref, dfg_ref, dbg_ref):
        i = pl.program_id(0)

        @pl.when(i == 0)
        def _():
            loss_ref[...] = jnp.zeros_like(loss_ref)
            dfg_ref[...] = jnp.zeros_like(dfg_ref)
            dbg_ref[...] = jnp.zeros_like(dbg_ref)

        pa = _dot(ya_ref[...], wpa_ref[...])
        pb = _dot(yb_ref[...], wpb_ref[...])
        bg = bg_ref[...]
        g0 = _sigmoid(g0_ref[...] + bg[:, :D_MODEL])
        g1 = _sigmoid(g1_ref[...] + bg[:, D_MODEL:])
        merged = g0 * pa + g1 * pb
        mb = _mx(merged)
        mb_ref[...] = mb
        x2 = x_ref[...] + _dot(mb, wo_ref[...])
        r = lax.rsqrt(jnp.mean(x2 * x2, axis=-1, keepdims=True) + EPS)
        xn = x2 * r
        fg = fg_ref[...]
        diff = xn * fg - t_ref[...]
        loss_ref[...] += 0.5 * jnp.sum(jnp.mean(diff * diff, axis=-1, keepdims=True))
        dy = diff * (1.0 / D_MODEL)
        dfg_ref[...] += _bcast_rows(jnp.sum(dy * xn, axis=0, keepdims=True), 8)
        tt = dy * fg
        dx2 = r * (tt - xn * jnp.mean(tt * xn, axis=-1, keepdims=True))
        dx2_ref[...] = dx2
        dx2b = _mx(dx2)
        dx2b_ref[...] = dx2b
        dmerged = _dot_nt(dx2b, wo_ref[...])
        dpa = _mx(dmerged * g0)
        dpb = _mx(dmerged * g1)
        dpab_ref[...] = dpa
        dpbb_ref[...] = dpb
        dg0 = dmerged * pa * (g0 * (1.0 - g0))
        dg1 = dmerged * pb * (g1 * (1.0 - g1))
        dg0_ref[...] = _mx(dg0)
        dg1_ref[...] = _mx(dg1)
        dbg_ref[:, :D_MODEL] += _bcast_rows(jnp.sum(dg0, axis=0, keepdims=True), 8)
        dbg_ref[:, D_MODEL:] += _bcast_rows(jnp.sum(dg1, axis=0, keepdims=True), 8)
        dya_ref[...] = _dot_nt(dpa, wpa_ref[...])
        dyb = _dot_nt(dpb, wpb_ref[...])
        mz = mz_ref[...]
        sg = _sigmoid(mz)
        dao_ref[...] = _mx(dyb * (mz * sg))
        dmz_ref[...] = _mx(dyb * ao_ref[...] * (sg + mz * sg * (1.0 - sg)))

    def rows(w=D_MODEL):
        return pl.BlockSpec((tm, w), lambda i: (i, 0))

    def const(shape):
        return pl.BlockSpec(shape, lambda i: (0, 0))

    def seg(k):
        return pl.BlockSpec((tm, SEG), lambda i: (i, k))

    f32 = jax.ShapeDtypeStruct((s, D_MODEL), F32)
    b16 = jax.ShapeDtypeStruct((s, D_MODEL), MXU_DTYPE)
    return pl.pallas_call(
        body,
        name="merge_fwd_bwd",
        grid=(s // tm,),
        in_specs=[
            rows(), rows(), rows(), rows(), rows(),
            seg(MZ_SEG), seg(GL_SEG), seg(GL_SEG + 1),
            const((1, 2 * D_MODEL)), const((1, D_MODEL)),
            _resident((D_MODEL, D_MODEL)), _resident((D_MODEL, D_MODEL)), _resident((D_MODEL, D_MODEL)),
        ],
        out_specs=[rows()] * 10 + [const((8, HEAD_DIM)), const((8, D_MODEL)), const((8, 2 * D_MODEL))],
        out_shape=[f32, f32, b16, b16, b16, b16, b16, b16, b16, b16,
                   jax.ShapeDtypeStruct((8, HEAD_DIM), F32),
                   jax.ShapeDtypeStruct((8, D_MODEL), F32),
                   jax.ShapeDtypeStruct((8, 2 * D_MODEL), F32)],
        compiler_params=_params(),
    )(x, target, ya, yb, ao, proj, proj, proj, b_gate, final_g, w_pa, w_pb, w_out)


def _flash_bwd(q_all, k_all, v_all, dao, ao, mblk, p_all, slab_sets):
    s = q_all.shape[0]
    t = min(s, T_ATT_BWD)
    n = s // t
    pairs = [(ki, qi) for ki in range(n) for qi in range(ki, n)]
    ki_list = jnp.asarray([p[0] for p in pairs], jnp.int32)
    qi_list = jnp.asarray([p[1] for p in pairs], jnp.int32)
    p_list = jnp.asarray([qi * (qi + 1) // 2 + ki for ki, qi in pairs], jnp.int32)
    n_ops = len(slab_sets)

    def body(ki_ref, qi_ref, pidx_ref, q_ref, k_ref, v_ref, do_ref, ao_ref, mblk_ref, p_ref, *rest):
        g_refs = rest[:n_ops]
        dq_ref, dk_ref, dv_ref = rest[n_ops:n_ops + 3]
        recv_refs = rest[n_ops + 3:2 * n_ops + 3]
        dk_acc, dv_acc, send_sems, recv_sems, local_sems = rest[2 * n_ops + 3:]
        head, step = pl.program_id(0), pl.program_id(1)
        ki, qi = ki_ref[step], qi_ref[step]

        @pl.when((head == 0) & (step == 0))
        def _():
            _Exchange(g_refs, recv_refs, send_sems, recv_sems, local_sems).start()

        @pl.when(qi == ki)
        def _():
            dk_acc[...] = jnp.zeros_like(dk_acc)
            dv_acc[...] = jnp.zeros_like(dv_acc)

        @pl.when(ki == 0)
        def _():
            dq_ref[pl.ds(pl.multiple_of(qi * t, t), t), :] = jnp.zeros((t, HEAD_PAD), F32)

        def pair(masked):
            nsub = ATT_SUB if masked else ATT_SUB_BWD
            ts = t // nsub
            dk_parts, dv_parts = [], []
            for r in range(nsub):
                rs = slice(r * ts, (r + 1) * ts)
                w = (r + 1) * ts if masked else t
                k = k_ref[:w]
                v = v_ref[:w]
                q = q_ref[rs]
                lane = lax.broadcasted_iota(jnp.int32, (ts, HEAD_DIM), 1)
                stats = mblk_ref[rs]
                m_blk = jnp.max(jnp.where(lane == ki, stats, -jnp.inf), axis=-1, keepdims=True)
                lse = jnp.max(jnp.where(lane == LSE_LANE, stats, -jnp.inf), axis=-1, keepdims=True)
                factor = jnp.exp2(m_blk - lse)
                p_st = p_ref[0, 0, rs, :w]
                do = do_ref[rs]
                do_f = do.astype(F32)
                delta = jnp.sum(do_f * ao_ref[rs], axis=-1, keepdims=True)
                dv_part = _dot_tn(p_st, _mx(do_f * factor))
                ds = p_st * _mx((_dot_nt(do, v) - delta) * factor)
                dk_part = _dot_tn(ds, q)
                rows = pl.ds(pl.multiple_of(qi * t + r * ts, ts), ts)
                dq_ref[rows, :] += _dot(ds, k)
                if masked:
                    dk_acc[:w] += dk_part
                    dv_acc[:w] += dv_part
                else:
                    dk_parts.append(dk_part)
                    dv_parts.append(dv_part)

            if not masked:
                dk_acc[...] += sum(dk_parts[1:], dk_parts[0])
                dv_acc[...] += sum(dv_parts[1:], dv_parts[0])

        @pl.when(qi == ki)
        def _():
            pair(True)

        @pl.when(qi > ki)
        def _():
            pair(False)

        @pl.when(qi == n - 1)
        def _():
            dk_ref[...] = dk_acc[...] * LN2
            dv_ref[...] = dv_acc[...]

        @pl.when((head == HEADS - 1) & (step == len(pairs) - 1))
        def _():
            _Exchange(g_refs, recv_refs, send_sems, recv_sems, local_sems).wait()

    q_map = lambda h, p, ki_ref, qi_ref, pidx_ref: (qi_ref[p], h)
    kv_map = lambda h, p, ki_ref, qi_ref, pidx_ref: (ki_ref[p], h)
    grid_spec = pltpu.PrefetchScalarGridSpec(
        num_scalar_prefetch=3,
        grid=(HEADS, len(pairs)),
        in_specs=[
            pl.BlockSpec((t, HEAD_PAD), q_map),
            pl.BlockSpec((t, HEAD_PAD), kv_map),
            pl.BlockSpec((t, HEAD_DIM), lambda h, p, ki_ref, qi_ref, pidx_ref: (ki_ref[p], 2 * h)),
            pl.BlockSpec((t, HEAD_DIM), q_map),
            pl.BlockSpec((t, HEAD_DIM), q_map),
            pl.BlockSpec((t, HEAD_DIM), q_map),
            pl.BlockSpec((1, 1, t, t), lambda h, p, ki_ref, qi_ref, pidx_ref: (h, pidx_ref[p], 0, 0)),
        ] + [HBM_SPEC] * n_ops,
        out_specs=[
            pl.BlockSpec((s, HEAD_PAD), lambda h, p, ki_ref, qi_ref, pidx_ref: (0, h)),
            pl.BlockSpec((t, HEAD_PAD), kv_map),
            pl.BlockSpec((t, HEAD_DIM), kv_map),
        ] + [HBM_SPEC] * n_ops,
        scratch_shapes=[pltpu.VMEM((t, HEAD_PAD), F32), pltpu.VMEM((t, HEAD_DIM), F32)]
        + _Exchange.semaphores(n_ops),
    )
    outs = pl.pallas_call(
        body,
        name="flash_bwd",
        grid_spec=grid_spec,
        out_shape=[
            jax.ShapeDtypeStruct((s, HEADS * HEAD_PAD), F32),
            jax.ShapeDtypeStruct((s, HEADS * HEAD_PAD), F32),
            jax.ShapeDtypeStruct((s, D_MODEL), F32),
        ] + [jax.ShapeDtypeStruct(a.shape, a.dtype) for a in slab_sets],
        compiler_params=_params(VMEM_LIMIT_BIG),
    )(ki_list, qi_list, p_list, q_all, k_all, v_all, dao, ao, mblk, p_all, *slab_sets)
    return outs[0], outs[1], outs[2], outs[3:]


def _mla_prep_bwd(dq_all, dk_all, dv_all, proj, q_a_g, kv_a_g, w_uq_p, w_kn, w_v, cos, sin_a, sin_b):
    s = proj.shape[0]
    tm = min(s, TM_ROW)

    def body(dq_ref, dk_ref, dv_ref, sm_ref, gq_ref, gk_ref, wq_ref, wkn_ref, wv_ref, cos_ref, sa_ref, sb_ref,
             dsm_ref, dqf_ref, dkn_ref, dvb_ref, dgq_ref, dgk_ref):
        i = pl.program_id(0)

        @pl.when(i == 0)
        def _():
            dgq_ref[...] = jnp.zeros_like(dgq_ref)
            dgk_ref[...] = jnp.zeros_like(dgk_ref)

        cos_t, sa, sb = cos_ref[...], sa_ref[...], sb_ref[...]
        dkpe = jnp.zeros((tm, HEAD_DIM), F32)
        for h in range(HEADS):
            lo = h * HEAD_PAD
            dqf_ref[:, lo:lo + HEAD_DIM] = _mx(dq_ref[:, lo:lo + HEAD_DIM] * QK_SCALE)
            dqf_ref[:, lo + HEAD_DIM:lo + HEAD_PAD] = _mx(
                _rope_t(dq_ref[:, lo + HEAD_DIM:lo + HEAD_PAD] * QK_SCALE, cos_t, sa, sb))
            dkn_ref[:, h * HEAD_DIM:(h + 1) * HEAD_DIM] = _mx(dk_ref[:, lo:lo + HEAD_DIM])
            dkpe = dkpe + dk_ref[:, lo + HEAD_DIM:lo + HEAD_PAD]
        dkr = _rope_t(dkpe, cos_t, sa, sb)
        dvb = _mx(dv_ref[...])
        dvb_ref[...] = dvb
        dcqn = _dot_nt(dqf_ref[...], wq_ref[...])
        dckvn = _dot_nt(dkn_ref[...], wkn_ref[...]) + _dot_nt(dvb, wv_ref[...])

        small = sm_ref[...]
        cq = small[:, :Q_LORA]
        ckv = small[:, Q_LORA:Q_LORA + KV_LORA]
        rq = lax.rsqrt(jnp.mean(cq * cq, axis=-1, keepdims=True) + EPS)
        rk = lax.rsqrt(jnp.mean(ckv * ckv, axis=-1, keepdims=True) + EPS)
        cqh = cq * rq
        ckh = ckv * rk
        dgq_ref[...] += _bcast_rows(jnp.sum(dcqn * cqh, axis=0, keepdims=True), 8)
        dgk_ref[...] += _bcast_rows(jnp.sum(dckvn * ckh, axis=0, keepdims=True), 8)
        tq = dcqn * gq_ref[...]
        tk = dckvn * gk_ref[...]
        dcq = rq * (tq - cqh * jnp.mean(tq * cqh, axis=-1, keepdims=True))
        dckv = rk * (tk - ckh * jnp.mean(tk * ckh, axis=-1, keepdims=True))
        dsm_ref[:, :Q_LORA] = _mx(dcq)
        dsm_ref[:, Q_LORA:Q_LORA + KV_LORA] = _mx(dckv)
        dsm_ref[:, Q_LORA + KV_LORA:Q_LORA + KV_LORA + HEAD_DIM] = _mx(dkr)
        dsm_ref[:, Q_LORA + KV_LORA + HEAD_DIM:] = jnp.zeros((tm, SEG - Q_LORA - KV_LORA - HEAD_DIM), MXU_DTYPE)

    def const(shape):
        return pl.BlockSpec(shape, lambda i: (0, 0))

    def rows(w):
        return pl.BlockSpec((tm, w), lambda i: (i, 0))

    return pl.pallas_call(
        body,
        name="mla_prep_bwd",
        grid=(s // tm,),
        in_specs=[
            rows(HEADS * HEAD_PAD), rows(HEADS * HEAD_PAD), rows(D_MODEL),
            pl.BlockSpec((tm, SEG), lambda i: (i, SMALL_SEG)),
            const((1, Q_LORA)), const((1, KV_LORA)),
            const((Q_LORA, HEADS * HEAD_PAD)), const((KV_LORA, D_MODEL)), const((KV_LORA, D_MODEL)),
            rows(HEAD_DIM), rows(HEAD_DIM), rows(HEAD_DIM),
        ],
        out_specs=[rows(SEG), rows(HEADS * HEAD_PAD), rows(D_MODEL), rows(D_MODEL),
                   const((8, Q_LORA)), const((8, KV_LORA))],
        out_shape=[
            jax.ShapeDtypeStruct((s, SEG), MXU_DTYPE),
            jax.ShapeDtypeStruct((s, HEADS * HEAD_PAD), MXU_DTYPE),
            jax.ShapeDtypeStruct((s, D_MODEL), MXU_DTYPE),
            jax.ShapeDtypeStruct((s, D_MODEL), MXU_DTYPE),
            jax.ShapeDtypeStruct((8, Q_LORA), F32),
            jax.ShapeDtypeStruct((8, KV_LORA), F32),
        ],
        compiler_params=_params(),
    )(dq_all, dk_all, dv_all, proj, q_a_g, kv_a_g, w_uq_p, w_kn, w_v, cos, sin_a, sin_b)


def _hgrn_bwd(proj, lb_logits, hg_norm_g, o_all, dya, states):
    s = proj.shape[0]
    t = min(s, T_HGRN)
    nb = s // t
    nc = t // HG_CHUNK

    def body(hq_ref, hf_ref, hi_ref, hz_ref, lb_ref, g_ref, o_ref, dya_ref, st_ref,
             dhq_ref, dhf_ref, dhi_ref, dhz_ref, dlb_ref, dg_ref, dstate, u_sc, g_sc, stf_sc, stb_sc, dstb_sc):
        h, b = pl.program_id(0), pl.program_id(1)

        @pl.when(b == 0)
        def _():
            dstate[...] = jnp.zeros_like(dstate)
            dlb_ref[...] = jnp.zeros_like(dlb_ref)

        @pl.when((b == 0) & (h == 0))
        def _():
            dg_ref[...] = jnp.zeros_like(dg_ref)

        lower = _chunk_lower_mask(t)
        pos = _chunk_pos(t)
        ghg = g_ref[...]
        for hh in range(HG_HEADS_PER_STEP):
            cols = slice(hh * HEAD_DIM, (hh + 1) * HEAD_DIM)
            hq, hf, hz = hq_ref[:, cols], hf_ref[:, cols], hz_ref[:, cols]
            gt = _hgrn_gates(hq, hf, lb_ref[:, cols], pos)
            vb = _mx(hi_ref[:, cols])
            qi, ki, ko = gt["qi"], gt["ki"], gt["ko"]
            qib, kib, kob = _mx(qi), _mx(ki), _mx(ko)

            o = o_ref[:, cols]
            sz = _sigmoid(hz)
            r = lax.rsqrt(jnp.mean(o * o, axis=-1, keepdims=True) + EPS)
            on = o * r
            dya_t = dya_ref[:, cols]
            don = dya_t * (hz * sz)
            dhz_ref[:, cols] = _mx(dya_t * (on * ghg) * (sz + hz * sz * (1.0 - sz)))
            dg_ref[...] += _bcast_rows(jnp.sum(don * on, axis=0, keepdims=True), 8)
            tt = don * ghg
            do = r * (tt - on * jnp.mean(tt * on, axis=-1, keepdims=True))
            dob = _mx(do)

            for c in range(nc):
                sl = slice(c * HG_CHUNK, (c + 1) * HG_CHUNK)
                u_sc[hh, c] = _dot_tn(vb[sl], kob[sl])
                g_sc[hh, c] = _dot_tn(dob[sl], qib[sl])

            st = st_ref[0, hh]
            for c in range(nc):
                stf_sc[hh, c] = st
                stb_sc[hh, c] = _mx(st)
                if c < nc - 1:
                    st = st * gt["dec"][c * HG_CHUNK:c * HG_CHUNK + 1, :] + u_sc[hh, c]

            dst = dstate[hh]
            dd_parts = [None] * nc
            for c in reversed(range(nc)):
                dec = gt["dec"][c * HG_CHUNK:c * HG_CHUNK + 1, :]
                dstb_sc[hh, c] = _mx(dst)
                dd_parts[c] = _bcast_rows(jnp.sum(dst * stf_sc[hh, c], axis=0, keepdims=True) * dec, HG_CHUNK)
                dst = dst * dec + g_sc[hh, c]
            dstate[hh] = dst

            a = jnp.where(lower, _dot_nt(qib, kib), 0.0)
            da = _mx(jnp.where(lower, _dot_nt(dob, vb), 0.0))
            dqi_intra = _dot(da, kib)
            dki = _dot_tn(da, qib)
            dv_intra = _dot_tn(_mx(a), dob)

            dqi_parts, dko_parts, dv_parts = [None] * nc, [None] * nc, [None] * nc
            for c in range(nc):
                sl = slice(c * HG_CHUNK, (c + 1) * HG_CHUNK)
                dv_parts[c] = dv_intra[sl] + _dot_nt(kob[sl], dstb_sc[hh, c])
                dko_parts[c] = _dot(vb[sl], dstb_sc[hh, c])
                dqi_parts[c] = dqi_intra[sl] + _dot(dob[sl], stb_sc[hh, c])
            dqi = jnp.concatenate(dqi_parts, axis=0)
            dko = jnp.concatenate(dko_parts, axis=0)
            dv = jnp.concatenate(dv_parts, axis=0)
            dd = jnp.concatenate(dd_parts, axis=0)

            dq = dqi * gt["eb"]
            dk = dki * gt["enb"] + dko * gt["eo"]
            db = dqi * qi - dki * ki - dko * ko
            dlogf = _rcumsum_chunk(db, pos) + _chunk_total(dko * ko) + dd
            df = dlogf / gt["f"] - dk
            lb, sig, sq = gt["lb"], gt["sig"], gt["sq"]
            dhf_ref[:, cols] = _mx(df * (1.0 - lb) * (sig * (1.0 - sig)))
            dhq_ref[:, cols] = _mx(dq * (sq + hq * sq * (1.0 - sq)))
            dhi_ref[:, cols] = _mx(dv)
            dlb = jnp.sum(df * (1.0 - sig), axis=0, keepdims=True) * (lb * (1.0 - lb))
            dlb_ref[:, cols] += jnp.concatenate([dlb, -dlb], axis=0)

    hw = HG_HEADS_PER_STEP * HEAD_DIM
    hsteps = HEADS // HG_HEADS_PER_STEP

    def seg(k):
        return pl.BlockSpec((t, hw), lambda h, b, k=k: (nb - 1 - b, k * hsteps + h))

    blk = pl.BlockSpec((t, hw), lambda h, b: (nb - 1 - b, h))
    b16 = jax.ShapeDtypeStruct((s, D_MODEL), MXU_DTYPE)
    return pl.pallas_call(
        body,
        name="hgrn_bwd",
        grid=(hsteps, nb),
        in_specs=[seg(0), seg(1), seg(2), seg(3),
                  pl.BlockSpec((2, hw), lambda h, b: (0, h)),
                  pl.BlockSpec((1, HEAD_DIM), lambda h, b: (0, 0)),
                  blk, blk,
                  pl.BlockSpec((1, HG_HEADS_PER_STEP, HEAD_DIM, HEAD_DIM), lambda h, b: (nb - 1 - b, h, 0, 0))],
        out_specs=[blk, blk, blk, blk,
                   pl.BlockSpec((2, hw), lambda h, b: (0, h)),
                   pl.BlockSpec((8, HEAD_DIM), lambda h, b: (0, 0))],
        out_shape=[b16, b16, b16, b16,
                   jax.ShapeDtypeStruct((2, D_MODEL), F32),
                   jax.ShapeDtypeStruct((8, HEAD_DIM), F32)],
        scratch_shapes=[pltpu.VMEM((HG_HEADS_PER_STEP, HEAD_DIM, HEAD_DIM), F32)]
        + [pltpu.VMEM((HG_HEADS_PER_STEP, nc, HEAD_DIM, HEAD_DIM), F32)] * 3
        + [pltpu.VMEM((HG_HEADS_PER_STEP, nc, HEAD_DIM, HEAD_DIM), MXU_DTYPE)] * 2,
        compiler_params=_params(),
    )(proj, proj, proj, proj, lb_logits, hg_norm_g, o_all, dya, states)


def _dh_bwd(segs, w_in_p, x, dx2, norm_g, late_slab, late_recv_init, misc_slabs):
    s = x.shape[0]
    tm = min(s, TM_ROW)
    nseg = len(segs)
    nsteps = s // tm
    late_xyc = ((LATE_DEV >> 2) & 1, (LATE_DEV >> 1) & 1, LATE_DEV & 1)

    def body(*refs):
        seg_refs = refs[:nseg]
        (w_ref, x_ref, dx2_ref, g_ref, late_ref, _, misc_ref,
         gx_ref, dng_ref, late_recv_ref, misc_recv_ref,
         dp_buf, send_sems, recv_sems, local_sems, late_send, late_recvs, late_local) = refs[nseg:]
        i = pl.program_id(0)
        me = 4 * lax.axis_index("x") + 2 * lax.axis_index("y") + lax.axis_index("c")

        def misc_exchange():
            return _Exchange([misc_ref], [misc_recv_ref], send_sems, recv_sems, local_sems)

        def late_copy(sender):
            return pltpu.make_async_remote_copy(
                src_ref=late_ref.at[0], dst_ref=late_recv_ref.at[sender], send_sem=late_send,
                recv_sem=late_recvs.at[(sender ^ LATE_DEV) - 1], device_id=late_xyc, device_id_type=MESH)

        def late_own():
            return pltpu.make_async_copy(late_ref.at[0], late_recv_ref.at[LATE_DEV], late_local)

        @pl.when(i == 0)
        def _():
            dng_ref[...] = jnp.zeros_like(dng_ref)
            misc_exchange().start()

        @pl.when((i == 0) & (me != LATE_DEV))
        def _():
            late_copy(me).start()

        @pl.when((i == 0) & (me == LATE_DEV))
        def _():
            late_own().start()

        for k, sref in enumerate(seg_refs):
            dp_buf[:, k * SEG:(k + 1) * SEG] = sref[...]
        dh = _dot_nt(dp_buf[...], w_ref[...])
        xf = x_ref[...]
        r = lax.rsqrt(jnp.mean(xf * xf, axis=-1, keepdims=True) + EPS)
        xh = xf * r
        dng_ref[...] += _bcast_rows(jnp.sum(dh * xh, axis=0, keepdims=True), 8)
        tt = dh * g_ref[...]
        gx_ref[...] = dx2_ref[...] + r * (tt - xh * jnp.mean(tt * xh, axis=-1, keepdims=True))

        @pl.when(i == nsteps - 1)
        def _():
            misc_exchange().wait()

        @pl.when((i == nsteps - 1) & (me != LATE_DEV))
        def _():
            late_copy(me).wait_send()

        @pl.when((i == nsteps - 1) & (me == LATE_DEV))
        def _():
            for k in range(1, N_DEV):
                late_copy(LATE_DEV ^ k).wait_recv()
            late_own().wait()

    rows = pl.BlockSpec((tm, D_MODEL), lambda i: (i, 0))
    return pl.pallas_call(
        body,
        name="dh_bwd",
        grid=(nsteps,),
        in_specs=[pl.BlockSpec((tm, SEG), lambda i: (i, 0))] * nseg + [
            _resident((D_MODEL, PROJ_W)),
            rows, rows,
            pl.BlockSpec((1, D_MODEL), lambda i: (0, 0)),
            HBM_SPEC, HBM_SPEC, HBM_SPEC,
        ],
        out_specs=[rows, pl.BlockSpec((8, D_MODEL), lambda i: (0, 0)), HBM_SPEC, HBM_SPEC],
        out_shape=[jax.ShapeDtypeStruct((s, D_MODEL), F32), jax.ShapeDtypeStruct((8, D_MODEL), F32),
                   jax.ShapeDtypeStruct(late_recv_init.shape, late_recv_init.dtype),
                   jax.ShapeDtypeStruct(misc_slabs.shape, misc_slabs.dtype)],
        input_output_aliases={nseg + 5: 2},
        scratch_shapes=[pltpu.VMEM((tm, PROJ_W), MXU_DTYPE)] + _Exchange.semaphores(1)
        + [pltpu.SemaphoreType.DMA, pltpu.SemaphoreType.DMA((N_DEV - 1,)), pltpu.SemaphoreType.DMA],
        compiler_params=_params(),
    )(*segs, w_in_p, x, dx2, norm_g, late_slab, late_recv_init, misc_slabs)


def _matmul_tn(a, b, name, out_dtype=F32):
    s, m = a.shape
    n = b.shape[1]
    ts = min(s, TS_TN)
    tn = min(n, SEG)
    nk = s // ts

    def body(a_ref, b_ref, o_ref, acc):
        k = pl.program_id(1)
        part = _dot_tn(a_ref[...], b_ref[...])

        @pl.when(k == 0)
        def _():
            acc[...] = part

        @pl.when(k > 0)
        def _():
            acc[...] += part

        @pl.when(k == nk - 1)
        def _():
            o_ref[...] = acc[...].astype(out_dtype)

    return pl.pallas_call(
        body,
        name=name,
        grid=(n // tn, nk),
        in_specs=[pl.BlockSpec((ts, m), lambda j, k: (k, 0)), pl.BlockSpec((ts, tn), lambda j, k: (k, j))],
        out_specs=pl.BlockSpec((m, tn), lambda j, k: (0, j)),
        out_shape=jax.ShapeDtypeStruct((m, n), out_dtype),
        scratch_shapes=[pltpu.VMEM((m, tn), F32)],
        compiler_params=_params(),
    )(a, b)


def _w_in_pieces():
    per = IN_COLS // N_DEV
    pad_at = SMALL_SEG * SEG + Q_LORA + KV_LORA + QK_ROPE
    pieces = []
    for j in range(N_DEV):
        u0, u1 = j * per, (j + 1) * per
        cuts = [u0] + ([pad_at] if u0 < pad_at < u1 else []) + [u1]
        for a, b in zip(cuts[:-1], cuts[1:]):
            pieces.append((j, a - u0, b - u0, a if a < pad_at else a + PROJ_W - IN_COLS))
    return pad_at, pieces


def _assemble_w_in(gathered):
    tr = TM_ROW
    pad_at, pieces = _w_in_pieces()

    def body(in_ref, out_ref):
        out_ref[:, pad_at:pad_at + PROJ_W - IN_COLS] = jnp.zeros((tr, PROJ_W - IN_COLS), gathered.dtype)
        for j, a, b, p0 in pieces:
            out_ref[:, p0:p0 + b - a] = in_ref[j, :, a:b]

    return pl.pallas_call(
        body,
        name="assemble_w_in",
        grid=(D_MODEL // tr,),
        in_specs=[pl.BlockSpec((N_DEV, tr, PACK_COLS), lambda i: (0, i, 0))],
        out_specs=pl.BlockSpec((tr, PROJ_W), lambda i: (i, 0)),
        out_shape=jax.ShapeDtypeStruct((D_MODEL, PROJ_W), gathered.dtype),
        compiler_params=_params(),
    )(gathered)


def _scatter_dw_in(dw_segs, devs, name):
    tr = TM_ROW
    _, pieces = _w_in_pieces()
    per = IN_COLS // N_DEV
    seg_ids = sorted(dw_segs)
    nseg = len(seg_ids)

    def body(*refs):
        out_ref, buf = refs[nseg:]
        for k in range(PROJ_W // SEG):
            if k in seg_ids:
                buf[:, k * SEG:(k + 1) * SEG] = refs[seg_ids.index(k)][...]
            else:
                buf[:, k * SEG:(k + 1) * SEG] = jnp.zeros((tr, SEG), F32)
        for slot, dev in enumerate(devs):
            out_ref[slot, :, per:] = jnp.zeros((tr, PACK_COLS - per), TRANSPORT_DTYPE)
            for j, a, b, p0 in pieces:
                if j == dev:
                    out_ref[slot, :, a:b] = buf[:, p0:p0 + b - a].astype(TRANSPORT_DTYPE)

    return pl.pallas_call(
        body,
        name=name,
        grid=(D_MODEL // tr,),
        in_specs=[pl.BlockSpec((tr, SEG), lambda i: (i, 0))] * nseg,
        out_specs=pl.BlockSpec((len(devs), tr, PACK_COLS), lambda i: (0, i, 0)),
        out_shape=jax.ShapeDtypeStruct((len(devs), D_MODEL, PACK_COLS), TRANSPORT_DTYPE),
        scratch_shapes=[pltpu.VMEM((tr, PROJ_W), F32)],
        compiler_params=_params(),
    )(*[dw_segs[k] for k in seg_ids])


def _rope_tables(s):
    inv = ROPE_THETA ** (-jnp.arange(0, QK_ROPE, 2, dtype=F32) / QK_ROPE)
    ang = jnp.arange(s, dtype=F32)[:, None] * inv[None, :]
    cos, sin = jnp.cos(ang), jnp.sin(ang)
    z32 = jnp.zeros_like(cos)
    z64 = jnp.zeros((s, HEAD_DIM - QK_ROPE), F32)
    cos_t = jnp.concatenate([cos, cos, z64], axis=1)
    sin_a = jnp.concatenate([-sin, z32, z64], axis=1)
    sin_b = jnp.concatenate([z32, sin, z64], axis=1)
    return cos_t, sin_a, sin_b


def _pack_misc(w_uq, w_ukv, norm_g, b_gate, lb_logits, hg_norm_g, q_a_g, kv_a_g, final_norm_g, extra):
    misc = jnp.concatenate([hg_norm_g.reshape(-1), q_a_g.reshape(-1), kv_a_g.reshape(-1), extra.reshape(-1),
                            jnp.zeros((PACK_COLS - HEAD_DIM - Q_LORA - KV_LORA - 1,), F32)])
    return jnp.concatenate([w_uq.reshape(ROWS_W_UQ, PACK_COLS), w_ukv.reshape(ROWS_W_UKV, PACK_COLS),
                            norm_g.reshape(1, -1), b_gate.reshape(2, -1), lb_logits.reshape(2, -1),
                            misc.reshape(1, -1), final_norm_g.reshape(1, -1), jnp.zeros((1, PACK_COLS), F32)], axis=0)


def _unpack_misc(p):
    sm = p[ROWS_W_UQ + ROWS_W_UKV:]
    misc = sm[5]
    return dict(
        w_uq=p[:ROWS_W_UQ].reshape(1, Q_LORA, QK_DIM),
        w_ukv=p[ROWS_W_UQ:ROWS_W_UQ + ROWS_W_UKV].reshape(1, KV_LORA, 2 * HEAD_DIM),
        norm_g=sm[0:1], b_gate=sm[1:3].reshape(1, -1), lb_logits=sm[3:5],
        hg_norm_g=misc[None, :HEAD_DIM], q_a_g=misc[None, HEAD_DIM:HEAD_DIM + Q_LORA],
        kv_a_g=misc[None, HEAD_DIM + Q_LORA:HEAD_DIM + Q_LORA + KV_LORA], final_norm_g=sm[6],
        extra=misc[HEAD_DIM + Q_LORA + KV_LORA],
    )


def _weight_shard_buffers(w_in, w_uq, w_ukv, w_pa, w_pb, w_out):
    w_in_pad = jnp.pad(w_in.reshape(D_MODEL, -1), ((0, 0), (0, PACK_COLS - IN_COLS // N_DEV)))
    parts = [a.reshape(-1, PACK_COLS) for a in (w_pa, w_pb, w_out, w_uq, w_ukv)]
    others = jnp.concatenate(parts + [jnp.zeros((ROWS_OTHER - ROWS_OTHER_USED, PACK_COLS), F32)], axis=0)
    return w_in_pad.astype(MXU_DTYPE), others.astype(MXU_DTYPE)


def _other_weights(gathered):
    r0 = 0
    mats = []
    for _ in range(3):
        mats.append(gathered[:, r0:r0 + ROWS_W_PROJ].reshape(D_MODEL, D_MODEL))
        r0 += ROWS_W_PROJ
    w_uq = gathered[:, r0:r0 + ROWS_W_UQ].reshape(N_DEV, Q_LORA, QK_DIM).transpose(1, 0, 2)
    w_uq_p = jnp.concatenate([w_uq, jnp.zeros((Q_LORA, HEADS, HEAD_PAD - QK_DIM), w_uq.dtype)], axis=2)
    w_uq_p = w_uq_p.reshape(Q_LORA, HEADS * HEAD_PAD)
    r0 += ROWS_W_UQ
    w_ukv = gathered[:, r0:r0 + ROWS_W_UKV].reshape(N_DEV, KV_LORA, 2 * HEAD_DIM).transpose(1, 0, 2)
    w_kn = w_ukv[:, :, :HEAD_DIM].reshape(KV_LORA, D_MODEL)
    w_v = w_ukv[:, :, HEAD_DIM:].reshape(KV_LORA, D_MODEL)
    return w_uq_p, w_kn, w_v, mats[0], mats[1], mats[2]


def _misc_slabs(dw_uq_p, dw_kn, dw_v, small):
    dw_uq = dw_uq_p.reshape(Q_LORA, HEADS, HEAD_PAD)[:, :, :QK_DIM].transpose(1, 0, 2)
    dw_ukv = jnp.concatenate([dw_kn.reshape(KV_LORA, HEADS, HEAD_DIM),
                              dw_v.reshape(KV_LORA, HEADS, HEAD_DIM)], axis=2).transpose(1, 0, 2)
    return jnp.stack([_pack_misc(dw_uq[j], dw_ukv[j], *small) for j in range(N_DEV)])


def _step_gradients(x, target, norm_g, b_gate, lb_logits, hg_norm_g, q_a_g, kv_a_g, final_g,
                    w_in_p, other_shard):
    s = x.shape[0]
    cos, sin_a, sin_b = _rope_tables(s)
    proj, h, gathered = _inproj(x, norm_g, w_in_p, other_shard)
    w_uq_p, w_kn, w_v, w_pa, w_pb, w_out = _other_weights(gathered)
    o_all, ya, states = _hgrn_fwd(proj, lb_logits, hg_norm_g)
    q_all, k_all, v_all, cqn, ckvn = _mla_prep(proj, q_a_g, kv_a_g, w_uq_p, w_kn, w_v, cos, sin_a, sin_b)
    ao, yb, mblk, p_all = _flash_fwd(q_all, k_all, v_all, proj)
    (dx2, dya, dao, dmz, dg0, dg1, merged_b, dpa_b, dpb_b, dx2_b,
     loss_acc, dfg_acc, dbg_acc) = _merge_fwd_bwd(x, target, ya, yb, ao, proj, b_gate, final_g, w_pa, w_pb, w_out)
    dhq, dhf, dhi, dhz, dlb, dhg_acc = _hgrn_bwd(proj, lb_logits, hg_norm_g, o_all, dya, states)

    early = {0: dhq, 1: dhf, 2: dhi, 3: dhz, MZ_SEG: dmz, GL_SEG: dg0, GL_SEG + 1: dg1}
    dw_early = {k: _matmul_tn(h, sg, "dw_in_%d" % k) for k, sg in early.items()}
    mats = [_matmul_tn(a, b, name, TRANSPORT_DTYPE).reshape(N_DEV, ROWS_W_PROJ, PACK_COLS)
            for a, b, name in ((ya, dpa_b, "dw_pa"), (yb, dpb_b, "dw_pb"), (merged_b, dx2_b, "dw_out"))]
    early_slabs = [_scatter_dw_in(dw_early, list(range(N_DEV)), "scatter_dw_in")] + mats
    dq_all, dk_all, dv_all, (recv_in, recv_pa, recv_pb, recv_out) = _flash_bwd(
        q_all, k_all, v_all, dao, ao, mblk, p_all, early_slabs)

    dsmall, dqf_b, dkn_b, dv_b, dgq_acc, dgk_acc = _mla_prep_bwd(
        dq_all, dk_all, dv_all, proj, q_a_g, kv_a_g, w_uq_p, w_kn, w_v, cos, sin_a, sin_b)
    late_slab = _scatter_dw_in({SMALL_SEG: _matmul_tn(h, dsmall, "dw_in_%d" % SMALL_SEG)}, [LATE_DEV],
                               "scatter_dw_in_late")
    segs = [dhq, dhf, dhi, dhz, dsmall, dmz, dg0, dg1]
    return dict(
        segs=segs, h=h, dx2=dx2, late_slab=late_slab,
        dw_uq_p=_matmul_tn(cqn, dqf_b, "dw_uq"), dw_kn=_matmul_tn(ckvn, dkn_b, "dw_kn"),
        dw_v=_matmul_tn(ckvn, dv_b, "dw_v"),
        small=dict(b_gate=dbg_acc[0:1], lb_logits=dlb, hg_norm_g=dhg_acc[0:1], q_a_g=dgq_acc[0:1],
                   kv_a_g=dgk_acc[0:1], final_norm_g=dfg_acc[0], loss=loss_acc[0, 0]),
        recv=dict(w_in=[recv_in], w_pa=[recv_pa], w_pb=[recv_pb], w_out=[recv_out]),
    )


def kernel(x, norm_g, w_in, b_gate, lb_logits, hg_norm_g, q_a_g, w_uq, kv_a_g, w_ukv, w_proj_a, w_proj_b, w_out, final_norm_g, loss_target, m_norm_g, m_w_in, m_b_gate, m_lb_logits, m_hg_norm_g, m_q_a_g, m_w_uq, m_kv_a_g, m_w_ukv, m_w_proj_a, m_w_proj_b, m_w_out, m_final_norm_g, v_norm_g, v_w_in, v_b_gate, v_lb_logits, v_hg_norm_g, v_q_a_g, v_w_uq, v_kv_a_g, v_w_ukv, v_w_proj_a, v_w_proj_b, v_w_out, v_final_norm_g):
    zero = jnp.zeros((1,), F32)
    xs = x[0]
    w_in_shard, other_shard = _weight_shard_buffers(w_in, w_uq, w_ukv, w_proj_a, w_proj_b, w_out)
    w_in_p = _assemble_w_in(_all_gather_packed(w_in_shard))
    g = _step_gradients(xs, loss_target[0], norm_g, b_gate, lb_logits, hg_norm_g, q_a_g, kv_a_g,
                        final_norm_g.reshape(1, -1), w_in_p, other_shard)
    sm = g["small"]
    misc_slabs = _misc_slabs(g["dw_uq_p"], g["dw_kn"], g["dw_v"],
                             (jnp.zeros_like(norm_g), sm["b_gate"], sm["lb_logits"], sm["hg_norm_g"], sm["q_a_g"],
                              sm["kv_a_g"], sm["final_norm_g"], sm["loss"]))
    late_recv_init = jnp.zeros((N_DEV, D_MODEL, PACK_COLS), TRANSPORT_DTYPE)
    grad_x, dng_acc, recv_late, recv_misc = _dh_bwd(g["segs"], w_in_p, xs, g["dx2"], norm_g,
                                                    g["late_slab"], late_recv_init, misc_slabs)
    recv_ng = _exchange_rows(jnp.broadcast_to(dng_acc[None], (N_DEV, 8, D_MODEL)))

    recv = g["recv"]
    out_in = _sum_adamw(recv["w_in"] + [recv_late], w_in[0], m_w_in[0], v_w_in[0], "adamw_w_in")
    out_pa = _sum_adamw(recv["w_pa"], w_proj_a[0], m_w_proj_a[0], v_w_proj_a[0], "adamw_w_pa")
    out_pb = _sum_adamw(recv["w_pb"], w_proj_b[0], m_w_proj_b[0], v_w_proj_b[0], "adamw_w_pb")
    out_out = _sum_adamw(recv["w_out"], w_out[0], m_w_out[0], v_w_out[0], "adamw_w_out")
    out_ng = _sum_adamw([recv_ng], *[jnp.broadcast_to(a, (8, D_MODEL)) for a in (norm_g, m_norm_g, v_norm_g)],
                        "adamw_norm_g")
    out_misc = _sum_adamw(
        [recv_misc],
        _pack_misc(w_uq, w_ukv, norm_g, b_gate, lb_logits, hg_norm_g, q_a_g, kv_a_g, final_norm_g, zero),
        _pack_misc(m_w_uq, m_w_ukv, m_norm_g, m_b_gate, m_lb_logits, m_hg_norm_g, m_q_a_g, m_kv_a_g,
                   m_final_norm_g, zero),
        _pack_misc(v_w_uq, v_w_ukv, v_norm_g, v_b_gate, v_lb_logits, v_hg_norm_g, v_q_a_g, v_kv_a_g,
                   v_final_norm_g, zero),
        "adamw_misc")
    names = ["norm_g", "w_in", "b_gate", "lb_logits", "hg_norm_g", "q_a_g", "w_uq", "kv_a_g", "w_ukv",
             "w_proj_a", "w_proj_b", "w_out", "final_norm_g"]
    kinds = []
    for i in range(4):
        d = _unpack_misc(out_misc[i])
        d.update(w_in=out_in[i][None], w_proj_a=out_pa[i][None], w_proj_b=out_pb[i][None], w_out=out_out[i][None],
                 norm_g=out_ng[i][0:1])
        kinds.append(d)
    return (kinds[0]["extra"], grad_x[None], *[d[n] for d in kinds for n in names])
```

```python
import functools

import jax
import jax.numpy as jnp
from jax import lax
from jax.experimental import pallas as pl
from jax.experimental.pallas import tpu as pltpu

D_MODEL = 1024
HEADS = 8
HEAD_DIM = 128
HG_CHUNK = 32
Q_LORA = 384
KV_LORA = 256
QK_ROPE = 64
QK_DIM = 192
ROPE_THETA = 10000.0
EPS = 1e-6
IN_COLS = 7872
ADAM_LR = 0.001
ADAM_B1 = 0.9
ADAM_B2 = 0.999
ADAM_EPS = 1e-08
ADAM_WD = 0.01
ADAM_STEP = 10

N_DEV = 8
SEG = 1024
PROJ_W = 8 * SEG
SMALL_SEG = 4
MZ_SEG = 5
GL_SEG = 6
HEAD_PAD = 256
PACK_COLS = 1024
ROWS_W_UQ = 72
ROWS_W_UKV = 64
ROWS_W_PROJ = 128
ROWS_OTHER_USED = 3 * ROWS_W_PROJ + ROWS_W_UQ + ROWS_W_UKV
ROWS_OTHER = 528
LATE_DEV = (SMALL_SEG * SEG) // (IN_COLS // N_DEV)
assert (SMALL_SEG * SEG + Q_LORA + KV_LORA + QK_ROPE - 1) // (IN_COLS // N_DEV) == LATE_DEV

QK_SCALE = QK_DIM ** -0.5
LOG2E = 1.4426950408889634
LN2 = 0.6931471805599453
Q_PRESCALE = QK_SCALE * LOG2E

MXU_DTYPE = jnp.bfloat16
TRANSPORT_DTYPE = jnp.bfloat16
VMEM_LIMIT = 48 * 1024 * 1024
VMEM_LIMIT_BIG = 60 * 1024 * 1024

T_HGRN = 256
HG_HEADS_PER_STEP = 4
TM_ROW = 256
T_ATT = 1024
T_ATT_BWD = T_ATT
ATT_SUB = 4
P_SLOTS = 4
LSE_LANE = 127
ATT_SUB_BWD = 2
TS_TN = 2048
TR_ADAM = 256

F32 = jnp.float32
MESH = pl.DeviceIdType.MESH


def _dot(a, b):
    return jnp.dot(a, b, preferred_element_type=F32)


def _dot_nt(a, b):
    return lax.dot_general(a, b, (((1,), (1,)), ((), ())), preferred_element_type=F32)


def _dot_tn(a, b):
    return lax.dot_general(a, b, (((0,), (0,)), ((), ())), preferred_element_type=F32)


def _mx(a):
    return a.astype(MXU_DTYPE)


def _sigmoid(x):
    return 1.0 / (1.0 + jnp.exp(-x))


def _params(vmem=VMEM_LIMIT, **kw):
    return pltpu.CompilerParams(vmem_limit_bytes=vmem, **kw)


def _bcast_rows(row, n):
    return jnp.broadcast_to(row, (n, row.shape[-1]))


def _resident(shape):
    return pl.BlockSpec(shape, lambda *_: (0, 0), pipeline_mode=pl.Buffered(1))


HBM_SPEC = pl.BlockSpec(memory_space=pltpu.HBM)


def _all_gather_packed(shard):
    rows, cols = shard.shape

    def body(x_ref, out_ref, send_sems, recv_sems, local_sem):
        x, y, c = lax.axis_index("x"), lax.axis_index("y"), lax.axis_index("c")
        me, sibling = (x, y, c), (x, y, 1 - c)
        chips = [(1 - x, y), (x, 1 - y), (1 - x, 1 - y)]

        def slot(px, py, pc):
            return out_ref.at[4 * px + 2 * py + pc]

        def copy(k, block, to, src=None):
            return pltpu.make_async_remote_copy(
                src_ref=slot(*block) if src is None else src,
                dst_ref=slot(*block),
                send_sem=send_sems.at[k],
                recv_sem=recv_sems.at[k],
                device_id=to,
                device_id_type=MESH,
            )

        mine = pltpu.make_async_copy(x_ref, slot(*me), local_sem)
        mine.start()
        first = [copy(0, me, sibling, src=x_ref)]
        first += [copy(1 + j, me, (*chip, c), src=x_ref) for j, chip in enumerate(chips)]
        for cp in first:
            cp.start()
        passed = [copy(4 + j, (*chip, c), sibling) for j, chip in enumerate(chips)]
        for j, chip in enumerate(chips):
            copy(1 + j, (*chip, c), me).wait_recv()
            passed[j].start()
        copy(0, sibling, me).wait_recv()
        for j, chip in enumerate(chips):
            copy(4 + j, (*chip, 1 - c), me).wait_recv()
        for cp in first + passed:
            cp.wait_send()
        mine.wait()

    return pl.pallas_call(
        body,
        name="ag_weights",
        out_shape=jax.ShapeDtypeStruct((N_DEV, rows, cols), shard.dtype),
        in_specs=[HBM_SPEC],
        out_specs=HBM_SPEC,
        scratch_shapes=[
            pltpu.SemaphoreType.DMA((7,)),
            pltpu.SemaphoreType.DMA((7,)),
            pltpu.SemaphoreType.DMA,
        ],
    )(shard)


class _Exchange:
    def __init__(self, g_refs, recv_refs, send_sems, recv_sems, local_sems, gather=False):
        x, y, c = lax.axis_index("x"), lax.axis_index("y"), lax.axis_index("c")
        me = 4 * x + 2 * y + c
        n_ops = len(g_refs)

        def source(i, dest):
            return g_refs[i] if gather else g_refs[i].at[dest]

        def copy(i, k, landing):
            px, py, pc = x ^ ((k >> 2) & 1), y ^ ((k >> 1) & 1), c ^ (k & 1)
            peer = 4 * px + 2 * py + pc
            return pltpu.make_async_remote_copy(
                src_ref=source(i, peer),
                dst_ref=recv_refs[i].at[peer if landing else me],
                send_sem=send_sems.at[i * (N_DEV - 1) + k - 1],
                recv_sem=recv_sems.at[i * (N_DEV - 1) + k - 1],
                device_id=(px, py, pc),
                device_id_type=MESH,
            )

        pairs = [(i, k) for i in range(n_ops) for k in range(1, N_DEV)]
        self.mine = lambda: [pltpu.make_async_copy(source(i, me), recv_refs[i].at[me], local_sems.at[i])
                             for i in range(n_ops)]
        self.sends = lambda: [copy(i, k, False) for i, k in pairs]
        self.landings = lambda: [copy(i, k, True) for i, k in pairs]

    def start(self):
        for cp in self.mine() + self.sends():
            cp.start()

    def wait(self):
        for cp in self.landings():
            cp.wait_recv()
        for cp in self.sends():
            cp.wait_send()
        for cp in self.mine():
            cp.wait()

    @staticmethod
    def semaphores(n_ops):
        return [pltpu.SemaphoreType.DMA((n_ops * (N_DEV - 1),)),
                pltpu.SemaphoreType.DMA((n_ops * (N_DEV - 1),)),
                pltpu.SemaphoreType.DMA((n_ops,))]


def _exchange_rows(slabs):
    def body(g_ref, recv_ref, send_sems, recv_sems, local_sems):
        exchange = _Exchange([g_ref], [recv_ref], send_sems, recv_sems, local_sems)
        exchange.start()
        exchange.wait()

    return pl.pallas_call(
        body,
        name="exchange_rows",
        out_shape=jax.ShapeDtypeStruct(slabs.shape, slabs.dtype),
        in_specs=[HBM_SPEC],
        out_specs=HBM_SPEC,
        scratch_shapes=_Exchange.semaphores(1),
    )(slabs)


def _sum_adamw(recvs, w, m, v, name):
    rows, cols = w.shape
    tr = min(rows, TR_ADAM)
    n_recv = len(recvs)

    def body(*refs):
        w_ref, m_ref, v_ref, g_out, d_out, m_out, v_out = refs[n_recv:]
        g = None
        for r_ref in refs[:n_recv]:
            for i in range(N_DEV):
                part = r_ref[i].astype(F32)
                g = part if g is None else g + part
        g = g[:, :cols]
        m_new = ADAM_B1 * m_ref[...] + (1.0 - ADAM_B1) * g
        v_new = ADAM_B2 * v_ref[...] + (1.0 - ADAM_B2) * (g * g)
        m_hat = m_new / (1.0 - ADAM_B1 ** ADAM_STEP)
        v_hat = v_new / (1.0 - ADAM_B2 ** ADAM_STEP)
        g_out[...] = g
        d_out[...] = -ADAM_LR * (m_hat / (jnp.sqrt(v_hat) + ADAM_EPS) + ADAM_WD * w_ref[...])
        m_out[...] = m_new
        v_out[...] = v_new

    row_spec = pl.BlockSpec((tr, cols), lambda i: (i, 0))
    shape = jax.ShapeDtypeStruct((rows, cols), F32)
    return pl.pallas_call(
        body,
        name=name,
        grid=(rows // tr,),
        in_specs=[pl.BlockSpec((N_DEV, tr, PACK_COLS), lambda i: (0, i, 0))] * n_recv + [row_spec] * 3,
        out_specs=[row_spec] * 4,
        out_shape=[shape] * 4,
        compiler_params=_params(),
    )(*recvs, w, m, v)


def _inproj(x, norm_g, w_in_p, other_shard):
    s = x.shape[0]
    tm = min(s, TM_ROW)
    nsteps = s // tm

    def body(x_ref, g_ref, w_ref, shard_ref, proj_ref, h_ref, gathered_ref, send_sems, recv_sems, local_sems):
        i = pl.program_id(0)

        def all_gather():
            return _Exchange([shard_ref], [gathered_ref], send_sems, recv_sems, local_sems, gather=True)

        @pl.when(i == 0)
        def _():
            all_gather().start()

        xf = x_ref[...]
        r = lax.rsqrt(jnp.mean(xf * xf, axis=-1, keepdims=True) + EPS)
        h = _mx(xf * r * g_ref[...])
        h_ref[...] = h
        for j in range(PROJ_W // SEG):
            cols = slice(j * SEG, (j + 1) * SEG)
            proj_ref[:, cols] = _dot(h, w_ref[:, cols])

        @pl.when(i == nsteps - 1)
        def _():
            all_gather().wait()

    return pl.pallas_call(
        body,
        name="inproj",
        grid=(nsteps,),
        in_specs=[
            pl.BlockSpec((tm, D_MODEL), lambda i: (i, 0)),
            pl.BlockSpec((1, D_MODEL), lambda i: (0, 0)),
            _resident((D_MODEL, PROJ_W)),
            HBM_SPEC,
        ],
        out_specs=[
            pl.BlockSpec((tm, PROJ_W), lambda i: (i, 0)),
            pl.BlockSpec((tm, D_MODEL), lambda i: (i, 0)),
            HBM_SPEC,
        ],
        out_shape=[
            jax.ShapeDtypeStruct((s, PROJ_W), F32),
            jax.ShapeDtypeStruct((s, D_MODEL), MXU_DTYPE),
            jax.ShapeDtypeStruct((N_DEV,) + other_shard.shape, other_shard.dtype),
        ],
        scratch_shapes=_Exchange.semaphores(1),
        compiler_params=_params(),
    )(x, norm_g, w_in_p, other_shard)


def _chunk_lower_mask(t):
    row = lax.broadcasted_iota(jnp.int32, (t, t), 0)
    col = lax.broadcasted_iota(jnp.int32, (t, t), 1)
    return ((row // HG_CHUNK) == (col // HG_CHUNK)) & (col <= row)


def _chunk_pos(t):
    return lax.broadcasted_iota(jnp.int32, (t, HEAD_DIM), 0) & (HG_CHUNK - 1)


def _cumsum_chunk(x, pos):
    sh = 1
    while sh < HG_CHUNK:
        x = x + jnp.where(pos >= sh, pltpu.roll(x, sh, 0), 0.0)
        sh *= 2
    return x


def _rcumsum_chunk(x, pos):
    t = x.shape[0]
    sh = 1
    while sh < HG_CHUNK:
        x = x + jnp.where(pos < HG_CHUNK - sh, pltpu.roll(x, t - sh, 0), 0.0)
        sh *= 2
    return x


def _chunk_total(x):
    t, w = x.shape
    tot = jnp.sum(x.reshape(t // HG_CHUNK, HG_CHUNK, w), axis=1, keepdims=True)
    return jnp.broadcast_to(tot, (t // HG_CHUNK, HG_CHUNK, w)).reshape(t, w)


def _hgrn_gates(hq, hf, lb_logits, pos):
    lb = _sigmoid(lb_logits[0:1, :] - lb_logits[1:2, :])
    sig = _sigmoid(hf)
    f = lb + (1.0 - lb) * sig
    sq = _sigmoid(hq)
    q = hq * sq
    k = 1.0 - f
    logf = jnp.log(f)
    bcum = _cumsum_chunk(logf, pos)
    blast = _chunk_total(logf)
    eb = jnp.exp(bcum)
    enb = jnp.exp(-bcum)
    eo = jnp.exp(blast - bcum)
    return dict(lb=lb, sig=sig, f=f, sq=sq, q=q, k=k, eb=eb, enb=enb, eo=eo,
                qi=q * eb, ki=k * enb, ko=k * eo, dec=jnp.exp(blast))


def _hgrn_fwd(proj, lb_logits, hg_norm_g):
    s = proj.shape[0]
    t = min(s, T_HGRN)
    nb = s // t
    nc = t // HG_CHUNK
    hw = HG_HEADS_PER_STEP * HEAD_DIM

    def body(hq_ref, hf_ref, hi_ref, hz_ref, lb_ref, g_ref, o_ref, ya_ref, st_ref, state, u_sc, stb_sc):
        b = pl.program_id(1)

        @pl.when(b == 0)
        def _():
            state[...] = jnp.zeros_like(state)

        lower = _chunk_lower_mask(t)
        pos = _chunk_pos(t)
        for hh in range(HG_HEADS_PER_STEP):
            cols = slice(hh * HEAD_DIM, (hh + 1) * HEAD_DIM)
            st = state[hh]
            st_ref[0, hh] = st
            gt = _hgrn_gates(hq_ref[:, cols], hf_ref[:, cols], lb_ref[:, cols], pos)
            vb = _mx(hi_ref[:, cols])
            qib, kib, kob = _mx(gt["qi"]), _mx(gt["ki"]), _mx(gt["ko"])
            a = jnp.where(lower, _dot_nt(qib, kib), 0.0)
            o_intra = _dot(_mx(a), vb)
            for c in range(nc):
                sl = slice(c * HG_CHUNK, (c + 1) * HG_CHUNK)
                u_sc[hh, c] = _dot_tn(vb[sl], kob[sl])
            for c in range(nc):
                stb_sc[hh, c] = _mx(st)
                st = st * gt["dec"][c * HG_CHUNK:c * HG_CHUNK + 1, :] + u_sc[hh, c]
            state[hh] = st
            outs = []
            for c in range(nc):
                sl = slice(c * HG_CHUNK, (c + 1) * HG_CHUNK)
                outs.append(o_intra[sl] + _dot_nt(qib[sl], stb_sc[hh, c]))
            o = jnp.concatenate(outs, axis=0)
            o_ref[:, cols] = o
            r = lax.rsqrt(jnp.mean(o * o, axis=-1, keepdims=True) + EPS)
            hz = hz_ref[:, cols]
            ya_ref[:, cols] = _mx((o * r * g_ref[...]) * (hz * _sigmoid(hz)))

    hsteps = HEADS // HG_HEADS_PER_STEP

    def seg(k):
        return pl.BlockSpec((t, hw), lambda h, b, k=k: (b, k * hsteps + h))

    return pl.pallas_call(
        body,
        name="hgrn_fwd",
        grid=(hsteps, nb),
        in_specs=[seg(0), seg(1), seg(2), seg(3),
                  pl.BlockSpec((2, hw), lambda h, b: (0, h)),
                  pl.BlockSpec((1, HEAD_DIM), lambda h, b: (0, 0))],
        out_specs=[
            pl.BlockSpec((t, hw), lambda h, b: (b, h)),
            pl.BlockSpec((t, hw), lambda h, b: (b, h)),
            pl.BlockSpec((1, HG_HEADS_PER_STEP, HEAD_DIM, HEAD_DIM), lambda h, b: (b, h, 0, 0)),
        ],
        out_shape=[
            jax.ShapeDtypeStruct((s, D_MODEL), F32),
            jax.ShapeDtypeStruct((s, D_MODEL), MXU_DTYPE),
            jax.ShapeDtypeStruct((nb, HEADS, HEAD_DIM, HEAD_DIM), F32),
        ],
        scratch_shapes=[pltpu.VMEM((HG_HEADS_PER_STEP, HEAD_DIM, HEAD_DIM), F32),
                        pltpu.VMEM((HG_HEADS_PER_STEP, nc, HEAD_DIM, HEAD_DIM), F32),
                        pltpu.VMEM((HG_HEADS_PER_STEP, nc, HEAD_DIM, HEAD_DIM), MXU_DTYPE)],
        compiler_params=_params(),
    )(proj, proj, proj, proj, lb_logits, hg_norm_g)


def _rope(x, cos, sin_a, sin_b):
    return x * cos + pltpu.roll(x, 96, 1) * sin_a + pltpu.roll(x, 32, 1) * sin_b


def _rope_t(d, cos, sin_a, sin_b):
    return d * cos + pltpu.roll(d * sin_a, 32, 1) + pltpu.roll(d * sin_b, 96, 1)


def _mla_prep(proj, q_a_g, kv_a_g, w_uq_p, w_kn, w_v, cos, sin_a, sin_b):
    s = proj.shape[0]
    tm = min(s, TM_ROW)

    def body(sm_ref, gq_ref, gk_ref, wq_ref, wkn_ref, wv_ref, cos_ref, sa_ref, sb_ref,
             q_ref, k_ref, v_ref, cqn_ref, ckvn_ref):
        small = sm_ref[...]
        cq = small[:, :Q_LORA]
        ckv = small[:, Q_LORA:Q_LORA + KV_LORA]
        krp = small[:, Q_LORA + KV_LORA:Q_LORA + KV_LORA + HEAD_DIM]
        rq = lax.rsqrt(jnp.mean(cq * cq, axis=-1, keepdims=True) + EPS)
        rk = lax.rsqrt(jnp.mean(ckv * ckv, axis=-1, keepdims=True) + EPS)
        cqn = _mx(cq * rq * gq_ref[...])
        ckvn = _mx(ckv * rk * gk_ref[...])
        cqn_ref[...] = cqn
        ckvn_ref[...] = ckvn
        q = _dot(cqn, wq_ref[...]) * Q_PRESCALE
        kn = _dot(ckvn, wkn_ref[...])
        v = _dot(ckvn, wv_ref[...])
        cos_t, sa, sb = cos_ref[...], sa_ref[...], sb_ref[...]
        kpe = _mx(_rope(krp, cos_t, sa, sb))
        ones_col = (lax.broadcasted_iota(jnp.int32, (tm, HEAD_DIM), 1) == 0).astype(MXU_DTYPE)
        for h in range(HEADS):
            lo = h * HEAD_PAD
            v_ref[:, lo:lo + HEAD_DIM] = _mx(v[:, h * HEAD_DIM:(h + 1) * HEAD_DIM])
            v_ref[:, lo + HEAD_DIM:lo + HEAD_PAD] = ones_col
            q_ref[:, lo:lo + HEAD_DIM] = _mx(q[:, lo:lo + HEAD_DIM])
            q_ref[:, lo + HEAD_DIM:lo + HEAD_PAD] = _mx(_rope(q[:, lo + HEAD_DIM:lo + HEAD_PAD], cos_t, sa, sb))
            k_ref[:, lo:lo + HEAD_DIM] = _mx(kn[:, h * HEAD_DIM:(h + 1) * HEAD_DIM])
            k_ref[:, lo + HEAD_DIM:lo + HEAD_PAD] = kpe

    def const(shape):
        return pl.BlockSpec(shape, lambda i: (0, 0))

    def rows(w):
        return pl.BlockSpec((tm, w), lambda i: (i, 0))

    return pl.pallas_call(
        body,
        name="mla_prep",
        grid=(s // tm,),
        in_specs=[
            pl.BlockSpec((tm, SEG), lambda i: (i, SMALL_SEG)),
            const((1, Q_LORA)), const((1, KV_LORA)),
            const((Q_LORA, HEADS * HEAD_PAD)), const((KV_LORA, D_MODEL)), const((KV_LORA, D_MODEL)),
            rows(HEAD_DIM), rows(HEAD_DIM), rows(HEAD_DIM),
        ],
        out_specs=[rows(HEADS * HEAD_PAD)] * 3 + [rows(Q_LORA), rows(KV_LORA)],
        out_shape=[
            jax.ShapeDtypeStruct((s, HEADS * HEAD_PAD), MXU_DTYPE),
            jax.ShapeDtypeStruct((s, HEADS * HEAD_PAD), MXU_DTYPE),
            jax.ShapeDtypeStruct((s, HEADS * HEAD_PAD), MXU_DTYPE),
            jax.ShapeDtypeStruct((s, Q_LORA), MXU_DTYPE),
            jax.ShapeDtypeStruct((s, KV_LORA), MXU_DTYPE),
        ],
        compiler_params=_params(),
    )(proj, q_a_g, kv_a_g, w_uq_p, w_kn, w_v, cos, sin_a, sin_b)


def _diag_mask(t):
    row = lax.broadcasted_iota(jnp.int32, (t, t), 0)
    col = lax.broadcasted_iota(jnp.int32, (t, t), 1)
    return row >= col


def _flash_fwd(q_all, k_all, v_all, proj):
    s = q_all.shape[0]
    t = min(s, T_ATT)
    n = s // t
    ts = t // ATT_SUB
    n_pairs = n * (n + 1) // 2

    def body(q_ref, k_ref, v_ref, mz_ref, ao_ref, yb_ref, mblk_ref, p_hbm, m_sc, acc_sc, stage, p_sems):
        head, qi = pl.program_id(0), pl.program_id(1)
        m_sc[...] = jnp.full_like(m_sc, -jnp.inf)
        acc_sc[...] = jnp.zeros_like(acc_sc)
        mblk_ref[...] = jnp.zeros_like(mblk_ref)
        lane = lax.broadcasted_iota(jnp.int32, (ts, HEAD_DIM), 1)

        def p_copy(slot, pair, r):
            rows = pl.ds(r * ts, ts)
            return pltpu.make_async_copy(stage.at[slot, rows], p_hbm.at[head, pair, rows], p_sems.at[slot, r])

        def p_wait(slot):
            for r in range(ATT_SUB):
                p_copy(slot, 0, r).wait()

        def key_block(ki, diagonal):
            base = pl.multiple_of(ki * t, t)
            slot = lax.rem(ki, P_SLOTS)
            sc, pb, alpha = {}, {}, {}

            @pl.when(ki >= P_SLOTS)
            def _():
                p_wait(slot)

            if diagonal:
                stage[slot] = jnp.zeros((t, t), MXU_DTYPE)

            def width(r):
                return (r + 1) * ts if diagonal else t

            def scores(r):
                w = width(r)
                s_r = _dot_nt(q_ref[r * ts:(r + 1) * ts], k_ref[pl.ds(base, w), :])
                if diagonal:
                    row = lax.broadcasted_iota(jnp.int32, (ts, w), 0) + r * ts
                    col = lax.broadcasted_iota(jnp.int32, (ts, w), 1)
                    s_r = jnp.where(row >= col, s_r, -jnp.inf)
                sc[r] = s_r

            def softmax(r):
                rs = slice(r * ts, (r + 1) * ts)
                m_prev = m_sc[rs]
                m_new = jnp.maximum(m_prev, jnp.max(sc[r], axis=-1, keepdims=True))
                pb[r] = _mx(jnp.exp2(sc[r] - m_new))
                alpha[r] = jnp.exp2(m_prev - m_new)
                m_sc[rs] = m_new
                mblk_ref[rs] = jnp.where(lane == ki, m_new, mblk_ref[rs])
                stage[slot, rs, :width(r)] = pb[r]
                p_copy(slot, qi * (qi + 1) // 2 + ki, r).start()

            def weighted_values(r):
                rs = slice(r * ts, (r + 1) * ts)
                acc_sc[rs] = alpha[r] * acc_sc[rs] + _dot(pb[r], v_ref[pl.ds(base, width(r)), :])

            for step in range(ATT_SUB + 2):
                if step < ATT_SUB:
                    scores(step)
                if 1 <= step <= ATT_SUB:
                    softmax(step - 1)
                if step >= 2:
                    weighted_values(step - 2)

        def below_diagonal(ki, carry):
            key_block(ki, False)
            return carry

        lax.fori_loop(0, qi, below_diagonal, 0)
        key_block(qi, True)

        for back in range(P_SLOTS):
            @pl.when(qi >= back)
            def _():
                p_wait(lax.rem(qi - back, P_SLOTS))

        acc = acc_sc[...]
        l = acc[:, HEAD_DIM:HEAD_DIM + 1]
        ao = acc[:, :HEAD_DIM] / l
        ao_ref[...] = ao
        lane_t = lax.broadcasted_iota(jnp.int32, (t, HEAD_DIM), 1)
        mblk_ref[...] = jnp.where(lane_t == LSE_LANE, m_sc[...] + jnp.log2(l), mblk_ref[...])
        mz = mz_ref[...]
        yb_ref[...] = _mx(ao * (mz * _sigmoid(mz)))

    q_map = lambda h, qi: (qi, h)
    return pl.pallas_call(
        body,
        name="flash_fwd",
        grid=(HEADS, n),
        in_specs=[
            pl.BlockSpec((t, HEAD_PAD), q_map),
            pl.BlockSpec((s, HEAD_PAD), lambda h, qi: (0, h)),
            pl.BlockSpec((s, HEAD_PAD), lambda h, qi: (0, h)),
            pl.BlockSpec((t, HEAD_DIM), lambda h, qi: (qi, MZ_SEG * HEADS + h)),
        ],
        out_specs=[pl.BlockSpec((t, HEAD_DIM), q_map)] * 3 + [HBM_SPEC],
        out_shape=[
            jax.ShapeDtypeStruct((s, D_MODEL), F32),
            jax.ShapeDtypeStruct((s, D_MODEL), MXU_DTYPE),
            jax.ShapeDtypeStruct((s, D_MODEL), F32),
            jax.ShapeDtypeStruct((HEADS, n_pairs, t, t), MXU_DTYPE),
        ],
        scratch_shapes=[
            pltpu.VMEM((t, 1), F32),
            pltpu.VMEM((t, HEAD_PAD), F32),
            pltpu.VMEM((P_SLOTS, t, t), MXU_DTYPE),
            pltpu.SemaphoreType.DMA((P_SLOTS, ATT_SUB)),
        ],
        compiler_params=_params(),
    )(q_all, k_all, v_all, proj)


def _merge_fwd_bwd(x, target, ya, yb, ao, proj, b_gate, final_g, w_pa, w_pb, w_out):
    s = x.shape[0]
    tm = min(s, TM_ROW)

    def body(x_ref, t_ref, ya_ref, yb_ref, ao_ref, mz_ref, g0_ref, g1_ref, bg_ref, fg_ref, wpa_ref, wpb_ref, wo_ref,
             dx2_ref, dya_ref, dao_ref, dmz_ref, dg0_ref, dg1_ref, mb_ref, dpab_ref, dpbb_ref, dx2b_ref,
             loss_ref, dfg_ref, dbg_ref):
        i = pl.program_id(0)

        @pl.when(i == 0)
        def _():
            loss_ref[...] = jnp.zeros_like(loss_ref)
            dfg_ref[...] = jnp.zeros_like(dfg_ref)
            dbg_ref[...] = jnp.zeros_like(dbg_ref)

        pa = _dot(ya_ref[...], wpa_ref[...])
        pb = _dot(yb_ref[...], wpb_ref[...])
        bg = bg_ref[...]
        g0 = _sigmoid(g0_ref[...] + bg[:, :D_MODEL])
        g1 = _sigmoid(g1_ref[...] + bg[:, D_MODEL:])
        merged = g0 * pa + g1 * pb
        mb = _mx(merged)
        mb_ref[...] = mb
        x2 = x_ref[...] + _dot(mb, wo_ref[...])
        r = lax.rsqrt(jnp.mean(x2 * x2, axis=-1, keepdims=True) + EPS)
        xn = x2 * r
        fg = fg_ref[...]
        diff = xn * fg - t_ref[...]
        loss_ref[...] += 0.5 * jnp.sum(jnp.mean(diff * diff, axis=-1, keepdims=True))
        dy = diff * (1.0 / D_MODEL)
        dfg_ref[...] += _bcast_rows(jnp.sum(dy * xn, axis=0, keepdims=True), 8)
        tt = dy * fg
        dx2 = r * (tt - xn * jnp.mean(tt * xn, axis=-1, keepdims=True))
        dx2_ref[...] = dx2
        dx2b = _mx(dx2)
        dx2b_ref[...] = dx2b
        dmerged = _dot_nt(dx2b, wo_ref[...])
        dpa = _mx(dmerged * g0)
        dpb = _mx(dmerged * g1)
        dpab_ref[...] = dpa
        dpbb_ref[...] = dpb
        dg0 = dmerged * pa * (g0 * (1.0 - g0))
        dg1 = dmerged * pb * (g1 * (1.0 - g1))
        dg0_ref[...] = _mx(dg0)
        dg1_ref[...] = _mx(dg1)
        dbg_ref[:, :D_MODEL] += _bcast_rows(jnp.sum(dg0, axis=0, keepdims=True), 8)
        dbg_ref[:, D_MODEL:] += _bcast_rows(jnp.sum(dg1, axis=0, keepdims=True), 8)
        dya_ref[...] = _dot_nt(dpa, wpa_ref[...])
        dyb = _dot_nt(dpb, wpb_ref[...])
        mz = mz_ref[...]
        sg = _sigmoid(mz)
        dao_ref[...] = _mx(dyb * (mz * sg))
        dmz_ref[...] = _mx(dyb * ao_ref[...] * (sg + mz * sg * (1.0 - sg)))

    def rows(w=D_MODEL):
        return pl.BlockSpec((tm, w), lambda i: (i, 0))

    def const(shape):
        return pl.BlockSpec(shape, lambda i: (0, 0))

    def seg(k):
        return pl.BlockSpec((tm, SEG), lambda i: (i, k))

    f32 = jax.ShapeDtypeStruct((s, D_MODEL), F32)
    b16 = jax.ShapeDtypeStruct((s, D_MODEL), MXU_DTYPE)
    return pl.pallas_call(
        body,
        name="merge_fwd_bwd",
        grid=(s // tm,),
        in_specs=[
            rows(), rows(), rows(), rows(), rows(),
            seg(MZ_SEG), seg(GL_SEG), seg(GL_SEG + 1),
            const((1, 2 * D_MODEL)), const((1, D_MODEL)),
            _resident((D_MODEL, D_MODEL)), _resident((D_MODEL, D_MODEL)), _resident((D_MODEL, D_MODEL)),
        ],
        out_specs=[rows()] * 10 + [const((8, HEAD_DIM)), const((8, D_MODEL)), const((8, 2 * D_MODEL))],
        out_shape=[f32, f32, b16, b16, b16, b16, b16, b16, b16, b16,
                   jax.ShapeDtypeStruct((8, HEAD_DIM), F32),
                   jax.ShapeDtypeStruct((8, D_MODEL), F32),
                   jax.ShapeDtypeStruct((8, 2 * D_MODEL), F32)],
        compiler_params=_params(),
    )(x, target, ya, yb, ao, proj, proj, proj, b_gate, final_g, w_pa, w_pb, w_out)


def _flash_bwd(q_all, k_all, v_all, dao, ao, mblk, p_all, slab_sets):
    s = q_all.shape[0]
    t = min(s, T_ATT_BWD)
    n = s // t
    pairs = [(ki, qi) for ki in range(n) for qi in range(ki, n)]
    ki_list = jnp.asarray([p[0] for p in pairs], jnp.int32)
    qi_list = jnp.asarray([p[1] for p in pairs], jnp.int32)
    p_list = jnp.asarray([qi * (qi + 1) // 2 + ki for ki, qi in pairs], jnp.int32)
    n_ops = len(slab_sets)

    def body(ki_ref, qi_ref, pidx_ref, q_ref, k_ref, v_ref, do_ref, ao_ref, mblk_ref, p_ref, *rest):
        g_refs = rest[:n_ops]
        dq_ref, dk_ref, dv_ref = rest[n_ops:n_ops + 3]
        recv_refs = rest[n_ops + 3:2 * n_ops + 3]
        dk_acc, dv_acc, send_sems, recv_sems, local_sems = rest[2 * n_ops + 3:]
        head, step = pl.program_id(0), pl.program_id(1)
        ki, qi = ki_ref[step], qi_ref[step]

        @pl.when((head == 0) & (step == 0))
        def _():
            _Exchange(g_refs, recv_refs, send_sems, recv_sems, local_sems).start()

        @pl.when(qi == ki)
        def _():
            dk_acc[...] = jnp.zeros_like(dk_acc)
            dv_acc[...] = jnp.zeros_like(dv_acc)

        @pl.when(ki == 0)
        def _():
            dq_ref[pl.ds(pl.multiple_of(qi * t, t), t), :] = jnp.zeros((t, HEAD_PAD), F32)

        def pair(masked):
            nsub = ATT_SUB if masked else ATT_SUB_BWD
            ts = t // nsub
            dk_parts, dv_parts = [], []
            for r in range(nsub):
                rs = slice(r * ts, (r + 1) * ts)
                w = (r + 1) * ts if masked else t
                k = k_ref[:w]
                v = v_ref[:w]
                q = q_ref[rs]
                lane = lax.broadcasted_iota(jnp.int32, (ts, HEAD_DIM), 1)
                stats = mblk_ref[rs]
                m_blk = jnp.max(jnp.where(lane == ki, stats, -jnp.inf), axis=-1, keepdims=True)
                lse = jnp.max(jnp.where(lane == LSE_LANE, stats, -jnp.inf), axis=-1, keepdims=True)
                factor = jnp.exp2(m_blk - lse)
                p_st = p_ref[0, 0, rs, :w]
                do = do_ref[rs]
                do_f = do.astype(F32)
                delta = jnp.sum(do_f * ao_ref[rs], axis=-1, keepdims=True)
                dv_part = _dot_tn(p_st, _mx(do_f * factor))
                ds = p_st * _mx((_dot_nt(do, v) - delta) * factor)
                dk_part = _dot_tn(ds, q)
                rows = pl.ds(pl.multiple_of(qi * t + r * ts, ts), ts)
                dq_ref[rows, :] += _dot(ds, k)
                if masked:
                    dk_acc[:w] += dk_part
                    dv_acc[:w] += dv_part
                else:
                    dk_parts.append(dk_part)
                    dv_parts.append(dv_part)

            if not masked:
                dk_acc[...] += sum(dk_parts[1:], dk_parts[0])
                dv_acc[...] += sum(dv_parts[1:], dv_parts[0])

        @pl.when(qi == ki)
        def _():
            pair(True)

        @pl.when(qi > ki)
        def _():
            pair(False)

        @pl.when(qi == n - 1)
        def _():
            dk_ref[...] = dk_acc[...] * LN2
            dv_ref[...] = dv_acc[...]

        @pl.when((head == HEADS - 1) & (step == len(pairs) - 1))
        def _():
            _Exchange(g_refs, recv_refs, send_sems, recv_sems, local_sems).wait()

    q_map = lambda h, p, ki_ref, qi_ref, pidx_ref: (qi_ref[p], h)
    kv_map = lambda h, p, ki_ref, qi_ref, pidx_ref: (ki_ref[p], h)
    grid_spec = pltpu.PrefetchScalarGridSpec(
        num_scalar_prefetch=3,
        grid=(HEADS, len(pairs)),
        in_specs=[
            pl.BlockSpec((t, HEAD_PAD), q_map),
            pl.BlockSpec((t, HEAD_PAD), kv_map),
            pl.BlockSpec((t, HEAD_DIM), lambda h, p, ki_ref, qi_ref, pidx_ref: (ki_ref[p], 2 * h)),
            pl.BlockSpec((t, HEAD_DIM), q_map),
            pl.BlockSpec((t, HEAD_DIM), q_map),
            pl.BlockSpec((t, HEAD_DIM), q_map),
            pl.BlockSpec((1, 1, t, t), lambda h, p, ki_ref, qi_ref, pidx_ref: (h, pidx_ref[p], 0, 0)),
        ] + [HBM_SPEC] * n_ops,
        out_specs=[
            pl.BlockSpec((s, HEAD_PAD), lambda h, p, ki_ref, qi_ref, pidx_ref: (0, h)),
            pl.BlockSpec((t, HEAD_PAD), kv_map),
            pl.BlockSpec((t, HEAD_DIM), kv_map),
        ] + [HBM_SPEC] * n_ops,
        scratch_shapes=[pltpu.VMEM((t, HEAD_PAD), F32), pltpu.VMEM((t, HEAD_DIM), F32)]
        + _Exchange.semaphores(n_ops),
    )
    outs = pl.pallas_call(
        body,
        name="flash_bwd",
        grid_spec=grid_spec,
        out_shape=[
            jax.ShapeDtypeStruct((s, HEADS * HEAD_PAD), F32),
            jax.ShapeDtypeStruct((s, HEADS * HEAD_PAD), F32),
            jax.ShapeDtypeStruct((s, D_MODEL), F32),
        ] + [jax.ShapeDtypeStruct(a.shape, a.dtype) for a in slab_sets],
        compiler_params=_params(VMEM_LIMIT_BIG),
    )(ki_list, qi_list, p_list, q_all, k_all, v_all, dao, ao, mblk, p_all, *slab_sets)
    return outs[0], outs[1], outs[2], outs[3:]


def _mla_prep_bwd(dq_all, dk_all, dv_all, proj, q_a_g, kv_a_g, w_uq_p, w_kn, w_v, cos, sin_a, sin_b):
    s = proj.shape[0]
    tm = min(s, TM_ROW)

    def body(dq_ref, dk_ref, dv_ref, sm_ref, gq_ref, gk_ref, wq_ref, wkn_ref, wv_ref, cos_ref, sa_ref, sb_ref,
             dsm_ref, dqf_ref, dkn_ref, dvb_ref, dgq_ref, dgk_ref):
        i = pl.program_id(0)

        @pl.when(i == 0)
        def _():
            dgq_ref[...] = jnp.zeros_like(dgq_ref)
            dgk_ref[...] = jnp.zeros_like(dgk_ref)

        cos_t, sa, sb = cos_ref[...], sa_ref[...], sb_ref[...]
        dkpe = jnp.zeros((tm, HEAD_DIM), F32)
        for h in range(HEADS):
            lo = h * HEAD_PAD
            dqf_ref[:, lo:lo + HEAD_DIM] = _mx(dq_ref[:, lo:lo + HEAD_DIM] * QK_SCALE)
            dqf_ref[:, lo + HEAD_DIM:lo + HEAD_PAD] = _mx(
                _rope_t(dq_ref[:, lo + HEAD_DIM:lo + HEAD_PAD] * QK_SCALE, cos_t, sa, sb))
            dkn_ref[:, h * HEAD_DIM:(h + 1) * HEAD_DIM] = _mx(dk_ref[:, lo:lo + HEAD_DIM])
            dkpe = dkpe + dk_ref[:, lo + HEAD_DIM:lo + HEAD_PAD]
        dkr = _rope_t(dkpe, cos_t, sa, sb)
        dvb = _mx(dv_ref[...])
        dvb_ref[...] = dvb
        dcqn = _dot_nt(dqf_ref[...], wq_ref[...])
        dckvn = _dot_nt(dkn_ref[...], wkn_ref[...]) + _dot_nt(dvb, wv_ref[...])

        small = sm_ref[...]
        cq = small[:, :Q_LORA]
        ckv = small[:, Q_LORA:Q_LORA + KV_LORA]
        rq = lax.rsqrt(jnp.mean(cq * cq, axis=-1, keepdims=True) + EPS)
        rk = lax.rsqrt(jnp.mean(ckv * ckv, axis=-1, keepdims=True) + EPS)
        cqh = cq * rq
        ckh = ckv * rk
        dgq_ref[...] += _bcast_rows(jnp.sum(dcqn * cqh, axis=0, keepdims=True), 8)
        dgk_ref[...] += _bcast_rows(jnp.sum(dckvn * ckh, axis=0, keepdims=True), 8)
        tq = dcqn * gq_ref[...]
        tk = dckvn * gk_ref[...]
        dcq = rq * (tq - cqh * jnp.mean(tq * cqh, axis=-1, keepdims=True))
        dckv = rk * (tk - ckh * jnp.mean(tk * ckh, axis=-1, keepdims=True))
        dsm_ref[:, :Q_LORA] = _mx(dcq)
        dsm_ref[:, Q_LORA:Q_LORA + KV_LORA] = _mx(dckv)
        dsm_ref[:, Q_LORA + KV_LORA:Q_LORA + KV_LORA + HEAD_DIM] = _mx(dkr)
        dsm_ref[:, Q_LORA + KV_LORA + HEAD_DIM:] = jnp.zeros((tm, SEG - Q_LORA - KV_LORA - HEAD_DIM), MXU_DTYPE)

    def const(shape):
        return pl.BlockSpec(shape, lambda i: (0, 0))

    def rows(w):
        return pl.BlockSpec((tm, w), lambda i: (i, 0))

    return pl.pallas_call(
        body,
        name="mla_prep_bwd",
        grid=(s // tm,),
        in_specs=[
            rows(HEADS * HEAD_PAD), rows(HEADS * HEAD_PAD), rows(D_MODEL),
            pl.BlockSpec((tm, SEG), lambda i: (i, SMALL_SEG)),
            const((1, Q_LORA)), const((1, KV_LORA)),
            const((Q_LORA, HEADS * HEAD_PAD)), const((KV_LORA, D_MODEL)), const((KV_LORA, D_MODEL)),
            rows(HEAD_DIM), rows(HEAD_DIM), rows(HEAD_DIM),
        ],
        out_specs=[rows(SEG), rows(HEADS * HEAD_PAD), rows(D_MODEL), rows(D_MODEL),
                   const((8, Q_LORA)), const((8, KV_LORA))],
        out_shape=[
            jax.ShapeDtypeStruct((s, SEG), MXU_DTYPE),
            jax.ShapeDtypeStruct((s, HEADS * HEAD_PAD), MXU_DTYPE),
            jax.ShapeDtypeStruct((s, D_MODEL), MXU_DTYPE),
            jax.ShapeDtypeStruct((s, D_MODEL), MXU_DTYPE),
            jax.ShapeDtypeStruct((8, Q_LORA), F32),
            jax.ShapeDtypeStruct((8, KV_LORA), F32),
        ],
        compiler_params=_params(),
    )(dq_all, dk_all, dv_all, proj, q_a_g, kv_a_g, w_uq_p, w_kn, w_v, cos, sin_a, sin_b)


def _hgrn_bwd(proj, lb_logits, hg_norm_g, o_all, dya, states):
    s = proj.shape[0]
    t = min(s, T_HGRN)
    nb = s // t
    nc = t // HG_CHUNK

    def body(hq_ref, hf_ref, hi_ref, hz_ref, lb_ref, g_ref, o_ref, dya_ref, st_ref,
             dhq_ref, dhf_ref, dhi_ref, dhz_ref, dlb_ref, dg_ref, dstate, u_sc, g_sc, stf_sc, stb_sc, dstb_sc):
        h, b = pl.program_id(0), pl.program_id(1)

        @pl.when(b == 0)
        def _():
            dstate[...] = jnp.zeros_like(dstate)
            dlb_ref[...] = jnp.zeros_like(dlb_ref)

        @pl.when((b == 0) & (h == 0))
        def _():
            dg_ref[...] = jnp.zeros_like(dg_ref)

        lower = _chunk_lower_mask(t)
        pos = _chunk_pos(t)
        ghg = g_ref[...]
        for hh in range(HG_HEADS_PER_STEP):
            cols = slice(hh * HEAD_DIM, (hh + 1) * HEAD_DIM)
            hq, hf, hz = hq_ref[:, cols], hf_ref[:, cols], hz_ref[:, cols]
            gt = _hgrn_gates(hq, hf, lb_ref[:, cols], pos)
            vb = _mx(hi_ref[:, cols])
            qi, ki, ko = gt["qi"], gt["ki"], gt["ko"]
            qib, kib, kob = _mx(qi), _mx(ki), _mx(ko)

            o = o_ref[:, cols]
            sz = _sigmoid(hz)
            r = lax.rsqrt(jnp.mean(o * o, axis=-1, keepdims=True) + EPS)
            on = o * r
            dya_t = dya_ref[:, cols]
            don = dya_t * (hz * sz)
            dhz_ref[:, cols] = _mx(dya_t * (on * ghg) * (sz + hz * sz * (1.0 - sz)))
            dg_ref[...] += _bcast_rows(jnp.sum(don * on, axis=0, keepdims=True), 8)
            tt = don * ghg
            do = r * (tt - on * jnp.mean(tt * on, axis=-1, keepdims=True))
            dob = _mx(do)

            for c in range(nc):
                sl = slice(c * HG_CHUNK, (c + 1) * HG_CHUNK)
                u_sc[hh, c] = _dot_tn(vb[sl], kob[sl])
                g_sc[hh, c] = _dot_tn(dob[sl], qib[sl])

            st = st_ref[0, hh]
            for c in range(nc):
                stf_sc[hh, c] = st
                stb_sc[hh, c] = _mx(st)
                if c < nc - 1:
                    st = st * gt["dec"][c * HG_CHUNK:c * HG_CHUNK + 1, :] + u_sc[hh, c]

            dst = dstate[hh]
            dd_parts = [None] * nc
            for c in reversed(range(nc)):
                dec = gt["dec"][c * HG_CHUNK:c * HG_CHUNK + 1, :]
                dstb_sc[hh, c] = _mx(dst)
                dd_parts[c] = _bcast_rows(jnp.sum(dst * stf_sc[hh, c], axis=0, keepdims=True) * dec, HG_CHUNK)
                dst = dst * dec + g_sc[hh, c]
            dstate[hh] = dst

            a = jnp.where(lower, _dot_nt(qib, kib), 0.0)
            da = _mx(jnp.where(lower, _dot_nt(dob, vb), 0.0))
            dqi_intra = _dot(da, kib)
            dki = _dot_tn(da, qib)
            dv_intra = _dot_tn(_mx(a), dob)

            dqi_parts, dko_parts, dv_parts = [None] * nc, [None] * nc, [None] * nc
            for c in range(nc):
                sl = slice(c * HG_CHUNK, (c + 1) * HG_CHUNK)
                dv_parts[c] = dv_intra[sl] + _dot_nt(kob[sl], dstb_sc[hh, c])
                dko_parts[c] = _dot(vb[sl], dstb_sc[hh, c])
                dqi_parts[c] = dqi_intra[sl] + _dot(dob[sl], stb_sc[hh, c])
            dqi = jnp.concatenate(dqi_parts, axis=0)
            dko = jnp.concatenate(dko_parts, axis=0)
            dv = jnp.concatenate(dv_parts, axis=0)
            dd = jnp.concatenate(dd_parts, axis=0)

            dq = dqi * gt["eb"]
            dk = dki * gt["enb"] + dko * gt["eo"]
            db = dqi * qi - dki * ki - dko * ko
            dlogf = _rcumsum_chunk(db, pos) + _chunk_total(dko * ko) + dd
            df = dlogf / gt["f"] - dk
            lb, sig, sq = gt["lb"], gt["sig"], gt["sq"]
            dhf_ref[:, cols] = _mx(df * (1.0 - lb) * (sig * (1.0 - sig)))
            dhq_ref[:, cols] = _mx(dq * (sq + hq * sq * (1.0 - sq)))
            dhi_ref[:, cols] = _mx(dv)
            dlb = jnp.sum(df * (1.0 - sig), axis=0, keepdims=True) * (lb * (1.0 - lb))
            dlb_ref[:, cols] += jnp.concatenate([dlb, -dlb], axis=0)

    hw = HG_HEADS_PER_STEP * HEAD_DIM
    hsteps = HEADS // HG_HEADS_PER_STEP

    def seg(k):
        return pl.BlockSpec((t, hw), lambda h, b, k=k: (nb - 1 - b, k * hsteps + h))

    blk = pl.BlockSpec((t, hw), lambda h, b: (nb - 1 - b, h))
    b16 = jax.ShapeDtypeStruct((s, D_MODEL), MXU_DTYPE)
    return pl.pallas_call(
        body,
        name="hgrn_bwd",
        grid=(hsteps, nb),
        in_specs=[seg(0), seg(1), seg(2), seg(3),
                  pl.BlockSpec((2, hw), lambda h, b: (0, h)),
                  pl.BlockSpec((1, HEAD_DIM), lambda h, b: (0, 0)),
                  blk, blk,
                  pl.BlockSpec((1, HG_HEADS_PER_STEP, HEAD_DIM, HEAD_DIM), lambda h, b: (nb - 1 - b, h, 0, 0))],
        out_specs=[blk, blk, blk, blk,
                   pl.BlockSpec((2, hw), lambda h, b: (0, h)),
                   pl.BlockSpec((8, HEAD_DIM), lambda h, b: (0, 0))],
        out_shape=[b16, b16, b16, b16,
                   jax.ShapeDtypeStruct((2, D_MODEL), F32),
                   jax.ShapeDtypeStruct((8, HEAD_DIM), F32)],
        scratch_shapes=[pltpu.VMEM((HG_HEADS_PER_STEP, HEAD_DIM, HEAD_DIM), F32)]
        + [pltpu.VMEM((HG_HEADS_PER_STEP, nc, HEAD_DIM, HEAD_DIM), F32)] * 3
        + [pltpu.VMEM((HG_HEADS_PER_STEP, nc, HEAD_DIM, HEAD_DIM), MXU_DTYPE)] * 2,
        compiler_params=_params(),
    )(proj, proj, proj, proj, lb_logits, hg_norm_g, o_all, dya, states)


def _dh_bwd(segs, w_in_p, x, dx2, norm_g, late_slab, late_recv_init, misc_slabs):
    s = x.shape[0]
    tm = min(s, TM_ROW)
    nseg = len(segs)
    nsteps = s // tm
    late_xyc = ((LATE_DEV >> 2) & 1, (LATE_DEV >> 1) & 1, LATE_DEV & 1)

    def body(*refs):
        seg_refs = refs[:nseg]
        (w_ref, x_ref, dx2_ref, g_ref, late_ref, _, misc_ref,
         gx_ref, dng_ref, late_recv_ref, misc_recv_ref,
         dp_buf, send_sems, recv_sems, local_sems, late_send, late_recvs, late_local) = refs[nseg:]
        i = pl.program_id(0)
        me = 4 * lax.axis_index("x") + 2 * lax.axis_index("y") + lax.axis_index("c")

        def misc_exchange():
            return _Exchange([misc_ref], [misc_recv_ref], send_sems, recv_sems, local_sems)

        def late_copy(sender):
            return pltpu.make_async_remote_copy(
                src_ref=late_ref.at[0], dst_ref=late_recv_ref.at[sender], send_sem=late_send,
                recv_sem=late_recvs.at[(sender ^ LATE_DEV) - 1], device_id=late_xyc, device_id_type=MESH)

        def late_own():
            return pltpu.make_async_copy(late_ref.at[0], late_recv_ref.at[LATE_DEV], late_local)

        @pl.when(i == 0)
        def _():
            dng_ref[...] = jnp.zeros_like(dng_ref)
            misc_exchange().start()

        @pl.when((i == 0) & (me != LATE_DEV))
        def _():
            late_copy(me).start()

        @pl.when((i == 0) & (me == LATE_DEV))
        def _():
            late_own().start()

        for k, sref in enumerate(seg_refs):
            dp_buf[:, k * SEG:(k + 1) * SEG] = sref[...]
        dh = _dot_nt(dp_buf[...], w_ref[...])
        xf = x_ref[...]
        r = lax.rsqrt(jnp.mean(xf * xf, axis=-1, keepdims=True) + EPS)
        xh = xf * r
        dng_ref[...] += _bcast_rows(jnp.sum(dh * xh, axis=0, keepdims=True), 8)
        tt = dh * g_ref[...]
        gx_ref[...] = dx2_ref[...] + r * (tt - xh * jnp.mean(tt * xh, axis=-1, keepdims=True))

        @pl.when(i == nsteps - 1)
        def _():
            misc_exchange().wait()

        @pl.when((i == nsteps - 1) & (me != LATE_DEV))
        def _():
            late_copy(me).wait_send()

        @pl.when((i == nsteps - 1) & (me == LATE_DEV))
        def _():
            for k in range(1, N_DEV):
                late_copy(LATE_DEV ^ k).wait_recv()
            late_own().wait()

    rows = pl.BlockSpec((tm, D_MODEL), lambda i: (i, 0))
    return pl.pallas_call(
        body,
        name="dh_bwd",
        grid=(nsteps,),
        in_specs=[pl.BlockSpec((tm, SEG), lambda i: (i, 0))] * nseg + [
            _resident((D_MODEL, PROJ_W)),
            rows, rows,
            pl.BlockSpec((1, D_MODEL), lambda i: (0, 0)),
            HBM_SPEC, HBM_SPEC, HBM_SPEC,
        ],
        out_specs=[rows, pl.BlockSpec((8, D_MODEL), lambda i: (0, 0)), HBM_SPEC, HBM_SPEC],
        out_shape=[jax.ShapeDtypeStruct((s, D_MODEL), F32), jax.ShapeDtypeStruct((8, D_MODEL), F32),
                   jax.ShapeDtypeStruct(late_recv_init.shape, late_recv_init.dtype),
                   jax.ShapeDtypeStruct(misc_slabs.shape, misc_slabs.dtype)],
        input_output_aliases={nseg + 5: 2},
        scratch_shapes=[pltpu.VMEM((tm, PROJ_W), MXU_DTYPE)] + _Exchange.semaphores(1)
        + [pltpu.SemaphoreType.DMA, pltpu.SemaphoreType.DMA((N_DEV - 1,)), pltpu.SemaphoreType.DMA],
        compiler_params=_params(),
    )(*segs, w_in_p, x, dx2, norm_g, late_slab, late_recv_init, misc_slabs)


def _matmul_tn(a, b, name, out_dtype=F32):
    s, m = a.shape
    n = b.shape[1]
    ts = min(s, TS_TN)
    tn = min(n, SEG)
    nk = s // ts

    def body(a_ref, b_ref, o_ref, acc):
        k = pl.program_id(1)
        part = _dot_tn(a_ref[...], b_ref[...])

        @pl.when(k == 0)
        def _():
            acc[...] = part

        @pl.when(k > 0)
        def _():
            acc[...] += part

        @pl.when(k == nk - 1)
        def _():
            o_ref[...] = acc[...].astype(out_dtype)

    return pl.pallas_call(
        body,
        name=name,
        grid=(n // tn, nk),
        in_specs=[pl.BlockSpec((ts, m), lambda j, k: (k, 0)), pl.BlockSpec((ts, tn), lambda j, k: (k, j))],
        out_specs=pl.BlockSpec((m, tn), lambda j, k: (0, j)),
        out_shape=jax.ShapeDtypeStruct((m, n), out_dtype),
        scratch_shapes=[pltpu.VMEM((m, tn), F32)],
        compiler_params=_params(),
    )(a, b)


def _w_in_pieces():
    per = IN_COLS // N_DEV
    pad_at = SMALL_SEG * SEG + Q_LORA + KV_LORA + QK_ROPE
    pieces = []
    for j in range(N_DEV):
        u0, u1 = j * per, (j + 1) * per
        cuts = [u0] + ([pad_at] if u0 < pad_at < u1 else []) + [u1]
        for a, b in zip(cuts[:-1], cuts[1:]):
            pieces.append((j, a - u0, b - u0, a if a < pad_at else a + PROJ_W - IN_COLS))
    return pad_at, pieces


def _assemble_w_in(gathered):
    tr = TM_ROW
    pad_at, pieces = _w_in_pieces()

    def body(in_ref, out_ref):
        out_ref[:, pad_at:pad_at + PROJ_W - IN_COLS] = jnp.zeros((tr, PROJ_W - IN_COLS), gathered.dtype)
        for j, a, b, p0 in pieces:
            out_ref[:, p0:p0 + b - a] = in_ref[j, :, a:b]

    return pl.pallas_call(
        body,
        name="assemble_w_in",
        grid=(D_MODEL // tr,),
        in_specs=[pl.BlockSpec((N_DEV, tr, PACK_COLS), lambda i: (0, i, 0))],
        out_specs=pl.BlockSpec((tr, PROJ_W), lambda i: (i, 0)),
        out_shape=jax.ShapeDtypeStruct((D_MODEL, PROJ_W), gathered.dtype),
        compiler_params=_params(),
    )(gathered)


def _scatter_dw_in(dw_segs, devs, name):
    tr = TM_ROW
    _, pieces = _w_in_pieces()
    per = IN_COLS // N_DEV
    seg_ids = sorted(dw_segs)
    nseg = len(seg_ids)

    def body(*refs):
        out_ref, buf = refs[nseg:]
        for k in range(PROJ_W // SEG):
            if k in seg_ids:
                buf[:, k * SEG:(k + 1) * SEG] = refs[seg_ids.index(k)][...]
            else:
                buf[:, k * SEG:(k + 1) * SEG] = jnp.zeros((tr, SEG), F32)
        for slot, dev in enumerate(devs):
            out_ref[slot, :, per:] = jnp.zeros((tr, PACK_COLS - per), TRANSPORT_DTYPE)
            for j, a, b, p0 in pieces:
                if j == dev:
                    out_ref[slot, :, a:b] = buf[:, p0:p0 + b - a].astype(TRANSPORT_DTYPE)

    return pl.pallas_call(
        body,
        name=name,
        grid=(D_MODEL // tr,),
        in_specs=[pl.BlockSpec((tr, SEG), lambda i: (i, 0))] * nseg,
        out_specs=pl.BlockSpec((len(devs), tr, PACK_COLS), lambda i: (0, i, 0)),
        out_shape=jax.ShapeDtypeStruct((len(devs), D_MODEL, PACK_COLS), TRANSPORT_DTYPE),
        scratch_shapes=[pltpu.VMEM((tr, PROJ_W), F32)],
        compiler_params=_params(),
    )(*[dw_segs[k] for k in seg_ids])


def _rope_tables(s):
    inv = ROPE_THETA ** (-jnp.arange(0, QK_ROPE, 2, dtype=F32) / QK_ROPE)
    ang = jnp.arange(s, dtype=F32)[:, None] * inv[None, :]
    cos, sin = jnp.cos(ang), jnp.sin(ang)
    z32 = jnp.zeros_like(cos)
    z64 = jnp.zeros((s, HEAD_DIM - QK_ROPE), F32)
    cos_t = jnp.concatenate([cos, cos, z64], axis=1)
    sin_a = jnp.concatenate([-sin, z32, z64], axis=1)
    sin_b = jnp.concatenate([z32, sin, z64], axis=1)
    return cos_t, sin_a, sin_b


def _pack_misc(w_uq, w_ukv, norm_g, b_gate, lb_logits, hg_norm_g, q_a_g, kv_a_g, final_norm_g, extra):
    misc = jnp.concatenate([hg_norm_g.reshape(-1), q_a_g.reshape(-1), kv_a_g.reshape(-1), extra.reshape(-1),
                            jnp.zeros((PACK_COLS - HEAD_DIM - Q_LORA - KV_LORA - 1,), F32)])
    return jnp.concatenate([w_uq.reshape(ROWS_W_UQ, PACK_COLS), w_ukv.reshape(ROWS_W_UKV, PACK_COLS),
                            norm_g.reshape(1, -1), b_gate.reshape(2, -1), lb_logits.reshape(2, -1),
                            misc.reshape(1, -1), final_norm_g.reshape(1, -1), jnp.zeros((1, PACK_COLS), F32)], axis=0)


def _unpack_misc(p):
    sm = p[ROWS_W_UQ + ROWS_W_UKV:]
    misc = sm[5]
    return dict(
        w_uq=p[:ROWS_W_UQ].reshape(1, Q_LORA, QK_DIM),
        w_ukv=p[ROWS_W_UQ:ROWS_W_UQ + ROWS_W_UKV].reshape(1, KV_LORA, 2 * HEAD_DIM),
        norm_g=sm[0:1], b_gate=sm[1:3].reshape(1, -1), lb_logits=sm[3:5],
        hg_norm_g=misc[None, :HEAD_DIM], q_a_g=misc[None, HEAD_DIM:HEAD_DIM + Q_LORA],
        kv_a_g=misc[None, HEAD_DIM + Q_LORA:HEAD_DIM + Q_LORA + KV_LORA], final_norm_g=sm[6],
        extra=misc[HEAD_DIM + Q_LORA + KV_LORA],
    )


def _weight_shard_buffers(w_in, w_uq, w_ukv, w_pa, w_pb, w_out):
    w_in_pad = jnp.pad(w_in.reshape(D_MODEL, -1), ((0, 0), (0, PACK_COLS - IN_COLS // N_DEV)))
    parts = [a.reshape(-1, PACK_COLS) for a in (w_pa, w_pb, w_out, w_uq, w_ukv)]
    others = jnp.concatenate(parts + [jnp.zeros((ROWS_OTHER - ROWS_OTHER_USED, PACK_COLS), F32)], axis=0)
    return w_in_pad.astype(MXU_DTYPE), others.astype(MXU_DTYPE)


def _other_weights(gathered):
    r0 = 0
    mats = []
    for _ in range(3):
        mats.append(gathered[:, r0:r0 + ROWS_W_PROJ].reshape(D_MODEL, D_MODEL))
        r0 += ROWS_W_PROJ
    w_uq = gathered[:, r0:r0 + ROWS_W_UQ].reshape(N_DEV, Q_LORA, QK_DIM).transpose(1, 0, 2)
    w_uq_p = jnp.concatenate([w_uq, jnp.zeros((Q_LORA, HEADS, HEAD_PAD - QK_DIM), w_uq.dtype)], axis=2)
    w_uq_p = w_uq_p.reshape(Q_LORA, HEADS * HEAD_PAD)
    r0 += ROWS_W_UQ
    w_ukv = gathered[:, r0:r0 + ROWS_W_UKV].reshape(N_DEV, KV_LORA, 2 * HEAD_DIM).transpose(1, 0, 2)
    w_kn = w_ukv[:, :, :HEAD_DIM].reshape(KV_LORA, D_MODEL)
    w_v = w_ukv[:, :, HEAD_DIM:].reshape(KV_LORA, D_MODEL)
    return w_uq_p, w_kn, w_v, mats[0], mats[1], mats[2]


def _misc_slabs(dw_uq_p, dw_kn, dw_v, small):
    dw_uq = dw_uq_p.reshape(Q_LORA, HEADS, HEAD_PAD)[:, :, :QK_DIM].transpose(1, 0, 2)
    dw_ukv = jnp.concatenate([dw_kn.reshape(KV_LORA, HEADS, HEAD_DIM),
                              dw_v.reshape(KV_LORA, HEADS, HEAD_DIM)], axis=2).transpose(1, 0, 2)
    return jnp.stack([_pack_misc(dw_uq[j], dw_ukv[j], *small) for j in range(N_DEV)])


def _step_gradients(x, target, norm_g, b_gate, lb_logits, hg_norm_g, q_a_g, kv_a_g, final_g,
                    w_in_p, other_shard):
    s = x.shape[0]
    cos, sin_a, sin_b = _rope_tables(s)
    proj, h, gathered = _inproj(x, norm_g, w_in_p, other_shard)
    w_uq_p, w_kn, w_v, w_pa, w_pb, w_out = _other_weights(gathered)
    o_all, ya, states = _hgrn_fwd(proj, lb_logits, hg_norm_g)
    q_all, k_all, v_all, cqn, ckvn = _mla_prep(proj, q_a_g, kv_a_g, w_uq_p, w_kn, w_v, cos, sin_a, sin_b)
    ao, yb, mblk, p_all = _flash_fwd(q_all, k_all, v_all, proj)
    (dx2, dya, dao, dmz, dg0, dg1, merged_b, dpa_b, dpb_b, dx2_b,
     loss_acc, dfg_acc, dbg_acc) = _merge_fwd_bwd(x, target, ya, yb, ao, proj, b_gate, final_g, w_pa, w_pb, w_out)
    dhq, dhf, dhi, dhz, dlb, dhg_acc = _hgrn_bwd(proj, lb_logits, hg_norm_g, o_all, dya, states)

    early = {0: dhq, 1: dhf, 2: dhi, 3: dhz, MZ_SEG: dmz, GL_SEG: dg0, GL_SEG + 1: dg1}
    dw_early = {k: _matmul_tn(h, sg, "dw_in_%d" % k) for k, sg in early.items()}
    mats = [_matmul_tn(a, b, name, TRANSPORT_DTYPE).reshape(N_DEV, ROWS_W_PROJ, PACK_COLS)
            for a, b, name in ((ya, dpa_b, "dw_pa"), (yb, dpb_b, "dw_pb"), (merged_b, dx2_b, "dw_out"))]
    early_slabs = [_scatter_dw_in(dw_early, list(range(N_DEV)), "scatter_dw_in")] + mats
    dq_all, dk_all, dv_all, (recv_in, recv_pa, recv_pb, recv_out) = _flash_bwd(
        q_all, k_all, v_all, dao, ao, mblk, p_all, early_slabs)

    dsmall, dqf_b, dkn_b, dv_b, dgq_acc, dgk_acc = _mla_prep_bwd(
        dq_all, dk_all, dv_all, proj, q_a_g, kv_a_g, w_uq_p, w_kn, w_v, cos, sin_a, sin_b)
    late_slab = _scatter_dw_in({SMALL_SEG: _matmul_tn(h, dsmall, "dw_in_%d" % SMALL_SEG)}, [LATE_DEV],
                               "scatter_dw_in_late")
    segs = [dhq, dhf, dhi, dhz, dsmall, dmz, dg0, dg1]
    return dict(
        segs=segs, h=h, dx2=dx2, late_slab=late_slab,
        dw_uq_p=_matmul_tn(cqn, dqf_b, "dw_uq"), dw_kn=_matmul_tn(ckvn, dkn_b, "dw_kn"),
        dw_v=_matmul_tn(ckvn, dv_b, "dw_v"),
        small=dict(b_gate=dbg_acc[0:1], lb_logits=dlb, hg_norm_g=dhg_acc[0:1], q_a_g=dgq_acc[0:1],
                   kv_a_g=dgk_acc[0:1], final_norm_g=dfg_acc[0], loss=loss_acc[0, 0]),
        recv=dict(w_in=[recv_in], w_pa=[recv_pa], w_pb=[recv_pb], w_out=[recv_out]),
    )


def kernel(x, norm_g, w_in, b_gate, lb_logits, hg_norm_g, q_a_g, w_uq, kv_a_g, w_ukv, w_proj_a, w_proj_b, w_out, final_norm_g, loss_target, m_norm_g, m_w_in, m_b_gate, m_lb_logits, m_hg_norm_g, m_q_a_g, m_w_uq, m_kv_a_g, m_w_ukv, m_w_proj_a, m_w_proj_b, m_w_out, m_final_norm_g, v_norm_g, v_w_in, v_b_gate, v_lb_logits, v_hg_norm_g, v_q_a_g, v_w_uq, v_kv_a_g, v_w_ukv, v_w_proj_a, v_w_proj_b, v_w_out, v_final_norm_g):
    zero = jnp.zeros((1,), F32)
    xs = x[0]
    w_in_shard, other_shard = _weight_shard_buffers(w_in, w_uq, w_ukv, w_proj_a, w_proj_b, w_out)
    w_in_p = _assemble_w_in(_all_gather_packed(w_in_shard))
    g = _step_gradients(xs, loss_target[0], norm_g, b_gate, lb_logits, hg_norm_g, q_a_g, kv_a_g,
                        final_norm_g.reshape(1, -1), w_in_p, other_shard)
    sm = g["small"]
    misc_slabs = _misc_slabs(g["dw_uq_p"], g["dw_kn"], g["dw_v"],
                             (jnp.zeros_like(norm_g), sm["b_gate"], sm["lb_logits"], sm["hg_norm_g"], sm["q_a_g"],
                              sm["kv_a_g"], sm["final_norm_g"], sm["loss"]))
    late_recv_init = jnp.zeros((N_DEV, D_MODEL, PACK_COLS), TRANSPORT_DTYPE)
    grad_x, dng_acc, recv_late, recv_misc = _dh_bwd(g["segs"], w_in_p, xs, g["dx2"], norm_g,
                                                    g["late_slab"], late_recv_init, misc_slabs)
    recv_ng = _exchange_rows(jnp.broadcast_to(dng_acc[None], (N_DEV, 8, D_MODEL)))

    recv = g["recv"]
    out_in = _sum_adamw(recv["w_in"] + [recv_late], w_in[0], m_w_in[0], v_w_in[0], "adamw_w_in")
    out_pa = _sum_adamw(recv["w_pa"], w_proj_a[0], m_w_proj_a[0], v_w_proj_a[0], "adamw_w_pa")
    out_pb = _sum_adamw(recv["w_pb"], w_proj_b[0], m_w_proj_b[0], v_w_proj_b[0], "adamw_w_pb")
    out_out = _sum_adamw(recv["w_out"], w_out[0], m_w_out[0], v_w_out[0], "adamw_w_out")
    out_ng = _sum_adamw([recv_ng], *[jnp.broadcast_to(a, (8, D_MODEL)) for a in (norm_g, m_norm_g, v_norm_g)],
                        "adamw_norm_g")
    out_misc = _sum_adamw(
        [recv_misc],
        _pack_misc(w_uq, w_ukv, norm_g, b_gate, lb_logits, hg_norm_g, q_a_g, kv_a_g, final_norm_g, zero),
        _pack_misc(m_w_uq, m_w_ukv, m_norm_g, m_b_gate, m_lb_logits, m_hg_norm_g, m_q_a_g, m_kv_a_g,
                   m_final_norm_g, zero),
        _pack_misc(v_w_uq, v_w_ukv, v_norm_g, v_b_gate, v_lb_logits, v_hg_norm_g, v_q_a_g, v_kv_a_g,
                   v_final_norm_g, zero),
        "adamw_misc")
    names = ["norm_g", "w_in", "b_gate", "lb_logits", "hg_norm_g", "q_a_g", "w_uq", "kv_a_g", "w_ukv",
             "w_proj_a", "w_proj_b", "w_out", "final_norm_g"]
    kinds = []
    for i in range(4):
        d = _unpack_misc(out_misc[i])
        d.update(w_in=out_in[i][None], w_proj_a=out_pa[i][None], w_proj_b=out_pb[i][None], w_out=out_out[i][None],
                 norm_g=out_ng[i][0:1])
        kinds.append(d)
    return (kinds[0]["extra"], grad_x[None], *[d[n] for d in kinds for n in names])
```

```python
import functools

import jax
import jax.numpy as jnp
from jax import lax
from jax.experimental import pallas as pl
from jax.experimental.pallas import tpu as pltpu

D_MODEL = 1024
HEADS = 8
HEAD_DIM = 128
HG_CHUNK = 32
Q_LORA = 384
KV_LORA = 256
QK_ROPE = 64
QK_DIM = 192
ROPE_THETA = 10000.0
EPS = 1e-6
IN_COLS = 7872
ADAM_LR = 0.001
ADAM_B1 = 0.9
ADAM_B2 = 0.999
ADAM_EPS = 1e-08
ADAM_WD = 0.01
ADAM_STEP = 10

N_DEV = 8
SEG = 1024
PROJ_W = 8 * SEG
SMALL_SEG = 4
MZ_SEG = 5
GL_SEG = 6
HEAD_PAD = 256
PACK_COLS = 1024
ROWS_W_UQ = 72
ROWS_W_UKV = 64
ROWS_W_PROJ = 128
ROWS_OTHER_USED = 3 * ROWS_W_PROJ + ROWS_W_UQ + ROWS_W_UKV
ROWS_OTHER = 528
LATE_DEV = (SMALL_SEG * SEG) // (IN_COLS // N_DEV)
assert (SMALL_SEG * SEG + Q_LORA + KV_LORA + QK_ROPE - 1) // (IN_COLS // N_DEV) == LATE_DEV

QK_SCALE = QK_DIM ** -0.5
LOG2E = 1.4426950408889634
LN2 = 0.6931471805599453
Q_PRESCALE = QK_SCALE * LOG2E

MXU_DTYPE = jnp.bfloat16
TRANSPORT_DTYPE = jnp.bfloat16
VMEM_LIMIT = 48 * 1024 * 1024
VMEM_LIMIT_BIG = 60 * 1024 * 1024

T_HGRN = 256
HG_HEADS_PER_STEP = 4
TM_ROW = 256
T_ATT = 1024
T_ATT_BWD = T_ATT
ATT_SUB = 4
P_SLOTS = 4
LSE_LANE = 127
ATT_SUB_BWD = 2
TS_TN = 2048
TR_ADAM = 256

F32 = jnp.float32
MESH = pl.DeviceIdType.MESH


def _dot(a, b):
    return jnp.dot(a, b, preferred_element_type=F32)


def _dot_nt(a, b):
    return lax.dot_general(a, b, (((1,), (1,)), ((), ())), preferred_element_type=F32)


def _dot_tn(a, b):
    return lax.dot_general(a, b, (((0,), (0,)), ((), ())), preferred_element_type=F32)


def _mx(a):
    return a.astype(MXU_DTYPE)


def _sigmoid(x):
    return 1.0 / (1.0 + jnp.exp(-x))


def _params(vmem=VMEM_LIMIT, **kw):
    return pltpu.CompilerParams(vmem_limit_bytes=vmem, **kw)


def _bcast_rows(row, n):
    return jnp.broadcast_to(row, (n, row.shape[-1]))


def _resident(shape):
    return pl.BlockSpec(shape, lambda *_: (0, 0), pipeline_mode=pl.Buffered(1))


HBM_SPEC = pl.BlockSpec(memory_space=pltpu.HBM)


def _all_gather_packed(shard):
    rows, cols = shard.shape

    def body(x_ref, out_ref, send_sems, recv_sems, local_sem):
        x, y, c = lax.axis_index("x"), lax.axis_index("y"), lax.axis_index("c")
        me, sibling = (x, y, c), (x, y, 1 - c)
        chips = [(1 - x, y), (x, 1 - y), (1 - x, 1 - y)]

        def slot(px, py, pc):
            return out_ref.at[4 * px + 2 * py + pc]

        def copy(k, block, to, src=None):
            return pltpu.make_async_remote_copy(
                src_ref=slot(*block) if src is None else src,
                dst_ref=slot(*block),
                send_sem=send_sems.at[k],
                recv_sem=recv_sems.at[k],
                device_id=to,
                device_id_type=MESH,
            )

        mine = pltpu.make_async_copy(x_ref, slot(*me), local_sem)
        mine.start()
        first = [copy(0, me, sibling, src=x_ref)]
        first += [copy(1 + j, me, (*chip, c), src=x_ref) for j, chip in enumerate(chips)]
        for cp in first:
            cp.start()
        passed = [copy(4 + j, (*chip, c), sibling) for j, chip in enumerate(chips)]
        for j, chip in enumerate(chips):
            copy(1 + j, (*chip, c), me).wait_recv()
            passed[j].start()
        copy(0, sibling, me).wait_recv()
        for j, chip in enumerate(chips):
            copy(4 + j, (*chip, 1 - c), me).wait_recv()
        for cp in first + passed:
            cp.wait_send()
        mine.wait()

    return pl.pallas_call(
        body,
        name="ag_weights",
        out_shape=jax.ShapeDtypeStruct((N_DEV, rows, cols), shard.dtype),
        in_specs=[HBM_SPEC],
        out_specs=HBM_SPEC,
        scratch_shapes=[
            pltpu.SemaphoreType.DMA((7,)),
            pltpu.SemaphoreType.DMA((7,)),
            pltpu.SemaphoreType.DMA,
        ],
    )(shard)


class _Exchange:
    def __init__(self, g_refs, recv_refs, send_sems, recv_sems, local_sems, gather=False):
        x, y, c = lax.axis_index("x"), lax.axis_index("y"), lax.axis_index("c")
        me = 4 * x + 2 * y + c
        n_ops = len(g_refs)

        def source(i, dest):
            return g_refs[i] if gather else g_refs[i].at[dest]

        def copy(i, k, landing):
            px, py, pc = x ^ ((k >> 2) & 1), y ^ ((k >> 1) & 1), c ^ (k & 1)
            peer = 4 * px + 2 * py + pc
            return pltpu.make_async_remote_copy(
                src_ref=source(i, peer),
                dst_ref=recv_refs[i].at[peer if landing else me],
                send_sem=send_sems.at[i * (N_DEV - 1) + k - 1],
                recv_sem=recv_sems.at[i * (N_DEV - 1) + k - 1],
                device_id=(px, py, pc),
                device_id_type=MESH,
            )

        pairs = [(i, k) for i in range(n_ops) for k in range(1, N_DEV)]
        self.mine = lambda: [pltpu.make_async_copy(source(i, me), recv_refs[i].at[me], local_sems.at[i])
                             for i in range(n_ops)]
        self.sends = lambda: [copy(i, k, False) for i, k in pairs]
        self.landings = lambda: [copy(i, k, True) for i, k in pairs]

    def start(self):
        for cp in self.mine() + self.sends():
            cp.start()

    def wait(self):
        for cp in self.landings():
            cp.wait_recv()
        for cp in self.sends():
            cp.wait_send()
        for cp in self.mine():
            cp.wait()

    @staticmethod
    def semaphores(n_ops):
        return [pltpu.SemaphoreType.DMA((n_ops * (N_DEV - 1),)),
                pltpu.SemaphoreType.DMA((n_ops * (N_DEV - 1),)),
                pltpu.SemaphoreType.DMA((n_ops,))]


def _exchange_rows(slabs):
    def body(g_ref, recv_ref, send_sems, recv_sems, local_sems):
        exchange = _Exchange([g_ref], [recv_ref], send_sems, recv_sems, local_sems)
        exchange.start()
        exchange.wait()

    return pl.pallas_call(
        body,
        name="exchange_rows",
        out_shape=jax.ShapeDtypeStruct(slabs.shape, slabs.dtype),
        in_specs=[HBM_SPEC],
        out_specs=HBM_SPEC,
        scratch_shapes=_Exchange.semaphores(1),
    )(slabs)


def _sum_adamw(recvs, w, m, v, name):
    rows, cols = w.shape
    tr = min(rows, TR_ADAM)
    n_recv = len(recvs)

    def body(*refs):
        w_ref, m_ref, v_ref, g_out, d_out, m_out, v_out = refs[n_recv:]
        g = None
        for r_ref in refs[:n_recv]:
            for i in range(N_DEV):
                part = r_ref[i].astype(F32)
                g = part if g is None else g + part
        g = g[:, :cols]
        m_new = ADAM_B1 * m_ref[...] + (1.0 - ADAM_B1) * g
        v_new = ADAM_B2 * v_ref[...] + (1.0 - ADAM_B2) * (g * g)
        m_hat = m_new / (1.0 - ADAM_B1 ** ADAM_STEP)
        v_hat = v_new / (1.0 - ADAM_B2 ** ADAM_STEP)
        g_out[...] = g
        d_out[...] = -ADAM_LR * (m_hat / (jnp.sqrt(v_hat) + ADAM_EPS) + ADAM_WD * w_ref[...])
        m_out[...] = m_new
        v_out[...] = v_new

    row_spec = pl.BlockSpec((tr, cols), lambda i: (i, 0))
    shape = jax.ShapeDtypeStruct((rows, cols), F32)
    return pl.pallas_call(
        body,
        name=name,
        grid=(rows // tr,),
        in_specs=[pl.BlockSpec((N_DEV, tr, PACK_COLS), lambda i: (0, i, 0))] * n_recv + [row_spec] * 3,
        out_specs=[row_spec] * 4,
        out_shape=[shape] * 4,
        compiler_params=_params(),
    )(*recvs, w, m, v)


def _inproj(x, norm_g, w_in_p, other_shard):
    s = x.shape[0]
    tm = min(s, TM_ROW)
    nsteps = s // tm

    def body(x_ref, g_ref, w_ref, shard_ref, proj_ref, h_ref, gathered_ref, send_sems, recv_sems, local_sems):
        i = pl.program_id(0)

        def all_gather():
            return _Exchange([shard_ref], [gathered_ref], send_sems, recv_sems, local_sems, gather=True)

        @pl.when(i == 0)
        def _():
            all_gather().start()

        xf = x_ref[...]
        r = lax.rsqrt(jnp.mean(xf * xf, axis=-1, keepdims=True) + EPS)
        h = _mx(xf * r * g_ref[...])
        h_ref[...] = h
        for j in range(PROJ_W // SEG):
            cols = slice(j * SEG, (j + 1) * SEG)
            proj_ref[:, cols] = _dot(h, w_ref[:, cols])

        @pl.when(i == nsteps - 1)
        def _():
            all_gather().wait()

    return pl.pallas_call(
        body,
        name="inproj",
        grid=(nsteps,),
        in_specs=[
            pl.BlockSpec((tm, D_MODEL), lambda i: (i, 0)),
            pl.BlockSpec((1, D_MODEL), lambda i: (0, 0)),
            _resident((D_MODEL, PROJ_W)),
            HBM_SPEC,
        ],
        out_specs=[
            pl.BlockSpec((tm, PROJ_W), lambda i: (i, 0)),
            pl.BlockSpec((tm, D_MODEL), lambda i: (i, 0)),
            HBM_SPEC,
        ],
        out_shape=[
            jax.ShapeDtypeStruct((s, PROJ_W), F32),
            jax.ShapeDtypeStruct((s, D_MODEL), MXU_DTYPE),
            jax.ShapeDtypeStruct((N_DEV,) + other_shard.shape, other_shard.dtype),
        ],
        scratch_shapes=_Exchange.semaphores(1),
        compiler_params=_params(),
    )(x, norm_g, w_in_p, other_shard)


def _chunk_lower_mask(t):
    row = lax.broadcasted_iota(jnp.int32, (t, t), 0)
    col = lax.broadcasted_iota(jnp.int32, (t, t), 1)
    return ((row // HG_CHUNK) == (col // HG_CHUNK)) & (col <= row)


def _chunk_pos(t):
    return lax.broadcasted_iota(jnp.int32, (t, HEAD_DIM), 0) & (HG_CHUNK - 1)


def _cumsum_chunk(x, pos):
    sh = 1
    while sh < HG_CHUNK:
        x = x + jnp.where(pos >= sh, pltpu.roll(x, sh, 0), 0.0)
        sh *= 2
    return x


def _rcumsum_chunk(x, pos):
    t = x.shape[0]
    sh = 1
    while sh < HG_CHUNK:
        x = x + jnp.where(pos < HG_CHUNK - sh, pltpu.roll(x, t - sh, 0), 0.0)
        sh *= 2
    return x


def _chunk_total(x):
    t, w = x.shape
    tot = jnp.sum(x.reshape(t // HG_CHUNK, HG_CHUNK, w), axis=1, keepdims=True)
    return jnp.broadcast_to(tot, (t // HG_CHUNK, HG_CHUNK, w)).reshape(t, w)


def _hgrn_gates(hq, hf, lb_logits, pos):
    lb = _sigmoid(lb_logits[0:1, :] - lb_logits[1:2, :])
    sig = _sigmoid(hf)
    f = lb + (1.0 - lb) * sig
    sq = _sigmoid(hq)
    q = hq * sq
    k = 1.0 - f
    logf = jnp.log(f)
    bcum = _cumsum_chunk(logf, pos)
    blast = _chunk_total(logf)
    eb = jnp.exp(bcum)
    enb = jnp.exp(-bcum)
    eo = jnp.exp(blast - bcum)
    return dict(lb=lb, sig=sig, f=f, sq=sq, q=q, k=k, eb=eb, enb=enb, eo=eo,
                qi=q * eb, ki=k * enb, ko=k * eo, dec=jnp.exp(blast))


def _hgrn_fwd(proj, lb_logits, hg_norm_g):
    s = proj.shape[0]
    t = min(s, T_HGRN)
    nb = s // t
    nc = t // HG_CHUNK
    hw = HG_HEADS_PER_STEP * HEAD_DIM

    def body(hq_ref, hf_ref, hi_ref, hz_ref, lb_ref, g_ref, o_ref, ya_ref, st_ref, state, u_sc, stb_sc):
        b = pl.program_id(1)

        @pl.when(b == 0)
        def _():
            state[...] = jnp.zeros_like(state)

        lower = _chunk_lower_mask(t)
        pos = _chunk_pos(t)
        for hh in range(HG_HEADS_PER_STEP):
            cols = slice(hh * HEAD_DIM, (hh + 1) * HEAD_DIM)
            st = state[hh]
            st_ref[0, hh] = st
            gt = _hgrn_gates(hq_ref[:, cols], hf_ref[:, cols], lb_ref[:, cols], pos)
            vb = _mx(hi_ref[:, cols])
            qib, kib, kob = _mx(gt["qi"]), _mx(gt["ki"]), _mx(gt["ko"])
            a = jnp.where(lower, _dot_nt(qib, kib), 0.0)
            o_intra = _dot(_mx(a), vb)
            for c in range(nc):
                sl = slice(c * HG_CHUNK, (c + 1) * HG_CHUNK)
                u_sc[hh, c] = _dot_tn(vb[sl], kob[sl])
            for c in range(nc):
                stb_sc[hh, c] = _mx(st)
                st = st * gt["dec"][c * HG_CHUNK:c * HG_CHUNK + 1, :] + u_sc[hh, c]
            state[hh] = st
            outs = []
            for c in range(nc):
                sl = slice(c * HG_CHUNK, (c + 1) * HG_CHUNK)
                outs.append(o_intra[sl] + _dot_nt(qib[sl], stb_sc[hh, c]))
            o = jnp.concatenate(outs, axis=0)
            o_ref[:, cols] = o
            r = lax.rsqrt(jnp.mean(o * o, axis=-1, keepdims=True) + EPS)
            hz = hz_ref[:, cols]
            ya_ref[:, cols] = _mx((o * r * g_ref[...]) * (hz * _sigmoid(hz)))

    hsteps = HEADS // HG_HEADS_PER_STEP

    def seg(k):
        return pl.BlockSpec((t, hw), lambda h, b, k=k: (b, k * hsteps + h))

    return pl.pallas_call(
        body,
        name="hgrn_fwd",
        grid=(hsteps, nb),
        in_specs=[seg(0), seg(1), seg(2), seg(3),
                  pl.BlockSpec((2, hw), lambda h, b: (0, h)),
                  pl.BlockSpec((1, HEAD_DIM), lambda h, b: (0, 0))],
        out_specs=[
            pl.BlockSpec((t, hw), lambda h, b: (b, h)),
            pl.BlockSpec((t, hw), lambda h, b: (b, h)),
            pl.BlockSpec((1, HG_HEADS_PER_STEP, HEAD_DIM, HEAD_DIM), lambda h, b: (b, h, 0, 0)),
        ],
        out_shape=[
            jax.ShapeDtypeStruct((s, D_MODEL), F32),
            jax.ShapeDtypeStruct((s, D_MODEL), MXU_DTYPE),
            jax.ShapeDtypeStruct((nb, HEADS, HEAD_DIM, HEAD_DIM), F32),
        ],
        scratch_shapes=[pltpu.VMEM((HG_HEADS_PER_STEP, HEAD_DIM, HEAD_DIM), F32),
                        pltpu.VMEM((HG_HEADS_PER_STEP, nc, HEAD_DIM, HEAD_DIM), F32),
                        pltpu.VMEM((HG_HEADS_PER_STEP, nc, HEAD_DIM, HEAD_DIM), MXU_DTYPE)],
        compiler_params=_params(),
    )(proj, proj, proj, proj, lb_logits, hg_norm_g)


def _rope(x, cos, sin_a, sin_b):
    return x * cos + pltpu.roll(x, 96, 1) * sin_a + pltpu.roll(x, 32, 1) * sin_b


def _rope_t(d, cos, sin_a, sin_b):
    return d * cos + pltpu.roll(d * sin_a, 32, 1) + pltpu.roll(d * sin_b, 96, 1)


def _mla_prep(proj, q_a_g, kv_a_g, w_uq_p, w_kn, w_v, cos, sin_a, sin_b):
    s = proj.shape[0]
    tm = min(s, TM_ROW)

    def body(sm_ref, gq_ref, gk_ref, wq_ref, wkn_ref, wv_ref, cos_ref, sa_ref, sb_ref,
             q_ref, k_ref, v_ref, cqn_ref, ckvn_ref):
        small = sm_ref[...]
        cq = small[:, :Q_LORA]
        ckv = small[:, Q_LORA:Q_LORA + KV_LORA]
        krp = small[:, Q_LORA + KV_LORA:Q_LORA + KV_LORA + HEAD_DIM]
        rq = lax.rsqrt(jnp.mean(cq * cq, axis=-1, keepdims=True) + EPS)
        rk = lax.rsqrt(jnp.mean(ckv * ckv, axis=-1, keepdims=True) + EPS)
        cqn = _mx(cq * rq * gq_ref[...])
        ckvn = _mx(ckv * rk * gk_ref[...])
        cqn_ref[...] = cqn
        ckvn_ref[...] = ckvn
        q = _dot(cqn, wq_ref[...]) * Q_PRESCALE
        kn = _dot(ckvn, wkn_ref[...])
        v = _dot(ckvn, wv_ref[...])
        cos_t, sa, sb = cos_ref[...], sa_ref[...], sb_ref[...]
        kpe = _mx(_rope(krp, cos_t, sa, sb))
        ones_col = (lax.broadcasted_iota(jnp.int32, (tm, HEAD_DIM), 1) == 0).astype(MXU_DTYPE)
        for h in range(HEADS):
            lo = h * HEAD_PAD
            v_ref[:, lo:lo + HEAD_DIM] = _mx(v[:, h * HEAD_DIM:(h + 1) * HEAD_DIM])
            v_ref[:, lo + HEAD_DIM:lo + HEAD_PAD] = ones_col
            q_ref[:, lo:lo + HEAD_DIM] = _mx(q[:, lo:lo + HEAD_DIM])
            q_ref[:, lo + HEAD_DIM:lo + HEAD_PAD] = _mx(_rope(q[:, lo + HEAD_DIM:lo + HEAD_PAD], cos_t, sa, sb))
            k_ref[:, lo:lo + HEAD_DIM] = _mx(kn[:, h * HEAD_DIM:(h + 1) * HEAD_DIM])
            k_ref[:, lo + HEAD_DIM:lo + HEAD_PAD] = kpe

    def const(shape):
        return pl.BlockSpec(shape, lambda i: (0, 0))

    def rows(w):
        return pl.BlockSpec((tm, w), lambda i: (i, 0))

    return pl.pallas_call(
        body,
        name="mla_prep",
        grid=(s // tm,),
        in_specs=[
            pl.BlockSpec((tm, SEG), lambda i: (i, SMALL_SEG)),
            const((1, Q_LORA)), const((1, KV_LORA)),
            const((Q_LORA, HEADS * HEAD_PAD)), const((KV_LORA, D_MODEL)), const((KV_LORA, D_MODEL)),
            rows(HEAD_DIM), rows(HEAD_DIM), rows(HEAD_DIM),
        ],
        out_specs=[rows(HEADS * HEAD_PAD)] * 3 + [rows(Q_LORA), rows(KV_LORA)],
        out_shape=[
            jax.ShapeDtypeStruct((s, HEADS * HEAD_PAD), MXU_DTYPE),
            jax.ShapeDtypeStruct((s, HEADS * HEAD_PAD), MXU_DTYPE),
            jax.ShapeDtypeStruct((s, HEADS * HEAD_PAD), MXU_DTYPE),
            jax.ShapeDtypeStruct((s, Q_LORA), MXU_DTYPE),
            jax.ShapeDtypeStruct((s, KV_LORA), MXU_DTYPE),
        ],
        compiler_params=_params(),
    )(proj, q_a_g, kv_a_g, w_uq_p, w_kn, w_v, cos, sin_a, sin_b)


def _diag_mask(t):
    row = lax.broadcasted_iota(jnp.int32, (t, t), 0)
    col = lax.broadcasted_iota(jnp.int32, (t, t), 1)
    return row >= col


def _flash_fwd(q_all, k_all, v_all, proj):
    s = q_all.shape[0]
    t = min(s, T_ATT)
    n = s // t
    ts = t // ATT_SUB
    n_pairs = n * (n + 1) // 2

    def body(q_ref, k_ref, v_ref, mz_ref, ao_ref, yb_ref, mblk_ref, p_hbm, m_sc, acc_sc, stage, p_sems):
        head, qi = pl.program_id(0), pl.program_id(1)
        m_sc[...] = jnp.full_like(m_sc, -jnp.inf)
        acc_sc[...] = jnp.zeros_like(acc_sc)
        mblk_ref[...] = jnp.zeros_like(mblk_ref)
        lane = lax.broadcasted_iota(jnp.int32, (ts, HEAD_DIM), 1)
        first_block = head * n_pairs + qi * (qi + 1) // 2

        def p_copy(slot, pair, r):
            rows = pl.ds(r * ts, ts)
            return pltpu.make_async_copy(stage.at[slot, rows], p_hbm.at[head, pair, rows], p_sems.at[slot, r])

        def p_wait(slot):
            for r in range(ATT_SUB):
                p_copy(slot, 0, r).wait()

        def key_block(ki, diagonal):
            base = pl.multiple_of(ki * t, t)
            count = first_block + ki
            slot = lax.rem(count, P_SLOTS)
            sc, pb, alpha = {}, {}, {}

            @pl.when(count >= P_SLOTS)
            def _():
                p_wait(slot)

            if diagonal:
                stage[slot] = jnp.zeros((t, t), MXU_DTYPE)

            def width(r):
                return (r + 1) * ts if diagonal else t

            def scores(r):
                w = width(r)
                s_r = _dot_nt(q_ref[r * ts:(r + 1) * ts], k_ref[pl.ds(base, w), :])
                if diagonal:
                    row = lax.broadcasted_iota(jnp.int32, (ts, w), 0) + r * ts
                    col = lax.broadcasted_iota(jnp.int32, (ts, w), 1)
                    s_r = jnp.where(row >= col, s_r, -jnp.inf)
                sc[r] = s_r

            def softmax(r):
                rs = slice(r * ts, (r + 1) * ts)
                m_prev = m_sc[rs]
                m_new = jnp.maximum(m_prev, jnp.max(sc[r], axis=-1, keepdims=True))
                pb[r] = _mx(jnp.exp2(sc[r] - m_new))
                alpha[r] = jnp.exp2(m_prev - m_new)
                m_sc[rs] = m_new
                mblk_ref[rs] = jnp.where(lane == ki, m_new, mblk_ref[rs])
                stage[slot, rs, :width(r)] = pb[r]
                p_copy(slot, qi * (qi + 1) // 2 + ki, r).start()

            def weighted_values(r):
                rs = slice(r * ts, (r + 1) * ts)
                acc_sc[rs] = alpha[r] * acc_sc[rs] + _dot(pb[r], v_ref[pl.ds(base, width(r)), :])

            for step in range(ATT_SUB + 2):
                if step < ATT_SUB:
                    scores(step)
                if 1 <= step <= ATT_SUB:
                    softmax(step - 1)
                if step >= 2:
                    weighted_values(step - 2)

        def below_diagonal(ki, carry):
            key_block(ki, False)
            return carry

        lax.fori_loop(0, qi, below_diagonal, 0)
        key_block(qi, True)

        @pl.when((head == HEADS - 1) & (qi == n - 1))
        def _():
            for slot in range(min(P_SLOTS, HEADS * n_pairs)):
                p_wait(slot)

        acc = acc_sc[...]
        l = acc[:, HEAD_DIM:HEAD_DIM + 1]
        ao = acc[:, :HEAD_DIM] / l
        ao_ref[...] = ao
        lane_t = lax.broadcasted_iota(jnp.int32, (t, HEAD_DIM), 1)
        mblk_ref[...] = jnp.where(lane_t == LSE_LANE, m_sc[...] + jnp.log2(l), mblk_ref[...])
        mz = mz_ref[...]
        yb_ref[...] = _mx(ao * (mz * _sigmoid(mz)))

    q_map = lambda h, qi: (qi, h)
    return pl.pallas_call(
        body,
        name="flash_fwd",
        grid=(HEADS, n),
        in_specs=[
            pl.BlockSpec((t, HEAD_PAD), q_map),
            pl.BlockSpec((s, HEAD_PAD), lambda h, qi: (0, h)),
            pl.BlockSpec((s, HEAD_PAD), lambda h, qi: (0, h)),
            pl.BlockSpec((t, HEAD_DIM), lambda h, qi: (qi, MZ_SEG * HEADS + h)),
        ],
        out_specs=[pl.BlockSpec((t, HEAD_DIM), q_map)] * 3 + [HBM_SPEC],
        out_shape=[
            jax.ShapeDtypeStruct((s, D_MODEL), F32),
            jax.ShapeDtypeStruct((s, D_MODEL), MXU_DTYPE),
            jax.ShapeDtypeStruct((s, D_MODEL), F32),
            jax.ShapeDtypeStruct((HEADS, n_pairs, t, t), MXU_DTYPE),
        ],
        scratch_shapes=[
            pltpu.VMEM((t, 1), F32),
            pltpu.VMEM((t, HEAD_PAD), F32),
            pltpu.VMEM((P_SLOTS, t, t), MXU_DTYPE),
            pltpu.SemaphoreType.DMA((P_SLOTS, ATT_SUB)),
        ],
        compiler_params=_params(),
    )(q_all, k_all, v_all, proj)


def _merge_fwd_bwd(x, target, ya, yb, ao, proj, b_gate, final_g, w_pa, w_pb, w_out):
    s = x.shape[0]
    tm = min(s, TM_ROW)

    def body(x_ref, t_ref, ya_ref, yb_ref, ao_ref, mz_ref, g0_ref, g1_ref, bg_ref, fg_ref, wpa_ref, wpb_ref, wo_ref,
             dx2_ref, dya_ref, dao_ref, dmz_ref, dg0_ref, dg1_ref, mb_ref, dpab_ref, dpbb_ref, dx2b_ref,
             loss_ref, dfg_ref, dbg_ref):
        i = pl.program_id(0)

        @pl.when(i == 0)
        def _():
            loss_ref[...] = jnp.zeros_like(loss_ref)
            dfg_ref[...] = jnp.zeros_like(dfg_ref)
            dbg_ref[...] = jnp.zeros_like(dbg_ref)

        pa = _dot(ya_ref[...], wpa_ref[...])
        pb = _dot(yb_ref[...], wpb_ref[...])
        bg = bg_ref[...]
        g0 = _sigmoid(g0_ref[...] + bg[:, :D_MODEL])
        g1 = _sigmoid(g1_ref[...] + bg[:, D_MODEL:])
        merged = g0 * pa + g1 * pb
        mb = _mx(merged)
        mb_ref[...] = mb
        x2 = x_ref[...] + _dot(mb, wo_ref[...])
        r = lax.rsqrt(jnp.mean(x2 * x2, axis=-1, keepdims=True) + EPS)
        xn = x2 * r
        fg = fg_ref[...]
        diff = xn * fg - t_ref[...]
        loss_ref[...] += 0.5 * jnp.sum(jnp.mean(diff * diff, axis=-1, keepdims=True))
        dy = diff * (1.0 / D_MODEL)
        dfg_ref[...] += _bcast_rows(jnp.sum(dy * xn, axis=0, keepdims=True), 8)
        tt = dy * fg
        dx2 = r * (tt - xn * jnp.mean(tt * xn, axis=-1, keepdims=True))
        dx2_ref[...] = dx2
        dx2b = _mx(dx2)
        dx2b_ref[...] = dx2b
        dmerged = _dot_nt(dx2b, wo_ref[...])
        dpa = _mx(dmerged * g0)
        dpb = _mx(dmerged * g1)
        dpab_ref[...] = dpa
        dpbb_ref[...] = dpb
        dg0 = dmerged * pa * (g0 * (1.0 - g0))
        dg1 = dmerged * pb * (g1 * (1.0 - g1))
        dg0_ref[...] = _mx(dg0)
        dg1_ref[...] = _mx(dg1)
        dbg_ref[:, :D_MODEL] += _bcast_rows(jnp.sum(dg0, axis=0, keepdims=True), 8)
        dbg_ref[:, D_MODEL:] += _bcast_rows(jnp.sum(dg1, axis=0, keepdims=True), 8)
        dya_ref[...] = _dot_nt(dpa, wpa_ref[...])
        dyb = _dot_nt(dpb, wpb_ref[...])
        mz = mz_ref[...]
        sg = _sigmoid(mz)
        dao_ref[...] = _mx(dyb * (mz * sg))
        dmz_ref[...] = _mx(dyb * ao_ref[...] * (sg + mz * sg * (1.0 - sg)))

    def rows(w=D_MODEL):
        return pl.BlockSpec((tm, w), lambda i: (i, 0))

    def const(shape):
        return pl.BlockSpec(shape, lambda i: (0, 0))

    def seg(k):
        return pl.BlockSpec((tm, SEG), lambda i: (i, k))

    f32 = jax.ShapeDtypeStruct((s, D_MODEL), F32)
    b16 = jax.ShapeDtypeStruct((s, D_MODEL), MXU_DTYPE)
    return pl.pallas_call(
        body,
        name="merge_fwd_bwd",
        grid=(s // tm,),
        in_specs=[
            rows(), rows(), rows(), rows(), rows(),
            seg(MZ_SEG), seg(GL_SEG), seg(GL_SEG + 1),
            const((1, 2 * D_MODEL)), const((1, D_MODEL)),
            _resident((D_MODEL, D_MODEL)), _resident((D_MODEL, D_MODEL)), _resident((D_MODEL, D_MODEL)),
        ],
        out_specs=[rows()] * 10 + [const((8, HEAD_DIM)), const((8, D_MODEL)), const((8, 2 * D_MODEL))],
        out_shape=[f32, f32, b16, b16, b16, b16, b16, b16, b16, b16,
                   jax.ShapeDtypeStruct((8, HEAD_DIM), F32),
                   jax.ShapeDtypeStruct((8, D_MODEL), F32),
                   jax.ShapeDtypeStruct((8, 2 * D_MODEL), F32)],
        compiler_params=_params(),
    )(x, target, ya, yb, ao, proj, proj, proj, b_gate, final_g, w_pa, w_pb, w_out)


def _flash_bwd(q_all, k_all, v_all, dao, ao, mblk, p_all, slab_sets):
    s = q_all.shape[0]
    t = min(s, T_ATT_BWD)
    n = s // t
    pairs = [(ki, qi) for ki in range(n) for qi in range(ki, n)]
    ki_list = jnp.asarray([p[0] for p in pairs], jnp.int32)
    qi_list = jnp.asarray([p[1] for p in pairs], jnp.int32)
    p_list = jnp.asarray([qi * (qi + 1) // 2 + ki for ki, qi in pairs], jnp.int32)
    n_ops = len(slab_sets)

    def body(ki_ref, qi_ref, pidx_ref, q_ref, k_ref, v_ref, do_ref, ao_ref, mblk_ref, p_ref, *rest):
        g_refs = rest[:n_ops]
        dq_ref, dk_ref, dv_ref = rest[n_ops:n_ops + 3]
        recv_refs = rest[n_ops + 3:2 * n_ops + 3]
        dk_acc, dv_acc, send_sems, recv_sems, local_sems = rest[2 * n_ops + 3:]
        head, step = pl.program_id(0), pl.program_id(1)
        ki, qi = ki_ref[step], qi_ref[step]

        @pl.when((head == 0) & (step == 0))
        def _():
            _Exchange(g_refs, recv_refs, send_sems, recv_sems, local_sems).start()

        @pl.when(qi == ki)
        def _():
            dk_acc[...] = jnp.zeros_like(dk_acc)
            dv_acc[...] = jnp.zeros_like(dv_acc)

        @pl.when(ki == 0)
        def _():
            dq_ref[pl.ds(pl.multiple_of(qi * t, t), t), :] = jnp.zeros((t, HEAD_PAD), F32)

        def pair(masked):
            nsub = ATT_SUB if masked else ATT_SUB_BWD
            ts = t // nsub
            dk_parts, dv_parts = [], []
            for r in range(nsub):
                rs = slice(r * ts, (r + 1) * ts)
                w = (r + 1) * ts if masked else t
                k = k_ref[:w]
                v = v_ref[:w]
                q = q_ref[rs]
                lane = lax.broadcasted_iota(jnp.int32, (ts, HEAD_DIM), 1)
                stats = mblk_ref[rs]
                m_blk = jnp.max(jnp.where(lane == ki, stats, -jnp.inf), axis=-1, keepdims=True)
                lse = jnp.max(jnp.where(lane == LSE_LANE, stats, -jnp.inf), axis=-1, keepdims=True)
                factor = jnp.exp2(m_blk - lse)
                p_st = p_ref[0, 0, rs, :w]
                do = do_ref[rs]
                do_f = do.astype(F32)
                delta = jnp.sum(do_f * ao_ref[rs], axis=-1, keepdims=True)
                dv_part = _dot_tn(p_st, _mx(do_f * factor))
                ds = p_st * _mx((_dot_nt(do, v) - delta) * factor)
                dk_part = _dot_tn(ds, q)
                rows = pl.ds(pl.multiple_of(qi * t + r * ts, ts), ts)
                dq_ref[rows, :] += _dot(ds, k)
                if masked:
                    dk_acc[:w] += dk_part
                    dv_acc[:w] += dv_part
                else:
                    dk_parts.append(dk_part)
                    dv_parts.append(dv_part)

            if not masked:
                dk_acc[...] += sum(dk_parts[1:], dk_parts[0])
                dv_acc[...] += sum(dv_parts[1:], dv_parts[0])

        @pl.when(qi == ki)
        def _():
            pair(True)

        @pl.when(qi > ki)
        def _():
            pair(False)

        @pl.when(qi == n - 1)
        def _():
            dk_ref[...] = _mx(dk_acc[...] * LN2)
            dv_ref[...] = _mx(dv_acc[...])

        @pl.when((head == HEADS - 1) & (step == len(pairs) - 1))
        def _():
            _Exchange(g_refs, recv_refs, send_sems, recv_sems, local_sems).wait()

    q_map = lambda h, p, ki_ref, qi_ref, pidx_ref: (qi_ref[p], h)
    kv_map = lambda h, p, ki_ref, qi_ref, pidx_ref: (ki_ref[p], h)
    grid_spec = pltpu.PrefetchScalarGridSpec(
        num_scalar_prefetch=3,
        grid=(HEADS, len(pairs)),
        in_specs=[
            pl.BlockSpec((t, HEAD_PAD), q_map),
            pl.BlockSpec((t, HEAD_PAD), kv_map),
            pl.BlockSpec((t, HEAD_DIM), lambda h, p, ki_ref, qi_ref, pidx_ref: (ki_ref[p], 2 * h)),
            pl.BlockSpec((t, HEAD_DIM), q_map),
            pl.BlockSpec((t, HEAD_DIM), q_map),
            pl.BlockSpec((t, HEAD_DIM), q_map),
            pl.BlockSpec((1, 1, t, t), lambda h, p, ki_ref, qi_ref, pidx_ref: (h, pidx_ref[p], 0, 0)),
        ] + [HBM_SPEC] * n_ops,
        out_specs=[
            pl.BlockSpec((s, HEAD_PAD), lambda h, p, ki_ref, qi_ref, pidx_ref: (0, h)),
            pl.BlockSpec((t, HEAD_PAD), kv_map),
            pl.BlockSpec((t, HEAD_DIM), kv_map),
        ] + [HBM_SPEC] * n_ops,
        scratch_shapes=[pltpu.VMEM((t, HEAD_PAD), F32), pltpu.VMEM((t, HEAD_DIM), F32)]
        + _Exchange.semaphores(n_ops),
    )
    outs = pl.pallas_call(
        body,
        name="flash_bwd",
        grid_spec=grid_spec,
        out_shape=[
            jax.ShapeDtypeStruct((s, HEADS * HEAD_PAD), F32),
            jax.ShapeDtypeStruct((s, HEADS * HEAD_PAD), MXU_DTYPE),
            jax.ShapeDtypeStruct((s, D_MODEL), MXU_DTYPE),
        ] + [jax.ShapeDtypeStruct(a.shape, a.dtype) for a in slab_sets],
        compiler_params=_params(VMEM_LIMIT_BIG),
    )(ki_list, qi_list, p_list, q_all, k_all, v_all, dao, ao, mblk, p_all, *slab_sets)
    return outs[0], outs[1], outs[2], outs[3:]


def _mla_prep_bwd(dq_all, dk_all, dv_all, proj, q_a_g, kv_a_g, w_uq_p, w_kn, w_v, cos, sin_a, sin_b):
    s = proj.shape[0]
    tm = min(s, TM_ROW)

    def body(dq_ref, dk_ref, dv_ref, sm_ref, gq_ref, gk_ref, wq_ref, wkn_ref, wv_ref, cos_ref, sa_ref, sb_ref,
             dsm_ref, dqf_ref, dkn_ref, dvb_ref, dgq_ref, dgk_ref):
        i = pl.program_id(0)

        @pl.when(i == 0)
        def _():
            dgq_ref[...] = jnp.zeros_like(dgq_ref)
            dgk_ref[...] = jnp.zeros_like(dgk_ref)

        cos_t, sa, sb = cos_ref[...], sa_ref[...], sb_ref[...]
        dkpe = jnp.zeros((tm, HEAD_DIM), F32)
        for h in range(HEADS):
            lo = h * HEAD_PAD
            dqf_ref[:, lo:lo + HEAD_DIM] = _mx(dq_ref[:, lo:lo + HEAD_DIM] * QK_SCALE)
            dqf_ref[:, lo + HEAD_DIM:lo + HEAD_PAD] = _mx(
                _rope_t(dq_ref[:, lo + HEAD_DIM:lo + HEAD_PAD] * QK_SCALE, cos_t, sa, sb))
            dkn_ref[:, h * HEAD_DIM:(h + 1) * HEAD_DIM] = dk_ref[:, lo:lo + HEAD_DIM]
            dkpe = dkpe + dk_ref[:, lo + HEAD_DIM:lo + HEAD_PAD].astype(F32)
        dkr = _rope_t(dkpe, cos_t, sa, sb)
        dvb = dv_ref[...]
        dvb_ref[...] = dvb
        dcqn = _dot_nt(dqf_ref[...], wq_ref[...])
        dckvn = _dot_nt(dkn_ref[...], wkn_ref[...]) + _dot_nt(dvb, wv_ref[...])

        small = sm_ref[...]
        cq = small[:, :Q_LORA]
        ckv = small[:, Q_LORA:Q_LORA + KV_LORA]
        rq = lax.rsqrt(jnp.mean(cq * cq, axis=-1, keepdims=True) + EPS)
        rk = lax.rsqrt(jnp.mean(ckv * ckv, axis=-1, keepdims=True) + EPS)
        cqh = cq * rq
        ckh = ckv * rk
        dgq_ref[...] += _bcast_rows(jnp.sum(dcqn * cqh, axis=0, keepdims=True), 8)
        dgk_ref[...] += _bcast_rows(jnp.sum(dckvn * ckh, axis=0, keepdims=True), 8)
        tq = dcqn * gq_ref[...]
        tk = dckvn * gk_ref[...]
        dcq = rq * (tq - cqh * jnp.mean(tq * cqh, axis=-1, keepdims=True))
        dckv = rk * (tk - ckh * jnp.mean(tk * ckh, axis=-1, keepdims=True))
        dsm_ref[:, :Q_LORA] = _mx(dcq)
        dsm_ref[:, Q_LORA:Q_LORA + KV_LORA] = _mx(dckv)
        dsm_ref[:, Q_LORA + KV_LORA:Q_LORA + KV_LORA + HEAD_DIM] = _mx(dkr)
        dsm_ref[:, Q_LORA + KV_LORA + HEAD_DIM:] = jnp.zeros((tm, SEG - Q_LORA - KV_LORA - HEAD_DIM), MXU_DTYPE)

    def const(shape):
        return pl.BlockSpec(shape, lambda i: (0, 0))

    def rows(w):
        return pl.BlockSpec((tm, w), lambda i: (i, 0))

    return pl.pallas_call(
        body,
        name="mla_prep_bwd",
        grid=(s // tm,),
        in_specs=[
            rows(HEADS * HEAD_PAD), rows(HEADS * HEAD_PAD), rows(D_MODEL),
            pl.BlockSpec((tm, SEG), lambda i: (i, SMALL_SEG)),
            const((1, Q_LORA)), const((1, KV_LORA)),
            const((Q_LORA, HEADS * HEAD_PAD)), const((KV_LORA, D_MODEL)), const((KV_LORA, D_MODEL)),
            rows(HEAD_DIM), rows(HEAD_DIM), rows(HEAD_DIM),
        ],
        out_specs=[rows(SEG), rows(HEADS * HEAD_PAD), rows(D_MODEL), rows(D_MODEL),
                   const((8, Q_LORA)), const((8, KV_LORA))],
        out_shape=[
            jax.ShapeDtypeStruct((s, SEG), MXU_DTYPE),
            jax.ShapeDtypeStruct((s, HEADS * HEAD_PAD), MXU_DTYPE),
            jax.ShapeDtypeStruct((s, D_MODEL), MXU_DTYPE),
            jax.ShapeDtypeStruct((s, D_MODEL), MXU_DTYPE),
            jax.ShapeDtypeStruct((8, Q_LORA), F32),
            jax.ShapeDtypeStruct((8, KV_LORA), F32),
        ],
        compiler_params=_params(),
    )(dq_all, dk_all, dv_all, proj, q_a_g, kv_a_g, w_uq_p, w_kn, w_v, cos, sin_a, sin_b)


def _hgrn_bwd(proj, lb_logits, hg_norm_g, o_all, dya, states):
    s = proj.shape[0]
    t = min(s, T_HGRN)
    nb = s // t
    nc = t // HG_CHUNK

    def body(hq_ref, hf_ref, hi_ref, hz_ref, lb_ref, g_ref, o_ref, dya_ref, st_ref,
             dhq_ref, dhf_ref, dhi_ref, dhz_ref, dlb_ref, dg_ref, dstate, u_sc, g_sc, stf_sc, stb_sc, dstb_sc):
        h, b = pl.program_id(0), pl.program_id(1)

        @pl.when(b == 0)
        def _():
            dstate[...] = jnp.zeros_like(dstate)
            dlb_ref[...] = jnp.zeros_like(dlb_ref)

        @pl.when((b == 0) & (h == 0))
        def _():
            dg_ref[...] = jnp.zeros_like(dg_ref)

        lower = _chunk_lower_mask(t)
        pos = _chunk_pos(t)
        ghg = g_ref[...]
        for hh in range(HG_HEADS_PER_STEP):
            cols = slice(hh * HEAD_DIM, (hh + 1) * HEAD_DIM)
            hq, hf, hz = hq_ref[:, cols], hf_ref[:, cols], hz_ref[:, cols]
            gt = _hgrn_gates(hq, hf, lb_ref[:, cols], pos)
            vb = _mx(hi_ref[:, cols])
            qi, ki, ko = gt["qi"], gt["ki"], gt["ko"]
            qib, kib, kob = _mx(qi), _mx(ki), _mx(ko)

            o = o_ref[:, cols]
            sz = _sigmoid(hz)
            r = lax.rsqrt(jnp.mean(o * o, axis=-1, keepdims=True) + EPS)
            on = o * r
            dya_t = dya_ref[:, cols]
            don = dya_t * (hz * sz)
            dhz_ref[:, cols] = _mx(dya_t * (on * ghg) * (sz + hz * sz * (1.0 - sz)))
            dg_ref[...] += _bcast_rows(jnp.sum(don * on, axis=0, keepdims=True), 8)
            tt = don * ghg
            do = r * (tt - on * jnp.mean(tt * on, axis=-1, keepdims=True))
            dob = _mx(do)

            for c in range(nc):
                sl = slice(c * HG_CHUNK, (c + 1) * HG_CHUNK)
                u_sc[hh, c] = _dot_tn(vb[sl], kob[sl])
                g_sc[hh, c] = _dot_tn(dob[sl], qib[sl])

            st = st_ref[0, hh]
            for c in range(nc):
                stf_sc[hh, c] = st
                stb_sc[hh, c] = _mx(st)
                if c < nc - 1:
                    st = st * gt["dec"][c * HG_CHUNK:c * HG_CHUNK + 1, :] + u_sc[hh, c]

            dst = dstate[hh]
            dd_parts = [None] * nc
            for c in reversed(range(nc)):
                dec = gt["dec"][c * HG_CHUNK:c * HG_CHUNK + 1, :]
                dstb_sc[hh, c] = _mx(dst)
                dd_parts[c] = _bcast_rows(jnp.sum(dst * stf_sc[hh, c], axis=0, keepdims=True) * dec, HG_CHUNK)
                dst = dst * dec + g_sc[hh, c]
            dstate[hh] = dst

            a = jnp.where(lower, _dot_nt(qib, kib), 0.0)
            da = _mx(jnp.where(lower, _dot_nt(dob, vb), 0.0))
            dqi_intra = _dot(da, kib)
            dki = _dot_tn(da, qib)
            dv_intra = _dot_tn(_mx(a), dob)

            dqi_parts, dko_parts, dv_parts = [None] * nc, [None] * nc, [None] * nc
            for c in range(nc):
                sl = slice(c * HG_CHUNK, (c + 1) * HG_CHUNK)
                dv_parts[c] = dv_intra[sl] + _dot_nt(kob[sl], dstb_sc[hh, c])
                dko_parts[c] = _dot(vb[sl], dstb_sc[hh, c])
                dqi_parts[c] = dqi_intra[sl] + _dot(dob[sl], stb_sc[hh, c])
            dqi = jnp.concatenate(dqi_parts, axis=0)
            dko = jnp.concatenate(dko_parts, axis=0)
            dv = jnp.concatenate(dv_parts, axis=0)
            dd = jnp.concatenate(dd_parts, axis=0)

            dq = dqi * gt["eb"]
            dk = dki * gt["enb"] + dko * gt["eo"]
            db = dqi * qi - dki * ki - dko * ko
            dlogf = _rcumsum_chunk(db, pos) + _chunk_total(dko * ko) + dd
            df = dlogf / gt["f"] - dk
            lb, sig, sq = gt["lb"], gt["sig"], gt["sq"]
            dhf_ref[:, cols] = _mx(df * (1.0 - lb) * (sig * (1.0 - sig)))
            dhq_ref[:, cols] = _mx(dq * (sq + hq * sq * (1.0 - sq)))
            dhi_ref[:, cols] = _mx(dv)
            dlb = jnp.sum(df * (1.0 - sig), axis=0, keepdims=True) * (lb * (1.0 - lb))
            dlb_ref[:, cols] += jnp.concatenate([dlb, -dlb], axis=0)

    hw = HG_HEADS_PER_STEP * HEAD_DIM
    hsteps = HEADS // HG_HEADS_PER_STEP

    def seg(k):
        return pl.BlockSpec((t, hw), lambda h, b, k=k: (nb - 1 - b, k * hsteps + h))

    blk = pl.BlockSpec((t, hw), lambda h, b: (nb - 1 - b, h))
    b16 = jax.ShapeDtypeStruct((s, D_MODEL), MXU_DTYPE)
    return pl.pallas_call(
        body,
        name="hgrn_bwd",
        grid=(hsteps, nb),
        in_specs=[seg(0), seg(1), seg(2), seg(3),
                  pl.BlockSpec((2, hw), lambda h, b: (0, h)),
                  pl.BlockSpec((1, HEAD_DIM), lambda h, b: (0, 0)),
                  blk, blk,
                  pl.BlockSpec((1, HG_HEADS_PER_STEP, HEAD_DIM, HEAD_DIM), lambda h, b: (nb - 1 - b, h, 0, 0))],
        out_specs=[blk, blk, blk, blk,
                   pl.BlockSpec((2, hw), lambda h, b: (0, h)),
                   pl.BlockSpec((8, HEAD_DIM), lambda h, b: (0, 0))],
        out_shape=[b16, b16, b16, b16,
                   jax.ShapeDtypeStruct((2, D_MODEL), F32),
                   jax.ShapeDtypeStruct((8, HEAD_DIM), F32)],
        scratch_shapes=[pltpu.VMEM((HG_HEADS_PER_STEP, HEAD_DIM, HEAD_DIM), F32)]
        + [pltpu.VMEM((HG_HEADS_PER_STEP, nc, HEAD_DIM, HEAD_DIM), F32)] * 3
        + [pltpu.VMEM((HG_HEADS_PER_STEP, nc, HEAD_DIM, HEAD_DIM), MXU_DTYPE)] * 2,
        compiler_params=_params(),
    )(proj, proj, proj, proj, lb_logits, hg_norm_g, o_all, dya, states)


def _dh_bwd(segs, w_in_p, x, dx2, norm_g, late_slab, late_recv_init, misc_slabs):
    s = x.shape[0]
    tm = min(s, TM_ROW)
    nseg = len(segs)
    nsteps = s // tm
    late_xyc = ((LATE_DEV >> 2) & 1, (LATE_DEV >> 1) & 1, LATE_DEV & 1)

    def body(*refs):
        seg_refs = refs[:nseg]
        (w_ref, x_ref, dx2_ref, g_ref, late_ref, _, misc_ref,
         gx_ref, dng_ref, late_recv_ref, misc_recv_ref,
         dp_buf, send_sems, recv_sems, local_sems, late_send, late_recvs, late_local) = refs[nseg:]
        i = pl.program_id(0)
        me = 4 * lax.axis_index("x") + 2 * lax.axis_index("y") + lax.axis_index("c")

        def misc_exchange():
            return _Exchange([misc_ref], [misc_recv_ref], send_sems, recv_sems, local_sems)

        def late_copy(sender):
            return pltpu.make_async_remote_copy(
                src_ref=late_ref.at[0], dst_ref=late_recv_ref.at[sender], send_sem=late_send,
                recv_sem=late_recvs.at[(sender ^ LATE_DEV) - 1], device_id=late_xyc, device_id_type=MESH)

        def late_own():
            return pltpu.make_async_copy(late_ref.at[0], late_recv_ref.at[LATE_DEV], late_local)

        @pl.when(i == 0)
        def _():
            dng_ref[...] = jnp.zeros_like(dng_ref)
            misc_exchange().start()

        @pl.when((i == 0) & (me != LATE_DEV))
        def _():
            late_copy(me).start()

        @pl.when((i == 0) & (me == LATE_DEV))
        def _():
            late_own().start()

        for k, sref in enumerate(seg_refs):
            dp_buf[:, k * SEG:(k + 1) * SEG] = sref[...]
        dh = _dot_nt(dp_buf[...], w_ref[...])
        xf = x_ref[...]
        r = lax.rsqrt(jnp.mean(xf * xf, axis=-1, keepdims=True) + EPS)
        xh = xf * r
        dng_ref[...] += _bcast_rows(jnp.sum(dh * xh, axis=0, keepdims=True), 8)
        tt = dh * g_ref[...]
        gx_ref[...] = dx2_ref[...] + r * (tt - xh * jnp.mean(tt * xh, axis=-1, keepdims=True))

        @pl.when(i == nsteps - 1)
        def _():
            misc_exchange().wait()

        @pl.when((i == nsteps - 1) & (me != LATE_DEV))
        def _():
            late_copy(me).wait_send()

        @pl.when((i == nsteps - 1) & (me == LATE_DEV))
        def _():
            for k in range(1, N_DEV):
                late_copy(LATE_DEV ^ k).wait_recv()
            late_own().wait()

    rows = pl.BlockSpec((tm, D_MODEL), lambda i: (i, 0))
    return pl.pallas_call(
        body,
        name="dh_bwd",
        grid=(nsteps,),
        in_specs=[pl.BlockSpec((tm, SEG), lambda i: (i, 0))] * nseg + [
            _resident((D_MODEL, PROJ_W)),
            rows, rows,
            pl.BlockSpec((1, D_MODEL), lambda i: (0, 0)),
            HBM_SPEC, HBM_SPEC, HBM_SPEC,
        ],
        out_specs=[rows, pl.BlockSpec((8, D_MODEL), lambda i: (0, 0)), HBM_SPEC, HBM_SPEC],
        out_shape=[jax.ShapeDtypeStruct((s, D_MODEL), F32), jax.ShapeDtypeStruct((8, D_MODEL), F32),
                   jax.ShapeDtypeStruct(late_recv_init.shape, late_recv_init.dtype),
                   jax.ShapeDtypeStruct(misc_slabs.shape, misc_slabs.dtype)],
        input_output_aliases={nseg + 5: 2},
        scratch_shapes=[pltpu.VMEM((tm, PROJ_W), MXU_DTYPE)] + _Exchange.semaphores(1)
        + [pltpu.SemaphoreType.DMA, pltpu.SemaphoreType.DMA((N_DEV - 1,)), pltpu.SemaphoreType.DMA],
        compiler_params=_params(),
    )(*segs, w_in_p, x, dx2, norm_g, late_slab, late_recv_init, misc_slabs)


def _matmul_tn(a, b, name, out_dtype=F32):
    s, m = a.shape
    n = b.shape[1]
    ts = min(s, TS_TN)
    tn = min(n, SEG)
    nk = s // ts

    def body(a_ref, b_ref, o_ref, acc):
        k = pl.program_id(1)
        part = _dot_tn(a_ref[...], b_ref[...])

        @pl.when(k == 0)
        def _():
            acc[...] = part

        @pl.when(k > 0)
        def _():
            acc[...] += part

        @pl.when(k == nk - 1)
        def _():
            o_ref[...] = acc[...].astype(out_dtype)

    return pl.pallas_call(
        body,
        name=name,
        grid=(n // tn, nk),
        in_specs=[pl.BlockSpec((ts, m), lambda j, k: (k, 0)), pl.BlockSpec((ts, tn), lambda j, k: (k, j))],
        out_specs=pl.BlockSpec((m, tn), lambda j, k: (0, j)),
        out_shape=jax.ShapeDtypeStruct((m, n), out_dtype),
        scratch_shapes=[pltpu.VMEM((m, tn), F32)],
        compiler_params=_params(),
    )(a, b)


def _w_in_pieces():
    per = IN_COLS // N_DEV
    pad_at = SMALL_SEG * SEG + Q_LORA + KV_LORA + QK_ROPE
    pieces = []
    for j in range(N_DEV):
        u0, u1 = j * per, (j + 1) * per
        cuts = [u0] + ([pad_at] if u0 < pad_at < u1 else []) + [u1]
        for a, b in zip(cuts[:-1], cuts[1:]):
            pieces.append((j, a - u0, b - u0, a if a < pad_at else a + PROJ_W - IN_COLS))
    return pad_at, pieces


def _assemble_w_in(gathered):
    tr = TM_ROW
    pad_at, pieces = _w_in_pieces()

    def body(in_ref, out_ref):
        out_ref[:, pad_at:pad_at + PROJ_W - IN_COLS] = jnp.zeros((tr, PROJ_W - IN_COLS), gathered.dtype)
        for j, a, b, p0 in pieces:
            out_ref[:, p0:p0 + b - a] = in_ref[j, :, a:b]

    return pl.pallas_call(
        body,
        name="assemble_w_in",
        grid=(D_MODEL // tr,),
        in_specs=[pl.BlockSpec((N_DEV, tr, PACK_COLS), lambda i: (0, i, 0))],
        out_specs=pl.BlockSpec((tr, PROJ_W), lambda i: (i, 0)),
        out_shape=jax.ShapeDtypeStruct((D_MODEL, PROJ_W), gathered.dtype),
        compiler_params=_params(),
    )(gathered)


def _scatter_dw_in(dw_segs, devs, name):
    tr = TM_ROW
    _, pieces = _w_in_pieces()
    per = IN_COLS // N_DEV
    seg_ids = sorted(dw_segs)
    nseg = len(seg_ids)

    def body(*refs):
        out_ref, buf = refs[nseg:]
        for k in range(PROJ_W // SEG):
            if k in seg_ids:
                buf[:, k * SEG:(k + 1) * SEG] = refs[seg_ids.index(k)][...]
            else:
                buf[:, k * SEG:(k + 1) * SEG] = jnp.zeros((tr, SEG), F32)
        for slot, dev in enumerate(devs):
            out_ref[slot, :, per:] = jnp.zeros((tr, PACK_COLS - per), TRANSPORT_DTYPE)
            for j, a, b, p0 in pieces:
                if j == dev:
                    out_ref[slot, :, a:b] = buf[:, p0:p0 + b - a].astype(TRANSPORT_DTYPE)

    return pl.pallas_call(
        body,
        name=name,
        grid=(D_MODEL // tr,),
        in_specs=[pl.BlockSpec((tr, SEG), lambda i: (i, 0))] * nseg,
        out_specs=pl.BlockSpec((len(devs), tr, PACK_COLS), lambda i: (0, i, 0)),
        out_shape=jax.ShapeDtypeStruct((len(devs), D_MODEL, PACK_COLS), TRANSPORT_DTYPE),
        scratch_shapes=[pltpu.VMEM((tr, PROJ_W), F32)],
        compiler_params=_params(),
    )(*[dw_segs[k] for k in seg_ids])


def _rope_tables(s):
    inv = ROPE_THETA ** (-jnp.arange(0, QK_ROPE, 2, dtype=F32) / QK_ROPE)
    ang = jnp.arange(s, dtype=F32)[:, None] * inv[None, :]
    cos, sin = jnp.cos(ang), jnp.sin(ang)
    z32 = jnp.zeros_like(cos)
    z64 = jnp.zeros((s, HEAD_DIM - QK_ROPE), F32)
    cos_t = jnp.concatenate([cos, cos, z64], axis=1)
    sin_a = jnp.concatenate([-sin, z32, z64], axis=1)
    sin_b = jnp.concatenate([z32, sin, z64], axis=1)
    return cos_t, sin_a, sin_b


def _pack_misc(w_uq, w_ukv, norm_g, b_gate, lb_logits, hg_norm_g, q_a_g, kv_a_g, final_norm_g, extra):
    misc = jnp.concatenate([hg_norm_g.reshape(-1), q_a_g.reshape(-1), kv_a_g.reshape(-1), extra.reshape(-1),
                            jnp.zeros((PACK_COLS - HEAD_DIM - Q_LORA - KV_LORA - 1,), F32)])
    return jnp.concatenate([w_uq.reshape(ROWS_W_UQ, PACK_COLS), w_ukv.reshape(ROWS_W_UKV, PACK_COLS),
                            norm_g.reshape(1, -1), b_gate.reshape(2, -1), lb_logits.reshape(2, -1),
                            misc.reshape(1, -1), final_norm_g.reshape(1, -1), jnp.zeros((1, PACK_COLS), F32)], axis=0)


def _unpack_misc(p):
    sm = p[ROWS_W_UQ + ROWS_W_UKV:]
    misc = sm[5]
    return dict(
        w_uq=p[:ROWS_W_UQ].reshape(1, Q_LORA, QK_DIM),
        w_ukv=p[ROWS_W_UQ:ROWS_W_UQ + ROWS_W_UKV].reshape(1, KV_LORA, 2 * HEAD_DIM),
        norm_g=sm[0:1], b_gate=sm[1:3].reshape(1, -1), lb_logits=sm[3:5],
        hg_norm_g=misc[None, :HEAD_DIM], q_a_g=misc[None, HEAD_DIM:HEAD_DIM + Q_LORA],
        kv_a_g=misc[None, HEAD_DIM + Q_LORA:HEAD_DIM + Q_LORA + KV_LORA], final_norm_g=sm[6],
        extra=misc[HEAD_DIM + Q_LORA + KV_LORA],
    )


def _weight_shard_buffers(w_in, w_uq, w_ukv, w_pa, w_pb, w_out):
    w_in_pad = jnp.pad(w_in.reshape(D_MODEL, -1), ((0, 0), (0, PACK_COLS - IN_COLS // N_DEV)))
    parts = [a.reshape(-1, PACK_COLS) for a in (w_pa, w_pb, w_out, w_uq, w_ukv)]
    others = jnp.concatenate(parts + [jnp.zeros((ROWS_OTHER - ROWS_OTHER_USED, PACK_COLS), F32)], axis=0)
    return w_in_pad.astype(MXU_DTYPE), others.astype(MXU_DTYPE)


def _other_weights(gathered):
    r0 = 0
    mats = []
    for _ in range(3):
        mats.append(gathered[:, r0:r0 + ROWS_W_PROJ].reshape(D_MODEL, D_MODEL))
        r0 += ROWS_W_PROJ
    w_uq = gathered[:, r0:r0 + ROWS_W_UQ].reshape(N_DEV, Q_LORA, QK_DIM).transpose(1, 0, 2)
    w_uq_p = jnp.concatenate([w_uq, jnp.zeros((Q_LORA, HEADS, HEAD_PAD - QK_DIM), w_uq.dtype)], axis=2)
    w_uq_p = w_uq_p.reshape(Q_LORA, HEADS * HEAD_PAD)
    r0 += ROWS_W_UQ
    w_ukv = gathered[:, r0:r0 + ROWS_W_UKV].reshape(N_DEV, KV_LORA, 2 * HEAD_DIM).transpose(1, 0, 2)
    w_kn = w_ukv[:, :, :HEAD_DIM].reshape(KV_LORA, D_MODEL)
    w_v = w_ukv[:, :, HEAD_DIM:].reshape(KV_LORA, D_MODEL)
    return w_uq_p, w_kn, w_v, mats[0], mats[1], mats[2]


def _misc_slabs(dw_uq_p, dw_kn, dw_v, small):
    dw_uq = dw_uq_p.reshape(Q_LORA, HEADS, HEAD_PAD)[:, :, :QK_DIM].transpose(1, 0, 2)
    dw_ukv = jnp.concatenate([dw_kn.reshape(KV_LORA, HEADS, HEAD_DIM),
                              dw_v.reshape(KV_LORA, HEADS, HEAD_DIM)], axis=2).transpose(1, 0, 2)
    return jnp.stack([_pack_misc(dw_uq[j], dw_ukv[j], *small) for j in range(N_DEV)])


def _step_gradients(x, target, norm_g, b_gate, lb_logits, hg_norm_g, q_a_g, kv_a_g, final_g,
                    w_in_p, other_shard):
    s = x.shape[0]
    cos, sin_a, sin_b = _rope_tables(s)
    proj, h, gathered = _inproj(x, norm_g, w_in_p, other_shard)
    w_uq_p, w_kn, w_v, w_pa, w_pb, w_out = _other_weights(gathered)
    o_all, ya, states = _hgrn_fwd(proj, lb_logits, hg_norm_g)
    q_all, k_all, v_all, cqn, ckvn = _mla_prep(proj, q_a_g, kv_a_g, w_uq_p, w_kn, w_v, cos, sin_a, sin_b)
    ao, yb, mblk, p_all = _flash_fwd(q_all, k_all, v_all, proj)
    (dx2, dya, dao, dmz, dg0, dg1, merged_b, dpa_b, dpb_b, dx2_b,
     loss_acc, dfg_acc, dbg_acc) = _merge_fwd_bwd(x, target, ya, yb, ao, proj, b_gate, final_g, w_pa, w_pb, w_out)
    dhq, dhf, dhi, dhz, dlb, dhg_acc = _hgrn_bwd(proj, lb_logits, hg_norm_g, o_all, dya, states)

    early = {0: dhq, 1: dhf, 2: dhi, 3: dhz, MZ_SEG: dmz, GL_SEG: dg0, GL_SEG + 1: dg1}
    dw_early = {k: _matmul_tn(h, sg, "dw_in_%d" % k) for k, sg in early.items()}
    mats = [_matmul_tn(a, b, name, TRANSPORT_DTYPE).reshape(N_DEV, ROWS_W_PROJ, PACK_COLS)
            for a, b, name in ((ya, dpa_b, "dw_pa"), (yb, dpb_b, "dw_pb"), (merged_b, dx2_b, "dw_out"))]
    early_slabs = [_scatter_dw_in(dw_early, list(range(N_DEV)), "scatter_dw_in")] + mats
    dq_all, dk_all, dv_all, (recv_in, recv_pa, recv_pb, recv_out) = _flash_bwd(
        q_all, k_all, v_all, dao, ao, mblk, p_all, early_slabs)

    dsmall, dqf_b, dkn_b, dv_b, dgq_acc, dgk_acc = _mla_prep_bwd(
        dq_all, dk_all, dv_all, proj, q_a_g, kv_a_g, w_uq_p, w_kn, w_v, cos, sin_a, sin_b)
    late_slab = _scatter_dw_in({SMALL_SEG: _matmul_tn(h, dsmall, "dw_in_%d" % SMALL_SEG)}, [LATE_DEV],
                               "scatter_dw_in_late")
    segs = [dhq, dhf, dhi, dhz, dsmall, dmz, dg0, dg1]
    return dict(
        segs=segs, h=h, dx2=dx2, late_slab=late_slab,
        dw_uq_p=_matmul_tn(cqn, dqf_b, "dw_uq"), dw_kn=_matmul_tn(ckvn, dkn_b, "dw_kn"),
        dw_v=_matmul_tn(ckvn, dv_b, "dw_v"),
        small=dict(b_gate=dbg_acc[0:1], lb_logits=dlb, hg_norm_g=dhg_acc[0:1], q_a_g=dgq_acc[0:1],
                   kv_a_g=dgk_acc[0:1], final_norm_g=dfg_acc[0], loss=loss_acc[0, 0]),
        recv=dict(w_in=[recv_in], w_pa=[recv_pa], w_pb=[recv_pb], w_out=[recv_out]),
    )


def kernel(x, norm_g, w_in, b_gate, lb_logits, hg_norm_g, q_a_g, w_uq, kv_a_g, w_ukv, w_proj_a, w_proj_b, w_out, final_norm_g, loss_target, m_norm_g, m_w_in, m_b_gate, m_lb_logits, m_hg_norm_g, m_q_a_g, m_w_uq, m_kv_a_g, m_w_ukv, m_w_proj_a, m_w_proj_b, m_w_out, m_final_norm_g, v_norm_g, v_w_in, v_b_gate, v_lb_logits, v_hg_norm_g, v_q_a_g, v_w_uq, v_kv_a_g, v_w_ukv, v_w_proj_a, v_w_proj_b, v_w_out, v_final_norm_g):
    zero = jnp.zeros((1,), F32)
    xs = x[0]
    w_in_shard, other_shard = _weight_shard_buffers(w_in, w_uq, w_ukv, w_proj_a, w_proj_b, w_out)
    w_in_p = _assemble_w_in(_all_gather_packed(w_in_shard))
    g = _step_gradients(xs, loss_target[0], norm_g, b_gate, lb_logits, hg_norm_g, q_a_g, kv_a_g,
                        final_norm_g.reshape(1, -1), w_in_p, other_shard)
    sm = g["small"]
    misc_slabs = _misc_slabs(g["dw_uq_p"], g["dw_kn"], g["dw_v"],
                             (jnp.zeros_like(norm_g), sm["b_gate"], sm["lb_logits"], sm["hg_norm_g"], sm["q_a_g"],
                              sm["kv_a_g"], sm["final_norm_g"], sm["loss"]))
    late_recv_init = jnp.zeros((N_DEV, D_MODEL, PACK_COLS), TRANSPORT_DTYPE)
    grad_x, dng_acc, recv_late, recv_misc = _dh_bwd(g["segs"], w_in_p, xs, g["dx2"], norm_g,
                                                    g["late_slab"], late_recv_init, misc_slabs)
    recv_ng = _exchange_rows(jnp.broadcast_to(dng_acc[None], (N_DEV, 8, D_MODEL)))

    recv = g["recv"]
    out_in = _sum_adamw(recv["w_in"] + [recv_late], w_in[0], m_w_in[0], v_w_in[0], "adamw_w_in")
    out_pa = _sum_adamw(recv["w_pa"], w_proj_a[0], m_w_proj_a[0], v_w_proj_a[0], "adamw_w_pa")
    out_pb = _sum_adamw(recv["w_pb"], w_proj_b[0], m_w_proj_b[0], v_w_proj_b[0], "adamw_w_pb")
    out_out = _sum_adamw(recv["w_out"], w_out[0], m_w_out[0], v_w_out[0], "adamw_w_out")
    out_ng = _sum_adamw([recv_ng], *[jnp.broadcast_to(a, (8, D_MODEL)) for a in (norm_g, m_norm_g, v_norm_g)],
                        "adamw_norm_g")
    out_misc = _sum_adamw(
        [recv_misc],
        _pack_misc(w_uq, w_ukv, norm_g, b_gate, lb_logits, hg_norm_g, q_a_g, kv_a_g, final_norm_g, zero),
        _pack_misc(m_w_uq, m_w_ukv, m_norm_g, m_b_gate, m_lb_logits, m_hg_norm_g, m_q_a_g, m_kv_a_g,
                   m_final_norm_g, zero),
        _pack_misc(v_w_uq, v_w_ukv, v_norm_g, v_b_gate, v_lb_logits, v_hg_norm_g, v_q_a_g, v_kv_a_g,
                   v_final_norm_g, zero),
        "adamw_misc")
    names = ["norm_g", "w_in", "b_gate", "lb_logits", "hg_norm_g", "q_a_g", "w_uq", "kv_a_g", "w_ukv",
             "w_proj_a", "w_proj_b", "w_out", "final_norm_g"]
    kinds = []
    for i in range(4):
        d = _unpack_misc(out_misc[i])
        d.update(w_in=out_in[i][None], w_proj_a=out_pa[i][None], w_proj_b=out_pb[i][None], w_out=out_out[i][None],
                 norm_g=out_ng[i][0:1])
        kinds.append(d)
    return (kinds[0]["extra"], grad_x[None], *[d[n] for d in kinds for n in names])
```

```python
import functools

import jax
import jax.numpy as jnp
from jax import lax
from jax.experimental import pallas as pl
from jax.experimental.pallas import tpu as pltpu

D_MODEL = 1024
HEADS = 8
HEAD_DIM = 128
HG_CHUNK = 32
Q_LORA = 384
KV_LORA = 256
QK_ROPE = 64
QK_DIM = 192
ROPE_THETA = 10000.0
EPS = 1e-6
IN_COLS = 7872
ADAM_LR = 0.001
ADAM_B1 = 0.9
ADAM_B2 = 0.999
ADAM_EPS = 1e-08
ADAM_WD = 0.01
ADAM_STEP = 10

N_DEV = 8
SEG = 1024
PROJ_W = 8 * SEG
SMALL_SEG = 4
MZ_SEG = 5
GL_SEG = 6
HEAD_PAD = 256
PACK_COLS = 1024
LOSS_LANE = HEAD_DIM + Q_LORA + KV_LORA
ROWS_W_UQ = 72
ROWS_W_UKV = 64
ROWS_W_PROJ = 128
ROWS_OTHER_USED = 3 * ROWS_W_PROJ + ROWS_W_UQ + ROWS_W_UKV
ROWS_OTHER = 528
LATE_DEV = (SMALL_SEG * SEG) // (IN_COLS // N_DEV)
assert (SMALL_SEG * SEG + Q_LORA + KV_LORA + QK_ROPE - 1) // (IN_COLS // N_DEV) == LATE_DEV

QK_SCALE = QK_DIM ** -0.5
LOG2E = 1.4426950408889634
LN2 = 0.6931471805599453
Q_PRESCALE = QK_SCALE * LOG2E

MXU_DTYPE = jnp.bfloat16
TRANSPORT_DTYPE = jnp.bfloat16
VMEM_LIMIT = 48 * 1024 * 1024
VMEM_LIMIT_BIG = 60 * 1024 * 1024

T_HGRN = 256
HG_HEADS_PER_STEP = 4
TM_ROW = 256
T_ATT = 1024
T_ATT_BWD = T_ATT
ATT_SUB = 4
P_SLOTS = 4
LSE_LANE = 127
ATT_SUB_BWD = 2
TS_TN = 2048
TR_ADAM = 256

F32 = jnp.float32
MESH = pl.DeviceIdType.MESH


def _dot(a, b):
    return jnp.dot(a, b, preferred_element_type=F32)


def _dot_nt(a, b):
    return lax.dot_general(a, b, (((1,), (1,)), ((), ())), preferred_element_type=F32)


def _dot_tn(a, b):
    return lax.dot_general(a, b, (((0,), (0,)), ((), ())), preferred_element_type=F32)


def _mx(a):
    return a.astype(MXU_DTYPE)


def _sigmoid(x):
    return 1.0 / (1.0 + jnp.exp(-x))


def _params(vmem=VMEM_LIMIT, **kw):
    return pltpu.CompilerParams(vmem_limit_bytes=vmem, **kw)


def _bcast_rows(row, n):
    return jnp.broadcast_to(row, (n, row.shape[-1]))


def _resident(shape):
    return pl.BlockSpec(shape, lambda *_: (0, 0), pipeline_mode=pl.Buffered(1))


HBM_SPEC = pl.BlockSpec(memory_space=pltpu.HBM)


def _all_gather_packed(shard):
    rows, cols = shard.shape

    def body(x_ref, out_ref, send_sems, recv_sems, local_sem):
        x, y, c = lax.axis_index("x"), lax.axis_index("y"), lax.axis_index("c")
        me, sibling = (x, y, c), (x, y, 1 - c)
        chips = [(1 - x, y), (x, 1 - y), (1 - x, 1 - y)]

        def slot(px, py, pc):
            return out_ref.at[4 * px + 2 * py + pc]

        def copy(k, block, to, src=None):
            return pltpu.make_async_remote_copy(
                src_ref=slot(*block) if src is None else src,
                dst_ref=slot(*block),
                send_sem=send_sems.at[k],
                recv_sem=recv_sems.at[k],
                device_id=to,
                device_id_type=MESH,
            )

        mine = pltpu.make_async_copy(x_ref, slot(*me), local_sem)
        mine.start()
        first = [copy(0, me, sibling, src=x_ref)]
        first += [copy(1 + j, me, (*chip, c), src=x_ref) for j, chip in enumerate(chips)]
        for cp in first:
            cp.start()
        passed = [copy(4 + j, (*chip, c), sibling) for j, chip in enumerate(chips)]
        for j, chip in enumerate(chips):
            copy(1 + j, (*chip, c), me).wait_recv()
            passed[j].start()
        copy(0, sibling, me).wait_recv()
        for j, chip in enumerate(chips):
            copy(4 + j, (*chip, 1 - c), me).wait_recv()
        for cp in first + passed:
            cp.wait_send()
        mine.wait()

    return pl.pallas_call(
        body,
        name="ag_weights",
        out_shape=jax.ShapeDtypeStruct((N_DEV, rows, cols), shard.dtype),
        in_specs=[HBM_SPEC],
        out_specs=HBM_SPEC,
        scratch_shapes=[
            pltpu.SemaphoreType.DMA((7,)),
            pltpu.SemaphoreType.DMA((7,)),
            pltpu.SemaphoreType.DMA,
        ],
    )(shard)


class _Exchange:
    def __init__(self, g_refs, recv_refs, send_sems, recv_sems, local_sems, gather=False):
        x, y, c = lax.axis_index("x"), lax.axis_index("y"), lax.axis_index("c")
        me = 4 * x + 2 * y + c
        n_ops = len(g_refs)

        def source(i, dest):
            return g_refs[i] if gather else g_refs[i].at[dest]

        def copy(i, k, landing):
            px, py, pc = x ^ ((k >> 2) & 1), y ^ ((k >> 1) & 1), c ^ (k & 1)
            peer = 4 * px + 2 * py + pc
            return pltpu.make_async_remote_copy(
                src_ref=source(i, peer),
                dst_ref=recv_refs[i].at[peer if landing else me],
                send_sem=send_sems.at[i * (N_DEV - 1) + k - 1],
                recv_sem=recv_sems.at[i * (N_DEV - 1) + k - 1],
                device_id=(px, py, pc),
                device_id_type=MESH,
            )

        pairs = [(i, k) for i in range(n_ops) for k in range(1, N_DEV)]
        self.mine = lambda: [pltpu.make_async_copy(source(i, me), recv_refs[i].at[me], local_sems.at[i])
                             for i in range(n_ops)]
        self.sends = lambda: [copy(i, k, False) for i, k in pairs]
        self.landings = lambda: [copy(i, k, True) for i, k in pairs]

    def start(self):
        for cp in self.mine() + self.sends():
            cp.start()

    def wait(self):
        for cp in self.landings():
            cp.wait_recv()
        for cp in self.sends():
            cp.wait_send()
        for cp in self.mine():
            cp.wait()

    @staticmethod
    def semaphores(n_ops):
        return [pltpu.SemaphoreType.DMA((n_ops * (N_DEV - 1),)),
                pltpu.SemaphoreType.DMA((n_ops * (N_DEV - 1),)),
                pltpu.SemaphoreType.DMA((n_ops,))]


def _exchange_rows(slabs):
    def body(g_ref, recv_ref, send_sems, recv_sems, local_sems):
        exchange = _Exchange([g_ref], [recv_ref], send_sems, recv_sems, local_sems)
        exchange.start()
        exchange.wait()

    return pl.pallas_call(
        body,
        name="exchange_rows",
        out_shape=jax.ShapeDtypeStruct(slabs.shape, slabs.dtype),
        in_specs=[HBM_SPEC],
        out_specs=HBM_SPEC,
        scratch_shapes=_Exchange.semaphores(1),
    )(slabs)


def _sum_adamw(recvs, w, m, v, name):
    rows, cols = w.shape
    tr = TR_ADAM if rows % TR_ADAM == 0 else rows
    n_recv = len(recvs)

    def body(*refs):
        w_ref, m_ref, v_ref, g_out, d_out, m_out, v_out = refs[n_recv:]
        g = None
        for r_ref in refs[:n_recv]:
            for i in range(N_DEV):
                part = r_ref[i].astype(F32)
                g = part if g is None else g + part
        g = g[:, :cols]
        m_new = ADAM_B1 * m_ref[...] + (1.0 - ADAM_B1) * g
        v_new = ADAM_B2 * v_ref[...] + (1.0 - ADAM_B2) * (g * g)
        m_hat = m_new / (1.0 - ADAM_B1 ** ADAM_STEP)
        v_hat = v_new / (1.0 - ADAM_B2 ** ADAM_STEP)
        g_out[...] = g
        d_out[...] = -ADAM_LR * (m_hat / (jnp.sqrt(v_hat) + ADAM_EPS) + ADAM_WD * w_ref[...])
        m_out[...] = m_new
        v_out[...] = v_new

    row_spec = pl.BlockSpec((tr, cols), lambda i: (i, 0))
    shape = jax.ShapeDtypeStruct((rows, cols), F32)
    return pl.pallas_call(
        body,
        name=name,
        grid=(rows // tr,),
        in_specs=[pl.BlockSpec((N_DEV, tr, recvs[0].shape[2]), lambda i: (0, i, 0))] * n_recv + [row_spec] * 3,
        out_specs=[row_spec] * 4,
        out_shape=[shape] * 4,
        compiler_params=_params(),
    )(*recvs, w, m, v)


def _inproj(x, norm_g, w_in_p, other_shard):
    s = x.shape[0]
    tm = min(s, TM_ROW)
    nsteps = s // tm

    def body(x_ref, g_ref, w_ref, shard_ref, proj_ref, h_ref, gathered_ref, send_sems, recv_sems, local_sems):
        i = pl.program_id(0)

        def all_gather():
            return _Exchange([shard_ref], [gathered_ref], send_sems, recv_sems, local_sems, gather=True)

        @pl.when(i == 0)
        def _():
            all_gather().start()

        xf = x_ref[...]
        r = lax.rsqrt(jnp.mean(xf * xf, axis=-1, keepdims=True) + EPS)
        h = _mx(xf * r * g_ref[...])
        h_ref[...] = h
        for j in range(PROJ_W // SEG):
            cols = slice(j * SEG, (j + 1) * SEG)
            proj_ref[:, cols] = _dot(h, w_ref[:, cols])

        @pl.when(i == nsteps - 1)
        def _():
            all_gather().wait()

    return pl.pallas_call(
        body,
        name="inproj",
        grid=(nsteps,),
        in_specs=[
            pl.BlockSpec((tm, D_MODEL), lambda i: (i, 0)),
            pl.BlockSpec((1, D_MODEL), lambda i: (0, 0)),
            _resident((D_MODEL, PROJ_W)),
            HBM_SPEC,
        ],
        out_specs=[
            pl.BlockSpec((tm, PROJ_W), lambda i: (i, 0)),
            pl.BlockSpec((tm, D_MODEL), lambda i: (i, 0)),
            HBM_SPEC,
        ],
        out_shape=[
            jax.ShapeDtypeStruct((s, PROJ_W), F32),
            jax.ShapeDtypeStruct((s, D_MODEL), MXU_DTYPE),
            jax.ShapeDtypeStruct((N_DEV,) + other_shard.shape, other_shard.dtype),
        ],
        scratch_shapes=_Exchange.semaphores(1),
        compiler_params=_params(),
    )(x, norm_g, w_in_p, other_shard)


def _chunk_lower_mask(t):
    row = lax.broadcasted_iota(jnp.int32, (t, t), 0)
    col = lax.broadcasted_iota(jnp.int32, (t, t), 1)
    return ((row // HG_CHUNK) == (col // HG_CHUNK)) & (col <= row)


def _chunk_pos(t):
    return lax.broadcasted_iota(jnp.int32, (t, HEAD_DIM), 0) & (HG_CHUNK - 1)


def _cumsum_chunk(x, pos):
    sh = 1
    while sh < HG_CHUNK:
        x = x + jnp.where(pos >= sh, pltpu.roll(x, sh, 0), 0.0)
        sh *= 2
    return x


def _rcumsum_chunk(x, pos):
    t = x.shape[0]
    sh = 1
    while sh < HG_CHUNK:
        x = x + jnp.where(pos < HG_CHUNK - sh, pltpu.roll(x, t - sh, 0), 0.0)
        sh *= 2
    return x


def _chunk_total(x):
    t, w = x.shape
    tot = jnp.sum(x.reshape(t // HG_CHUNK, HG_CHUNK, w), axis=1, keepdims=True)
    return jnp.broadcast_to(tot, (t // HG_CHUNK, HG_CHUNK, w)).reshape(t, w)


def _hgrn_gates(hq, hf, lb_logits, pos):
    lb = _sigmoid(lb_logits[0:1, :] - lb_logits[1:2, :])
    sig = _sigmoid(hf)
    f = lb + (1.0 - lb) * sig
    sq = _sigmoid(hq)
    q = hq * sq
    k = 1.0 - f
    logf = jnp.log(f)
    bcum = _cumsum_chunk(logf, pos)
    blast = _chunk_total(logf)
    eb = jnp.exp(bcum)
    enb = jnp.exp(-bcum)
    eo = jnp.exp(blast - bcum)
    return dict(lb=lb, sig=sig, f=f, sq=sq, q=q, k=k, eb=eb, enb=enb, eo=eo,
                qi=q * eb, ki=k * enb, ko=k * eo, dec=jnp.exp(blast))


def _hgrn_fwd(proj, lb_logits, hg_norm_g):
    s = proj.shape[0]
    t = min(s, T_HGRN)
    nb = s // t
    nc = t // HG_CHUNK
    hw = HG_HEADS_PER_STEP * HEAD_DIM

    def body(hq_ref, hf_ref, hi_ref, hz_ref, lb_ref, g_ref, o_ref, ya_ref, st_ref, state, u_sc, stb_sc):
        b = pl.program_id(1)

        @pl.when(b == 0)
        def _():
            state[...] = jnp.zeros_like(state)

        lower = _chunk_lower_mask(t)
        pos = _chunk_pos(t)
        for hh in range(HG_HEADS_PER_STEP):
            cols = slice(hh * HEAD_DIM, (hh + 1) * HEAD_DIM)
            st = state[hh]
            st_ref[0, hh] = st
            gt = _hgrn_gates(hq_ref[:, cols], hf_ref[:, cols], lb_ref[:, cols], pos)
            vb = _mx(hi_ref[:, cols])
            qib, kib, kob = _mx(gt["qi"]), _mx(gt["ki"]), _mx(gt["ko"])
            a = jnp.where(lower, _dot_nt(qib, kib), 0.0)
            o_intra = _dot(_mx(a), vb)
            for c in range(nc):
                sl = slice(c * HG_CHUNK, (c + 1) * HG_CHUNK)
                u_sc[hh, c] = _dot_tn(vb[sl], kob[sl])
            for c in range(nc):
                stb_sc[hh, c] = _mx(st)
                st = st * gt["dec"][c * HG_CHUNK:c * HG_CHUNK + 1, :] + u_sc[hh, c]
            state[hh] = st
            outs = []
            for c in range(nc):
                sl = slice(c * HG_CHUNK, (c + 1) * HG_CHUNK)
                outs.append(o_intra[sl] + _dot_nt(qib[sl], stb_sc[hh, c]))
            o = jnp.concatenate(outs, axis=0)
            o_ref[:, cols] = o
            r = lax.rsqrt(jnp.mean(o * o, axis=-1, keepdims=True) + EPS)
            hz = hz_ref[:, cols]
            ya_ref[:, cols] = _mx((o * r * g_ref[...]) * (hz * _sigmoid(hz)))

    hsteps = HEADS // HG_HEADS_PER_STEP

    def seg(k):
        return pl.BlockSpec((t, hw), lambda h, b, k=k: (b, k * hsteps + h))

    return pl.pallas_call(
        body,
        name="hgrn_fwd",
        grid=(hsteps, nb),
        in_specs=[seg(0), seg(1), seg(2), seg(3),
                  pl.BlockSpec((2, hw), lambda h, b: (0, h)),
                  pl.BlockSpec((1, HEAD_DIM), lambda h, b: (0, 0))],
        out_specs=[
            pl.BlockSpec((t, hw), lambda h, b: (b, h)),
            pl.BlockSpec((t, hw), lambda h, b: (b, h)),
            pl.BlockSpec((1, HG_HEADS_PER_STEP, HEAD_DIM, HEAD_DIM), lambda h, b: (b, h, 0, 0)),
        ],
        out_shape=[
            jax.ShapeDtypeStruct((s, D_MODEL), F32),
            jax.ShapeDtypeStruct((s, D_MODEL), MXU_DTYPE),
            jax.ShapeDtypeStruct((nb, HEADS, HEAD_DIM, HEAD_DIM), F32),
        ],
        scratch_shapes=[pltpu.VMEM((HG_HEADS_PER_STEP, HEAD_DIM, HEAD_DIM), F32),
                        pltpu.VMEM((HG_HEADS_PER_STEP, nc, HEAD_DIM, HEAD_DIM), F32),
                        pltpu.VMEM((HG_HEADS_PER_STEP, nc, HEAD_DIM, HEAD_DIM), MXU_DTYPE)],
        compiler_params=_params(),
    )(proj, proj, proj, proj, lb_logits, hg_norm_g)


def _rope(x, cos, sin_a, sin_b):
    return x * cos + pltpu.roll(x, 96, 1) * sin_a + pltpu.roll(x, 32, 1) * sin_b


def _rope_t(d, cos, sin_a, sin_b):
    return d * cos + pltpu.roll(d * sin_a, 32, 1) + pltpu.roll(d * sin_b, 96, 1)


def _mla_prep(proj, q_a_g, kv_a_g, w_uq_p, w_kn, w_v, cos, sin_a, sin_b):
    s = proj.shape[0]
    tm = min(s, TM_ROW)

    def body(sm_ref, gq_ref, gk_ref, wq_ref, wkn_ref, wv_ref, cos_ref, sa_ref, sb_ref,
             q_ref, k_ref, v_ref, cqn_ref, ckvn_ref):
        small = sm_ref[...]
        cq = small[:, :Q_LORA]
        ckv = small[:, Q_LORA:Q_LORA + KV_LORA]
        krp = small[:, Q_LORA + KV_LORA:Q_LORA + KV_LORA + HEAD_DIM]
        rq = lax.rsqrt(jnp.mean(cq * cq, axis=-1, keepdims=True) + EPS)
        rk = lax.rsqrt(jnp.mean(ckv * ckv, axis=-1, keepdims=True) + EPS)
        cqn = _mx(cq * rq * gq_ref[...])
        ckvn = _mx(ckv * rk * gk_ref[...])
        cqn_ref[...] = cqn
        ckvn_ref[...] = ckvn
        q = _dot(cqn, wq_ref[...]) * Q_PRESCALE
        kn = _dot(ckvn, wkn_ref[...])
        v = _dot(ckvn, wv_ref[...])
        cos_t, sa, sb = cos_ref[...], sa_ref[...], sb_ref[...]
        kpe = _mx(_rope(krp, cos_t, sa, sb))
        ones_col = (lax.broadcasted_iota(jnp.int32, (tm, HEAD_DIM), 1) == 0).astype(MXU_DTYPE)
        for h in range(HEADS):
            lo = h * HEAD_PAD
            v_ref[:, lo:lo + HEAD_DIM] = _mx(v[:, h * HEAD_DIM:(h + 1) * HEAD_DIM])
            v_ref[:, lo + HEAD_DIM:lo + HEAD_PAD] = ones_col
            q_ref[:, lo:lo + HEAD_DIM] = _mx(q[:, lo:lo + HEAD_DIM])
            q_ref[:, lo + HEAD_DIM:lo + HEAD_PAD] = _mx(_rope(q[:, lo + HEAD_DIM:lo + HEAD_PAD], cos_t, sa, sb))
            k_ref[:, lo:lo + HEAD_DIM] = _mx(kn[:, h * HEAD_DIM:(h + 1) * HEAD_DIM])
            k_ref[:, lo + HEAD_DIM:lo + HEAD_PAD] = kpe

    def const(shape):
        return pl.BlockSpec(shape, lambda i: (0, 0))

    def rows(w):
        return pl.BlockSpec((tm, w), lambda i: (i, 0))

    return pl.pallas_call(
        body,
        name="mla_prep",
        grid=(s // tm,),
        in_specs=[
            pl.BlockSpec((tm, SEG), lambda i: (i, SMALL_SEG)),
            const((1, Q_LORA)), const((1, KV_LORA)),
            const((Q_LORA, HEADS * HEAD_PAD)), const((KV_LORA, D_MODEL)), const((KV_LORA, D_MODEL)),
            rows(HEAD_DIM), rows(HEAD_DIM), rows(HEAD_DIM),
        ],
        out_specs=[rows(HEADS * HEAD_PAD)] * 3 + [rows(Q_LORA), rows(KV_LORA)],
        out_shape=[
            jax.ShapeDtypeStruct((s, HEADS * HEAD_PAD), MXU_DTYPE),
            jax.ShapeDtypeStruct((s, HEADS * HEAD_PAD), MXU_DTYPE),
            jax.ShapeDtypeStruct((s, HEADS * HEAD_PAD), MXU_DTYPE),
            jax.ShapeDtypeStruct((s, Q_LORA), MXU_DTYPE),
            jax.ShapeDtypeStruct((s, KV_LORA), MXU_DTYPE),
        ],
        compiler_params=_params(),
    )(proj, q_a_g, kv_a_g, w_uq_p, w_kn, w_v, cos, sin_a, sin_b)


def _diag_mask(t):
    row = lax.broadcasted_iota(jnp.int32, (t, t), 0)
    col = lax.broadcasted_iota(jnp.int32, (t, t), 1)
    return row >= col


def _flash_fwd(q_all, k_all, v_all, proj):
    s = q_all.shape[0]
    t = min(s, T_ATT)
    n = s // t
    ts = t // ATT_SUB
    n_pairs = n * (n + 1) // 2

    def body(q_ref, k_ref, v_ref, mz_ref, ao_ref, yb_ref, mblk_ref, p_hbm, m_sc, acc_sc, stage, p_sems):
        head, qi = pl.program_id(0), pl.program_id(1)
        m_sc[...] = jnp.full_like(m_sc, -jnp.inf)
        acc_sc[...] = jnp.zeros_like(acc_sc)
        mblk_ref[...] = jnp.zeros_like(mblk_ref)
        lane = lax.broadcasted_iota(jnp.int32, (ts, HEAD_DIM), 1)
        first_block = head * n_pairs + qi * (qi + 1) // 2

        def p_copy(slot, pair, r):
            rows = pl.ds(r * ts, ts)
            return pltpu.make_async_copy(stage.at[slot, rows], p_hbm.at[head, pair, rows], p_sems.at[slot, r])

        def p_wait(slot):
            for r in range(ATT_SUB):
                p_copy(slot, 0, r).wait()

        def key_block(ki, diagonal):
            base = pl.multiple_of(ki * t, t)
            count = first_block + ki
            slot = lax.rem(count, P_SLOTS)
            sc, pb, alpha = {}, {}, {}

            @pl.when(count >= P_SLOTS)
            def _():
                p_wait(slot)

            if diagonal:
                stage[slot] = jnp.zeros((t, t), MXU_DTYPE)

            def width(r):
                return (r + 1) * ts if diagonal else t

            def scores(r):
                w = width(r)
                s_r = _dot_nt(q_ref[r * ts:(r + 1) * ts], k_ref[pl.ds(base, w), :])
                if diagonal:
                    row = lax.broadcasted_iota(jnp.int32, (ts, w), 0) + r * ts
                    col = lax.broadcasted_iota(jnp.int32, (ts, w), 1)
                    s_r = jnp.where(row >= col, s_r, -jnp.inf)
                sc[r] = s_r

            def softmax(r):
                rs = slice(r * ts, (r + 1) * ts)
                m_prev = m_sc[rs]
                m_new = jnp.maximum(m_prev, jnp.max(sc[r], axis=-1, keepdims=True))
                pb[r] = _mx(jnp.exp2(sc[r] - m_new))
                alpha[r] = jnp.exp2(m_prev - m_new)
                m_sc[rs] = m_new
                mblk_ref[rs] = jnp.where(lane == ki, m_new, mblk_ref[rs])
                stage[slot, rs, :width(r)] = pb[r]
                p_copy(slot, qi * (qi + 1) // 2 + ki, r).start()

            def weighted_values(r):
                rs = slice(r * ts, (r + 1) * ts)
                acc_sc[rs] = alpha[r] * acc_sc[rs] + _dot(pb[r], v_ref[pl.ds(base, width(r)), :])

            for step in range(ATT_SUB + 2):
                if step < ATT_SUB:
                    scores(step)
                if 1 <= step <= ATT_SUB:
                    softmax(step - 1)
                if step >= 2:
                    weighted_values(step - 2)

        def below_diagonal(ki, carry):
            key_block(ki, False)
            return carry

        lax.fori_loop(0, qi, below_diagonal, 0)
        key_block(qi, True)

        @pl.when((head == HEADS - 1) & (qi == n - 1))
        def _():
            for slot in range(min(P_SLOTS, HEADS * n_pairs)):
                p_wait(slot)

        acc = acc_sc[...]
        l = acc[:, HEAD_DIM:HEAD_DIM + 1]
        ao = acc[:, :HEAD_DIM] / l
        ao_ref[...] = ao
        lane_t = lax.broadcasted_iota(jnp.int32, (t, HEAD_DIM), 1)
        mblk_ref[...] = jnp.where(lane_t == LSE_LANE, m_sc[...] + jnp.log2(l), mblk_ref[...])
        mz = mz_ref[...]
        yb_ref[...] = _mx(ao * (mz * _sigmoid(mz)))

    q_map = lambda h, qi: (qi, h)
    return pl.pallas_call(
        body,
        name="flash_fwd",
        grid=(HEADS, n),
        in_specs=[
            pl.BlockSpec((t, HEAD_PAD), q_map),
            pl.BlockSpec((s, HEAD_PAD), lambda h, qi: (0, h)),
            pl.BlockSpec((s, HEAD_PAD), lambda h, qi: (0, h)),
            pl.BlockSpec((t, HEAD_DIM), lambda h, qi: (qi, MZ_SEG * HEADS + h)),
        ],
        out_specs=[pl.BlockSpec((t, HEAD_DIM), q_map)] * 3 + [HBM_SPEC],
        out_shape=[
            jax.ShapeDtypeStruct((s, D_MODEL), F32),
            jax.ShapeDtypeStruct((s, D_MODEL), MXU_DTYPE),
            jax.ShapeDtypeStruct((s, D_MODEL), F32),
            jax.ShapeDtypeStruct((HEADS, n_pairs, t, t), MXU_DTYPE),
        ],
        scratch_shapes=[
            pltpu.VMEM((t, 1), F32),
            pltpu.VMEM((t, HEAD_PAD), F32),
            pltpu.VMEM((P_SLOTS, t, t), MXU_DTYPE),
            pltpu.SemaphoreType.DMA((P_SLOTS, ATT_SUB)),
        ],
        compiler_params=_params(),
    )(q_all, k_all, v_all, proj)


def _merge_fwd_bwd(x, target, ya, yb, ao, proj, b_gate, final_g, w_pa, w_pb, w_out):
    s = x.shape[0]
    tm = min(s, TM_ROW)

    def body(x_ref, t_ref, ya_ref, yb_ref, ao_ref, mz_ref, g0_ref, g1_ref, bg_ref, fg_ref, wpa_ref, wpb_ref, wo_ref,
             dx2_ref, dya_ref, dao_ref, dmz_ref, dg0_ref, dg1_ref, mb_ref, dpab_ref, dpbb_ref, dx2b_ref,
             loss_ref, dfg_ref, dbg_ref):
        i = pl.program_id(0)

        @pl.when(i == 0)
        def _():
            loss_ref[...] = jnp.zeros_like(loss_ref)
            dfg_ref[...] = jnp.zeros_like(dfg_ref)
            dbg_ref[...] = jnp.zeros_like(dbg_ref)

        pa = _dot(ya_ref[...], wpa_ref[...])
        pb = _dot(yb_ref[...], wpb_ref[...])
        bg = bg_ref[...]
        g0 = _sigmoid(g0_ref[...] + bg[:, :D_MODEL])
        g1 = _sigmoid(g1_ref[...] + bg[:, D_MODEL:])
        merged = g0 * pa + g1 * pb
        mb = _mx(merged)
        mb_ref[...] = mb
        x2 = x_ref[...] + _dot(mb, wo_ref[...])
        r = lax.rsqrt(jnp.mean(x2 * x2, axis=-1, keepdims=True) + EPS)
        xn = x2 * r
        fg = fg_ref[...]
        diff = xn * fg - t_ref[...]
        loss_ref[...] += 0.5 * jnp.sum(jnp.mean(diff * diff, axis=-1, keepdims=True))
        dy = diff * (1.0 / D_MODEL)
        dfg_ref[...] += _bcast_rows(jnp.sum(dy * xn, axis=0, keepdims=True), 8)
        tt = dy * fg
        dx2 = r * (tt - xn * jnp.mean(tt * xn, axis=-1, keepdims=True))
        dx2_ref[...] = dx2
        dx2b = _mx(dx2)
        dx2b_ref[...] = dx2b
        dmerged = _dot_nt(dx2b, wo_ref[...])
        dpa = _mx(dmerged * g0)
        dpb = _mx(dmerged * g1)
        dpab_ref[...] = dpa
        dpbb_ref[...] = dpb
        dg0 = dmerged * pa * (g0 * (1.0 - g0))
        dg1 = dmerged * pb * (g1 * (1.0 - g1))
        dg0_ref[...] = _mx(dg0)
        dg1_ref[...] = _mx(dg1)
        dbg_ref[:, :D_MODEL] += _bcast_rows(jnp.sum(dg0, axis=0, keepdims=True), 8)
        dbg_ref[:, D_MODEL:] += _bcast_rows(jnp.sum(dg1, axis=0, keepdims=True), 8)
        dya_ref[...] = _dot_nt(dpa, wpa_ref[...])
        dyb = _dot_nt(dpb, wpb_ref[...])
        mz = mz_ref[...]
        sg = _sigmoid(mz)
        dao_ref[...] = _mx(dyb * (mz * sg))
        dmz_ref[...] = _mx(dyb * ao_ref[...] * (sg + mz * sg * (1.0 - sg)))

    def rows(w=D_MODEL):
        return pl.BlockSpec((tm, w), lambda i: (i, 0))

    def const(shape):
        return pl.BlockSpec(shape, lambda i: (0, 0))

    def seg(k):
        return pl.BlockSpec((tm, SEG), lambda i: (i, k))

    f32 = jax.ShapeDtypeStruct((s, D_MODEL), F32)
    b16 = jax.ShapeDtypeStruct((s, D_MODEL), MXU_DTYPE)
    return pl.pallas_call(
        body,
        name="merge_fwd_bwd",
        grid=(s // tm,),
        in_specs=[
            rows(), rows(), rows(), rows(), rows(),
            seg(MZ_SEG), seg(GL_SEG), seg(GL_SEG + 1),
            const((1, 2 * D_MODEL)), const((1, D_MODEL)),
            _resident((D_MODEL, D_MODEL)), _resident((D_MODEL, D_MODEL)), _resident((D_MODEL, D_MODEL)),
        ],
        out_specs=[rows()] * 10 + [const((8, HEAD_DIM)), const((8, D_MODEL)), const((8, 2 * D_MODEL))],
        out_shape=[f32, f32, b16, b16, b16, b16, b16, b16, b16, b16,
                   jax.ShapeDtypeStruct((8, HEAD_DIM), F32),
                   jax.ShapeDtypeStruct((8, D_MODEL), F32),
                   jax.ShapeDtypeStruct((8, 2 * D_MODEL), F32)],
        compiler_params=_params(),
    )(x, target, ya, yb, ao, proj, proj, proj, b_gate, final_g, w_pa, w_pb, w_out)


def _flash_bwd(q_all, k_all, v_all, dao, ao, mblk, p_all, slab_sets):
    s = q_all.shape[0]
    t = min(s, T_ATT_BWD)
    n = s // t
    pairs = [(ki, qi) for ki in range(n) for qi in range(ki, n)]
    ki_list = jnp.asarray([p[0] for p in pairs], jnp.int32)
    qi_list = jnp.asarray([p[1] for p in pairs], jnp.int32)
    p_list = jnp.asarray([qi * (qi + 1) // 2 + ki for ki, qi in pairs], jnp.int32)
    n_ops = len(slab_sets)

    def body(ki_ref, qi_ref, pidx_ref, q_ref, k_ref, v_ref, do_ref, ao_ref, mblk_ref, p_ref, *rest):
        g_refs = rest[:n_ops]
        dq_ref, dk_ref, dv_ref = rest[n_ops:n_ops + 3]
        recv_refs = rest[n_ops + 3:2 * n_ops + 3]
        dk_acc, dv_acc, send_sems, recv_sems, local_sems = rest[2 * n_ops + 3:]
        head, step = pl.program_id(0), pl.program_id(1)
        ki, qi = ki_ref[step], qi_ref[step]

        @pl.when((head == 0) & (step == 0))
        def _():
            _Exchange(g_refs, recv_refs, send_sems, recv_sems, local_sems).start()

        @pl.when(qi == ki)
        def _():
            dk_acc[...] = jnp.zeros_like(dk_acc)
            dv_acc[...] = jnp.zeros_like(dv_acc)

        @pl.when(ki == 0)
        def _():
            dq_ref[pl.ds(pl.multiple_of(qi * t, t), t), :] = jnp.zeros((t, HEAD_PAD), F32)

        def pair(masked):
            nsub = ATT_SUB if masked else ATT_SUB_BWD
            ts = t // nsub
            dk_parts, dv_parts = [], []
            for r in range(nsub):
                rs = slice(r * ts, (r + 1) * ts)
                w = (r + 1) * ts if masked else t
                k = k_ref[:w]
                v = v_ref[:w]
                q = q_ref[rs]
                lane = lax.broadcasted_iota(jnp.int32, (ts, HEAD_DIM), 1)
                stats = mblk_ref[rs]
                m_blk = jnp.max(jnp.where(lane == ki, stats, -jnp.inf), axis=-1, keepdims=True)
                lse = jnp.max(jnp.where(lane == LSE_LANE, stats, -jnp.inf), axis=-1, keepdims=True)
                factor = jnp.exp2(m_blk - lse)
                p_st = p_ref[0, 0, rs, :w]
                do = do_ref[rs]
                do_f = do.astype(F32)
                delta = jnp.sum(do_f * ao_ref[rs], axis=-1, keepdims=True)
                dv_part = _dot_tn(p_st, _mx(do_f * factor))
                ds = p_st * _mx((_dot_nt(do, v) - delta) * factor)
                dk_part = _dot_tn(ds, q)
                rows = pl.ds(pl.multiple_of(qi * t + r * ts, ts), ts)
                dq_ref[rows, :] += _dot(ds, k)
                if masked:
                    dk_acc[:w] += dk_part
                    dv_acc[:w] += dv_part
                else:
                    dk_parts.append(dk_part)
                    dv_parts.append(dv_part)

            if not masked:
                dk_acc[...] += sum(dk_parts[1:], dk_parts[0])
                dv_acc[...] += sum(dv_parts[1:], dv_parts[0])

        @pl.when(qi == ki)
        def _():
            pair(True)

        @pl.when(qi > ki)
        def _():
            pair(False)

        @pl.when(qi == n - 1)
        def _():
            dk_ref[...] = _mx(dk_acc[...] * LN2)
            dv_ref[...] = _mx(dv_acc[...])

        @pl.when((head == HEADS - 1) & (step == len(pairs) - 1))
        def _():
            _Exchange(g_refs, recv_refs, send_sems, recv_sems, local_sems).wait()

    q_map = lambda h, p, ki_ref, qi_ref, pidx_ref: (qi_ref[p], h)
    kv_map = lambda h, p, ki_ref, qi_ref, pidx_ref: (ki_ref[p], h)
    grid_spec = pltpu.PrefetchScalarGridSpec(
        num_scalar_prefetch=3,
        grid=(HEADS, len(pairs)),
        in_specs=[
            pl.BlockSpec((t, HEAD_PAD), q_map),
            pl.BlockSpec((t, HEAD_PAD), kv_map),
            pl.BlockSpec((t, HEAD_DIM), lambda h, p, ki_ref, qi_ref, pidx_ref: (ki_ref[p], 2 * h)),
            pl.BlockSpec((t, HEAD_DIM), q_map),
            pl.BlockSpec((t, HEAD_DIM), q_map),
            pl.BlockSpec((t, HEAD_DIM), q_map),
            pl.BlockSpec((1, 1, t, t), lambda h, p, ki_ref, qi_ref, pidx_ref: (h, pidx_ref[p], 0, 0)),
        ] + [HBM_SPEC] * n_ops,
        out_specs=[
            pl.BlockSpec((s, HEAD_PAD), lambda h, p, ki_ref, qi_ref, pidx_ref: (0, h)),
            pl.BlockSpec((t, HEAD_PAD), kv_map),
            pl.BlockSpec((t, HEAD_DIM), kv_map),
        ] + [HBM_SPEC] * n_ops,
        scratch_shapes=[pltpu.VMEM((t, HEAD_PAD), F32), pltpu.VMEM((t, HEAD_DIM), F32)]
        + _Exchange.semaphores(n_ops),
    )
    outs = pl.pallas_call(
        body,
        name="flash_bwd",
        grid_spec=grid_spec,
        out_shape=[
            jax.ShapeDtypeStruct((s, HEADS * HEAD_PAD), F32),
            jax.ShapeDtypeStruct((s, HEADS * HEAD_PAD), MXU_DTYPE),
            jax.ShapeDtypeStruct((s, D_MODEL), MXU_DTYPE),
        ] + [jax.ShapeDtypeStruct(a.shape, a.dtype) for a in slab_sets],
        compiler_params=_params(VMEM_LIMIT_BIG),
    )(ki_list, qi_list, p_list, q_all, k_all, v_all, dao, ao, mblk, p_all, *slab_sets)
    return outs[0], outs[1], outs[2], outs[3:]


def _mla_prep_bwd(dq_all, dk_all, dv_all, proj, q_a_g, kv_a_g, w_uq_p, w_kn, w_v, cos, sin_a, sin_b):
    s = proj.shape[0]
    tm = min(s, TM_ROW)

    def body(dq_ref, dk_ref, dv_ref, sm_ref, gq_ref, gk_ref, wq_ref, wkn_ref, wv_ref, cos_ref, sa_ref, sb_ref,
             dsm_ref, dqf_ref, dkn_ref, dvb_ref, dgq_ref, dgk_ref):
        i = pl.program_id(0)

        @pl.when(i == 0)
        def _():
            dgq_ref[...] = jnp.zeros_like(dgq_ref)
            dgk_ref[...] = jnp.zeros_like(dgk_ref)

        cos_t, sa, sb = cos_ref[...], sa_ref[...], sb_ref[...]
        dkpe = jnp.zeros((tm, HEAD_DIM), F32)
        for h in range(HEADS):
            lo = h * HEAD_PAD
            dqf_ref[:, lo:lo + HEAD_DIM] = _mx(dq_ref[:, lo:lo + HEAD_DIM] * QK_SCALE)
            dqf_ref[:, lo + HEAD_DIM:lo + HEAD_PAD] = _mx(
                _rope_t(dq_ref[:, lo + HEAD_DIM:lo + HEAD_PAD] * QK_SCALE, cos_t, sa, sb))
            dkn_ref[:, h * HEAD_DIM:(h + 1) * HEAD_DIM] = dk_ref[:, lo:lo + HEAD_DIM]
            dkpe = dkpe + dk_ref[:, lo + HEAD_DIM:lo + HEAD_PAD].astype(F32)
        dkr = _rope_t(dkpe, cos_t, sa, sb)
        dvb = dv_ref[...]
        dvb_ref[...] = dvb
        dcqn = _dot_nt(dqf_ref[...], wq_ref[...])
        dckvn = _dot_nt(dkn_ref[...], wkn_ref[...]) + _dot_nt(dvb, wv_ref[...])

        small = sm_ref[...]
        cq = small[:, :Q_LORA]
        ckv = small[:, Q_LORA:Q_LORA + KV_LORA]
        rq = lax.rsqrt(jnp.mean(cq * cq, axis=-1, keepdims=True) + EPS)
        rk = lax.rsqrt(jnp.mean(ckv * ckv, axis=-1, keepdims=True) + EPS)
        cqh = cq * rq
        ckh = ckv * rk
        dgq_ref[...] += _bcast_rows(jnp.sum(dcqn * cqh, axis=0, keepdims=True), 8)
        dgk_ref[...] += _bcast_rows(jnp.sum(dckvn * ckh, axis=0, keepdims=True), 8)
        tq = dcqn * gq_ref[...]
        tk = dckvn * gk_ref[...]
        dcq = rq * (tq - cqh * jnp.mean(tq * cqh, axis=-1, keepdims=True))
        dckv = rk * (tk - ckh * jnp.mean(tk * ckh, axis=-1, keepdims=True))
        dsm_ref[:, :Q_LORA] = _mx(dcq)
        dsm_ref[:, Q_LORA:Q_LORA + KV_LORA] = _mx(dckv)
        dsm_ref[:, Q_LORA + KV_LORA:Q_LORA + KV_LORA + HEAD_DIM] = _mx(dkr)
        dsm_ref[:, Q_LORA + KV_LORA + HEAD_DIM:] = jnp.zeros((tm, SEG - Q_LORA - KV_LORA - HEAD_DIM), MXU_DTYPE)

    def const(shape):
        return pl.BlockSpec(shape, lambda i: (0, 0))

    def rows(w):
        return pl.BlockSpec((tm, w), lambda i: (i, 0))

    return pl.pallas_call(
        body,
        name="mla_prep_bwd",
        grid=(s // tm,),
        in_specs=[
            rows(HEADS * HEAD_PAD), rows(HEADS * HEAD_PAD), rows(D_MODEL),
            pl.BlockSpec((tm, SEG), lambda i: (i, SMALL_SEG)),
            const((1, Q_LORA)), const((1, KV_LORA)),
            const((Q_LORA, HEADS * HEAD_PAD)), const((KV_LORA, D_MODEL)), const((KV_LORA, D_MODEL)),
            rows(HEAD_DIM), rows(HEAD_DIM), rows(HEAD_DIM),
        ],
        out_specs=[rows(SEG), rows(HEADS * HEAD_PAD), rows(D_MODEL), rows(D_MODEL),
                   const((8, Q_LORA)), const((8, KV_LORA))],
        out_shape=[
            jax.ShapeDtypeStruct((s, SEG), MXU_DTYPE),
            jax.ShapeDtypeStruct((s, HEADS * HEAD_PAD), MXU_DTYPE),
            jax.ShapeDtypeStruct((s, D_MODEL), MXU_DTYPE),
            jax.ShapeDtypeStruct((s, D_MODEL), MXU_DTYPE),
            jax.ShapeDtypeStruct((8, Q_LORA), F32),
            jax.ShapeDtypeStruct((8, KV_LORA), F32),
        ],
        compiler_params=_params(),
    )(dq_all, dk_all, dv_all, proj, q_a_g, kv_a_g, w_uq_p, w_kn, w_v, cos, sin_a, sin_b)


def _hgrn_bwd(proj, lb_logits, hg_norm_g, o_all, dya, states):
    s = proj.shape[0]
    t = min(s, T_HGRN)
    nb = s // t
    nc = t // HG_CHUNK

    def body(hq_ref, hf_ref, hi_ref, hz_ref, lb_ref, g_ref, o_ref, dya_ref, st_ref,
             dhq_ref, dhf_ref, dhi_ref, dhz_ref, dlb_ref, dg_ref, dstate, u_sc, g_sc, stf_sc, stb_sc, dstb_sc):
        h, b = pl.program_id(0), pl.program_id(1)

        @pl.when(b == 0)
        def _():
            dstate[...] = jnp.zeros_like(dstate)
            dlb_ref[...] = jnp.zeros_like(dlb_ref)

        @pl.when((b == 0) & (h == 0))
        def _():
            dg_ref[...] = jnp.zeros_like(dg_ref)

        lower = _chunk_lower_mask(t)
        pos = _chunk_pos(t)
        ghg = g_ref[...]
        for hh in range(HG_HEADS_PER_STEP):
            cols = slice(hh * HEAD_DIM, (hh + 1) * HEAD_DIM)
            hq, hf, hz = hq_ref[:, cols], hf_ref[:, cols], hz_ref[:, cols]
            gt = _hgrn_gates(hq, hf, lb_ref[:, cols], pos)
            vb = _mx(hi_ref[:, cols])
            qi, ki, ko = gt["qi"], gt["ki"], gt["ko"]
            qib, kib, kob = _mx(qi), _mx(ki), _mx(ko)

            o = o_ref[:, cols]
            sz = _sigmoid(hz)
            r = lax.rsqrt(jnp.mean(o * o, axis=-1, keepdims=True) + EPS)
            on = o * r
            dya_t = dya_ref[:, cols]
            don = dya_t * (hz * sz)
            dhz_ref[:, cols] = _mx(dya_t * (on * ghg) * (sz + hz * sz * (1.0 - sz)))
            dg_ref[...] += _bcast_rows(jnp.sum(don * on, axis=0, keepdims=True), 8)
            tt = don * ghg
            do = r * (tt - on * jnp.mean(tt * on, axis=-1, keepdims=True))
            dob = _mx(do)

            for c in range(nc):
                sl = slice(c * HG_CHUNK, (c + 1) * HG_CHUNK)
                u_sc[hh, c] = _dot_tn(vb[sl], kob[sl])
                g_sc[hh, c] = _dot_tn(dob[sl], qib[sl])

            st = st_ref[0, hh]
            for c in range(nc):
                stf_sc[hh, c] = st
                stb_sc[hh, c] = _mx(st)
                if c < nc - 1:
                    st = st * gt["dec"][c * HG_CHUNK:c * HG_CHUNK + 1, :] + u_sc[hh, c]

            dst = dstate[hh]
            dd_parts = [None] * nc
            for c in reversed(range(nc)):
                dec = gt["dec"][c * HG_CHUNK:c * HG_CHUNK + 1, :]
                dstb_sc[hh, c] = _mx(dst)
                dd_parts[c] = _bcast_rows(jnp.sum(dst * stf_sc[hh, c], axis=0, keepdims=True) * dec, HG_CHUNK)
                dst = dst * dec + g_sc[hh, c]
            dstate[hh] = dst

            a = jnp.where(lower, _dot_nt(qib, kib), 0.0)
            da = _mx(jnp.where(lower, _dot_nt(dob, vb), 0.0))
            dqi_intra = _dot(da, kib)
            dki = _dot_tn(da, qib)
            dv_intra = _dot_tn(_mx(a), dob)

            dqi_parts, dko_parts, dv_parts = [None] * nc, [None] * nc, [None] * nc
            for c in range(nc):
                sl = slice(c * HG_CHUNK, (c + 1) * HG_CHUNK)
                dv_parts[c] = dv_intra[sl] + _dot_nt(kob[sl], dstb_sc[hh, c])
                dko_parts[c] = _dot(vb[sl], dstb_sc[hh, c])
                dqi_parts[c] = dqi_intra[sl] + _dot(dob[sl], stb_sc[hh, c])
            dqi = jnp.concatenate(dqi_parts, axis=0)
            dko = jnp.concatenate(dko_parts, axis=0)
            dv = jnp.concatenate(dv_parts, axis=0)
            dd = jnp.concatenate(dd_parts, axis=0)

            dq = dqi * gt["eb"]
            dk = dki * gt["enb"] + dko * gt["eo"]
            db = dqi * qi - dki * ki - dko * ko
            dlogf = _rcumsum_chunk(db, pos) + _chunk_total(dko * ko) + dd
            df = dlogf / gt["f"] - dk
            lb, sig, sq = gt["lb"], gt["sig"], gt["sq"]
            dhf_ref[:, cols] = _mx(df * (1.0 - lb) * (sig * (1.0 - sig)))
            dhq_ref[:, cols] = _mx(dq * (sq + hq * sq * (1.0 - sq)))
            dhi_ref[:, cols] = _mx(dv)
            dlb = jnp.sum(df * (1.0 - sig), axis=0, keepdims=True) * (lb * (1.0 - lb))
            dlb_ref[:, cols] += jnp.concatenate([dlb, -dlb], axis=0)

    hw = HG_HEADS_PER_STEP * HEAD_DIM
    hsteps = HEADS // HG_HEADS_PER_STEP

    def seg(k):
        return pl.BlockSpec((t, hw), lambda h, b, k=k: (nb - 1 - b, k * hsteps + h))

    blk = pl.BlockSpec((t, hw), lambda h, b: (nb - 1 - b, h))
    b16 = jax.ShapeDtypeStruct((s, D_MODEL), MXU_DTYPE)
    return pl.pallas_call(
        body,
        name="hgrn_bwd",
        grid=(hsteps, nb),
        in_specs=[seg(0), seg(1), seg(2), seg(3),
                  pl.BlockSpec((2, hw), lambda h, b: (0, h)),
                  pl.BlockSpec((1, HEAD_DIM), lambda h, b: (0, 0)),
                  blk, blk,
                  pl.BlockSpec((1, HG_HEADS_PER_STEP, HEAD_DIM, HEAD_DIM), lambda h, b: (nb - 1 - b, h, 0, 0))],
        out_specs=[blk, blk, blk, blk,
                   pl.BlockSpec((2, hw), lambda h, b: (0, h)),
                   pl.BlockSpec((8, HEAD_DIM), lambda h, b: (0, 0))],
        out_shape=[b16, b16, b16, b16,
                   jax.ShapeDtypeStruct((2, D_MODEL), F32),
                   jax.ShapeDtypeStruct((8, HEAD_DIM), F32)],
        scratch_shapes=[pltpu.VMEM((HG_HEADS_PER_STEP, HEAD_DIM, HEAD_DIM), F32)]
        + [pltpu.VMEM((HG_HEADS_PER_STEP, nc, HEAD_DIM, HEAD_DIM), F32)] * 3
        + [pltpu.VMEM((HG_HEADS_PER_STEP, nc, HEAD_DIM, HEAD_DIM), MXU_DTYPE)] * 2,
        compiler_params=_params(),
    )(proj, proj, proj, proj, lb_logits, hg_norm_g, o_all, dya, states)


def _dh_bwd(segs, w_in_p, x, dx2, norm_g, late_slab, late_recv_init, slab_sets):
    s = x.shape[0]
    tm = min(s, TM_ROW)
    nseg = len(segs)
    nsteps = s // tm
    n_ops = len(slab_sets)
    late_xyc = ((LATE_DEV >> 2) & 1, (LATE_DEV >> 1) & 1, LATE_DEV & 1)

    def body(*refs):
        seg_refs = refs[:nseg]
        w_ref, x_ref, dx2_ref, g_ref, late_ref, _ = refs[nseg:nseg + 6]
        g_refs = refs[nseg + 6:nseg + 6 + n_ops]
        gx_ref, dng_ref, late_recv_ref = refs[nseg + 6 + n_ops:nseg + 9 + n_ops]
        recv_refs = refs[nseg + 9 + n_ops:nseg + 9 + 2 * n_ops]
        (dp_buf, send_sems, recv_sems, local_sems,
         late_send, late_recvs, late_local) = refs[nseg + 9 + 2 * n_ops:]
        i = pl.program_id(0)
        me = 4 * lax.axis_index("x") + 2 * lax.axis_index("y") + lax.axis_index("c")

        def misc_exchange():
            return _Exchange(g_refs, recv_refs, send_sems, recv_sems, local_sems)

        def late_copy(sender):
            return pltpu.make_async_remote_copy(
                src_ref=late_ref.at[0], dst_ref=late_recv_ref.at[sender], send_sem=late_send,
                recv_sem=late_recvs.at[(sender ^ LATE_DEV) - 1], device_id=late_xyc, device_id_type=MESH)

        def late_own():
            return pltpu.make_async_copy(late_ref.at[0], late_recv_ref.at[LATE_DEV], late_local)

        @pl.when(i == 0)
        def _():
            dng_ref[...] = jnp.zeros_like(dng_ref)
            misc_exchange().start()

        @pl.when((i == 0) & (me != LATE_DEV))
        def _():
            late_copy(me).start()

        @pl.when((i == 0) & (me == LATE_DEV))
        def _():
            late_own().start()

        for k, sref in enumerate(seg_refs):
            dp_buf[:, k * SEG:(k + 1) * SEG] = sref[...]
        dh = _dot_nt(dp_buf[...], w_ref[...])
        xf = x_ref[...]
        r = lax.rsqrt(jnp.mean(xf * xf, axis=-1, keepdims=True) + EPS)
        xh = xf * r
        dng_ref[...] += _bcast_rows(jnp.sum(dh * xh, axis=0, keepdims=True), 8)
        tt = dh * g_ref[...]
        gx_ref[...] = dx2_ref[...] + r * (tt - xh * jnp.mean(tt * xh, axis=-1, keepdims=True))

        @pl.when(i == nsteps - 1)
        def _():
            misc_exchange().wait()

        @pl.when((i == nsteps - 1) & (me != LATE_DEV))
        def _():
            late_copy(me).wait_send()

        @pl.when((i == nsteps - 1) & (me == LATE_DEV))
        def _():
            for k in range(1, N_DEV):
                late_copy(LATE_DEV ^ k).wait_recv()
            late_own().wait()

    rows = pl.BlockSpec((tm, D_MODEL), lambda i: (i, 0))
    return pl.pallas_call(
        body,
        name="dh_bwd",
        grid=(nsteps,),
        in_specs=[pl.BlockSpec((tm, SEG), lambda i: (i, 0))] * nseg + [
            _resident((D_MODEL, PROJ_W)),
            rows, rows,
            pl.BlockSpec((1, D_MODEL), lambda i: (0, 0)),
            HBM_SPEC, HBM_SPEC,
        ] + [HBM_SPEC] * n_ops,
        out_specs=[rows, pl.BlockSpec((8, D_MODEL), lambda i: (0, 0)), HBM_SPEC] + [HBM_SPEC] * n_ops,
        out_shape=[jax.ShapeDtypeStruct((s, D_MODEL), F32), jax.ShapeDtypeStruct((8, D_MODEL), F32),
                   jax.ShapeDtypeStruct(late_recv_init.shape, late_recv_init.dtype)]
        + [jax.ShapeDtypeStruct(a.shape, a.dtype) for a in slab_sets],
        input_output_aliases={nseg + 5: 2},
        scratch_shapes=[pltpu.VMEM((tm, PROJ_W), MXU_DTYPE)] + _Exchange.semaphores(n_ops)
        + [pltpu.SemaphoreType.DMA, pltpu.SemaphoreType.DMA((N_DEV - 1,)), pltpu.SemaphoreType.DMA],
        compiler_params=_params(),
    )(*segs, w_in_p, x, dx2, norm_g, late_slab, late_recv_init, *slab_sets)


def _matmul_tn(a, b, name, out_dtype=F32):
    s, m = a.shape
    n = b.shape[1]
    ts = min(s, TS_TN)
    tn = min(n, SEG)
    nk = s // ts

    def body(a_ref, b_ref, o_ref, acc):
        k = pl.program_id(1)
        part = _dot_tn(a_ref[...], b_ref[...])

        @pl.when(k == 0)
        def _():
            acc[...] = part

        @pl.when(k > 0)
        def _():
            acc[...] += part

        @pl.when(k == nk - 1)
        def _():
            o_ref[...] = acc[...].astype(out_dtype)

    return pl.pallas_call(
        body,
        name=name,
        grid=(n // tn, nk),
        in_specs=[pl.BlockSpec((ts, m), lambda j, k: (k, 0)), pl.BlockSpec((ts, tn), lambda j, k: (k, j))],
        out_specs=pl.BlockSpec((m, tn), lambda j, k: (0, j)),
        out_shape=jax.ShapeDtypeStruct((m, n), out_dtype),
        scratch_shapes=[pltpu.VMEM((m, tn), F32)],
        compiler_params=_params(),
    )(a, b)


def _w_in_pieces():
    per = IN_COLS // N_DEV
    pad_at = SMALL_SEG * SEG + Q_LORA + KV_LORA + QK_ROPE
    pieces = []
    for j in range(N_DEV):
        u0, u1 = j * per, (j + 1) * per
        cuts = [u0] + ([pad_at] if u0 < pad_at < u1 else []) + [u1]
        for a, b in zip(cuts[:-1], cuts[1:]):
            pieces.append((j, a - u0, b - u0, a if a < pad_at else a + PROJ_W - IN_COLS))
    return pad_at, pieces


def _assemble_w_in(gathered):
    tr = TM_ROW
    pad_at, pieces = _w_in_pieces()

    def body(in_ref, out_ref):
        out_ref[:, pad_at:pad_at + PROJ_W - IN_COLS] = jnp.zeros((tr, PROJ_W - IN_COLS), gathered.dtype)
        for j, a, b, p0 in pieces:
            out_ref[:, p0:p0 + b - a] = in_ref[j, :, a:b]

    return pl.pallas_call(
        body,
        name="assemble_w_in",
        grid=(D_MODEL // tr,),
        in_specs=[pl.BlockSpec((N_DEV, tr, PACK_COLS), lambda i: (0, i, 0))],
        out_specs=pl.BlockSpec((tr, PROJ_W), lambda i: (i, 0)),
        out_shape=jax.ShapeDtypeStruct((D_MODEL, PROJ_W), gathered.dtype),
        compiler_params=_params(),
    )(gathered)


def _scatter_dw_in(dw_segs, devs, name):
    tr = TM_ROW
    _, pieces = _w_in_pieces()
    per = IN_COLS // N_DEV
    seg_ids = sorted(dw_segs)
    nseg = len(seg_ids)

    def body(*refs):
        out_ref, buf = refs[nseg:]
        for k in range(PROJ_W // SEG):
            if k in seg_ids:
                buf[:, k * SEG:(k + 1) * SEG] = refs[seg_ids.index(k)][...]
            else:
                buf[:, k * SEG:(k + 1) * SEG] = jnp.zeros((tr, SEG), F32)
        for slot, dev in enumerate(devs):
            out_ref[slot, :, per:] = jnp.zeros((tr, PACK_COLS - per), TRANSPORT_DTYPE)
            for j, a, b, p0 in pieces:
                if j == dev:
                    out_ref[slot, :, a:b] = buf[:, p0:p0 + b - a].astype(TRANSPORT_DTYPE)

    return pl.pallas_call(
        body,
        name=name,
        grid=(D_MODEL // tr,),
        in_specs=[pl.BlockSpec((tr, SEG), lambda i: (i, 0))] * nseg,
        out_specs=pl.BlockSpec((len(devs), tr, PACK_COLS), lambda i: (0, i, 0)),
        out_shape=jax.ShapeDtypeStruct((len(devs), D_MODEL, PACK_COLS), TRANSPORT_DTYPE),
        scratch_shapes=[pltpu.VMEM((tr, PROJ_W), F32)],
        compiler_params=_params(),
    )(*[dw_segs[k] for k in seg_ids])


def _rope_tables(s):
    inv = ROPE_THETA ** (-jnp.arange(0, QK_ROPE, 2, dtype=F32) / QK_ROPE)
    ang = jnp.arange(s, dtype=F32)[:, None] * inv[None, :]
    cos, sin = jnp.cos(ang), jnp.sin(ang)
    z32 = jnp.zeros_like(cos)
    z64 = jnp.zeros((s, HEAD_DIM - QK_ROPE), F32)
    cos_t = jnp.concatenate([cos, cos, z64], axis=1)
    sin_a = jnp.concatenate([-sin, z32, z64], axis=1)
    sin_b = jnp.concatenate([z32, sin, z64], axis=1)
    return cos_t, sin_a, sin_b


def _small_rows(b_gate, lb_logits, hg_norm_g, q_a_g, kv_a_g, final_norm_g, loss):
    row5 = jnp.concatenate([hg_norm_g.reshape(1, -1), q_a_g.reshape(1, -1), kv_a_g.reshape(1, -1), loss.reshape(1, 1),
                            jnp.zeros((1, PACK_COLS - LOSS_LANE - 1), F32)], axis=1)
    zero_row = jnp.zeros((1, PACK_COLS), F32)
    return jnp.concatenate([zero_row, b_gate.reshape(2, -1), lb_logits, row5, final_norm_g.reshape(1, -1), zero_row],
                           axis=0)


def _adamw_small(recv_small, recv_norm_g, weights, m, v):
    n = len(weights)

    def body(rs_ref, rn_ref, *refs):
        w_refs, m_refs, v_refs = refs[:n], refs[n:2 * n], refs[2 * n:3 * n]
        loss_ref = refs[3 * n]
        outs = refs[3 * n + 1:]
        gs, gn = rs_ref[0], rn_ref[0]
        for i in range(1, N_DEV):
            gs = gs + rs_ref[i]
            gn = gn + rn_ref[i]
        loss_ref[...] = gs[5:6, LOSS_LANE:LOSS_LANE + HEAD_DIM]
        grads = [gn[0:1], jnp.concatenate([gs[1:2], gs[2:3]], axis=1), gs[3:5],
                 gs[5:6, :HEAD_DIM], gs[5:6, HEAD_DIM:HEAD_DIM + Q_LORA], gs[5:6, HEAD_DIM + Q_LORA:LOSS_LANE], gs[6:7]]
        for k, g in enumerate(grads):
            m_new = ADAM_B1 * m_refs[k][...] + (1.0 - ADAM_B1) * g
            v_new = ADAM_B2 * v_refs[k][...] + (1.0 - ADAM_B2) * (g * g)
            m_hat = m_new / (1.0 - ADAM_B1 ** ADAM_STEP)
            v_hat = v_new / (1.0 - ADAM_B2 ** ADAM_STEP)
            outs[k][...] = g
            outs[n + k][...] = -ADAM_LR * (m_hat / (jnp.sqrt(v_hat) + ADAM_EPS) + ADAM_WD * w_refs[k][...])
            outs[2 * n + k][...] = m_new
            outs[3 * n + k][...] = v_new

    shapes = [jax.ShapeDtypeStruct(w.shape, F32) for w in weights]
    res = pl.pallas_call(
        body,
        name="adamw_small",
        out_shape=[jax.ShapeDtypeStruct((1, HEAD_DIM), F32)] + shapes * 4,
        compiler_params=_params(),
    )(recv_small, recv_norm_g, *weights, *m, *v)
    return res[0], [res[1 + k * n:1 + (k + 1) * n] for k in range(4)]


def _weight_shard_buffers(w_in, w_uq, w_ukv, w_pa, w_pb, w_out):
    w_in_pad = jnp.pad(w_in.reshape(D_MODEL, -1), ((0, 0), (0, PACK_COLS - IN_COLS // N_DEV)))
    parts = [a.reshape(-1, PACK_COLS) for a in (w_pa, w_pb, w_out, w_uq, w_ukv)]
    others = jnp.concatenate(parts + [jnp.zeros((ROWS_OTHER - ROWS_OTHER_USED, PACK_COLS), F32)], axis=0)
    return w_in_pad.astype(MXU_DTYPE), others.astype(MXU_DTYPE)


def _other_weights(gathered):
    r0 = 0
    mats = []
    for _ in range(3):
        mats.append(gathered[:, r0:r0 + ROWS_W_PROJ].reshape(D_MODEL, D_MODEL))
        r0 += ROWS_W_PROJ
    w_uq = gathered[:, r0:r0 + ROWS_W_UQ].reshape(N_DEV, Q_LORA, QK_DIM).transpose(1, 0, 2)
    w_uq_p = jnp.concatenate([w_uq, jnp.zeros((Q_LORA, HEADS, HEAD_PAD - QK_DIM), w_uq.dtype)], axis=2)
    w_uq_p = w_uq_p.reshape(Q_LORA, HEADS * HEAD_PAD)
    r0 += ROWS_W_UQ
    w_ukv = gathered[:, r0:r0 + ROWS_W_UKV].reshape(N_DEV, KV_LORA, 2 * HEAD_DIM).transpose(1, 0, 2)
    w_kn = w_ukv[:, :, :HEAD_DIM].reshape(KV_LORA, D_MODEL)
    w_v = w_ukv[:, :, HEAD_DIM:].reshape(KV_LORA, D_MODEL)
    return w_uq_p, w_kn, w_v, mats[0], mats[1], mats[2]


def _late_slab_sets(dw_uq_p, dw_kn, dw_v, small_rows):
    uq = dw_uq_p.reshape(Q_LORA, HEADS, HEAD_PAD).transpose(1, 0, 2)
    ukv = jnp.concatenate([dw_kn.reshape(KV_LORA, HEADS, HEAD_DIM),
                           dw_v.reshape(KV_LORA, HEADS, HEAD_DIM)], axis=2).transpose(1, 0, 2)
    return [uq, ukv, jnp.broadcast_to(small_rows[None], (N_DEV,) + small_rows.shape)]


def _step_gradients(x, target, norm_g, b_gate, lb_logits, hg_norm_g, q_a_g, kv_a_g, final_g,
                    w_in_p, other_shard):
    s = x.shape[0]
    cos, sin_a, sin_b = _rope_tables(s)
    proj, h, gathered = _inproj(x, norm_g, w_in_p, other_shard)
    w_uq_p, w_kn, w_v, w_pa, w_pb, w_out = _other_weights(gathered)
    o_all, ya, states = _hgrn_fwd(proj, lb_logits, hg_norm_g)
    q_all, k_all, v_all, cqn, ckvn = _mla_prep(proj, q_a_g, kv_a_g, w_uq_p, w_kn, w_v, cos, sin_a, sin_b)
    ao, yb, mblk, p_all = _flash_fwd(q_all, k_all, v_all, proj)
    (dx2, dya, dao, dmz, dg0, dg1, merged_b, dpa_b, dpb_b, dx2_b,
     loss_acc, dfg_acc, dbg_acc) = _merge_fwd_bwd(x, target, ya, yb, ao, proj, b_gate, final_g, w_pa, w_pb, w_out)
    dhq, dhf, dhi, dhz, dlb, dhg_acc = _hgrn_bwd(proj, lb_logits, hg_norm_g, o_all, dya, states)

    early = {0: dhq, 1: dhf, 2: dhi, 3: dhz, MZ_SEG: dmz, GL_SEG: dg0, GL_SEG + 1: dg1}
    dw_early = {k: _matmul_tn(h, sg, "dw_in_%d" % k) for k, sg in early.items()}
    mats = [_matmul_tn(a, b, name, TRANSPORT_DTYPE).reshape(N_DEV, ROWS_W_PROJ, PACK_COLS)
            for a, b, name in ((ya, dpa_b, "dw_pa"), (yb, dpb_b, "dw_pb"), (merged_b, dx2_b, "dw_out"))]
    early_slabs = [_scatter_dw_in(dw_early, list(range(N_DEV)), "scatter_dw_in")] + mats
    dq_all, dk_all, dv_all, (recv_in, recv_pa, recv_pb, recv_out) = _flash_bwd(
        q_all, k_all, v_all, dao, ao, mblk, p_all, early_slabs)

    dsmall, dqf_b, dkn_b, dv_b, dgq_acc, dgk_acc = _mla_prep_bwd(
        dq_all, dk_all, dv_all, proj, q_a_g, kv_a_g, w_uq_p, w_kn, w_v, cos, sin_a, sin_b)
    late_slab = _scatter_dw_in({SMALL_SEG: _matmul_tn(h, dsmall, "dw_in_%d" % SMALL_SEG)}, [LATE_DEV],
                               "scatter_dw_in_late")
    segs = [dhq, dhf, dhi, dhz, dsmall, dmz, dg0, dg1]
    return dict(
        segs=segs, h=h, dx2=dx2, late_slab=late_slab,
        dw_uq_p=_matmul_tn(cqn, dqf_b, "dw_uq"), dw_kn=_matmul_tn(ckvn, dkn_b, "dw_kn"),
        dw_v=_matmul_tn(ckvn, dv_b, "dw_v"),
        small=dict(b_gate=dbg_acc[0:1], lb_logits=dlb, hg_norm_g=dhg_acc[0:1], q_a_g=dgq_acc[0:1],
                   kv_a_g=dgk_acc[0:1], final_norm_g=dfg_acc[0], loss=loss_acc[0, 0]),
        recv=dict(w_in=[recv_in], w_pa=[recv_pa], w_pb=[recv_pb], w_out=[recv_out]),
    )


def kernel(x, norm_g, w_in, b_gate, lb_logits, hg_norm_g, q_a_g, w_uq, kv_a_g, w_ukv, w_proj_a, w_proj_b, w_out, final_norm_g, loss_target, m_norm_g, m_w_in, m_b_gate, m_lb_logits, m_hg_norm_g, m_q_a_g, m_w_uq, m_kv_a_g, m_w_ukv, m_w_proj_a, m_w_proj_b, m_w_out, m_final_norm_g, v_norm_g, v_w_in, v_b_gate, v_lb_logits, v_hg_norm_g, v_q_a_g, v_w_uq, v_kv_a_g, v_w_ukv, v_w_proj_a, v_w_proj_b, v_w_out, v_final_norm_g):
    xs = x[0]
    w_in_shard, other_shard = _weight_shard_buffers(w_in, w_uq, w_ukv, w_proj_a, w_proj_b, w_out)
    w_in_p = _assemble_w_in(_all_gather_packed(w_in_shard))
    g = _step_gradients(xs, loss_target[0], norm_g, b_gate, lb_logits, hg_norm_g, q_a_g, kv_a_g,
                        final_norm_g.reshape(1, -1), w_in_p, other_shard)
    sm = g["small"]
    late_sets = _late_slab_sets(g["dw_uq_p"], g["dw_kn"], g["dw_v"],
                                _small_rows(sm["b_gate"], sm["lb_logits"], sm["hg_norm_g"], sm["q_a_g"],
                                            sm["kv_a_g"], sm["final_norm_g"], sm["loss"]))
    late_recv_init = jnp.zeros((N_DEV, D_MODEL, PACK_COLS), TRANSPORT_DTYPE)
    grad_x, dng_acc, recv_late, recv_uq, recv_ukv, recv_small = _dh_bwd(
        g["segs"], w_in_p, xs, g["dx2"], norm_g, g["late_slab"], late_recv_init, late_sets)
    recv_ng = _exchange_rows(jnp.broadcast_to(dng_acc[None], (N_DEV, 8, D_MODEL)))

    recv = g["recv"]
    big = dict(
        w_in=_sum_adamw(recv["w_in"] + [recv_late], w_in[0], m_w_in[0], v_w_in[0], "adamw_w_in"),
        w_proj_a=_sum_adamw(recv["w_pa"], w_proj_a[0], m_w_proj_a[0], v_w_proj_a[0], "adamw_w_pa"),
        w_proj_b=_sum_adamw(recv["w_pb"], w_proj_b[0], m_w_proj_b[0], v_w_proj_b[0], "adamw_w_pb"),
        w_out=_sum_adamw(recv["w_out"], w_out[0], m_w_out[0], v_w_out[0], "adamw_w_out"),
        w_uq=_sum_adamw([recv_uq], w_uq[0], m_w_uq[0], v_w_uq[0], "adamw_w_uq"),
        w_ukv=_sum_adamw([recv_ukv], w_ukv[0], m_w_ukv[0], v_w_ukv[0], "adamw_w_ukv"),
    )
    small_names = ["norm_g", "b_gate", "lb_logits", "hg_norm_g", "q_a_g", "kv_a_g", "final_norm_g"]
    loss_row, small = _adamw_small(
        recv_small, recv_ng,
        (norm_g, b_gate, lb_logits, hg_norm_g, q_a_g, kv_a_g, final_norm_g.reshape(1, -1)),
        (m_norm_g, m_b_gate, m_lb_logits, m_hg_norm_g, m_q_a_g, m_kv_a_g, m_final_norm_g.reshape(1, -1)),
        (v_norm_g, v_b_gate, v_lb_logits, v_hg_norm_g, v_q_a_g, v_kv_a_g, v_final_norm_g.reshape(1, -1)))
    names = ["norm_g", "w_in", "b_gate", "lb_logits", "hg_norm_g", "q_a_g", "w_uq", "kv_a_g", "w_ukv",
             "w_proj_a", "w_proj_b", "w_out", "final_norm_g"]
    results = []
    for kind in range(4):
        by_name = {n: big[n][kind][None] for n in big}
        by_name.update(zip(small_names, small[kind]))
        by_name["final_norm_g"] = by_name["final_norm_g"].reshape(-1)
        results += [by_name[n] for n in names]
    return (loss_row[0, 0], grad_x[None], *results)
```

```python
import jax
import jax.numpy as jnp
from jax import lax
from jax.experimental import pallas as pl
from jax.experimental.pallas import tpu as pltpu

D_MODEL = 1024
HEADS = 8
HEAD_DIM = 128
HG_CHUNK = 32
Q_LORA = 384
KV_LORA = 256
QK_ROPE = 64
QK_DIM = 192
ROPE_THETA = 10000.0
EPS = 1e-6
IN_COLS = 7872
ADAM_LR = 0.001
ADAM_B1 = 0.9
ADAM_B2 = 0.999
ADAM_EPS = 1e-08
ADAM_WD = 0.01
ADAM_STEP = 10

N_DEV = 8
SEG = 1024
PROJ_W = 8 * SEG
SMALL_SEG = 4
MZ_SEG = 5
GL_SEG = 6
HEAD_PAD = 256
PACK_COLS = 1024
LOSS_LANE = HEAD_DIM + Q_LORA + KV_LORA
ROWS_W_UQ = 72
ROWS_W_UKV = 64
ROWS_W_PROJ = 128
ROWS_OTHER_USED = 3 * ROWS_W_PROJ + ROWS_W_UQ + ROWS_W_UKV
ROWS_OTHER = 528
LATE_DEV = (SMALL_SEG * SEG) // (IN_COLS // N_DEV)
assert (SMALL_SEG * SEG + Q_LORA + KV_LORA + QK_ROPE - 1) // (IN_COLS // N_DEV) == LATE_DEV

QK_SCALE = QK_DIM ** -0.5
LOG2E = 1.4426950408889634
LN2 = 0.6931471805599453
Q_PRESCALE = QK_SCALE * LOG2E

MXU_DTYPE = jnp.bfloat16
TRANSPORT_DTYPE = jnp.bfloat16
VMEM_LIMIT = 48 * 1024 * 1024
VMEM_LIMIT_BIG = 60 * 1024 * 1024

T_HGRN = 256
HG_HEADS_PER_STEP = 4
TM_ROW = 256
T_ATT = 1024
T_ATT_BWD = T_ATT
ATT_SUB = 4
P_SLOTS = 4
LSE_LANE = 127
ATT_SUB_BWD = 2
TS_TN = 2048
TR_ADAM = 256

F32 = jnp.float32
MESH = pl.DeviceIdType.MESH


def _dot(a, b):
    return jnp.dot(a, b, preferred_element_type=F32)


def _dot_nt(a, b):
    return lax.dot_general(a, b, (((1,), (1,)), ((), ())), preferred_element_type=F32)


def _dot_tn(a, b):
    return lax.dot_general(a, b, (((0,), (0,)), ((), ())), preferred_element_type=F32)


def _mx(a):
    return a.astype(MXU_DTYPE)


def _sigmoid(x):
    return 1.0 / (1.0 + jnp.exp(-x))


def _params(vmem=VMEM_LIMIT, **kw):
    return pltpu.CompilerParams(vmem_limit_bytes=vmem, **kw)


def _bcast_rows(row, n):
    return jnp.broadcast_to(row, (n, row.shape[-1]))


def _resident(shape):
    return pl.BlockSpec(shape, lambda *_: (0, 0), pipeline_mode=pl.Buffered(1))


HBM_SPEC = pl.BlockSpec(memory_space=pltpu.HBM)


def _all_gather_packed(shard):
    rows, cols = shard.shape

    def body(x_ref, out_ref, send_sems, recv_sems, local_sem):
        x, y, c = lax.axis_index("x"), lax.axis_index("y"), lax.axis_index("c")
        me, sibling = (x, y, c), (x, y, 1 - c)
        chips = [(1 - x, y), (x, 1 - y), (1 - x, 1 - y)]

        def slot(px, py, pc):
            return out_ref.at[4 * px + 2 * py + pc]

        def copy(k, block, to, src=None):
            return pltpu.make_async_remote_copy(
                src_ref=slot(*block) if src is None else src,
                dst_ref=slot(*block),
                send_sem=send_sems.at[k],
                recv_sem=recv_sems.at[k],
                device_id=to,
                device_id_type=MESH,
            )

        mine = pltpu.make_async_copy(x_ref, slot(*me), local_sem)
        mine.start()
        first = [copy(0, me, sibling, src=x_ref)]
        first += [copy(1 + j, me, (*chip, c), src=x_ref) for j, chip in enumerate(chips)]
        for cp in first:
            cp.start()
        passed = [copy(4 + j, (*chip, c), sibling) for j, chip in enumerate(chips)]
        for j, chip in enumerate(chips):
            copy(1 + j, (*chip, c), me).wait_recv()
            passed[j].start()
        copy(0, sibling, me).wait_recv()
        for j, chip in enumerate(chips):
            copy(4 + j, (*chip, 1 - c), me).wait_recv()
        for cp in first + passed:
            cp.wait_send()
        mine.wait()

    return pl.pallas_call(
        body,
        name="ag_weights",
        out_shape=jax.ShapeDtypeStruct((N_DEV, rows, cols), shard.dtype),
        in_specs=[HBM_SPEC],
        out_specs=HBM_SPEC,
        scratch_shapes=[
            pltpu.SemaphoreType.DMA((7,)),
            pltpu.SemaphoreType.DMA((7,)),
            pltpu.SemaphoreType.DMA,
        ],
    )(shard)


class _Exchange:
    def __init__(self, g_refs, recv_refs, send_sems, recv_sems, local_sems, gather=False):
        x, y, c = lax.axis_index("x"), lax.axis_index("y"), lax.axis_index("c")
        me = 4 * x + 2 * y + c
        n_ops = len(g_refs)

        def source(i, dest):
            return g_refs[i] if gather else g_refs[i].at[dest]

        def copy(i, k, landing):
            px, py, pc = x ^ ((k >> 2) & 1), y ^ ((k >> 1) & 1), c ^ (k & 1)
            peer = 4 * px + 2 * py + pc
            return pltpu.make_async_remote_copy(
                src_ref=source(i, peer),
                dst_ref=recv_refs[i].at[peer if landing else me],
                send_sem=send_sems.at[i * (N_DEV - 1) + k - 1],
                recv_sem=recv_sems.at[i * (N_DEV - 1) + k - 1],
                device_id=(px, py, pc),
                device_id_type=MESH,
            )

        pairs = [(i, k) for i in range(n_ops) for k in range(1, N_DEV)]
        self.mine = lambda: [pltpu.make_async_copy(source(i, me), recv_refs[i].at[me], local_sems.at[i])
                             for i in range(n_ops)]
        self.sends = lambda: [copy(i, k, False) for i, k in pairs]
        self.landings = lambda: [copy(i, k, True) for i, k in pairs]

    def start(self):
        for cp in self.mine() + self.sends():
            cp.start()

    def wait(self):
        for cp in self.landings():
            cp.wait_recv()
        for cp in self.sends():
            cp.wait_send()
        for cp in self.mine():
            cp.wait()

    @staticmethod
    def semaphores(n_ops):
        return [pltpu.SemaphoreType.DMA((n_ops * (N_DEV - 1),)),
                pltpu.SemaphoreType.DMA((n_ops * (N_DEV - 1),)),
                pltpu.SemaphoreType.DMA((n_ops,))]


def _exchange_rows(slabs):
    def body(g_ref, recv_ref, send_sems, recv_sems, local_sems):
        exchange = _Exchange([g_ref], [recv_ref], send_sems, recv_sems, local_sems)
        exchange.start()
        exchange.wait()

    return pl.pallas_call(
        body,
        name="exchange_rows",
        out_shape=jax.ShapeDtypeStruct(slabs.shape, slabs.dtype),
        in_specs=[HBM_SPEC],
        out_specs=HBM_SPEC,
        scratch_shapes=_Exchange.semaphores(1),
    )(slabs)


def _sum_adamw(recvs, w, m, v, name):
    rows, cols = w.shape
    tr = TR_ADAM if rows % TR_ADAM == 0 else rows
    n_recv = len(recvs)

    def body(*refs):
        w_ref, m_ref, v_ref, g_out, d_out, m_out, v_out = refs[n_recv:]
        g = None
        for r_ref in refs[:n_recv]:
            for i in range(N_DEV):
                part = r_ref[i].astype(F32)
                g = part if g is None else g + part
        g = g[:, :cols]
        m_new = ADAM_B1 * m_ref[...] + (1.0 - ADAM_B1) * g
        v_new = ADAM_B2 * v_ref[...] + (1.0 - ADAM_B2) * (g * g)
        m_hat = m_new / (1.0 - ADAM_B1 ** ADAM_STEP)
        v_hat = v_new / (1.0 - ADAM_B2 ** ADAM_STEP)
        g_out[...] = g
        d_out[...] = -ADAM_LR * (m_hat / (jnp.sqrt(v_hat) + ADAM_EPS) + ADAM_WD * w_ref[...])
        m_out[...] = m_new
        v_out[...] = v_new

    row_spec = pl.BlockSpec((tr, cols), lambda i: (i, 0))
    shape = jax.ShapeDtypeStruct((rows, cols), F32)
    return pl.pallas_call(
        body,
        name=name,
        grid=(rows // tr,),
        in_specs=[pl.BlockSpec((N_DEV, tr, recvs[0].shape[2]), lambda i: (0, i, 0))] * n_recv + [row_spec] * 3,
        out_specs=[row_spec] * 4,
        out_shape=[shape] * 4,
        compiler_params=_params(),
    )(*recvs, w, m, v)


def _inproj(x, norm_g, w_in_p, other_shard):
    s = x.shape[0]
    tm = min(s, TM_ROW)
    nsteps = s // tm

    def body(x_ref, g_ref, w_ref, shard_ref, proj_ref, h_ref, gathered_ref, send_sems, recv_sems, local_sems):
        i = pl.program_id(0)

        def all_gather():
            return _Exchange([shard_ref], [gathered_ref], send_sems, recv_sems, local_sems, gather=True)

        @pl.when(i == 0)
        def _():
            all_gather().start()

        xf = x_ref[...]
        r = lax.rsqrt(jnp.mean(xf * xf, axis=-1, keepdims=True) + EPS)
        h = _mx(xf * r * g_ref[...])
        h_ref[...] = h
        for j in range(PROJ_W // SEG):
            cols = slice(j * SEG, (j + 1) * SEG)
            proj_ref[:, cols] = _dot(h, w_ref[:, cols])

        @pl.when(i == nsteps - 1)
        def _():
            all_gather().wait()

    return pl.pallas_call(
        body,
        name="inproj",
        grid=(nsteps,),
        in_specs=[
            pl.BlockSpec((tm, D_MODEL), lambda i: (i, 0)),
            pl.BlockSpec((1, D_MODEL), lambda i: (0, 0)),
            _resident((D_MODEL, PROJ_W)),
            HBM_SPEC,
        ],
        out_specs=[
            pl.BlockSpec((tm, PROJ_W), lambda i: (i, 0)),
            pl.BlockSpec((tm, D_MODEL), lambda i: (i, 0)),
            HBM_SPEC,
        ],
        out_shape=[
            jax.ShapeDtypeStruct((s, PROJ_W), F32),
            jax.ShapeDtypeStruct((s, D_MODEL), MXU_DTYPE),
            jax.ShapeDtypeStruct((N_DEV,) + other_shard.shape, other_shard.dtype),
        ],
        scratch_shapes=_Exchange.semaphores(1),
        compiler_params=_params(),
    )(x, norm_g, w_in_p, other_shard)


def _chunk_lower_mask(t):
    row = lax.broadcasted_iota(jnp.int32, (t, t), 0)
    col = lax.broadcasted_iota(jnp.int32, (t, t), 1)
    return ((row // HG_CHUNK) == (col // HG_CHUNK)) & (col <= row)


def _chunk_pos(t):
    return lax.broadcasted_iota(jnp.int32, (t, HEAD_DIM), 0) & (HG_CHUNK - 1)


def _cumsum_chunk(x, pos):
    sh = 1
    while sh < HG_CHUNK:
        x = x + jnp.where(pos >= sh, pltpu.roll(x, sh, 0), 0.0)
        sh *= 2
    return x


def _rcumsum_chunk(x, pos):
    t = x.shape[0]
    sh = 1
    while sh < HG_CHUNK:
        x = x + jnp.where(pos < HG_CHUNK - sh, pltpu.roll(x, t - sh, 0), 0.0)
        sh *= 2
    return x


def _chunk_total(x):
    t, w = x.shape
    tot = jnp.sum(x.reshape(t // HG_CHUNK, HG_CHUNK, w), axis=1, keepdims=True)
    return jnp.broadcast_to(tot, (t // HG_CHUNK, HG_CHUNK, w)).reshape(t, w)


def _hgrn_gates(hq, hf, lb_logits, pos):
    lb = _sigmoid(lb_logits[0:1, :] - lb_logits[1:2, :])
    sig = _sigmoid(hf)
    f = lb + (1.0 - lb) * sig
    sq = _sigmoid(hq)
    q = hq * sq
    k = 1.0 - f
    logf = jnp.log(f)
    bcum = _cumsum_chunk(logf, pos)
    blast = _chunk_total(logf)
    eb = jnp.exp(bcum)
    enb = jnp.exp(-bcum)
    eo = jnp.exp(blast - bcum)
    return dict(lb=lb, sig=sig, f=f, sq=sq, q=q, k=k, eb=eb, enb=enb, eo=eo,
                qi=q * eb, ki=k * enb, ko=k * eo, dec=jnp.exp(blast))


def _hgrn_fwd(proj, lb_logits, hg_norm_g):
    s = proj.shape[0]
    t = min(s, T_HGRN)
    nb = s // t
    nc = t // HG_CHUNK
    hw = HG_HEADS_PER_STEP * HEAD_DIM

    def body(hq_ref, hf_ref, hi_ref, hz_ref, lb_ref, g_ref, o_ref, ya_ref, st_ref, state, u_sc, stb_sc):
        b = pl.program_id(1)

        @pl.when(b == 0)
        def _():
            state[...] = jnp.zeros_like(state)

        lower = _chunk_lower_mask(t)
        pos = _chunk_pos(t)
        for hh in range(HG_HEADS_PER_STEP):
            cols = slice(hh * HEAD_DIM, (hh + 1) * HEAD_DIM)
            st = state[hh]
            st_ref[0, hh] = st
            gt = _hgrn_gates(hq_ref[:, cols], hf_ref[:, cols], lb_ref[:, cols], pos)
            vb = _mx(hi_ref[:, cols])
            qib, kib, kob = _mx(gt["qi"]), _mx(gt["ki"]), _mx(gt["ko"])
            a = jnp.where(lower, _dot_nt(qib, kib), 0.0)
            o_intra = _dot(_mx(a), vb)
            for c in range(nc):
                sl = slice(c * HG_CHUNK, (c + 1) * HG_CHUNK)
                u_sc[hh, c] = _dot_tn(vb[sl], kob[sl])
            for c in range(nc):
                stb_sc[hh, c] = _mx(st)
                st = st * gt["dec"][c * HG_CHUNK:c * HG_CHUNK + 1, :] + u_sc[hh, c]
            state[hh] = st
            outs = []
            for c in range(nc):
                sl = slice(c * HG_CHUNK, (c + 1) * HG_CHUNK)
                outs.append(o_intra[sl] + _dot_nt(qib[sl], stb_sc[hh, c]))
            o = jnp.concatenate(outs, axis=0)
            o_ref[:, cols] = o
            r = lax.rsqrt(jnp.mean(o * o, axis=-1, keepdims=True) + EPS)
            hz = hz_ref[:, cols]
            ya_ref[:, cols] = _mx((o * r * g_ref[...]) * (hz * _sigmoid(hz)))

    hsteps = HEADS // HG_HEADS_PER_STEP

    def seg(k):
        return pl.BlockSpec((t, hw), lambda h, b, k=k: (b, k * hsteps + h))

    return pl.pallas_call(
        body,
        name="hgrn_fwd",
        grid=(hsteps, nb),
        in_specs=[seg(0), seg(1), seg(2), seg(3),
                  pl.BlockSpec((2, hw), lambda h, b: (0, h)),
                  pl.BlockSpec((1, HEAD_DIM), lambda h, b: (0, 0))],
        out_specs=[
            pl.BlockSpec((t, hw), lambda h, b: (b, h)),
            pl.BlockSpec((t, hw), lambda h, b: (b, h)),
            pl.BlockSpec((1, HG_HEADS_PER_STEP, HEAD_DIM, HEAD_DIM), lambda h, b: (b, h, 0, 0)),
        ],
        out_shape=[
            jax.ShapeDtypeStruct((s, D_MODEL), F32),
            jax.ShapeDtypeStruct((s, D_MODEL), MXU_DTYPE),
            jax.ShapeDtypeStruct((nb, HEADS, HEAD_DIM, HEAD_DIM), F32),
        ],
        scratch_shapes=[pltpu.VMEM((HG_HEADS_PER_STEP, HEAD_DIM, HEAD_DIM), F32),
                        pltpu.VMEM((HG_HEADS_PER_STEP, nc, HEAD_DIM, HEAD_DIM), F32),
                        pltpu.VMEM((HG_HEADS_PER_STEP, nc, HEAD_DIM, HEAD_DIM), MXU_DTYPE)],
        compiler_params=_params(),
    )(proj, proj, proj, proj, lb_logits, hg_norm_g)


def _rope(x, cos, sin_a, sin_b):
    return x * cos + pltpu.roll(x, 96, 1) * sin_a + pltpu.roll(x, 32, 1) * sin_b


def _rope_t(d, cos, sin_a, sin_b):
    return d * cos + pltpu.roll(d * sin_a, 32, 1) + pltpu.roll(d * sin_b, 96, 1)


def _mla_prep(proj, q_a_g, kv_a_g, w_uq_p, w_kn, w_v, cos, sin_a, sin_b):
    s = proj.shape[0]
    tm = min(s, TM_ROW)

    def body(sm_ref, gq_ref, gk_ref, wq_ref, wkn_ref, wv_ref, cos_ref, sa_ref, sb_ref,
             q_ref, k_ref, v_ref, cqn_ref, ckvn_ref):
        small = sm_ref[...]
        cq = small[:, :Q_LORA]
        ckv = small[:, Q_LORA:Q_LORA + KV_LORA]
        krp = small[:, Q_LORA + KV_LORA:Q_LORA + KV_LORA + HEAD_DIM]
        rq = lax.rsqrt(jnp.mean(cq * cq, axis=-1, keepdims=True) + EPS)
        rk = lax.rsqrt(jnp.mean(ckv * ckv, axis=-1, keepdims=True) + EPS)
        cqn = _mx(cq * rq * gq_ref[...])
        ckvn = _mx(ckv * rk * gk_ref[...])
        cqn_ref[...] = cqn
        ckvn_ref[...] = ckvn
        q = _dot(cqn, wq_ref[...]) * Q_PRESCALE
        kn = _dot(ckvn, wkn_ref[...])
        v = _dot(ckvn, wv_ref[...])
        cos_t, sa, sb = cos_ref[...], sa_ref[...], sb_ref[...]
        kpe = _mx(_rope(krp, cos_t, sa, sb))
        ones_col = (lax.broadcasted_iota(jnp.int32, (tm, HEAD_DIM), 1) == 0).astype(MXU_DTYPE)
        for h in range(HEADS):
            lo = h * HEAD_PAD
            v_ref[:, lo:lo + HEAD_DIM] = _mx(v[:, h * HEAD_DIM:(h + 1) * HEAD_DIM])
            v_ref[:, lo + HEAD_DIM:lo + HEAD_PAD] = ones_col
            q_ref[:, lo:lo + HEAD_DIM] = _mx(q[:, lo:lo + HEAD_DIM])
            q_ref[:, lo + HEAD_DIM:lo + HEAD_PAD] = _mx(_rope(q[:, lo + HEAD_DIM:lo + HEAD_PAD], cos_t, sa, sb))
            k_ref[:, lo:lo + HEAD_DIM] = _mx(kn[:, h * HEAD_DIM:(h + 1) * HEAD_DIM])
            k_ref[:, lo + HEAD_DIM:lo + HEAD_PAD] = kpe

    def const(shape):
        return pl.BlockSpec(shape, lambda i: (0, 0))

    def rows(w):
        return pl.BlockSpec((tm, w), lambda i: (i, 0))

    return pl.pallas_call(
        body,
        name="mla_prep",
        grid=(s // tm,),
        in_specs=[
            pl.BlockSpec((tm, SEG), lambda i: (i, SMALL_SEG)),
            const((1, Q_LORA)), const((1, KV_LORA)),
            const((Q_LORA, HEADS * HEAD_PAD)), const((KV_LORA, D_MODEL)), const((KV_LORA, D_MODEL)),
            rows(HEAD_DIM), rows(HEAD_DIM), rows(HEAD_DIM),
        ],
        out_specs=[rows(HEADS * HEAD_PAD)] * 3 + [rows(Q_LORA), rows(KV_LORA)],
        out_shape=[
            jax.ShapeDtypeStruct((s, HEADS * HEAD_PAD), MXU_DTYPE),
            jax.ShapeDtypeStruct((s, HEADS * HEAD_PAD), MXU_DTYPE),
            jax.ShapeDtypeStruct((s, HEADS * HEAD_PAD), MXU_DTYPE),
            jax.ShapeDtypeStruct((s, Q_LORA), MXU_DTYPE),
            jax.ShapeDtypeStruct((s, KV_LORA), MXU_DTYPE),
        ],
        compiler_params=_params(),
    )(proj, q_a_g, kv_a_g, w_uq_p, w_kn, w_v, cos, sin_a, sin_b)


def _flash_fwd(q_all, k_all, v_all, proj):
    s = q_all.shape[0]
    t = min(s, T_ATT)
    n = s // t
    ts = t // ATT_SUB
    n_pairs = n * (n + 1) // 2

    def body(q_ref, k_ref, v_ref, mz_ref, ao_ref, yb_ref, mblk_ref, p_hbm, m_sc, acc_sc, stage, p_sems):
        head, qi = pl.program_id(0), pl.program_id(1)
        m_sc[...] = jnp.full_like(m_sc, -jnp.inf)
        acc_sc[...] = jnp.zeros_like(acc_sc)
        mblk_ref[...] = jnp.zeros_like(mblk_ref)
        lane = lax.broadcasted_iota(jnp.int32, (ts, HEAD_DIM), 1)
        first_block = head * n_pairs + qi * (qi + 1) // 2

        def p_copy(slot, pair, r):
            rows = pl.ds(r * ts, ts)
            return pltpu.make_async_copy(stage.at[slot, rows], p_hbm.at[head, pair, rows], p_sems.at[slot, r])

        def p_wait(slot):
            for r in range(ATT_SUB):
                p_copy(slot, 0, r).wait()

        def key_block(ki, diagonal):
            base = pl.multiple_of(ki * t, t)
            count = first_block + ki
            slot = lax.rem(count, P_SLOTS)
            sc, pb, alpha = {}, {}, {}

            @pl.when(count >= P_SLOTS)
            def _():
                p_wait(slot)

            if diagonal:
                stage[slot] = jnp.zeros((t, t), MXU_DTYPE)

            def width(r):
                return (r + 1) * ts if diagonal else t

            def scores(r):
                w = width(r)
                s_r = _dot_nt(q_ref[r * ts:(r + 1) * ts], k_ref[pl.ds(base, w), :])
                if diagonal:
                    row = lax.broadcasted_iota(jnp.int32, (ts, w), 0) + r * ts
                    col = lax.broadcasted_iota(jnp.int32, (ts, w), 1)
                    s_r = jnp.where(row >= col, s_r, -jnp.inf)
                sc[r] = s_r

            def softmax(r):
                rs = slice(r * ts, (r + 1) * ts)
                m_prev = m_sc[rs]
                m_new = jnp.maximum(m_prev, jnp.max(sc[r], axis=-1, keepdims=True))
                pb[r] = _mx(jnp.exp2(sc[r] - m_new))
                alpha[r] = jnp.exp2(m_prev - m_new)
                m_sc[rs] = m_new
                mblk_ref[rs] = jnp.where(lane == ki, m_new, mblk_ref[rs])
                stage[slot, rs, :width(r)] = pb[r]
                p_copy(slot, qi * (qi + 1) // 2 + ki, r).start()

            def weighted_values(r):
                rs = slice(r * ts, (r + 1) * ts)
                acc_sc[rs] = alpha[r] * acc_sc[rs] + _dot(pb[r], v_ref[pl.ds(base, width(r)), :])

            for step in range(ATT_SUB + 2):
                if step < ATT_SUB:
                    scores(step)
                if 1 <= step <= ATT_SUB:
                    softmax(step - 1)
                if step >= 2:
                    weighted_values(step - 2)

        def below_diagonal(ki, carry):
            key_block(ki, False)
            return carry

        lax.fori_loop(0, qi, below_diagonal, 0)
        key_block(qi, True)

        @pl.when((head == HEADS - 1) & (qi == n - 1))
        def _():
            for slot in range(min(P_SLOTS, HEADS * n_pairs)):
                p_wait(slot)

        acc = acc_sc[...]
        l = acc[:, HEAD_DIM:HEAD_DIM + 1]
        ao = acc[:, :HEAD_DIM] / l
        ao_ref[...] = ao
        lane_t = lax.broadcasted_iota(jnp.int32, (t, HEAD_DIM), 1)
        mblk_ref[...] = jnp.where(lane_t == LSE_LANE, m_sc[...] + jnp.log2(l), mblk_ref[...])
        mz = mz_ref[...]
        yb_ref[...] = _mx(ao * (mz * _sigmoid(mz)))

    q_map = lambda h, qi: (qi, h)
    return pl.pallas_call(
        body,
        name="flash_fwd",
        grid=(HEADS, n),
        in_specs=[
            pl.BlockSpec((t, HEAD_PAD), q_map),
            pl.BlockSpec((s, HEAD_PAD), lambda h, qi: (0, h)),
            pl.BlockSpec((s, HEAD_PAD), lambda h, qi: (0, h)),
            pl.BlockSpec((t, HEAD_DIM), lambda h, qi: (qi, MZ_SEG * HEADS + h)),
        ],
        out_specs=[pl.BlockSpec((t, HEAD_DIM), q_map)] * 3 + [HBM_SPEC],
        out_shape=[
            jax.ShapeDtypeStruct((s, D_MODEL), F32),
            jax.ShapeDtypeStruct((s, D_MODEL), MXU_DTYPE),
            jax.ShapeDtypeStruct((s, D_MODEL), F32),
            jax.ShapeDtypeStruct((HEADS, n_pairs, t, t), MXU_DTYPE),
        ],
        scratch_shapes=[
            pltpu.VMEM((t, 1), F32),
            pltpu.VMEM((t, HEAD_PAD), F32),
            pltpu.VMEM((P_SLOTS, t, t), MXU_DTYPE),
            pltpu.SemaphoreType.DMA((P_SLOTS, ATT_SUB)),
        ],
        compiler_params=_params(),
    )(q_all, k_all, v_all, proj)


def _merge_fwd_bwd(x, target, ya, yb, ao, proj, b_gate, final_g, w_pa, w_pb, w_out):
    s = x.shape[0]
    tm = min(s, TM_ROW)

    def body(x_ref, t_ref, ya_ref, yb_ref, ao_ref, mz_ref, g0_ref, g1_ref, bg_ref, fg_ref, wpa_ref, wpb_ref, wo_ref,
             dx2_ref, dya_ref, dao_ref, dtail_ref, mb_ref, dpab_ref, dpbb_ref, dx2b_ref,
             loss_ref, dfg_ref, dbg_ref):
        i = pl.program_id(0)

        @pl.when(i == 0)
        def _():
            loss_ref[...] = jnp.zeros_like(loss_ref)
            dfg_ref[...] = jnp.zeros_like(dfg_ref)
            dbg_ref[...] = jnp.zeros_like(dbg_ref)

        pa = _dot(ya_ref[...], wpa_ref[...])
        pb = _dot(yb_ref[...], wpb_ref[...])
        bg = bg_ref[...]
        g0 = _sigmoid(g0_ref[...] + bg[:, :D_MODEL])
        g1 = _sigmoid(g1_ref[...] + bg[:, D_MODEL:])
        merged = g0 * pa + g1 * pb
        mb = _mx(merged)
        mb_ref[...] = mb
        x2 = x_ref[...] + _dot(mb, wo_ref[...])
        r = lax.rsqrt(jnp.mean(x2 * x2, axis=-1, keepdims=True) + EPS)
        xn = x2 * r
        fg = fg_ref[...]
        diff = xn * fg - t_ref[...]
        loss_ref[...] += 0.5 * jnp.sum(jnp.mean(diff * diff, axis=-1, keepdims=True))
        dy = diff * (1.0 / D_MODEL)
        dfg_ref[...] += _bcast_rows(jnp.sum(dy * xn, axis=0, keepdims=True), 8)
        tt = dy * fg
        dx2 = r * (tt - xn * jnp.mean(tt * xn, axis=-1, keepdims=True))
        dx2_ref[...] = dx2
        dx2b = _mx(dx2)
        dx2b_ref[...] = dx2b
        dmerged = _dot_nt(dx2b, wo_ref[...])
        dpa = _mx(dmerged * g0)
        dpb = _mx(dmerged * g1)
        dpab_ref[...] = dpa
        dpbb_ref[...] = dpb
        dg0 = dmerged * pa * (g0 * (1.0 - g0))
        dg1 = dmerged * pb * (g1 * (1.0 - g1))
        dtail_ref[1] = _mx(dg0)
        dtail_ref[2] = _mx(dg1)
        dbg_ref[:, :D_MODEL] += _bcast_rows(jnp.sum(dg0, axis=0, keepdims=True), 8)
        dbg_ref[:, D_MODEL:] += _bcast_rows(jnp.sum(dg1, axis=0, keepdims=True), 8)
        dya_ref[...] = _dot_nt(dpa, wpa_ref[...])
        dyb = _dot_nt(dpb, wpb_ref[...])
        mz = mz_ref[...]
        sg = _sigmoid(mz)
        dao_ref[...] = _mx(dyb * (mz * sg))
        dtail_ref[0] = _mx(dyb * ao_ref[...] * (sg + mz * sg * (1.0 - sg)))

    def rows(w=D_MODEL):
        return pl.BlockSpec((tm, w), lambda i: (i, 0))

    def const(shape):
        return pl.BlockSpec(shape, lambda i: (0, 0))

    def seg(k):
        return pl.BlockSpec((tm, SEG), lambda i: (i, k))

    f32 = jax.ShapeDtypeStruct((s, D_MODEL), F32)
    b16 = jax.ShapeDtypeStruct((s, D_MODEL), MXU_DTYPE)
    return pl.pallas_call(
        body,
        name="merge_fwd_bwd",
        grid=(s // tm,),
        in_specs=[
            rows(), rows(), rows(), rows(), rows(),
            seg(MZ_SEG), seg(GL_SEG), seg(GL_SEG + 1),
            const((1, 2 * D_MODEL)), const((1, D_MODEL)),
            _resident((D_MODEL, D_MODEL)), _resident((D_MODEL, D_MODEL)), _resident((D_MODEL, D_MODEL)),
        ],
        out_specs=[rows()] * 3 + [pl.BlockSpec((3, tm, D_MODEL), lambda i: (0, i, 0))] + [rows()] * 4
        + [const((8, HEAD_DIM)), const((8, D_MODEL)), const((8, 2 * D_MODEL))],
        out_shape=[f32, f32, b16, jax.ShapeDtypeStruct((3, s, D_MODEL), MXU_DTYPE), b16, b16, b16, b16,
                   jax.ShapeDtypeStruct((8, HEAD_DIM), F32),
                   jax.ShapeDtypeStruct((8, D_MODEL), F32),
                   jax.ShapeDtypeStruct((8, 2 * D_MODEL), F32)],
        compiler_params=_params(),
    )(x, target, ya, yb, ao, proj, proj, proj, b_gate, final_g, w_pa, w_pb, w_out)


def _flash_bwd(q_all, k_all, v_all, dao, ao, mblk, p_all, slab_sets):
    s = q_all.shape[0]
    t = min(s, T_ATT_BWD)
    n = s // t
    pairs = [(ki, qi) for ki in range(n) for qi in range(ki, n)]
    ki_list = jnp.asarray([p[0] for p in pairs], jnp.int32)
    qi_list = jnp.asarray([p[1] for p in pairs], jnp.int32)
    p_list = jnp.asarray([qi * (qi + 1) // 2 + ki for ki, qi in pairs], jnp.int32)
    n_ops = len(slab_sets)

    def body(ki_ref, qi_ref, pidx_ref, q_ref, k_ref, v_ref, do_ref, ao_ref, mblk_ref, p_ref, *rest):
        g_refs = rest[:n_ops]
        dq_ref, dk_ref, dv_ref = rest[n_ops:n_ops + 3]
        recv_refs = rest[n_ops + 3:2 * n_ops + 3]
        dk_acc, dv_acc, send_sems, recv_sems, local_sems = rest[2 * n_ops + 3:]
        head, step = pl.program_id(0), pl.program_id(1)
        ki, qi = ki_ref[step], qi_ref[step]

        @pl.when((head == 0) & (step == 0))
        def _():
            _Exchange(g_refs, recv_refs, send_sems, recv_sems, local_sems).start()

        @pl.when(qi == ki)
        def _():
            dk_acc[...] = jnp.zeros_like(dk_acc)
            dv_acc[...] = jnp.zeros_like(dv_acc)

        @pl.when(ki == 0)
        def _():
            dq_ref[pl.ds(pl.multiple_of(qi * t, t), t), :] = jnp.zeros((t, HEAD_PAD), F32)

        def pair(masked):
            nsub = ATT_SUB if masked else ATT_SUB_BWD
            ts = t // nsub
            dk_parts, dv_parts = [], []
            for r in range(nsub):
                rs = slice(r * ts, (r + 1) * ts)
                w = (r + 1) * ts if masked else t
                k = k_ref[:w]
                v = v_ref[:w]
                q = q_ref[rs]
                lane = lax.broadcasted_iota(jnp.int32, (ts, HEAD_DIM), 1)
                stats = mblk_ref[rs]
                m_blk = jnp.max(jnp.where(lane == ki, stats, -jnp.inf), axis=-1, keepdims=True)
                lse = jnp.max(jnp.where(lane == LSE_LANE, stats, -jnp.inf), axis=-1, keepdims=True)
                factor = jnp.exp2(m_blk - lse)
                p_st = p_ref[0, 0, rs, :w]
                do = do_ref[rs]
                do_f = do.astype(F32)
                delta = jnp.sum(do_f * ao_ref[rs], axis=-1, keepdims=True)
                dv_part = _dot_tn(p_st, _mx(do_f * factor))
                ds = p_st * _mx((_dot_nt(do, v) - delta) * factor)
                dk_part = _dot_tn(ds, q)
                rows = pl.ds(pl.multiple_of(qi * t + r * ts, ts), ts)
                dq_ref[rows, :] += _dot(ds, k)
                if masked:
                    dk_acc[:w] += dk_part
                    dv_acc[:w] += dv_part
                else:
                    dk_parts.append(dk_part)
                    dv_parts.append(dv_part)

            if not masked:
                dk_acc[...] += sum(dk_parts[1:], dk_parts[0])
                dv_acc[...] += sum(dv_parts[1:], dv_parts[0])

        @pl.when(qi == ki)
        def _():
            pair(True)

        @pl.when(qi > ki)
        def _():
            pair(False)

        @pl.when(qi == n - 1)
        def _():
            dk_ref[...] = _mx(dk_acc[...] * LN2)
            dv_ref[...] = _mx(dv_acc[...])

        @pl.when((head == HEADS - 1) & (step == len(pairs) - 1))
        def _():
            _Exchange(g_refs, recv_refs, send_sems, recv_sems, local_sems).wait()

    q_map = lambda h, p, ki_ref, qi_ref, pidx_ref: (qi_ref[p], h)
    kv_map = lambda h, p, ki_ref, qi_ref, pidx_ref: (ki_ref[p], h)
    grid_spec = pltpu.PrefetchScalarGridSpec(
        num_scalar_prefetch=3,
        grid=(HEADS, len(pairs)),
        in_specs=[
            pl.BlockSpec((t, HEAD_PAD), q_map),
            pl.BlockSpec((t, HEAD_PAD), kv_map),
            pl.BlockSpec((t, HEAD_DIM), lambda h, p, ki_ref, qi_ref, pidx_ref: (ki_ref[p], 2 * h)),
            pl.BlockSpec((t, HEAD_DIM), q_map),
            pl.BlockSpec((t, HEAD_DIM), q_map),
            pl.BlockSpec((t, HEAD_DIM), q_map),
            pl.BlockSpec((1, 1, t, t), lambda h, p, ki_ref, qi_ref, pidx_ref: (h, pidx_ref[p], 0, 0)),
        ] + [HBM_SPEC] * n_ops,
        out_specs=[
            pl.BlockSpec((s, HEAD_PAD), lambda h, p, ki_ref, qi_ref, pidx_ref: (0, h)),
            pl.BlockSpec((t, HEAD_PAD), kv_map),
            pl.BlockSpec((t, HEAD_DIM), kv_map),
        ] + [HBM_SPEC] * n_ops,
        scratch_shapes=[pltpu.VMEM((t, HEAD_PAD), F32), pltpu.VMEM((t, HEAD_DIM), F32)]
        + _Exchange.semaphores(n_ops),
    )
    outs = pl.pallas_call(
        body,
        name="flash_bwd",
        grid_spec=grid_spec,
        out_shape=[
            jax.ShapeDtypeStruct((s, HEADS * HEAD_PAD), F32),
            jax.ShapeDtypeStruct((s, HEADS * HEAD_PAD), MXU_DTYPE),
            jax.ShapeDtypeStruct((s, D_MODEL), MXU_DTYPE),
        ] + [jax.ShapeDtypeStruct(a.shape, a.dtype) for a in slab_sets],
        compiler_params=_params(VMEM_LIMIT_BIG),
    )(ki_list, qi_list, p_list, q_all, k_all, v_all, dao, ao, mblk, p_all, *slab_sets)
    return outs[0], outs[1], outs[2], outs[3:]


def _mla_prep_bwd(dq_all, dk_all, dv_all, proj, q_a_g, kv_a_g, w_uq_p, w_kn, w_v, cos, sin_a, sin_b):
    s = proj.shape[0]
    tm = min(s, TM_ROW)

    def body(dq_ref, dk_ref, dv_ref, sm_ref, gq_ref, gk_ref, wq_ref, wkn_ref, wv_ref, cos_ref, sa_ref, sb_ref,
             dsm_ref, dqf_ref, dkn_ref, dvb_ref, dgq_ref, dgk_ref):
        i = pl.program_id(0)

        @pl.when(i == 0)
        def _():
            dgq_ref[...] = jnp.zeros_like(dgq_ref)
            dgk_ref[...] = jnp.zeros_like(dgk_ref)

        cos_t, sa, sb = cos_ref[...], sa_ref[...], sb_ref[...]
        dkpe = jnp.zeros((tm, HEAD_DIM), F32)
        for h in range(HEADS):
            lo = h * HEAD_PAD
            dqf_ref[:, lo:lo + HEAD_DIM] = _mx(dq_ref[:, lo:lo + HEAD_DIM] * QK_SCALE)
            dqf_ref[:, lo + HEAD_DIM:lo + HEAD_PAD] = _mx(
                _rope_t(dq_ref[:, lo + HEAD_DIM:lo + HEAD_PAD] * QK_SCALE, cos_t, sa, sb))
            dkn_ref[:, h * HEAD_DIM:(h + 1) * HEAD_DIM] = dk_ref[:, lo:lo + HEAD_DIM]
            dkpe = dkpe + dk_ref[:, lo + HEAD_DIM:lo + HEAD_PAD].astype(F32)
        dkr = _rope_t(dkpe, cos_t, sa, sb)
        dvb = dv_ref[...]
        dvb_ref[...] = dvb
        dcqn = _dot_nt(dqf_ref[...], wq_ref[...])
        dckvn = _dot_nt(dkn_ref[...], wkn_ref[...]) + _dot_nt(dvb, wv_ref[...])

        small = sm_ref[...]
        cq = small[:, :Q_LORA]
        ckv = small[:, Q_LORA:Q_LORA + KV_LORA]
        rq = lax.rsqrt(jnp.mean(cq * cq, axis=-1, keepdims=True) + EPS)
        rk = lax.rsqrt(jnp.mean(ckv * ckv, axis=-1, keepdims=True) + EPS)
        cqh = cq * rq
        ckh = ckv * rk
        dgq_ref[...] += _bcast_rows(jnp.sum(dcqn * cqh, axis=0, keepdims=True), 8)
        dgk_ref[...] += _bcast_rows(jnp.sum(dckvn * ckh, axis=0, keepdims=True), 8)
        tq = dcqn * gq_ref[...]
        tk = dckvn * gk_ref[...]
        dcq = rq * (tq - cqh * jnp.mean(tq * cqh, axis=-1, keepdims=True))
        dckv = rk * (tk - ckh * jnp.mean(tk * ckh, axis=-1, keepdims=True))
        dsm_ref[:, :Q_LORA] = _mx(dcq)
        dsm_ref[:, Q_LORA:Q_LORA + KV_LORA] = _mx(dckv)
        dsm_ref[:, Q_LORA + KV_LORA:Q_LORA + KV_LORA + HEAD_DIM] = _mx(dkr)
        dsm_ref[:, Q_LORA + KV_LORA + HEAD_DIM:] = jnp.zeros((tm, SEG - Q_LORA - KV_LORA - HEAD_DIM), MXU_DTYPE)

    def const(shape):
        return pl.BlockSpec(shape, lambda i: (0, 0))

    def rows(w):
        return pl.BlockSpec((tm, w), lambda i: (i, 0))

    return pl.pallas_call(
        body,
        name="mla_prep_bwd",
        grid=(s // tm,),
        in_specs=[
            rows(HEADS * HEAD_PAD), rows(HEADS * HEAD_PAD), rows(D_MODEL),
            pl.BlockSpec((tm, SEG), lambda i: (i, SMALL_SEG)),
            const((1, Q_LORA)), const((1, KV_LORA)),
            const((Q_LORA, HEADS * HEAD_PAD)), const((KV_LORA, D_MODEL)), const((KV_LORA, D_MODEL)),
            rows(HEAD_DIM), rows(HEAD_DIM), rows(HEAD_DIM),
        ],
        out_specs=[rows(SEG), rows(HEADS * HEAD_PAD), rows(D_MODEL), rows(D_MODEL),
                   const((8, Q_LORA)), const((8, KV_LORA))],
        out_shape=[
            jax.ShapeDtypeStruct((s, SEG), MXU_DTYPE),
            jax.ShapeDtypeStruct((s, HEADS * HEAD_PAD), MXU_DTYPE),
            jax.ShapeDtypeStruct((s, D_MODEL), MXU_DTYPE),
            jax.ShapeDtypeStruct((s, D_MODEL), MXU_DTYPE),
            jax.ShapeDtypeStruct((8, Q_LORA), F32),
            jax.ShapeDtypeStruct((8, KV_LORA), F32),
        ],
        compiler_params=_params(),
    )(dq_all, dk_all, dv_all, proj, q_a_g, kv_a_g, w_uq_p, w_kn, w_v, cos, sin_a, sin_b)


def _hgrn_bwd(proj, lb_logits, hg_norm_g, o_all, dya, states):
    s = proj.shape[0]
    t = min(s, T_HGRN)
    nb = s // t
    nc = t // HG_CHUNK

    def body(hq_ref, hf_ref, hi_ref, hz_ref, lb_ref, g_ref, o_ref, dya_ref, st_ref,
             dh4_ref, dlb_ref, dg_ref, dstate, u_sc, g_sc, stf_sc, stb_sc, dstb_sc):
        h, b = pl.program_id(0), pl.program_id(1)

        @pl.when(b == 0)
        def _():
            dstate[...] = jnp.zeros_like(dstate)
            dlb_ref[...] = jnp.zeros_like(dlb_ref)

        @pl.when((b == 0) & (h == 0))
        def _():
            dg_ref[...] = jnp.zeros_like(dg_ref)

        lower = _chunk_lower_mask(t)
        pos = _chunk_pos(t)
        ghg = g_ref[...]
        for hh in range(HG_HEADS_PER_STEP):
            cols = slice(hh * HEAD_DIM, (hh + 1) * HEAD_DIM)
            hq, hf, hz = hq_ref[:, cols], hf_ref[:, cols], hz_ref[:, cols]
            gt = _hgrn_gates(hq, hf, lb_ref[:, cols], pos)
            vb = _mx(hi_ref[:, cols])
            qi, ki, ko = gt["qi"], gt["ki"], gt["ko"]
            qib, kib, kob = _mx(qi), _mx(ki), _mx(ko)

            o = o_ref[:, cols]
            sz = _sigmoid(hz)
            r = lax.rsqrt(jnp.mean(o * o, axis=-1, keepdims=True) + EPS)
            on = o * r
            dya_t = dya_ref[:, cols]
            don = dya_t * (hz * sz)
            dh4_ref[3, :, cols] = _mx(dya_t * (on * ghg) * (sz + hz * sz * (1.0 - sz)))
            dg_ref[...] += _bcast_rows(jnp.sum(don * on, axis=0, keepdims=True), 8)
            tt = don * ghg
            do = r * (tt - on * jnp.mean(tt * on, axis=-1, keepdims=True))
            dob = _mx(do)

            for c in range(nc):
                sl = slice(c * HG_CHUNK, (c + 1) * HG_CHUNK)
                u_sc[hh, c] = _dot_tn(vb[sl], kob[sl])
                g_sc[hh, c] = _dot_tn(dob[sl], qib[sl])

            st = st_ref[0, hh]
            for c in range(nc):
                stf_sc[hh, c] = st
                stb_sc[hh, c] = _mx(st)
                if c < nc - 1:
                    st = st * gt["dec"][c * HG_CHUNK:c * HG_CHUNK + 1, :] + u_sc[hh, c]

            dst = dstate[hh]
            dd_parts = [None] * nc
            for c in reversed(range(nc)):
                dec = gt["dec"][c * HG_CHUNK:c * HG_CHUNK + 1, :]
                dstb_sc[hh, c] = _mx(dst)
                dd_parts[c] = _bcast_rows(jnp.sum(dst * stf_sc[hh, c], axis=0, keepdims=True) * dec, HG_CHUNK)
                dst = dst * dec + g_sc[hh, c]
            dstate[hh] = dst

            a = jnp.where(lower, _dot_nt(qib, kib), 0.0)
            da = _mx(jnp.where(lower, _dot_nt(dob, vb), 0.0))
            dqi_intra = _dot(da, kib)
            dki = _dot_tn(da, qib)
            dv_intra = _dot_tn(_mx(a), dob)

            dqi_parts, dko_parts, dv_parts = [None] * nc, [None] * nc, [None] * nc
            for c in range(nc):
                sl = slice(c * HG_CHUNK, (c + 1) * HG_CHUNK)
                dv_parts[c] = dv_intra[sl] + _dot_nt(kob[sl], dstb_sc[hh, c])
                dko_parts[c] = _dot(vb[sl], dstb_sc[hh, c])
                dqi_parts[c] = dqi_intra[sl] + _dot(dob[sl], stb_sc[hh, c])
            dqi = jnp.concatenate(dqi_parts, axis=0)
            dko = jnp.concatenate(dko_parts, axis=0)
            dv = jnp.concatenate(dv_parts, axis=0)
            dd = jnp.concatenate(dd_parts, axis=0)

            dq = dqi * gt["eb"]
            dk = dki * gt["enb"] + dko * gt["eo"]
            db = dqi * qi - dki * ki - dko * ko
            dlogf = _rcumsum_chunk(db, pos) + _chunk_total(dko * ko) + dd
            df = dlogf / gt["f"] - dk
            lb, sig, sq = gt["lb"], gt["sig"], gt["sq"]
            dh4_ref[1, :, cols] = _mx(df * (1.0 - lb) * (sig * (1.0 - sig)))
            dh4_ref[0, :, cols] = _mx(dq * (sq + hq * sq * (1.0 - sq)))
            dh4_ref[2, :, cols] = _mx(dv)
            dlb = jnp.sum(df * (1.0 - sig), axis=0, keepdims=True) * (lb * (1.0 - lb))
            dlb_ref[:, cols] += jnp.concatenate([dlb, -dlb], axis=0)

    hw = HG_HEADS_PER_STEP * HEAD_DIM
    hsteps = HEADS // HG_HEADS_PER_STEP

    def seg(k):
        return pl.BlockSpec((t, hw), lambda h, b, k=k: (nb - 1 - b, k * hsteps + h))

    blk = pl.BlockSpec((t, hw), lambda h, b: (nb - 1 - b, h))
    return pl.pallas_call(
        body,
        name="hgrn_bwd",
        grid=(hsteps, nb),
        in_specs=[seg(0), seg(1), seg(2), seg(3),
                  pl.BlockSpec((2, hw), lambda h, b: (0, h)),
                  pl.BlockSpec((1, HEAD_DIM), lambda h, b: (0, 0)),
                  blk, blk,
                  pl.BlockSpec((1, HG_HEADS_PER_STEP, HEAD_DIM, HEAD_DIM), lambda h, b: (nb - 1 - b, h, 0, 0))],
        out_specs=[pl.BlockSpec((4, t, hw), lambda h, b: (0, nb - 1 - b, h)),
                   pl.BlockSpec((2, hw), lambda h, b: (0, h)),
                   pl.BlockSpec((8, HEAD_DIM), lambda h, b: (0, 0))],
        out_shape=[jax.ShapeDtypeStruct((4, s, D_MODEL), MXU_DTYPE),
                   jax.ShapeDtypeStruct((2, D_MODEL), F32),
                   jax.ShapeDtypeStruct((8, HEAD_DIM), F32)],
        scratch_shapes=[pltpu.VMEM((HG_HEADS_PER_STEP, HEAD_DIM, HEAD_DIM), F32)]
        + [pltpu.VMEM((HG_HEADS_PER_STEP, nc, HEAD_DIM, HEAD_DIM), F32)] * 3
        + [pltpu.VMEM((HG_HEADS_PER_STEP, nc, HEAD_DIM, HEAD_DIM), MXU_DTYPE)] * 2,
        compiler_params=_params(),
    )(proj, proj, proj, proj, lb_logits, hg_norm_g, o_all, dya, states)


def _dh_bwd(segs, w_in_p, x, dx2, norm_g, late_slab, late_recv_init, slab_sets):
    s = x.shape[0]
    tm = min(s, TM_ROW)
    seg_ops = [sg if isinstance(sg, tuple) else (sg, None) for sg in segs]
    nseg = len(segs)
    nsteps = s // tm
    n_ops = len(slab_sets)
    late_xyc = ((LATE_DEV >> 2) & 1, (LATE_DEV >> 1) & 1, LATE_DEV & 1)

    def body(*refs):
        seg_refs = refs[:nseg]
        w_ref, x_ref, dx2_ref, g_ref, late_ref, _ = refs[nseg:nseg + 6]
        g_refs = refs[nseg + 6:nseg + 6 + n_ops]
        gx_ref, dng_ref, late_recv_ref = refs[nseg + 6 + n_ops:nseg + 9 + n_ops]
        recv_refs = refs[nseg + 9 + n_ops:nseg + 9 + 2 * n_ops]
        (dp_buf, send_sems, recv_sems, local_sems,
         late_send, late_recvs, late_local) = refs[nseg + 9 + 2 * n_ops:]
        i = pl.program_id(0)
        me = 4 * lax.axis_index("x") + 2 * lax.axis_index("y") + lax.axis_index("c")

        def misc_exchange():
            return _Exchange(g_refs, recv_refs, send_sems, recv_sems, local_sems)

        def late_copy(sender):
            return pltpu.make_async_remote_copy(
                src_ref=late_ref.at[0], dst_ref=late_recv_ref.at[sender], send_sem=late_send,
                recv_sem=late_recvs.at[(sender ^ LATE_DEV) - 1], device_id=late_xyc, device_id_type=MESH)

        def late_own():
            return pltpu.make_async_copy(late_ref.at[0], late_recv_ref.at[LATE_DEV], late_local)

        @pl.when(i == 0)
        def _():
            dng_ref[...] = jnp.zeros_like(dng_ref)
            misc_exchange().start()

        @pl.when((i == 0) & (me != LATE_DEV))
        def _():
            late_copy(me).start()

        @pl.when((i == 0) & (me == LATE_DEV))
        def _():
            late_own().start()

        for k, sref in enumerate(seg_refs):
            dp_buf[:, k * SEG:(k + 1) * SEG] = sref[...]
        dh = _dot_nt(dp_buf[...], w_ref[...])
        xf = x_ref[...]
        r = lax.rsqrt(jnp.mean(xf * xf, axis=-1, keepdims=True) + EPS)
        xh = xf * r
        dng_ref[...] += _bcast_rows(jnp.sum(dh * xh, axis=0, keepdims=True), 8)
        tt = dh * g_ref[...]
        gx_ref[...] = dx2_ref[...] + r * (tt - xh * jnp.mean(tt * xh, axis=-1, keepdims=True))

        @pl.when(i == nsteps - 1)
        def _():
            misc_exchange().wait()

        @pl.when((i == nsteps - 1) & (me != LATE_DEV))
        def _():
            late_copy(me).wait_send()

        @pl.when((i == nsteps - 1) & (me == LATE_DEV))
        def _():
            for k in range(1, N_DEV):
                late_copy(LATE_DEV ^ k).wait_recv()
            late_own().wait()

    rows = pl.BlockSpec((tm, D_MODEL), lambda i: (i, 0))
    return pl.pallas_call(
        body,
        name="dh_bwd",
        grid=(nsteps,),
        in_specs=[pl.BlockSpec((tm, SEG), lambda i: (i, 0)) if j is None
                  else pl.BlockSpec((None, tm, SEG), lambda i, j=j: (j, i, 0)) for _, j in seg_ops] + [
            _resident((D_MODEL, PROJ_W)),
            rows, rows,
            pl.BlockSpec((1, D_MODEL), lambda i: (0, 0)),
            HBM_SPEC, HBM_SPEC,
        ] + [HBM_SPEC] * n_ops,
        out_specs=[rows, pl.BlockSpec((8, D_MODEL), lambda i: (0, 0)), HBM_SPEC] + [HBM_SPEC] * n_ops,
        out_shape=[jax.ShapeDtypeStruct((s, D_MODEL), F32), jax.ShapeDtypeStruct((8, D_MODEL), F32),
                   jax.ShapeDtypeStruct(late_recv_init.shape, late_recv_init.dtype)]
        + [jax.ShapeDtypeStruct(a.shape, a.dtype) for a in slab_sets],
        input_output_aliases={nseg + 5: 2},
        scratch_shapes=[pltpu.VMEM((tm, PROJ_W), MXU_DTYPE)] + _Exchange.semaphores(n_ops)
        + [pltpu.SemaphoreType.DMA, pltpu.SemaphoreType.DMA((N_DEV - 1,)), pltpu.SemaphoreType.DMA],
        compiler_params=_params(),
    )(*[a for a, _ in seg_ops], w_in_p, x, dx2, norm_g, late_slab, late_recv_init, *slab_sets)


def _matmul_tn(a, b, name, out_dtype=F32):
    s, m = a.shape
    stacked = b.ndim == 3
    n = b.shape[0] * b.shape[2] if stacked else b.shape[1]
    ts = min(s, TS_TN)
    tn = min(n, SEG)
    nk = s // ts
    if stacked:
        b_spec = pl.BlockSpec((None, ts, tn), lambda j, k: (j, k, 0))
    else:
        b_spec = pl.BlockSpec((ts, tn), lambda j, k: (k, j))

    def body(a_ref, b_ref, o_ref, acc):
        k = pl.program_id(1)
        part = _dot_tn(a_ref[...], b_ref[...])

        @pl.when(k == 0)
        def _():
            acc[...] = part

        @pl.when(k > 0)
        def _():
            acc[...] += part

        @pl.when(k == nk - 1)
        def _():
            o_ref[...] = acc[...].astype(out_dtype)

    return pl.pallas_call(
        body,
        name=name,
        grid=(n // tn, nk),
        in_specs=[pl.BlockSpec((ts, m), lambda j, k: (k, 0)), b_spec],
        out_specs=pl.BlockSpec((m, tn), lambda j, k: (0, j)),
        out_shape=jax.ShapeDtypeStruct((m, n), out_dtype),
        scratch_shapes=[pltpu.VMEM((m, tn), F32)],
        compiler_params=_params(),
    )(a, b)


def _w_in_pieces():
    per = IN_COLS // N_DEV
    pad_at = SMALL_SEG * SEG + Q_LORA + KV_LORA + QK_ROPE
    pieces = []
    for j in range(N_DEV):
        u0, u1 = j * per, (j + 1) * per
        cuts = [u0] + ([pad_at] if u0 < pad_at < u1 else []) + [u1]
        for a, b in zip(cuts[:-1], cuts[1:]):
            pieces.append((j, a - u0, b - u0, a if a < pad_at else a + PROJ_W - IN_COLS))
    return pad_at, pieces


def _assemble_w_in(gathered):
    tr = TM_ROW
    pad_at, pieces = _w_in_pieces()

    def body(in_ref, out_ref):
        out_ref[:, pad_at:pad_at + PROJ_W - IN_COLS] = jnp.zeros((tr, PROJ_W - IN_COLS), gathered.dtype)
        for j, a, b, p0 in pieces:
            out_ref[:, p0:p0 + b - a] = in_ref[j, :, a:b]

    return pl.pallas_call(
        body,
        name="assemble_w_in",
        grid=(D_MODEL // tr,),
        in_specs=[pl.BlockSpec((N_DEV, tr, PACK_COLS), lambda i: (0, i, 0))],
        out_specs=pl.BlockSpec((tr, PROJ_W), lambda i: (i, 0)),
        out_shape=jax.ShapeDtypeStruct((D_MODEL, PROJ_W), gathered.dtype),
        compiler_params=_params(),
    )(gathered)


def _scatter_dw_in(dw_segs, devs, name):
    tr = TM_ROW
    _, pieces = _w_in_pieces()
    per = IN_COLS // N_DEV
    seg_ids = sorted(dw_segs)
    nseg = len(seg_ids)
    seg_ops = [dw_segs[k] if isinstance(dw_segs[k], tuple) else (dw_segs[k], 0) for k in seg_ids]

    def body(*refs):
        out_ref, buf = refs[nseg:]
        for k in range(PROJ_W // SEG):
            if k in seg_ids:
                buf[:, k * SEG:(k + 1) * SEG] = refs[seg_ids.index(k)][...]
            else:
                buf[:, k * SEG:(k + 1) * SEG] = jnp.zeros((tr, SEG), F32)
        for slot, dev in enumerate(devs):
            out_ref[slot, :, per:] = jnp.zeros((tr, PACK_COLS - per), TRANSPORT_DTYPE)
            for j, a, b, p0 in pieces:
                if j == dev:
                    out_ref[slot, :, a:b] = buf[:, p0:p0 + b - a].astype(TRANSPORT_DTYPE)

    return pl.pallas_call(
        body,
        name=name,
        grid=(D_MODEL // tr,),
        in_specs=[pl.BlockSpec((tr, SEG), lambda i, j=j: (i, j)) for _, j in seg_ops],
        out_specs=pl.BlockSpec((len(devs), tr, PACK_COLS), lambda i: (0, i, 0)),
        out_shape=jax.ShapeDtypeStruct((len(devs), D_MODEL, PACK_COLS), TRANSPORT_DTYPE),
        scratch_shapes=[pltpu.VMEM((tr, PROJ_W), F32)],
        compiler_params=_params(),
    )(*[a for a, _ in seg_ops])


def _rope_tables(s):
    inv = ROPE_THETA ** (-jnp.arange(0, QK_ROPE, 2, dtype=F32) / QK_ROPE)
    ang = jnp.arange(s, dtype=F32)[:, None] * inv[None, :]
    cos, sin = jnp.cos(ang), jnp.sin(ang)
    z32 = jnp.zeros_like(cos)
    z64 = jnp.zeros((s, HEAD_DIM - QK_ROPE), F32)
    cos_t = jnp.concatenate([cos, cos, z64], axis=1)
    sin_a = jnp.concatenate([-sin, z32, z64], axis=1)
    sin_b = jnp.concatenate([z32, sin, z64], axis=1)
    return cos_t, sin_a, sin_b


def _small_rows(b_gate, lb_logits, hg_norm_g, q_a_g, kv_a_g, final_norm_g, loss):
    row5 = jnp.concatenate([hg_norm_g.reshape(1, -1), q_a_g.reshape(1, -1), kv_a_g.reshape(1, -1), loss.reshape(1, 1),
                            jnp.zeros((1, PACK_COLS - LOSS_LANE - 1), F32)], axis=1)
    zero_row = jnp.zeros((1, PACK_COLS), F32)
    return jnp.concatenate([zero_row, b_gate.reshape(2, -1), lb_logits, row5, final_norm_g.reshape(1, -1), zero_row],
                           axis=0)


def _adamw_small(recv_small, recv_norm_g, weights, m, v):
    n = len(weights)

    def body(rs_ref, rn_ref, *refs):
        w_refs, m_refs, v_refs = refs[:n], refs[n:2 * n], refs[2 * n:3 * n]
        loss_ref = refs[3 * n]
        outs = refs[3 * n + 1:]
        gs, gn = rs_ref[0], rn_ref[0]
        for i in range(1, N_DEV):
            gs = gs + rs_ref[i]
            gn = gn + rn_ref[i]
        loss_ref[...] = gs[5:6, LOSS_LANE:LOSS_LANE + HEAD_DIM]
        grads = [gn[0:1], jnp.concatenate([gs[1:2], gs[2:3]], axis=1), gs[3:5],
                 gs[5:6, :HEAD_DIM], gs[5:6, HEAD_DIM:HEAD_DIM + Q_LORA], gs[5:6, HEAD_DIM + Q_LORA:LOSS_LANE], gs[6:7]]
        for k, g in enumerate(grads):
            m_new = ADAM_B1 * m_refs[k][...] + (1.0 - ADAM_B1) * g
            v_new = ADAM_B2 * v_refs[k][...] + (1.0 - ADAM_B2) * (g * g)
            m_hat = m_new / (1.0 - ADAM_B1 ** ADAM_STEP)
            v_hat = v_new / (1.0 - ADAM_B2 ** ADAM_STEP)
            outs[k][...] = g
            outs[n + k][...] = -ADAM_LR * (m_hat / (jnp.sqrt(v_hat) + ADAM_EPS) + ADAM_WD * w_refs[k][...])
            outs[2 * n + k][...] = m_new
            outs[3 * n + k][...] = v_new

    shapes = [jax.ShapeDtypeStruct(w.shape, F32) for w in weights]
    res = pl.pallas_call(
        body,
        name="adamw_small",
        out_shape=[jax.ShapeDtypeStruct((1, HEAD_DIM), F32)] + shapes * 4,
        compiler_params=_params(),
    )(recv_small, recv_norm_g, *weights, *m, *v)
    return res[0], [res[1 + k * n:1 + (k + 1) * n] for k in range(4)]


def _weight_shard_buffers(w_in, w_uq, w_ukv, w_pa, w_pb, w_out):
    w_in_pad = jnp.pad(w_in.reshape(D_MODEL, -1), ((0, 0), (0, PACK_COLS - IN_COLS // N_DEV)))
    parts = [a.reshape(-1, PACK_COLS) for a in (w_pa, w_pb, w_out, w_uq, w_ukv)]
    others = jnp.concatenate(parts + [jnp.zeros((ROWS_OTHER - ROWS_OTHER_USED, PACK_COLS), F32)], axis=0)
    return w_in_pad.astype(MXU_DTYPE), others.astype(MXU_DTYPE)


def _other_weights(gathered):
    r0 = 0
    mats = []
    for _ in range(3):
        mats.append(gathered[:, r0:r0 + ROWS_W_PROJ].reshape(D_MODEL, D_MODEL))
        r0 += ROWS_W_PROJ
    w_uq = gathered[:, r0:r0 + ROWS_W_UQ].reshape(N_DEV, Q_LORA, QK_DIM).transpose(1, 0, 2)
    w_uq_p = jnp.concatenate([w_uq, jnp.zeros((Q_LORA, HEADS, HEAD_PAD - QK_DIM), w_uq.dtype)], axis=2)
    w_uq_p = w_uq_p.reshape(Q_LORA, HEADS * HEAD_PAD)
    r0 += ROWS_W_UQ
    w_ukv = gathered[:, r0:r0 + ROWS_W_UKV].reshape(N_DEV, KV_LORA, 2 * HEAD_DIM).transpose(1, 0, 2)
    w_kn = w_ukv[:, :, :HEAD_DIM].reshape(KV_LORA, D_MODEL)
    w_v = w_ukv[:, :, HEAD_DIM:].reshape(KV_LORA, D_MODEL)
    return w_uq_p, w_kn, w_v, mats[0], mats[1], mats[2]


def _late_slab_sets(dw_uq_p, dw_kn, dw_v, small_rows):
    uq = dw_uq_p.reshape(Q_LORA, HEADS, HEAD_PAD).transpose(1, 0, 2)
    ukv = jnp.concatenate([dw_kn.reshape(KV_LORA, HEADS, HEAD_DIM),
                           dw_v.reshape(KV_LORA, HEADS, HEAD_DIM)], axis=2).transpose(1, 0, 2)
    return [uq, ukv, jnp.broadcast_to(small_rows[None], (N_DEV,) + small_rows.shape)]


def _step_gradients(x, target, norm_g, b_gate, lb_logits, hg_norm_g, q_a_g, kv_a_g, final_g,
                    w_in_p, other_shard):
    s = x.shape[0]
    cos, sin_a, sin_b = _rope_tables(s)
    proj, h, gathered = _inproj(x, norm_g, w_in_p, other_shard)
    w_uq_p, w_kn, w_v, w_pa, w_pb, w_out = _other_weights(gathered)
    o_all, ya, states = _hgrn_fwd(proj, lb_logits, hg_norm_g)
    q_all, k_all, v_all, cqn, ckvn = _mla_prep(proj, q_a_g, kv_a_g, w_uq_p, w_kn, w_v, cos, sin_a, sin_b)
    ao, yb, mblk, p_all = _flash_fwd(q_all, k_all, v_all, proj)
    (dx2, dya, dao, d_tail, merged_b, dpa_b, dpb_b, dx2_b,
     loss_acc, dfg_acc, dbg_acc) = _merge_fwd_bwd(x, target, ya, yb, ao, proj, b_gate, final_g, w_pa, w_pb, w_out)
    d_head, dlb, dhg_acc = _hgrn_bwd(proj, lb_logits, hg_norm_g, o_all, dya, states)

    dw_head = _matmul_tn(h, d_head, "dw_in_head")
    dw_tail = _matmul_tn(h, d_tail, "dw_in_tail")
    dw_early = {k: (dw_head, k) for k in range(4)}
    dw_early.update({MZ_SEG + k: (dw_tail, k) for k in range(3)})
    mats = [_matmul_tn(a, b, name, TRANSPORT_DTYPE).reshape(N_DEV, ROWS_W_PROJ, PACK_COLS)
            for a, b, name in ((ya, dpa_b, "dw_pa"), (yb, dpb_b, "dw_pb"), (merged_b, dx2_b, "dw_out"))]
    early_slabs = [_scatter_dw_in(dw_early, list(range(N_DEV)), "scatter_dw_in")] + mats
    dq_all, dk_all, dv_all, (recv_in, recv_pa, recv_pb, recv_out) = _flash_bwd(
        q_all, k_all, v_all, dao, ao, mblk, p_all, early_slabs)

    dsmall, dqf_b, dkn_b, dv_b, dgq_acc, dgk_acc = _mla_prep_bwd(
        dq_all, dk_all, dv_all, proj, q_a_g, kv_a_g, w_uq_p, w_kn, w_v, cos, sin_a, sin_b)
    late_slab = _scatter_dw_in({SMALL_SEG: _matmul_tn(h, dsmall, "dw_in_%d" % SMALL_SEG)}, [LATE_DEV],
                               "scatter_dw_in_late")
    segs = [(d_head, k) for k in range(4)] + [dsmall] + [(d_tail, k) for k in range(3)]
    return dict(
        segs=segs, h=h, dx2=dx2, late_slab=late_slab,
        dw_uq_p=_matmul_tn(cqn, dqf_b, "dw_uq"), dw_kn=_matmul_tn(ckvn, dkn_b, "dw_kn"),
        dw_v=_matmul_tn(ckvn, dv_b, "dw_v"),
        small=dict(b_gate=dbg_acc[0:1], lb_logits=dlb, hg_norm_g=dhg_acc[0:1], q_a_g=dgq_acc[0:1],
                   kv_a_g=dgk_acc[0:1], final_norm_g=dfg_acc[0], loss=loss_acc[0, 0]),
        recv=dict(w_in=[recv_in], w_pa=[recv_pa], w_pb=[recv_pb], w_out=[recv_out]),
    )


def kernel(x, norm_g, w_in, b_gate, lb_logits, hg_norm_g, q_a_g, w_uq, kv_a_g, w_ukv, w_proj_a, w_proj_b, w_out, final_norm_g, loss_target, m_norm_g, m_w_in, m_b_gate, m_lb_logits, m_hg_norm_g, m_q_a_g, m_w_uq, m_kv_a_g, m_w_ukv, m_w_proj_a, m_w_proj_b, m_w_out, m_final_norm_g, v_norm_g, v_w_in, v_b_gate, v_lb_logits, v_hg_norm_g, v_q_a_g, v_w_uq, v_kv_a_g, v_w_ukv, v_w_proj_a, v_w_proj_b, v_w_out, v_final_norm_g):
    xs = x[0]
    w_in_shard, other_shard = _weight_shard_buffers(w_in, w_uq, w_ukv, w_proj_a, w_proj_b, w_out)
    w_in_p = _assemble_w_in(_all_gather_packed(w_in_shard))
    g = _step_gradients(xs, loss_target[0], norm_g, b_gate, lb_logits, hg_norm_g, q_a_g, kv_a_g,
                        final_norm_g.reshape(1, -1), w_in_p, other_shard)
    sm = g["small"]
    late_sets = _late_slab_sets(g["dw_uq_p"], g["dw_kn"], g["dw_v"],
                                _small_rows(sm["b_gate"], sm["lb_logits"], sm["hg_norm_g"], sm["q_a_g"],
                                            sm["kv_a_g"], sm["final_norm_g"], sm["loss"]))
    late_recv_init = jnp.zeros((N_DEV, D_MODEL, PACK_COLS), TRANSPORT_DTYPE)
    grad_x, dng_acc, recv_late, recv_uq, recv_ukv, recv_small = _dh_bwd(
        g["segs"], w_in_p, xs, g["dx2"], norm_g, g["late_slab"], late_recv_init, late_sets)
    recv_ng = _exchange_rows(jnp.broadcast_to(dng_acc[None], (N_DEV, 8, D_MODEL)))

    recv = g["recv"]
    big = dict(
        w_in=_sum_adamw(recv["w_in"] + [recv_late], w_in[0], m_w_in[0], v_w_in[0], "adamw_w_in"),
        w_proj_a=_sum_adamw(recv["w_pa"], w_proj_a[0], m_w_proj_a[0], v_w_proj_a[0], "adamw_w_pa"),
        w_proj_b=_sum_adamw(recv["w_pb"], w_proj_b[0], m_w_proj_b[0], v_w_proj_b[0], "adamw_w_pb"),
        w_out=_sum_adamw(recv["w_out"], w_out[0], m_w_out[0], v_w_out[0], "adamw_w_out"),
        w_uq=_sum_adamw([recv_uq], w_uq[0], m_w_uq[0], v_w_uq[0], "adamw_w_uq"),
        w_ukv=_sum_adamw([recv_ukv], w_ukv[0], m_w_ukv[0], v_w_ukv[0], "adamw_w_ukv"),
    )
    small_names = ["norm_g", "b_gate", "lb_logits", "hg_norm_g", "q_a_g", "kv_a_g", "final_norm_g"]
    loss_row, small = _adamw_small(
        recv_small, recv_ng,
        (norm_g, b_gate, lb_logits, hg_norm_g, q_a_g, kv_a_g, final_norm_g.reshape(1, -1)),
        (m_norm_g, m_b_gate, m_lb_logits, m_hg_norm_g, m_q_a_g, m_kv_a_g, m_final_norm_g.reshape(1, -1)),
        (v_norm_g, v_b_gate, v_lb_logits, v_hg_norm_g, v_q_a_g, v_kv_a_g, v_final_norm_g.reshape(1, -1)))
    names = ["norm_g", "w_in", "b_gate", "lb_logits", "hg_norm_g", "q_a_g", "w_uq", "kv_a_g", "w_ukv",
             "w_proj_a", "w_proj_b", "w_out", "final_norm_g"]
    results = []
    for kind in range(4):
        by_name = {n: big[n][kind][None] for n in big}
        by_name.update(zip(small_names, small[kind]))
        by_name["final_norm_g"] = by_name["final_norm_g"].reshape(-1)
        results += [by_name[n] for n in names]
    return (loss_row[0, 0], grad_x[None], *results)
```

```python
import jax
import jax.numpy as jnp
from jax import lax
from jax.experimental import pallas as pl
from jax.experimental.pallas import tpu as pltpu

D_MODEL = 1024
HEADS = 8
HEAD_DIM = 128
HG_CHUNK = 32
Q_LORA = 384
KV_LORA = 256
QK_ROPE = 64
QK_DIM = 192
ROPE_THETA = 10000.0
EPS = 1e-6
IN_COLS = 7872
ADAM_LR = 0.001
ADAM_B1 = 0.9
ADAM_B2 = 0.999
ADAM_EPS = 1e-08
ADAM_WD = 0.01
ADAM_STEP = 10

N_DEV = 8
SEG = 1024
PROJ_W = 8 * SEG
SMALL_SEG = 4
MZ_SEG = 5
GL_SEG = 6
HEAD_PAD = 256
PACK_COLS = 1024
LOSS_LANE = HEAD_DIM + Q_LORA + KV_LORA
ROWS_W_UQ = 72
ROWS_W_UKV = 64
ROWS_W_PROJ = 128
ROWS_OTHER_USED = 3 * ROWS_W_PROJ + ROWS_W_UQ + ROWS_W_UKV
ROWS_OTHER = 528
LATE_DEV = (SMALL_SEG * SEG) // (IN_COLS // N_DEV)
assert (SMALL_SEG * SEG + Q_LORA + KV_LORA + QK_ROPE - 1) // (IN_COLS // N_DEV) == LATE_DEV

QK_SCALE = QK_DIM ** -0.5
LOG2E = 1.4426950408889634
LN2 = 0.6931471805599453
Q_PRESCALE = QK_SCALE * LOG2E

MXU_DTYPE = jnp.bfloat16
TRANSPORT_DTYPE = jnp.bfloat16
VMEM_LIMIT = 48 * 1024 * 1024
VMEM_LIMIT_BIG = 60 * 1024 * 1024

T_HGRN = 512
HG_HEADS_PER_STEP = 4
TM_ROW = 256
T_ATT = 1024
T_ATT_BWD = T_ATT
ATT_SUB = 4
P_SLOTS = 4
LSE_LANE = 127
ATT_SUB_BWD = 2
TS_TN = 2048
TR_ADAM = 256

F32 = jnp.float32
MESH = pl.DeviceIdType.MESH


def _dot(a, b):
    return jnp.dot(a, b, preferred_element_type=F32)


def _dot_nt(a, b):
    return lax.dot_general(a, b, (((1,), (1,)), ((), ())), preferred_element_type=F32)


def _dot_tn(a, b):
    return lax.dot_general(a, b, (((0,), (0,)), ((), ())), preferred_element_type=F32)


def _mx(a):
    return a.astype(MXU_DTYPE)


def _sigmoid(x):
    return 1.0 / (1.0 + jnp.exp(-x))


def _params(vmem=VMEM_LIMIT, **kw):
    return pltpu.CompilerParams(vmem_limit_bytes=vmem, **kw)


def _bcast_rows(row, n):
    return jnp.broadcast_to(row, (n, row.shape[-1]))


def _resident(shape):
    return pl.BlockSpec(shape, lambda *_: (0, 0), pipeline_mode=pl.Buffered(1))


HBM_SPEC = pl.BlockSpec(memory_space=pltpu.HBM)


def _all_gather_packed(shard):
    rows, cols = shard.shape

    def body(x_ref, out_ref, send_sems, recv_sems, local_sem):
        x, y, c = lax.axis_index("x"), lax.axis_index("y"), lax.axis_index("c")
        me, sibling = (x, y, c), (x, y, 1 - c)
        chips = [(1 - x, y), (x, 1 - y), (1 - x, 1 - y)]

        def slot(px, py, pc):
            return out_ref.at[4 * px + 2 * py + pc]

        def copy(k, block, to, src=None):
            return pltpu.make_async_remote_copy(
                src_ref=slot(*block) if src is None else src,
                dst_ref=slot(*block),
                send_sem=send_sems.at[k],
                recv_sem=recv_sems.at[k],
                device_id=to,
                device_id_type=MESH,
            )

        mine = pltpu.make_async_copy(x_ref, slot(*me), local_sem)
        mine.start()
        first = [copy(0, me, sibling, src=x_ref)]
        first += [copy(1 + j, me, (*chip, c), src=x_ref) for j, chip in enumerate(chips)]
        for cp in first:
            cp.start()
        passed = [copy(4 + j, (*chip, c), sibling) for j, chip in enumerate(chips)]
        for j, chip in enumerate(chips):
            copy(1 + j, (*chip, c), me).wait_recv()
            passed[j].start()
        copy(0, sibling, me).wait_recv()
        for j, chip in enumerate(chips):
            copy(4 + j, (*chip, 1 - c), me).wait_recv()
        for cp in first + passed:
            cp.wait_send()
        mine.wait()

    return pl.pallas_call(
        body,
        name="ag_weights",
        out_shape=jax.ShapeDtypeStruct((N_DEV, rows, cols), shard.dtype),
        in_specs=[HBM_SPEC],
        out_specs=HBM_SPEC,
        scratch_shapes=[
            pltpu.SemaphoreType.DMA((7,)),
            pltpu.SemaphoreType.DMA((7,)),
            pltpu.SemaphoreType.DMA,
        ],
    )(shard)


class _Exchange:
    def __init__(self, g_refs, recv_refs, send_sems, recv_sems, local_sems, gather=False):
        x, y, c = lax.axis_index("x"), lax.axis_index("y"), lax.axis_index("c")
        me = 4 * x + 2 * y + c
        n_ops = len(g_refs)

        def source(i, dest):
            return g_refs[i] if gather else g_refs[i].at[dest]

        def copy(i, k, landing):
            px, py, pc = x ^ ((k >> 2) & 1), y ^ ((k >> 1) & 1), c ^ (k & 1)
            peer = 4 * px + 2 * py + pc
            return pltpu.make_async_remote_copy(
                src_ref=source(i, peer),
                dst_ref=recv_refs[i].at[peer if landing else me],
                send_sem=send_sems.at[i * (N_DEV - 1) + k - 1],
                recv_sem=recv_sems.at[i * (N_DEV - 1) + k - 1],
                device_id=(px, py, pc),
                device_id_type=MESH,
            )

        pairs = [(i, k) for i in range(n_ops) for k in range(1, N_DEV)]
        self.mine = lambda: [pltpu.make_async_copy(source(i, me), recv_refs[i].at[me], local_sems.at[i])
                             for i in range(n_ops)]
        self.sends = lambda: [copy(i, k, False) for i, k in pairs]
        self.landings = lambda: [copy(i, k, True) for i, k in pairs]

    def start(self):
        for cp in self.mine() + self.sends():
            cp.start()

    def wait(self):
        for cp in self.landings():
            cp.wait_recv()
        for cp in self.sends():
            cp.wait_send()
        for cp in self.mine():
            cp.wait()

    @staticmethod
    def semaphores(n_ops):
        return [pltpu.SemaphoreType.DMA((n_ops * (N_DEV - 1),)),
                pltpu.SemaphoreType.DMA((n_ops * (N_DEV - 1),)),
                pltpu.SemaphoreType.DMA((n_ops,))]


def _exchange_rows(slabs):
    def body(g_ref, recv_ref, send_sems, recv_sems, local_sems):
        exchange = _Exchange([g_ref], [recv_ref], send_sems, recv_sems, local_sems)
        exchange.start()
        exchange.wait()

    return pl.pallas_call(
        body,
        name="exchange_rows",
        out_shape=jax.ShapeDtypeStruct(slabs.shape, slabs.dtype),
        in_specs=[HBM_SPEC],
        out_specs=HBM_SPEC,
        scratch_shapes=_Exchange.semaphores(1),
    )(slabs)


def _sum_adamw(recvs, w, m, v, name):
    rows, cols = w.shape
    tr = TR_ADAM if rows % TR_ADAM == 0 else rows
    n_recv = len(recvs)

    def body(*refs):
        w_ref, m_ref, v_ref, g_out, d_out, m_out, v_out = refs[n_recv:]
        g = None
        for r_ref in refs[:n_recv]:
            for i in range(N_DEV):
                part = r_ref[i].astype(F32)
                g = part if g is None else g + part
        g = g[:, :cols]
        m_new = ADAM_B1 * m_ref[...] + (1.0 - ADAM_B1) * g
        v_new = ADAM_B2 * v_ref[...] + (1.0 - ADAM_B2) * (g * g)
        m_hat = m_new / (1.0 - ADAM_B1 ** ADAM_STEP)
        v_hat = v_new / (1.0 - ADAM_B2 ** ADAM_STEP)
        g_out[...] = g
        d_out[...] = -ADAM_LR * (m_hat / (jnp.sqrt(v_hat) + ADAM_EPS) + ADAM_WD * w_ref[...])
        m_out[...] = m_new
        v_out[...] = v_new

    row_spec = pl.BlockSpec((tr, cols), lambda i: (i, 0))
    shape = jax.ShapeDtypeStruct((rows, cols), F32)
    return pl.pallas_call(
        body,
        name=name,
        grid=(rows // tr,),
        in_specs=[pl.BlockSpec((N_DEV, tr, recvs[0].shape[2]), lambda i: (0, i, 0))] * n_recv + [row_spec] * 3,
        out_specs=[row_spec] * 4,
        out_shape=[shape] * 4,
        compiler_params=_params(),
    )(*recvs, w, m, v)


def _inproj(x, norm_g, w_in_p, other_shard):
    s = x.shape[0]
    tm = min(s, TM_ROW)
    nsteps = s // tm

    def body(x_ref, g_ref, w_ref, shard_ref, proj_ref, h_ref, gathered_ref, send_sems, recv_sems, local_sems):
        i = pl.program_id(0)

        def all_gather():
            return _Exchange([shard_ref], [gathered_ref], send_sems, recv_sems, local_sems, gather=True)

        @pl.when(i == 0)
        def _():
            all_gather().start()

        xf = x_ref[...]
        r = lax.rsqrt(jnp.mean(xf * xf, axis=-1, keepdims=True) + EPS)
        h = _mx(xf * r * g_ref[...])
        h_ref[...] = h
        for j in range(PROJ_W // SEG):
            cols = slice(j * SEG, (j + 1) * SEG)
            proj_ref[:, cols] = _dot(h, w_ref[:, cols])

        @pl.when(i == nsteps - 1)
        def _():
            all_gather().wait()

    return pl.pallas_call(
        body,
        name="inproj",
        grid=(nsteps,),
        in_specs=[
            pl.BlockSpec((tm, D_MODEL), lambda i: (i, 0)),
            pl.BlockSpec((1, D_MODEL), lambda i: (0, 0)),
            _resident((D_MODEL, PROJ_W)),
            HBM_SPEC,
        ],
        out_specs=[
            pl.BlockSpec((tm, PROJ_W), lambda i: (i, 0)),
            pl.BlockSpec((tm, D_MODEL), lambda i: (i, 0)),
            HBM_SPEC,
        ],
        out_shape=[
            jax.ShapeDtypeStruct((s, PROJ_W), F32),
            jax.ShapeDtypeStruct((s, D_MODEL), MXU_DTYPE),
            jax.ShapeDtypeStruct((N_DEV,) + other_shard.shape, other_shard.dtype),
        ],
        scratch_shapes=_Exchange.semaphores(1),
        compiler_params=_params(),
    )(x, norm_g, w_in_p, other_shard)


def _chunk_lower_mask(t):
    row = lax.broadcasted_iota(jnp.int32, (t, t), 0)
    col = lax.broadcasted_iota(jnp.int32, (t, t), 1)
    return ((row // HG_CHUNK) == (col // HG_CHUNK)) & (col <= row)


def _chunk_pos(t):
    return lax.broadcasted_iota(jnp.int32, (t, HEAD_DIM), 0) & (HG_CHUNK - 1)


def _cumsum_chunk(x, pos):
    sh = 1
    while sh < HG_CHUNK:
        x = x + jnp.where(pos >= sh, pltpu.roll(x, sh, 0), 0.0)
        sh *= 2
    return x


def _rcumsum_chunk(x, pos):
    t = x.shape[0]
    sh = 1
    while sh < HG_CHUNK:
        x = x + jnp.where(pos < HG_CHUNK - sh, pltpu.roll(x, t - sh, 0), 0.0)
        sh *= 2
    return x


def _chunk_total(x):
    t, w = x.shape
    tot = jnp.sum(x.reshape(t // HG_CHUNK, HG_CHUNK, w), axis=1, keepdims=True)
    return jnp.broadcast_to(tot, (t // HG_CHUNK, HG_CHUNK, w)).reshape(t, w)


def _hgrn_gates(hq, hf, lb_logits, pos):
    lb = _sigmoid(lb_logits[0:1, :] - lb_logits[1:2, :])
    sig = _sigmoid(hf)
    f = lb + (1.0 - lb) * sig
    sq = _sigmoid(hq)
    q = hq * sq
    k = 1.0 - f
    logf = jnp.log(f)
    bcum = _cumsum_chunk(logf, pos)
    blast = _chunk_total(logf)
    eb = jnp.exp(bcum)
    enb = jnp.exp(-bcum)
    eo = jnp.exp(blast - bcum)
    return dict(lb=lb, sig=sig, f=f, sq=sq, q=q, k=k, eb=eb, enb=enb, eo=eo,
                qi=q * eb, ki=k * enb, ko=k * eo, dec=jnp.exp(blast))


def _hgrn_fwd(proj, lb_logits, hg_norm_g):
    s = proj.shape[0]
    t = min(s, T_HGRN)
    nb = s // t
    nc = t // HG_CHUNK
    hw = HG_HEADS_PER_STEP * HEAD_DIM

    def body(hq_ref, hf_ref, hi_ref, hz_ref, lb_ref, g_ref, o_ref, ya_ref, st_ref, state, u_sc, stb_sc):
        b = pl.program_id(1)

        @pl.when(b == 0)
        def _():
            state[...] = jnp.zeros_like(state)

        lower = _chunk_lower_mask(t)
        pos = _chunk_pos(t)
        for hh in range(HG_HEADS_PER_STEP):
            cols = slice(hh * HEAD_DIM, (hh + 1) * HEAD_DIM)
            st = state[hh]
            st_ref[0, hh] = st
            gt = _hgrn_gates(hq_ref[:, cols], hf_ref[:, cols], lb_ref[:, cols], pos)
            vb = _mx(hi_ref[:, cols])
            qib, kib, kob = _mx(gt["qi"]), _mx(gt["ki"]), _mx(gt["ko"])
            a = jnp.where(lower, _dot_nt(qib, kib), 0.0)
            o_intra = _dot(_mx(a), vb)
            for c in range(nc):
                sl = slice(c * HG_CHUNK, (c + 1) * HG_CHUNK)
                u_sc[hh, c] = _dot_tn(vb[sl], kob[sl])
            for c in range(nc):
                stb_sc[hh, c] = _mx(st)
                st = st * gt["dec"][c * HG_CHUNK:c * HG_CHUNK + 1, :] + u_sc[hh, c]
            state[hh] = st
            outs = []
            for c in range(nc):
                sl = slice(c * HG_CHUNK, (c + 1) * HG_CHUNK)
                outs.append(o_intra[sl] + _dot_nt(qib[sl], stb_sc[hh, c]))
            o = jnp.concatenate(outs, axis=0)
            o_ref[:, cols] = o
            r = lax.rsqrt(jnp.mean(o * o, axis=-1, keepdims=True) + EPS)
            hz = hz_ref[:, cols]
            ya_ref[:, cols] = _mx((o * r * g_ref[...]) * (hz * _sigmoid(hz)))

    hsteps = HEADS // HG_HEADS_PER_STEP

    def seg(k):
        return pl.BlockSpec((t, hw), lambda h, b, k=k: (b, k * hsteps + h))

    return pl.pallas_call(
        body,
        name="hgrn_fwd",
        grid=(hsteps, nb),
        in_specs=[seg(0), seg(1), seg(2), seg(3),
                  pl.BlockSpec((2, hw), lambda h, b: (0, h)),
                  pl.BlockSpec((1, HEAD_DIM), lambda h, b: (0, 0))],
        out_specs=[
            pl.BlockSpec((t, hw), lambda h, b: (b, h)),
            pl.BlockSpec((t, hw), lambda h, b: (b, h)),
            pl.BlockSpec((1, HG_HEADS_PER_STEP, HEAD_DIM, HEAD_DIM), lambda h, b: (b, h, 0, 0)),
        ],
        out_shape=[
            jax.ShapeDtypeStruct((s, D_MODEL), F32),
            jax.ShapeDtypeStruct((s, D_MODEL), MXU_DTYPE),
            jax.ShapeDtypeStruct((nb, HEADS, HEAD_DIM, HEAD_DIM), F32),
        ],
        scratch_shapes=[pltpu.VMEM((HG_HEADS_PER_STEP, HEAD_DIM, HEAD_DIM), F32),
                        pltpu.VMEM((HG_HEADS_PER_STEP, nc, HEAD_DIM, HEAD_DIM), F32),
                        pltpu.VMEM((HG_HEADS_PER_STEP, nc, HEAD_DIM, HEAD_DIM), MXU_DTYPE)],
        compiler_params=_params(),
    )(proj, proj, proj, proj, lb_logits, hg_norm_g)


def _rope(x, cos, sin_a, sin_b):
    return x * cos + pltpu.roll(x, 96, 1) * sin_a + pltpu.roll(x, 32, 1) * sin_b


def _rope_t(d, cos, sin_a, sin_b):
    return d * cos + pltpu.roll(d * sin_a, 32, 1) + pltpu.roll(d * sin_b, 96, 1)


def _mla_prep(proj, q_a_g, kv_a_g, w_uq_p, w_kn, w_v, cos, sin_a, sin_b):
    s = proj.shape[0]
    tm = min(s, TM_ROW)

    def body(sm_ref, gq_ref, gk_ref, wq_ref, wkn_ref, wv_ref, cos_ref, sa_ref, sb_ref,
             q_ref, k_ref, v_ref, cqn_ref, ckvn_ref):
        small = sm_ref[...]
        cq = small[:, :Q_LORA]
        ckv = small[:, Q_LORA:Q_LORA + KV_LORA]
        krp = small[:, Q_LORA + KV_LORA:Q_LORA + KV_LORA + HEAD_DIM]
        rq = lax.rsqrt(jnp.mean(cq * cq, axis=-1, keepdims=True) + EPS)
        rk = lax.rsqrt(jnp.mean(ckv * ckv, axis=-1, keepdims=True) + EPS)
        cqn = _mx(cq * rq * gq_ref[...])
        ckvn = _mx(ckv * rk * gk_ref[...])
        cqn_ref[...] = cqn
        ckvn_ref[...] = ckvn
        q = _dot(cqn, wq_ref[...]) * Q_PRESCALE
        kn = _dot(ckvn, wkn_ref[...])
        v = _dot(ckvn, wv_ref[...])
        cos_t, sa, sb = cos_ref[...], sa_ref[...], sb_ref[...]
        kpe = _mx(_rope(krp, cos_t, sa, sb))
        ones_col = (lax.broadcasted_iota(jnp.int32, (tm, HEAD_DIM), 1) == 0).astype(MXU_DTYPE)
        for h in range(HEADS):
            lo = h * HEAD_PAD
            v_ref[:, lo:lo + HEAD_DIM] = _mx(v[:, h * HEAD_DIM:(h + 1) * HEAD_DIM])
            v_ref[:, lo + HEAD_DIM:lo + HEAD_PAD] = ones_col
            q_ref[:, lo:lo + HEAD_DIM] = _mx(q[:, lo:lo + HEAD_DIM])
            q_ref[:, lo + HEAD_DIM:lo + HEAD_PAD] = _mx(_rope(q[:, lo + HEAD_DIM:lo + HEAD_PAD], cos_t, sa, sb))
            k_ref[:, lo:lo + HEAD_DIM] = _mx(kn[:, h * HEAD_DIM:(h + 1) * HEAD_DIM])
            k_ref[:, lo + HEAD_DIM:lo + HEAD_PAD] = kpe

    def const(shape):
        return pl.BlockSpec(shape, lambda i: (0, 0))

    def rows(w):
        return pl.BlockSpec((tm, w), lambda i: (i, 0))

    return pl.pallas_call(
        body,
        name="mla_prep",
        grid=(s // tm,),
        in_specs=[
            pl.BlockSpec((tm, SEG), lambda i: (i, SMALL_SEG)),
            const((1, Q_LORA)), const((1, KV_LORA)),
            const((Q_LORA, HEADS * HEAD_PAD)), const((KV_LORA, D_MODEL)), const((KV_LORA, D_MODEL)),
            rows(HEAD_DIM), rows(HEAD_DIM), rows(HEAD_DIM),
        ],
        out_specs=[rows(HEADS * HEAD_PAD)] * 3 + [rows(Q_LORA), rows(KV_LORA)],
        out_shape=[
            jax.ShapeDtypeStruct((s, HEADS * HEAD_PAD), MXU_DTYPE),
            jax.ShapeDtypeStruct((s, HEADS * HEAD_PAD), MXU_DTYPE),
            jax.ShapeDtypeStruct((s, HEADS * HEAD_PAD), MXU_DTYPE),
            jax.ShapeDtypeStruct((s, Q_LORA), MXU_DTYPE),
            jax.ShapeDtypeStruct((s, KV_LORA), MXU_DTYPE),
        ],
        compiler_params=_params(),
    )(proj, q_a_g, kv_a_g, w_uq_p, w_kn, w_v, cos, sin_a, sin_b)


def _flash_fwd(q_all, k_all, v_all, proj):
    s = q_all.shape[0]
    t = min(s, T_ATT)
    n = s // t
    ts = t // ATT_SUB
    n_pairs = n * (n + 1) // 2

    def body(q_ref, k_ref, v_ref, mz_ref, ao_ref, yb_ref, mblk_ref, p_hbm, m_sc, acc_sc, stage, p_sems):
        head, qi = pl.program_id(0), pl.program_id(1)
        m_sc[...] = jnp.full_like(m_sc, -jnp.inf)
        acc_sc[...] = jnp.zeros_like(acc_sc)
        mblk_ref[...] = jnp.zeros_like(mblk_ref)
        lane = lax.broadcasted_iota(jnp.int32, (ts, HEAD_DIM), 1)
        first_block = head * n_pairs + qi * (qi + 1) // 2

        def p_copy(slot, pair, r):
            rows = pl.ds(r * ts, ts)
            return pltpu.make_async_copy(stage.at[slot, rows], p_hbm.at[head, pair, rows], p_sems.at[slot, r])

        def p_wait(slot):
            for r in range(ATT_SUB):
                p_copy(slot, 0, r).wait()

        def key_block(ki, diagonal):
            base = pl.multiple_of(ki * t, t)
            count = first_block + ki
            slot = lax.rem(count, P_SLOTS)
            sc, pb, alpha = {}, {}, {}

            @pl.when(count >= P_SLOTS)
            def _():
                p_wait(slot)

            if diagonal:
                stage[slot] = jnp.zeros((t, t), MXU_DTYPE)

            def width(r):
                return (r + 1) * ts if diagonal else t

            def scores(r):
                w = width(r)
                s_r = _dot_nt(q_ref[r * ts:(r + 1) * ts], k_ref[pl.ds(base, w), :])
                if diagonal:
                    row = lax.broadcasted_iota(jnp.int32, (ts, w), 0) + r * ts
                    col = lax.broadcasted_iota(jnp.int32, (ts, w), 1)
                    s_r = jnp.where(row >= col, s_r, -jnp.inf)
                sc[r] = s_r

            def softmax(r):
                rs = slice(r * ts, (r + 1) * ts)
                m_prev = m_sc[rs]
                m_new = jnp.maximum(m_prev, jnp.max(sc[r], axis=-1, keepdims=True))
                pb[r] = _mx(jnp.exp2(sc[r] - m_new))
                alpha[r] = jnp.exp2(m_prev - m_new)
                m_sc[rs] = m_new
                mblk_ref[rs] = jnp.where(lane == ki, m_new, mblk_ref[rs])
                stage[slot, rs, :width(r)] = pb[r]
                p_copy(slot, qi * (qi + 1) // 2 + ki, r).start()

            def weighted_values(r):
                rs = slice(r * ts, (r + 1) * ts)
                acc_sc[rs] = alpha[r] * acc_sc[rs] + _dot(pb[r], v_ref[pl.ds(base, width(r)), :])

            for step in range(ATT_SUB + 2):
                if step < ATT_SUB:
                    scores(step)
                if 1 <= step <= ATT_SUB:
                    softmax(step - 1)
                if step >= 2:
                    weighted_values(step - 2)

        def below_diagonal(ki, carry):
            key_block(ki, False)
            return carry

        lax.fori_loop(0, qi, below_diagonal, 0)
        key_block(qi, True)

        @pl.when((head == HEADS - 1) & (qi == n - 1))
        def _():
            for slot in range(min(P_SLOTS, HEADS * n_pairs)):
                p_wait(slot)

        acc = acc_sc[...]
        l = acc[:, HEAD_DIM:HEAD_DIM + 1]
        ao = acc[:, :HEAD_DIM] / l
        ao_ref[...] = ao
        lane_t = lax.broadcasted_iota(jnp.int32, (t, HEAD_DIM), 1)
        mblk_ref[...] = jnp.where(lane_t == LSE_LANE, m_sc[...] + jnp.log2(l), mblk_ref[...])
        mz = mz_ref[...]
        yb_ref[...] = _mx(ao * (mz * _sigmoid(mz)))

    q_map = lambda h, qi: (qi, h)
    return pl.pallas_call(
        body,
        name="flash_fwd",
        grid=(HEADS, n),
        in_specs=[
            pl.BlockSpec((t, HEAD_PAD), q_map),
            pl.BlockSpec((s, HEAD_PAD), lambda h, qi: (0, h)),
            pl.BlockSpec((s, HEAD_PAD), lambda h, qi: (0, h)),
            pl.BlockSpec((t, HEAD_DIM), lambda h, qi: (qi, MZ_SEG * HEADS + h)),
        ],
        out_specs=[pl.BlockSpec((t, HEAD_DIM), q_map)] * 3 + [HBM_SPEC],
        out_shape=[
            jax.ShapeDtypeStruct((s, D_MODEL), F32),
            jax.ShapeDtypeStruct((s, D_MODEL), MXU_DTYPE),
            jax.ShapeDtypeStruct((s, D_MODEL), F32),
            jax.ShapeDtypeStruct((HEADS, n_pairs, t, t), MXU_DTYPE),
        ],
        scratch_shapes=[
            pltpu.VMEM((t, 1), F32),
            pltpu.VMEM((t, HEAD_PAD), F32),
            pltpu.VMEM((P_SLOTS, t, t), MXU_DTYPE),
            pltpu.SemaphoreType.DMA((P_SLOTS, ATT_SUB)),
        ],
        compiler_params=_params(),
    )(q_all, k_all, v_all, proj)


def _merge_fwd_bwd(x, target, ya, yb, ao, proj, b_gate, final_g, w_pa, w_pb, w_out):
    s = x.shape[0]
    tm = min(s, TM_ROW)

    def body(x_ref, t_ref, ya_ref, yb_ref, ao_ref, mz_ref, g0_ref, g1_ref, bg_ref, fg_ref, wpa_ref, wpb_ref, wo_ref,
             dx2_ref, dya_ref, dao_ref, dtail_ref, mb_ref, dpab_ref, dpbb_ref, dx2b_ref,
             loss_ref, dfg_ref, dbg_ref):
        i = pl.program_id(0)

        @pl.when(i == 0)
        def _():
            loss_ref[...] = jnp.zeros_like(loss_ref)
            dfg_ref[...] = jnp.zeros_like(dfg_ref)
            dbg_ref[...] = jnp.zeros_like(dbg_ref)

        pa = _dot(ya_ref[...], wpa_ref[...])
        pb = _dot(yb_ref[...], wpb_ref[...])
        bg = bg_ref[...]
        g0 = _sigmoid(g0_ref[...] + bg[:, :D_MODEL])
        g1 = _sigmoid(g1_ref[...] + bg[:, D_MODEL:])
        merged = g0 * pa + g1 * pb
        mb = _mx(merged)
        mb_ref[...] = mb
        x2 = x_ref[...] + _dot(mb, wo_ref[...])
        r = lax.rsqrt(jnp.mean(x2 * x2, axis=-1, keepdims=True) + EPS)
        xn = x2 * r
        fg = fg_ref[...]
        diff = xn * fg - t_ref[...]
        loss_ref[...] += 0.5 * jnp.sum(jnp.mean(diff * diff, axis=-1, keepdims=True))
        dy = diff * (1.0 / D_MODEL)
        dfg_ref[...] += _bcast_rows(jnp.sum(dy * xn, axis=0, keepdims=True), 8)
        tt = dy * fg
        dx2 = r * (tt - xn * jnp.mean(tt * xn, axis=-1, keepdims=True))
        dx2_ref[...] = dx2
        dx2b = _mx(dx2)
        dx2b_ref[...] = dx2b
        dmerged = _dot_nt(dx2b, wo_ref[...])
        dpa = _mx(dmerged * g0)
        dpb = _mx(dmerged * g1)
        dpab_ref[...] = dpa
        dpbb_ref[...] = dpb
        dg0 = dmerged * pa * (g0 * (1.0 - g0))
        dg1 = dmerged * pb * (g1 * (1.0 - g1))
        dtail_ref[1] = _mx(dg0)
        dtail_ref[2] = _mx(dg1)
        dbg_ref[:, :D_MODEL] += _bcast_rows(jnp.sum(dg0, axis=0, keepdims=True), 8)
        dbg_ref[:, D_MODEL:] += _bcast_rows(jnp.sum(dg1, axis=0, keepdims=True), 8)
        dya_ref[...] = _dot_nt(dpa, wpa_ref[...])
        dyb = _dot_nt(dpb, wpb_ref[...])
        mz = mz_ref[...]
        sg = _sigmoid(mz)
        dao_ref[...] = _mx(dyb * (mz * sg))
        dtail_ref[0] = _mx(dyb * ao_ref[...] * (sg + mz * sg * (1.0 - sg)))

    def rows(w=D_MODEL):
        return pl.BlockSpec((tm, w), lambda i: (i, 0))

    def const(shape):
        return pl.BlockSpec(shape, lambda i: (0, 0))

    def seg(k):
        return pl.BlockSpec((tm, SEG), lambda i: (i, k))

    f32 = jax.ShapeDtypeStruct((s, D_MODEL), F32)
    b16 = jax.ShapeDtypeStruct((s, D_MODEL), MXU_DTYPE)
    return pl.pallas_call(
        body,
        name="merge_fwd_bwd",
        grid=(s // tm,),
        in_specs=[
            rows(), rows(), rows(), rows(), rows(),
            seg(MZ_SEG), seg(GL_SEG), seg(GL_SEG + 1),
            const((1, 2 * D_MODEL)), const((1, D_MODEL)),
            _resident((D_MODEL, D_MODEL)), _resident((D_MODEL, D_MODEL)), _resident((D_MODEL, D_MODEL)),
        ],
        out_specs=[rows()] * 3 + [pl.BlockSpec((3, tm, D_MODEL), lambda i: (0, i, 0))] + [rows()] * 4
        + [const((8, HEAD_DIM)), const((8, D_MODEL)), const((8, 2 * D_MODEL))],
        out_shape=[f32, f32, b16, jax.ShapeDtypeStruct((3, s, D_MODEL), MXU_DTYPE), b16, b16, b16, b16,
                   jax.ShapeDtypeStruct((8, HEAD_DIM), F32),
                   jax.ShapeDtypeStruct((8, D_MODEL), F32),
                   jax.ShapeDtypeStruct((8, 2 * D_MODEL), F32)],
        compiler_params=_params(),
    )(x, target, ya, yb, ao, proj, proj, proj, b_gate, final_g, w_pa, w_pb, w_out)


def _flash_bwd(q_all, k_all, v_all, dao, ao, mblk, p_all, slab_sets):
    s = q_all.shape[0]
    t = min(s, T_ATT_BWD)
    n = s // t
    pairs = [(ki, qi) for ki in range(n) for qi in range(ki, n)]
    ki_list = jnp.asarray([p[0] for p in pairs], jnp.int32)
    qi_list = jnp.asarray([p[1] for p in pairs], jnp.int32)
    p_list = jnp.asarray([qi * (qi + 1) // 2 + ki for ki, qi in pairs], jnp.int32)
    n_ops = len(slab_sets)

    def body(ki_ref, qi_ref, pidx_ref, q_ref, k_ref, v_ref, do_ref, ao_ref, mblk_ref, p_ref, *rest):
        g_refs = rest[:n_ops]
        dq_ref, dk_ref, dv_ref = rest[n_ops:n_ops + 3]
        recv_refs = rest[n_ops + 3:2 * n_ops + 3]
        dk_acc, dv_acc, send_sems, recv_sems, local_sems = rest[2 * n_ops + 3:]
        head, step = pl.program_id(0), pl.program_id(1)
        ki, qi = ki_ref[step], qi_ref[step]

        @pl.when((head == 0) & (step == 0))
        def _():
            _Exchange(g_refs, recv_refs, send_sems, recv_sems, local_sems).start()

        @pl.when(qi == ki)
        def _():
            dk_acc[...] = jnp.zeros_like(dk_acc)
            dv_acc[...] = jnp.zeros_like(dv_acc)

        @pl.when(ki == 0)
        def _():
            dq_ref[pl.ds(pl.multiple_of(qi * t, t), t), :] = jnp.zeros((t, HEAD_PAD), F32)

        def pair(masked):
            nsub = ATT_SUB if masked else ATT_SUB_BWD
            ts = t // nsub
            dk_parts, dv_parts = [], []
            for r in range(nsub):
                rs = slice(r * ts, (r + 1) * ts)
                w = (r + 1) * ts if masked else t
                k = k_ref[:w]
                v = v_ref[:w]
                q = q_ref[rs]
                lane = lax.broadcasted_iota(jnp.int32, (ts, HEAD_DIM), 1)
                stats = mblk_ref[rs]
                m_blk = jnp.max(jnp.where(lane == ki, stats, -jnp.inf), axis=-1, keepdims=True)
                lse = jnp.max(jnp.where(lane == LSE_LANE, stats, -jnp.inf), axis=-1, keepdims=True)
                factor = jnp.exp2(m_blk - lse)
                p_st = p_ref[0, 0, rs, :w]
                do = do_ref[rs]
                do_f = do.astype(F32)
                delta = jnp.sum(do_f * ao_ref[rs], axis=-1, keepdims=True)
                dv_part = _dot_tn(p_st, _mx(do_f * factor))
                ds = p_st * _mx((_dot_nt(do, v) - delta) * factor)
                dk_part = _dot_tn(ds, q)
                rows = pl.ds(pl.multiple_of(qi * t + r * ts, ts), ts)
                dq_ref[rows, :] += _dot(ds, k)
                if masked:
                    dk_acc[:w] += dk_part
                    dv_acc[:w] += dv_part
                else:
                    dk_parts.append(dk_part)
                    dv_parts.append(dv_part)

            if not masked:
                dk_acc[...] += sum(dk_parts[1:], dk_parts[0])
                dv_acc[...] += sum(dv_parts[1:], dv_parts[0])

        @pl.when(qi == ki)
        def _():
            pair(True)

        @pl.when(qi > ki)
        def _():
            pair(False)

        @pl.when(qi == n - 1)
        def _():
            dk_ref[...] = _mx(dk_acc[...] * LN2)
            dv_ref[...] = _mx(dv_acc[...])

        @pl.when((head == HEADS - 1) & (step == len(pairs) - 1))
        def _():
            _Exchange(g_refs, recv_refs, send_sems, recv_sems, local_sems).wait()

    q_map = lambda h, p, ki_ref, qi_ref, pidx_ref: (qi_ref[p], h)
    kv_map = lambda h, p, ki_ref, qi_ref, pidx_ref: (ki_ref[p], h)
    grid_spec = pltpu.PrefetchScalarGridSpec(
        num_scalar_prefetch=3,
        grid=(HEADS, len(pairs)),
        in_specs=[
            pl.BlockSpec((t, HEAD_PAD), q_map),
            pl.BlockSpec((t, HEAD_PAD), kv_map),
            pl.BlockSpec((t, HEAD_DIM), lambda h, p, ki_ref, qi_ref, pidx_ref: (ki_ref[p], 2 * h)),
            pl.BlockSpec((t, HEAD_DIM), q_map),
            pl.BlockSpec((t, HEAD_DIM), q_map),
            pl.BlockSpec((t, HEAD_DIM), q_map),
            pl.BlockSpec((1, 1, t, t), lambda h, p, ki_ref, qi_ref, pidx_ref: (h, pidx_ref[p], 0, 0)),
        ] + [HBM_SPEC] * n_ops,
        out_specs=[
            pl.BlockSpec((s, HEAD_PAD), lambda h, p, ki_ref, qi_ref, pidx_ref: (0, h)),
            pl.BlockSpec((t, HEAD_PAD), kv_map),
            pl.BlockSpec((t, HEAD_DIM), kv_map),
        ] + [HBM_SPEC] * n_ops,
        scratch_shapes=[pltpu.VMEM((t, HEAD_PAD), F32), pltpu.VMEM((t, HEAD_DIM), F32)]
        + _Exchange.semaphores(n_ops),
    )
    outs = pl.pallas_call(
        body,
        name="flash_bwd",
        grid_spec=grid_spec,
        out_shape=[
            jax.ShapeDtypeStruct((s, HEADS * HEAD_PAD), F32),
            jax.ShapeDtypeStruct((s, HEADS * HEAD_PAD), MXU_DTYPE),
            jax.ShapeDtypeStruct((s, D_MODEL), MXU_DTYPE),
        ] + [jax.ShapeDtypeStruct(a.shape, a.dtype) for a in slab_sets],
        compiler_params=_params(VMEM_LIMIT_BIG),
    )(ki_list, qi_list, p_list, q_all, k_all, v_all, dao, ao, mblk, p_all, *slab_sets)
    return outs[0], outs[1], outs[2], outs[3:]


def _mla_prep_bwd(dq_all, dk_all, dv_all, proj, q_a_g, kv_a_g, w_uq_p, w_kn, w_v, cos, sin_a, sin_b):
    s = proj.shape[0]
    tm = min(s, TM_ROW)

    def body(dq_ref, dk_ref, dv_ref, sm_ref, gq_ref, gk_ref, wq_ref, wkn_ref, wv_ref, cos_ref, sa_ref, sb_ref,
             dsm_ref, dqf_ref, dkn_ref, dvb_ref, dgq_ref, dgk_ref):
        i = pl.program_id(0)

        @pl.when(i == 0)
        def _():
            dgq_ref[...] = jnp.zeros_like(dgq_ref)
            dgk_ref[...] = jnp.zeros_like(dgk_ref)

        cos_t, sa, sb = cos_ref[...], sa_ref[...], sb_ref[...]
        dkpe = jnp.zeros((tm, HEAD_DIM), F32)
        for h in range(HEADS):
            lo = h * HEAD_PAD
            dqf_ref[:, lo:lo + HEAD_DIM] = _mx(dq_ref[:, lo:lo + HEAD_DIM] * QK_SCALE)
            dqf_ref[:, lo + HEAD_DIM:lo + HEAD_PAD] = _mx(
                _rope_t(dq_ref[:, lo + HEAD_DIM:lo + HEAD_PAD] * QK_SCALE, cos_t, sa, sb))
            dkn_ref[:, h * HEAD_DIM:(h + 1) * HEAD_DIM] = dk_ref[:, lo:lo + HEAD_DIM]
            dkpe = dkpe + dk_ref[:, lo + HEAD_DIM:lo + HEAD_PAD].astype(F32)
        dkr = _rope_t(dkpe, cos_t, sa, sb)
        dvb = dv_ref[...]
        dvb_ref[...] = dvb
        dcqn = _dot_nt(dqf_ref[...], wq_ref[...])
        dckvn = _dot_nt(dkn_ref[...], wkn_ref[...]) + _dot_nt(dvb, wv_ref[...])

        small = sm_ref[...]
        cq = small[:, :Q_LORA]
        ckv = small[:, Q_LORA:Q_LORA + KV_LORA]
        rq = lax.rsqrt(jnp.mean(cq * cq, axis=-1, keepdims=True) + EPS)
        rk = lax.rsqrt(jnp.mean(ckv * ckv, axis=-1, keepdims=True) + EPS)
        cqh = cq * rq
        ckh = ckv * rk
        dgq_ref[...] += _bcast_rows(jnp.sum(dcqn * cqh, axis=0, keepdims=True), 8)
        dgk_ref[...] += _bcast_rows(jnp.sum(dckvn * ckh, axis=0, keepdims=True), 8)
        tq = dcqn * gq_ref[...]
        tk = dckvn * gk_ref[...]
        dcq = rq * (tq - cqh * jnp.mean(tq * cqh, axis=-1, keepdims=True))
        dckv = rk * (tk - ckh * jnp.mean(tk * ckh, axis=-1, keepdims=True))
        dsm_ref[:, :Q_LORA] = _mx(dcq)
        dsm_ref[:, Q_LORA:Q_LORA + KV_LORA] = _mx(dckv)
        dsm_ref[:, Q_LORA + KV_LORA:Q_LORA + KV_LORA + HEAD_DIM] = _mx(dkr)
        dsm_ref[:, Q_LORA + KV_LORA + HEAD_DIM:] = jnp.zeros((tm, SEG - Q_LORA - KV_LORA - HEAD_DIM), MXU_DTYPE)

    def const(shape):
        return pl.BlockSpec(shape, lambda i: (0, 0))

    def rows(w):
        return pl.BlockSpec((tm, w), lambda i: (i, 0))

    return pl.pallas_call(
        body,
        name="mla_prep_bwd",
        grid=(s // tm,),
        in_specs=[
            rows(HEADS * HEAD_PAD), rows(HEADS * HEAD_PAD), rows(D_MODEL),
            pl.BlockSpec((tm, SEG), lambda i: (i, SMALL_SEG)),
            const((1, Q_LORA)), const((1, KV_LORA)),
            const((Q_LORA, HEADS * HEAD_PAD)), const((KV_LORA, D_MODEL)), const((KV_LORA, D_MODEL)),
            rows(HEAD_DIM), rows(HEAD_DIM), rows(HEAD_DIM),
        ],
        out_specs=[rows(SEG), rows(HEADS * HEAD_PAD), rows(D_MODEL), rows(D_MODEL),
                   const((8, Q_LORA)), const((8, KV_LORA))],
        out_shape=[
            jax.ShapeDtypeStruct((s, SEG), MXU_DTYPE),
            jax.ShapeDtypeStruct((s, HEADS * HEAD_PAD), MXU_DTYPE),
            jax.ShapeDtypeStruct((s, D_MODEL), MXU_DTYPE),
            jax.ShapeDtypeStruct((s, D_MODEL), MXU_DTYPE),
            jax.ShapeDtypeStruct((8, Q_LORA), F32),
            jax.ShapeDtypeStruct((8, KV_LORA), F32),
        ],
        compiler_params=_params(),
    )(dq_all, dk_all, dv_all, proj, q_a_g, kv_a_g, w_uq_p, w_kn, w_v, cos, sin_a, sin_b)


def _hgrn_bwd(proj, lb_logits, hg_norm_g, o_all, dya, states):
    s = proj.shape[0]
    t = min(s, T_HGRN)
    nb = s // t
    nc = t // HG_CHUNK

    def body(hq_ref, hf_ref, hi_ref, hz_ref, lb_ref, g_ref, o_ref, dya_ref, st_ref,
             dh4_ref, dlb_ref, dg_ref, dstate, u_sc, g_sc, stf_sc, stb_sc, dstb_sc):
        h, b = pl.program_id(0), pl.program_id(1)

        @pl.when(b == 0)
        def _():
            dstate[...] = jnp.zeros_like(dstate)
            dlb_ref[...] = jnp.zeros_like(dlb_ref)

        @pl.when((b == 0) & (h == 0))
        def _():
            dg_ref[...] = jnp.zeros_like(dg_ref)

        lower = _chunk_lower_mask(t)
        pos = _chunk_pos(t)
        ghg = g_ref[...]
        for hh in range(HG_HEADS_PER_STEP):
            cols = slice(hh * HEAD_DIM, (hh + 1) * HEAD_DIM)
            hq, hf, hz = hq_ref[:, cols], hf_ref[:, cols], hz_ref[:, cols]
            gt = _hgrn_gates(hq, hf, lb_ref[:, cols], pos)
            vb = _mx(hi_ref[:, cols])
            qi, ki, ko = gt["qi"], gt["ki"], gt["ko"]
            qib, kib, kob = _mx(qi), _mx(ki), _mx(ko)

            o = o_ref[:, cols]
            sz = _sigmoid(hz)
            r = lax.rsqrt(jnp.mean(o * o, axis=-1, keepdims=True) + EPS)
            on = o * r
            dya_t = dya_ref[:, cols]
            don = dya_t * (hz * sz)
            dh4_ref[3, :, cols] = _mx(dya_t * (on * ghg) * (sz + hz * sz * (1.0 - sz)))
            dg_ref[...] += _bcast_rows(jnp.sum(don * on, axis=0, keepdims=True), 8)
            tt = don * ghg
            do = r * (tt - on * jnp.mean(tt * on, axis=-1, keepdims=True))
            dob = _mx(do)

            for c in range(nc):
                sl = slice(c * HG_CHUNK, (c + 1) * HG_CHUNK)
                u_sc[hh, c] = _dot_tn(vb[sl], kob[sl])
                g_sc[hh, c] = _dot_tn(dob[sl], qib[sl])

            st = st_ref[0, hh]
            for c in range(nc):
                stf_sc[hh, c] = st
                stb_sc[hh, c] = _mx(st)
                if c < nc - 1:
                    st = st * gt["dec"][c * HG_CHUNK:c * HG_CHUNK + 1, :] + u_sc[hh, c]

            dst = dstate[hh]
            dd_parts = [None] * nc
            for c in reversed(range(nc)):
                dec = gt["dec"][c * HG_CHUNK:c * HG_CHUNK + 1, :]
                dstb_sc[hh, c] = _mx(dst)
                dd_parts[c] = _bcast_rows(jnp.sum(dst * stf_sc[hh, c], axis=0, keepdims=True) * dec, HG_CHUNK)
                dst = dst * dec + g_sc[hh, c]
            dstate[hh] = dst

            a = jnp.where(lower, _dot_nt(qib, kib), 0.0)
            da = _mx(jnp.where(lower, _dot_nt(dob, vb), 0.0))
            dqi_intra = _dot(da, kib)
            dki = _dot_tn(da, qib)
            dv_intra = _dot_tn(_mx(a), dob)

            dqi_parts, dko_parts, dv_parts = [None] * nc, [None] * nc, [None] * nc
            for c in range(nc):
                sl = slice(c * HG_CHUNK, (c + 1) * HG_CHUNK)
                dv_parts[c] = dv_intra[sl] + _dot_nt(kob[sl], dstb_sc[hh, c])
                dko_parts[c] = _dot(vb[sl], dstb_sc[hh, c])
                dqi_parts[c] = dqi_intra[sl] + _dot(dob[sl], stb_sc[hh, c])
            dqi = jnp.concatenate(dqi_parts, axis=0)
            dko = jnp.concatenate(dko_parts, axis=0)
            dv = jnp.concatenate(dv_parts, axis=0)
            dd = jnp.concatenate(dd_parts, axis=0)

            dq = dqi * gt["eb"]
            dk = dki * gt["enb"] + dko * gt["eo"]
            db = dqi * qi - dki * ki - dko * ko
            dlogf = _rcumsum_chunk(db, pos) + _chunk_total(dko * ko) + dd
            df = dlogf / gt["f"] - dk
            lb, sig, sq = gt["lb"], gt["sig"], gt["sq"]
            dh4_ref[1, :, cols] = _mx(df * (1.0 - lb) * (sig * (1.0 - sig)))
            dh4_ref[0, :, cols] = _mx(dq * (sq + hq * sq * (1.0 - sq)))
            dh4_ref[2, :, cols] = _mx(dv)
            dlb = jnp.sum(df * (1.0 - sig), axis=0, keepdims=True) * (lb * (1.0 - lb))
            dlb_ref[:, cols] += jnp.concatenate([dlb, -dlb], axis=0)

    hw = HG_HEADS_PER_STEP * HEAD_DIM
    hsteps = HEADS // HG_HEADS_PER_STEP

    def seg(k):
        return pl.BlockSpec((t, hw), lambda h, b, k=k: (nb - 1 - b, k * hsteps + h))

    blk = pl.BlockSpec((t, hw), lambda h, b: (nb - 1 - b, h))
    return pl.pallas_call(
        body,
        name="hgrn_bwd",
        grid=(hsteps, nb),
        in_specs=[seg(0), seg(1), seg(2), seg(3),
                  pl.BlockSpec((2, hw), lambda h, b: (0, h)),
                  pl.BlockSpec((1, HEAD_DIM), lambda h, b: (0, 0)),
                  blk, blk,
                  pl.BlockSpec((1, HG_HEADS_PER_STEP, HEAD_DIM, HEAD_DIM), lambda h, b: (nb - 1 - b, h, 0, 0))],
        out_specs=[pl.BlockSpec((4, t, hw), lambda h, b: (0, nb - 1 - b, h)),
                   pl.BlockSpec((2, hw), lambda h, b: (0, h)),
                   pl.BlockSpec((8, HEAD_DIM), lambda h, b: (0, 0))],
        out_shape=[jax.ShapeDtypeStruct((4, s, D_MODEL), MXU_DTYPE),
                   jax.ShapeDtypeStruct((2, D_MODEL), F32),
                   jax.ShapeDtypeStruct((8, HEAD_DIM), F32)],
        scratch_shapes=[pltpu.VMEM((HG_HEADS_PER_STEP, HEAD_DIM, HEAD_DIM), F32)]
        + [pltpu.VMEM((HG_HEADS_PER_STEP, nc, HEAD_DIM, HEAD_DIM), F32)] * 3
        + [pltpu.VMEM((HG_HEADS_PER_STEP, nc, HEAD_DIM, HEAD_DIM), MXU_DTYPE)] * 2,
        compiler_params=_params(),
    )(proj, proj, proj, proj, lb_logits, hg_norm_g, o_all, dya, states)


def _dh_bwd(segs, w_in_p, x, dx2, norm_g, late_slab, late_recv_init, slab_sets):
    s = x.shape[0]
    tm = min(s, TM_ROW)
    seg_ops = [sg if isinstance(sg, tuple) else (sg, None) for sg in segs]
    nseg = len(segs)
    nsteps = s // tm
    n_ops = len(slab_sets)
    late_xyc = ((LATE_DEV >> 2) & 1, (LATE_DEV >> 1) & 1, LATE_DEV & 1)

    def body(*refs):
        seg_refs = refs[:nseg]
        w_ref, x_ref, dx2_ref, g_ref, late_ref, _ = refs[nseg:nseg + 6]
        g_refs = refs[nseg + 6:nseg + 6 + n_ops]
        gx_ref, dng_ref, late_recv_ref = refs[nseg + 6 + n_ops:nseg + 9 + n_ops]
        recv_refs = refs[nseg + 9 + n_ops:nseg + 9 + 2 * n_ops]
        (dp_buf, send_sems, recv_sems, local_sems,
         late_send, late_recvs, late_local) = refs[nseg + 9 + 2 * n_ops:]
        i = pl.program_id(0)
        me = 4 * lax.axis_index("x") + 2 * lax.axis_index("y") + lax.axis_index("c")

        def misc_exchange():
            return _Exchange(g_refs, recv_refs, send_sems, recv_sems, local_sems)

        def late_copy(sender):
            return pltpu.make_async_remote_copy(
                src_ref=late_ref.at[0], dst_ref=late_recv_ref.at[sender], send_sem=late_send,
                recv_sem=late_recvs.at[(sender ^ LATE_DEV) - 1], device_id=late_xyc, device_id_type=MESH)

        def late_own():
            return pltpu.make_async_copy(late_ref.at[0], late_recv_ref.at[LATE_DEV], late_local)

        @pl.when(i == 0)
        def _():
            dng_ref[...] = jnp.zeros_like(dng_ref)
            misc_exchange().start()

        @pl.when((i == 0) & (me != LATE_DEV))
        def _():
            late_copy(me).start()

        @pl.when((i == 0) & (me == LATE_DEV))
        def _():
            late_own().start()

        for k, sref in enumerate(seg_refs):
            dp_buf[:, k * SEG:(k + 1) * SEG] = sref[...]
        dh = _dot_nt(dp_buf[...], w_ref[...])
        xf = x_ref[...]
        r = lax.rsqrt(jnp.mean(xf * xf, axis=-1, keepdims=True) + EPS)
        xh = xf * r
        dng_ref[...] += _bcast_rows(jnp.sum(dh * xh, axis=0, keepdims=True), 8)
        tt = dh * g_ref[...]
        gx_ref[...] = dx2_ref[...] + r * (tt - xh * jnp.mean(tt * xh, axis=-1, keepdims=True))

        @pl.when(i == nsteps - 1)
        def _():
            misc_exchange().wait()

        @pl.when((i == nsteps - 1) & (me != LATE_DEV))
        def _():
            late_copy(me).wait_send()

        @pl.when((i == nsteps - 1) & (me == LATE_DEV))
        def _():
            for k in range(1, N_DEV):
                late_copy(LATE_DEV ^ k).wait_recv()
            late_own().wait()

    rows = pl.BlockSpec((tm, D_MODEL), lambda i: (i, 0))
    return pl.pallas_call(
        body,
        name="dh_bwd",
        grid=(nsteps,),
        in_specs=[pl.BlockSpec((tm, SEG), lambda i: (i, 0)) if j is None
                  else pl.BlockSpec((None, tm, SEG), lambda i, j=j: (j, i, 0)) for _, j in seg_ops] + [
            _resident((D_MODEL, PROJ_W)),
            rows, rows,
            pl.BlockSpec((1, D_MODEL), lambda i: (0, 0)),
            HBM_SPEC, HBM_SPEC,
        ] + [HBM_SPEC] * n_ops,
        out_specs=[rows, pl.BlockSpec((8, D_MODEL), lambda i: (0, 0)), HBM_SPEC] + [HBM_SPEC] * n_ops,
        out_shape=[jax.ShapeDtypeStruct((s, D_MODEL), F32), jax.ShapeDtypeStruct((8, D_MODEL), F32),
                   jax.ShapeDtypeStruct(late_recv_init.shape, late_recv_init.dtype)]
        + [jax.ShapeDtypeStruct(a.shape, a.dtype) for a in slab_sets],
        input_output_aliases={nseg + 5: 2},
        scratch_shapes=[pltpu.VMEM((tm, PROJ_W), MXU_DTYPE)] + _Exchange.semaphores(n_ops)
        + [pltpu.SemaphoreType.DMA, pltpu.SemaphoreType.DMA((N_DEV - 1,)), pltpu.SemaphoreType.DMA],
        compiler_params=_params(),
    )(*[a for a, _ in seg_ops], w_in_p, x, dx2, norm_g, late_slab, late_recv_init, *slab_sets)


def _matmul_tn(a, b, name, out_dtype=F32):
    s, m = a.shape
    stacked = b.ndim == 3
    n = b.shape[0] * b.shape[2] if stacked else b.shape[1]
    ts = min(s, TS_TN)
    tn = min(n, SEG)
    nk = s // ts
    if stacked:
        b_spec = pl.BlockSpec((None, ts, tn), lambda j, k: (j, k, 0))
    else:
        b_spec = pl.BlockSpec((ts, tn), lambda j, k: (k, j))

    def body(a_ref, b_ref, o_ref, acc):
        k = pl.program_id(1)
        part = _dot_tn(a_ref[...], b_ref[...])

        @pl.when(k == 0)
        def _():
            acc[...] = part

        @pl.when(k > 0)
        def _():
            acc[...] += part

        @pl.when(k == nk - 1)
        def _():
            o_ref[...] = acc[...].astype(out_dtype)

    return pl.pallas_call(
        body,
        name=name,
        grid=(n // tn, nk),
        in_specs=[pl.BlockSpec((ts, m), lambda j, k: (k, 0)), b_spec],
        out_specs=pl.BlockSpec((m, tn), lambda j, k: (0, j)),
        out_shape=jax.ShapeDtypeStruct((m, n), out_dtype),
        scratch_shapes=[pltpu.VMEM((m, tn), F32)],
        compiler_params=_params(),
    )(a, b)


def _w_in_pieces():
    per = IN_COLS // N_DEV
    pad_at = SMALL_SEG * SEG + Q_LORA + KV_LORA + QK_ROPE
    pieces = []
    for j in range(N_DEV):
        u0, u1 = j * per, (j + 1) * per
        cuts = [u0] + ([pad_at] if u0 < pad_at < u1 else []) + [u1]
        for a, b in zip(cuts[:-1], cuts[1:]):
            pieces.append((j, a - u0, b - u0, a if a < pad_at else a + PROJ_W - IN_COLS))
    return pad_at, pieces


def _assemble_w_in(gathered):
    tr = TM_ROW
    pad_at, pieces = _w_in_pieces()

    def body(in_ref, out_ref):
        out_ref[:, pad_at:pad_at + PROJ_W - IN_COLS] = jnp.zeros((tr, PROJ_W - IN_COLS), gathered.dtype)
        for j, a, b, p0 in pieces:
            out_ref[:, p0:p0 + b - a] = in_ref[j, :, a:b]

    return pl.pallas_call(
        body,
        name="assemble_w_in",
        grid=(D_MODEL // tr,),
        in_specs=[pl.BlockSpec((N_DEV, tr, PACK_COLS), lambda i: (0, i, 0))],
        out_specs=pl.BlockSpec((tr, PROJ_W), lambda i: (i, 0)),
        out_shape=jax.ShapeDtypeStruct((D_MODEL, PROJ_W), gathered.dtype),
        compiler_params=_params(),
    )(gathered)


def _scatter_dw_in(dw_segs, devs, name):
    tr = TM_ROW
    _, pieces = _w_in_pieces()
    per = IN_COLS // N_DEV
    seg_ids = sorted(dw_segs)
    nseg = len(seg_ids)
    seg_ops = [dw_segs[k] if isinstance(dw_segs[k], tuple) else (dw_segs[k], 0) for k in seg_ids]

    def body(*refs):
        out_ref, buf = refs[nseg:]
        for k in range(PROJ_W // SEG):
            if k in seg_ids:
                buf[:, k * SEG:(k + 1) * SEG] = refs[seg_ids.index(k)][...]
            else:
                buf[:, k * SEG:(k + 1) * SEG] = jnp.zeros((tr, SEG), F32)
        for slot, dev in enumerate(devs):
            out_ref[slot, :, per:] = jnp.zeros((tr, PACK_COLS - per), TRANSPORT_DTYPE)
            for j, a, b, p0 in pieces:
                if j == dev:
                    out_ref[slot, :, a:b] = buf[:, p0:p0 + b - a].astype(TRANSPORT_DTYPE)

    return pl.pallas_call(
        body,
        name=name,
        grid=(D_MODEL // tr,),
        in_specs=[pl.BlockSpec((tr, SEG), lambda i, j=j: (i, j)) for _, j in seg_ops],
        out_specs=pl.BlockSpec((len(devs), tr, PACK_COLS), lambda i: (0, i, 0)),
        out_shape=jax.ShapeDtypeStruct((len(devs), D_MODEL, PACK_COLS), TRANSPORT_DTYPE),
        scratch_shapes=[pltpu.VMEM((tr, PROJ_W), F32)],
        compiler_params=_params(),
    )(*[a for a, _ in seg_ops])


def _rope_tables(s):
    inv = ROPE_THETA ** (-jnp.arange(0, QK_ROPE, 2, dtype=F32) / QK_ROPE)
    ang = jnp.arange(s, dtype=F32)[:, None] * inv[None, :]
    cos, sin = jnp.cos(ang), jnp.sin(ang)
    z32 = jnp.zeros_like(cos)
    z64 = jnp.zeros((s, HEAD_DIM - QK_ROPE), F32)
    cos_t = jnp.concatenate([cos, cos, z64], axis=1)
    sin_a = jnp.concatenate([-sin, z32, z64], axis=1)
    sin_b = jnp.concatenate([z32, sin, z64], axis=1)
    return cos_t, sin_a, sin_b


def _small_rows(b_gate, lb_logits, hg_norm_g, q_a_g, kv_a_g, final_norm_g, loss):
    row5 = jnp.concatenate([hg_norm_g.reshape(1, -1), q_a_g.reshape(1, -1), kv_a_g.reshape(1, -1), loss.reshape(1, 1),
                            jnp.zeros((1, PACK_COLS - LOSS_LANE - 1), F32)], axis=1)
    zero_row = jnp.zeros((1, PACK_COLS), F32)
    return jnp.concatenate([zero_row, b_gate.reshape(2, -1), lb_logits, row5, final_norm_g.reshape(1, -1), zero_row],
                           axis=0)


def _adamw_small(recv_small, recv_norm_g, weights, m, v):
    n = len(weights)

    def body(rs_ref, rn_ref, *refs):
        w_refs, m_refs, v_refs = refs[:n], refs[n:2 * n], refs[2 * n:3 * n]
        loss_ref = refs[3 * n]
        outs = refs[3 * n + 1:]
        gs, gn = rs_ref[0], rn_ref[0]
        for i in range(1, N_DEV):
            gs = gs + rs_ref[i]
            gn = gn + rn_ref[i]
        loss_ref[...] = gs[5:6, LOSS_LANE:LOSS_LANE + HEAD_DIM]
        grads = [gn[0:1], jnp.concatenate([gs[1:2], gs[2:3]], axis=1), gs[3:5],
                 gs[5:6, :HEAD_DIM], gs[5:6, HEAD_DIM:HEAD_DIM + Q_LORA], gs[5:6, HEAD_DIM + Q_LORA:LOSS_LANE], gs[6:7]]
        for k, g in enumerate(grads):
            m_new = ADAM_B1 * m_refs[k][...] + (1.0 - ADAM_B1) * g
            v_new = ADAM_B2 * v_refs[k][...] + (1.0 - ADAM_B2) * (g * g)
            m_hat = m_new / (1.0 - ADAM_B1 ** ADAM_STEP)
            v_hat = v_new / (1.0 - ADAM_B2 ** ADAM_STEP)
            outs[k][...] = g
            outs[n + k][...] = -ADAM_LR * (m_hat / (jnp.sqrt(v_hat) + ADAM_EPS) + ADAM_WD * w_refs[k][...])
            outs[2 * n + k][...] = m_new
            outs[3 * n + k][...] = v_new

    shapes = [jax.ShapeDtypeStruct(w.shape, F32) for w in weights]
    res = pl.pallas_call(
        body,
        name="adamw_small",
        out_shape=[jax.ShapeDtypeStruct((1, HEAD_DIM), F32)] + shapes * 4,
        compiler_params=_params(),
    )(recv_small, recv_norm_g, *weights, *m, *v)
    return res[0], [res[1 + k * n:1 + (k + 1) * n] for k in range(4)]


def _weight_shard_buffers(w_in, w_uq, w_ukv, w_pa, w_pb, w_out):
    w_in_pad = jnp.pad(w_in.reshape(D_MODEL, -1), ((0, 0), (0, PACK_COLS - IN_COLS // N_DEV)))
    parts = [a.reshape(-1, PACK_COLS) for a in (w_pa, w_pb, w_out, w_uq, w_ukv)]
    others = jnp.concatenate(parts + [jnp.zeros((ROWS_OTHER - ROWS_OTHER_USED, PACK_COLS), F32)], axis=0)
    return w_in_pad.astype(MXU_DTYPE), others.astype(MXU_DTYPE)


def _other_weights(gathered):
    r0 = 0
    mats = []
    for _ in range(3):
        mats.append(gathered[:, r0:r0 + ROWS_W_PROJ].reshape(D_MODEL, D_MODEL))
        r0 += ROWS_W_PROJ
    w_uq = gathered[:, r0:r0 + ROWS_W_UQ].reshape(N_DEV, Q_LORA, QK_DIM).transpose(1, 0, 2)
    w_uq_p = jnp.concatenate([w_uq, jnp.zeros((Q_LORA, HEADS, HEAD_PAD - QK_DIM), w_uq.dtype)], axis=2)
    w_uq_p = w_uq_p.reshape(Q_LORA, HEADS * HEAD_PAD)
    r0 += ROWS_W_UQ
    w_ukv = gathered[:, r0:r0 + ROWS_W_UKV].reshape(N_DEV, KV_LORA, 2 * HEAD_DIM).transpose(1, 0, 2)
    w_kn = w_ukv[:, :, :HEAD_DIM].reshape(KV_LORA, D_MODEL)
    w_v = w_ukv[:, :, HEAD_DIM:].reshape(KV_LORA, D_MODEL)
    return w_uq_p, w_kn, w_v, mats[0], mats[1], mats[2]


def _late_slab_sets(dw_uq_p, dw_kn, dw_v, small_rows):
    uq = dw_uq_p.reshape(Q_LORA, HEADS, HEAD_PAD).transpose(1, 0, 2)
    ukv = jnp.concatenate([dw_kn.reshape(KV_LORA, HEADS, HEAD_DIM),
                           dw_v.reshape(KV_LORA, HEADS, HEAD_DIM)], axis=2).transpose(1, 0, 2)
    return [uq, ukv, jnp.broadcast_to(small_rows[None], (N_DEV,) + small_rows.shape)]


def _step_gradients(x, target, norm_g, b_gate, lb_logits, hg_norm_g, q_a_g, kv_a_g, final_g,
                    w_in_p, other_shard):
    s = x.shape[0]
    cos, sin_a, sin_b = _rope_tables(s)
    proj, h, gathered = _inproj(x, norm_g, w_in_p, other_shard)
    w_uq_p, w_kn, w_v, w_pa, w_pb, w_out = _other_weights(gathered)
    o_all, ya, states = _hgrn_fwd(proj, lb_logits, hg_norm_g)
    q_all, k_all, v_all, cqn, ckvn = _mla_prep(proj, q_a_g, kv_a_g, w_uq_p, w_kn, w_v, cos, sin_a, sin_b)
    ao, yb, mblk, p_all = _flash_fwd(q_all, k_all, v_all, proj)
    (dx2, dya, dao, d_tail, merged_b, dpa_b, dpb_b, dx2_b,
     loss_acc, dfg_acc, dbg_acc) = _merge_fwd_bwd(x, target, ya, yb, ao, proj, b_gate, final_g, w_pa, w_pb, w_out)
    d_head, dlb, dhg_acc = _hgrn_bwd(proj, lb_logits, hg_norm_g, o_all, dya, states)

    dw_head = _matmul_tn(h, d_head, "dw_in_head")
    dw_tail = _matmul_tn(h, d_tail, "dw_in_tail")
    dw_early = {k: (dw_head, k) for k in range(4)}
    dw_early.update({MZ_SEG + k: (dw_tail, k) for k in range(3)})
    mats = [_matmul_tn(a, b, name, TRANSPORT_DTYPE).reshape(N_DEV, ROWS_W_PROJ, PACK_COLS)
            for a, b, name in ((ya, dpa_b, "dw_pa"), (yb, dpb_b, "dw_pb"), (merged_b, dx2_b, "dw_out"))]
    early_slabs = [_scatter_dw_in(dw_early, list(range(N_DEV)), "scatter_dw_in")] + mats
    dq_all, dk_all, dv_all, (recv_in, recv_pa, recv_pb, recv_out) = _flash_bwd(
        q_all, k_all, v_all, dao, ao, mblk, p_all, early_slabs)

    dsmall, dqf_b, dkn_b, dv_b, dgq_acc, dgk_acc = _mla_prep_bwd(
        dq_all, dk_all, dv_all, proj, q_a_g, kv_a_g, w_uq_p, w_kn, w_v, cos, sin_a, sin_b)
    late_slab = _scatter_dw_in({SMALL_SEG: _matmul_tn(h, dsmall, "dw_in_%d" % SMALL_SEG)}, [LATE_DEV],
                               "scatter_dw_in_late")
    segs = [(d_head, k) for k in range(4)] + [dsmall] + [(d_tail, k) for k in range(3)]
    return dict(
        segs=segs, h=h, dx2=dx2, late_slab=late_slab,
        dw_uq_p=_matmul_tn(cqn, dqf_b, "dw_uq"), dw_kn=_matmul_tn(ckvn, dkn_b, "dw_kn"),
        dw_v=_matmul_tn(ckvn, dv_b, "dw_v"),
        small=dict(b_gate=dbg_acc[0:1], lb_logits=dlb, hg_norm_g=dhg_acc[0:1], q_a_g=dgq_acc[0:1],
                   kv_a_g=dgk_acc[0:1], final_norm_g=dfg_acc[0], loss=loss_acc[0, 0]),
        recv=dict(w_in=[recv_in], w_pa=[recv_pa], w_pb=[recv_pb], w_out=[recv_out]),
    )


def kernel(x, norm_g, w_in, b_gate, lb_logits, hg_norm_g, q_a_g, w_uq, kv_a_g, w_ukv, w_proj_a, w_proj_b, w_out, final_norm_g, loss_target, m_norm_g, m_w_in, m_b_gate, m_lb_logits, m_hg_norm_g, m_q_a_g, m_w_uq, m_kv_a_g, m_w_ukv, m_w_proj_a, m_w_proj_b, m_w_out, m_final_norm_g, v_norm_g, v_w_in, v_b_gate, v_lb_logits, v_hg_norm_g, v_q_a_g, v_w_uq, v_kv_a_g, v_w_ukv, v_w_proj_a, v_w_proj_b, v_w_out, v_final_norm_g):
    xs = x[0]
    w_in_shard, other_shard = _weight_shard_buffers(w_in, w_uq, w_ukv, w_proj_a, w_proj_b, w_out)
    w_in_p = _assemble_w_in(_all_gather_packed(w_in_shard))
    g = _step_gradients(xs, loss_target[0], norm_g, b_gate, lb_logits, hg_norm_g, q_a_g, kv_a_g,
                        final_norm_g.reshape(1, -1), w_in_p, other_shard)
    sm = g["small"]
    late_sets = _late_slab_sets(g["dw_uq_p"], g["dw_kn"], g["dw_v"],
                                _small_rows(sm["b_gate"], sm["lb_logits"], sm["hg_norm_g"], sm["q_a_g"],
                                            sm["kv_a_g"], sm["final_norm_g"], sm["loss"]))
    late_recv_init = jnp.zeros((N_DEV, D_MODEL, PACK_COLS), TRANSPORT_DTYPE)
    grad_x, dng_acc, recv_late, recv_uq, recv_ukv, recv_small = _dh_bwd(
        g["segs"], w_in_p, xs, g["dx2"], norm_g, g["late_slab"], late_recv_init, late_sets)
    recv_ng = _exchange_rows(jnp.broadcast_to(dng_acc[None], (N_DEV, 8, D_MODEL)))

    recv = g["recv"]
    big = dict(
        w_in=_sum_adamw(recv["w_in"] + [recv_late], w_in[0], m_w_in[0], v_w_in[0], "adamw_w_in"),
        w_proj_a=_sum_adamw(recv["w_pa"], w_proj_a[0], m_w_proj_a[0], v_w_proj_a[0], "adamw_w_pa"),
        w_proj_b=_sum_adamw(recv["w_pb"], w_proj_b[0], m_w_proj_b[0], v_w_proj_b[0], "adamw_w_pb"),
        w_out=_sum_adamw(recv["w_out"], w_out[0], m_w_out[0], v_w_out[0], "adamw_w_out"),
        w_uq=_sum_adamw([recv_uq], w_uq[0], m_w_uq[0], v_w_uq[0], "adamw_w_uq"),
        w_ukv=_sum_adamw([recv_ukv], w_ukv[0], m_w_ukv[0], v_w_ukv[0], "adamw_w_ukv"),
    )
    small_names = ["norm_g", "b_gate", "lb_logits", "hg_norm_g", "q_a_g", "kv_a_g", "final_norm_g"]
    loss_row, small = _adamw_small(
        recv_small, recv_ng,
        (norm_g, b_gate, lb_logits, hg_norm_g, q_a_g, kv_a_g, final_norm_g.reshape(1, -1)),
        (m_norm_g, m_b_gate, m_lb_logits, m_hg_norm_g, m_q_a_g, m_kv_a_g, m_final_norm_g.reshape(1, -1)),
        (v_norm_g, v_b_gate, v_lb_logits, v_hg_norm_g, v_q_a_g, v_kv_a_g, v_final_norm_g.reshape(1, -1)))
    names = ["norm_g", "w_in", "b_gate", "lb_logits", "hg_norm_g", "q_a_g", "w_uq", "kv_a_g", "w_ukv",
             "w_proj_a", "w_proj_b", "w_out", "final_norm_g"]
    results = []
    for kind in range(4):
        by_name = {n: big[n][kind][None] for n in big}
        by_name.update(zip(small_names, small[kind]))
        by_name["final_norm_g"] = by_name["final_norm_g"].reshape(-1)
        results += [by_name[n] for n in names]
    return (loss_row[0, 0], grad_x[None], *results)
```

```python
import jax
import jax.numpy as jnp
from jax import lax
from jax.experimental import pallas as pl
from jax.experimental.pallas import tpu as pltpu

D_MODEL = 1024
HEADS = 8
HEAD_DIM = 128
HG_CHUNK = 32
Q_LORA = 384
KV_LORA = 256
QK_ROPE = 64
QK_DIM = 192
ROPE_THETA = 10000.0
EPS = 1e-6
IN_COLS = 7872
ADAM_LR = 0.001
ADAM_B1 = 0.9
ADAM_B2 = 0.999
ADAM_EPS = 1e-08
ADAM_WD = 0.01
ADAM_STEP = 10

N_DEV = 8
SEG = 1024
PROJ_W = 8 * SEG
SMALL_SEG = 4
MZ_SEG = 5
GL_SEG = 6
HEAD_PAD = 256
PACK_COLS = 1024
LOSS_LANE = HEAD_DIM + Q_LORA + KV_LORA
ROWS_W_UQ = 72
ROWS_W_UKV = 64
ROWS_W_PROJ = 128
ROWS_OTHER_USED = 3 * ROWS_W_PROJ + ROWS_W_UQ + ROWS_W_UKV
ROWS_OTHER = 528
LATE_DEV = (SMALL_SEG * SEG) // (IN_COLS // N_DEV)
assert (SMALL_SEG * SEG + Q_LORA + KV_LORA + QK_ROPE - 1) // (IN_COLS // N_DEV) == LATE_DEV

QK_SCALE = QK_DIM ** -0.5
LOG2E = 1.4426950408889634
LN2 = 0.6931471805599453
Q_PRESCALE = QK_SCALE * LOG2E

MXU_DTYPE = jnp.bfloat16
TRANSPORT_DTYPE = jnp.bfloat16
VMEM_LIMIT = 48 * 1024 * 1024
VMEM_LIMIT_BIG = 60 * 1024 * 1024

T_HGRN = 512
HG_HEADS_PER_STEP = 4
TM_ROW = 256
TM_MLA = 512
T_ATT = 1024
T_ATT_BWD = T_ATT
ATT_SUB = 4
P_SLOTS = 4
LSE_LANE = 127
ATT_SUB_BWD = 2
TS_TN = 2048
TR_ADAM = 256

F32 = jnp.float32
MESH = pl.DeviceIdType.MESH


def _dot(a, b):
    return jnp.dot(a, b, preferred_element_type=F32)


def _dot_nt(a, b):
    return lax.dot_general(a, b, (((1,), (1,)), ((), ())), preferred_element_type=F32)


def _dot_tn(a, b):
    return lax.dot_general(a, b, (((0,), (0,)), ((), ())), preferred_element_type=F32)


def _mx(a):
    return a.astype(MXU_DTYPE)


def _sigmoid(x):
    return 1.0 / (1.0 + jnp.exp(-x))


def _params(vmem=VMEM_LIMIT, **kw):
    return pltpu.CompilerParams(vmem_limit_bytes=vmem, **kw)


def _bcast_rows(row, n):
    return jnp.broadcast_to(row, (n, row.shape[-1]))


def _resident(shape):
    return pl.BlockSpec(shape, lambda *_: (0, 0), pipeline_mode=pl.Buffered(1))


HBM_SPEC = pl.BlockSpec(memory_space=pltpu.HBM)


def _all_gather_packed(shard):
    rows, cols = shard.shape

    def body(x_ref, out_ref, send_sems, recv_sems, local_sem):
        x, y, c = lax.axis_index("x"), lax.axis_index("y"), lax.axis_index("c")
        me, sibling = (x, y, c), (x, y, 1 - c)
        chips = [(1 - x, y), (x, 1 - y), (1 - x, 1 - y)]

        def slot(px, py, pc):
            return out_ref.at[4 * px + 2 * py + pc]

        def copy(k, block, to, src=None):
            return pltpu.make_async_remote_copy(
                src_ref=slot(*block) if src is None else src,
                dst_ref=slot(*block),
                send_sem=send_sems.at[k],
                recv_sem=recv_sems.at[k],
                device_id=to,
                device_id_type=MESH,
            )

        mine = pltpu.make_async_copy(x_ref, slot(*me), local_sem)
        mine.start()
        first = [copy(0, me, sibling, src=x_ref)]
        first += [copy(1 + j, me, (*chip, c), src=x_ref) for j, chip in enumerate(chips)]
        for cp in first:
            cp.start()
        passed = [copy(4 + j, (*chip, c), sibling) for j, chip in enumerate(chips)]
        for j, chip in enumerate(chips):
            copy(1 + j, (*chip, c), me).wait_recv()
            passed[j].start()
        copy(0, sibling, me).wait_recv()
        for j, chip in enumerate(chips):
            copy(4 + j, (*chip, 1 - c), me).wait_recv()
        for cp in first + passed:
            cp.wait_send()
        mine.wait()

    return pl.pallas_call(
        body,
        name="ag_weights",
        out_shape=jax.ShapeDtypeStruct((N_DEV, rows, cols), shard.dtype),
        in_specs=[HBM_SPEC],
        out_specs=HBM_SPEC,
        scratch_shapes=[
            pltpu.SemaphoreType.DMA((7,)),
            pltpu.SemaphoreType.DMA((7,)),
            pltpu.SemaphoreType.DMA,
        ],
    )(shard)


class _Exchange:
    def __init__(self, g_refs, recv_refs, send_sems, recv_sems, local_sems, gather=False):
        x, y, c = lax.axis_index("x"), lax.axis_index("y"), lax.axis_index("c")
        me = 4 * x + 2 * y + c
        n_ops = len(g_refs)

        def source(i, dest):
            return g_refs[i] if gather else g_refs[i].at[dest]

        def copy(i, k, landing):
            px, py, pc = x ^ ((k >> 2) & 1), y ^ ((k >> 1) & 1), c ^ (k & 1)
            peer = 4 * px + 2 * py + pc
            return pltpu.make_async_remote_copy(
                src_ref=source(i, peer),
                dst_ref=recv_refs[i].at[peer if landing else me],
                send_sem=send_sems.at[i * (N_DEV - 1) + k - 1],
                recv_sem=recv_sems.at[i * (N_DEV - 1) + k - 1],
                device_id=(px, py, pc),
                device_id_type=MESH,
            )

        pairs = [(i, k) for i in range(n_ops) for k in range(1, N_DEV)]
        self.mine = lambda: [pltpu.make_async_copy(source(i, me), recv_refs[i].at[me], local_sems.at[i])
                             for i in range(n_ops)]
        self.sends = lambda: [copy(i, k, False) for i, k in pairs]
        self.landings = lambda: [copy(i, k, True) for i, k in pairs]

    def start(self):
        for cp in self.mine() + self.sends():
            cp.start()

    def wait(self):
        for cp in self.landings():
            cp.wait_recv()
        for cp in self.sends():
            cp.wait_send()
        for cp in self.mine():
            cp.wait()

    @staticmethod
    def semaphores(n_ops):
        return [pltpu.SemaphoreType.DMA((n_ops * (N_DEV - 1),)),
                pltpu.SemaphoreType.DMA((n_ops * (N_DEV - 1),)),
                pltpu.SemaphoreType.DMA((n_ops,))]


def _exchange_rows(slabs):
    def body(g_ref, recv_ref, send_sems, recv_sems, local_sems):
        exchange = _Exchange([g_ref], [recv_ref], send_sems, recv_sems, local_sems)
        exchange.start()
        exchange.wait()

    return pl.pallas_call(
        body,
        name="exchange_rows",
        out_shape=jax.ShapeDtypeStruct(slabs.shape, slabs.dtype),
        in_specs=[HBM_SPEC],
        out_specs=HBM_SPEC,
        scratch_shapes=_Exchange.semaphores(1),
    )(slabs)


def _sum_adamw(recvs, w, m, v, name):
    rows, cols = w.shape
    tr = TR_ADAM if rows % TR_ADAM == 0 else rows
    n_recv = len(recvs)

    def body(*refs):
        w_ref, m_ref, v_ref, g_out, d_out, m_out, v_out = refs[n_recv:]
        g = None
        for r_ref in refs[:n_recv]:
            for i in range(N_DEV):
                part = r_ref[i].astype(F32)
                g = part if g is None else g + part
        g = g[:, :cols]
        m_new = ADAM_B1 * m_ref[...] + (1.0 - ADAM_B1) * g
        v_new = ADAM_B2 * v_ref[...] + (1.0 - ADAM_B2) * (g * g)
        m_hat = m_new / (1.0 - ADAM_B1 ** ADAM_STEP)
        v_hat = v_new / (1.0 - ADAM_B2 ** ADAM_STEP)
        g_out[...] = g
        d_out[...] = -ADAM_LR * (m_hat / (jnp.sqrt(v_hat) + ADAM_EPS) + ADAM_WD * w_ref[...])
        m_out[...] = m_new
        v_out[...] = v_new

    row_spec = pl.BlockSpec((tr, cols), lambda i: (i, 0))
    shape = jax.ShapeDtypeStruct((rows, cols), F32)
    return pl.pallas_call(
        body,
        name=name,
        grid=(rows // tr,),
        in_specs=[pl.BlockSpec((N_DEV, tr, recvs[0].shape[2]), lambda i: (0, i, 0))] * n_recv + [row_spec] * 3,
        out_specs=[row_spec] * 4,
        out_shape=[shape] * 4,
        compiler_params=_params(),
    )(*recvs, w, m, v)


def _inproj(x, norm_g, w_in_p, other_shard):
    s = x.shape[0]
    tm = min(s, TM_ROW)
    nsteps = s // tm

    def body(x_ref, g_ref, w_ref, shard_ref, proj_ref, h_ref, gathered_ref, send_sems, recv_sems, local_sems):
        i = pl.program_id(0)

        def all_gather():
            return _Exchange([shard_ref], [gathered_ref], send_sems, recv_sems, local_sems, gather=True)

        @pl.when(i == 0)
        def _():
            all_gather().start()

        xf = x_ref[...]
        r = lax.rsqrt(jnp.mean(xf * xf, axis=-1, keepdims=True) + EPS)
        h = _mx(xf * r * g_ref[...])
        h_ref[...] = h
        for j in range(PROJ_W // SEG):
            cols = slice(j * SEG, (j + 1) * SEG)
            proj_ref[:, cols] = _dot(h, w_ref[:, cols])

        @pl.when(i == nsteps - 1)
        def _():
            all_gather().wait()

    return pl.pallas_call(
        body,
        name="inproj",
        grid=(nsteps,),
        in_specs=[
            pl.BlockSpec((tm, D_MODEL), lambda i: (i, 0)),
            pl.BlockSpec((1, D_MODEL), lambda i: (0, 0)),
            _resident((D_MODEL, PROJ_W)),
            HBM_SPEC,
        ],
        out_specs=[
            pl.BlockSpec((tm, PROJ_W), lambda i: (i, 0)),
            pl.BlockSpec((tm, D_MODEL), lambda i: (i, 0)),
            HBM_SPEC,
        ],
        out_shape=[
            jax.ShapeDtypeStruct((s, PROJ_W), F32),
            jax.ShapeDtypeStruct((s, D_MODEL), MXU_DTYPE),
            jax.ShapeDtypeStruct((N_DEV,) + other_shard.shape, other_shard.dtype),
        ],
        scratch_shapes=_Exchange.semaphores(1),
        compiler_params=_params(),
    )(x, norm_g, w_in_p, other_shard)


def _chunk_lower_mask(t):
    row = lax.broadcasted_iota(jnp.int32, (t, t), 0)
    col = lax.broadcasted_iota(jnp.int32, (t, t), 1)
    return ((row // HG_CHUNK) == (col // HG_CHUNK)) & (col <= row)


def _chunk_pos(t):
    return lax.broadcasted_iota(jnp.int32, (t, HEAD_DIM), 0) & (HG_CHUNK - 1)


def _cumsum_chunk(x, pos):
    sh = 1
    while sh < HG_CHUNK:
        x = x + jnp.where(pos >= sh, pltpu.roll(x, sh, 0), 0.0)
        sh *= 2
    return x


def _rcumsum_chunk(x, pos):
    t = x.shape[0]
    sh = 1
    while sh < HG_CHUNK:
        x = x + jnp.where(pos < HG_CHUNK - sh, pltpu.roll(x, t - sh, 0), 0.0)
        sh *= 2
    return x


def _chunk_total(x):
    t, w = x.shape
    tot = jnp.sum(x.reshape(t // HG_CHUNK, HG_CHUNK, w), axis=1, keepdims=True)
    return jnp.broadcast_to(tot, (t // HG_CHUNK, HG_CHUNK, w)).reshape(t, w)


def _hgrn_gates(hq, hf, lb_logits, pos):
    lb = _sigmoid(lb_logits[0:1, :] - lb_logits[1:2, :])
    sig = _sigmoid(hf)
    f = lb + (1.0 - lb) * sig
    sq = _sigmoid(hq)
    q = hq * sq
    k = 1.0 - f
    logf = jnp.log(f)
    bcum = _cumsum_chunk(logf, pos)
    blast = _chunk_total(logf)
    eb = jnp.exp(bcum)
    enb = jnp.exp(-bcum)
    eo = jnp.exp(blast - bcum)
    return dict(lb=lb, sig=sig, f=f, sq=sq, q=q, k=k, eb=eb, enb=enb, eo=eo,
                qi=q * eb, ki=k * enb, ko=k * eo, dec=jnp.exp(blast))


def _hgrn_fwd(proj, lb_logits, hg_norm_g):
    s = proj.shape[0]
    t = min(s, T_HGRN)
    nb = s // t
    nc = t // HG_CHUNK
    hw = HG_HEADS_PER_STEP * HEAD_DIM

    def body(hq_ref, hf_ref, hi_ref, hz_ref, lb_ref, g_ref, o_ref, ya_ref, st_ref, state, u_sc, stb_sc):
        b = pl.program_id(1)

        @pl.when(b == 0)
        def _():
            state[...] = jnp.zeros_like(state)

        lower = _chunk_lower_mask(t)
        pos = _chunk_pos(t)
        for hh in range(HG_HEADS_PER_STEP):
            cols = slice(hh * HEAD_DIM, (hh + 1) * HEAD_DIM)
            st = state[hh]
            st_ref[0, hh] = st
            gt = _hgrn_gates(hq_ref[:, cols], hf_ref[:, cols], lb_ref[:, cols], pos)
            vb = _mx(hi_ref[:, cols])
            qib, kib, kob = _mx(gt["qi"]), _mx(gt["ki"]), _mx(gt["ko"])
            a = jnp.where(lower, _dot_nt(qib, kib), 0.0)
            o_intra = _dot(_mx(a), vb)
            for c in range(nc):
                sl = slice(c * HG_CHUNK, (c + 1) * HG_CHUNK)
                u_sc[hh, c] = _dot_tn(vb[sl], kob[sl])
            for c in range(nc):
                stb_sc[hh, c] = _mx(st)
                st = st * gt["dec"][c * HG_CHUNK:c * HG_CHUNK + 1, :] + u_sc[hh, c]
            state[hh] = st
            outs = []
            for c in range(nc):
                sl = slice(c * HG_CHUNK, (c + 1) * HG_CHUNK)
                outs.append(o_intra[sl] + _dot_nt(qib[sl], stb_sc[hh, c]))
            o = jnp.concatenate(outs, axis=0)
            o_ref[:, cols] = o
            r = lax.rsqrt(jnp.mean(o * o, axis=-1, keepdims=True) + EPS)
            hz = hz_ref[:, cols]
            ya_ref[:, cols] = _mx((o * r * g_ref[...]) * (hz * _sigmoid(hz)))

    hsteps = HEADS // HG_HEADS_PER_STEP

    def seg(k):
        return pl.BlockSpec((t, hw), lambda h, b, k=k: (b, k * hsteps + h))

    return pl.pallas_call(
        body,
        name="hgrn_fwd",
        grid=(hsteps, nb),
        in_specs=[seg(0), seg(1), seg(2), seg(3),
                  pl.BlockSpec((2, hw), lambda h, b: (0, h)),
                  pl.BlockSpec((1, HEAD_DIM), lambda h, b: (0, 0))],
        out_specs=[
            pl.BlockSpec((t, hw), lambda h, b: (b, h)),
            pl.BlockSpec((t, hw), lambda h, b: (b, h)),
            pl.BlockSpec((1, HG_HEADS_PER_STEP, HEAD_DIM, HEAD_DIM), lambda h, b: (b, h, 0, 0)),
        ],
        out_shape=[
            jax.ShapeDtypeStruct((s, D_MODEL), F32),
            jax.ShapeDtypeStruct((s, D_MODEL), MXU_DTYPE),
            jax.ShapeDtypeStruct((nb, HEADS, HEAD_DIM, HEAD_DIM), F32),
        ],
        scratch_shapes=[pltpu.VMEM((HG_HEADS_PER_STEP, HEAD_DIM, HEAD_DIM), F32),
                        pltpu.VMEM((HG_HEADS_PER_STEP, nc, HEAD_DIM, HEAD_DIM), F32),
                        pltpu.VMEM((HG_HEADS_PER_STEP, nc, HEAD_DIM, HEAD_DIM), MXU_DTYPE)],
        compiler_params=_params(),
    )(proj, proj, proj, proj, lb_logits, hg_norm_g)


def _rope(x, cos, sin_a, sin_b):
    return x * cos + pltpu.roll(x, 96, 1) * sin_a + pltpu.roll(x, 32, 1) * sin_b


def _rope_t(d, cos, sin_a, sin_b):
    return d * cos + pltpu.roll(d * sin_a, 32, 1) + pltpu.roll(d * sin_b, 96, 1)


def _mla_prep(proj, q_a_g, kv_a_g, w_uq_p, w_kn, w_v, cos, sin_a, sin_b):
    s = proj.shape[0]
    tm = min(s, TM_MLA)

    def body(sm_ref, gq_ref, gk_ref, wq_ref, wkn_ref, wv_ref, cos_ref, sa_ref, sb_ref,
             q_ref, k_ref, v_ref, cqn_ref, ckvn_ref):
        small = sm_ref[...]
        cq = small[:, :Q_LORA]
        ckv = small[:, Q_LORA:Q_LORA + KV_LORA]
        krp = small[:, Q_LORA + KV_LORA:Q_LORA + KV_LORA + HEAD_DIM]
        rq = lax.rsqrt(jnp.mean(cq * cq, axis=-1, keepdims=True) + EPS)
        rk = lax.rsqrt(jnp.mean(ckv * ckv, axis=-1, keepdims=True) + EPS)
        cqn = _mx(cq * rq * gq_ref[...])
        ckvn = _mx(ckv * rk * gk_ref[...])
        cqn_ref[...] = cqn
        ckvn_ref[...] = ckvn
        q = _dot(cqn, wq_ref[...]) * Q_PRESCALE
        kn = _dot(ckvn, wkn_ref[...])
        v = _dot(ckvn, wv_ref[...])
        cos_t, sa, sb = cos_ref[...], sa_ref[...], sb_ref[...]
        kpe = _mx(_rope(krp, cos_t, sa, sb))
        ones_col = (lax.broadcasted_iota(jnp.int32, (tm, HEAD_DIM), 1) == 0).astype(MXU_DTYPE)
        for h in range(HEADS):
            lo = h * HEAD_PAD
            v_ref[:, lo:lo + HEAD_DIM] = _mx(v[:, h * HEAD_DIM:(h + 1) * HEAD_DIM])
            v_ref[:, lo + HEAD_DIM:lo + HEAD_PAD] = ones_col
            q_ref[:, lo:lo + HEAD_DIM] = _mx(q[:, lo:lo + HEAD_DIM])
            q_ref[:, lo + HEAD_DIM:lo + HEAD_PAD] = _mx(_rope(q[:, lo + HEAD_DIM:lo + HEAD_PAD], cos_t, sa, sb))
            k_ref[:, lo:lo + HEAD_DIM] = _mx(kn[:, h * HEAD_DIM:(h + 1) * HEAD_DIM])
            k_ref[:, lo + HEAD_DIM:lo + HEAD_PAD] = kpe

    def const(shape):
        return pl.BlockSpec(shape, lambda i: (0, 0))

    def rows(w):
        return pl.BlockSpec((tm, w), lambda i: (i, 0))

    return pl.pallas_call(
        body,
        name="mla_prep",
        grid=(s // tm,),
        in_specs=[
            pl.BlockSpec((tm, SEG), lambda i: (i, SMALL_SEG)),
            const((1, Q_LORA)), const((1, KV_LORA)),
            const((Q_LORA, HEADS * HEAD_PAD)), const((KV_LORA, D_MODEL)), const((KV_LORA, D_MODEL)),
            rows(HEAD_DIM), rows(HEAD_DIM), rows(HEAD_DIM),
        ],
        out_specs=[rows(HEADS * HEAD_PAD)] * 3 + [rows(Q_LORA), rows(KV_LORA)],
        out_shape=[
            jax.ShapeDtypeStruct((s, HEADS * HEAD_PAD), MXU_DTYPE),
            jax.ShapeDtypeStruct((s, HEADS * HEAD_PAD), MXU_DTYPE),
            jax.ShapeDtypeStruct((s, HEADS * HEAD_PAD), MXU_DTYPE),
            jax.ShapeDtypeStruct((s, Q_LORA), MXU_DTYPE),
            jax.ShapeDtypeStruct((s, KV_LORA), MXU_DTYPE),
        ],
        compiler_params=_params(),
    )(proj, q_a_g, kv_a_g, w_uq_p, w_kn, w_v, cos, sin_a, sin_b)


def _flash_fwd(q_all, k_all, v_all, proj):
    s = q_all.shape[0]
    t = min(s, T_ATT)
    n = s // t
    ts = t // ATT_SUB
    n_pairs = n * (n + 1) // 2

    def body(q_ref, k_ref, v_ref, mz_ref, ao_ref, yb_ref, mblk_ref, p_hbm, m_sc, acc_sc, stage, p_sems):
        head, qi = pl.program_id(0), pl.program_id(1)
        m_sc[...] = jnp.full_like(m_sc, -jnp.inf)
        acc_sc[...] = jnp.zeros_like(acc_sc)
        mblk_ref[...] = jnp.zeros_like(mblk_ref)
        lane = lax.broadcasted_iota(jnp.int32, (ts, HEAD_DIM), 1)
        first_block = head * n_pairs + qi * (qi + 1) // 2

        def p_copy(slot, pair, r):
            rows = pl.ds(r * ts, ts)
            return pltpu.make_async_copy(stage.at[slot, rows], p_hbm.at[head, pair, rows], p_sems.at[slot, r])

        def p_wait(slot):
            for r in range(ATT_SUB):
                p_copy(slot, 0, r).wait()

        def key_block(ki, diagonal):
            base = pl.multiple_of(ki * t, t)
            count = first_block + ki
            slot = lax.rem(count, P_SLOTS)
            sc, pb, alpha = {}, {}, {}

            @pl.when(count >= P_SLOTS)
            def _():
                p_wait(slot)

            if diagonal:
                stage[slot] = jnp.zeros((t, t), MXU_DTYPE)

            def width(r):
                return (r + 1) * ts if diagonal else t

            def scores(r):
                w = width(r)
                s_r = _dot_nt(q_ref[r * ts:(r + 1) * ts], k_ref[pl.ds(base, w), :])
                if diagonal:
                    row = lax.broadcasted_iota(jnp.int32, (ts, w), 0) + r * ts
                    col = lax.broadcasted_iota(jnp.int32, (ts, w), 1)
                    s_r = jnp.where(row >= col, s_r, -jnp.inf)
                sc[r] = s_r

            def softmax(r):
                rs = slice(r * ts, (r + 1) * ts)
                m_prev = m_sc[rs]
                m_new = jnp.maximum(m_prev, jnp.max(sc[r], axis=-1, keepdims=True))
                pb[r] = _mx(jnp.exp2(sc[r] - m_new))
                alpha[r] = jnp.exp2(m_prev - m_new)
                m_sc[rs] = m_new
                mblk_ref[rs] = jnp.where(lane == ki, m_new, mblk_ref[rs])
                stage[slot, rs, :width(r)] = pb[r]
                p_copy(slot, qi * (qi + 1) // 2 + ki, r).start()

            def weighted_values(r):
                rs = slice(r * ts, (r + 1) * ts)
                acc_sc[rs] = alpha[r] * acc_sc[rs] + _dot(pb[r], v_ref[pl.ds(base, width(r)), :])

            for step in range(ATT_SUB + 2):
                if step < ATT_SUB:
                    scores(step)
                if 1 <= step <= ATT_SUB:
                    softmax(step - 1)
                if step >= 2:
                    weighted_values(step - 2)

        def below_diagonal(ki, carry):
            key_block(ki, False)
            return carry

        lax.fori_loop(0, qi, below_diagonal, 0)
        key_block(qi, True)

        @pl.when((head == HEADS - 1) & (qi == n - 1))
        def _():
            for slot in range(min(P_SLOTS, HEADS * n_pairs)):
                p_wait(slot)

        acc = acc_sc[...]
        l = acc[:, HEAD_DIM:HEAD_DIM + 1]
        ao = acc[:, :HEAD_DIM] / l
        ao_ref[...] = ao
        lane_t = lax.broadcasted_iota(jnp.int32, (t, HEAD_DIM), 1)
        mblk_ref[...] = jnp.where(lane_t == LSE_LANE, m_sc[...] + jnp.log2(l), mblk_ref[...])
        mz = mz_ref[...]
        yb_ref[...] = _mx(ao * (mz * _sigmoid(mz)))

    q_map = lambda h, qi: (qi, h)
    return pl.pallas_call(
        body,
        name="flash_fwd",
        grid=(HEADS, n),
        in_specs=[
            pl.BlockSpec((t, HEAD_PAD), q_map),
            pl.BlockSpec((s, HEAD_PAD), lambda h, qi: (0, h)),
            pl.BlockSpec((s, HEAD_PAD), lambda h, qi: (0, h)),
            pl.BlockSpec((t, HEAD_DIM), lambda h, qi: (qi, MZ_SEG * HEADS + h)),
        ],
        out_specs=[pl.BlockSpec((t, HEAD_DIM), q_map)] * 3 + [HBM_SPEC],
        out_shape=[
            jax.ShapeDtypeStruct((s, D_MODEL), F32),
            jax.ShapeDtypeStruct((s, D_MODEL), MXU_DTYPE),
            jax.ShapeDtypeStruct((s, D_MODEL), F32),
            jax.ShapeDtypeStruct((HEADS, n_pairs, t, t), MXU_DTYPE),
        ],
        scratch_shapes=[
            pltpu.VMEM((t, 1), F32),
            pltpu.VMEM((t, HEAD_PAD), F32),
            pltpu.VMEM((P_SLOTS, t, t), MXU_DTYPE),
            pltpu.SemaphoreType.DMA((P_SLOTS, ATT_SUB)),
        ],
        compiler_params=_params(),
    )(q_all, k_all, v_all, proj)


def _merge_fwd_bwd(x, target, ya, yb, ao, proj, b_gate, final_g, w_pa, w_pb, w_out):
    s = x.shape[0]
    tm = min(s, TM_ROW)

    def body(x_ref, t_ref, ya_ref, yb_ref, ao_ref, mz_ref, g0_ref, g1_ref, bg_ref, fg_ref, wpa_ref, wpb_ref, wo_ref,
             dx2_ref, dya_ref, dao_ref, dtail_ref, mb_ref, dpab_ref, dpbb_ref, dx2b_ref,
             loss_ref, dfg_ref, dbg_ref):
        i = pl.program_id(0)

        @pl.when(i == 0)
        def _():
            loss_ref[...] = jnp.zeros_like(loss_ref)
            dfg_ref[...] = jnp.zeros_like(dfg_ref)
            dbg_ref[...] = jnp.zeros_like(dbg_ref)

        pa = _dot(ya_ref[...], wpa_ref[...])
        pb = _dot(yb_ref[...], wpb_ref[...])
        bg = bg_ref[...]
        g0 = _sigmoid(g0_ref[...] + bg[:, :D_MODEL])
        g1 = _sigmoid(g1_ref[...] + bg[:, D_MODEL:])
        merged = g0 * pa + g1 * pb
        mb = _mx(merged)
        mb_ref[...] = mb
        x2 = x_ref[...] + _dot(mb, wo_ref[...])
        r = lax.rsqrt(jnp.mean(x2 * x2, axis=-1, keepdims=True) + EPS)
        xn = x2 * r
        fg = fg_ref[...]
        diff = xn * fg - t_ref[...]
        loss_ref[...] += 0.5 * jnp.sum(jnp.mean(diff * diff, axis=-1, keepdims=True))
        dy = diff * (1.0 / D_MODEL)
        dfg_ref[...] += _bcast_rows(jnp.sum(dy * xn, axis=0, keepdims=True), 8)
        tt = dy * fg
        dx2 = r * (tt - xn * jnp.mean(tt * xn, axis=-1, keepdims=True))
        dx2_ref[...] = dx2
        dx2b = _mx(dx2)
        dx2b_ref[...] = dx2b
        dmerged = _dot_nt(dx2b, wo_ref[...])
        dpa = _mx(dmerged * g0)
        dpb = _mx(dmerged * g1)
        dpab_ref[...] = dpa
        dpbb_ref[...] = dpb
        dg0 = dmerged * pa * (g0 * (1.0 - g0))
        dg1 = dmerged * pb * (g1 * (1.0 - g1))
        dtail_ref[1] = _mx(dg0)
        dtail_ref[2] = _mx(dg1)
        dbg_ref[:, :D_MODEL] += _bcast_rows(jnp.sum(dg0, axis=0, keepdims=True), 8)
        dbg_ref[:, D_MODEL:] += _bcast_rows(jnp.sum(dg1, axis=0, keepdims=True), 8)
        dya_ref[...] = _dot_nt(dpa, wpa_ref[...])
        dyb = _dot_nt(dpb, wpb_ref[...])
        mz = mz_ref[...]
        sg = _sigmoid(mz)
        dao_ref[...] = _mx(dyb * (mz * sg))
        dtail_ref[0] = _mx(dyb * ao_ref[...] * (sg + mz * sg * (1.0 - sg)))

    def rows(w=D_MODEL):
        return pl.BlockSpec((tm, w), lambda i: (i, 0))

    def const(shape):
        return pl.BlockSpec(shape, lambda i: (0, 0))

    def seg(k):
        return pl.BlockSpec((tm, SEG), lambda i: (i, k))

    f32 = jax.ShapeDtypeStruct((s, D_MODEL), F32)
    b16 = jax.ShapeDtypeStruct((s, D_MODEL), MXU_DTYPE)
    return pl.pallas_call(
        body,
        name="merge_fwd_bwd",
        grid=(s // tm,),
        in_specs=[
            rows(), rows(), rows(), rows(), rows(),
            seg(MZ_SEG), seg(GL_SEG), seg(GL_SEG + 1),
            const((1, 2 * D_MODEL)), const((1, D_MODEL)),
            _resident((D_MODEL, D_MODEL)), _resident((D_MODEL, D_MODEL)), _resident((D_MODEL, D_MODEL)),
        ],
        out_specs=[rows()] * 3 + [pl.BlockSpec((3, tm, D_MODEL), lambda i: (0, i, 0))] + [rows()] * 4
        + [const((8, HEAD_DIM)), const((8, D_MODEL)), const((8, 2 * D_MODEL))],
        out_shape=[f32, f32, b16, jax.ShapeDtypeStruct((3, s, D_MODEL), MXU_DTYPE), b16, b16, b16, b16,
                   jax.ShapeDtypeStruct((8, HEAD_DIM), F32),
                   jax.ShapeDtypeStruct((8, D_MODEL), F32),
                   jax.ShapeDtypeStruct((8, 2 * D_MODEL), F32)],
        compiler_params=_params(),
    )(x, target, ya, yb, ao, proj, proj, proj, b_gate, final_g, w_pa, w_pb, w_out)


def _flash_bwd(q_all, k_all, v_all, dao, ao, mblk, p_all, slab_sets):
    s = q_all.shape[0]
    t = min(s, T_ATT_BWD)
    n = s // t
    pairs = [(ki, qi) for ki in range(n) for qi in range(ki, n)]
    ki_list = jnp.asarray([p[0] for p in pairs], jnp.int32)
    qi_list = jnp.asarray([p[1] for p in pairs], jnp.int32)
    p_list = jnp.asarray([qi * (qi + 1) // 2 + ki for ki, qi in pairs], jnp.int32)
    n_ops = len(slab_sets)

    def body(ki_ref, qi_ref, pidx_ref, q_ref, k_ref, v_ref, do_ref, ao_ref, mblk_ref, p_ref, *rest):
        g_refs = rest[:n_ops]
        dq_ref, dk_ref, dv_ref = rest[n_ops:n_ops + 3]
        recv_refs = rest[n_ops + 3:2 * n_ops + 3]
        dk_acc, dv_acc, send_sems, recv_sems, local_sems = rest[2 * n_ops + 3:]
        head, step = pl.program_id(0), pl.program_id(1)
        ki, qi = ki_ref[step], qi_ref[step]

        @pl.when((head == 0) & (step == 0))
        def _():
            _Exchange(g_refs, recv_refs, send_sems, recv_sems, local_sems).start()

        @pl.when(qi == ki)
        def _():
            dk_acc[...] = jnp.zeros_like(dk_acc)
            dv_acc[...] = jnp.zeros_like(dv_acc)

        @pl.when(ki == 0)
        def _():
            dq_ref[pl.ds(pl.multiple_of(qi * t, t), t), :] = jnp.zeros((t, HEAD_PAD), F32)

        def pair(masked):
            nsub = ATT_SUB if masked else ATT_SUB_BWD
            ts = t // nsub
            dk_parts, dv_parts = [], []
            for r in range(nsub):
                rs = slice(r * ts, (r + 1) * ts)
                w = (r + 1) * ts if masked else t
                k = k_ref[:w]
                v = v_ref[:w]
                q = q_ref[rs]
                lane = lax.broadcasted_iota(jnp.int32, (ts, HEAD_DIM), 1)
                stats = mblk_ref[rs]
                m_blk = jnp.max(jnp.where(lane == ki, stats, -jnp.inf), axis=-1, keepdims=True)
                lse = jnp.max(jnp.where(lane == LSE_LANE, stats, -jnp.inf), axis=-1, keepdims=True)
                factor = jnp.exp2(m_blk - lse)
                p_st = p_ref[0, 0, rs, :w]
                do = do_ref[rs]
                do_f = do.astype(F32)
                delta = jnp.sum(do_f * ao_ref[rs], axis=-1, keepdims=True)
                dv_part = _dot_tn(p_st, _mx(do_f * factor))
                ds = p_st * _mx((_dot_nt(do, v) - delta) * factor)
                dk_part = _dot_tn(ds, q)
                rows = pl.ds(pl.multiple_of(qi * t + r * ts, ts), ts)
                dq_ref[rows, :] += _dot(ds, k)
                if masked:
                    dk_acc[:w] += dk_part
                    dv_acc[:w] += dv_part
                else:
                    dk_parts.append(dk_part)
                    dv_parts.append(dv_part)

            if not masked:
                dk_acc[...] += sum(dk_parts[1:], dk_parts[0])
                dv_acc[...] += sum(dv_parts[1:], dv_parts[0])

        @pl.when(qi == ki)
        def _():
            pair(True)

        @pl.when(qi > ki)
        def _():
            pair(False)

        @pl.when(qi == n - 1)
        def _():
            dk_ref[...] = _mx(dk_acc[...] * LN2)
            dv_ref[...] = _mx(dv_acc[...])

        @pl.when((head == HEADS - 1) & (step == len(pairs) - 1))
        def _():
            _Exchange(g_refs, recv_refs, send_sems, recv_sems, local_sems).wait()

    q_map = lambda h, p, ki_ref, qi_ref, pidx_ref: (qi_ref[p], h)
    kv_map = lambda h, p, ki_ref, qi_ref, pidx_ref: (ki_ref[p], h)
    grid_spec = pltpu.PrefetchScalarGridSpec(
        num_scalar_prefetch=3,
        grid=(HEADS, len(pairs)),
        in_specs=[
            pl.BlockSpec((t, HEAD_PAD), q_map),
            pl.BlockSpec((t, HEAD_PAD), kv_map),
            pl.BlockSpec((t, HEAD_DIM), lambda h, p, ki_ref, qi_ref, pidx_ref: (ki_ref[p], 2 * h)),
            pl.BlockSpec((t, HEAD_DIM), q_map),
            pl.BlockSpec((t, HEAD_DIM), q_map),
            pl.BlockSpec((t, HEAD_DIM), q_map),
            pl.BlockSpec((1, 1, t, t), lambda h, p, ki_ref, qi_ref, pidx_ref: (h, pidx_ref[p], 0, 0)),
        ] + [HBM_SPEC] * n_ops,
        out_specs=[
            pl.BlockSpec((s, HEAD_PAD), lambda h, p, ki_ref, qi_ref, pidx_ref: (0, h)),
            pl.BlockSpec((t, HEAD_PAD), kv_map),
            pl.BlockSpec((t, HEAD_DIM), kv_map),
        ] + [HBM_SPEC] * n_ops,
        scratch_shapes=[pltpu.VMEM((t, HEAD_PAD), F32), pltpu.VMEM((t, HEAD_DIM), F32)]
        + _Exchange.semaphores(n_ops),
    )
    outs = pl.pallas_call(
        body,
        name="flash_bwd",
        grid_spec=grid_spec,
        out_shape=[
            jax.ShapeDtypeStruct((s, HEADS * HEAD_PAD), F32),
            jax.ShapeDtypeStruct((s, HEADS * HEAD_PAD), MXU_DTYPE),
            jax.ShapeDtypeStruct((s, D_MODEL), MXU_DTYPE),
        ] + [jax.ShapeDtypeStruct(a.shape, a.dtype) for a in slab_sets],
        compiler_params=_params(VMEM_LIMIT_BIG),
    )(ki_list, qi_list, p_list, q_all, k_all, v_all, dao, ao, mblk, p_all, *slab_sets)
    return outs[0], outs[1], outs[2], outs[3:]


def _mla_prep_bwd(dq_all, dk_all, dv_all, proj, q_a_g, kv_a_g, w_uq_p, w_kn, w_v, cos, sin_a, sin_b):
    s = proj.shape[0]
    tm = min(s, TM_MLA)

    def body(dq_ref, dk_ref, dv_ref, sm_ref, gq_ref, gk_ref, wq_ref, wkn_ref, wv_ref, cos_ref, sa_ref, sb_ref,
             dsm_ref, dqf_ref, dkn_ref, dvb_ref, dgq_ref, dgk_ref):
        i = pl.program_id(0)

        @pl.when(i == 0)
        def _():
            dgq_ref[...] = jnp.zeros_like(dgq_ref)
            dgk_ref[...] = jnp.zeros_like(dgk_ref)

        cos_t, sa, sb = cos_ref[...], sa_ref[...], sb_ref[...]
        dkpe = jnp.zeros((tm, HEAD_DIM), F32)
        for h in range(HEADS):
            lo = h * HEAD_PAD
            dqf_ref[:, lo:lo + HEAD_DIM] = _mx(dq_ref[:, lo:lo + HEAD_DIM] * QK_SCALE)
            dqf_ref[:, lo + HEAD_DIM:lo + HEAD_PAD] = _mx(
                _rope_t(dq_ref[:, lo + HEAD_DIM:lo + HEAD_PAD] * QK_SCALE, cos_t, sa, sb))
            dkn_ref[:, h * HEAD_DIM:(h + 1) * HEAD_DIM] = dk_ref[:, lo:lo + HEAD_DIM]
            dkpe = dkpe + dk_ref[:, lo + HEAD_DIM:lo + HEAD_PAD].astype(F32)
        dkr = _rope_t(dkpe, cos_t, sa, sb)
        dvb = dv_ref[...]
        dvb_ref[...] = dvb
        dcqn = _dot_nt(dqf_ref[...], wq_ref[...])
        dckvn = _dot_nt(dkn_ref[...], wkn_ref[...]) + _dot_nt(dvb, wv_ref[...])

        small = sm_ref[...]
        cq = small[:, :Q_LORA]
        ckv = small[:, Q_LORA:Q_LORA + KV_LORA]
        rq = lax.rsqrt(jnp.mean(cq * cq, axis=-1, keepdims=True) + EPS)
        rk = lax.rsqrt(jnp.mean(ckv * ckv, axis=-1, keepdims=True) + EPS)
        cqh = cq * rq
        ckh = ckv * rk
        dgq_ref[...] += _bcast_rows(jnp.sum(dcqn * cqh, axis=0, keepdims=True), 8)
        dgk_ref[...] += _bcast_rows(jnp.sum(dckvn * ckh, axis=0, keepdims=True), 8)
        tq = dcqn * gq_ref[...]
        tk = dckvn * gk_ref[...]
        dcq = rq * (tq - cqh * jnp.mean(tq * cqh, axis=-1, keepdims=True))
        dckv = rk * (tk - ckh * jnp.mean(tk * ckh, axis=-1, keepdims=True))
        dsm_ref[:, :Q_LORA] = _mx(dcq)
        dsm_ref[:, Q_LORA:Q_LORA + KV_LORA] = _mx(dckv)
        dsm_ref[:, Q_LORA + KV_LORA:Q_LORA + KV_LORA + HEAD_DIM] = _mx(dkr)
        dsm_ref[:, Q_LORA + KV_LORA + HEAD_DIM:] = jnp.zeros((tm, SEG - Q_LORA - KV_LORA - HEAD_DIM), MXU_DTYPE)

    def const(shape):
        return pl.BlockSpec(shape, lambda i: (0, 0))

    def rows(w):
        return pl.BlockSpec((tm, w), lambda i: (i, 0))

    return pl.pallas_call(
        body,
        name="mla_prep_bwd",
        grid=(s // tm,),
        in_specs=[
            rows(HEADS * HEAD_PAD), rows(HEADS * HEAD_PAD), rows(D_MODEL),
            pl.BlockSpec((tm, SEG), lambda i: (i, SMALL_SEG)),
            const((1, Q_LORA)), const((1, KV_LORA)),
            const((Q_LORA, HEADS * HEAD_PAD)), const((KV_LORA, D_MODEL)), const((KV_LORA, D_MODEL)),
            rows(HEAD_DIM), rows(HEAD_DIM), rows(HEAD_DIM),
        ],
        out_specs=[rows(SEG), rows(HEADS * HEAD_PAD), rows(D_MODEL), rows(D_MODEL),
                   const((8, Q_LORA)), const((8, KV_LORA))],
        out_shape=[
            jax.ShapeDtypeStruct((s, SEG), MXU_DTYPE),
            jax.ShapeDtypeStruct((s, HEADS * HEAD_PAD), MXU_DTYPE),
            jax.ShapeDtypeStruct((s, D_MODEL), MXU_DTYPE),
            jax.ShapeDtypeStruct((s, D_MODEL), MXU_DTYPE),
            jax.ShapeDtypeStruct((8, Q_LORA), F32),
            jax.ShapeDtypeStruct((8, KV_LORA), F32),
        ],
        compiler_params=_params(),
    )(dq_all, dk_all, dv_all, proj, q_a_g, kv_a_g, w_uq_p, w_kn, w_v, cos, sin_a, sin_b)


def _hgrn_bwd(proj, lb_logits, hg_norm_g, o_all, dya, states):
    s = proj.shape[0]
    t = min(s, T_HGRN)
    nb = s // t
    nc = t // HG_CHUNK

    def body(hq_ref, hf_ref, hi_ref, hz_ref, lb_ref, g_ref, o_ref, dya_ref, st_ref,
             dh4_ref, dlb_ref, dg_ref, dstate, u_sc, g_sc, stf_sc, stb_sc, dstb_sc):
        h, b = pl.program_id(0), pl.program_id(1)

        @pl.when(b == 0)
        def _():
            dstate[...] = jnp.zeros_like(dstate)
            dlb_ref[...] = jnp.zeros_like(dlb_ref)

        @pl.when((b == 0) & (h == 0))
        def _():
            dg_ref[...] = jnp.zeros_like(dg_ref)

        lower = _chunk_lower_mask(t)
        pos = _chunk_pos(t)
        ghg = g_ref[...]
        for hh in range(HG_HEADS_PER_STEP):
            cols = slice(hh * HEAD_DIM, (hh + 1) * HEAD_DIM)
            hq, hf, hz = hq_ref[:, cols], hf_ref[:, cols], hz_ref[:, cols]
            gt = _hgrn_gates(hq, hf, lb_ref[:, cols], pos)
            vb = _mx(hi_ref[:, cols])
            qi, ki, ko = gt["qi"], gt["ki"], gt["ko"]
            qib, kib, kob = _mx(qi), _mx(ki), _mx(ko)

            o = o_ref[:, cols]
            sz = _sigmoid(hz)
            r = lax.rsqrt(jnp.mean(o * o, axis=-1, keepdims=True) + EPS)
            on = o * r
            dya_t = dya_ref[:, cols]
            don = dya_t * (hz * sz)
            dh4_ref[3, :, cols] = _mx(dya_t * (on * ghg) * (sz + hz * sz * (1.0 - sz)))
            dg_ref[...] += _bcast_rows(jnp.sum(don * on, axis=0, keepdims=True), 8)
            tt = don * ghg
            do = r * (tt - on * jnp.mean(tt * on, axis=-1, keepdims=True))
            dob = _mx(do)

            for c in range(nc):
                sl = slice(c * HG_CHUNK, (c + 1) * HG_CHUNK)
                u_sc[hh, c] = _dot_tn(vb[sl], kob[sl])
                g_sc[hh, c] = _dot_tn(dob[sl], qib[sl])

            st = st_ref[0, hh]
            for c in range(nc):
                stf_sc[hh, c] = st
                stb_sc[hh, c] = _mx(st)
                if c < nc - 1:
                    st = st * gt["dec"][c * HG_CHUNK:c * HG_CHUNK + 1, :] + u_sc[hh, c]

            dst = dstate[hh]
            dd_parts = [None] * nc
            for c in reversed(range(nc)):
                dec = gt["dec"][c * HG_CHUNK:c * HG_CHUNK + 1, :]
                dstb_sc[hh, c] = _mx(dst)
                dd_parts[c] = _bcast_rows(jnp.sum(dst * stf_sc[hh, c], axis=0, keepdims=True) * dec, HG_CHUNK)
                dst = dst * dec + g_sc[hh, c]
            dstate[hh] = dst

            a = jnp.where(lower, _dot_nt(qib, kib), 0.0)
            da = _mx(jnp.where(lower, _dot_nt(dob, vb), 0.0))
            dqi_intra = _dot(da, kib)
            dki = _dot_tn(da, qib)
            dv_intra = _dot_tn(_mx(a), dob)

            dqi_parts, dko_parts, dv_parts = [None] * nc, [None] * nc, [None] * nc
            for c in range(nc):
                sl = slice(c * HG_CHUNK, (c + 1) * HG_CHUNK)
                dv_parts[c] = dv_intra[sl] + _dot_nt(kob[sl], dstb_sc[hh, c])
                dko_parts[c] = _dot(vb[sl], dstb_sc[hh, c])
                dqi_parts[c] = dqi_intra[sl] + _dot(dob[sl], stb_sc[hh, c])
            dqi = jnp.concatenate(dqi_parts, axis=0)
            dko = jnp.concatenate(dko_parts, axis=0)
            dv = jnp.concatenate(dv_parts, axis=0)
            dd = jnp.concatenate(dd_parts, axis=0)

            dq = dqi * gt["eb"]
            dk = dki * gt["enb"] + dko * gt["eo"]
            db = dqi * qi - dki * ki - dko * ko
            dlogf = _rcumsum_chunk(db, pos) + _chunk_total(dko * ko) + dd
            df = dlogf / gt["f"] - dk
            lb, sig, sq = gt["lb"], gt["sig"], gt["sq"]
            dh4_ref[1, :, cols] = _mx(df * (1.0 - lb) * (sig * (1.0 - sig)))
            dh4_ref[0, :, cols] = _mx(dq * (sq + hq * sq * (1.0 - sq)))
            dh4_ref[2, :, cols] = _mx(dv)
            dlb = jnp.sum(df * (1.0 - sig), axis=0, keepdims=True) * (lb * (1.0 - lb))
            dlb_ref[:, cols] += jnp.concatenate([dlb, -dlb], axis=0)

    hw = HG_HEADS_PER_STEP * HEAD_DIM
    hsteps = HEADS // HG_HEADS_PER_STEP

    def seg(k):
        return pl.BlockSpec((t, hw), lambda h, b, k=k: (nb - 1 - b, k * hsteps + h))

    blk = pl.BlockSpec((t, hw), lambda h, b: (nb - 1 - b, h))
    return pl.pallas_call(
        body,
        name="hgrn_bwd",
        grid=(hsteps, nb),
        in_specs=[seg(0), seg(1), seg(2), seg(3),
                  pl.BlockSpec((2, hw), lambda h, b: (0, h)),
                  pl.BlockSpec((1, HEAD_DIM), lambda h, b: (0, 0)),
                  blk, blk,
                  pl.BlockSpec((1, HG_HEADS_PER_STEP, HEAD_DIM, HEAD_DIM), lambda h, b: (nb - 1 - b, h, 0, 0))],
        out_specs=[pl.BlockSpec((4, t, hw), lambda h, b: (0, nb - 1 - b, h)),
                   pl.BlockSpec((2, hw), lambda h, b: (0, h)),
                   pl.BlockSpec((8, HEAD_DIM), lambda h, b: (0, 0))],
        out_shape=[jax.ShapeDtypeStruct((4, s, D_MODEL), MXU_DTYPE),
                   jax.ShapeDtypeStruct((2, D_MODEL), F32),
                   jax.ShapeDtypeStruct((8, HEAD_DIM), F32)],
        scratch_shapes=[pltpu.VMEM((HG_HEADS_PER_STEP, HEAD_DIM, HEAD_DIM), F32)]
        + [pltpu.VMEM((HG_HEADS_PER_STEP, nc, HEAD_DIM, HEAD_DIM), F32)] * 3
        + [pltpu.VMEM((HG_HEADS_PER_STEP, nc, HEAD_DIM, HEAD_DIM), MXU_DTYPE)] * 2,
        compiler_params=_params(),
    )(proj, proj, proj, proj, lb_logits, hg_norm_g, o_all, dya, states)


def _dh_bwd(segs, w_in_p, x, dx2, norm_g, late_slab, late_recv_init, slab_sets):
    s = x.shape[0]
    tm = min(s, TM_ROW)
    seg_ops = [sg if isinstance(sg, tuple) else (sg, None) for sg in segs]
    nseg = len(segs)
    nsteps = s // tm
    n_ops = len(slab_sets)
    late_xyc = ((LATE_DEV >> 2) & 1, (LATE_DEV >> 1) & 1, LATE_DEV & 1)

    def body(*refs):
        seg_refs = refs[:nseg]
        w_ref, x_ref, dx2_ref, g_ref, late_ref, _ = refs[nseg:nseg + 6]
        g_refs = refs[nseg + 6:nseg + 6 + n_ops]
        gx_ref, dng_ref, late_recv_ref = refs[nseg + 6 + n_ops:nseg + 9 + n_ops]
        recv_refs = refs[nseg + 9 + n_ops:nseg + 9 + 2 * n_ops]
        (dp_buf, send_sems, recv_sems, local_sems,
         late_send, late_recvs, late_local) = refs[nseg + 9 + 2 * n_ops:]
        i = pl.program_id(0)
        me = 4 * lax.axis_index("x") + 2 * lax.axis_index("y") + lax.axis_index("c")

        def misc_exchange():
            return _Exchange(g_refs, recv_refs, send_sems, recv_sems, local_sems)

        def late_copy(sender):
            return pltpu.make_async_remote_copy(
                src_ref=late_ref.at[0], dst_ref=late_recv_ref.at[sender], send_sem=late_send,
                recv_sem=late_recvs.at[(sender ^ LATE_DEV) - 1], device_id=late_xyc, device_id_type=MESH)

        def late_own():
            return pltpu.make_async_copy(late_ref.at[0], late_recv_ref.at[LATE_DEV], late_local)

        @pl.when(i == 0)
        def _():
            dng_ref[...] = jnp.zeros_like(dng_ref)
            misc_exchange().start()

        @pl.when((i == 0) & (me != LATE_DEV))
        def _():
            late_copy(me).start()

        @pl.when((i == 0) & (me == LATE_DEV))
        def _():
            late_own().start()

        for k, sref in enumerate(seg_refs):
            dp_buf[:, k * SEG:(k + 1) * SEG] = sref[...]
        dh = _dot_nt(dp_buf[...], w_ref[...])
        xf = x_ref[...]
        r = lax.rsqrt(jnp.mean(xf * xf, axis=-1, keepdims=True) + EPS)
        xh = xf * r
        dng_ref[...] += _bcast_rows(jnp.sum(dh * xh, axis=0, keepdims=True), 8)
        tt = dh * g_ref[...]
        gx_ref[...] = dx2_ref[...] + r * (tt - xh * jnp.mean(tt * xh, axis=-1, keepdims=True))

        @pl.when(i == nsteps - 1)
        def _():
            misc_exchange().wait()

        @pl.when((i == nsteps - 1) & (me != LATE_DEV))
        def _():
            late_copy(me).wait_send()

        @pl.when((i == nsteps - 1) & (me == LATE_DEV))
        def _():
            for k in range(1, N_DEV):
                late_copy(LATE_DEV ^ k).wait_recv()
            late_own().wait()

    rows = pl.BlockSpec((tm, D_MODEL), lambda i: (i, 0))
    return pl.pallas_call(
        body,
        name="dh_bwd",
        grid=(nsteps,),
        in_specs=[pl.BlockSpec((tm, SEG), lambda i: (i, 0)) if j is None
                  else pl.BlockSpec((None, tm, SEG), lambda i, j=j: (j, i, 0)) for _, j in seg_ops] + [
            _resident((D_MODEL, PROJ_W)),
            rows, rows,
            pl.BlockSpec((1, D_MODEL), lambda i: (0, 0)),
            HBM_SPEC, HBM_SPEC,
        ] + [HBM_SPEC] * n_ops,
        out_specs=[rows, pl.BlockSpec((8, D_MODEL), lambda i: (0, 0)), HBM_SPEC] + [HBM_SPEC] * n_ops,
        out_shape=[jax.ShapeDtypeStruct((s, D_MODEL), F32), jax.ShapeDtypeStruct((8, D_MODEL), F32),
                   jax.ShapeDtypeStruct(late_recv_init.shape, late_recv_init.dtype)]
        + [jax.ShapeDtypeStruct(a.shape, a.dtype) for a in slab_sets],
        input_output_aliases={nseg + 5: 2},
        scratch_shapes=[pltpu.VMEM((tm, PROJ_W), MXU_DTYPE)] + _Exchange.semaphores(n_ops)
        + [pltpu.SemaphoreType.DMA, pltpu.SemaphoreType.DMA((N_DEV - 1,)), pltpu.SemaphoreType.DMA],
        compiler_params=_params(),
    )(*[a for a, _ in seg_ops], w_in_p, x, dx2, norm_g, late_slab, late_recv_init, *slab_sets)


def _matmul_tn(a, b, name, out_dtype=F32):
    s, m = a.shape
    stacked = b.ndim == 3
    n = b.shape[0] * b.shape[2] if stacked else b.shape[1]
    ts = min(s, TS_TN)
    tn = min(n, SEG)
    nk = s // ts
    if stacked:
        b_spec = pl.BlockSpec((None, ts, tn), lambda j, k: (j, k, 0))
    else:
        b_spec = pl.BlockSpec((ts, tn), lambda j, k: (k, j))

    def body(a_ref, b_ref, o_ref, acc):
        k = pl.program_id(1)
        part = _dot_tn(a_ref[...], b_ref[...])

        @pl.when(k == 0)
        def _():
            acc[...] = part

        @pl.when(k > 0)
        def _():
            acc[...] += part

        @pl.when(k == nk - 1)
        def _():
            o_ref[...] = acc[...].astype(out_dtype)

    return pl.pallas_call(
        body,
        name=name,
        grid=(n // tn, nk),
        in_specs=[pl.BlockSpec((ts, m), lambda j, k: (k, 0)), b_spec],
        out_specs=pl.BlockSpec((m, tn), lambda j, k: (0, j)),
        out_shape=jax.ShapeDtypeStruct((m, n), out_dtype),
        scratch_shapes=[pltpu.VMEM((m, tn), F32)],
        compiler_params=_params(),
    )(a, b)


def _w_in_pieces():
    per = IN_COLS // N_DEV
    pad_at = SMALL_SEG * SEG + Q_LORA + KV_LORA + QK_ROPE
    pieces = []
    for j in range(N_DEV):
        u0, u1 = j * per, (j + 1) * per
        cuts = [u0] + ([pad_at] if u0 < pad_at < u1 else []) + [u1]
        for a, b in zip(cuts[:-1], cuts[1:]):
            pieces.append((j, a - u0, b - u0, a if a < pad_at else a + PROJ_W - IN_COLS))
    return pad_at, pieces


def _assemble_w_in(gathered):
    tr = TM_ROW
    pad_at, pieces = _w_in_pieces()

    def body(in_ref, out_ref):
        out_ref[:, pad_at:pad_at + PROJ_W - IN_COLS] = jnp.zeros((tr, PROJ_W - IN_COLS), gathered.dtype)
        for j, a, b, p0 in pieces:
            out_ref[:, p0:p0 + b - a] = in_ref[j, :, a:b]

    return pl.pallas_call(
        body,
        name="assemble_w_in",
        grid=(D_MODEL // tr,),
        in_specs=[pl.BlockSpec((N_DEV, tr, PACK_COLS), lambda i: (0, i, 0))],
        out_specs=pl.BlockSpec((tr, PROJ_W), lambda i: (i, 0)),
        out_shape=jax.ShapeDtypeStruct((D_MODEL, PROJ_W), gathered.dtype),
        compiler_params=_params(),
    )(gathered)


def _scatter_dw_in(dw_segs, devs, name):
    tr = TM_ROW
    _, pieces = _w_in_pieces()
    per = IN_COLS // N_DEV
    seg_ids = sorted(dw_segs)
    nseg = len(seg_ids)
    seg_ops = [dw_segs[k] if isinstance(dw_segs[k], tuple) else (dw_segs[k], 0) for k in seg_ids]

    def body(*refs):
        out_ref, buf = refs[nseg:]
        for k in range(PROJ_W // SEG):
            if k in seg_ids:
                buf[:, k * SEG:(k + 1) * SEG] = refs[seg_ids.index(k)][...]
            else:
                buf[:, k * SEG:(k + 1) * SEG] = jnp.zeros((tr, SEG), F32)
        for slot, dev in enumerate(devs):
            out_ref[slot, :, per:] = jnp.zeros((tr, PACK_COLS - per), TRANSPORT_DTYPE)
            for j, a, b, p0 in pieces:
                if j == dev:
                    out_ref[slot, :, a:b] = buf[:, p0:p0 + b - a].astype(TRANSPORT_DTYPE)

    return pl.pallas_call(
        body,
        name=name,
        grid=(D_MODEL // tr,),
        in_specs=[pl.BlockSpec((tr, SEG), lambda i, j=j: (i, j)) for _, j in seg_ops],
        out_specs=pl.BlockSpec((len(devs), tr, PACK_COLS), lambda i: (0, i, 0)),
        out_shape=jax.ShapeDtypeStruct((len(devs), D_MODEL, PACK_COLS), TRANSPORT_DTYPE),
        scratch_shapes=[pltpu.VMEM((tr, PROJ_W), F32)],
        compiler_params=_params(),
    )(*[a for a, _ in seg_ops])


def _rope_tables(s):
    inv = ROPE_THETA ** (-jnp.arange(0, QK_ROPE, 2, dtype=F32) / QK_ROPE)
    ang = jnp.arange(s, dtype=F32)[:, None] * inv[None, :]
    cos, sin = jnp.cos(ang), jnp.sin(ang)
    z32 = jnp.zeros_like(cos)
    z64 = jnp.zeros((s, HEAD_DIM - QK_ROPE), F32)
    cos_t = jnp.concatenate([cos, cos, z64], axis=1)
    sin_a = jnp.concatenate([-sin, z32, z64], axis=1)
    sin_b = jnp.concatenate([z32, sin, z64], axis=1)
    return cos_t, sin_a, sin_b


def _small_rows(b_gate, lb_logits, hg_norm_g, q_a_g, kv_a_g, final_norm_g, loss):
    row5 = jnp.concatenate([hg_norm_g.reshape(1, -1), q_a_g.reshape(1, -1), kv_a_g.reshape(1, -1), loss.reshape(1, 1),
                            jnp.zeros((1, PACK_COLS - LOSS_LANE - 1), F32)], axis=1)
    zero_row = jnp.zeros((1, PACK_COLS), F32)
    return jnp.concatenate([zero_row, b_gate.reshape(2, -1), lb_logits, row5, final_norm_g.reshape(1, -1), zero_row],
                           axis=0)


def _adamw_small(recv_small, recv_norm_g, weights, m, v):
    n = len(weights)

    def body(rs_ref, rn_ref, *refs):
        w_refs, m_refs, v_refs = refs[:n], refs[n:2 * n], refs[2 * n:3 * n]
        loss_ref = refs[3 * n]
        outs = refs[3 * n + 1:]
        gs, gn = rs_ref[0], rn_ref[0]
        for i in range(1, N_DEV):
            gs = gs + rs_ref[i]
            gn = gn + rn_ref[i]
        loss_ref[...] = gs[5:6, LOSS_LANE:LOSS_LANE + HEAD_DIM]
        grads = [gn[0:1], jnp.concatenate([gs[1:2], gs[2:3]], axis=1), gs[3:5],
                 gs[5:6, :HEAD_DIM], gs[5:6, HEAD_DIM:HEAD_DIM + Q_LORA], gs[5:6, HEAD_DIM + Q_LORA:LOSS_LANE], gs[6:7]]
        for k, g in enumerate(grads):
            m_new = ADAM_B1 * m_refs[k][...] + (1.0 - ADAM_B1) * g
            v_new = ADAM_B2 * v_refs[k][...] + (1.0 - ADAM_B2) * (g * g)
            m_hat = m_new / (1.0 - ADAM_B1 ** ADAM_STEP)
            v_hat = v_new / (1.0 - ADAM_B2 ** ADAM_STEP)
            outs[k][...] = g
            outs[n + k][...] = -ADAM_LR * (m_hat / (jnp.sqrt(v_hat) + ADAM_EPS) + ADAM_WD * w_refs[k][...])
            outs[2 * n + k][...] = m_new
            outs[3 * n + k][...] = v_new

    shapes = [jax.ShapeDtypeStruct(w.shape, F32) for w in weights]
    res = pl.pallas_call(
        body,
        name="adamw_small",
        out_shape=[jax.ShapeDtypeStruct((1, HEAD_DIM), F32)] + shapes * 4,
        compiler_params=_params(),
    )(recv_small, recv_norm_g, *weights, *m, *v)
    return res[0], [res[1 + k * n:1 + (k + 1) * n] for k in range(4)]


def _weight_shard_buffers(w_in, w_uq, w_ukv, w_pa, w_pb, w_out):
    w_in_pad = jnp.pad(w_in.reshape(D_MODEL, -1), ((0, 0), (0, PACK_COLS - IN_COLS // N_DEV)))
    parts = [a.reshape(-1, PACK_COLS) for a in (w_pa, w_pb, w_out, w_uq, w_ukv)]
    others = jnp.concatenate(parts + [jnp.zeros((ROWS_OTHER - ROWS_OTHER_USED, PACK_COLS), F32)], axis=0)
    return w_in_pad.astype(MXU_DTYPE), others.astype(MXU_DTYPE)


def _other_weights(gathered):
    r0 = 0
    mats = []
    for _ in range(3):
        mats.append(gathered[:, r0:r0 + ROWS_W_PROJ].reshape(D_MODEL, D_MODEL))
        r0 += ROWS_W_PROJ
    w_uq = gathered[:, r0:r0 + ROWS_W_UQ].reshape(N_DEV, Q_LORA, QK_DIM).transpose(1, 0, 2)
    w_uq_p = jnp.concatenate([w_uq, jnp.zeros((Q_LORA, HEADS, HEAD_PAD - QK_DIM), w_uq.dtype)], axis=2)
    w_uq_p = w_uq_p.reshape(Q_LORA, HEADS * HEAD_PAD)
    r0 += ROWS_W_UQ
    w_ukv = gathered[:, r0:r0 + ROWS_W_UKV].reshape(N_DEV, KV_LORA, 2 * HEAD_DIM).transpose(1, 0, 2)
    w_kn = w_ukv[:, :, :HEAD_DIM].reshape(KV_LORA, D_MODEL)
    w_v = w_ukv[:, :, HEAD_DIM:].reshape(KV_LORA, D_MODEL)
    return w_uq_p, w_kn, w_v, mats[0], mats[1], mats[2]


def _late_slab_sets(dw_uq_p, dw_kn, dw_v, small_rows):
    uq = dw_uq_p.reshape(Q_LORA, HEADS, HEAD_PAD).transpose(1, 0, 2)
    ukv = jnp.concatenate([dw_kn.reshape(KV_LORA, HEADS, HEAD_DIM),
                           dw_v.reshape(KV_LORA, HEADS, HEAD_DIM)], axis=2).transpose(1, 0, 2)
    return [uq, ukv, jnp.broadcast_to(small_rows[None], (N_DEV,) + small_rows.shape)]


def _step_gradients(x, target, norm_g, b_gate, lb_logits, hg_norm_g, q_a_g, kv_a_g, final_g,
                    w_in_p, other_shard):
    s = x.shape[0]
    cos, sin_a, sin_b = _rope_tables(s)
    proj, h, gathered = _inproj(x, norm_g, w_in_p, other_shard)
    w_uq_p, w_kn, w_v, w_pa, w_pb, w_out = _other_weights(gathered)
    o_all, ya, states = _hgrn_fwd(proj, lb_logits, hg_norm_g)
    q_all, k_all, v_all, cqn, ckvn = _mla_prep(proj, q_a_g, kv_a_g, w_uq_p, w_kn, w_v, cos, sin_a, sin_b)
    ao, yb, mblk, p_all = _flash_fwd(q_all, k_all, v_all, proj)
    (dx2, dya, dao, d_tail, merged_b, dpa_b, dpb_b, dx2_b,
     loss_acc, dfg_acc, dbg_acc) = _merge_fwd_bwd(x, target, ya, yb, ao, proj, b_gate, final_g, w_pa, w_pb, w_out)
    d_head, dlb, dhg_acc = _hgrn_bwd(proj, lb_logits, hg_norm_g, o_all, dya, states)

    dw_head = _matmul_tn(h, d_head, "dw_in_head")
    dw_tail = _matmul_tn(h, d_tail, "dw_in_tail")
    dw_early = {k: (dw_head, k) for k in range(4)}
    dw_early.update({MZ_SEG + k: (dw_tail, k) for k in range(3)})
    mats = [_matmul_tn(a, b, name, TRANSPORT_DTYPE).reshape(N_DEV, ROWS_W_PROJ, PACK_COLS)
            for a, b, name in ((ya, dpa_b, "dw_pa"), (yb, dpb_b, "dw_pb"), (merged_b, dx2_b, "dw_out"))]
    early_slabs = [_scatter_dw_in(dw_early, list(range(N_DEV)), "scatter_dw_in")] + mats
    dq_all, dk_all, dv_all, (recv_in, recv_pa, recv_pb, recv_out) = _flash_bwd(
        q_all, k_all, v_all, dao, ao, mblk, p_all, early_slabs)

    dsmall, dqf_b, dkn_b, dv_b, dgq_acc, dgk_acc = _mla_prep_bwd(
        dq_all, dk_all, dv_all, proj, q_a_g, kv_a_g, w_uq_p, w_kn, w_v, cos, sin_a, sin_b)
    late_slab = _scatter_dw_in({SMALL_SEG: _matmul_tn(h, dsmall, "dw_in_%d" % SMALL_SEG)}, [LATE_DEV],
                               "scatter_dw_in_late")
    segs = [(d_head, k) for k in range(4)] + [dsmall] + [(d_tail, k) for k in range(3)]
    return dict(
        segs=segs, h=h, dx2=dx2, late_slab=late_slab,
        dw_uq_p=_matmul_tn(cqn, dqf_b, "dw_uq"), dw_kn=_matmul_tn(ckvn, dkn_b, "dw_kn"),
        dw_v=_matmul_tn(ckvn, dv_b, "dw_v"),
        small=dict(b_gate=dbg_acc[0:1], lb_logits=dlb, hg_norm_g=dhg_acc[0:1], q_a_g=dgq_acc[0:1],
                   kv_a_g=dgk_acc[0:1], final_norm_g=dfg_acc[0], loss=loss_acc[0, 0]),
        recv=dict(w_in=[recv_in], w_pa=[recv_pa], w_pb=[recv_pb], w_out=[recv_out]),
    )


def kernel(x, norm_g, w_in, b_gate, lb_logits, hg_norm_g, q_a_g, w_uq, kv_a_g, w_ukv, w_proj_a, w_proj_b, w_out, final_norm_g, loss_target, m_norm_g, m_w_in, m_b_gate, m_lb_logits, m_hg_norm_g, m_q_a_g, m_w_uq, m_kv_a_g, m_w_ukv, m_w_proj_a, m_w_proj_b, m_w_out, m_final_norm_g, v_norm_g, v_w_in, v_b_gate, v_lb_logits, v_hg_norm_g, v_q_a_g, v_w_uq, v_kv_a_g, v_w_ukv, v_w_proj_a, v_w_proj_b, v_w_out, v_final_norm_g):
    xs = x[0]
    w_in_shard, other_shard = _weight_shard_buffers(w_in, w_uq, w_ukv, w_proj_a, w_proj_b, w_out)
    w_in_p = _assemble_w_in(_all_gather_packed(w_in_shard))
    g = _step_gradients(xs, loss_target[0], norm_g, b_gate, lb_logits, hg_norm_g, q_a_g, kv_a_g,
                        final_norm_g.reshape(1, -1), w_in_p, other_shard)
    sm = g["small"]
    late_sets = _late_slab_sets(g["dw_uq_p"], g["dw_kn"], g["dw_v"],
                                _small_rows(sm["b_gate"], sm["lb_logits"], sm["hg_norm_g"], sm["q_a_g"],
                                            sm["kv_a_g"], sm["final_norm_g"], sm["loss"]))
    late_recv_init = jnp.zeros((N_DEV, D_MODEL, PACK_COLS), TRANSPORT_DTYPE)
    grad_x, dng_acc, recv_late, recv_uq, recv_ukv, recv_small = _dh_bwd(
        g["segs"], w_in_p, xs, g["dx2"], norm_g, g["late_slab"], late_recv_init, late_sets)
    recv_ng = _exchange_rows(jnp.broadcast_to(dng_acc[None], (N_DEV, 8, D_MODEL)))

    recv = g["recv"]
    big = dict(
        w_in=_sum_adamw(recv["w_in"] + [recv_late], w_in[0], m_w_in[0], v_w_in[0], "adamw_w_in"),
        w_proj_a=_sum_adamw(recv["w_pa"], w_proj_a[0], m_w_proj_a[0], v_w_proj_a[0], "adamw_w_pa"),
        w_proj_b=_sum_adamw(recv["w_pb"], w_proj_b[0], m_w_proj_b[0], v_w_proj_b[0], "adamw_w_pb"),
        w_out=_sum_adamw(recv["w_out"], w_out[0], m_w_out[0], v_w_out[0], "adamw_w_out"),
        w_uq=_sum_adamw([recv_uq], w_uq[0], m_w_uq[0], v_w_uq[0], "adamw_w_uq"),
        w_ukv=_sum_adamw([recv_ukv], w_ukv[0], m_w_ukv[0], v_w_ukv[0], "adamw_w_ukv"),
    )
    small_names = ["norm_g", "b_gate", "lb_logits", "hg_norm_g", "q_a_g", "kv_a_g", "final_norm_g"]
    loss_row, small = _adamw_small(
        recv_small, recv_ng,
        (norm_g, b_gate, lb_logits, hg_norm_g, q_a_g, kv_a_g, final_norm_g.reshape(1, -1)),
        (m_norm_g, m_b_gate, m_lb_logits, m_hg_norm_g, m_q_a_g, m_kv_a_g, m_final_norm_g.reshape(1, -1)),
        (v_norm_g, v_b_gate, v_lb_logits, v_hg_norm_g, v_q_a_g, v_kv_a_g, v_final_norm_g.reshape(1, -1)))
    names = ["norm_g", "w_in", "b_gate", "lb_logits", "hg_norm_g", "q_a_g", "w_uq", "kv_a_g", "w_ukv",
             "w_proj_a", "w_proj_b", "w_out", "final_norm_g"]
    results = []
    for kind in range(4):
        by_name = {n: big[n][kind][None] for n in big}
        by_name.update(zip(small_names, small[kind]))
        by_name["final_norm_g"] = by_name["final_norm_g"].reshape(-1)
        results += [by_name[n] for n in names]
    return (loss_row[0, 0], grad_x[None], *results)
```

```python
import jax
import jax.numpy as jnp
from jax import lax
from jax.experimental import pallas as pl
from jax.experimental.pallas import tpu as pltpu

D_MODEL = 1024
HEADS = 8
HEAD_DIM = 128
HG_CHUNK = 32
Q_LORA = 384
KV_LORA = 256
QK_ROPE = 64
QK_DIM = 192
ROPE_THETA = 10000.0
EPS = 1e-6
IN_COLS = 7872
ADAM_LR = 0.001
ADAM_B1 = 0.9
ADAM_B2 = 0.999
ADAM_EPS = 1e-08
ADAM_WD = 0.01
ADAM_STEP = 10

N_DEV = 8
SEG = 1024
PROJ_W = 8 * SEG
SMALL_SEG = 4
MZ_SEG = 5
GL_SEG = 6
HEAD_PAD = 256
PACK_COLS = 1024
LOSS_LANE = HEAD_DIM + Q_LORA + KV_LORA
ROWS_W_UQ = 72
ROWS_W_UKV = 64
ROWS_W_PROJ = 128
ROWS_OTHER_USED = 3 * ROWS_W_PROJ + ROWS_W_UQ + ROWS_W_UKV
ROWS_OTHER = 528
LATE_DEV = (SMALL_SEG * SEG) // (IN_COLS // N_DEV)
assert (SMALL_SEG * SEG + Q_LORA + KV_LORA + QK_ROPE - 1) // (IN_COLS // N_DEV) == LATE_DEV

QK_SCALE = QK_DIM ** -0.5
LOG2E = 1.4426950408889634
LN2 = 0.6931471805599453
Q_PRESCALE = QK_SCALE * LOG2E

MXU_DTYPE = jnp.bfloat16
TRANSPORT_DTYPE = jnp.bfloat16
VMEM_LIMIT = 48 * 1024 * 1024
VMEM_LIMIT_BIG = 60 * 1024 * 1024

T_HGRN = 512
HG_HEADS_PER_STEP = 4
TM_ROW = 256
TM_MLA = 512
TM_INPROJ = 512
T_ATT = 1024
T_ATT_BWD = T_ATT
ATT_SUB = 4
P_SLOTS = 4
LSE_LANE = 127
ATT_SUB_BWD = 2
TS_TN = 2048
TR_ADAM = 256

F32 = jnp.float32
MESH = pl.DeviceIdType.MESH


def _dot(a, b):
    return jnp.dot(a, b, preferred_element_type=F32)


def _dot_nt(a, b):
    return lax.dot_general(a, b, (((1,), (1,)), ((), ())), preferred_element_type=F32)


def _dot_tn(a, b):
    return lax.dot_general(a, b, (((0,), (0,)), ((), ())), preferred_element_type=F32)


def _mx(a):
    return a.astype(MXU_DTYPE)


def _sigmoid(x):
    return 1.0 / (1.0 + jnp.exp(-x))


def _params(vmem=VMEM_LIMIT, **kw):
    return pltpu.CompilerParams(vmem_limit_bytes=vmem, **kw)


def _bcast_rows(row, n):
    return jnp.broadcast_to(row, (n, row.shape[-1]))


def _resident(shape):
    return pl.BlockSpec(shape, lambda *_: (0, 0), pipeline_mode=pl.Buffered(1))


HBM_SPEC = pl.BlockSpec(memory_space=pltpu.HBM)


def _all_gather_packed(shard):
    rows, cols = shard.shape

    def body(x_ref, out_ref, send_sems, recv_sems, local_sem):
        x, y, c = lax.axis_index("x"), lax.axis_index("y"), lax.axis_index("c")
        me, sibling = (x, y, c), (x, y, 1 - c)
        chips = [(1 - x, y), (x, 1 - y), (1 - x, 1 - y)]

        def slot(px, py, pc):
            return out_ref.at[4 * px + 2 * py + pc]

        def copy(k, block, to, src=None):
            return pltpu.make_async_remote_copy(
                src_ref=slot(*block) if src is None else src,
                dst_ref=slot(*block),
                send_sem=send_sems.at[k],
                recv_sem=recv_sems.at[k],
                device_id=to,
                device_id_type=MESH,
            )

        mine = pltpu.make_async_copy(x_ref, slot(*me), local_sem)
        mine.start()
        first = [copy(0, me, sibling, src=x_ref)]
        first += [copy(1 + j, me, (*chip, c), src=x_ref) for j, chip in enumerate(chips)]
        for cp in first:
            cp.start()
        passed = [copy(4 + j, (*chip, c), sibling) for j, chip in enumerate(chips)]
        for j, chip in enumerate(chips):
            copy(1 + j, (*chip, c), me).wait_recv()
            passed[j].start()
        copy(0, sibling, me).wait_recv()
        for j, chip in enumerate(chips):
            copy(4 + j, (*chip, 1 - c), me).wait_recv()
        for cp in first + passed:
            cp.wait_send()
        mine.wait()

    return pl.pallas_call(
        body,
        name="ag_weights",
        out_shape=jax.ShapeDtypeStruct((N_DEV, rows, cols), shard.dtype),
        in_specs=[HBM_SPEC],
        out_specs=HBM_SPEC,
        scratch_shapes=[
            pltpu.SemaphoreType.DMA((7,)),
            pltpu.SemaphoreType.DMA((7,)),
            pltpu.SemaphoreType.DMA,
        ],
    )(shard)


class _Exchange:
    def __init__(self, g_refs, recv_refs, send_sems, recv_sems, local_sems, gather=False):
        x, y, c = lax.axis_index("x"), lax.axis_index("y"), lax.axis_index("c")
        me = 4 * x + 2 * y + c
        n_ops = len(g_refs)

        def source(i, dest):
            return g_refs[i] if gather else g_refs[i].at[dest]

        def copy(i, k, landing):
            px, py, pc = x ^ ((k >> 2) & 1), y ^ ((k >> 1) & 1), c ^ (k & 1)
            peer = 4 * px + 2 * py + pc
            return pltpu.make_async_remote_copy(
                src_ref=source(i, peer),
                dst_ref=recv_refs[i].at[peer if landing else me],
                send_sem=send_sems.at[i * (N_DEV - 1) + k - 1],
                recv_sem=recv_sems.at[i * (N_DEV - 1) + k - 1],
                device_id=(px, py, pc),
                device_id_type=MESH,
            )

        pairs = [(i, k) for i in range(n_ops) for k in range(1, N_DEV)]
        self.mine = lambda: [pltpu.make_async_copy(source(i, me), recv_refs[i].at[me], local_sems.at[i])
                             for i in range(n_ops)]
        self.sends = lambda: [copy(i, k, False) for i, k in pairs]
        self.landings = lambda: [copy(i, k, True) for i, k in pairs]

    def start(self):
        for cp in self.mine() + self.sends():
            cp.start()

    def wait(self):
        for cp in self.landings():
            cp.wait_recv()
        for cp in self.sends():
            cp.wait_send()
        for cp in self.mine():
            cp.wait()

    @staticmethod
    def semaphores(n_ops):
        return [pltpu.SemaphoreType.DMA((n_ops * (N_DEV - 1),)),
                pltpu.SemaphoreType.DMA((n_ops * (N_DEV - 1),)),
                pltpu.SemaphoreType.DMA((n_ops,))]


def _exchange_rows(slabs):
    def body(g_ref, recv_ref, send_sems, recv_sems, local_sems):
        exchange = _Exchange([g_ref], [recv_ref], send_sems, recv_sems, local_sems)
        exchange.start()
        exchange.wait()

    return pl.pallas_call(
        body,
        name="exchange_rows",
        out_shape=jax.ShapeDtypeStruct(slabs.shape, slabs.dtype),
        in_specs=[HBM_SPEC],
        out_specs=HBM_SPEC,
        scratch_shapes=_Exchange.semaphores(1),
    )(slabs)


def _sum_adamw(recvs, w, m, v, name):
    rows, cols = w.shape
    tr = TR_ADAM if rows % TR_ADAM == 0 else rows
    n_recv = len(recvs)

    def body(*refs):
        w_ref, m_ref, v_ref, g_out, d_out, m_out, v_out = refs[n_recv:]
        g = None
        for r_ref in refs[:n_recv]:
            for i in range(N_DEV):
                part = r_ref[i].astype(F32)
                g = part if g is None else g + part
        g = g[:, :cols]
        m_new = ADAM_B1 * m_ref[...] + (1.0 - ADAM_B1) * g
        v_new = ADAM_B2 * v_ref[...] + (1.0 - ADAM_B2) * (g * g)
        m_hat = m_new / (1.0 - ADAM_B1 ** ADAM_STEP)
        v_hat = v_new / (1.0 - ADAM_B2 ** ADAM_STEP)
        g_out[...] = g
        d_out[...] = -ADAM_LR * (m_hat / (jnp.sqrt(v_hat) + ADAM_EPS) + ADAM_WD * w_ref[...])
        m_out[...] = m_new
        v_out[...] = v_new

    row_spec = pl.BlockSpec((tr, cols), lambda i: (i, 0))
    shape = jax.ShapeDtypeStruct((rows, cols), F32)
    return pl.pallas_call(
        body,
        name=name,
        grid=(rows // tr,),
        in_specs=[pl.BlockSpec((N_DEV, tr, recvs[0].shape[2]), lambda i: (0, i, 0))] * n_recv + [row_spec] * 3,
        out_specs=[row_spec] * 4,
        out_shape=[shape] * 4,
        compiler_params=_params(),
    )(*recvs, w, m, v)


def _inproj(x, norm_g, w_in_p, other_shard):
    s = x.shape[0]
    tm = min(s, TM_INPROJ)
    nsteps = s // tm
    halves = 2
    segs_per_half = PROJ_W // SEG // halves

    def body(x_ref, g_ref, w_ref, shard_ref, proj_ref, h_ref, gathered_ref, send_sems, recv_sems, local_sems):
        i, j = pl.program_id(0), pl.program_id(1)

        def all_gather():
            return _Exchange([shard_ref], [gathered_ref], send_sems, recv_sems, local_sems, gather=True)

        @pl.when((i == 0) & (j == 0))
        def _():
            all_gather().start()

        xf = x_ref[...]
        r = lax.rsqrt(jnp.mean(xf * xf, axis=-1, keepdims=True) + EPS)
        h = _mx(xf * r * g_ref[...])
        h_ref[...] = h
        for half in range(halves):
            @pl.when(j == half)
            def _():
                for k in range(segs_per_half):
                    w_cols = slice((half * segs_per_half + k) * SEG, (half * segs_per_half + k + 1) * SEG)
                    proj_ref[:, k * SEG:(k + 1) * SEG] = _dot(h, w_ref[:, w_cols])

        @pl.when((i == nsteps - 1) & (j == halves - 1))
        def _():
            all_gather().wait()

    return pl.pallas_call(
        body,
        name="inproj",
        grid=(nsteps, halves),
        in_specs=[
            pl.BlockSpec((tm, D_MODEL), lambda i, j: (i, 0)),
            pl.BlockSpec((1, D_MODEL), lambda i, j: (0, 0)),
            _resident((D_MODEL, PROJ_W)),
            HBM_SPEC,
        ],
        out_specs=[
            pl.BlockSpec((tm, PROJ_W // halves), lambda i, j: (i, j)),
            pl.BlockSpec((tm, D_MODEL), lambda i, j: (i, 0)),
            HBM_SPEC,
        ],
        out_shape=[
            jax.ShapeDtypeStruct((s, PROJ_W), F32),
            jax.ShapeDtypeStruct((s, D_MODEL), MXU_DTYPE),
            jax.ShapeDtypeStruct((N_DEV,) + other_shard.shape, other_shard.dtype),
        ],
        scratch_shapes=_Exchange.semaphores(1),
        compiler_params=_params(),
    )(x, norm_g, w_in_p, other_shard)


def _chunk_lower_mask(t):
    row = lax.broadcasted_iota(jnp.int32, (t, t), 0)
    col = lax.broadcasted_iota(jnp.int32, (t, t), 1)
    return ((row // HG_CHUNK) == (col // HG_CHUNK)) & (col <= row)


def _chunk_pos(t):
    return lax.broadcasted_iota(jnp.int32, (t, HEAD_DIM), 0) & (HG_CHUNK - 1)


def _cumsum_chunk(x, pos):
    sh = 1
    while sh < HG_CHUNK:
        x = x + jnp.where(pos >= sh, pltpu.roll(x, sh, 0), 0.0)
        sh *= 2
    return x


def _rcumsum_chunk(x, pos):
    t = x.shape[0]
    sh = 1
    while sh < HG_CHUNK:
        x = x + jnp.where(pos < HG_CHUNK - sh, pltpu.roll(x, t - sh, 0), 0.0)
        sh *= 2
    return x


def _chunk_total(x):
    t, w = x.shape
    tot = jnp.sum(x.reshape(t // HG_CHUNK, HG_CHUNK, w), axis=1, keepdims=True)
    return jnp.broadcast_to(tot, (t // HG_CHUNK, HG_CHUNK, w)).reshape(t, w)


def _hgrn_gates(hq, hf, lb_logits, pos):
    lb = _sigmoid(lb_logits[0:1, :] - lb_logits[1:2, :])
    sig = _sigmoid(hf)
    f = lb + (1.0 - lb) * sig
    sq = _sigmoid(hq)
    q = hq * sq
    k = 1.0 - f
    logf = jnp.log(f)
    bcum = _cumsum_chunk(logf, pos)
    blast = _chunk_total(logf)
    eb = jnp.exp(bcum)
    enb = jnp.exp(-bcum)
    eo = jnp.exp(blast - bcum)
    return dict(lb=lb, sig=sig, f=f, sq=sq, q=q, k=k, eb=eb, enb=enb, eo=eo,
                qi=q * eb, ki=k * enb, ko=k * eo, dec=jnp.exp(blast))


def _hgrn_fwd(proj, lb_logits, hg_norm_g):
    s = proj.shape[0]
    t = min(s, T_HGRN)
    nb = s // t
    nc = t // HG_CHUNK
    hw = HG_HEADS_PER_STEP * HEAD_DIM

    def body(hq_ref, hf_ref, hi_ref, hz_ref, lb_ref, g_ref, o_ref, ya_ref, st_ref, state, u_sc, stb_sc):
        b = pl.program_id(1)

        @pl.when(b == 0)
        def _():
            state[...] = jnp.zeros_like(state)

        lower = _chunk_lower_mask(t)
        pos = _chunk_pos(t)
        for hh in range(HG_HEADS_PER_STEP):
            cols = slice(hh * HEAD_DIM, (hh + 1) * HEAD_DIM)
            st = state[hh]
            st_ref[0, hh] = st
            gt = _hgrn_gates(hq_ref[:, cols], hf_ref[:, cols], lb_ref[:, cols], pos)
            vb = _mx(hi_ref[:, cols])
            qib, kib, kob = _mx(gt["qi"]), _mx(gt["ki"]), _mx(gt["ko"])
            a = jnp.where(lower, _dot_nt(qib, kib), 0.0)
            o_intra = _dot(_mx(a), vb)
            for c in range(nc):
                sl = slice(c * HG_CHUNK, (c + 1) * HG_CHUNK)
                u_sc[hh, c] = _dot_tn(vb[sl], kob[sl])
            for c in range(nc):
                stb_sc[hh, c] = _mx(st)
                st = st * gt["dec"][c * HG_CHUNK:c * HG_CHUNK + 1, :] + u_sc[hh, c]
            state[hh] = st
            outs = []
            for c in range(nc):
                sl = slice(c * HG_CHUNK, (c + 1) * HG_CHUNK)
                outs.append(o_intra[sl] + _dot_nt(qib[sl], stb_sc[hh, c]))
            o = jnp.concatenate(outs, axis=0)
            o_ref[:, cols] = o
            r = lax.rsqrt(jnp.mean(o * o, axis=-1, keepdims=True) + EPS)
            hz = hz_ref[:, cols]
            ya_ref[:, cols] = _mx((o * r * g_ref[...]) * (hz * _sigmoid(hz)))

    hsteps = HEADS // HG_HEADS_PER_STEP

    def seg(k):
        return pl.BlockSpec((t, hw), lambda h, b, k=k: (b, k * hsteps + h))

    return pl.pallas_call(
        body,
        name="hgrn_fwd",
        grid=(hsteps, nb),
        in_specs=[seg(0), seg(1), seg(2), seg(3),
                  pl.BlockSpec((2, hw), lambda h, b: (0, h)),
                  pl.BlockSpec((1, HEAD_DIM), lambda h, b: (0, 0))],
        out_specs=[
            pl.BlockSpec((t, hw), lambda h, b: (b, h)),
            pl.BlockSpec((t, hw), lambda h, b: (b, h)),
            pl.BlockSpec((1, HG_HEADS_PER_STEP, HEAD_DIM, HEAD_DIM), lambda h, b: (b, h, 0, 0)),
        ],
        out_shape=[
            jax.ShapeDtypeStruct((s, D_MODEL), F32),
            jax.ShapeDtypeStruct((s, D_MODEL), MXU_DTYPE),
            jax.ShapeDtypeStruct((nb, HEADS, HEAD_DIM, HEAD_DIM), F32),
        ],
        scratch_shapes=[pltpu.VMEM((HG_HEADS_PER_STEP, HEAD_DIM, HEAD_DIM), F32),
                        pltpu.VMEM((HG_HEADS_PER_STEP, nc, HEAD_DIM, HEAD_DIM), F32),
                        pltpu.VMEM((HG_HEADS_PER_STEP, nc, HEAD_DIM, HEAD_DIM), MXU_DTYPE)],
        compiler_params=_params(),
    )(proj, proj, proj, proj, lb_logits, hg_norm_g)


def _rope(x, cos, sin_a, sin_b):
    return x * cos + pltpu.roll(x, 96, 1) * sin_a + pltpu.roll(x, 32, 1) * sin_b


def _rope_t(d, cos, sin_a, sin_b):
    return d * cos + pltpu.roll(d * sin_a, 32, 1) + pltpu.roll(d * sin_b, 96, 1)


def _mla_prep(proj, q_a_g, kv_a_g, w_uq_p, w_kn, w_v, cos, sin_a, sin_b):
    s = proj.shape[0]
    tm = min(s, TM_MLA)

    def body(sm_ref, gq_ref, gk_ref, wq_ref, wkn_ref, wv_ref, cos_ref, sa_ref, sb_ref,
             q_ref, k_ref, v_ref, cqn_ref, ckvn_ref):
        small = sm_ref[...]
        cq = small[:, :Q_LORA]
        ckv = small[:, Q_LORA:Q_LORA + KV_LORA]
        krp = small[:, Q_LORA + KV_LORA:Q_LORA + KV_LORA + HEAD_DIM]
        rq = lax.rsqrt(jnp.mean(cq * cq, axis=-1, keepdims=True) + EPS)
        rk = lax.rsqrt(jnp.mean(ckv * ckv, axis=-1, keepdims=True) + EPS)
        cqn = _mx(cq * rq * gq_ref[...])
        ckvn = _mx(ckv * rk * gk_ref[...])
        cqn_ref[...] = cqn
        ckvn_ref[...] = ckvn
        q = _dot(cqn, wq_ref[...]) * Q_PRESCALE
        kn = _dot(ckvn, wkn_ref[...])
        v = _dot(ckvn, wv_ref[...])
        cos_t, sa, sb = cos_ref[...], sa_ref[...], sb_ref[...]
        kpe = _mx(_rope(krp, cos_t, sa, sb))
        ones_col = (lax.broadcasted_iota(jnp.int32, (tm, HEAD_DIM), 1) == 0).astype(MXU_DTYPE)
        for h in range(HEADS):
            lo = h * HEAD_PAD
            v_ref[:, lo:lo + HEAD_DIM] = _mx(v[:, h * HEAD_DIM:(h + 1) * HEAD_DIM])
            v_ref[:, lo + HEAD_DIM:lo + HEAD_PAD] = ones_col
            q_ref[:, lo:lo + HEAD_DIM] = _mx(q[:, lo:lo + HEAD_DIM])
            q_ref[:, lo + HEAD_DIM:lo + HEAD_PAD] = _mx(_rope(q[:, lo + HEAD_DIM:lo + HEAD_PAD], cos_t, sa, sb))
            k_ref[:, lo:lo + HEAD_DIM] = _mx(kn[:, h * HEAD_DIM:(h + 1) * HEAD_DIM])
            k_ref[:, lo + HEAD_DIM:lo + HEAD_PAD] = kpe

    def const(shape):
        return pl.BlockSpec(shape, lambda i: (0, 0))

    def rows(w):
        return pl.BlockSpec((tm, w), lambda i: (i, 0))

    return pl.pallas_call(
        body,
        name="mla_prep",
        grid=(s // tm,),
        in_specs=[
            pl.BlockSpec((tm, SEG), lambda i: (i, SMALL_SEG)),
            const((1, Q_LORA)), const((1, KV_LORA)),
            const((Q_LORA, HEADS * HEAD_PAD)), const((KV_LORA, D_MODEL)), const((KV_LORA, D_MODEL)),
            rows(HEAD_DIM), rows(HEAD_DIM), rows(HEAD_DIM),
        ],
        out_specs=[rows(HEADS * HEAD_PAD)] * 3 + [rows(Q_LORA), rows(KV_LORA)],
        out_shape=[
            jax.ShapeDtypeStruct((s, HEADS * HEAD_PAD), MXU_DTYPE),
            jax.ShapeDtypeStruct((s, HEADS * HEAD_PAD), MXU_DTYPE),
            jax.ShapeDtypeStruct((s, HEADS * HEAD_PAD), MXU_DTYPE),
            jax.ShapeDtypeStruct((s, Q_LORA), MXU_DTYPE),
            jax.ShapeDtypeStruct((s, KV_LORA), MXU_DTYPE),
        ],
        compiler_params=_params(),
    )(proj, q_a_g, kv_a_g, w_uq_p, w_kn, w_v, cos, sin_a, sin_b)


def _flash_fwd(q_all, k_all, v_all, proj):
    s = q_all.shape[0]
    t = min(s, T_ATT)
    n = s // t
    ts = t // ATT_SUB
    n_pairs = n * (n + 1) // 2

    def body(q_ref, k_ref, v_ref, mz_ref, ao_ref, yb_ref, mblk_ref, p_hbm, m_sc, acc_sc, stage, p_sems):
        head, qi = pl.program_id(0), pl.program_id(1)
        m_sc[...] = jnp.full_like(m_sc, -jnp.inf)
        acc_sc[...] = jnp.zeros_like(acc_sc)
        mblk_ref[...] = jnp.zeros_like(mblk_ref)
        lane = lax.broadcasted_iota(jnp.int32, (ts, HEAD_DIM), 1)
        first_block = head * n_pairs + qi * (qi + 1) // 2

        def p_copy(slot, pair, r):
            rows = pl.ds(r * ts, ts)
            return pltpu.make_async_copy(stage.at[slot, rows], p_hbm.at[head, pair, rows], p_sems.at[slot, r])

        def p_wait(slot):
            for r in range(ATT_SUB):
                p_copy(slot, 0, r).wait()

        def key_block(ki, diagonal):
            base = pl.multiple_of(ki * t, t)
            count = first_block + ki
            slot = lax.rem(count, P_SLOTS)
            sc, pb, alpha = {}, {}, {}

            @pl.when(count >= P_SLOTS)
            def _():
                p_wait(slot)

            if diagonal:
                stage[slot] = jnp.zeros((t, t), MXU_DTYPE)

            def width(r):
                return (r + 1) * ts if diagonal else t

            def scores(r):
                w = width(r)
                s_r = _dot_nt(q_ref[r * ts:(r + 1) * ts], k_ref[pl.ds(base, w), :])
                if diagonal:
                    row = lax.broadcasted_iota(jnp.int32, (ts, w), 0) + r * ts
                    col = lax.broadcasted_iota(jnp.int32, (ts, w), 1)
                    s_r = jnp.where(row >= col, s_r, -jnp.inf)
                sc[r] = s_r

            def softmax(r):
                rs = slice(r * ts, (r + 1) * ts)
                m_prev = m_sc[rs]
                m_new = jnp.maximum(m_prev, jnp.max(sc[r], axis=-1, keepdims=True))
                pb[r] = _mx(jnp.exp2(sc[r] - m_new))
                alpha[r] = jnp.exp2(m_prev - m_new)
                m_sc[rs] = m_new
                mblk_ref[rs] = jnp.where(lane == ki, m_new, mblk_ref[rs])
                stage[slot, rs, :width(r)] = pb[r]
                p_copy(slot, qi * (qi + 1) // 2 + ki, r).start()

            def weighted_values(r):
                rs = slice(r * ts, (r + 1) * ts)
                acc_sc[rs] = alpha[r] * acc_sc[rs] + _dot(pb[r], v_ref[pl.ds(base, width(r)), :])

            for step in range(ATT_SUB + 2):
                if step < ATT_SUB:
                    scores(step)
                if 1 <= step <= ATT_SUB:
                    softmax(step - 1)
                if step >= 2:
                    weighted_values(step - 2)

        def below_diagonal(ki, carry):
            key_block(ki, False)
            return carry

        lax.fori_loop(0, qi, below_diagonal, 0)
        key_block(qi, True)

        @pl.when((head == HEADS - 1) & (qi == n - 1))
        def _():
            for slot in range(min(P_SLOTS, HEADS * n_pairs)):
                p_wait(slot)

        acc = acc_sc[...]
        l = acc[:, HEAD_DIM:HEAD_DIM + 1]
        ao = acc[:, :HEAD_DIM] / l
        ao_ref[...] = ao
        lane_t = lax.broadcasted_iota(jnp.int32, (t, HEAD_DIM), 1)
        mblk_ref[...] = jnp.where(lane_t == LSE_LANE, m_sc[...] + jnp.log2(l), mblk_ref[...])
        mz = mz_ref[...]
        yb_ref[...] = _mx(ao * (mz * _sigmoid(mz)))

    q_map = lambda h, qi: (qi, h)
    return pl.pallas_call(
        body,
        name="flash_fwd",
        grid=(HEADS, n),
        in_specs=[
            pl.BlockSpec((t, HEAD_PAD), q_map),
            pl.BlockSpec((s, HEAD_PAD), lambda h, qi: (0, h)),
            pl.BlockSpec((s, HEAD_PAD), lambda h, qi: (0, h)),
            pl.BlockSpec((t, HEAD_DIM), lambda h, qi: (qi, MZ_SEG * HEADS + h)),
        ],
        out_specs=[pl.BlockSpec((t, HEAD_DIM), q_map)] * 3 + [HBM_SPEC],
        out_shape=[
            jax.ShapeDtypeStruct((s, D_MODEL), F32),
            jax.ShapeDtypeStruct((s, D_MODEL), MXU_DTYPE),
            jax.ShapeDtypeStruct((s, D_MODEL), F32),
            jax.ShapeDtypeStruct((HEADS, n_pairs, t, t), MXU_DTYPE),
        ],
        scratch_shapes=[
            pltpu.VMEM((t, 1), F32),
            pltpu.VMEM((t, HEAD_PAD), F32),
            pltpu.VMEM((P_SLOTS, t, t), MXU_DTYPE),
            pltpu.SemaphoreType.DMA((P_SLOTS, ATT_SUB)),
        ],
        compiler_params=_params(),
    )(q_all, k_all, v_all, proj)


def _merge_fwd_bwd(x, target, ya, yb, ao, proj, b_gate, final_g, w_pa, w_pb, w_out):
    s = x.shape[0]
    tm = min(s, TM_ROW)

    def body(x_ref, t_ref, ya_ref, yb_ref, ao_ref, mz_ref, g0_ref, g1_ref, bg_ref, fg_ref, wpa_ref, wpb_ref, wo_ref,
             dx2_ref, dya_ref, dao_ref, dtail_ref, mb_ref, dpab_ref, dpbb_ref, dx2b_ref,
             loss_ref, dfg_ref, dbg_ref):
        i = pl.program_id(0)

        @pl.when(i == 0)
        def _():
            loss_ref[...] = jnp.zeros_like(loss_ref)
            dfg_ref[...] = jnp.zeros_like(dfg_ref)
            dbg_ref[...] = jnp.zeros_like(dbg_ref)

        pa = _dot(ya_ref[...], wpa_ref[...])
        pb = _dot(yb_ref[...], wpb_ref[...])
        bg = bg_ref[...]
        g0 = _sigmoid(g0_ref[...] + bg[:, :D_MODEL])
        g1 = _sigmoid(g1_ref[...] + bg[:, D_MODEL:])
        merged = g0 * pa + g1 * pb
        mb = _mx(merged)
        mb_ref[...] = mb
        x2 = x_ref[...] + _dot(mb, wo_ref[...])
        r = lax.rsqrt(jnp.mean(x2 * x2, axis=-1, keepdims=True) + EPS)
        xn = x2 * r
        fg = fg_ref[...]
        diff = xn * fg - t_ref[...]
        loss_ref[...] += 0.5 * jnp.sum(jnp.mean(diff * diff, axis=-1, keepdims=True))
        dy = diff * (1.0 / D_MODEL)
        dfg_ref[...] += _bcast_rows(jnp.sum(dy * xn, axis=0, keepdims=True), 8)
        tt = dy * fg
        dx2 = r * (tt - xn * jnp.mean(tt * xn, axis=-1, keepdims=True))
        dx2_ref[...] = dx2
        dx2b = _mx(dx2)
        dx2b_ref[...] = dx2b
        dmerged = _dot_nt(dx2b, wo_ref[...])
        dpa = _mx(dmerged * g0)
        dpb = _mx(dmerged * g1)
        dpab_ref[...] = dpa
        dpbb_ref[...] = dpb
        dg0 = dmerged * pa * (g0 * (1.0 - g0))
        dg1 = dmerged * pb * (g1 * (1.0 - g1))
        dtail_ref[1] = _mx(dg0)
        dtail_ref[2] = _mx(dg1)
        dbg_ref[:, :D_MODEL] += _bcast_rows(jnp.sum(dg0, axis=0, keepdims=True), 8)
        dbg_ref[:, D_MODEL:] += _bcast_rows(jnp.sum(dg1, axis=0, keepdims=True), 8)
        dya_ref[...] = _dot_nt(dpa, wpa_ref[...])
        dyb = _dot_nt(dpb, wpb_ref[...])
        mz = mz_ref[...]
        sg = _sigmoid(mz)
        dao_ref[...] = _mx(dyb * (mz * sg))
        dtail_ref[0] = _mx(dyb * ao_ref[...] * (sg + mz * sg * (1.0 - sg)))

    def rows(w=D_MODEL):
        return pl.BlockSpec((tm, w), lambda i: (i, 0))

    def const(shape):
        return pl.BlockSpec(shape, lambda i: (0, 0))

    def seg(k):
        return pl.BlockSpec((tm, SEG), lambda i: (i, k))

    f32 = jax.ShapeDtypeStruct((s, D_MODEL), F32)
    b16 = jax.ShapeDtypeStruct((s, D_MODEL), MXU_DTYPE)
    return pl.pallas_call(
        body,
        name="merge_fwd_bwd",
        grid=(s // tm,),
        in_specs=[
            rows(), rows(), rows(), rows(), rows(),
            seg(MZ_SEG), seg(GL_SEG), seg(GL_SEG + 1),
            const((1, 2 * D_MODEL)), const((1, D_MODEL)),
            _resident((D_MODEL, D_MODEL)), _resident((D_MODEL, D_MODEL)), _resident((D_MODEL, D_MODEL)),
        ],
        out_specs=[rows()] * 3 + [pl.BlockSpec((3, tm, D_MODEL), lambda i: (0, i, 0))] + [rows()] * 4
        + [const((8, HEAD_DIM)), const((8, D_MODEL)), const((8, 2 * D_MODEL))],
        out_shape=[f32, f32, b16, jax.ShapeDtypeStruct((3, s, D_MODEL), MXU_DTYPE), b16, b16, b16, b16,
                   jax.ShapeDtypeStruct((8, HEAD_DIM), F32),
                   jax.ShapeDtypeStruct((8, D_MODEL), F32),
                   jax.ShapeDtypeStruct((8, 2 * D_MODEL), F32)],
        compiler_params=_params(),
    )(x, target, ya, yb, ao, proj, proj, proj, b_gate, final_g, w_pa, w_pb, w_out)


def _flash_bwd(q_all, k_all, v_all, dao, ao, mblk, p_all, slab_sets):
    s = q_all.shape[0]
    t = min(s, T_ATT_BWD)
    n = s // t
    pairs = [(ki, qi) for ki in range(n) for qi in range(ki, n)]
    ki_list = jnp.asarray([p[0] for p in pairs], jnp.int32)
    qi_list = jnp.asarray([p[1] for p in pairs], jnp.int32)
    p_list = jnp.asarray([qi * (qi + 1) // 2 + ki for ki, qi in pairs], jnp.int32)
    n_ops = len(slab_sets)

    def body(ki_ref, qi_ref, pidx_ref, q_ref, k_ref, v_ref, do_ref, ao_ref, mblk_ref, p_ref, *rest):
        g_refs = rest[:n_ops]
        dq_ref, dk_ref, dv_ref = rest[n_ops:n_ops + 3]
        recv_refs = rest[n_ops + 3:2 * n_ops + 3]
        dk_acc, dv_acc, send_sems, recv_sems, local_sems = rest[2 * n_ops + 3:]
        head, step = pl.program_id(0), pl.program_id(1)
        ki, qi = ki_ref[step], qi_ref[step]

        @pl.when((head == 0) & (step == 0))
        def _():
            _Exchange(g_refs, recv_refs, send_sems, recv_sems, local_sems).start()

        @pl.when(qi == ki)
        def _():
            dk_acc[...] = jnp.zeros_like(dk_acc)
            dv_acc[...] = jnp.zeros_like(dv_acc)

        @pl.when(ki == 0)
        def _():
            dq_ref[pl.ds(pl.multiple_of(qi * t, t), t), :] = jnp.zeros((t, HEAD_PAD), F32)

        def pair(masked):
            nsub = ATT_SUB if masked else ATT_SUB_BWD
            ts = t // nsub
            dk_parts, dv_parts = [], []
            for r in range(nsub):
                rs = slice(r * ts, (r + 1) * ts)
                w = (r + 1) * ts if masked else t
                k = k_ref[:w]
                v = v_ref[:w]
                q = q_ref[rs]
                lane = lax.broadcasted_iota(jnp.int32, (ts, HEAD_DIM), 1)
                stats = mblk_ref[rs]
                m_blk = jnp.max(jnp.where(lane == ki, stats, -jnp.inf), axis=-1, keepdims=True)
                lse = jnp.max(jnp.where(lane == LSE_LANE, stats, -jnp.inf), axis=-1, keepdims=True)
                factor = jnp.exp2(m_blk - lse)
                p_st = p_ref[0, 0, rs, :w]
                do = do_ref[rs]
                do_f = do.astype(F32)
                delta = jnp.sum(do_f * ao_ref[rs], axis=-1, keepdims=True)
                dv_part = _dot_tn(p_st, _mx(do_f * factor))
                ds = p_st * _mx((_dot_nt(do, v) - delta) * factor)
                dk_part = _dot_tn(ds, q)
                rows = pl.ds(pl.multiple_of(qi * t + r * ts, ts), ts)
                dq_ref[rows, :] += _dot(ds, k)
                if masked:
                    dk_acc[:w] += dk_part
                    dv_acc[:w] += dv_part
                else:
                    dk_parts.append(dk_part)
                    dv_parts.append(dv_part)

            if not masked:
                dk_acc[...] += sum(dk_parts[1:], dk_parts[0])
                dv_acc[...] += sum(dv_parts[1:], dv_parts[0])

        @pl.when(qi == ki)
        def _():
            pair(True)

        @pl.when(qi > ki)
        def _():
            pair(False)

        @pl.when(qi == n - 1)
        def _():
            dk_ref[...] = _mx(dk_acc[...] * LN2)
            dv_ref[...] = _mx(dv_acc[...])

        @pl.when((head == HEADS - 1) & (step == len(pairs) - 1))
        def _():
            _Exchange(g_refs, recv_refs, send_sems, recv_sems, local_sems).wait()

    q_map = lambda h, p, ki_ref, qi_ref, pidx_ref: (qi_ref[p], h)
    kv_map = lambda h, p, ki_ref, qi_ref, pidx_ref: (ki_ref[p], h)
    grid_spec = pltpu.PrefetchScalarGridSpec(
        num_scalar_prefetch=3,
        grid=(HEADS, len(pairs)),
        in_specs=[
            pl.BlockSpec((t, HEAD_PAD), q_map),
            pl.BlockSpec((t, HEAD_PAD), kv_map),
            pl.BlockSpec((t, HEAD_DIM), lambda h, p, ki_ref, qi_ref, pidx_ref: (ki_ref[p], 2 * h)),
            pl.BlockSpec((t, HEAD_DIM), q_map),
            pl.BlockSpec((t, HEAD_DIM), q_map),
            pl.BlockSpec((t, HEAD_DIM), q_map),
            pl.BlockSpec((1, 1, t, t), lambda h, p, ki_ref, qi_ref, pidx_ref: (h, pidx_ref[p], 0, 0)),
        ] + [HBM_SPEC] * n_ops,
        out_specs=[
            pl.BlockSpec((s, HEAD_PAD), lambda h, p, ki_ref, qi_ref, pidx_ref: (0, h)),
            pl.BlockSpec((t, HEAD_PAD), kv_map),
            pl.BlockSpec((t, HEAD_DIM), kv_map),
        ] + [HBM_SPEC] * n_ops,
        scratch_shapes=[pltpu.VMEM((t, HEAD_PAD), F32), pltpu.VMEM((t, HEAD_DIM), F32)]
        + _Exchange.semaphores(n_ops),
    )
    outs = pl.pallas_call(
        body,
        name="flash_bwd",
        grid_spec=grid_spec,
        out_shape=[
            jax.ShapeDtypeStruct((s, HEADS * HEAD_PAD), F32),
            jax.ShapeDtypeStruct((s, HEADS * HEAD_PAD), MXU_DTYPE),
            jax.ShapeDtypeStruct((s, D_MODEL), MXU_DTYPE),
        ] + [jax.ShapeDtypeStruct(a.shape, a.dtype) for a in slab_sets],
        compiler_params=_params(VMEM_LIMIT_BIG),
    )(ki_list, qi_list, p_list, q_all, k_all, v_all, dao, ao, mblk, p_all, *slab_sets)
    return outs[0], outs[1], outs[2], outs[3:]


def _mla_prep_bwd(dq_all, dk_all, dv_all, proj, q_a_g, kv_a_g, w_uq_p, w_kn, w_v, cos, sin_a, sin_b):
    s = proj.shape[0]
    tm = min(s, TM_MLA)

    def body(dq_ref, dk_ref, dv_ref, sm_ref, gq_ref, gk_ref, wq_ref, wkn_ref, wv_ref, cos_ref, sa_ref, sb_ref,
             dsm_ref, dqf_ref, dkn_ref, dvb_ref, dgq_ref, dgk_ref):
        i = pl.program_id(0)

        @pl.when(i == 0)
        def _():
            dgq_ref[...] = jnp.zeros_like(dgq_ref)
            dgk_ref[...] = jnp.zeros_like(dgk_ref)

        cos_t, sa, sb = cos_ref[...], sa_ref[...], sb_ref[...]
        dkpe = jnp.zeros((tm, HEAD_DIM), F32)
        for h in range(HEADS):
            lo = h * HEAD_PAD
            dqf_ref[:, lo:lo + HEAD_DIM] = _mx(dq_ref[:, lo:lo + HEAD_DIM] * QK_SCALE)
            dqf_ref[:, lo + HEAD_DIM:lo + HEAD_PAD] = _mx(
                _rope_t(dq_ref[:, lo + HEAD_DIM:lo + HEAD_PAD] * QK_SCALE, cos_t, sa, sb))
            dkn_ref[:, h * HEAD_DIM:(h + 1) * HEAD_DIM] = dk_ref[:, lo:lo + HEAD_DIM]
            dkpe = dkpe + dk_ref[:, lo + HEAD_DIM:lo + HEAD_PAD].astype(F32)
        dkr = _rope_t(dkpe, cos_t, sa, sb)
        dvb = dv_ref[...]
        dvb_ref[...] = dvb
        dcqn = _dot_nt(dqf_ref[...], wq_ref[...])
        dckvn = _dot_nt(dkn_ref[...], wkn_ref[...]) + _dot_nt(dvb, wv_ref[...])

        small = sm_ref[...]
        cq = small[:, :Q_LORA]
        ckv = small[:, Q_LORA:Q_LORA + KV_LORA]
        rq = lax.rsqrt(jnp.mean(cq * cq, axis=-1, keepdims=True) + EPS)
        rk = lax.rsqrt(jnp.mean(ckv * ckv, axis=-1, keepdims=True) + EPS)
        cqh = cq * rq
        ckh = ckv * rk
        dgq_ref[...] += _bcast_rows(jnp.sum(dcqn * cqh, axis=0, keepdims=True), 8)
        dgk_ref[...] += _bcast_rows(jnp.sum(dckvn * ckh, axis=0, keepdims=True), 8)
        tq = dcqn * gq_ref[...]
        tk = dckvn * gk_ref[...]
        dcq = rq * (tq - cqh * jnp.mean(tq * cqh, axis=-1, keepdims=True))
        dckv = rk * (tk - ckh * jnp.mean(tk * ckh, axis=-1, keepdims=True))
        dsm_ref[:, :Q_LORA] = _mx(dcq)
        dsm_ref[:, Q_LORA:Q_LORA + KV_LORA] = _mx(dckv)
        dsm_ref[:, Q_LORA + KV_LORA:Q_LORA + KV_LORA + HEAD_DIM] = _mx(dkr)
        dsm_ref[:, Q_LORA + KV_LORA + HEAD_DIM:] = jnp.zeros((tm, SEG - Q_LORA - KV_LORA - HEAD_DIM), MXU_DTYPE)

    def const(shape):
        return pl.BlockSpec(shape, lambda i: (0, 0))

    def rows(w):
        return pl.BlockSpec((tm, w), lambda i: (i, 0))

    return pl.pallas_call(
        body,
        name="mla_prep_bwd",
        grid=(s // tm,),
        in_specs=[
            rows(HEADS * HEAD_PAD), rows(HEADS * HEAD_PAD), rows(D_MODEL),
            pl.BlockSpec((tm, SEG), lambda i: (i, SMALL_SEG)),
            const((1, Q_LORA)), const((1, KV_LORA)),
            const((Q_LORA, HEADS * HEAD_PAD)), const((KV_LORA, D_MODEL)), const((KV_LORA, D_MODEL)),
            rows(HEAD_DIM), rows(HEAD_DIM), rows(HEAD_DIM),
        ],
        out_specs=[rows(SEG), rows(HEADS * HEAD_PAD), rows(D_MODEL), rows(D_MODEL),
                   const((8, Q_LORA)), const((8, KV_LORA))],
        out_shape=[
            jax.ShapeDtypeStruct((s, SEG), MXU_DTYPE),
            jax.ShapeDtypeStruct((s, HEADS * HEAD_PAD), MXU_DTYPE),
            jax.ShapeDtypeStruct((s, D_MODEL), MXU_DTYPE),
            jax.ShapeDtypeStruct((s, D_MODEL), MXU_DTYPE),
            jax.ShapeDtypeStruct((8, Q_LORA), F32),
            jax.ShapeDtypeStruct((8, KV_LORA), F32),
        ],
        compiler_params=_params(),
    )(dq_all, dk_all, dv_all, proj, q_a_g, kv_a_g, w_uq_p, w_kn, w_v, cos, sin_a, sin_b)


def _hgrn_bwd(proj, lb_logits, hg_norm_g, o_all, dya, states):
    s = proj.shape[0]
    t = min(s, T_HGRN)
    nb = s // t
    nc = t // HG_CHUNK

    def body(hq_ref, hf_ref, hi_ref, hz_ref, lb_ref, g_ref, o_ref, dya_ref, st_ref,
             dh4_ref, dlb_ref, dg_ref, dstate, u_sc, g_sc, stf_sc, stb_sc, dstb_sc):
        h, b = pl.program_id(0), pl.program_id(1)

        @pl.when(b == 0)
        def _():
            dstate[...] = jnp.zeros_like(dstate)
            dlb_ref[...] = jnp.zeros_like(dlb_ref)

        @pl.when((b == 0) & (h == 0))
        def _():
            dg_ref[...] = jnp.zeros_like(dg_ref)

        lower = _chunk_lower_mask(t)
        pos = _chunk_pos(t)
        ghg = g_ref[...]
        for hh in range(HG_HEADS_PER_STEP):
            cols = slice(hh * HEAD_DIM, (hh + 1) * HEAD_DIM)
            hq, hf, hz = hq_ref[:, cols], hf_ref[:, cols], hz_ref[:, cols]
            gt = _hgrn_gates(hq, hf, lb_ref[:, cols], pos)
            vb = _mx(hi_ref[:, cols])
            qi, ki, ko = gt["qi"], gt["ki"], gt["ko"]
            qib, kib, kob = _mx(qi), _mx(ki), _mx(ko)

            o = o_ref[:, cols]
            sz = _sigmoid(hz)
            r = lax.rsqrt(jnp.mean(o * o, axis=-1, keepdims=True) + EPS)
            on = o * r
            dya_t = dya_ref[:, cols]
            don = dya_t * (hz * sz)
            dh4_ref[3, :, cols] = _mx(dya_t * (on * ghg) * (sz + hz * sz * (1.0 - sz)))
            dg_ref[...] += _bcast_rows(jnp.sum(don * on, axis=0, keepdims=True), 8)
            tt = don * ghg
            do = r * (tt - on * jnp.mean(tt * on, axis=-1, keepdims=True))
            dob = _mx(do)

            for c in range(nc):
                sl = slice(c * HG_CHUNK, (c + 1) * HG_CHUNK)
                u_sc[hh, c] = _dot_tn(vb[sl], kob[sl])
                g_sc[hh, c] = _dot_tn(dob[sl], qib[sl])

            st = st_ref[0, hh]
            for c in range(nc):
                stf_sc[hh, c] = st
                stb_sc[hh, c] = _mx(st)
                if c < nc - 1:
                    st = st * gt["dec"][c * HG_CHUNK:c * HG_CHUNK + 1, :] + u_sc[hh, c]

            dst = dstate[hh]
            dd_parts = [None] * nc
            for c in reversed(range(nc)):
                dec = gt["dec"][c * HG_CHUNK:c * HG_CHUNK + 1, :]
                dstb_sc[hh, c] = _mx(dst)
                dd_parts[c] = _bcast_rows(jnp.sum(dst * stf_sc[hh, c], axis=0, keepdims=True) * dec, HG_CHUNK)
                dst = dst * dec + g_sc[hh, c]
            dstate[hh] = dst

            a = jnp.where(lower, _dot_nt(qib, kib), 0.0)
            da = _mx(jnp.where(lower, _dot_nt(dob, vb), 0.0))
            dqi_intra = _dot(da, kib)
            dki = _dot_tn(da, qib)
            dv_intra = _dot_tn(_mx(a), dob)

            dqi_parts, dko_parts, dv_parts = [None] * nc, [None] * nc, [None] * nc
            for c in range(nc):
                sl = slice(c * HG_CHUNK, (c + 1) * HG_CHUNK)
                dv_parts[c] = dv_intra[sl] + _dot_nt(kob[sl], dstb_sc[hh, c])
                dko_parts[c] = _dot(vb[sl], dstb_sc[hh, c])
                dqi_parts[c] = dqi_intra[sl] + _dot(dob[sl], stb_sc[hh, c])
            dqi = jnp.concatenate(dqi_parts, axis=0)
            dko = jnp.concatenate(dko_parts, axis=0)
            dv = jnp.concatenate(dv_parts, axis=0)
            dd = jnp.concatenate(dd_parts, axis=0)

            dq = dqi * gt["eb"]
            dk = dki * gt["enb"] + dko * gt["eo"]
            db = dqi * qi - dki * ki - dko * ko
            dlogf = _rcumsum_chunk(db, pos) + _chunk_total(dko * ko) + dd
            df = dlogf / gt["f"] - dk
            lb, sig, sq = gt["lb"], gt["sig"], gt["sq"]
            dh4_ref[1, :, cols] = _mx(df * (1.0 - lb) * (sig * (1.0 - sig)))
            dh4_ref[0, :, cols] = _mx(dq * (sq + hq * sq * (1.0 - sq)))
            dh4_ref[2, :, cols] = _mx(dv)
            dlb = jnp.sum(df * (1.0 - sig), axis=0, keepdims=True) * (lb * (1.0 - lb))
            dlb_ref[:, cols] += jnp.concatenate([dlb, -dlb], axis=0)

    hw = HG_HEADS_PER_STEP * HEAD_DIM
    hsteps = HEADS // HG_HEADS_PER_STEP

    def seg(k):
        return pl.BlockSpec((t, hw), lambda h, b, k=k: (nb - 1 - b, k * hsteps + h))

    blk = pl.BlockSpec((t, hw), lambda h, b: (nb - 1 - b, h))
    return pl.pallas_call(
        body,
        name="hgrn_bwd",
        grid=(hsteps, nb),
        in_specs=[seg(0), seg(1), seg(2), seg(3),
                  pl.BlockSpec((2, hw), lambda h, b: (0, h)),
                  pl.BlockSpec((1, HEAD_DIM), lambda h, b: (0, 0)),
                  blk, blk,
                  pl.BlockSpec((1, HG_HEADS_PER_STEP, HEAD_DIM, HEAD_DIM), lambda h, b: (nb - 1 - b, h, 0, 0))],
        out_specs=[pl.BlockSpec((4, t, hw), lambda h, b: (0, nb - 1 - b, h)),
                   pl.BlockSpec((2, hw), lambda h, b: (0, h)),
                   pl.BlockSpec((8, HEAD_DIM), lambda h, b: (0, 0))],
        out_shape=[jax.ShapeDtypeStruct((4, s, D_MODEL), MXU_DTYPE),
                   jax.ShapeDtypeStruct((2, D_MODEL), F32),
                   jax.ShapeDtypeStruct((8, HEAD_DIM), F32)],
        scratch_shapes=[pltpu.VMEM((HG_HEADS_PER_STEP, HEAD_DIM, HEAD_DIM), F32)]
        + [pltpu.VMEM((HG_HEADS_PER_STEP, nc, HEAD_DIM, HEAD_DIM), F32)] * 3
        + [pltpu.VMEM((HG_HEADS_PER_STEP, nc, HEAD_DIM, HEAD_DIM), MXU_DTYPE)] * 2,
        compiler_params=_params(),
    )(proj, proj, proj, proj, lb_logits, hg_norm_g, o_all, dya, states)


def _dh_bwd(segs, w_in_p, x, dx2, norm_g, late_slab, late_recv_init, slab_sets):
    s = x.shape[0]
    tm = min(s, TM_ROW)
    seg_ops = [sg if isinstance(sg, tuple) else (sg, None) for sg in segs]
    nseg = len(segs)
    nsteps = s // tm
    n_ops = len(slab_sets)
    late_xyc = ((LATE_DEV >> 2) & 1, (LATE_DEV >> 1) & 1, LATE_DEV & 1)

    def body(*refs):
        seg_refs = refs[:nseg]
        w_ref, x_ref, dx2_ref, g_ref, late_ref, _ = refs[nseg:nseg + 6]
        g_refs = refs[nseg + 6:nseg + 6 + n_ops]
        gx_ref, dng_ref, late_recv_ref = refs[nseg + 6 + n_ops:nseg + 9 + n_ops]
        recv_refs = refs[nseg + 9 + n_ops:nseg + 9 + 2 * n_ops]
        (dp_buf, send_sems, recv_sems, local_sems,
         late_send, late_recvs, late_local) = refs[nseg + 9 + 2 * n_ops:]
        i = pl.program_id(0)
        me = 4 * lax.axis_index("x") + 2 * lax.axis_index("y") + lax.axis_index("c")

        def misc_exchange():
            return _Exchange(g_refs, recv_refs, send_sems, recv_sems, local_sems)

        def late_copy(sender):
            return pltpu.make_async_remote_copy(
                src_ref=late_ref.at[0], dst_ref=late_recv_ref.at[sender], send_sem=late_send,
                recv_sem=late_recvs.at[(sender ^ LATE_DEV) - 1], device_id=late_xyc, device_id_type=MESH)

        def late_own():
            return pltpu.make_async_copy(late_ref.at[0], late_recv_ref.at[LATE_DEV], late_local)

        @pl.when(i == 0)
        def _():
            dng_ref[...] = jnp.zeros_like(dng_ref)
            misc_exchange().start()

        @pl.when((i == 0) & (me != LATE_DEV))
        def _():
            late_copy(me).start()

        @pl.when((i == 0) & (me == LATE_DEV))
        def _():
            late_own().start()

        for k, sref in enumerate(seg_refs):
            dp_buf[:, k * SEG:(k + 1) * SEG] = sref[...]
        dh = _dot_nt(dp_buf[...], w_ref[...])
        xf = x_ref[...]
        r = lax.rsqrt(jnp.mean(xf * xf, axis=-1, keepdims=True) + EPS)
        xh = xf * r
        dng_ref[...] += _bcast_rows(jnp.sum(dh * xh, axis=0, keepdims=True), 8)
        tt = dh * g_ref[...]
        gx_ref[...] = dx2_ref[...] + r * (tt - xh * jnp.mean(tt * xh, axis=-1, keepdims=True))

        @pl.when(i == nsteps - 1)
        def _():
            misc_exchange().wait()

        @pl.when((i == nsteps - 1) & (me != LATE_DEV))
        def _():
            late_copy(me).wait_send()

        @pl.when((i == nsteps - 1) & (me == LATE_DEV))
        def _():
            for k in range(1, N_DEV):
                late_copy(LATE_DEV ^ k).wait_recv()
            late_own().wait()

    rows = pl.BlockSpec((tm, D_MODEL), lambda i: (i, 0))
    return pl.pallas_call(
        body,
        name="dh_bwd",
        grid=(nsteps,),
        in_specs=[pl.BlockSpec((tm, SEG), lambda i: (i, 0)) if j is None
                  else pl.BlockSpec((None, tm, SEG), lambda i, j=j: (j, i, 0)) for _, j in seg_ops] + [
            _resident((D_MODEL, PROJ_W)),
            rows, rows,
            pl.BlockSpec((1, D_MODEL), lambda i: (0, 0)),
            HBM_SPEC, HBM_SPEC,
        ] + [HBM_SPEC] * n_ops,
        out_specs=[rows, pl.BlockSpec((8, D_MODEL), lambda i: (0, 0)), HBM_SPEC] + [HBM_SPEC] * n_ops,
        out_shape=[jax.ShapeDtypeStruct((s, D_MODEL), F32), jax.ShapeDtypeStruct((8, D_MODEL), F32),
                   jax.ShapeDtypeStruct(late_recv_init.shape, late_recv_init.dtype)]
        + [jax.ShapeDtypeStruct(a.shape, a.dtype) for a in slab_sets],
        input_output_aliases={nseg + 5: 2},
        scratch_shapes=[pltpu.VMEM((tm, PROJ_W), MXU_DTYPE)] + _Exchange.semaphores(n_ops)
        + [pltpu.SemaphoreType.DMA, pltpu.SemaphoreType.DMA((N_DEV - 1,)), pltpu.SemaphoreType.DMA],
        compiler_params=_params(),
    )(*[a for a, _ in seg_ops], w_in_p, x, dx2, norm_g, late_slab, late_recv_init, *slab_sets)


def _matmul_tn(a, b, name, out_dtype=F32):
    s, m = a.shape
    stacked = b.ndim == 3
    n = b.shape[0] * b.shape[2] if stacked else b.shape[1]
    ts = min(s, TS_TN)
    tn = min(n, SEG)
    nk = s // ts
    if stacked:
        b_spec = pl.BlockSpec((None, ts, tn), lambda j, k: (j, k, 0))
    else:
        b_spec = pl.BlockSpec((ts, tn), lambda j, k: (k, j))

    def body(a_ref, b_ref, o_ref, acc):
        k = pl.program_id(1)
        part = _dot_tn(a_ref[...], b_ref[...])

        @pl.when(k == 0)
        def _():
            acc[...] = part

        @pl.when(k > 0)
        def _():
            acc[...] += part

        @pl.when(k == nk - 1)
        def _():
            o_ref[...] = acc[...].astype(out_dtype)

    return pl.pallas_call(
        body,
        name=name,
        grid=(n // tn, nk),
        in_specs=[pl.BlockSpec((ts, m), lambda j, k: (k, 0)), b_spec],
        out_specs=pl.BlockSpec((m, tn), lambda j, k: (0, j)),
        out_shape=jax.ShapeDtypeStruct((m, n), out_dtype),
        scratch_shapes=[pltpu.VMEM((m, tn), F32)],
        compiler_params=_params(),
    )(a, b)


def _w_in_pieces():
    per = IN_COLS // N_DEV
    pad_at = SMALL_SEG * SEG + Q_LORA + KV_LORA + QK_ROPE
    pieces = []
    for j in range(N_DEV):
        u0, u1 = j * per, (j + 1) * per
        cuts = [u0] + ([pad_at] if u0 < pad_at < u1 else []) + [u1]
        for a, b in zip(cuts[:-1], cuts[1:]):
            pieces.append((j, a - u0, b - u0, a if a < pad_at else a + PROJ_W - IN_COLS))
    return pad_at, pieces


def _assemble_w_in(gathered):
    tr = TM_ROW
    pad_at, pieces = _w_in_pieces()

    def body(in_ref, out_ref):
        out_ref[:, pad_at:pad_at + PROJ_W - IN_COLS] = jnp.zeros((tr, PROJ_W - IN_COLS), gathered.dtype)
        for j, a, b, p0 in pieces:
            out_ref[:, p0:p0 + b - a] = in_ref[j, :, a:b]

    return pl.pallas_call(
        body,
        name="assemble_w_in",
        grid=(D_MODEL // tr,),
        in_specs=[pl.BlockSpec((N_DEV, tr, PACK_COLS), lambda i: (0, i, 0))],
        out_specs=pl.BlockSpec((tr, PROJ_W), lambda i: (i, 0)),
        out_shape=jax.ShapeDtypeStruct((D_MODEL, PROJ_W), gathered.dtype),
        compiler_params=_params(),
    )(gathered)


def _scatter_dw_in(dw_segs, devs, name):
    tr = TM_ROW
    _, pieces = _w_in_pieces()
    per = IN_COLS // N_DEV
    seg_ids = sorted(dw_segs)
    nseg = len(seg_ids)
    seg_ops = [dw_segs[k] if isinstance(dw_segs[k], tuple) else (dw_segs[k], 0) for k in seg_ids]

    def body(*refs):
        out_ref, buf = refs[nseg:]
        for k in range(PROJ_W // SEG):
            if k in seg_ids:
                buf[:, k * SEG:(k + 1) * SEG] = refs[seg_ids.index(k)][...]
            else:
                buf[:, k * SEG:(k + 1) * SEG] = jnp.zeros((tr, SEG), F32)
        for slot, dev in enumerate(devs):
            out_ref[slot, :, per:] = jnp.zeros((tr, PACK_COLS - per), TRANSPORT_DTYPE)
            for j, a, b, p0 in pieces:
                if j == dev:
                    out_ref[slot, :, a:b] = buf[:, p0:p0 + b - a].astype(TRANSPORT_DTYPE)

    return pl.pallas_call(
        body,
        name=name,
        grid=(D_MODEL // tr,),
        in_specs=[pl.BlockSpec((tr, SEG), lambda i, j=j: (i, j)) for _, j in seg_ops],
        out_specs=pl.BlockSpec((len(devs), tr, PACK_COLS), lambda i: (0, i, 0)),
        out_shape=jax.ShapeDtypeStruct((len(devs), D_MODEL, PACK_COLS), TRANSPORT_DTYPE),
        scratch_shapes=[pltpu.VMEM((tr, PROJ_W), F32)],
        compiler_params=_params(),
    )(*[a for a, _ in seg_ops])


def _rope_tables(s):
    inv = ROPE_THETA ** (-jnp.arange(0, QK_ROPE, 2, dtype=F32) / QK_ROPE)
    ang = jnp.arange(s, dtype=F32)[:, None] * inv[None, :]
    cos, sin = jnp.cos(ang), jnp.sin(ang)
    z32 = jnp.zeros_like(cos)
    z64 = jnp.zeros((s, HEAD_DIM - QK_ROPE), F32)
    cos_t = jnp.concatenate([cos, cos, z64], axis=1)
    sin_a = jnp.concatenate([-sin, z32, z64], axis=1)
    sin_b = jnp.concatenate([z32, sin, z64], axis=1)
    return cos_t, sin_a, sin_b


def _small_rows(b_gate, lb_logits, hg_norm_g, q_a_g, kv_a_g, final_norm_g, loss):
    row5 = jnp.concatenate([hg_norm_g.reshape(1, -1), q_a_g.reshape(1, -1), kv_a_g.reshape(1, -1), loss.reshape(1, 1),
                            jnp.zeros((1, PACK_COLS - LOSS_LANE - 1), F32)], axis=1)
    zero_row = jnp.zeros((1, PACK_COLS), F32)
    return jnp.concatenate([zero_row, b_gate.reshape(2, -1), lb_logits, row5, final_norm_g.reshape(1, -1), zero_row],
                           axis=0)


def _adamw_small(recv_small, recv_norm_g, weights, m, v):
    n = len(weights)

    def body(rs_ref, rn_ref, *refs):
        w_refs, m_refs, v_refs = refs[:n], refs[n:2 * n], refs[2 * n:3 * n]
        loss_ref = refs[3 * n]
        outs = refs[3 * n + 1:]
        gs, gn = rs_ref[0], rn_ref[0]
        for i in range(1, N_DEV):
            gs = gs + rs_ref[i]
            gn = gn + rn_ref[i]
        loss_ref[...] = gs[5:6, LOSS_LANE:LOSS_LANE + HEAD_DIM]
        grads = [gn[0:1], jnp.concatenate([gs[1:2], gs[2:3]], axis=1), gs[3:5],
                 gs[5:6, :HEAD_DIM], gs[5:6, HEAD_DIM:HEAD_DIM + Q_LORA], gs[5:6, HEAD_DIM + Q_LORA:LOSS_LANE], gs[6:7]]
        for k, g in enumerate(grads):
            m_new = ADAM_B1 * m_refs[k][...] + (1.0 - ADAM_B1) * g
            v_new = ADAM_B2 * v_refs[k][...] + (1.0 - ADAM_B2) * (g * g)
            m_hat = m_new / (1.0 - ADAM_B1 ** ADAM_STEP)
            v_hat = v_new / (1.0 - ADAM_B2 ** ADAM_STEP)
            outs[k][...] = g
            outs[n + k][...] = -ADAM_LR * (m_hat / (jnp.sqrt(v_hat) + ADAM_EPS) + ADAM_WD * w_refs[k][...])
            outs[2 * n + k][...] = m_new
            outs[3 * n + k][...] = v_new

    shapes = [jax.ShapeDtypeStruct(w.shape, F32) for w in weights]
    res = pl.pallas_call(
        body,
        name="adamw_small",
        out_shape=[jax.ShapeDtypeStruct((1, HEAD_DIM), F32)] + shapes * 4,
        compiler_params=_params(),
    )(recv_small, recv_norm_g, *weights, *m, *v)
    return res[0], [res[1 + k * n:1 + (k + 1) * n] for k in range(4)]


def _weight_shard_buffers(w_in, w_uq, w_ukv, w_pa, w_pb, w_out):
    w_in_pad = jnp.pad(w_in.reshape(D_MODEL, -1), ((0, 0), (0, PACK_COLS - IN_COLS // N_DEV)))
    parts = [a.reshape(-1, PACK_COLS) for a in (w_pa, w_pb, w_out, w_uq, w_ukv)]
    others = jnp.concatenate(parts + [jnp.zeros((ROWS_OTHER - ROWS_OTHER_USED, PACK_COLS), F32)], axis=0)
    return w_in_pad.astype(MXU_DTYPE), others.astype(MXU_DTYPE)


def _other_weights(gathered):
    r0 = 0
    mats = []
    for _ in range(3):
        mats.append(gathered[:, r0:r0 + ROWS_W_PROJ].reshape(D_MODEL, D_MODEL))
        r0 += ROWS_W_PROJ
    w_uq = gathered[:, r0:r0 + ROWS_W_UQ].reshape(N_DEV, Q_LORA, QK_DIM).transpose(1, 0, 2)
    w_uq_p = jnp.concatenate([w_uq, jnp.zeros((Q_LORA, HEADS, HEAD_PAD - QK_DIM), w_uq.dtype)], axis=2)
    w_uq_p = w_uq_p.reshape(Q_LORA, HEADS * HEAD_PAD)
    r0 += ROWS_W_UQ
    w_ukv = gathered[:, r0:r0 + ROWS_W_UKV].reshape(N_DEV, KV_LORA, 2 * HEAD_DIM).transpose(1, 0, 2)
    w_kn = w_ukv[:, :, :HEAD_DIM].reshape(KV_LORA, D_MODEL)
    w_v = w_ukv[:, :, HEAD_DIM:].reshape(KV_LORA, D_MODEL)
    return w_uq_p, w_kn, w_v, mats[0], mats[1], mats[2]


def _late_slab_sets(dw_uq_p, dw_kn, dw_v, small_rows):
    uq = dw_uq_p.reshape(Q_LORA, HEADS, HEAD_PAD).transpose(1, 0, 2)
    ukv = jnp.concatenate([dw_kn.reshape(KV_LORA, HEADS, HEAD_DIM),
                           dw_v.reshape(KV_LORA, HEADS, HEAD_DIM)], axis=2).transpose(1, 0, 2)
    return [uq, ukv, jnp.broadcast_to(small_rows[None], (N_DEV,) + small_rows.shape)]


def _step_gradients(x, target, norm_g, b_gate, lb_logits, hg_norm_g, q_a_g, kv_a_g, final_g,
                    w_in_p, other_shard):
    s = x.shape[0]
    cos, sin_a, sin_b = _rope_tables(s)
    proj, h, gathered = _inproj(x, norm_g, w_in_p, other_shard)
    w_uq_p, w_kn, w_v, w_pa, w_pb, w_out = _other_weights(gathered)
    o_all, ya, states = _hgrn_fwd(proj, lb_logits, hg_norm_g)
    q_all, k_all, v_all, cqn, ckvn = _mla_prep(proj, q_a_g, kv_a_g, w_uq_p, w_kn, w_v, cos, sin_a, sin_b)
    ao, yb, mblk, p_all = _flash_fwd(q_all, k_all, v_all, proj)
    (dx2, dya, dao, d_tail, merged_b, dpa_b, dpb_b, dx2_b,
     loss_acc, dfg_acc, dbg_acc) = _merge_fwd_bwd(x, target, ya, yb, ao, proj, b_gate, final_g, w_pa, w_pb, w_out)
    d_head, dlb, dhg_acc = _hgrn_bwd(proj, lb_logits, hg_norm_g, o_all, dya, states)

    dw_head = _matmul_tn(h, d_head, "dw_in_head")
    dw_tail = _matmul_tn(h, d_tail, "dw_in_tail")
    dw_early = {k: (dw_head, k) for k in range(4)}
    dw_early.update({MZ_SEG + k: (dw_tail, k) for k in range(3)})
    mats = [_matmul_tn(a, b, name, TRANSPORT_DTYPE).reshape(N_DEV, ROWS_W_PROJ, PACK_COLS)
            for a, b, name in ((ya, dpa_b, "dw_pa"), (yb, dpb_b, "dw_pb"), (merged_b, dx2_b, "dw_out"))]
    early_slabs = [_scatter_dw_in(dw_early, list(range(N_DEV)), "scatter_dw_in")] + mats
    dq_all, dk_all, dv_all, (recv_in, recv_pa, recv_pb, recv_out) = _flash_bwd(
        q_all, k_all, v_all, dao, ao, mblk, p_all, early_slabs)

    dsmall, dqf_b, dkn_b, dv_b, dgq_acc, dgk_acc = _mla_prep_bwd(
        dq_all, dk_all, dv_all, proj, q_a_g, kv_a_g, w_uq_p, w_kn, w_v, cos, sin_a, sin_b)
    late_slab = _scatter_dw_in({SMALL_SEG: _matmul_tn(h, dsmall, "dw_in_%d" % SMALL_SEG)}, [LATE_DEV],
                               "scatter_dw_in_late")
    segs = [(d_head, k) for k in range(4)] + [dsmall] + [(d_tail, k) for k in range(3)]
    return dict(
        segs=segs, h=h, dx2=dx2, late_slab=late_slab,
        dw_uq_p=_matmul_tn(cqn, dqf_b, "dw_uq"), dw_kn=_matmul_tn(ckvn, dkn_b, "dw_kn"),
        dw_v=_matmul_tn(ckvn, dv_b, "dw_v"),
        small=dict(b_gate=dbg_acc[0:1], lb_logits=dlb, hg_norm_g=dhg_acc[0:1], q_a_g=dgq_acc[0:1],
                   kv_a_g=dgk_acc[0:1], final_norm_g=dfg_acc[0], loss=loss_acc[0, 0]),
        recv=dict(w_in=[recv_in], w_pa=[recv_pa], w_pb=[recv_pb], w_out=[recv_out]),
    )


def kernel(x, norm_g, w_in, b_gate, lb_logits, hg_norm_g, q_a_g, w_uq, kv_a_g, w_ukv, w_proj_a, w_proj_b, w_out, final_norm_g, loss_target, m_norm_g, m_w_in, m_b_gate, m_lb_logits, m_hg_norm_g, m_q_a_g, m_w_uq, m_kv_a_g, m_w_ukv, m_w_proj_a, m_w_proj_b, m_w_out, m_final_norm_g, v_norm_g, v_w_in, v_b_gate, v_lb_logits, v_hg_norm_g, v_q_a_g, v_w_uq, v_kv_a_g, v_w_ukv, v_w_proj_a, v_w_proj_b, v_w_out, v_final_norm_g):
    xs = x[0]
    w_in_shard, other_shard = _weight_shard_buffers(w_in, w_uq, w_ukv, w_proj_a, w_proj_b, w_out)
    w_in_p = _assemble_w_in(_all_gather_packed(w_in_shard))
    g = _step_gradients(xs, loss_target[0], norm_g, b_gate, lb_logits, hg_norm_g, q_a_g, kv_a_g,
                        final_norm_g.reshape(1, -1), w_in_p, other_shard)
    sm = g["small"]
    late_sets = _late_slab_sets(g["dw_uq_p"], g["dw_kn"], g["dw_v"],
                                _small_rows(sm["b_gate"], sm["lb_logits"], sm["hg_norm_g"], sm["q_a_g"],
                                            sm["kv_a_g"], sm["final_norm_g"], sm["loss"]))
    late_recv_init = jnp.zeros((N_DEV, D_MODEL, PACK_COLS), TRANSPORT_DTYPE)
    grad_x, dng_acc, recv_late, recv_uq, recv_ukv, recv_small = _dh_bwd(
        g["segs"], w_in_p, xs, g["dx2"], norm_g, g["late_slab"], late_recv_init, late_sets)
    recv_ng = _exchange_rows(jnp.broadcast_to(dng_acc[None], (N_DEV, 8, D_MODEL)))

    recv = g["recv"]
    big = dict(
        w_in=_sum_adamw(recv["w_in"] + [recv_late], w_in[0], m_w_in[0], v_w_in[0], "adamw_w_in"),
        w_proj_a=_sum_adamw(recv["w_pa"], w_proj_a[0], m_w_proj_a[0], v_w_proj_a[0], "adamw_w_pa"),
        w_proj_b=_sum_adamw(recv["w_pb"], w_proj_b[0], m_w_proj_b[0], v_w_proj_b[0], "adamw_w_pb"),
        w_out=_sum_adamw(recv["w_out"], w_out[0], m_w_out[0], v_w_out[0], "adamw_w_out"),
        w_uq=_sum_adamw([recv_uq], w_uq[0], m_w_uq[0], v_w_uq[0], "adamw_w_uq"),
        w_ukv=_sum_adamw([recv_ukv], w_ukv[0], m_w_ukv[0], v_w_ukv[0], "adamw_w_ukv"),
    )
    small_names = ["norm_g", "b_gate", "lb_logits", "hg_norm_g", "q_a_g", "kv_a_g", "final_norm_g"]
    loss_row, small = _adamw_small(
        recv_small, recv_ng,
        (norm_g, b_gate, lb_logits, hg_norm_g, q_a_g, kv_a_g, final_norm_g.reshape(1, -1)),
        (m_norm_g, m_b_gate, m_lb_logits, m_hg_norm_g, m_q_a_g, m_kv_a_g, m_final_norm_g.reshape(1, -1)),
        (v_norm_g, v_b_gate, v_lb_logits, v_hg_norm_g, v_q_a_g, v_kv_a_g, v_final_norm_g.reshape(1, -1)))
    names = ["norm_g", "w_in", "b_gate", "lb_logits", "hg_norm_g", "q_a_g", "w_uq", "kv_a_g", "w_ukv",
             "w_proj_a", "w_proj_b", "w_out", "final_norm_g"]
    results = []
    for kind in range(4):
        by_name = {n: big[n][kind][None] for n in big}
        by_name.update(zip(small_names, small[kind]))
        by_name["final_norm_g"] = by_name["final_norm_g"].reshape(-1)
        results += [by_name[n] for n in names]
    return (loss_row[0, 0], grad_x[None], *results)
```

```python
import jax
import jax.numpy as jnp
from jax import lax
from jax.experimental import pallas as pl
from jax.experimental.pallas import tpu as pltpu

D_MODEL = 1024
HEADS = 8
HEAD_DIM = 128
HG_CHUNK = 32
Q_LORA = 384
KV_LORA = 256
QK_ROPE = 64
QK_DIM = 192
ROPE_THETA = 10000.0
EPS = 1e-6
IN_COLS = 7872
ADAM_LR = 0.001
ADAM_B1 = 0.9
ADAM_B2 = 0.999
ADAM_EPS = 1e-08
ADAM_WD = 0.01
ADAM_STEP = 10

N_DEV = 8
SEG = 1024
PROJ_W = 8 * SEG
SMALL_SEG = 4
MZ_SEG = 5
GL_SEG = 6
HEAD_PAD = 256
PACK_COLS = 1024
LOSS_LANE = HEAD_DIM + Q_LORA + KV_LORA
ROWS_W_UQ = 72
ROWS_W_UKV = 64
ROWS_W_PROJ = 128
ROWS_OTHER_USED = 3 * ROWS_W_PROJ + ROWS_W_UQ + ROWS_W_UKV
ROWS_OTHER = 528
LATE_DEV = (SMALL_SEG * SEG) // (IN_COLS // N_DEV)
assert (SMALL_SEG * SEG + Q_LORA + KV_LORA + QK_ROPE - 1) // (IN_COLS // N_DEV) == LATE_DEV

QK_SCALE = QK_DIM ** -0.5
LOG2E = 1.4426950408889634
LN2 = 0.6931471805599453
Q_PRESCALE = QK_SCALE * LOG2E

MXU_DTYPE = jnp.bfloat16
TRANSPORT_DTYPE = jnp.bfloat16
VMEM_LIMIT = 48 * 1024 * 1024
VMEM_LIMIT_BIG = 60 * 1024 * 1024

T_HGRN = 512
HG_HEADS_PER_STEP = 4
TM_ROW = 256
TM_MLA = 512
T_ATT = 1024
T_ATT_BWD = T_ATT
ATT_SUB = 4
P_SLOTS = 4
LSE_LANE = 127
ATT_SUB_BWD = 2
TS_TN = 2048
TR_ADAM = 256

F32 = jnp.float32
MESH = pl.DeviceIdType.MESH


def _dot(a, b):
    return jnp.dot(a, b, preferred_element_type=F32)


def _dot_nt(a, b):
    return lax.dot_general(a, b, (((1,), (1,)), ((), ())), preferred_element_type=F32)


def _dot_tn(a, b):
    return lax.dot_general(a, b, (((0,), (0,)), ((), ())), preferred_element_type=F32)


def _mx(a):
    return a.astype(MXU_DTYPE)


def _sigmoid(x):
    return 1.0 / (1.0 + jnp.exp(-x))


def _params(vmem=VMEM_LIMIT, **kw):
    return pltpu.CompilerParams(vmem_limit_bytes=vmem, **kw)


def _bcast_rows(row, n):
    return jnp.broadcast_to(row, (n, row.shape[-1]))


def _resident(shape):
    return pl.BlockSpec(shape, lambda *_: (0, 0), pipeline_mode=pl.Buffered(1))


HBM_SPEC = pl.BlockSpec(memory_space=pltpu.HBM)


def _all_gather_packed(shard):
    rows, cols = shard.shape

    def body(x_ref, out_ref, send_sems, recv_sems, local_sem):
        x, y, c = lax.axis_index("x"), lax.axis_index("y"), lax.axis_index("c")
        me, sibling = (x, y, c), (x, y, 1 - c)
        chips = [(1 - x, y), (x, 1 - y), (1 - x, 1 - y)]

        def slot(px, py, pc):
            return out_ref.at[4 * px + 2 * py + pc]

        def copy(k, block, to, src=None):
            return pltpu.make_async_remote_copy(
                src_ref=slot(*block) if src is None else src,
                dst_ref=slot(*block),
                send_sem=send_sems.at[k],
                recv_sem=recv_sems.at[k],
                device_id=to,
                device_id_type=MESH,
            )

        mine = pltpu.make_async_copy(x_ref, slot(*me), local_sem)
        mine.start()
        first = [copy(0, me, sibling, src=x_ref)]
        first += [copy(1 + j, me, (*chip, c), src=x_ref) for j, chip in enumerate(chips)]
        for cp in first:
            cp.start()
        passed = [copy(4 + j, (*chip, c), sibling) for j, chip in enumerate(chips)]
        for j, chip in enumerate(chips):
            copy(1 + j, (*chip, c), me).wait_recv()
            passed[j].start()
        copy(0, sibling, me).wait_recv()
        for j, chip in enumerate(chips):
            copy(4 + j, (*chip, 1 - c), me).wait_recv()
        for cp in first + passed:
            cp.wait_send()
        mine.wait()

    return pl.pallas_call(
        body,
        name="ag_weights",
        out_shape=jax.ShapeDtypeStruct((N_DEV, rows, cols), shard.dtype),
        in_specs=[HBM_SPEC],
        out_specs=HBM_SPEC,
        scratch_shapes=[
            pltpu.SemaphoreType.DMA((7,)),
            pltpu.SemaphoreType.DMA((7,)),
            pltpu.SemaphoreType.DMA,
        ],
    )(shard)


class _Exchange:
    def __init__(self, g_refs, recv_refs, send_sems, recv_sems, local_sems, gather=False):
        x, y, c = lax.axis_index("x"), lax.axis_index("y"), lax.axis_index("c")
        me = 4 * x + 2 * y + c
        n_ops = len(g_refs)

        def source(i, dest):
            return g_refs[i] if gather else g_refs[i].at[dest]

        def copy(i, k, landing):
            px, py, pc = x ^ ((k >> 2) & 1), y ^ ((k >> 1) & 1), c ^ (k & 1)
            peer = 4 * px + 2 * py + pc
            return pltpu.make_async_remote_copy(
                src_ref=source(i, peer),
                dst_ref=recv_refs[i].at[peer if landing else me],
                send_sem=send_sems.at[i * (N_DEV - 1) + k - 1],
                recv_sem=recv_sems.at[i * (N_DEV - 1) + k - 1],
                device_id=(px, py, pc),
                device_id_type=MESH,
            )

        pairs = [(i, k) for i in range(n_ops) for k in range(1, N_DEV)]
        self.mine = lambda: [pltpu.make_async_copy(source(i, me), recv_refs[i].at[me], local_sems.at[i])
                             for i in range(n_ops)]
        self.sends = lambda: [copy(i, k, False) for i, k in pairs]
        self.landings = lambda: [copy(i, k, True) for i, k in pairs]

    def start(self):
        for cp in self.mine() + self.sends():
            cp.start()

    def wait(self):
        for cp in self.landings():
            cp.wait_recv()
        for cp in self.sends():
            cp.wait_send()
        for cp in self.mine():
            cp.wait()

    @staticmethod
    def semaphores(n_ops):
        return [pltpu.SemaphoreType.DMA((n_ops * (N_DEV - 1),)),
                pltpu.SemaphoreType.DMA((n_ops * (N_DEV - 1),)),
                pltpu.SemaphoreType.DMA((n_ops,))]


def _exchange_rows(slabs):
    def body(g_ref, recv_ref, send_sems, recv_sems, local_sems):
        exchange = _Exchange([g_ref], [recv_ref], send_sems, recv_sems, local_sems)
        exchange.start()
        exchange.wait()

    return pl.pallas_call(
        body,
        name="exchange_rows",
        out_shape=jax.ShapeDtypeStruct(slabs.shape, slabs.dtype),
        in_specs=[HBM_SPEC],
        out_specs=HBM_SPEC,
        scratch_shapes=_Exchange.semaphores(1),
    )(slabs)


def _sum_adamw(recvs, w, m, v, name):
    rows, cols = w.shape
    tr = TR_ADAM if rows % TR_ADAM == 0 else rows
    n_recv = len(recvs)

    def body(*refs):
        w_ref, m_ref, v_ref, g_out, d_out, m_out, v_out = refs[n_recv:]
        g = None
        for r_ref in refs[:n_recv]:
            for i in range(N_DEV):
                part = r_ref[i].astype(F32)
                g = part if g is None else g + part
        g = g[:, :cols]
        m_new = ADAM_B1 * m_ref[...] + (1.0 - ADAM_B1) * g
        v_new = ADAM_B2 * v_ref[...] + (1.0 - ADAM_B2) * (g * g)
        m_hat = m_new / (1.0 - ADAM_B1 ** ADAM_STEP)
        v_hat = v_new / (1.0 - ADAM_B2 ** ADAM_STEP)
        g_out[...] = g
        d_out[...] = -ADAM_LR * (m_hat / (jnp.sqrt(v_hat) + ADAM_EPS) + ADAM_WD * w_ref[...])
        m_out[...] = m_new
        v_out[...] = v_new

    row_spec = pl.BlockSpec((tr, cols), lambda i: (i, 0))
    shape = jax.ShapeDtypeStruct((rows, cols), F32)
    return pl.pallas_call(
        body,
        name=name,
        grid=(rows // tr,),
        in_specs=[pl.BlockSpec((N_DEV, tr, recvs[0].shape[2]), lambda i: (0, i, 0))] * n_recv + [row_spec] * 3,
        out_specs=[row_spec] * 4,
        out_shape=[shape] * 4,
        compiler_params=_params(),
    )(*recvs, w, m, v)


def _inproj(x, norm_g, w_in_p, other_shard):
    s = x.shape[0]
    tm = min(s, TM_ROW)
    nsteps = s // tm

    def body(x_ref, g_ref, w_ref, shard_ref, proj_ref, h_ref, gathered_ref, send_sems, recv_sems, local_sems):
        i = pl.program_id(0)

        def all_gather():
            return _Exchange([shard_ref], [gathered_ref], send_sems, recv_sems, local_sems, gather=True)

        @pl.when(i == 0)
        def _():
            all_gather().start()

        xf = x_ref[...]
        r = lax.rsqrt(jnp.mean(xf * xf, axis=-1, keepdims=True) + EPS)
        h = _mx(xf * r * g_ref[...])
        h_ref[...] = h
        for j in range(PROJ_W // SEG):
            cols = slice(j * SEG, (j + 1) * SEG)
            proj_ref[:, cols] = _dot(h, w_ref[:, cols])

        @pl.when(i == nsteps - 1)
        def _():
            all_gather().wait()

    return pl.pallas_call(
        body,
        name="inproj",
        grid=(nsteps,),
        in_specs=[
            pl.BlockSpec((tm, D_MODEL), lambda i: (i, 0)),
            pl.BlockSpec((1, D_MODEL), lambda i: (0, 0)),
            _resident((D_MODEL, PROJ_W)),
            HBM_SPEC,
        ],
        out_specs=[
            pl.BlockSpec((tm, PROJ_W), lambda i: (i, 0)),
            pl.BlockSpec((tm, D_MODEL), lambda i: (i, 0)),
            HBM_SPEC,
        ],
        out_shape=[
            jax.ShapeDtypeStruct((s, PROJ_W), F32),
            jax.ShapeDtypeStruct((s, D_MODEL), MXU_DTYPE),
            jax.ShapeDtypeStruct((N_DEV,) + other_shard.shape, other_shard.dtype),
        ],
        scratch_shapes=_Exchange.semaphores(1),
        compiler_params=_params(),
    )(x, norm_g, w_in_p, other_shard)


def _chunk_lower_mask(t):
    row = lax.broadcasted_iota(jnp.int32, (t, t), 0)
    col = lax.broadcasted_iota(jnp.int32, (t, t), 1)
    return ((row // HG_CHUNK) == (col // HG_CHUNK)) & (col <= row)


def _chunk_pos(t):
    return lax.broadcasted_iota(jnp.int32, (t, HEAD_DIM), 0) & (HG_CHUNK - 1)


def _cumsum_chunk(x, pos):
    sh = 1
    while sh < HG_CHUNK:
        x = x + jnp.where(pos >= sh, pltpu.roll(x, sh, 0), 0.0)
        sh *= 2
    return x


def _rcumsum_chunk(x, pos):
    t = x.shape[0]
    sh = 1
    while sh < HG_CHUNK:
        x = x + jnp.where(pos < HG_CHUNK - sh, pltpu.roll(x, t - sh, 0), 0.0)
        sh *= 2
    return x


def _chunk_total(x):
    t, w = x.shape
    tot = jnp.sum(x.reshape(t // HG_CHUNK, HG_CHUNK, w), axis=1, keepdims=True)
    return jnp.broadcast_to(tot, (t // HG_CHUNK, HG_CHUNK, w)).reshape(t, w)


def _hgrn_gates(hq, hf, lb_logits, pos):
    lb = _sigmoid(lb_logits[0:1, :] - lb_logits[1:2, :])
    sig = _sigmoid(hf)
    f = lb + (1.0 - lb) * sig
    sq = _sigmoid(hq)
    q = hq * sq
    k = 1.0 - f
    logf = jnp.log(f)
    bcum = _cumsum_chunk(logf, pos)
    blast = _chunk_total(logf)
    eb = jnp.exp(bcum)
    enb = jnp.exp(-bcum)
    eo = jnp.exp(blast - bcum)
    return dict(lb=lb, sig=sig, f=f, sq=sq, q=q, k=k, eb=eb, enb=enb, eo=eo,
                qi=q * eb, ki=k * enb, ko=k * eo, dec=jnp.exp(blast))


def _hgrn_fwd(proj, lb_logits, hg_norm_g):
    s = proj.shape[0]
    t = min(s, T_HGRN)
    nb = s // t
    nc = t // HG_CHUNK
    hw = HG_HEADS_PER_STEP * HEAD_DIM

    def body(hq_ref, hf_ref, hi_ref, hz_ref, lb_ref, g_ref, o_ref, ya_ref, st_ref, state, u_sc, stb_sc):
        b = pl.program_id(1)

        @pl.when(b == 0)
        def _():
            state[...] = jnp.zeros_like(state)

        lower = _chunk_lower_mask(t)
        pos = _chunk_pos(t)
        for hh in range(HG_HEADS_PER_STEP):
            cols = slice(hh * HEAD_DIM, (hh + 1) * HEAD_DIM)
            st = state[hh]
            st_ref[0, hh] = st
            gt = _hgrn_gates(hq_ref[:, cols], hf_ref[:, cols], lb_ref[:, cols], pos)
            vb = _mx(hi_ref[:, cols])
            qib, kib, kob = _mx(gt["qi"]), _mx(gt["ki"]), _mx(gt["ko"])
            a = jnp.where(lower, _dot_nt(qib, kib), 0.0)
            o_intra = _dot(_mx(a), vb)
            for c in range(nc):
                sl = slice(c * HG_CHUNK, (c + 1) * HG_CHUNK)
                u_sc[hh, c] = _dot_tn(vb[sl], kob[sl])
            for c in range(nc):
                stb_sc[hh, c] = _mx(st)
                st = st * gt["dec"][c * HG_CHUNK:c * HG_CHUNK + 1, :] + u_sc[hh, c]
            state[hh] = st
            outs = []
            for c in range(nc):
                sl = slice(c * HG_CHUNK, (c + 1) * HG_CHUNK)
                outs.append(o_intra[sl] + _dot_nt(qib[sl], stb_sc[hh, c]))
            o = jnp.concatenate(outs, axis=0)
            o_ref[:, cols] = o
            r = lax.rsqrt(jnp.mean(o * o, axis=-1, keepdims=True) + EPS)
            hz = hz_ref[:, cols]
            ya_ref[:, cols] = _mx((o * r * g_ref[...]) * (hz * _sigmoid(hz)))

    hsteps = HEADS // HG_HEADS_PER_STEP

    def seg(k):
        return pl.BlockSpec((t, hw), lambda h, b, k=k: (b, k * hsteps + h))

    return pl.pallas_call(
        body,
        name="hgrn_fwd",
        grid=(hsteps, nb),
        in_specs=[seg(0), seg(1), seg(2), seg(3),
                  pl.BlockSpec((2, hw), lambda h, b: (0, h)),
                  pl.BlockSpec((1, HEAD_DIM), lambda h, b: (0, 0))],
        out_specs=[
            pl.BlockSpec((t, hw), lambda h, b: (b, h)),
            pl.BlockSpec((t, hw), lambda h, b: (b, h)),
            pl.BlockSpec((1, HG_HEADS_PER_STEP, HEAD_DIM, HEAD_DIM), lambda h, b: (b, h, 0, 0)),
        ],
        out_shape=[
            jax.ShapeDtypeStruct((s, D_MODEL), F32),
            jax.ShapeDtypeStruct((s, D_MODEL), MXU_DTYPE),
            jax.ShapeDtypeStruct((nb, HEADS, HEAD_DIM, HEAD_DIM), F32),
        ],
        scratch_shapes=[pltpu.VMEM((HG_HEADS_PER_STEP, HEAD_DIM, HEAD_DIM), F32),
                        pltpu.VMEM((HG_HEADS_PER_STEP, nc, HEAD_DIM, HEAD_DIM), F32),
                        pltpu.VMEM((HG_HEADS_PER_STEP, nc, HEAD_DIM, HEAD_DIM), MXU_DTYPE)],
        compiler_params=_params(),
    )(proj, proj, proj, proj, lb_logits, hg_norm_g)


def _rope(x, cos, sin_a, sin_b):
    return x * cos + pltpu.roll(x, 96, 1) * sin_a + pltpu.roll(x, 32, 1) * sin_b


def _rope_t(d, cos, sin_a, sin_b):
    return d * cos + pltpu.roll(d * sin_a, 32, 1) + pltpu.roll(d * sin_b, 96, 1)


def _mla_prep(proj, q_a_g, kv_a_g, w_uq_p, w_kn, w_v, cos, sin_a, sin_b):
    s = proj.shape[0]
    tm = min(s, TM_MLA)

    def body(sm_ref, gq_ref, gk_ref, wq_ref, wkn_ref, wv_ref, cos_ref, sa_ref, sb_ref,
             q_ref, k_ref, v_ref, cqn_ref, ckvn_ref):
        small = sm_ref[...]
        cq = small[:, :Q_LORA]
        ckv = small[:, Q_LORA:Q_LORA + KV_LORA]
        krp = small[:, Q_LORA + KV_LORA:Q_LORA + KV_LORA + HEAD_DIM]
        rq = lax.rsqrt(jnp.mean(cq * cq, axis=-1, keepdims=True) + EPS)
        rk = lax.rsqrt(jnp.mean(ckv * ckv, axis=-1, keepdims=True) + EPS)
        cqn = _mx(cq * rq * gq_ref[...])
        ckvn = _mx(ckv * rk * gk_ref[...])
        cqn_ref[...] = cqn
        ckvn_ref[...] = ckvn
        q = _dot(cqn, wq_ref[...]) * Q_PRESCALE
        kn = _dot(ckvn, wkn_ref[...])
        v = _dot(ckvn, wv_ref[...])
        cos_t, sa, sb = cos_ref[...], sa_ref[...], sb_ref[...]
        kpe = _mx(_rope(krp, cos_t, sa, sb))
        ones_col = (lax.broadcasted_iota(jnp.int32, (tm, HEAD_DIM), 1) == 0).astype(MXU_DTYPE)
        for h in range(HEADS):
            lo = h * HEAD_PAD
            v_ref[:, lo:lo + HEAD_DIM] = _mx(v[:, h * HEAD_DIM:(h + 1) * HEAD_DIM])
            v_ref[:, lo + HEAD_DIM:lo + HEAD_PAD] = ones_col
            q_ref[:, lo:lo + HEAD_DIM] = _mx(q[:, lo:lo + HEAD_DIM])
            q_ref[:, lo + HEAD_DIM:lo + HEAD_PAD] = _mx(_rope(q[:, lo + HEAD_DIM:lo + HEAD_PAD], cos_t, sa, sb))
            k_ref[:, lo:lo + HEAD_DIM] = _mx(kn[:, h * HEAD_DIM:(h + 1) * HEAD_DIM])
            k_ref[:, lo + HEAD_DIM:lo + HEAD_PAD] = kpe

    def const(shape):
        return pl.BlockSpec(shape, lambda i: (0, 0))

    def rows(w):
        return pl.BlockSpec((tm, w), lambda i: (i, 0))

    return pl.pallas_call(
        body,
        name="mla_prep",
        grid=(s // tm,),
        in_specs=[
            pl.BlockSpec((tm, SEG), lambda i: (i, SMALL_SEG)),
            const((1, Q_LORA)), const((1, KV_LORA)),
            const((Q_LORA, HEADS * HEAD_PAD)), const((KV_LORA, D_MODEL)), const((KV_LORA, D_MODEL)),
            rows(HEAD_DIM), rows(HEAD_DIM), rows(HEAD_DIM),
        ],
        out_specs=[rows(HEADS * HEAD_PAD)] * 3 + [rows(Q_LORA), rows(KV_LORA)],
        out_shape=[
            jax.ShapeDtypeStruct((s, HEADS * HEAD_PAD), MXU_DTYPE),
            jax.ShapeDtypeStruct((s, HEADS * HEAD_PAD), MXU_DTYPE),
            jax.ShapeDtypeStruct((s, HEADS * HEAD_PAD), MXU_DTYPE),
            jax.ShapeDtypeStruct((s, Q_LORA), MXU_DTYPE),
            jax.ShapeDtypeStruct((s, KV_LORA), MXU_DTYPE),
        ],
        compiler_params=_params(),
    )(proj, q_a_g, kv_a_g, w_uq_p, w_kn, w_v, cos, sin_a, sin_b)


def _flash_fwd(q_all, k_all, v_all, proj):
    s = q_all.shape[0]
    t = min(s, T_ATT)
    n = s // t
    ts = t // ATT_SUB
    n_pairs = n * (n + 1) // 2

    def body(q_ref, k_ref, v_ref, mz_ref, ao_ref, yb_ref, mblk_ref, p_hbm, m_sc, acc_sc, stage, p_sems):
        head, qi = pl.program_id(0), pl.program_id(1)
        m_sc[...] = jnp.full_like(m_sc, -jnp.inf)
        acc_sc[...] = jnp.zeros_like(acc_sc)
        mblk_ref[...] = jnp.zeros_like(mblk_ref)
        lane = lax.broadcasted_iota(jnp.int32, (ts, HEAD_DIM), 1)
        first_block = head * n_pairs + qi * (qi + 1) // 2

        def p_copy(slot, pair, r):
            rows = pl.ds(r * ts, ts)
            return pltpu.make_async_copy(stage.at[slot, rows], p_hbm.at[head, pair, rows], p_sems.at[slot, r])

        def p_wait(slot):
            for r in range(ATT_SUB):
                p_copy(slot, 0, r).wait()

        def key_block(ki, diagonal):
            base = pl.multiple_of(ki * t, t)
            count = first_block + ki
            slot = lax.rem(count, P_SLOTS)
            sc, pb, alpha = {}, {}, {}

            @pl.when(count >= P_SLOTS)
            def _():
                p_wait(slot)

            if diagonal:
                stage[slot] = jnp.zeros((t, t), MXU_DTYPE)

            def width(r):
                return (r + 1) * ts if diagonal else t

            def scores(r):
                w = width(r)
                s_r = _dot_nt(q_ref[r * ts:(r + 1) * ts], k_ref[pl.ds(base, w), :])
                if diagonal:
                    row = lax.broadcasted_iota(jnp.int32, (ts, w), 0) + r * ts
                    col = lax.broadcasted_iota(jnp.int32, (ts, w), 1)
                    s_r = jnp.where(row >= col, s_r, -jnp.inf)
                sc[r] = s_r

            def softmax(r):
                rs = slice(r * ts, (r + 1) * ts)
                m_prev = m_sc[rs]
                m_new = jnp.maximum(m_prev, jnp.max(sc[r], axis=-1, keepdims=True))
                pb[r] = _mx(jnp.exp2(sc[r] - m_new))
                alpha[r] = jnp.exp2(m_prev - m_new)
                m_sc[rs] = m_new
                mblk_ref[rs] = jnp.where(lane == ki, m_new, mblk_ref[rs])
                stage[slot, rs, :width(r)] = pb[r]
                p_copy(slot, qi * (qi + 1) // 2 + ki, r).start()

            def weighted_values(r):
                rs = slice(r * ts, (r + 1) * ts)
                acc_sc[rs] = alpha[r] * acc_sc[rs] + _dot(pb[r], v_ref[pl.ds(base, width(r)), :])

            for step in range(ATT_SUB + 2):
                if step < ATT_SUB:
                    scores(step)
                if 1 <= step <= ATT_SUB:
                    softmax(step - 1)
                if step >= 2:
                    weighted_values(step - 2)

        def below_diagonal(ki, carry):
            key_block(ki, False)
            return carry

        lax.fori_loop(0, qi, below_diagonal, 0)
        key_block(qi, True)

        @pl.when((head == HEADS - 1) & (qi == n - 1))
        def _():
            for slot in range(min(P_SLOTS, HEADS * n_pairs)):
                p_wait(slot)

        acc = acc_sc[...]
        l = acc[:, HEAD_DIM:HEAD_DIM + 1]
        ao = acc[:, :HEAD_DIM] / l
        ao_ref[...] = ao
        lane_t = lax.broadcasted_iota(jnp.int32, (t, HEAD_DIM), 1)
        mblk_ref[...] = jnp.where(lane_t == LSE_LANE, m_sc[...] + jnp.log2(l), mblk_ref[...])
        mz = mz_ref[...]
        yb_ref[...] = _mx(ao * (mz * _sigmoid(mz)))

    q_map = lambda h, qi: (qi, h)
    return pl.pallas_call(
        body,
        name="flash_fwd",
        grid=(HEADS, n),
        in_specs=[
            pl.BlockSpec((t, HEAD_PAD), q_map),
            pl.BlockSpec((s, HEAD_PAD), lambda h, qi: (0, h)),
            pl.BlockSpec((s, HEAD_PAD), lambda h, qi: (0, h)),
            pl.BlockSpec((t, HEAD_DIM), lambda h, qi: (qi, MZ_SEG * HEADS + h)),
        ],
        out_specs=[pl.BlockSpec((t, HEAD_DIM), q_map)] * 3 + [HBM_SPEC],
        out_shape=[
            jax.ShapeDtypeStruct((s, D_MODEL), F32),
            jax.ShapeDtypeStruct((s, D_MODEL), MXU_DTYPE),
            jax.ShapeDtypeStruct((s, D_MODEL), F32),
            jax.ShapeDtypeStruct((HEADS, n_pairs, t, t), MXU_DTYPE),
        ],
        scratch_shapes=[
            pltpu.VMEM((t, 1), F32),
            pltpu.VMEM((t, HEAD_PAD), F32),
            pltpu.VMEM((P_SLOTS, t, t), MXU_DTYPE),
            pltpu.SemaphoreType.DMA((P_SLOTS, ATT_SUB)),
        ],
        compiler_params=_params(),
    )(q_all, k_all, v_all, proj)


def _merge_fwd_bwd(x, target, ya, yb, ao, proj, b_gate, final_g, w_pa, w_pb, w_out):
    s = x.shape[0]
    tm = min(s, TM_ROW)

    def body(x_ref, t_ref, ya_ref, yb_ref, ao_ref, mz_ref, g0_ref, g1_ref, bg_ref, fg_ref, wpa_ref, wpb_ref, wo_ref,
             dx2_ref, dya_ref, dao_ref, dtail_ref, mb_ref, dpab_ref, dpbb_ref, dx2b_ref,
             loss_ref, dfg_ref, dbg_ref):
        i = pl.program_id(0)

        @pl.when(i == 0)
        def _():
            loss_ref[...] = jnp.zeros_like(loss_ref)
            dfg_ref[...] = jnp.zeros_like(dfg_ref)
            dbg_ref[...] = jnp.zeros_like(dbg_ref)

        pa = _dot(ya_ref[...], wpa_ref[...])
        pb = _dot(yb_ref[...], wpb_ref[...])
        bg = bg_ref[...]
        g0 = _sigmoid(g0_ref[...] + bg[:, :D_MODEL])
        g1 = _sigmoid(g1_ref[...] + bg[:, D_MODEL:])
        merged = g0 * pa + g1 * pb
        mb = _mx(merged)
        mb_ref[...] = mb
        x2 = x_ref[...] + _dot(mb, wo_ref[...])
        r = lax.rsqrt(jnp.mean(x2 * x2, axis=-1, keepdims=True) + EPS)
        xn = x2 * r
        fg = fg_ref[...]
        diff = xn * fg - t_ref[...]
        loss_ref[...] += 0.5 * jnp.sum(jnp.mean(diff * diff, axis=-1, keepdims=True))
        dy = diff * (1.0 / D_MODEL)
        dfg_ref[...] += _bcast_rows(jnp.sum(dy * xn, axis=0, keepdims=True), 8)
        tt = dy * fg
        dx2 = r * (tt - xn * jnp.mean(tt * xn, axis=-1, keepdims=True))
        dx2_ref[...] = dx2
        dx2b = _mx(dx2)
        dx2b_ref[...] = dx2b
        dmerged = _dot_nt(dx2b, wo_ref[...])
        dpa = _mx(dmerged * g0)
        dpb = _mx(dmerged * g1)
        dpab_ref[...] = dpa
        dpbb_ref[...] = dpb
        dg0 = dmerged * pa * (g0 * (1.0 - g0))
        dg1 = dmerged * pb * (g1 * (1.0 - g1))
        dtail_ref[1] = _mx(dg0)
        dtail_ref[2] = _mx(dg1)
        dbg_ref[:, :D_MODEL] += _bcast_rows(jnp.sum(dg0, axis=0, keepdims=True), 8)
        dbg_ref[:, D_MODEL:] += _bcast_rows(jnp.sum(dg1, axis=0, keepdims=True), 8)
        dya_ref[...] = _dot_nt(dpa, wpa_ref[...])
        dyb = _dot_nt(dpb, wpb_ref[...])
        mz = mz_ref[...]
        sg = _sigmoid(mz)
        dao_ref[...] = _mx(dyb * (mz * sg))
        dtail_ref[0] = _mx(dyb * ao_ref[...] * (sg + mz * sg * (1.0 - sg)))

    def rows(w=D_MODEL):
        return pl.BlockSpec((tm, w), lambda i: (i, 0))

    def const(shape):
        return pl.BlockSpec(shape, lambda i: (0, 0))

    def seg(k):
        return pl.BlockSpec((tm, SEG), lambda i: (i, k))

    f32 = jax.ShapeDtypeStruct((s, D_MODEL), F32)
    b16 = jax.ShapeDtypeStruct((s, D_MODEL), MXU_DTYPE)
    return pl.pallas_call(
        body,
        name="merge_fwd_bwd",
        grid=(s // tm,),
        in_specs=[
            rows(), rows(), rows(), rows(), rows(),
            seg(MZ_SEG), seg(GL_SEG), seg(GL_SEG + 1),
            const((1, 2 * D_MODEL)), const((1, D_MODEL)),
            _resident((D_MODEL, D_MODEL)), _resident((D_MODEL, D_MODEL)), _resident((D_MODEL, D_MODEL)),
        ],
        out_specs=[rows()] * 3 + [pl.BlockSpec((3, tm, D_MODEL), lambda i: (0, i, 0))] + [rows()] * 4
        + [const((8, HEAD_DIM)), const((8, D_MODEL)), const((8, 2 * D_MODEL))],
        out_shape=[f32, f32, b16, jax.ShapeDtypeStruct((3, s, D_MODEL), MXU_DTYPE), b16, b16, b16, b16,
                   jax.ShapeDtypeStruct((8, HEAD_DIM), F32),
                   jax.ShapeDtypeStruct((8, D_MODEL), F32),
                   jax.ShapeDtypeStruct((8, 2 * D_MODEL), F32)],
        compiler_params=_params(),
    )(x, target, ya, yb, ao, proj, proj, proj, b_gate, final_g, w_pa, w_pb, w_out)


def _flash_bwd(q_all, k_all, v_all, dao, ao, mblk, p_all, slab_sets):
    s = q_all.shape[0]
    t = min(s, T_ATT_BWD)
    n = s // t
    pairs = [(ki, qi) for ki in range(n) for qi in range(ki, n)]
    ki_list = jnp.asarray([p[0] for p in pairs], jnp.int32)
    qi_list = jnp.asarray([p[1] for p in pairs], jnp.int32)
    p_list = jnp.asarray([qi * (qi + 1) // 2 + ki for ki, qi in pairs], jnp.int32)
    n_ops = len(slab_sets)

    def body(ki_ref, qi_ref, pidx_ref, q_ref, k_ref, v_ref, do_ref, ao_ref, mblk_ref, p_ref, *rest):
        g_refs = rest[:n_ops]
        dq_ref, dk_ref, dv_ref = rest[n_ops:n_ops + 3]
        recv_refs = rest[n_ops + 3:2 * n_ops + 3]
        dk_acc, dv_acc, send_sems, recv_sems, local_sems, p_buf, p_sems = rest[2 * n_ops + 3:]
        head, step = pl.program_id(0), pl.program_id(1)

        count = head * len(pairs) + step

        def p_fetch(block):
            h_blk, s_blk = block // len(pairs), lax.rem(block, len(pairs))
            slot = lax.rem(block, 3)
            return pltpu.make_async_copy(p_ref.at[h_blk, pidx_ref[s_blk]], p_buf.at[slot], p_sems.at[slot])

        @pl.when(count == 0)
        def _():
            p_fetch(0).start()
            p_fetch(1).start()

        @pl.when(count + 2 < HEADS * len(pairs))
        def _():
            p_fetch(count + 2).start()

        p_fetch(count).wait()
        p_slot = lax.rem(count, 3)
        ki, qi = ki_ref[step], qi_ref[step]

        @pl.when((head == 0) & (step == 0))
        def _():
            _Exchange(g_refs, recv_refs, send_sems, recv_sems, local_sems).start()

        @pl.when(qi == ki)
        def _():
            dk_acc[...] = jnp.zeros_like(dk_acc)
            dv_acc[...] = jnp.zeros_like(dv_acc)

        @pl.when(ki == 0)
        def _():
            dq_ref[pl.ds(pl.multiple_of(qi * t, t), t), :] = jnp.zeros((t, HEAD_PAD), F32)

        def pair(masked):
            nsub = ATT_SUB if masked else ATT_SUB_BWD
            ts = t // nsub
            dk_parts, dv_parts = [], []
            for r in range(nsub):
                rs = slice(r * ts, (r + 1) * ts)
                w = (r + 1) * ts if masked else t
                k = k_ref[:w]
                v = v_ref[:w]
                q = q_ref[rs]
                lane = lax.broadcasted_iota(jnp.int32, (ts, HEAD_DIM), 1)
                stats = mblk_ref[rs]
                m_blk = jnp.max(jnp.where(lane == ki, stats, -jnp.inf), axis=-1, keepdims=True)
                lse = jnp.max(jnp.where(lane == LSE_LANE, stats, -jnp.inf), axis=-1, keepdims=True)
                factor = jnp.exp2(m_blk - lse)
                p_st = p_buf[p_slot, rs, :w]
                do = do_ref[rs]
                do_f = do.astype(F32)
                delta = jnp.sum(do_f * ao_ref[rs], axis=-1, keepdims=True)
                dv_part = _dot_tn(p_st, _mx(do_f * factor))
                ds = p_st * _mx((_dot_nt(do, v) - delta) * factor)
                dk_part = _dot_tn(ds, q)
                rows = pl.ds(pl.multiple_of(qi * t + r * ts, ts), ts)
                dq_ref[rows, :] += _dot(ds, k)
                if masked:
                    dk_acc[:w] += dk_part
                    dv_acc[:w] += dv_part
                else:
                    dk_parts.append(dk_part)
                    dv_parts.append(dv_part)

            if not masked:
                dk_acc[...] += sum(dk_parts[1:], dk_parts[0])
                dv_acc[...] += sum(dv_parts[1:], dv_parts[0])

        @pl.when(qi == ki)
        def _():
            pair(True)

        @pl.when(qi > ki)
        def _():
            pair(False)

        @pl.when(qi == n - 1)
        def _():
            dk_ref[...] = _mx(dk_acc[...] * LN2)
            dv_ref[...] = _mx(dv_acc[...])

        @pl.when((head == HEADS - 1) & (step == len(pairs) - 1))
        def _():
            _Exchange(g_refs, recv_refs, send_sems, recv_sems, local_sems).wait()

    q_map = lambda h, p, ki_ref, qi_ref, pidx_ref: (qi_ref[p], h)
    kv_map = lambda h, p, ki_ref, qi_ref, pidx_ref: (ki_ref[p], h)
    grid_spec = pltpu.PrefetchScalarGridSpec(
        num_scalar_prefetch=3,
        grid=(HEADS, len(pairs)),
        in_specs=[
            pl.BlockSpec((t, HEAD_PAD), q_map),
            pl.BlockSpec((t, HEAD_PAD), kv_map),
            pl.BlockSpec((t, HEAD_DIM), lambda h, p, ki_ref, qi_ref, pidx_ref: (ki_ref[p], 2 * h)),
            pl.BlockSpec((t, HEAD_DIM), q_map),
            pl.BlockSpec((t, HEAD_DIM), q_map),
            pl.BlockSpec((t, HEAD_DIM), q_map),
            HBM_SPEC,
        ] + [HBM_SPEC] * n_ops,
        out_specs=[
            pl.BlockSpec((s, HEAD_PAD), lambda h, p, ki_ref, qi_ref, pidx_ref: (0, h)),
            pl.BlockSpec((t, HEAD_PAD), kv_map),
            pl.BlockSpec((t, HEAD_DIM), kv_map),
        ] + [HBM_SPEC] * n_ops,
        scratch_shapes=[pltpu.VMEM((t, HEAD_PAD), F32), pltpu.VMEM((t, HEAD_DIM), F32)]
        + _Exchange.semaphores(n_ops) + [pltpu.VMEM((3, t, t), MXU_DTYPE), pltpu.SemaphoreType.DMA((3,))],
    )
    outs = pl.pallas_call(
        body,
        name="flash_bwd",
        grid_spec=grid_spec,
        out_shape=[
            jax.ShapeDtypeStruct((s, HEADS * HEAD_PAD), F32),
            jax.ShapeDtypeStruct((s, HEADS * HEAD_PAD), MXU_DTYPE),
            jax.ShapeDtypeStruct((s, D_MODEL), MXU_DTYPE),
        ] + [jax.ShapeDtypeStruct(a.shape, a.dtype) for a in slab_sets],
        compiler_params=_params(VMEM_LIMIT_BIG),
    )(ki_list, qi_list, p_list, q_all, k_all, v_all, dao, ao, mblk, p_all, *slab_sets)
    return outs[0], outs[1], outs[2], outs[3:]


def _mla_prep_bwd(dq_all, dk_all, dv_all, proj, q_a_g, kv_a_g, w_uq_p, w_kn, w_v, cos, sin_a, sin_b):
    s = proj.shape[0]
    tm = min(s, TM_MLA)

    def body(dq_ref, dk_ref, dv_ref, sm_ref, gq_ref, gk_ref, wq_ref, wkn_ref, wv_ref, cos_ref, sa_ref, sb_ref,
             dsm_ref, dqf_ref, dkn_ref, dvb_ref, dgq_ref, dgk_ref):
        i = pl.program_id(0)

        @pl.when(i == 0)
        def _():
            dgq_ref[...] = jnp.zeros_like(dgq_ref)
            dgk_ref[...] = jnp.zeros_like(dgk_ref)

        cos_t, sa, sb = cos_ref[...], sa_ref[...], sb_ref[...]
        dkpe = jnp.zeros((tm, HEAD_DIM), F32)
        for h in range(HEADS):
            lo = h * HEAD_PAD
            dqf_ref[:, lo:lo + HEAD_DIM] = _mx(dq_ref[:, lo:lo + HEAD_DIM] * QK_SCALE)
            dqf_ref[:, lo + HEAD_DIM:lo + HEAD_PAD] = _mx(
                _rope_t(dq_ref[:, lo + HEAD_DIM:lo + HEAD_PAD] * QK_SCALE, cos_t, sa, sb))
            dkn_ref[:, h * HEAD_DIM:(h + 1) * HEAD_DIM] = dk_ref[:, lo:lo + HEAD_DIM]
            dkpe = dkpe + dk_ref[:, lo + HEAD_DIM:lo + HEAD_PAD].astype(F32)
        dkr = _rope_t(dkpe, cos_t, sa, sb)
        dvb = dv_ref[...]
        dvb_ref[...] = dvb
        dcqn = _dot_nt(dqf_ref[...], wq_ref[...])
        dckvn = _dot_nt(dkn_ref[...], wkn_ref[...]) + _dot_nt(dvb, wv_ref[...])

        small = sm_ref[...]
        cq = small[:, :Q_LORA]
        ckv = small[:, Q_LORA:Q_LORA + KV_LORA]
        rq = lax.rsqrt(jnp.mean(cq * cq, axis=-1, keepdims=True) + EPS)
        rk = lax.rsqrt(jnp.mean(ckv * ckv, axis=-1, keepdims=True) + EPS)
        cqh = cq * rq
        ckh = ckv * rk
        dgq_ref[...] += _bcast_rows(jnp.sum(dcqn * cqh, axis=0, keepdims=True), 8)
        dgk_ref[...] += _bcast_rows(jnp.sum(dckvn * ckh, axis=0, keepdims=True), 8)
        tq = dcqn * gq_ref[...]
        tk = dckvn * gk_ref[...]
        dcq = rq * (tq - cqh * jnp.mean(tq * cqh, axis=-1, keepdims=True))
        dckv = rk * (tk - ckh * jnp.mean(tk * ckh, axis=-1, keepdims=True))
        dsm_ref[:, :Q_LORA] = _mx(dcq)
        dsm_ref[:, Q_LORA:Q_LORA + KV_LORA] = _mx(dckv)
        dsm_ref[:, Q_LORA + KV_LORA:Q_LORA + KV_LORA + HEAD_DIM] = _mx(dkr)
        dsm_ref[:, Q_LORA + KV_LORA + HEAD_DIM:] = jnp.zeros((tm, SEG - Q_LORA - KV_LORA - HEAD_DIM), MXU_DTYPE)

    def const(shape):
        return pl.BlockSpec(shape, lambda i: (0, 0))

    def rows(w):
        return pl.BlockSpec((tm, w), lambda i: (i, 0))

    return pl.pallas_call(
        body,
        name="mla_prep_bwd",
        grid=(s // tm,),
        in_specs=[
            rows(HEADS * HEAD_PAD), rows(HEADS * HEAD_PAD), rows(D_MODEL),
            pl.BlockSpec((tm, SEG), lambda i: (i, SMALL_SEG)),
            const((1, Q_LORA)), const((1, KV_LORA)),
            const((Q_LORA, HEADS * HEAD_PAD)), const((KV_LORA, D_MODEL)), const((KV_LORA, D_MODEL)),
            rows(HEAD_DIM), rows(HEAD_DIM), rows(HEAD_DIM),
        ],
        out_specs=[rows(SEG), rows(HEADS * HEAD_PAD), rows(D_MODEL), rows(D_MODEL),
                   const((8, Q_LORA)), const((8, KV_LORA))],
        out_shape=[
            jax.ShapeDtypeStruct((s, SEG), MXU_DTYPE),
            jax.ShapeDtypeStruct((s, HEADS * HEAD_PAD), MXU_DTYPE),
            jax.ShapeDtypeStruct((s, D_MODEL), MXU_DTYPE),
            jax.ShapeDtypeStruct((s, D_MODEL), MXU_DTYPE),
            jax.ShapeDtypeStruct((8, Q_LORA), F32),
            jax.ShapeDtypeStruct((8, KV_LORA), F32),
        ],
        compiler_params=_params(),
    )(dq_all, dk_all, dv_all, proj, q_a_g, kv_a_g, w_uq_p, w_kn, w_v, cos, sin_a, sin_b)


def _hgrn_bwd(proj, lb_logits, hg_norm_g, o_all, dya, states):
    s = proj.shape[0]
    t = min(s, T_HGRN)
    nb = s // t
    nc = t // HG_CHUNK

    def body(hq_ref, hf_ref, hi_ref, hz_ref, lb_ref, g_ref, o_ref, dya_ref, st_ref,
             dh4_ref, dlb_ref, dg_ref, dstate, u_sc, g_sc, stf_sc, stb_sc, dstb_sc):
        h, b = pl.program_id(0), pl.program_id(1)

        @pl.when(b == 0)
        def _():
            dstate[...] = jnp.zeros_like(dstate)
            dlb_ref[...] = jnp.zeros_like(dlb_ref)

        @pl.when((b == 0) & (h == 0))
        def _():
            dg_ref[...] = jnp.zeros_like(dg_ref)

        lower = _chunk_lower_mask(t)
        pos = _chunk_pos(t)
        ghg = g_ref[...]
        for hh in range(HG_HEADS_PER_STEP):
            cols = slice(hh * HEAD_DIM, (hh + 1) * HEAD_DIM)
            hq, hf, hz = hq_ref[:, cols], hf_ref[:, cols], hz_ref[:, cols]
            gt = _hgrn_gates(hq, hf, lb_ref[:, cols], pos)
            vb = _mx(hi_ref[:, cols])
            qi, ki, ko = gt["qi"], gt["ki"], gt["ko"]
            qib, kib, kob = _mx(qi), _mx(ki), _mx(ko)

            o = o_ref[:, cols]
            sz = _sigmoid(hz)
            r = lax.rsqrt(jnp.mean(o * o, axis=-1, keepdims=True) + EPS)
            on = o * r
            dya_t = dya_ref[:, cols]
            don = dya_t * (hz * sz)
            dh4_ref[3, :, cols] = _mx(dya_t * (on * ghg) * (sz + hz * sz * (1.0 - sz)))
            dg_ref[...] += _bcast_rows(jnp.sum(don * on, axis=0, keepdims=True), 8)
            tt = don * ghg
            do = r * (tt - on * jnp.mean(tt * on, axis=-1, keepdims=True))
            dob = _mx(do)

            for c in range(nc):
                sl = slice(c * HG_CHUNK, (c + 1) * HG_CHUNK)
                u_sc[hh, c] = _dot_tn(vb[sl], kob[sl])
                g_sc[hh, c] = _dot_tn(dob[sl], qib[sl])

            st = st_ref[0, hh]
            for c in range(nc):
                stf_sc[hh, c] = st
                stb_sc[hh, c] = _mx(st)
                if c < nc - 1:
                    st = st * gt["dec"][c * HG_CHUNK:c * HG_CHUNK + 1, :] + u_sc[hh, c]

            dst = dstate[hh]
            dd_parts = [None] * nc
            for c in reversed(range(nc)):
                dec = gt["dec"][c * HG_CHUNK:c * HG_CHUNK + 1, :]
                dstb_sc[hh, c] = _mx(dst)
                dd_parts[c] = _bcast_rows(jnp.sum(dst * stf_sc[hh, c], axis=0, keepdims=True) * dec, HG_CHUNK)
                dst = dst * dec + g_sc[hh, c]
            dstate[hh] = dst

            a = jnp.where(lower, _dot_nt(qib, kib), 0.0)
            da = _mx(jnp.where(lower, _dot_nt(dob, vb), 0.0))
            dqi_intra = _dot(da, kib)
            dki = _dot_tn(da, qib)
            dv_intra = _dot_tn(_mx(a), dob)

            dqi_parts, dko_parts, dv_parts = [None] * nc, [None] * nc, [None] * nc
            for c in range(nc):
                sl = slice(c * HG_CHUNK, (c + 1) * HG_CHUNK)
                dv_parts[c] = dv_intra[sl] + _dot_nt(kob[sl], dstb_sc[hh, c])
                dko_parts[c] = _dot(vb[sl], dstb_sc[hh, c])
                dqi_parts[c] = dqi_intra[sl] + _dot(dob[sl], stb_sc[hh, c])
            dqi = jnp.concatenate(dqi_parts, axis=0)
            dko = jnp.concatenate(dko_parts, axis=0)
            dv = jnp.concatenate(dv_parts, axis=0)
            dd = jnp.concatenate(dd_parts, axis=0)

            dq = dqi * gt["eb"]
            dk = dki * gt["enb"] + dko * gt["eo"]
            db = dqi * qi - dki * ki - dko * ko
            dlogf = _rcumsum_chunk(db, pos) + _chunk_total(dko * ko) + dd
            df = dlogf / gt["f"] - dk
            lb, sig, sq = gt["lb"], gt["sig"], gt["sq"]
            dh4_ref[1, :, cols] = _mx(df * (1.0 - lb) * (sig * (1.0 - sig)))
            dh4_ref[0, :, cols] = _mx(dq * (sq + hq * sq * (1.0 - sq)))
            dh4_ref[2, :, cols] = _mx(dv)
            dlb = jnp.sum(df * (1.0 - sig), axis=0, keepdims=True) * (lb * (1.0 - lb))
            dlb_ref[:, cols] += jnp.concatenate([dlb, -dlb], axis=0)

    hw = HG_HEADS_PER_STEP * HEAD_DIM
    hsteps = HEADS // HG_HEADS_PER_STEP

    def seg(k):
        return pl.BlockSpec((t, hw), lambda h, b, k=k: (nb - 1 - b, k * hsteps + h))

    blk = pl.BlockSpec((t, hw), lambda h, b: (nb - 1 - b, h))
    return pl.pallas_call(
        body,
        name="hgrn_bwd",
        grid=(hsteps, nb),
        in_specs=[seg(0), seg(1), seg(2), seg(3),
                  pl.BlockSpec((2, hw), lambda h, b: (0, h)),
                  pl.BlockSpec((1, HEAD_DIM), lambda h, b: (0, 0)),
                  blk, blk,
                  pl.BlockSpec((1, HG_HEADS_PER_STEP, HEAD_DIM, HEAD_DIM), lambda h, b: (nb - 1 - b, h, 0, 0))],
        out_specs=[pl.BlockSpec((4, t, hw), lambda h, b: (0, nb - 1 - b, h)),
                   pl.BlockSpec((2, hw), lambda h, b: (0, h)),
                   pl.BlockSpec((8, HEAD_DIM), lambda h, b: (0, 0))],
        out_shape=[jax.ShapeDtypeStruct((4, s, D_MODEL), MXU_DTYPE),
                   jax.ShapeDtypeStruct((2, D_MODEL), F32),
                   jax.ShapeDtypeStruct((8, HEAD_DIM), F32)],
        scratch_shapes=[pltpu.VMEM((HG_HEADS_PER_STEP, HEAD_DIM, HEAD_DIM), F32)]
        + [pltpu.VMEM((HG_HEADS_PER_STEP, nc, HEAD_DIM, HEAD_DIM), F32)] * 3
        + [pltpu.VMEM((HG_HEADS_PER_STEP, nc, HEAD_DIM, HEAD_DIM), MXU_DTYPE)] * 2,
        compiler_params=_params(),
    )(proj, proj, proj, proj, lb_logits, hg_norm_g, o_all, dya, states)


def _dh_bwd(segs, w_in_p, x, dx2, norm_g, late_slab, late_recv_init, slab_sets):
    s = x.shape[0]
    tm = min(s, TM_ROW)
    seg_ops = [sg if isinstance(sg, tuple) else (sg, None) for sg in segs]
    nseg = len(segs)
    nsteps = s // tm
    n_ops = len(slab_sets)
    late_xyc = ((LATE_DEV >> 2) & 1, (LATE_DEV >> 1) & 1, LATE_DEV & 1)

    def body(*refs):
        seg_refs = refs[:nseg]
        w_ref, x_ref, dx2_ref, g_ref, late_ref, _ = refs[nseg:nseg + 6]
        g_refs = refs[nseg + 6:nseg + 6 + n_ops]
        gx_ref, dng_ref, late_recv_ref = refs[nseg + 6 + n_ops:nseg + 9 + n_ops]
        recv_refs = refs[nseg + 9 + n_ops:nseg + 9 + 2 * n_ops]
        (dp_buf, send_sems, recv_sems, local_sems,
         late_send, late_recvs, late_local) = refs[nseg + 9 + 2 * n_ops:]
        i = pl.program_id(0)
        me = 4 * lax.axis_index("x") + 2 * lax.axis_index("y") + lax.axis_index("c")

        def misc_exchange():
            return _Exchange(g_refs, recv_refs, send_sems, recv_sems, local_sems)

        def late_copy(sender):
            return pltpu.make_async_remote_copy(
                src_ref=late_ref.at[0], dst_ref=late_recv_ref.at[sender], send_sem=late_send,
                recv_sem=late_recvs.at[(sender ^ LATE_DEV) - 1], device_id=late_xyc, device_id_type=MESH)

        def late_own():
            return pltpu.make_async_copy(late_ref.at[0], late_recv_ref.at[LATE_DEV], late_local)

        @pl.when(i == 0)
        def _():
            dng_ref[...] = jnp.zeros_like(dng_ref)
            misc_exchange().start()

        @pl.when((i == 0) & (me != LATE_DEV))
        def _():
            late_copy(me).start()

        @pl.when((i == 0) & (me == LATE_DEV))
        def _():
            late_own().start()

        for k, sref in enumerate(seg_refs):
            dp_buf[:, k * SEG:(k + 1) * SEG] = sref[...]
        dh = _dot_nt(dp_buf[...], w_ref[...])
        xf = x_ref[...]
        r = lax.rsqrt(jnp.mean(xf * xf, axis=-1, keepdims=True) + EPS)
        xh = xf * r
        dng_ref[...] += _bcast_rows(jnp.sum(dh * xh, axis=0, keepdims=True), 8)
        tt = dh * g_ref[...]
        gx_ref[...] = dx2_ref[...] + r * (tt - xh * jnp.mean(tt * xh, axis=-1, keepdims=True))

        @pl.when(i == nsteps - 1)
        def _():
            misc_exchange().wait()

        @pl.when((i == nsteps - 1) & (me != LATE_DEV))
        def _():
            late_copy(me).wait_send()

        @pl.when((i == nsteps - 1) & (me == LATE_DEV))
        def _():
            for k in range(1, N_DEV):
                late_copy(LATE_DEV ^ k).wait_recv()
            late_own().wait()

    rows = pl.BlockSpec((tm, D_MODEL), lambda i: (i, 0))
    return pl.pallas_call(
        body,
        name="dh_bwd",
        grid=(nsteps,),
        in_specs=[pl.BlockSpec((tm, SEG), lambda i: (i, 0)) if j is None
                  else pl.BlockSpec((None, tm, SEG), lambda i, j=j: (j, i, 0)) for _, j in seg_ops] + [
            _resident((D_MODEL, PROJ_W)),
            rows, rows,
            pl.BlockSpec((1, D_MODEL), lambda i: (0, 0)),
            HBM_SPEC, HBM_SPEC,
        ] + [HBM_SPEC] * n_ops,
        out_specs=[rows, pl.BlockSpec((8, D_MODEL), lambda i: (0, 0)), HBM_SPEC] + [HBM_SPEC] * n_ops,
        out_shape=[jax.ShapeDtypeStruct((s, D_MODEL), F32), jax.ShapeDtypeStruct((8, D_MODEL), F32),
                   jax.ShapeDtypeStruct(late_recv_init.shape, late_recv_init.dtype)]
        + [jax.ShapeDtypeStruct(a.shape, a.dtype) for a in slab_sets],
        input_output_aliases={nseg + 5: 2},
        scratch_shapes=[pltpu.VMEM((tm, PROJ_W), MXU_DTYPE)] + _Exchange.semaphores(n_ops)
        + [pltpu.SemaphoreType.DMA, pltpu.SemaphoreType.DMA((N_DEV - 1,)), pltpu.SemaphoreType.DMA],
        compiler_params=_params(),
    )(*[a for a, _ in seg_ops], w_in_p, x, dx2, norm_g, late_slab, late_recv_init, *slab_sets)


def _matmul_tn(a, b, name, out_dtype=F32):
    s, m = a.shape
    stacked = b.ndim == 3
    n = b.shape[0] * b.shape[2] if stacked else b.shape[1]
    ts = min(s, TS_TN)
    tn = min(n, SEG)
    nk = s // ts
    if stacked:
        b_spec = pl.BlockSpec((None, ts, tn), lambda j, k: (j, k, 0))
    else:
        b_spec = pl.BlockSpec((ts, tn), lambda j, k: (k, j))

    def body(a_ref, b_ref, o_ref, acc):
        k = pl.program_id(1)
        part = _dot_tn(a_ref[...], b_ref[...])

        @pl.when(k == 0)
        def _():
            acc[...] = part

        @pl.when(k > 0)
        def _():
            acc[...] += part

        @pl.when(k == nk - 1)
        def _():
            o_ref[...] = acc[...].astype(out_dtype)

    return pl.pallas_call(
        body,
        name=name,
        grid=(n // tn, nk),
        in_specs=[pl.BlockSpec((ts, m), lambda j, k: (k, 0)), b_spec],
        out_specs=pl.BlockSpec((m, tn), lambda j, k: (0, j)),
        out_shape=jax.ShapeDtypeStruct((m, n), out_dtype),
        scratch_shapes=[pltpu.VMEM((m, tn), F32)],
        compiler_params=_params(),
    )(a, b)


def _w_in_pieces():
    per = IN_COLS // N_DEV
    pad_at = SMALL_SEG * SEG + Q_LORA + KV_LORA + QK_ROPE
    pieces = []
    for j in range(N_DEV):
        u0, u1 = j * per, (j + 1) * per
        cuts = [u0] + ([pad_at] if u0 < pad_at < u1 else []) + [u1]
        for a, b in zip(cuts[:-1], cuts[1:]):
            pieces.append((j, a - u0, b - u0, a if a < pad_at else a + PROJ_W - IN_COLS))
    return pad_at, pieces


def _assemble_w_in(gathered):
    tr = TM_ROW
    pad_at, pieces = _w_in_pieces()

    def body(in_ref, out_ref):
        out_ref[:, pad_at:pad_at + PROJ_W - IN_COLS] = jnp.zeros((tr, PROJ_W - IN_COLS), gathered.dtype)
        for j, a, b, p0 in pieces:
            out_ref[:, p0:p0 + b - a] = in_ref[j, :, a:b]

    return pl.pallas_call(
        body,
        name="assemble_w_in",
        grid=(D_MODEL // tr,),
        in_specs=[pl.BlockSpec((N_DEV, tr, PACK_COLS), lambda i: (0, i, 0))],
        out_specs=pl.BlockSpec((tr, PROJ_W), lambda i: (i, 0)),
        out_shape=jax.ShapeDtypeStruct((D_MODEL, PROJ_W), gathered.dtype),
        compiler_params=_params(),
    )(gathered)


def _scatter_dw_in(dw_segs, devs, name):
    tr = TM_ROW
    _, pieces = _w_in_pieces()
    per = IN_COLS // N_DEV
    seg_ids = sorted(dw_segs)
    nseg = len(seg_ids)
    seg_ops = [dw_segs[k] if isinstance(dw_segs[k], tuple) else (dw_segs[k], 0) for k in seg_ids]

    def body(*refs):
        out_ref, buf = refs[nseg:]
        for k in range(PROJ_W // SEG):
            if k in seg_ids:
                buf[:, k * SEG:(k + 1) * SEG] = refs[seg_ids.index(k)][...]
            else:
                buf[:, k * SEG:(k + 1) * SEG] = jnp.zeros((tr, SEG), F32)
        for slot, dev in enumerate(devs):
            out_ref[slot, :, per:] = jnp.zeros((tr, PACK_COLS - per), TRANSPORT_DTYPE)
            for j, a, b, p0 in pieces:
                if j == dev:
                    out_ref[slot, :, a:b] = buf[:, p0:p0 + b - a].astype(TRANSPORT_DTYPE)

    return pl.pallas_call(
        body,
        name=name,
        grid=(D_MODEL // tr,),
        in_specs=[pl.BlockSpec((tr, SEG), lambda i, j=j: (i, j)) for _, j in seg_ops],
        out_specs=pl.BlockSpec((len(devs), tr, PACK_COLS), lambda i: (0, i, 0)),
        out_shape=jax.ShapeDtypeStruct((len(devs), D_MODEL, PACK_COLS), TRANSPORT_DTYPE),
        scratch_shapes=[pltpu.VMEM((tr, PROJ_W), F32)],
        compiler_params=_params(),
    )(*[a for a, _ in seg_ops])


def _rope_tables(s):
    inv = ROPE_THETA ** (-jnp.arange(0, QK_ROPE, 2, dtype=F32) / QK_ROPE)
    ang = jnp.arange(s, dtype=F32)[:, None] * inv[None, :]
    cos, sin = jnp.cos(ang), jnp.sin(ang)
    z32 = jnp.zeros_like(cos)
    z64 = jnp.zeros((s, HEAD_DIM - QK_ROPE), F32)
    cos_t = jnp.concatenate([cos, cos, z64], axis=1)
    sin_a = jnp.concatenate([-sin, z32, z64], axis=1)
    sin_b = jnp.concatenate([z32, sin, z64], axis=1)
    return cos_t, sin_a, sin_b


def _small_rows(b_gate, lb_logits, hg_norm_g, q_a_g, kv_a_g, final_norm_g, loss):
    row5 = jnp.concatenate([hg_norm_g.reshape(1, -1), q_a_g.reshape(1, -1), kv_a_g.reshape(1, -1), loss.reshape(1, 1),
                            jnp.zeros((1, PACK_COLS - LOSS_LANE - 1), F32)], axis=1)
    zero_row = jnp.zeros((1, PACK_COLS), F32)
    return jnp.concatenate([zero_row, b_gate.reshape(2, -1), lb_logits, row5, final_norm_g.reshape(1, -1), zero_row],
                           axis=0)


def _adamw_small(recv_small, recv_norm_g, weights, m, v):
    n = len(weights)

    def body(rs_ref, rn_ref, *refs):
        w_refs, m_refs, v_refs = refs[:n], refs[n:2 * n], refs[2 * n:3 * n]
        loss_ref = refs[3 * n]
        outs = refs[3 * n + 1:]
        gs, gn = rs_ref[0], rn_ref[0]
        for i in range(1, N_DEV):
            gs = gs + rs_ref[i]
            gn = gn + rn_ref[i]
        loss_ref[...] = gs[5:6, LOSS_LANE:LOSS_LANE + HEAD_DIM]
        grads = [gn[0:1], jnp.concatenate([gs[1:2], gs[2:3]], axis=1), gs[3:5],
                 gs[5:6, :HEAD_DIM], gs[5:6, HEAD_DIM:HEAD_DIM + Q_LORA], gs[5:6, HEAD_DIM + Q_LORA:LOSS_LANE], gs[6:7]]
        for k, g in enumerate(grads):
            m_new = ADAM_B1 * m_refs[k][...] + (1.0 - ADAM_B1) * g
            v_new = ADAM_B2 * v_refs[k][...] + (1.0 - ADAM_B2) * (g * g)
            m_hat = m_new / (1.0 - ADAM_B1 ** ADAM_STEP)
            v_hat = v_new / (1.0 - ADAM_B2 ** ADAM_STEP)
            outs[k][...] = g
            outs[n + k][...] = -ADAM_LR * (m_hat / (jnp.sqrt(v_hat) + ADAM_EPS) + ADAM_WD * w_refs[k][...])
            outs[2 * n + k][...] = m_new
            outs[3 * n + k][...] = v_new

    shapes = [jax.ShapeDtypeStruct(w.shape, F32) for w in weights]
    res = pl.pallas_call(
        body,
        name="adamw_small",
        out_shape=[jax.ShapeDtypeStruct((1, HEAD_DIM), F32)] + shapes * 4,
        compiler_params=_params(),
    )(recv_small, recv_norm_g, *weights, *m, *v)
    return res[0], [res[1 + k * n:1 + (k + 1) * n] for k in range(4)]


def _weight_shard_buffers(w_in, w_uq, w_ukv, w_pa, w_pb, w_out):
    w_in_pad = jnp.pad(w_in.reshape(D_MODEL, -1), ((0, 0), (0, PACK_COLS - IN_COLS // N_DEV)))
    parts = [a.reshape(-1, PACK_COLS) for a in (w_pa, w_pb, w_out, w_uq, w_ukv)]
    others = jnp.concatenate(parts + [jnp.zeros((ROWS_OTHER - ROWS_OTHER_USED, PACK_COLS), F32)], axis=0)
    return w_in_pad.astype(MXU_DTYPE), others.astype(MXU_DTYPE)


def _other_weights(gathered):
    r0 = 0
    mats = []
    for _ in range(3):
        mats.append(gathered[:, r0:r0 + ROWS_W_PROJ].reshape(D_MODEL, D_MODEL))
        r0 += ROWS_W_PROJ
    w_uq = gathered[:, r0:r0 + ROWS_W_UQ].reshape(N_DEV, Q_LORA, QK_DIM).transpose(1, 0, 2)
    w_uq_p = jnp.concatenate([w_uq, jnp.zeros((Q_LORA, HEADS, HEAD_PAD - QK_DIM), w_uq.dtype)], axis=2)
    w_uq_p = w_uq_p.reshape(Q_LORA, HEADS * HEAD_PAD)
    r0 += ROWS_W_UQ
    w_ukv = gathered[:, r0:r0 + ROWS_W_UKV].reshape(N_DEV, KV_LORA, 2 * HEAD_DIM).transpose(1, 0, 2)
    w_kn = w_ukv[:, :, :HEAD_DIM].reshape(KV_LORA, D_MODEL)
    w_v = w_ukv[:, :, HEAD_DIM:].reshape(KV_LORA, D_MODEL)
    return w_uq_p, w_kn, w_v, mats[0], mats[1], mats[2]


def _late_slab_sets(dw_uq_p, dw_kn, dw_v, small_rows):
    uq = dw_uq_p.reshape(Q_LORA, HEADS, HEAD_PAD).transpose(1, 0, 2)
    ukv = jnp.concatenate([dw_kn.reshape(KV_LORA, HEADS, HEAD_DIM),
                           dw_v.reshape(KV_LORA, HEADS, HEAD_DIM)], axis=2).transpose(1, 0, 2)
    return [uq, ukv, jnp.broadcast_to(small_rows[None], (N_DEV,) + small_rows.shape)]


def _step_gradients(x, target, norm_g, b_gate, lb_logits, hg_norm_g, q_a_g, kv_a_g, final_g,
                    w_in_p, other_shard):
    s = x.shape[0]
    cos, sin_a, sin_b = _rope_tables(s)
    proj, h, gathered = _inproj(x, norm_g, w_in_p, other_shard)
    w_uq_p, w_kn, w_v, w_pa, w_pb, w_out = _other_weights(gathered)
    o_all, ya, states = _hgrn_fwd(proj, lb_logits, hg_norm_g)
    q_all, k_all, v_all, cqn, ckvn = _mla_prep(proj, q_a_g, kv_a_g, w_uq_p, w_kn, w_v, cos, sin_a, sin_b)
    ao, yb, mblk, p_all = _flash_fwd(q_all, k_all, v_all, proj)
    (dx2, dya, dao, d_tail, merged_b, dpa_b, dpb_b, dx2_b,
     loss_acc, dfg_acc, dbg_acc) = _merge_fwd_bwd(x, target, ya, yb, ao, proj, b_gate, final_g, w_pa, w_pb, w_out)
    d_head, dlb, dhg_acc = _hgrn_bwd(proj, lb_logits, hg_norm_g, o_all, dya, states)

    dw_head = _matmul_tn(h, d_head, "dw_in_head")
    dw_tail = _matmul_tn(h, d_tail, "dw_in_tail")
    dw_early = {k: (dw_head, k) for k in range(4)}
    dw_early.update({MZ_SEG + k: (dw_tail, k) for k in range(3)})
    mats = [_matmul_tn(a, b, name, TRANSPORT_DTYPE).reshape(N_DEV, ROWS_W_PROJ, PACK_COLS)
            for a, b, name in ((ya, dpa_b, "dw_pa"), (yb, dpb_b, "dw_pb"), (merged_b, dx2_b, "dw_out"))]
    early_slabs = [_scatter_dw_in(dw_early, list(range(N_DEV)), "scatter_dw_in")] + mats
    dq_all, dk_all, dv_all, (recv_in, recv_pa, recv_pb, recv_out) = _flash_bwd(
        q_all, k_all, v_all, dao, ao, mblk, p_all, early_slabs)

    dsmall, dqf_b, dkn_b, dv_b, dgq_acc, dgk_acc = _mla_prep_bwd(
        dq_all, dk_all, dv_all, proj, q_a_g, kv_a_g, w_uq_p, w_kn, w_v, cos, sin_a, sin_b)
    late_slab = _scatter_dw_in({SMALL_SEG: _matmul_tn(h, dsmall, "dw_in_%d" % SMALL_SEG)}, [LATE_DEV],
                               "scatter_dw_in_late")
    segs = [(d_head, k) for k in range(4)] + [dsmall] + [(d_tail, k) for k in range(3)]
    return dict(
        segs=segs, h=h, dx2=dx2, late_slab=late_slab,
        dw_uq_p=_matmul_tn(cqn, dqf_b, "dw_uq"), dw_kn=_matmul_tn(ckvn, dkn_b, "dw_kn"),
        dw_v=_matmul_tn(ckvn, dv_b, "dw_v"),
        small=dict(b_gate=dbg_acc[0:1], lb_logits=dlb, hg_norm_g=dhg_acc[0:1], q_a_g=dgq_acc[0:1],
                   kv_a_g=dgk_acc[0:1], final_norm_g=dfg_acc[0], loss=loss_acc[0, 0]),
        recv=dict(w_in=[recv_in], w_pa=[recv_pa], w_pb=[recv_pb], w_out=[recv_out]),
    )


def kernel(x, norm_g, w_in, b_gate, lb_logits, hg_norm_g, q_a_g, w_uq, kv_a_g, w_ukv, w_proj_a, w_proj_b, w_out, final_norm_g, loss_target, m_norm_g, m_w_in, m_b_gate, m_lb_logits, m_hg_norm_g, m_q_a_g, m_w_uq, m_kv_a_g, m_w_ukv, m_w_proj_a, m_w_proj_b, m_w_out, m_final_norm_g, v_norm_g, v_w_in, v_b_gate, v_lb_logits, v_hg_norm_g, v_q_a_g, v_w_uq, v_kv_a_g, v_w_ukv, v_w_proj_a, v_w_proj_b, v_w_out, v_final_norm_g):
    xs = x[0]
    w_in_shard, other_shard = _weight_shard_buffers(w_in, w_uq, w_ukv, w_proj_a, w_proj_b, w_out)
    w_in_p = _assemble_w_in(_all_gather_packed(w_in_shard))
    g = _step_gradients(xs, loss_target[0], norm_g, b_gate, lb_logits, hg_norm_g, q_a_g, kv_a_g,
                        final_norm_g.reshape(1, -1), w_in_p, other_shard)
    sm = g["small"]
    late_sets = _late_slab_sets(g["dw_uq_p"], g["dw_kn"], g["dw_v"],
                                _small_rows(sm["b_gate"], sm["lb_logits"], sm["hg_norm_g"], sm["q_a_g"],
                                            sm["kv_a_g"], sm["final_norm_g"], sm["loss"]))
    late_recv_init = jnp.zeros((N_DEV, D_MODEL, PACK_COLS), TRANSPORT_DTYPE)
    grad_x, dng_acc, recv_late, recv_uq, recv_ukv, recv_small = _dh_bwd(
        g["segs"], w_in_p, xs, g["dx2"], norm_g, g["late_slab"], late_recv_init, late_sets)
    recv_ng = _exchange_rows(jnp.broadcast_to(dng_acc[None], (N_DEV, 8, D_MODEL)))

    recv = g["recv"]
    big = dict(
        w_in=_sum_adamw(recv["w_in"] + [recv_late], w_in[0], m_w_in[0], v_w_in[0], "adamw_w_in"),
        w_proj_a=_sum_adamw(recv["w_pa"], w_proj_a[0], m_w_proj_a[0], v_w_proj_a[0], "adamw_w_pa"),
        w_proj_b=_sum_adamw(recv["w_pb"], w_proj_b[0], m_w_proj_b[0], v_w_proj_b[0], "adamw_w_pb"),
        w_out=_sum_adamw(recv["w_out"], w_out[0], m_w_out[0], v_w_out[0], "adamw_w_out"),
        w_uq=_sum_adamw([recv_uq], w_uq[0], m_w_uq[0], v_w_uq[0], "adamw_w_uq"),
        w_ukv=_sum_adamw([recv_ukv], w_ukv[0], m_w_ukv[0], v_w_ukv[0], "adamw_w_ukv"),
    )
    small_names = ["norm_g", "b_gate", "lb_logits", "hg_norm_g", "q_a_g", "kv_a_g", "final_norm_g"]
    loss_row, small = _adamw_small(
        recv_small, recv_ng,
        (norm_g, b_gate, lb_logits, hg_norm_g, q_a_g, kv_a_g, final_norm_g.reshape(1, -1)),
        (m_norm_g, m_b_gate, m_lb_logits, m_hg_norm_g, m_q_a_g, m_kv_a_g, m_final_norm_g.reshape(1, -1)),
        (v_norm_g, v_b_gate, v_lb_logits, v_hg_norm_g, v_q_a_g, v_kv_a_g, v_final_norm_g.reshape(1, -1)))
    names = ["norm_g", "w_in", "b_gate", "lb_logits", "hg_norm_g", "q_a_g", "w_uq", "kv_a_g", "w_ukv",
             "w_proj_a", "w_proj_b", "w_out", "final_norm_g"]
    results = []
    for kind in range(4):
        by_name = {n: big[n][kind][None] for n in big}
        by_name.update(zip(small_names, small[kind]))
        by_name["final_norm_g"] = by_name["final_norm_g"].reshape(-1)
        results += [by_name[n] for n in names]
    return (loss_row[0, 0], grad_x[None], *results)
```
